```python
import math
import jax, jax.numpy as jnp
from jax import lax
import numpy as np

D_MODEL = 1024
BATCH = 8
SEQ = 2048
DEPTH = 4

MEM_LEN = 256
EXPAND = 2
MIX_WIDTH = EXPAND * D_MODEL
A_WIDTH = MIX_WIDTH // 2
A_HEAD_DIM = 64
A_HEADS = A_WIDTH // A_HEAD_DIM
A_PATTERNS = ((128, 1), (512, 4), (2048, 16))
A_BLOCK = 128
B_WIDTH = MIX_WIDTH - A_WIDTH
POOL_WINDOWS = (2, 4, 8, 16)
B_GROUP = B_WIDTH // len(POOL_WINDOWS)
C_WIDTH = D_MODEL
C_CHUNK = 128
C_GROUPS = 4
C_GROUP_DIM = C_WIDTH // C_GROUPS
D_WIDTH = D_MODEL // 2
S5_GROUP_DIM = 16
S5_GROUPS = D_WIDTH // S5_GROUP_DIM
S5_STATE = 64
X_HEADS = 4
X_HEAD_DIM = D_MODEL // X_HEADS
N_EVEN = (DEPTH + 1) // 2
N_ODD = DEPTH // 2
EPS = 1e-6
NEG = -1e30

kernel_name = 'hybrid_dilated_pool_sgu_s5_trunk'


def rms_norm(x, g):
    xf = x.astype(jnp.float32)
    y = xf * lax.rsqrt(jnp.mean(xf * xf, axis=-1, keepdims=True) + EPS)
    return (y * g.astype(jnp.float32)).astype(x.dtype)


def _dilated_pattern(q, k, v, window, dilation):
    b, s, h, dh = q.shape
    d = dilation
    L = s // d
    w = window // d
    nb = -(-L // A_BLOCK)
    lp = nb * A_BLOCK
    n = b * d

    def to_dilated(t):
        t = t.reshape(b, L, d, h, dh).transpose(0, 2, 1, 3, 4).reshape(n, L, h, dh)
        return jnp.pad(t, ((0, 0), (0, lp - L), (0, 0), (0, 0)))

    def band(t):
        tp = jnp.pad(t, ((0, 0), (A_BLOCK, 0), (0, 0), (0, 0))).reshape(n, nb + 1, A_BLOCK, h, dh)
        return jnp.concatenate([tp[:, :-1], tp[:, 1:]], axis=2)

    qb = to_dilated(q).reshape(n, nb, A_BLOCK, h, dh)
    kb = band(to_dilated(k))
    vb = band(to_dilated(v))
    i = jnp.arange(A_BLOCK)[:, None]
    j = jnp.arange(2 * A_BLOCK)[None, :]
    dist = i + A_BLOCK - j
    blk = jnp.arange(nb)[:, None, None]
    valid = (dist >= 0) & (dist <= w) & ((j >= A_BLOCK) | (blk > 0))
    sc = jnp.einsum('nbihd,nbjhd->nbhij', qb, kb, preferred_element_type=jnp.float32)
    sc = jnp.where(valid[None, :, None], sc, NEG)
    m = jnp.max(sc, axis=-1, keepdims=True)
    p = jnp.exp(sc - m)
    den = jnp.sum(p, axis=-1, keepdims=True)
    o = jnp.einsum('nbhij,nbjhd->nbihd', (p / den).astype(v.dtype), vb)
    lse = (m + jnp.log(den))[..., 0].transpose(0, 1, 3, 2)

    def from_dilated(t):
        rest = t.shape[3:]
        t = t.reshape((n, lp) + rest)[:, :L]
        return t.reshape((b, d, L) + rest).swapaxes(1, 2).reshape((b, s) + rest)

    return from_dilated(o), from_dilated(lse)


def dilated_attention(q, k, v):
    outs, lses = zip(*[_dilated_pattern(q, k, v, w, d) for (w, d) in A_PATTERNS])
    wts = jax.nn.softmax(jnp.stack(lses, axis=0), axis=0)
    o = jnp.sum(jnp.stack(outs, axis=0).astype(jnp.float32) * wts[..., None], axis=0)
    return o.astype(q.dtype)


def multiscale_pool(v, pool_w, pool_scale):
    b, s, _ = v.shape
    vf = v.astype(jnp.float32)
    c0 = jnp.pad(jnp.cumsum(vf, axis=1), ((0, 0), (1, 0), (0, 0)))
    pos = jnp.arange(1, s + 1, dtype=jnp.float32)[None, :, None]
    groups = []
    for g, w in enumerate(POOL_WINDOWS):
        sl = slice(g * B_GROUP, (g + 1) * B_GROUP)
        cg = c0[..., sl]
        lower = jnp.pad(cg, ((0, 0), (w - 1, 0), (0, 0)))[:, :s]
        mean = (cg[:, 1:] - lower) / jnp.minimum(pos, float(w))
        groups.append(mean - vf[..., sl])
    pooled = jnp.stack(groups, axis=2).astype(v.dtype)
    mixed = jnp.einsum('bsgc,gcd->bsgd', pooled, pool_w).reshape(b, s, B_WIDTH)
    return mixed * pool_scale


def spatial_gating(u, v, ln_g, ln_b, w_s, b_s):
    b, s, _ = u.shape
    vf = v.astype(jnp.float32)
    mu = jnp.mean(vf, axis=-1, keepdims=True)
    var = jnp.mean(jnp.square(vf - mu), axis=-1, keepdims=True)
    vn = ((vf - mu) * lax.rsqrt(var + EPS) * ln_g.astype(jnp.float32) + ln_b.astype(jnp.float32)).astype(v.dtype)
    nc = s // C_CHUNK
    vc = vn.reshape(b, nc, C_CHUNK, C_GROUPS, C_GROUP_DIM)
    mask = jnp.tril(jnp.ones((C_CHUNK, C_CHUNK), dtype=bool))
    w = jnp.where(mask[None], w_s, jnp.zeros_like(w_s))
    mixed = jnp.einsum('gij,bnjgc->bnigc', w, vc) + b_s.T[None, None, :, :, None]
    return u * mixed.reshape(b, s, C_WIDTH)


def _ssm_combine(e1, e2):
    a1r, a1i, b1r, b1i = e1
    a2r, a2i, b2r, b2i = e2
    return (a2r * a1r - a2i * a1i,
            a2r * a1i + a2i * a1r,
            a2r * b1r - a2i * b1i + b2r,
            a2r * b1i + a2i * b1r + b2i)


def s5_ssm(u, a_re, a_im, log_dt, b_re, b_im, c_re, c_im, d_skip, w1, w2):
    bsz, s, _ = u.shape
    f32 = jnp.float32
    uf = u.astype(f32).reshape(bsz, s, S5_GROUPS, S5_GROUP_DIM)
    ar, ai = a_re.astype(f32), a_im.astype(f32)
    dt = jnp.exp(log_dt.astype(f32))[:, None]
    mag = jnp.exp(dt * ar)
    abar_re = mag * jnp.cos(dt * ai)
    abar_im = mag * jnp.sin(dt * ai)
    nr, ni = abar_re - 1.0, abar_im
    inv = 1.0 / (ar * ar + ai * ai)
    coef_re = (nr * ar + ni * ai) * inv
    coef_im = (ni * ar - nr * ai) * inv
    br, bi = b_re.astype(f32), b_im.astype(f32)
    bbar_re = coef_re[..., None] * br - coef_im[..., None] * bi
    bbar_im = coef_re[..., None] * bi + coef_im[..., None] * br
    bu_re = jnp.einsum('bsgh,gph->bsgp', uf, bbar_re)
    bu_im = jnp.einsum('bsgh,gph->bsgp', uf, bbar_im)
    shape_a = (1, s, S5_GROUPS, S5_STATE)
    a_seq_re = jnp.broadcast_to(abar_re[None, None], shape_a)
    a_seq_im = jnp.broadcast_to(abar_im[None, None], shape_a)
    _, _, h_re, h_im = lax.associative_scan(_ssm_combine, (a_seq_re, a_seq_im, bu_re, bu_im), axis=1)
    y = (jnp.einsum('bsgp,ghp->bsgh', h_re, c_re.astype(f32))
         - jnp.einsum('bsgp,ghp->bsgh', h_im, c_im.astype(f32))
         + d_skip.astype(f32).reshape(S5_GROUPS, S5_GROUP_DIM) * uf)
    y = jax.nn.gelu(y.reshape(bsz, s, D_WIDTH)).astype(u.dtype)
    return (y @ w1) * jax.nn.sigmoid(y @ w2)


def memory_cross_attention(h, mem_n, w_q, w_kv, w_o):
    b, s, _ = h.shape
    m = mem_n.shape[1]
    q = (h @ w_q).reshape(b, s, X_HEADS, X_HEAD_DIM)
    kv = (mem_n @ w_kv).reshape(b, m, 2, X_HEADS, X_HEAD_DIM)
    k, v = kv[:, :, 0], kv[:, :, 1]
    sc = jnp.einsum('bshd,bmhd->bhsm', q, k, preferred_element_type=jnp.float32) * (X_HEAD_DIM ** -0.5)
    p = jax.nn.softmax(sc, axis=-1).astype(v.dtype)
    o = jnp.einsum('bhsm,bmhd->bshd', p, v).reshape(b, s, D_MODEL)
    return o @ w_o


def _fwd_setup_inputs(seed: int = 0) -> dict:
    key = jax.random.key(seed)
    ks = iter(jax.random.split(key, 40))

    def nrm(shape, scale):
        return scale * jax.random.normal(next(ks), shape, jnp.float32)

    def gain(shape):
        return 1.0 + nrm(shape, 0.05)

    return {
        'x': nrm((BATCH, SEQ, D_MODEL), 1.0),
        'mem': nrm((BATCH, MEM_LEN, D_MODEL), 1.0),
        'norm_ab': gain((N_EVEN, D_MODEL)),
        'w_in_ab': nrm((N_EVEN, D_MODEL, 4 * A_WIDTH + 2 * B_WIDTH), D_MODEL ** -0.5),
        'pool_w': nrm((N_EVEN, len(POOL_WINDOWS), B_GROUP, B_GROUP), B_GROUP ** -0.5),
        'pool_scale': gain((N_EVEN, B_WIDTH)),
        'w_out_ab': nrm((N_EVEN, A_WIDTH + B_WIDTH, D_MODEL), (A_WIDTH + B_WIDTH) ** -0.5),
        'norm_cd': gain((N_ODD, D_MODEL)),
        'w_in_cd': nrm((N_ODD, D_MODEL, 3 * C_WIDTH + 2 * D_WIDTH), D_MODEL ** -0.5),
        'sgu_ln_g': gain((N_ODD, C_WIDTH)),
        'sgu_ln_b': nrm((N_ODD, C_WIDTH), 0.02),
        'sgu_w': nrm((N_ODD, C_GROUPS, C_CHUNK, C_CHUNK), C_CHUNK ** -0.5),
        'sgu_b': gain((N_ODD, C_GROUPS, C_CHUNK)),
        's5_a_re': -0.5 + nrm((N_ODD, S5_GROUPS, S5_STATE), 0.01),
        's5_a_im': jnp.pi * jnp.arange(S5_STATE, dtype=jnp.float32)[None, None, :] + nrm((N_ODD, S5_GROUPS, S5_STATE), 0.01),
        's5_log_dt': jax.random.uniform(next(ks), (N_ODD, S5_GROUPS), jnp.float32, math.log(1e-3), math.log(1e-1)),
        's5_b_re': nrm((N_ODD, S5_GROUPS, S5_STATE, S5_GROUP_DIM), (2 * S5_GROUP_DIM) ** -0.5),
        's5_b_im': nrm((N_ODD, S5_GROUPS, S5_STATE, S5_GROUP_DIM), (2 * S5_GROUP_DIM) ** -0.5),
        's5_c_re': nrm((N_ODD, S5_GROUPS, S5_GROUP_DIM, S5_STATE), S5_STATE ** -0.5),
        's5_c_im': nrm((N_ODD, S5_GROUPS, S5_GROUP_DIM, S5_STATE), S5_STATE ** -0.5),
        's5_d': nrm((N_ODD, D_WIDTH), 1.0),
        'glu_w1': nrm((N_ODD, D_WIDTH, D_WIDTH), D_WIDTH ** -0.5),
        'glu_w2': nrm((N_ODD, D_WIDTH, D_WIDTH), D_WIDTH ** -0.5),
        'w_out_cd': nrm((N_ODD, C_WIDTH + D_WIDTH, D_MODEL), (C_WIDTH + D_WIDTH) ** -0.5),
        'norm_x': gain((DEPTH, D_MODEL)),
        'w_xq': nrm((DEPTH, D_MODEL, D_MODEL), D_MODEL ** -0.5),
        'w_xkv': nrm((DEPTH, D_MODEL, 2 * D_MODEL), D_MODEL ** -0.5),
        'w_xo': nrm((DEPTH, D_MODEL, D_MODEL), D_MODEL ** -0.5),
        'mem_norm': gain((D_MODEL,)),
        'final_norm': gain((D_MODEL,)),
    }


def _fwd_reference(x, mem, norm_ab, w_in_ab, pool_w, pool_scale, w_out_ab,
              norm_cd, w_in_cd, sgu_ln_g, sgu_ln_b, sgu_w, sgu_b,
              s5_a_re, s5_a_im, s5_log_dt, s5_b_re, s5_b_im, s5_c_re, s5_c_im, s5_d,
              glu_w1, glu_w2, w_out_cd, norm_x, w_xq, w_xkv, w_xo, mem_norm, final_norm):
    b, s, _ = x.shape
    mem_n = rms_norm(mem, mem_norm)
    for layer in range(DEPTH):
        i = layer // 2
        if layer % 2 == 0:
            hn = rms_norm(x, norm_ab[i])
            z = hn @ w_in_ab[i]
            q, k, v, g_a, v_b, g_b = jnp.split(
                z, [A_WIDTH, 2 * A_WIDTH, 3 * A_WIDTH, 4 * A_WIDTH, 4 * A_WIDTH + B_WIDTH], axis=-1)
            q = q.reshape(b, s, A_HEADS, A_HEAD_DIM) * (A_HEAD_DIM ** -0.5)
            k = k.reshape(b, s, A_HEADS, A_HEAD_DIM)
            v = v.reshape(b, s, A_HEADS, A_HEAD_DIM)
            a_out = dilated_attention(q, k, v).reshape(b, s, A_WIDTH) * jax.nn.silu(g_a)
            b_out = multiscale_pool(v_b, pool_w[i], pool_scale[i]) * jax.nn.silu(g_b)
            y = jnp.concatenate([a_out, b_out], axis=-1) @ w_out_ab[i]
        else:
            hn = rms_norm(x, norm_cd[i])
            z = hn @ w_in_cd[i]
            u_c, v_c, g_c, x_d, g_d = jnp.split(
                z, [C_WIDTH, 2 * C_WIDTH, 3 * C_WIDTH, 3 * C_WIDTH + D_WIDTH], axis=-1)
            c_out = spatial_gating(u_c, v_c, sgu_ln_g[i], sgu_ln_b[i], sgu_w[i], sgu_b[i]) * jax.nn.silu(g_c)
            d_out = s5_ssm(x_d, s5_a_re[i], s5_a_im[i], s5_log_dt[i], s5_b_re[i], s5_b_im[i],
                           s5_c_re[i], s5_c_im[i], s5_d[i], glu_w1[i], glu_w2[i]) * jax.nn.silu(g_d)
            y = jnp.concatenate([c_out, d_out], axis=-1) @ w_out_cd[i]
        x = x + y
        x = x + memory_cross_attention(rms_norm(x, norm_x[layer]), mem_n, w_xq[layer], w_xkv[layer], w_xo[layer])
    return rms_norm(x, final_norm)


import jax as _jax
import jax.numpy as _jnp

TWIN_FORMAT = 'train_step'
FWD_PARAMS = ['x', 'mem', 'norm_ab', 'w_in_ab', 'pool_w', 'pool_scale', 'w_out_ab', 'norm_cd', 'w_in_cd', 'sgu_ln_g', 'sgu_ln_b', 'sgu_w', 'sgu_b', 's5_a_re', 's5_a_im', 's5_log_dt', 's5_b_re', 's5_b_im', 's5_c_re', 's5_c_im', 's5_d', 'glu_w1', 'glu_w2', 'w_out_cd', 'norm_x', 'w_xq', 'w_xkv', 'w_xo', 'mem_norm', 'final_norm']
TWIN_WEIGHTS = ['norm_ab', 'w_in_ab', 'pool_w', 'pool_scale', 'w_out_ab', 'norm_cd', 'w_in_cd', 'sgu_ln_g', 'sgu_ln_b', 'sgu_w', 'sgu_b', 's5_a_re', 's5_a_im', 's5_log_dt', 's5_b_re', 's5_b_im', 's5_c_re', 's5_c_im', 's5_d', 'glu_w1', 'glu_w2', 'w_out_cd', 'norm_x', 'w_xq', 'w_xkv', 'w_xo', 'mem_norm', 'final_norm']
TWIN_DIFF_INPUT = 'x'
TWIN_INPUTS = ['x', 'mem', 'norm_ab', 'w_in_ab', 'pool_w', 'pool_scale', 'w_out_ab', 'norm_cd', 'w_in_cd', 'sgu_ln_g', 'sgu_ln_b', 'sgu_w', 'sgu_b', 's5_a_re', 's5_a_im', 's5_log_dt', 's5_b_re', 's5_b_im', 's5_c_re', 's5_c_im', 's5_d', 'glu_w1', 'glu_w2', 'w_out_cd', 'norm_x', 'w_xq', 'w_xkv', 'w_xo', 'mem_norm', 'final_norm', 'loss_target', 'm_norm_ab', 'm_w_in_ab', 'm_pool_w', 'm_pool_scale', 'm_w_out_ab', 'm_norm_cd', 'm_w_in_cd', 'm_sgu_ln_g', 'm_sgu_ln_b', 'm_sgu_w', 'm_sgu_b', 'm_s5_a_re', 'm_s5_a_im', 'm_s5_log_dt', 'm_s5_b_re', 'm_s5_b_im', 'm_s5_c_re', 'm_s5_c_im', 'm_s5_d', 'm_glu_w1', 'm_glu_w2', 'm_w_out_cd', 'm_norm_x', 'm_w_xq', 'm_w_xkv', 'm_w_xo', 'm_mem_norm', 'm_final_norm', 'v_norm_ab', 'v_w_in_ab', 'v_pool_w', 'v_pool_scale', 'v_w_out_ab', 'v_norm_cd', 'v_w_in_cd', 'v_sgu_ln_g', 'v_sgu_ln_b', 'v_sgu_w', 'v_sgu_b', 'v_s5_a_re', 'v_s5_a_im', 'v_s5_log_dt', 'v_s5_b_re', 'v_s5_b_im', 'v_s5_c_re', 'v_s5_c_im', 'v_s5_d', 'v_glu_w1', 'v_glu_w2', 'v_w_out_cd', 'v_norm_x', 'v_w_xq', 'v_w_xkv', 'v_w_xo', 'v_mem_norm', 'v_final_norm']
TWIN_OUTPUTS = ['loss', 'grad_x', 'grad_norm_ab', 'grad_w_in_ab', 'grad_pool_w', 'grad_pool_scale', 'grad_w_out_ab', 'grad_norm_cd', 'grad_w_in_cd', 'grad_sgu_ln_g', 'grad_sgu_ln_b', 'grad_sgu_w', 'grad_sgu_b', 'grad_s5_a_re', 'grad_s5_a_im', 'grad_s5_log_dt', 'grad_s5_b_re', 'grad_s5_b_im', 'grad_s5_c_re', 'grad_s5_c_im', 'grad_s5_d', 'grad_glu_w1', 'grad_glu_w2', 'grad_w_out_cd', 'grad_norm_x', 'grad_w_xq', 'grad_w_xkv', 'grad_w_xo', 'grad_mem_norm', 'grad_final_norm', 'delta_norm_ab', 'delta_w_in_ab', 'delta_pool_w', 'delta_pool_scale', 'delta_w_out_ab', 'delta_norm_cd', 'delta_w_in_cd', 'delta_sgu_ln_g', 'delta_sgu_ln_b', 'delta_sgu_w', 'delta_sgu_b', 'delta_s5_a_re', 'delta_s5_a_im', 'delta_s5_log_dt', 'delta_s5_b_re', 'delta_s5_b_im', 'delta_s5_c_re', 'delta_s5_c_im', 'delta_s5_d', 'delta_glu_w1', 'delta_glu_w2', 'delta_w_out_cd', 'delta_norm_x', 'delta_w_xq', 'delta_w_xkv', 'delta_w_xo', 'delta_mem_norm', 'delta_final_norm', 'new_m_norm_ab', 'new_m_w_in_ab', 'new_m_pool_w', 'new_m_pool_scale', 'new_m_w_out_ab', 'new_m_norm_cd', 'new_m_w_in_cd', 'new_m_sgu_ln_g', 'new_m_sgu_ln_b', 'new_m_sgu_w', 'new_m_sgu_b', 'new_m_s5_a_re', 'new_m_s5_a_im', 'new_m_s5_log_dt', 'new_m_s5_b_re', 'new_m_s5_b_im', 'new_m_s5_c_re', 'new_m_s5_c_im', 'new_m_s5_d', 'new_m_glu_w1', 'new_m_glu_w2', 'new_m_w_out_cd', 'new_m_norm_x', 'new_m_w_xq', 'new_m_w_xkv', 'new_m_w_xo', 'new_m_mem_norm', 'new_m_final_norm', 'new_v_norm_ab', 'new_v_w_in_ab', 'new_v_pool_w', 'new_v_pool_scale', 'new_v_w_out_ab', 'new_v_norm_cd', 'new_v_w_in_cd', 'new_v_sgu_ln_g', 'new_v_sgu_ln_b', 'new_v_sgu_w', 'new_v_sgu_b', 'new_v_s5_a_re', 'new_v_s5_a_im', 'new_v_s5_log_dt', 'new_v_s5_b_re', 'new_v_s5_b_im', 'new_v_s5_c_re', 'new_v_s5_c_im', 'new_v_s5_d', 'new_v_glu_w1', 'new_v_glu_w2', 'new_v_w_out_cd', 'new_v_norm_x', 'new_v_w_xq', 'new_v_w_xkv', 'new_v_w_xo', 'new_v_mem_norm', 'new_v_final_norm']
TWIN_LEAF_KINDS = {'loss': 'loss', 'grad_x': 'grad_x', 'grad_norm_ab': 'grad_w', 'grad_w_in_ab': 'grad_w', 'grad_pool_w': 'grad_w', 'grad_pool_scale': 'grad_w', 'grad_w_out_ab': 'grad_w', 'grad_norm_cd': 'grad_w', 'grad_w_in_cd': 'grad_w', 'grad_sgu_ln_g': 'grad_w', 'grad_sgu_ln_b': 'grad_w', 'grad_sgu_w': 'grad_w', 'grad_sgu_b': 'grad_w', 'grad_s5_a_re': 'grad_w', 'grad_s5_a_im': 'grad_w', 'grad_s5_log_dt': 'grad_w', 'grad_s5_b_re': 'grad_w', 'grad_s5_b_im': 'grad_w', 'grad_s5_c_re': 'grad_w', 'grad_s5_c_im': 'grad_w', 'grad_s5_d': 'grad_w', 'grad_glu_w1': 'grad_w', 'grad_glu_w2': 'grad_w', 'grad_w_out_cd': 'grad_w', 'grad_norm_x': 'grad_w', 'grad_w_xq': 'grad_w', 'grad_w_xkv': 'grad_w', 'grad_w_xo': 'grad_w', 'grad_mem_norm': 'grad_w', 'grad_final_norm': 'grad_w', 'delta_norm_ab': 'delta_w', 'delta_w_in_ab': 'delta_w', 'delta_pool_w': 'delta_w', 'delta_pool_scale': 'delta_w', 'delta_w_out_ab': 'delta_w', 'delta_norm_cd': 'delta_w', 'delta_w_in_cd': 'delta_w', 'delta_sgu_ln_g': 'delta_w', 'delta_sgu_ln_b': 'delta_w', 'delta_sgu_w': 'delta_w', 'delta_sgu_b': 'delta_w', 'delta_s5_a_re': 'delta_w', 'delta_s5_a_im': 'delta_w', 'delta_s5_log_dt': 'delta_w', 'delta_s5_b_re': 'delta_w', 'delta_s5_b_im': 'delta_w', 'delta_s5_c_re': 'delta_w', 'delta_s5_c_im': 'delta_w', 'delta_s5_d': 'delta_w', 'delta_glu_w1': 'delta_w', 'delta_glu_w2': 'delta_w', 'delta_w_out_cd': 'delta_w', 'delta_norm_x': 'delta_w', 'delta_w_xq': 'delta_w', 'delta_w_xkv': 'delta_w', 'delta_w_xo': 'delta_w', 'delta_mem_norm': 'delta_w', 'delta_final_norm': 'delta_w', 'new_m_norm_ab': 'new_m', 'new_m_w_in_ab': 'new_m', 'new_m_pool_w': 'new_m', 'new_m_pool_scale': 'new_m', 'new_m_w_out_ab': 'new_m', 'new_m_norm_cd': 'new_m', 'new_m_w_in_cd': 'new_m', 'new_m_sgu_ln_g': 'new_m', 'new_m_sgu_ln_b': 'new_m', 'new_m_sgu_w': 'new_m', 'new_m_sgu_b': 'new_m', 'new_m_s5_a_re': 'new_m', 'new_m_s5_a_im': 'new_m', 'new_m_s5_log_dt': 'new_m', 'new_m_s5_b_re': 'new_m', 'new_m_s5_b_im': 'new_m', 'new_m_s5_c_re': 'new_m', 'new_m_s5_c_im': 'new_m', 'new_m_s5_d': 'new_m', 'new_m_glu_w1': 'new_m', 'new_m_glu_w2': 'new_m', 'new_m_w_out_cd': 'new_m', 'new_m_norm_x': 'new_m', 'new_m_w_xq': 'new_m', 'new_m_w_xkv': 'new_m', 'new_m_w_xo': 'new_m', 'new_m_mem_norm': 'new_m', 'new_m_final_norm': 'new_m', 'new_v_norm_ab': 'new_v', 'new_v_w_in_ab': 'new_v', 'new_v_pool_w': 'new_v', 'new_v_pool_scale': 'new_v', 'new_v_w_out_ab': 'new_v', 'new_v_norm_cd': 'new_v', 'new_v_w_in_cd': 'new_v', 'new_v_sgu_ln_g': 'new_v', 'new_v_sgu_ln_b': 'new_v', 'new_v_sgu_w': 'new_v', 'new_v_sgu_b': 'new_v', 'new_v_s5_a_re': 'new_v', 'new_v_s5_a_im': 'new_v', 'new_v_s5_log_dt': 'new_v', 'new_v_s5_b_re': 'new_v', 'new_v_s5_b_im': 'new_v', 'new_v_s5_c_re': 'new_v', 'new_v_s5_c_im': 'new_v', 'new_v_s5_d': 'new_v', 'new_v_glu_w1': 'new_v', 'new_v_glu_w2': 'new_v', 'new_v_w_out_cd': 'new_v', 'new_v_norm_x': 'new_v', 'new_v_w_xq': 'new_v', 'new_v_w_xkv': 'new_v', 'new_v_w_xo': 'new_v', 'new_v_mem_norm': 'new_v', 'new_v_final_norm': 'new_v'}


def _forward(args):
    return _fwd_reference(*[args[k] for k in FWD_PARAMS])


def _output_shape():
    out = _jax.eval_shape(lambda: _forward(_fwd_setup_inputs(0)))
    return out.shape, out.dtype

N_MICROBATCH = 1
ADAM_LR = 0.001
ADAM_B1 = 0.9
ADAM_B2 = 0.999
ADAM_EPS = 1e-08
ADAM_WD = 0.01
ADAM_STEP = 10
PER_EXAMPLE_BATCH_AXIS = {'x': 0, 'mem': 0, 'loss_target': 0}
SHARED_INPUTS = []
_WEIGHT_DTYPES = {'norm_ab': _jnp.float32, 'w_in_ab': _jnp.float32, 'pool_w': _jnp.float32, 'pool_scale': _jnp.float32, 'w_out_ab': _jnp.float32, 'norm_cd': _jnp.float32, 'w_in_cd': _jnp.float32, 'sgu_ln_g': _jnp.float32, 'sgu_ln_b': _jnp.float32, 'sgu_w': _jnp.float32, 'sgu_b': _jnp.float32, 's5_a_re': _jnp.float32, 's5_a_im': _jnp.float32, 's5_log_dt': _jnp.float32, 's5_b_re': _jnp.float32, 's5_b_im': _jnp.float32, 's5_c_re': _jnp.float32, 's5_c_im': _jnp.float32, 's5_d': _jnp.float32, 'glu_w1': _jnp.float32, 'glu_w2': _jnp.float32, 'w_out_cd': _jnp.float32, 'norm_x': _jnp.float32, 'w_xq': _jnp.float32, 'w_xkv': _jnp.float32, 'w_xo': _jnp.float32, 'mem_norm': _jnp.float32, 'final_norm': _jnp.float32}
MOMENT_SCALE = {'norm_ab': 7.560531e-02, 'w_in_ab': 3.121876e-02, 'pool_w': 5.064923e-02, 'pool_scale': 5.282594e-02, 'w_out_ab': 5.198813e-02, 'norm_cd': 1.008046e-01, 'w_in_cd': 4.903112e-02, 'sgu_ln_g': 3.632078e-02, 'sgu_ln_b': 3.551333e-02, 'sgu_w': 5.008766e-02, 'sgu_b': 7.155594e-02, 's5_a_re': 1.316140e-03, 's5_a_im': 1.496867e-03, 's5_log_dt': 1.187727e+00, 's5_b_re': 8.615796e-04, 's5_b_im': 8.717569e-04, 's5_c_re': 1.239404e-03, 's5_c_im': 1.250214e-03, 's5_d': 2.000722e-02, 'glu_w1': 1.730103e-02, 'glu_w2': 4.906291e-03, 'w_out_cd': 6.364738e-02, 'norm_x': 1.292558e-02, 'w_xq': 1.269271e-02, 'w_xkv': 1.281549e-02, 'w_xo': 1.303025e-02, 'mem_norm': 3.881308e-02, 'final_norm': 1.601054e+01}


def _to_microbatches(a, axis):
    t = _jnp.moveaxis(a, axis, 0)
    t = t.reshape((N_MICROBATCH, t.shape[0] // N_MICROBATCH) + t.shape[1:])
    return _jnp.moveaxis(t, 1, axis + 1)


def setup_inputs(seed: int = 0) -> dict:
    inp = _fwd_setup_inputs(seed)
    key = _jax.random.fold_in(_jax.random.key(seed), 7919)
    shape, _ = _output_shape()
    out = dict(inp)
    out["loss_target"] = _jax.random.normal(_jax.random.fold_in(key, 0), shape, _jnp.float32)
    for i, name in enumerate(TWIN_WEIGHTS):
        w = inp[name].astype(_jnp.float32)
        if MOMENT_SCALE is None:
            s = _jnp.sqrt(_jnp.mean(_jnp.square(w)) + 1e-30)
        else:
            s = MOMENT_SCALE[name]
        km, kv = _jax.random.split(_jax.random.fold_in(key, i + 1))
        out[name] = w
        out["m_" + name] = s * _jax.random.normal(km, w.shape, _jnp.float32)
        out["v_" + name] = (s * s) * _jax.random.uniform(kv, w.shape, _jnp.float32, 0.5, 1.5)
    if N_MICROBATCH > 1:
        for name, axis in PER_EXAMPLE_BATCH_AXIS.items():
            out[name] = _to_microbatches(out[name], axis)
    return {'x': out['x'], 'mem': out['mem'], 'norm_ab': out['norm_ab'], 'w_in_ab': out['w_in_ab'], 'pool_w': out['pool_w'], 'pool_scale': out['pool_scale'], 'w_out_ab': out['w_out_ab'], 'norm_cd': out['norm_cd'], 'w_in_cd': out['w_in_cd'], 'sgu_ln_g': out['sgu_ln_g'], 'sgu_ln_b': out['sgu_ln_b'], 'sgu_w': out['sgu_w'], 'sgu_b': out['sgu_b'], 's5_a_re': out['s5_a_re'], 's5_a_im': out['s5_a_im'], 's5_log_dt': out['s5_log_dt'], 's5_b_re': out['s5_b_re'], 's5_b_im': out['s5_b_im'], 's5_c_re': out['s5_c_re'], 's5_c_im': out['s5_c_im'], 's5_d': out['s5_d'], 'glu_w1': out['glu_w1'], 'glu_w2': out['glu_w2'], 'w_out_cd': out['w_out_cd'], 'norm_x': out['norm_x'], 'w_xq': out['w_xq'], 'w_xkv': out['w_xkv'], 'w_xo': out['w_xo'], 'mem_norm': out['mem_norm'], 'final_norm': out['final_norm'], 'loss_target': out['loss_target'], 'm_norm_ab': out['m_norm_ab'], 'm_w_in_ab': out['m_w_in_ab'], 'm_pool_w': out['m_pool_w'], 'm_pool_scale': out['m_pool_scale'], 'm_w_out_ab': out['m_w_out_ab'], 'm_norm_cd': out['m_norm_cd'], 'm_w_in_cd': out['m_w_in_cd'], 'm_sgu_ln_g': out['m_sgu_ln_g'], 'm_sgu_ln_b': out['m_sgu_ln_b'], 'm_sgu_w': out['m_sgu_w'], 'm_sgu_b': out['m_sgu_b'], 'm_s5_a_re': out['m_s5_a_re'], 'm_s5_a_im': out['m_s5_a_im'], 'm_s5_log_dt': out['m_s5_log_dt'], 'm_s5_b_re': out['m_s5_b_re'], 'm_s5_b_im': out['m_s5_b_im'], 'm_s5_c_re': out['m_s5_c_re'], 'm_s5_c_im': out['m_s5_c_im'], 'm_s5_d': out['m_s5_d'], 'm_glu_w1': out['m_glu_w1'], 'm_glu_w2': out['m_glu_w2'], 'm_w_out_cd': out['m_w_out_cd'], 'm_norm_x': out['m_norm_x'], 'm_w_xq': out['m_w_xq'], 'm_w_xkv': out['m_w_xkv'], 'm_w_xo': out['m_w_xo'], 'm_mem_norm': out['m_mem_norm'], 'm_final_norm': out['m_final_norm'], 'v_norm_ab': out['v_norm_ab'], 'v_w_in_ab': out['v_w_in_ab'], 'v_pool_w': out['v_pool_w'], 'v_pool_scale': out['v_pool_scale'], 'v_w_out_ab': out['v_w_out_ab'], 'v_norm_cd': out['v_norm_cd'], 'v_w_in_cd': out['v_w_in_cd'], 'v_sgu_ln_g': out['v_sgu_ln_g'], 'v_sgu_ln_b': out['v_sgu_ln_b'], 'v_sgu_w': out['v_sgu_w'], 'v_sgu_b': out['v_sgu_b'], 'v_s5_a_re': out['v_s5_a_re'], 'v_s5_a_im': out['v_s5_a_im'], 'v_s5_log_dt': out['v_s5_log_dt'], 'v_s5_b_re': out['v_s5_b_re'], 'v_s5_b_im': out['v_s5_b_im'], 'v_s5_c_re': out['v_s5_c_re'], 'v_s5_c_im': out['v_s5_c_im'], 'v_s5_d': out['v_s5_d'], 'v_glu_w1': out['v_glu_w1'], 'v_glu_w2': out['v_glu_w2'], 'v_w_out_cd': out['v_w_out_cd'], 'v_norm_x': out['v_norm_x'], 'v_w_xq': out['v_w_xq'], 'v_w_xkv': out['v_w_xkv'], 'v_w_xo': out['v_w_xo'], 'v_mem_norm': out['v_mem_norm'], 'v_final_norm': out['v_final_norm']}


def _loss(weights, diff, rest, loss_target):
    with _jax.named_scope("forward"):
        args = {**rest, TWIN_DIFF_INPUT: diff, **{k: w.astype(_WEIGHT_DTYPES[k]) for k, w in weights.items()}}
        y = _forward(args)
    with _jax.named_scope("loss_head"):
        err = _jnp.square(y.astype(_jnp.float32) - loss_target)
        return 0.5 * _jnp.sum(_jnp.mean(err, axis=-1)) if err.ndim else 0.5 * err


def _adamw(w, g, m, v):
    m = ADAM_B1 * m + (1.0 - ADAM_B1) * g
    v = ADAM_B2 * v + (1.0 - ADAM_B2) * _jnp.square(g)
    m_hat = m / (1.0 - ADAM_B1 ** ADAM_STEP)
    v_hat = v / (1.0 - ADAM_B2 ** ADAM_STEP)
    delta = -ADAM_LR * (m_hat / (_jnp.sqrt(v_hat) + ADAM_EPS) + ADAM_WD * w)
    return delta, m, v


def reference(x, mem, norm_ab, w_in_ab, pool_w, pool_scale, w_out_ab, norm_cd, w_in_cd, sgu_ln_g, sgu_ln_b, sgu_w, sgu_b, s5_a_re, s5_a_im, s5_log_dt, s5_b_re, s5_b_im, s5_c_re, s5_c_im, s5_d, glu_w1, glu_w2, w_out_cd, norm_x, w_xq, w_xkv, w_xo, mem_norm, final_norm, loss_target, m_norm_ab, m_w_in_ab, m_pool_w, m_pool_scale, m_w_out_ab, m_norm_cd, m_w_in_cd, m_sgu_ln_g, m_sgu_ln_b, m_sgu_w, m_sgu_b, m_s5_a_re, m_s5_a_im, m_s5_log_dt, m_s5_b_re, m_s5_b_im, m_s5_c_re, m_s5_c_im, m_s5_d, m_glu_w1, m_glu_w2, m_w_out_cd, m_norm_x, m_w_xq, m_w_xkv, m_w_xo, m_mem_norm, m_final_norm, v_norm_ab, v_w_in_ab, v_pool_w, v_pool_scale, v_w_out_ab, v_norm_cd, v_w_in_cd, v_sgu_ln_g, v_sgu_ln_b, v_sgu_w, v_sgu_b, v_s5_a_re, v_s5_a_im, v_s5_log_dt, v_s5_b_re, v_s5_b_im, v_s5_c_re, v_s5_c_im, v_s5_d, v_glu_w1, v_glu_w2, v_w_out_cd, v_norm_x, v_w_xq, v_w_xkv, v_w_xo, v_mem_norm, v_final_norm):
    given = dict(x=x, mem=mem, norm_ab=norm_ab, w_in_ab=w_in_ab, pool_w=pool_w, pool_scale=pool_scale, w_out_ab=w_out_ab, norm_cd=norm_cd, w_in_cd=w_in_cd, sgu_ln_g=sgu_ln_g, sgu_ln_b=sgu_ln_b, sgu_w=sgu_w, sgu_b=sgu_b, s5_a_re=s5_a_re, s5_a_im=s5_a_im, s5_log_dt=s5_log_dt, s5_b_re=s5_b_re, s5_b_im=s5_b_im, s5_c_re=s5_c_re, s5_c_im=s5_c_im, s5_d=s5_d, glu_w1=glu_w1, glu_w2=glu_w2, w_out_cd=w_out_cd, norm_x=norm_x, w_xq=w_xq, w_xkv=w_xkv, w_xo=w_xo, mem_norm=mem_norm, final_norm=final_norm, loss_target=loss_target, m_norm_ab=m_norm_ab, m_w_in_ab=m_w_in_ab, m_pool_w=m_pool_w, m_pool_scale=m_pool_scale, m_w_out_ab=m_w_out_ab, m_norm_cd=m_norm_cd, m_w_in_cd=m_w_in_cd, m_sgu_ln_g=m_sgu_ln_g, m_sgu_ln_b=m_sgu_ln_b, m_sgu_w=m_sgu_w, m_sgu_b=m_sgu_b, m_s5_a_re=m_s5_a_re, m_s5_a_im=m_s5_a_im, m_s5_log_dt=m_s5_log_dt, m_s5_b_re=m_s5_b_re, m_s5_b_im=m_s5_b_im, m_s5_c_re=m_s5_c_re, m_s5_c_im=m_s5_c_im, m_s5_d=m_s5_d, m_glu_w1=m_glu_w1, m_glu_w2=m_glu_w2, m_w_out_cd=m_w_out_cd, m_norm_x=m_norm_x, m_w_xq=m_w_xq, m_w_xkv=m_w_xkv, m_w_xo=m_w_xo, m_mem_norm=m_mem_norm, m_final_norm=m_final_norm, v_norm_ab=v_norm_ab, v_w_in_ab=v_w_in_ab, v_pool_w=v_pool_w, v_pool_scale=v_pool_scale, v_w_out_ab=v_w_out_ab, v_norm_cd=v_norm_cd, v_w_in_cd=v_w_in_cd, v_sgu_ln_g=v_sgu_ln_g, v_sgu_ln_b=v_sgu_ln_b, v_sgu_w=v_sgu_w, v_sgu_b=v_sgu_b, v_s5_a_re=v_s5_a_re, v_s5_a_im=v_s5_a_im, v_s5_log_dt=v_s5_log_dt, v_s5_b_re=v_s5_b_re, v_s5_b_im=v_s5_b_im, v_s5_c_re=v_s5_c_re, v_s5_c_im=v_s5_c_im, v_s5_d=v_s5_d, v_glu_w1=v_glu_w1, v_glu_w2=v_glu_w2, v_w_out_cd=v_w_out_cd, v_norm_x=v_norm_x, v_w_xq=v_w_xq, v_w_xkv=v_w_xkv, v_w_xo=v_w_xo, v_mem_norm=v_mem_norm, v_final_norm=v_final_norm)
    weights = {n: given[n] for n in TWIN_WEIGHTS}
    shared = {n: given[n] for n in SHARED_INPUTS}
    per_example = {n: given[n] for n in ['x', 'mem']}
    grad_fn = _jax.value_and_grad(_loss, argnums=(0, 1))

    def one_microbatch(ex, loss_target):
        ex = dict(ex)
        diff = ex.pop(TWIN_DIFF_INPUT)
        return grad_fn(weights, diff, {**shared, **ex}, loss_target)

    if N_MICROBATCH == 1:
        loss, (grad_w, grad_x) = one_microbatch(per_example, given["loss_target"])
    else:
        def body(carry, xs):
            loss_sum, grad_sum = carry
            l_k, (gw_k, gx_k) = one_microbatch(xs[0], xs[1])
            with _jax.named_scope("update"):
                return (loss_sum + l_k, _jax.tree.map(_jnp.add, grad_sum, gw_k)), gx_k

        init = (_jnp.zeros((), _jnp.float32), _jax.tree.map(_jnp.zeros_like, weights))
        (loss, grad_w), grad_x = _jax.lax.scan(body, init, (per_example, given["loss_target"]))
    with _jax.named_scope("update"):
        delta_w, new_m, new_v = {}, {}, {}
        for n in TWIN_WEIGHTS:
            delta_w[n], new_m[n], new_v[n] = _adamw(weights[n], grad_w[n], given["m_" + n], given["v_" + n])
    return (loss, grad_x, *[grad_w[n] for n in TWIN_WEIGHTS], *[delta_w[n] for n in TWIN_WEIGHTS],
            *[new_m[n] for n in TWIN_WEIGHTS], *[new_v[n] for n in TWIN_WEIGHTS])
```

```python
import functools
import math

import jax
import jax.numpy as jnp
from jax import lax
from jax.experimental import pallas as pl
from jax.experimental.pallas import tpu as pltpu

F32 = jnp.float32
BF16 = jnp.bfloat16

SEQ = 2048
D_MODEL = 1024
MEM_LEN = 256
DEPTH = 4
N_DEV = 8
EPS = 1e-6
NEG = -1e30
A_HEAD_DIM = 64
X_HEAD_DIM = 256
S5_GROUPS = 32
S5_STATE = 64
S5_GROUP_DIM = 16

ADAM_LR = 0.001
ADAM_B1 = 0.9
ADAM_B2 = 0.999
ADAM_EPS = 1e-08
ADAM_WD = 0.01
ADAM_STEP = 10

V7X_VMEM_LIMIT_BYTES = 56 * 1024 * 1024
MESH_AXES = ("x", "y", "c")


def _pc(body, *, name, out_shape, grid=None, in_specs=None, out_specs=None, scratch_shapes=(), aliases=None, sem=None):
    kw = {}
    if grid is not None:
        kw["grid"] = grid
    if in_specs is not None:
        kw["in_specs"] = in_specs
    if out_specs is not None:
        kw["out_specs"] = out_specs
    if aliases:
        kw["input_output_aliases"] = aliases
    return pl.pallas_call(
        body,
        name=name,
        out_shape=out_shape,
        scratch_shapes=list(scratch_shapes),
        compiler_params=pltpu.CompilerParams(dimension_semantics=sem, vmem_limit_bytes=V7X_VMEM_LIMIT_BYTES),
        interpret=False,
        **kw,
    )


def _cols(arr, c0=0, width=None, nsplit=1):
    width = arr.shape[1] - c0 if width is None else width
    assert c0 % width == 0 and width % nsplit == 0
    return (arr, c0, width, nsplit)


def _par(arr, nsplit=1):
    return (arr, nsplit)


def _ld(ref, nsplit):
    if nsplit == 1:
        return ref[...].astype(F32)
    if len(ref.shape) == 3:
        return tuple(ref[k].astype(F32) for k in range(nsplit))
    w = ref.shape[-1] // nsplit
    return tuple(ref[:, k * w:(k + 1) * w].astype(F32) for k in range(nsplit))


def _st(ref, val, nsplit, accumulate=False):
    if nsplit == 1:
        val = (val,)
    for k in range(nsplit):
        if nsplit == 1:
            idx = (Ellipsis,)
        elif len(ref.shape) == 3:
            idx = (k,)
        else:
            w = ref.shape[-1] // nsplit
            idx = (slice(None), slice(k * w, (k + 1) * w))
        if accumulate:
            ref[idx] += val[k].astype(ref.dtype)
        else:
            ref[idx] = val[k].astype(ref.dtype)


def _row_spec(tr, op):
    _, c0, w, _ = op
    return pl.BlockSpec((tr, w), lambda i, cb=c0 // w: (i, cb))


def _full_spec(arr):
    return pl.BlockSpec(arr.shape, lambda i, nd=arr.ndim: (0,) * nd)


def _rw_fwd(name, f, rows, pars, outs, tr):
    n_rows = rows[0][0].shape[0]
    nr, npar = len(rows), len(pars)

    def body(*refs):
        r = [_ld(refs[i], rows[i][3]) for i in range(nr)]
        p = [_ld(refs[nr + i], pars[i][1]) for i in range(npar)]
        res = f(r, p)
        for k, (_, _, ns) in enumerate(outs):
            _st(refs[nr + npar + k], res[k], ns)

    res = _pc(
        body, name=name, grid=(n_rows // tr,),
        in_specs=[_row_spec(tr, op) for op in rows] + [_full_spec(a) for a, _ in pars],
        out_specs=[pl.BlockSpec((tr, w), lambda i: (i, 0)) for w, _, _ in outs],
        out_shape=[jax.ShapeDtypeStruct((n_rows, w), dt) for w, dt, _ in outs],
        sem=("arbitrary",),
    )(*[op[0] for op in rows], *[a for a, _ in pars])
    return list(res)


def _rw_bwd(name, f, rows, pars, douts, drow, dpar, tr):
    n_rows = rows[0][0].shape[0]
    nr, npar = len(rows), len(pars)
    dgiven = [d for d in douts if d is not None]
    nd = len(dgiven)

    def body(*refs):
        r = [_ld(refs[i], rows[i][3]) for i in range(nr)]
        p = [_ld(refs[nr + i], pars[i][1]) for i in range(npar)]
        d = [_ld(refs[nr + npar + i], dgiven[i][3]) for i in range(nd)]
        orefs = refs[nr + npar + nd:]

        def g(dr, dp):
            rr, pp = list(r), list(p)
            for j, (idx, _) in enumerate(drow):
                rr[idx] = dr[j]
            for j, idx in enumerate(dpar):
                pp[idx] = dp[j]
            return tuple(f(rr, pp))

        out, vjp = jax.vjp(g, [r[idx] for idx, _ in drow], [p[idx] for idx in dpar])
        ct, j = [], 0
        for k, o in enumerate(out):
            if douts[k] is None:
                ct.append(jax.tree.map(jnp.zeros_like, o))
            else:
                ct.append(d[j])
                j += 1
        gdr, gdp = vjp(tuple(ct))
        for j, (idx, _) in enumerate(drow):
            _st(orefs[j], gdr[j], rows[idx][3])

        @pl.when(pl.program_id(0) == 0)
        def _():
            for j in range(len(dpar)):
                orefs[len(drow) + j][...] = jnp.zeros_like(orefs[len(drow) + j])

        for j, idx in enumerate(dpar):
            _st(orefs[len(drow) + j], gdp[j], pars[idx][1], accumulate=True)

    res = _pc(
        body, name=name, grid=(n_rows // tr,),
        in_specs=[_row_spec(tr, op) for op in rows] + [_full_spec(a) for a, _ in pars] + [_row_spec(tr, op) for op in dgiven],
        out_specs=[pl.BlockSpec((tr, rows[idx][2]), lambda i: (i, 0)) for idx, _ in drow] + [_full_spec(pars[idx][0]) for idx in dpar],
        out_shape=[jax.ShapeDtypeStruct((n_rows, rows[idx][2]), dt) for idx, dt in drow]
        + [jax.ShapeDtypeStruct(pars[idx][0].shape, F32) for idx in dpar],
        sem=("arbitrary",),
    )(*[op[0] for op in rows], *[a for a, _ in pars], *[op[0] for op in dgiven])
    res = list(res)
    return res[:len(drow)], res[len(drow):]


def _sigmoid(x):
    return jax.nn.sigmoid(x)


def _silu(x):
    return x * _sigmoid(x)


def _rms(x, g):
    return x * lax.rsqrt(jnp.mean(x * x, axis=-1, keepdims=True) + EPS) * g


def _f_rms(r, p):
    return [_rms(r[0], p[0])]


def _f_rms_res(r, p):
    return [r[0], _rms(r[0], p[0])]


def _f_cast(r, p):
    return [r[0]]


def _f_gate_ab(r, p):
    o, ga, mixed, gb = r
    return [(o * _silu(ga), mixed * p[0] * _silu(gb))]


def _f_sgu(r, p):
    u, v, gc = r
    lg, lb, w, b = p
    n = float(D_MODEL)
    mu = sum(jnp.sum(vk, axis=-1, keepdims=True) for vk in v) / n
    var = sum(jnp.sum(jnp.square(vk - mu), axis=-1, keepdims=True) for vk in v) / n
    rs = lax.rsqrt(var + EPS)
    t = w[0].shape[0]
    tri = lax.broadcasted_iota(jnp.int32, (t, t), 0) >= lax.broadcasted_iota(jnp.int32, (t, t), 1)
    outs = []
    for k in range(len(v)):
        vn = (v[k] - mu) * rs * lg[k] + lb[k]
        mixed = jnp.dot(jnp.where(tri, w[k], 0.0), vn, preferred_element_type=F32) + b[k]
        outs.append(u[k] * mixed * _silu(gc[k]))
    return [tuple(outs)]


def _gelu(x):
    return 0.5 * x * (1.0 + jnp.tanh(math.sqrt(2.0 / math.pi) * (x + 0.044715 * (x * x * x))))


def _f_gelu_y(r, p):
    yc, uf = r
    return [_gelu(yc + p[0] * uf)]


def _f_glu_gate(r, p):
    t12, gd = r
    return [t12[0] * _sigmoid(t12[1]) * _silu(gd)]


def _f_s5_prep(r, p):
    ar, ai, ldt = r
    dt = jnp.exp(ldt)
    mag = jnp.exp(dt * ar)
    abar_re = mag * jnp.cos(dt * ai)
    abar_im = mag * jnp.sin(dt * ai)
    nr, ni = abar_re - 1.0, abar_im
    inv = 1.0 / (ar * ar + ai * ai)
    return [abar_re, abar_im, (nr * ar + ni * ai) * inv, (ni * ar - nr * ai) * inv]


def _f_bbar(r, p):
    br, bi, cr, ci = r
    return [cr * br - ci * bi, cr * bi + ci * br]


def _loss_head(x, target, g):
    tr = 256
    n_rows, width = x.shape

    def f(xv, gv, tv):
        err = jnp.square(_rms(xv, gv) - tv)
        return 0.5 * jnp.mean(err, axis=-1, keepdims=True)

    def body(x_ref, t_ref, g_ref, loss_ref, dx_ref, dg_ref):
        @pl.when(pl.program_id(0) == 0)
        def _():
            loss_ref[...] = jnp.zeros_like(loss_ref)
            dg_ref[...] = jnp.zeros_like(dg_ref)

        tv = t_ref[...]
        row_loss, vjp = jax.vjp(lambda a, b: f(a, b, tv), x_ref[...], g_ref[...])
        dx, dg = vjp(jnp.ones_like(row_loss))
        dx_ref[...] = dx
        dg_ref[...] += dg
        loss_ref[...] += jnp.broadcast_to(jnp.sum(row_loss, axis=0, keepdims=True), loss_ref.shape)

    blk = pl.BlockSpec((tr, width), lambda i: (i, 0))
    one = pl.BlockSpec((1, width), lambda i: (0, 0))
    return _pc(
        body, name="loss_head", grid=(n_rows // tr,), in_specs=[blk, blk, one],
        out_specs=[pl.BlockSpec((1, 128), lambda i: (0, 0)), blk, one],
        out_shape=[jax.ShapeDtypeStruct((1, 128), F32), jax.ShapeDtypeStruct(x.shape, F32), jax.ShapeDtypeStruct((1, width), F32)],
        sem=("arbitrary",),
    )(x, target, g)


def _tile(n, cap):
    t = min(n, cap)
    while n % t:
        t -= 128
    assert t > 0
    return t


def _mm(name, a, b, *, ta=False, tb=False, out_dtype=F32, bl=None, a_off=0, a_width=None, res=None, out_buf=None, out_layers=None, ol=0):
    assert not (ta and tb)
    b2 = b.shape[-2:]
    if ta:
        kc = a.shape[0]
        m = a.shape[1] - a_off if a_width is None else a_width
        n = b2[1]
        assert b2[0] == kc
    else:
        m = a.shape[0]
        kc = a.shape[1] - a_off if a_width is None else a_width
        if tb:
            n = b2[0]
            assert b2[1] == kc
        else:
            n = b2[1]
            assert b2[0] == kc
    tm, tn, tk = _tile(m, 512), _tile(n, 1024), _tile(kc, 1024)
    nk = kc // tk
    if ta:
        assert a_off % tm == 0
        a_spec = pl.BlockSpec((tk, tm), lambda i, j, k, o=a_off // tm: (k, i + o))
        dims = (((0,), (0,)), ((), ()))
    else:
        assert a_off % tk == 0
        a_spec = pl.BlockSpec((tm, tk), lambda i, j, k, o=a_off // tk: (i, k + o))
        dims = (((1,), (1,)), ((), ())) if tb else (((1,), (0,)), ((), ()))
    b_blk, b_idx = ((tn, tk), lambda i, j, k: (j, k)) if tb else ((tk, tn), lambda i, j, k: (k, j))
    if b.ndim == 3:
        b_spec = pl.BlockSpec((None,) + b_blk, lambda i, j, k, f=b_idx, l=bl: (l,) + f(i, j, k))
    else:
        b_spec = pl.BlockSpec(b_blk, b_idx)
    in_specs, args = [a_spec, b_spec], [a, b]
    if res is not None:
        in_specs.append(pl.BlockSpec((tm, tn), lambda i, j, k: (i, j)))
        args.append(res)
    aliases = None
    if out_layers is not None:
        out_shape = jax.ShapeDtypeStruct((out_layers, m, n), out_dtype)
        out_spec = pl.BlockSpec((None, tm, tn), lambda i, j, k, l=ol: (l, i, j))
        if out_buf is not None:
            in_specs.append(pl.BlockSpec(memory_space=pl.ANY))
            args.append(out_buf)
            aliases = {len(args) - 1: 0}
    else:
        out_shape = jax.ShapeDtypeStruct((m, n), out_dtype)
        out_spec = pl.BlockSpec((tm, tn), lambda i, j, k: (i, j))
    has_res, has_buf = res is not None, out_buf is not None

    def body(*refs):
        a_ref, b_ref = refs[0], refs[1]
        res_ref = refs[2] if has_res else None
        o_ref, acc_ref = refs[-2], refs[-1]
        k = pl.program_id(2)

        @pl.when(k == 0)
        def _():
            acc_ref[...] = jnp.zeros_like(acc_ref)

        acc_ref[...] += lax.dot_general(a_ref[...].astype(BF16), b_ref[...].astype(BF16), dims, preferred_element_type=F32)

        @pl.when(k == nk - 1)
        def _():
            acc = acc_ref[...]
            if has_res:
                acc = acc + res_ref[...].astype(F32)
            o_ref[...] = acc.astype(o_ref.dtype)

    return _pc(
        body, name=name, grid=(m // tm, n // tn, nk), in_specs=in_specs, out_specs=out_spec, out_shape=out_shape,
        scratch_shapes=[pltpu.VMEM((tm, tn), F32)], aliases=aliases, sem=("parallel", "parallel", "arbitrary"),
    )(*args)


def _head_masks(width, nsub):
    lane = lax.broadcasted_iota(jnp.int32, (1, width), 1)
    hd = width // nsub
    return [(lane >= h * hd) & (lane < (h + 1) * hd) for h in range(nsub)]


def _dilated_log_count(row0, tq, ext):
    delta = (row0 + lax.broadcasted_iota(jnp.int32, (tq, ext), 0)) - lax.broadcasted_iota(jnp.int32, (tq, ext), 1)
    cnt = (delta <= 128).astype(jnp.int32) + (((delta & 3) == 0) & (delta <= 512)).astype(jnp.int32) + ((delta & 15) == 0).astype(jnp.int32)
    logc = jnp.where(cnt == 3, math.log(3.0), jnp.where(cnt == 2, math.log(2.0), 0.0))
    return jnp.where((delta >= 0) & (cnt > 0), logc, NEG)


def _softmax_rows(s):
    m = jnp.max(s, axis=-1, keepdims=True)
    p = jnp.exp(s - m)
    return p / jnp.sum(p, axis=-1, keepdims=True)


_NT = (((1,), (1,)), ((), ()))
_TN = (((0,), (0,)), ((), ()))


def _attn_fwd(name, qa, ka, va, *, qc, kc, vc, width, nblk, nsub, causal, tq, scale, out_dtype):
    sq, t_len = qa.shape[0], ka.shape[0]

    def body(q_ref, k_ref, v_ref, o_ref):
        kb = k_ref[...].astype(BF16)
        vb = v_ref[...].astype(BF16)
        masks = _head_masks(width, nsub)
        for r in range(sq // tq):
            ext = (r + 1) * tq if causal else t_len
            q = q_ref[r * tq:(r + 1) * tq, :].astype(F32)
            ke, ve = kb[:ext], vb[:ext]
            bias = _dilated_log_count(r * tq, tq, ext) if causal else None
            o = None
            for h in range(nsub):
                qm = (jnp.where(masks[h], q, 0.0) if nsub > 1 else q).astype(BF16)
                s = lax.dot_general(qm, ke, _NT, preferred_element_type=F32) * scale
                if causal:
                    s = jnp.where(bias > 0.5 * NEG, s + bias, NEG)
                pn = _softmax_rows(s).astype(BF16)
                oh = jnp.dot(pn, ve, preferred_element_type=F32)
                o = oh if o is None else jnp.where(masks[h], oh, o)
            o_ref[r * tq:(r + 1) * tq, :] = o.astype(o_ref.dtype)

    return _pc(
        body, name=name, grid=(nblk,),
        in_specs=[pl.BlockSpec((sq, width), lambda i, c=qc: (0, c + i)), pl.BlockSpec((t_len, width), lambda i, c=kc: (0, c + i)),
                  pl.BlockSpec((t_len, width), lambda i, c=vc: (0, c + i))],
        out_specs=pl.BlockSpec((sq, width), lambda i: (0, i)),
        out_shape=jax.ShapeDtypeStruct((sq, nblk * width), out_dtype), sem=("parallel",),
    )(qa, ka, va)


def _attn_bwd(name, qa, ka, va, doa, *, qc, kc, vc, width, nblk, nsub, causal, tq, scale, out_dtype):
    sq, t_len = qa.shape[0], ka.shape[0]

    def body(q_ref, k_ref, v_ref, do_ref, dq_ref, dk_ref, dv_ref, dk_acc, dv_acc):
        kb = k_ref[...].astype(BF16)
        vb = v_ref[...].astype(BF16)
        masks = _head_masks(width, nsub)
        dk_acc[...] = jnp.zeros_like(dk_acc)
        dv_acc[...] = jnp.zeros_like(dv_acc)
        for r in range(sq // tq):
            ext = (r + 1) * tq if causal else t_len
            q = q_ref[r * tq:(r + 1) * tq, :].astype(F32)
            do = do_ref[r * tq:(r + 1) * tq, :].astype(F32)
            ke, ve = kb[:ext], vb[:ext]
            bias = _dilated_log_count(r * tq, tq, ext) if causal else None
            dq = None
            for h in range(nsub):
                qm = (jnp.where(masks[h], q, 0.0) if nsub > 1 else q).astype(BF16)
                dom = (jnp.where(masks[h], do, 0.0) if nsub > 1 else do).astype(BF16)
                s = lax.dot_general(qm, ke, _NT, preferred_element_type=F32) * scale
                if causal:
                    s = jnp.where(bias > 0.5 * NEG, s + bias, NEG)
                pn = _softmax_rows(s)
                dpn = lax.dot_general(dom, ve, _NT, preferred_element_type=F32)
                ds = pn * (dpn - jnp.sum(pn * dpn, axis=-1, keepdims=True))
                dsb = (ds * scale).astype(BF16)
                dqh = jnp.dot(dsb, ke, preferred_element_type=F32)
                dq = dqh if dq is None else jnp.where(masks[h], dqh, dq)
                dk_acc[0:ext, :] += lax.dot_general(dsb, qm, _TN, preferred_element_type=F32)
                dv_acc[0:ext, :] += lax.dot_general(pn.astype(BF16), dom, _TN, preferred_element_type=F32)
            dq_ref[r * tq:(r + 1) * tq, :] = dq.astype(dq_ref.dtype)
        dk_ref[...] = dk_acc[...].astype(dk_ref.dtype)
        dv_ref[...] = dv_acc[...].astype(dv_ref.dtype)

    return _pc(
        body, name=name, grid=(nblk,),
        in_specs=[pl.BlockSpec((sq, width), lambda i, c=qc: (0, c + i)), pl.BlockSpec((t_len, width), lambda i, c=kc: (0, c + i)),
                  pl.BlockSpec((t_len, width), lambda i, c=vc: (0, c + i)), pl.BlockSpec((sq, width), lambda i: (0, i))],
        out_specs=[pl.BlockSpec((sq, width), lambda i: (0, i)), pl.BlockSpec((t_len, width), lambda i: (0, i)), pl.BlockSpec((t_len, width), lambda i: (0, i))],
        out_shape=[jax.ShapeDtypeStruct((sq, nblk * width), out_dtype), jax.ShapeDtypeStruct((t_len, nblk * width), out_dtype),
                   jax.ShapeDtypeStruct((t_len, nblk * width), out_dtype)],
        scratch_shapes=[pltpu.VMEM((t_len, width), F32), pltpu.VMEM((t_len, width), F32)], sem=("parallel",),
    )(qa, ka, va, doa)


_SELF = dict(qc=0, kc=8, vc=16, width=128, nblk=8, nsub=2, causal=True, tq=256, scale=A_HEAD_DIM ** -0.5)
_CROSS = dict(qc=0, kc=0, vc=4, width=256, nblk=4, nsub=1, causal=False, tq=512, scale=X_HEAD_DIM ** -0.5)


def _window_sum(x, g, row, backward):
    n = x.shape[0]

    def shift(y, k):
        if backward:
            return jnp.where(row < n - k, pltpu.roll(y, n - k, 0), 0.0)
        return jnp.where(row >= k, pltpu.roll(y, k, 0), 0.0)

    s2 = x + shift(x, 1)
    s4 = s2 + shift(s2, 2)
    s8 = s4 + shift(s4, 4)
    s16 = s8 + shift(s8, 8)
    return jnp.where(g == 0, s2, jnp.where(g == 1, s4, jnp.where(g == 2, s8, s16)))


def _pool(name, arr, c0, backward, out_dtype):
    n = arr.shape[0]
    gw = 256

    def body(v_ref, o_ref):
        g = pl.program_id(0)
        v = v_ref[...].astype(F32)
        row = lax.broadcasted_iota(jnp.int32, v.shape, 0)
        w = jnp.where(g == 0, 2, jnp.where(g == 1, 4, jnp.where(g == 2, 8, 16)))
        cnt = jnp.minimum(row + 1, w).astype(F32)
        if backward:
            o_ref[...] = (_window_sum(v / cnt, g, row, True) - v).astype(o_ref.dtype)
        else:
            o_ref[...] = (_window_sum(v, g, row, False) / cnt - v).astype(o_ref.dtype)

    return _pc(
        body, name=name, grid=(4,), in_specs=[pl.BlockSpec((n, gw), lambda i, c=c0 // gw: (0, c + i))],
        out_specs=pl.BlockSpec((n, gw), lambda i: (0, i)), out_shape=jax.ShapeDtypeStruct((n, 4 * gw), out_dtype), sem=("parallel",),
    )(arr)


_SCAN_ROWS = 256


def _scan_fwd(bu3, a2):
    n = bu3.shape[0]

    def body(bu_ref, a_ref, h_ref, carry):
        @pl.when(pl.program_id(0) == 0)
        def _():
            carry[...] = jnp.zeros_like(carry)

        ar, ai = a_ref[0:16, :], a_ref[16:32, :]

        def step(t, c):
            hr, hi = c
            nr = ar * hr - ai * hi + bu_ref[t, 0:16, :]
            ni = ar * hi + ai * hr + bu_ref[t, 16:32, :]
            h_ref[t, 0:16, :] = nr
            h_ref[t, 16:32, :] = ni
            return nr, ni

        hr, hi = lax.fori_loop(0, _SCAN_ROWS, step, (carry[0:16, :], carry[16:32, :]), unroll=8)
        carry[0:16, :] = hr
        carry[16:32, :] = hi

    blk = pl.BlockSpec((_SCAN_ROWS, 32, 128), lambda i: (i, 0, 0))
    return _pc(
        body, name="s5_scan_fwd", grid=(n // _SCAN_ROWS,), in_specs=[blk, pl.BlockSpec((32, 128), lambda i: (0, 0))], out_specs=blk,
        out_shape=jax.ShapeDtypeStruct(bu3.shape, F32), scratch_shapes=[pltpu.VMEM((32, 128), F32)], sem=("arbitrary",),
    )(bu3, a2)


def _scan_bwd(dh3, h3, a2):
    n = dh3.shape[0]
    nb = n // _SCAN_ROWS

    def body(dh_ref, h_ref, a_ref, dbu_ref, da_ref, carry):
        @pl.when(pl.program_id(0) == 0)
        def _():
            carry[...] = jnp.zeros_like(carry)
            da_ref[...] = jnp.zeros_like(da_ref)

        ar, ai = a_ref[0:16, :], a_ref[16:32, :]

        def step(tt, c):
            gr, gi, dar, dai = c
            t = _SCAN_ROWS - 1 - tt
            hr, hi = h_ref[t, 0:16, :], h_ref[t, 16:32, :]
            dar = dar + gr * hr + gi * hi
            dai = dai - gr * hi + gi * hr
            ngr = dh_ref[t, 0:16, :] + ar * gr + ai * gi
            ngi = dh_ref[t, 16:32, :] - ai * gr + ar * gi
            dbu_ref[t, 0:16, :] = ngr
            dbu_ref[t, 16:32, :] = ngi
            return ngr, ngi, dar, dai

        z = jnp.zeros((16, 128), F32)
        gr, gi, dar, dai = lax.fori_loop(0, _SCAN_ROWS, step, (carry[0:16, :], carry[16:32, :], z, z), unroll=8)
        carry[0:16, :] = gr
        carry[16:32, :] = gi
        da_ref[0:16, :] += dar
        da_ref[16:32, :] += dai

    blk = pl.BlockSpec((_SCAN_ROWS, 32, 128), lambda i: (nb - 1 - i, 0, 0))
    small = pl.BlockSpec((32, 128), lambda i: (0, 0))
    return _pc(
        body, name="s5_scan_bwd", grid=(nb,), in_specs=[blk, blk, small], out_specs=[blk, small],
        out_shape=[jax.ShapeDtypeStruct(dh3.shape, F32), jax.ShapeDtypeStruct((32, 128), F32)],
        scratch_shapes=[pltpu.VMEM((32, 128), F32)], sem=("arbitrary",),
    )(dh3, h3, a2)


def _bdense(bb_re, bb_im):
    eye = jnp.eye(S5_GROUPS, dtype=F32)

    def one(bb):
        return jnp.einsum("gph,gk->ghkp", bb.reshape(S5_GROUPS, S5_STATE, S5_GROUP_DIM), eye).reshape(512, 2048)

    return jnp.concatenate([one(bb_re), one(bb_im)], axis=1)


def _cdense(c_re, c_im):
    eye = jnp.eye(S5_GROUPS, dtype=F32)

    def one(cc):
        return jnp.einsum("ghp,gk->gpkh", cc, eye).reshape(2048, 512)

    return jnp.concatenate([one(c_re), -one(c_im)], axis=0)


def _pool_dense(pw):
    eye = jnp.eye(4, dtype=pw.dtype)
    return jnp.einsum("gcd,gk->gckd", pw, eye).reshape(1024, 1024)


def _row2(v):
    return v.reshape(1, -1)


def _even_fwd(x, W, i):
    hn = _rw_fwd("rms_fwd", _f_rms, [_cols(x)], [_par(_row2(W["norm_ab"][i]))], [(D_MODEL, BF16, 1)], 256)[0]
    z = _mm("mm_in_ab", hn, W["w_in_ab"], bl=i)
    o = _attn_fwd("attn_self_fwd", z, z, z, out_dtype=F32, **_SELF)
    pooled = _pool("pool_fwd", z, 4096, False, BF16)
    wp = _pool_dense(W["pool_w"][i])
    mixed = _mm("mm_pool", pooled, wp)
    scale = _row2(W["pool_scale"][i])
    ab = _rw_fwd("gate_ab_fwd", _f_gate_ab, [_cols(o), _cols(z, 3072, 1024), _cols(mixed), _cols(z, 5120, 1024)], [_par(scale)],
                 [(2048, BF16, 2)], 256)[0]
    x1 = _mm("mm_out_ab", ab, W["w_out_ab"], bl=i, res=x)
    return x1, dict(x=x, hn=hn, z=z, o=o, pooled=pooled, wp=wp, mixed=mixed, ab=ab)


def _even_bwd(dx1, sv, W, G, i):
    x, hn, z = sv["x"], sv["hn"], sv["z"]
    dab = _mm("mm_out_ab_dx", dx1, W["w_out_ab"], bl=i, tb=True)
    G["w_out_ab"] = _mm("mm_out_ab_dw", sv["ab"], dx1, ta=True, out_buf=G.get("w_out_ab"), out_layers=2, ol=i)
    scale = _row2(W["pool_scale"][i])
    (do, dga, dmixed, dgb), (dscale,) = _rw_bwd(
        "gate_ab_bwd", _f_gate_ab, [_cols(sv["o"]), _cols(z, 3072, 1024), _cols(sv["mixed"]), _cols(z, 5120, 1024)], [_par(scale)],
        [_cols(dab, nsplit=2)], [(0, F32), (1, BF16), (2, BF16), (3, BF16)], [0], 256)
    G["pool_scale"][i] = dscale.reshape(-1)
    dpooled = _mm("mm_pool_dx", dmixed, sv["wp"], tb=True)
    dwp = _mm("mm_pool_dw", sv["pooled"], dmixed, ta=True)
    G["pool_w"][i] = jnp.stack([dwp[g * 256:(g + 1) * 256, g * 256:(g + 1) * 256] for g in range(4)])
    dvb = _pool("pool_bwd", dpooled, 0, True, BF16)
    dq, dk, dv = _attn_bwd("attn_self_bwd", z, z, z, do, out_dtype=BF16, **_SELF)
    dz = jnp.concatenate([dq, dk, dv, dga, dvb, dgb], axis=1)
    dhn = _mm("mm_in_ab_dx", dz, W["w_in_ab"], bl=i, tb=True)
    G["w_in_ab"] = _mm("mm_in_ab_dw", hn, dz, ta=True, out_buf=G.get("w_in_ab"), out_layers=2, ol=i)
    (dx,), (dg,) = _rw_bwd("rms_bwd", _f_rms_res, [_cols(x)], [_par(_row2(W["norm_ab"][i]))], [_cols(dx1), _cols(dhn)], [(0, F32)], [0], 256)
    G["norm_ab"][i] = dg.reshape(-1)
    return dx


def _s5_params(W, i):
    ar, ai = W["s5_a_re"][i], W["s5_a_im"][i]
    ldt = W["s5_log_dt"][i].reshape(S5_GROUPS, 1)
    return [_cols(ar), _cols(ai), _cols(ldt)]


def _odd_fwd(x, W, i):
    hn = _rw_fwd("rms_fwd", _f_rms, [_cols(x)], [_par(_row2(W["norm_cd"][i]))], [(D_MODEL, BF16, 1)], 256)[0]
    z = _mm("mm_in_cd", hn, W["w_in_cd"], bl=i)
    sgu_p = [_par(_row2(W["sgu_ln_g"][i]), 4), _par(_row2(W["sgu_ln_b"][i]), 4), _par(W["sgu_w"][i], 4), _par(W["sgu_b"][i][..., None], 4)]
    c_out = _rw_fwd("sgu_fwd", _f_sgu, [_cols(z, 0, 1024, 4), _cols(z, 1024, 1024, 4), _cols(z, 2048, 1024, 4)], sgu_p, [(1024, BF16, 4)], 128)[0]
    prep_rows = _s5_params(W, i)
    abar_re, abar_im, coef_re, coef_im = _rw_fwd("s5_prep_fwd", _f_s5_prep, prep_rows, [], [(S5_STATE, F32, 1)] * 4, S5_GROUPS)
    bb_rows = [_cols(W["s5_b_re"][i].reshape(2048, 16)), _cols(W["s5_b_im"][i].reshape(2048, 16)), _cols(coef_re.reshape(2048, 1)), _cols(coef_im.reshape(2048, 1))]
    bb_re, bb_im = _rw_fwd("s5_bbar_fwd", _f_bbar, bb_rows, [], [(16, F32, 1)] * 2, 256)
    bd = _bdense(bb_re, bb_im).astype(BF16)
    cf = _cdense(W["s5_c_re"][i], W["s5_c_im"][i]).astype(BF16)
    a2 = jnp.concatenate([abar_re.reshape(16, 128), abar_im.reshape(16, 128)], axis=0)
    bu = _mm("mm_s5_bu", z, bd, a_off=3072, a_width=512)
    h3 = _scan_fwd(bu.reshape(SEQ, 32, 128), a2)
    h2 = h3.reshape(SEQ, 4096)
    yc = _mm("mm_s5_y", h2, cf)
    dpar = _row2(W["s5_d"][i])
    yg = _rw_fwd("gelu_fwd", _f_gelu_y, [_cols(yc), _cols(z, 3072, 512)], [_par(dpar)], [(512, BF16, 1)], 256)[0]
    w12 = jnp.concatenate([W["glu_w1"][i], W["glu_w2"][i]], axis=1)
    t12 = _mm("mm_glu", yg, w12)
    d_out = _rw_fwd("glu_gate_fwd", _f_glu_gate, [_cols(t12, nsplit=2), _cols(z, 3584, 512)], [], [(512, BF16, 1)], 256)[0]
    cd = jnp.concatenate([c_out, d_out], axis=1)
    x1 = _mm("mm_out_cd", cd, W["w_out_cd"], bl=i, res=x)
    sv = dict(x=x, hn=hn, z=z, sgu_p=sgu_p, prep_rows=prep_rows, bb_rows=bb_rows, bb=(bb_re, bb_im), bd=bd, cf=cf, a2=a2,
              h3=h3, h2=h2, yc=yc, dpar=dpar, yg=yg, w12=w12, t12=t12, cd=cd)
    return x1, sv


def _odd_bwd(dx1, sv, W, G, i):
    x, hn, z = sv["x"], sv["hn"], sv["z"]
    dcd = _mm("mm_out_cd_dx", dx1, W["w_out_cd"], bl=i, tb=True)
    G["w_out_cd"] = _mm("mm_out_cd_dw", sv["cd"], dx1, ta=True, out_buf=G.get("w_out_cd"), out_layers=2, ol=i)
    (du, dv, dgc), (dlg, dlb, dsw, dsb) = _rw_bwd(
        "sgu_bwd", _f_sgu, [_cols(z, 0, 1024, 4), _cols(z, 1024, 1024, 4), _cols(z, 2048, 1024, 4)], sv["sgu_p"],
        [_cols(dcd, 0, 1024, 4)], [(0, BF16), (1, BF16), (2, BF16)], [0, 1, 2, 3], 128)
    G["sgu_ln_g"][i], G["sgu_ln_b"][i] = dlg.reshape(-1), dlb.reshape(-1)
    G["sgu_w"][i], G["sgu_b"][i] = dsw, dsb[..., 0]
    (dt12, dgd), _ = _rw_bwd("glu_gate_bwd", _f_glu_gate, [_cols(sv["t12"], nsplit=2), _cols(z, 3584, 512)], [], [_cols(dcd, 1024, 512)],
                             [(0, BF16), (1, BF16)], [], 256)
    dyg = _mm("mm_glu_dx", dt12, sv["w12"], tb=True)
    dw12 = _mm("mm_glu_dw", sv["yg"], dt12, ta=True)
    G["glu_w1"][i], G["glu_w2"][i] = dw12[:, :512], dw12[:, 512:]
    (dyc, duf1), (dd,) = _rw_bwd("gelu_bwd", _f_gelu_y, [_cols(sv["yc"]), _cols(z, 3072, 512)], [_par(sv["dpar"])], [_cols(dyg)],
                                 [(0, BF16), (1, F32)], [0], 256)
    G["s5_d"][i] = dd.reshape(-1)
    dh2 = _mm("mm_s5_y_dx", dyc, sv["cf"], tb=True)
    dcf = _mm("mm_s5_y_dw", sv["h2"], dyc, ta=True)
    _, cvjp = jax.vjp(_cdense, W["s5_c_re"][i], W["s5_c_im"][i])
    G["s5_c_re"][i], G["s5_c_im"][i] = cvjp(dcf)
    dbu3, da2 = _scan_bwd(dh2.reshape(SEQ, 32, 128), sv["h3"], sv["a2"])
    dbu = dbu3.reshape(SEQ, 4096)
    duf2 = _mm("mm_s5_bu_dx", dbu, sv["bd"], tb=True)
    dbd = _mm("mm_s5_bu_dw", z, dbu, ta=True, a_off=3072, a_width=512)
    _, bvjp = jax.vjp(_bdense, *sv["bb"])
    dbb_re, dbb_im = bvjp(dbd)
    (dbr, dbi, dcr, dci), _ = _rw_bwd("s5_bbar_bwd", _f_bbar, sv["bb_rows"], [], [_cols(dbb_re), _cols(dbb_im)],
                                      [(0, F32), (1, F32), (2, F32), (3, F32)], [], 256)
    G["s5_b_re"][i], G["s5_b_im"][i] = dbr.reshape(S5_GROUPS, S5_STATE, S5_GROUP_DIM), dbi.reshape(S5_GROUPS, S5_STATE, S5_GROUP_DIM)
    douts = [_cols(da2[0:16].reshape(S5_GROUPS, S5_STATE)), _cols(da2[16:32].reshape(S5_GROUPS, S5_STATE)),
             _cols(dcr.reshape(S5_GROUPS, S5_STATE)), _cols(dci.reshape(S5_GROUPS, S5_STATE))]
    (dar, dai, dldt), _ = _rw_bwd("s5_prep_bwd", _f_s5_prep, sv["prep_rows"], [], douts, [(0, F32), (1, F32), (2, F32)], [], S5_GROUPS)
    G["s5_a_re"][i], G["s5_a_im"][i], G["s5_log_dt"][i] = dar, dai, dldt.reshape(-1)
    dxd = (duf1 + duf2).astype(BF16)
    dz = jnp.concatenate([du, dv, dgc, dxd, dgd], axis=1)
    dhn = _mm("mm_in_cd_dx", dz, W["w_in_cd"], bl=i, tb=True)
    G["w_in_cd"] = _mm("mm_in_cd_dw", hn, dz, ta=True, out_buf=G.get("w_in_cd"), out_layers=2, ol=i)
    (dx,), (dg,) = _rw_bwd("rms_bwd", _f_rms_res, [_cols(x)], [_par(_row2(W["norm_cd"][i]))], [_cols(dx1), _cols(dhn)], [(0, F32)], [0], 256)
    G["norm_cd"][i] = dg.reshape(-1)
    return dx


def _cross_fwd(x1, mem_n, W, l):
    hx = _rw_fwd("rms_fwd", _f_rms, [_cols(x1)], [_par(_row2(W["norm_x"][l]))], [(D_MODEL, BF16, 1)], 256)[0]
    qx = _mm("mm_xq", hx, W["w_xq"], bl=l, out_dtype=BF16)
    kv = _mm("mm_xkv", mem_n, W["w_xkv"], bl=l, out_dtype=BF16)
    ox = _attn_fwd("attn_cross_fwd", qx, kv, kv, out_dtype=BF16, **_CROSS)
    x2 = _mm("mm_xo", ox, W["w_xo"], bl=l, res=x1)
    return x2, dict(x1=x1, hx=hx, qx=qx, kv=kv, ox=ox)


def _cross_bwd(dx2, dmem_n, sv, mem_n, W, G, l):
    dox = _mm("mm_xo_dx", dx2, W["w_xo"], bl=l, tb=True, out_dtype=BF16)
    G["w_xo"] = _mm("mm_xo_dw", sv["ox"], dx2, ta=True, out_buf=G.get("w_xo"), out_layers=DEPTH, ol=l)
    dqx, dk, dv = _attn_bwd("attn_cross_bwd", sv["qx"], sv["kv"], sv["kv"], dox, out_dtype=BF16, **_CROSS)
    dkv = jnp.concatenate([dk, dv], axis=1)
    dhx = _mm("mm_xq_dx", dqx, W["w_xq"], bl=l, tb=True)
    G["w_xq"] = _mm("mm_xq_dw", sv["hx"], dqx, ta=True, out_buf=G.get("w_xq"), out_layers=DEPTH, ol=l)
    dmem_n = _mm("mm_xkv_dx", dkv, W["w_xkv"], bl=l, tb=True, res=dmem_n)
    G["w_xkv"] = _mm("mm_xkv_dw", mem_n, dkv, ta=True, out_buf=G.get("w_xkv"), out_layers=DEPTH, ol=l)
    (dx1,), (dg,) = _rw_bwd("rms_bwd", _f_rms_res, [_cols(sv["x1"])], [_par(_row2(W["norm_x"][l]))], [_cols(dx2), _cols(dhx)], [(0, F32)], [0], 256)
    G["norm_x"][l] = dg.reshape(-1)
    return dx1, dmem_n


def _local_step(x, mem, target, W):
    G = {k: [None, None] for k in ("pool_scale", "pool_w", "norm_ab", "norm_cd", "sgu_ln_g", "sgu_ln_b", "sgu_w", "sgu_b", "glu_w1", "glu_w2",
                                  "s5_d", "s5_c_re", "s5_c_im", "s5_b_re", "s5_b_im", "s5_a_re", "s5_a_im", "s5_log_dt")}
    G["norm_x"] = [None] * DEPTH
    mem_rows = [_cols(mem)]
    mem_par = [_par(_row2(W["mem_norm"]))]
    mem_n = _rw_fwd("rms_fwd_mem", _f_rms, mem_rows, mem_par, [(D_MODEL, BF16, 1)], 256)[0]
    saved = []
    for layer in range(DEPTH):
        i = layer // 2
        if layer % 2 == 0:
            x, sv = _even_fwd(x, W, i)
        else:
            x, sv = _odd_fwd(x, W, i)
        x, svx = _cross_fwd(x, mem_n, W, layer)
        saved.append((sv, svx))
    loss, dx, dfinal = _loss_head(x, target, _row2(W["final_norm"]))
    G["final_norm"] = dfinal.reshape(-1)
    dmem_n = None
    for layer in reversed(range(DEPTH)):
        i = layer // 2
        sv, svx = saved[layer]
        dx, dmem_n = _cross_bwd(dx, dmem_n, svx, mem_n, W, G, layer)
        dx = _even_bwd(dx, sv, W, G, i) if layer % 2 == 0 else _odd_bwd(dx, sv, W, G, i)
    _, (dmn,) = _rw_bwd("rms_bwd_mem", _f_rms, mem_rows, mem_par, [_cols(dmem_n)], [], [0], 256)
    G["mem_norm"] = dmn.reshape(-1)
    for k, v in G.items():
        if isinstance(v, list):
            G[k] = jnp.stack(v)
    return loss, dx, G


def _mesh_pos():
    return lax.axis_index("x"), lax.axis_index("y"), lax.axis_index("c")


def _peer(pos, k):
    x, y, c = pos
    return (x ^ ((k >> 2) & 1), y ^ ((k >> 1) & 1), c ^ (k & 1))


def _lin(pos):
    return 4 * pos[0] + 2 * pos[1] + pos[2]


def _part(ref, spec, idx):
    if spec is None:
        return ref
    ax, n = spec
    return ref.at[(slice(None),) * ax + (pl.ds(pl.multiple_of(idx * n, n), n),)]


_ANY = pl.BlockSpec(memory_space=pl.ANY)


def _all_gather(name, shards, specs, fulls):
    nt = len(shards)

    def body(*refs):
        ins, outs = refs[:nt], refs[nt:2 * nt]
        send_sems, recv_sems, local_sems = refs[2 * nt:]
        pos = _mesh_pos()
        me = _lin(pos)
        local = [pltpu.make_async_copy(ins[t], _part(outs[t], specs[t], me), local_sems.at[t]) for t in range(nt)]
        for cp in local:
            cp.start()

        def remote(t, k, block):
            return pltpu.make_async_remote_copy(
                src_ref=ins[t], dst_ref=_part(outs[t], specs[t], block), send_sem=send_sems.at[t, k - 1], recv_sem=recv_sems.at[t, k - 1],
                device_id=_peer(pos, k), device_id_type=pl.DeviceIdType.MESH)

        for k in range(1, N_DEV):
            for t in range(nt):
                remote(t, k, me).start()
        for k in range(1, N_DEV):
            for t in range(nt):
                remote(t, k, me).wait_send()
        for k in range(1, N_DEV):
            for t in range(nt):
                remote(t, k, _lin(_peer(pos, k))).wait_recv()
        for cp in local:
            cp.wait()

    return _pc(
        body, name=name, in_specs=[_ANY] * nt, out_specs=[_ANY] * nt,
        out_shape=[jax.ShapeDtypeStruct(fulls[t], shards[t].dtype) for t in range(nt)],
        scratch_shapes=[pltpu.SemaphoreType.DMA((nt, N_DEV - 1)), pltpu.SemaphoreType.DMA((nt, N_DEV - 1)), pltpu.SemaphoreType.DMA((nt,))],
    )(*shards)


def _scatter(name, wholes, specs):
    nt = len(wholes)

    def shard_shape(t):
        if specs[t] is None:
            return wholes[t].shape
        ax, n = specs[t]
        return wholes[t].shape[:ax] + (n,) + wholes[t].shape[ax + 1:]

    def body(*refs):
        ins, outs = refs[:nt], refs[nt:2 * nt]
        send_sems, recv_sems, local_sems = refs[2 * nt:]
        pos = _mesh_pos()
        me = _lin(pos)
        local = [pltpu.make_async_copy(_part(ins[t], specs[t], me), outs[t].at[me], local_sems.at[t]) for t in range(nt)]
        for cp in local:
            cp.start()

        def remote(t, k, slot):
            peer = _peer(pos, k)
            return pltpu.make_async_remote_copy(
                src_ref=_part(ins[t], specs[t], _lin(peer)), dst_ref=outs[t].at[slot], send_sem=send_sems.at[t, k - 1], recv_sem=recv_sems.at[t, k - 1],
                device_id=peer, device_id_type=pl.DeviceIdType.MESH)

        for k in range(1, N_DEV):
            for t in range(nt):
                remote(t, k, me).start()
        for k in range(1, N_DEV):
            for t in range(nt):
                remote(t, k, me).wait_send()
        for k in range(1, N_DEV):
            for t in range(nt):
                remote(t, k, _lin(_peer(pos, k))).wait_recv()
        for cp in local:
            cp.wait()

    return _pc(
        body, name=name, in_specs=[_ANY] * nt, out_specs=[_ANY] * nt,
        out_shape=[jax.ShapeDtypeStruct((N_DEV,) + shard_shape(t), wholes[t].dtype) for t in range(nt)],
        scratch_shapes=[pltpu.SemaphoreType.DMA((nt, N_DEV - 1)), pltpu.SemaphoreType.DMA((nt, N_DEV - 1)), pltpu.SemaphoreType.DMA((nt,))],
    )(*wholes)


def _adam(name, w, parts, m, v):
    r, c = w.shape
    tr = _row_block(r, c, 1 << 20)

    def body(w_ref, p_ref, m_ref, v_ref, g_ref, d_ref, nm_ref, nv_ref):
        g = p_ref[0]
        for k in range(1, N_DEV):
            g = g + p_ref[k]
        mm = ADAM_B1 * m_ref[...] + (1.0 - ADAM_B1) * g
        vv = ADAM_B2 * v_ref[...] + (1.0 - ADAM_B2) * jnp.square(g)
        m_hat = mm / (1.0 - ADAM_B1 ** ADAM_STEP)
        v_hat = vv / (1.0 - ADAM_B2 ** ADAM_STEP)
        g_ref[...] = g
        d_ref[...] = -ADAM_LR * (m_hat / (jnp.sqrt(v_hat) + ADAM_EPS) + ADAM_WD * w_ref[...])
        nm_ref[...] = mm
        nv_ref[...] = vv

    blk = pl.BlockSpec((tr, c), lambda i: (i, 0))
    return _pc(
        body, name=name, grid=(r // tr,), in_specs=[blk, pl.BlockSpec((N_DEV, tr, c), lambda i: (0, i, 0)), blk, blk], out_specs=[blk] * 4,
        out_shape=[jax.ShapeDtypeStruct((r, c), F32)] * 4, sem=("parallel",),
    )(w, parts, m, v)


def _pack(arrs):
    out = []
    for a in arrs:
        f = a.reshape(-1)
        out.append(jnp.pad(f, (0, (-f.shape[0]) % 1024)).reshape(-1, 128))
    return jnp.concatenate(out, axis=0)


def _unpack(packed, shapes):
    out, r = [], 0
    for s in shapes:
        n = math.prod(s)
        rows = (n + 1023) // 1024 * 8
        out.append(packed[r:r + rows].reshape(-1)[:n].reshape(s))
        r += rows
    return out


_BIG = [
    ("w_in_ab", 2, (2, 1024, 6144)), ("pool_w", 2, (2, 4, 256, 256)), ("w_out_ab", 1, (2, 2048, 1024)), ("w_in_cd", 2, (2, 1024, 4096)),
    ("glu_w1", 1, (2, 512, 512)), ("glu_w2", 1, (2, 512, 512)), ("w_out_cd", 1, (2, 1536, 1024)), ("w_xq", 1, (4, 1024, 1024)),
    ("w_xkv", 2, (4, 1024, 2048)), ("w_xo", 1, (4, 1024, 1024)),
]
_SMALL_SPLIT = ["norm_cd", "sgu_ln_g", "sgu_ln_b", "s5_d"]
_REPLICATED = ["norm_ab", "pool_scale", "sgu_w", "sgu_b", "s5_a_re", "s5_a_im", "s5_log_dt", "s5_b_re", "s5_b_im", "s5_c_re", "s5_c_im",
               "norm_x", "mem_norm", "final_norm"]
_WEIGHTS = ["norm_ab", "w_in_ab", "pool_w", "pool_scale", "w_out_ab", "norm_cd", "w_in_cd", "sgu_ln_g", "sgu_ln_b", "sgu_w", "sgu_b", "s5_a_re",
            "s5_a_im", "s5_log_dt", "s5_b_re", "s5_b_im", "s5_c_re", "s5_c_im", "s5_d", "glu_w1", "glu_w2", "w_out_cd", "norm_x", "w_xq",
            "w_xkv", "w_xo", "mem_norm", "final_norm"]


def _rows2d(a):
    return a.reshape(-1, a.shape[-1])


def _small_rows(block):
    return jnp.pad(block, ((0, 0), (0, 128 - block.shape[1])))


def kernel(x, mem, norm_ab, w_in_ab, pool_w, pool_scale, w_out_ab, norm_cd, w_in_cd, sgu_ln_g, sgu_ln_b, sgu_w, sgu_b, s5_a_re, s5_a_im, s5_log_dt, s5_b_re, s5_b_im, s5_c_re, s5_c_im, s5_d, glu_w1, glu_w2, w_out_cd, norm_x, w_xq, w_xkv, w_xo, mem_norm, final_norm, loss_target, m_norm_ab, m_w_in_ab, m_pool_w, m_pool_scale, m_w_out_ab, m_norm_cd, m_w_in_cd, m_sgu_ln_g, m_sgu_ln_b, m_sgu_w, m_sgu_b, m_s5_a_re, m_s5_a_im, m_s5_log_dt, m_s5_b_re, m_s5_b_im, m_s5_c_re, m_s5_c_im, m_s5_d, m_glu_w1, m_glu_w2, m_w_out_cd, m_norm_x, m_w_xq, m_w_xkv, m_w_xo, m_mem_norm, m_final_norm, v_norm_ab, v_w_in_ab, v_pool_w, v_pool_scale, v_w_out_ab, v_norm_cd, v_w_in_cd, v_sgu_ln_g, v_sgu_ln_b, v_sgu_w, v_sgu_b, v_s5_a_re, v_s5_a_im, v_s5_log_dt, v_s5_b_re, v_s5_b_im, v_s5_c_re, v_s5_c_im, v_s5_d, v_glu_w1, v_glu_w2, v_w_out_cd, v_norm_x, v_w_xq, v_w_xkv, v_w_xo, v_mem_norm, v_final_norm):
    args = locals()
    w = {n: args[n] for n in _WEIGHTS}
    m = {n: args["m_" + n] for n in _WEIGHTS}
    v = {n: args["v_" + n] for n in _WEIGHTS}

    shards, specs, fulls = [], [], []
    for name, ax, full in _BIG:
        blk = w[name]
        b2 = _rows2d(blk)
        cast = _rw_fwd("cast_" + name, _f_cast, [_cols(b2)], [], [(b2.shape[1], BF16, 1)], _tile_rows(b2.shape))[0]
        shards.append(cast.reshape(blk.shape))
        specs.append((ax, blk.shape[ax]))
        fulls.append(full)
    small = jnp.concatenate([_small_rows(w[n]) for n in _SMALL_SPLIT], axis=0)
    shards.append(small)
    specs.append((0, 8))
    fulls.append((N_DEV * 8, 128))
    gathered = _all_gather("gather_weights", shards, specs, fulls)
    W = {name: gathered[t] for t, (name, _, _) in enumerate(_BIG)}
    sm = gathered[-1].reshape(N_DEV, 4, 2, 128)
    for j, n in enumerate(_SMALL_SPLIT):
        width = w[n].shape[1]
        W[n] = sm[:, j, :, :width].transpose(1, 0, 2).reshape(2, N_DEV * width)
    for n in _REPLICATED:
        W[n] = w[n]

    loss, dx, G = _local_step(x[0], mem[0], loss_target[0], W)
    loss = lax.psum(loss[0, 0], MESH_AXES)

    wholes = [G[name] for name, _, _ in _BIG]
    sspecs = [(ax, w[name].shape[ax]) for name, ax, _ in _BIG]
    gs = jnp.stack([G[n].reshape(2, N_DEV, -1).transpose(1, 0, 2) for n in _SMALL_SPLIT[:3]]
                   + [jnp.pad(G["s5_d"].reshape(2, N_DEV, -1).transpose(1, 0, 2), ((0, 0), (0, 0), (0, 64)))], axis=1)
    wholes.append(gs.reshape(N_DEV * 8, 128))
    sspecs.append((0, 8))
    wholes.append(_pack([G[n] for n in _REPLICATED]))
    sspecs.append(None)
    parts = _scatter("scatter_grads", wholes, sspecs)

    out = {}
    for t, (name, _, _) in enumerate(_BIG):
        shp = w[name].shape
        r2 = _rows2d(w[name]).shape
        res = _adam("adam_" + name, _rows2d(w[name]), parts[t].reshape((N_DEV,) + r2), _rows2d(m[name]), _rows2d(v[name]))
        out[name] = [a.reshape(shp) for a in res]
    pk = lambda d: jnp.concatenate([_small_rows(d[n]) for n in _SMALL_SPLIT], axis=0)
    res = _adam("adam_small", pk(w), parts[len(_BIG)], pk(m), pk(v))
    for j, n in enumerate(_SMALL_SPLIT):
        out[n] = [a[2 * j:2 * j + 2, :w[n].shape[1]] for a in res]
    pk = lambda d: _pack([d[n] for n in _REPLICATED])
    res = _adam("adam_replicated", pk(w), parts[len(_BIG) + 1], pk(m), pk(v))
    shapes = [w[n].shape for n in _REPLICATED]
    un = [_unpack(a, shapes) for a in res]
    for j, n in enumerate(_REPLICATED):
        out[n] = [un[q][j] for q in range(4)]

    return (loss, dx[None], *[out[n][0] for n in _WEIGHTS], *[out[n][1] for n in _WEIGHTS], *[out[n][2] for n in _WEIGHTS],
            *[out[n][3] for n in _WEIGHTS])


def _row_block(r, c, limit):
    best = None
    for tr in range(16, r + 1, 16):
        if r % tr == 0 and tr * c * 4 <= limit:
            best = tr
    return r if best is None else best


def _tile_rows(shape):
    return _row_block(shape[0], shape[1], 2 << 20)
```

```python
import functools
import math

import jax
import jax.numpy as jnp
from jax import lax
from jax.experimental import pallas as pl
from jax.experimental.pallas import tpu as pltpu

F32 = jnp.float32
BF16 = jnp.bfloat16

SEQ = 2048
D_MODEL = 1024
MEM_LEN = 256
DEPTH = 4
N_DEV = 8
EPS = 1e-6
NEG = -1e30
A_HEAD_DIM = 64
X_HEAD_DIM = 256
S5_GROUPS = 32
S5_STATE = 64
S5_GROUP_DIM = 16

ADAM_LR = 0.001
ADAM_B1 = 0.9
ADAM_B2 = 0.999
ADAM_EPS = 1e-08
ADAM_WD = 0.01
ADAM_STEP = 10

V7X_VMEM_LIMIT_BYTES = 56 * 1024 * 1024
MESH_AXES = ("x", "y", "c")


def _pc(body, *, name, out_shape, grid=None, in_specs=None, out_specs=None, scratch_shapes=(), aliases=None, sem=None):
    kw = {}
    if grid is not None:
        kw["grid"] = grid
    if in_specs is not None:
        kw["in_specs"] = in_specs
    if out_specs is not None:
        kw["out_specs"] = out_specs
    if aliases:
        kw["input_output_aliases"] = aliases
    return pl.pallas_call(
        body,
        name=name,
        out_shape=out_shape,
        scratch_shapes=list(scratch_shapes),
        compiler_params=pltpu.CompilerParams(dimension_semantics=sem, vmem_limit_bytes=V7X_VMEM_LIMIT_BYTES),
        interpret=False,
        **kw,
    )


def _cols(arr, c0=0, width=None, nsplit=1, r0=0):
    width = arr.shape[1] - c0 if width is None else width
    assert c0 % width == 0 and width % nsplit == 0
    return (arr, c0, width, nsplit, r0)


def _par(arr, nsplit=1):
    return (arr, nsplit)


def _ld(ref, nsplit):
    if nsplit == 1:
        return ref[...].astype(F32)
    if len(ref.shape) == 3:
        return tuple(ref[k].astype(F32) for k in range(nsplit))
    w = ref.shape[-1] // nsplit
    return tuple(ref[:, k * w:(k + 1) * w].astype(F32) for k in range(nsplit))


def _st(ref, val, nsplit, accumulate=False):
    if nsplit == 1:
        val = (val,)
    for k in range(nsplit):
        if nsplit == 1:
            idx = (Ellipsis,)
        elif len(ref.shape) == 3:
            idx = (k,)
        else:
            w = ref.shape[-1] // nsplit
            idx = (slice(None), slice(k * w, (k + 1) * w))
        if accumulate:
            ref[idx] += val[k].astype(ref.dtype)
        else:
            ref[idx] = val[k].astype(ref.dtype)


def _row_spec(tr, op):
    _, c0, w, _, r0 = op
    assert r0 % tr == 0
    return pl.BlockSpec((tr, w), lambda i, cb=c0 // w, rb=r0 // tr: (i + rb, cb))


def _full_spec(arr):
    return pl.BlockSpec(arr.shape, lambda i, nd=arr.ndim: (0,) * nd)


def _rw_fwd(name, f, rows, pars, outs, tr, n_rows=None):
    n_rows = rows[0][0].shape[0] if n_rows is None else n_rows
    nr, npar = len(rows), len(pars)

    def body(*refs):
        r = [_ld(refs[i], rows[i][3]) for i in range(nr)]
        p = [_ld(refs[nr + i], pars[i][1]) for i in range(npar)]
        res = f(r, p)
        for k, (_, _, ns) in enumerate(outs):
            _st(refs[nr + npar + k], res[k], ns)

    res = _pc(
        body, name=name, grid=(n_rows // tr,),
        in_specs=[_row_spec(tr, op) for op in rows] + [_full_spec(a) for a, _ in pars],
        out_specs=[pl.BlockSpec((tr, w), lambda i: (i, 0)) for w, _, _ in outs],
        out_shape=[jax.ShapeDtypeStruct((n_rows, w), dt) for w, dt, _ in outs],
        sem=("arbitrary",),
    )(*[op[0] for op in rows], *[a for a, _ in pars])
    return list(res)


def _rw_bwd(name, f, rows, pars, douts, drow, dpar, tr):
    n_rows = rows[0][0].shape[0]
    nr, npar = len(rows), len(pars)
    dgiven = [d for d in douts if d is not None]
    nd = len(dgiven)

    def body(*refs):
        r = [_ld(refs[i], rows[i][3]) for i in range(nr)]
        p = [_ld(refs[nr + i], pars[i][1]) for i in range(npar)]
        d = [_ld(refs[nr + npar + i], dgiven[i][3]) for i in range(nd)]
        orefs = refs[nr + npar + nd:]

        def g(dr, dp):
            rr, pp = list(r), list(p)
            for j, (idx, _) in enumerate(drow):
                rr[idx] = dr[j]
            for j, idx in enumerate(dpar):
                pp[idx] = dp[j]
            return tuple(f(rr, pp))

        out, vjp = jax.vjp(g, [r[idx] for idx, _ in drow], [p[idx] for idx in dpar])
        ct, j = [], 0
        for k, o in enumerate(out):
            if douts[k] is None:
                ct.append(jax.tree.map(jnp.zeros_like, o))
            else:
                ct.append(d[j])
                j += 1
        gdr, gdp = vjp(tuple(ct))
        for j, (idx, _) in enumerate(drow):
            _st(orefs[j], gdr[j], rows[idx][3])

        @pl.when(pl.program_id(0) == 0)
        def _():
            for j in range(len(dpar)):
                orefs[len(drow) + j][...] = jnp.zeros_like(orefs[len(drow) + j])

        for j, idx in enumerate(dpar):
            _st(orefs[len(drow) + j], gdp[j], pars[idx][1], accumulate=True)

    res = _pc(
        body, name=name, grid=(n_rows // tr,),
        in_specs=[_row_spec(tr, op) for op in rows] + [_full_spec(a) for a, _ in pars] + [_row_spec(tr, op) for op in dgiven],
        out_specs=[pl.BlockSpec((tr, rows[idx][2]), lambda i: (i, 0)) for idx, _ in drow] + [_full_spec(pars[idx][0]) for idx in dpar],
        out_shape=[jax.ShapeDtypeStruct((n_rows, rows[idx][2]), dt) for idx, dt in drow]
        + [jax.ShapeDtypeStruct(pars[idx][0].shape, F32) for idx in dpar],
        sem=("arbitrary",),
    )(*[op[0] for op in rows], *[a for a, _ in pars], *[op[0] for op in dgiven])
    res = list(res)
    return res[:len(drow)], res[len(drow):]


def _sigmoid(x):
    return jax.nn.sigmoid(x)


def _silu(x):
    return x * _sigmoid(x)


def _rms(x, g):
    return x * lax.rsqrt(jnp.mean(x * x, axis=-1, keepdims=True) + EPS) * g


def _f_rms(r, p):
    return [_rms(r[0], p[0])]


def _f_rms_res(r, p):
    return [r[0], _rms(r[0], p[0])]


def _f_cast(r, p):
    return [r[0]]


def _f_gate_ab(r, p):
    o, ga, mixed, gb = r
    return [(o * _silu(ga), mixed * p[0] * _silu(gb))]


def _f_sgu(r, p):
    u, v, gc = r
    lg, lb, w, b = p
    n = float(D_MODEL)
    mu = sum(jnp.sum(vk, axis=-1, keepdims=True) for vk in v) / n
    var = sum(jnp.sum(jnp.square(vk - mu), axis=-1, keepdims=True) for vk in v) / n
    rs = lax.rsqrt(var + EPS)
    t = w[0].shape[0]
    tri = lax.broadcasted_iota(jnp.int32, (t, t), 0) >= lax.broadcasted_iota(jnp.int32, (t, t), 1)
    outs = []
    for k in range(len(v)):
        vn = (v[k] - mu) * rs * lg[k] + lb[k]
        mixed = jnp.dot(jnp.where(tri, w[k], 0.0), vn, preferred_element_type=F32) + b[k]
        outs.append(u[k] * mixed * _silu(gc[k]))
    return [tuple(outs)]


def _gelu(x):
    return 0.5 * x * (1.0 + jnp.tanh(math.sqrt(2.0 / math.pi) * (x + 0.044715 * (x * x * x))))


def _f_gelu_y(r, p):
    yc, uf = r
    return [_gelu(yc + p[0] * uf)]


def _f_glu_gate(r, p):
    t12, gd = r
    return [t12[0] * _sigmoid(t12[1]) * _silu(gd)]


def _f_s5_prep(r, p):
    ar, ai, ldt = r
    dt = jnp.exp(ldt)
    mag = jnp.exp(dt * ar)
    abar_re = mag * jnp.cos(dt * ai)
    abar_im = mag * jnp.sin(dt * ai)
    nr, ni = abar_re - 1.0, abar_im
    inv = 1.0 / (ar * ar + ai * ai)
    return [abar_re, abar_im, (nr * ar + ni * ai) * inv, (ni * ar - nr * ai) * inv]


def _f_bbar(r, p):
    br, bi, cr, ci = r
    return [cr * br - ci * bi, cr * bi + ci * br]


def _loss_head(x, target, g):
    tr = 256
    n_rows, width = x.shape

    def f(xv, gv, tv):
        err = jnp.square(_rms(xv, gv) - tv)
        return 0.5 * jnp.mean(err, axis=-1, keepdims=True)

    def body(x_ref, t_ref, g_ref, loss_ref, dx_ref, dg_ref):
        @pl.when(pl.program_id(0) == 0)
        def _():
            loss_ref[...] = jnp.zeros_like(loss_ref)
            dg_ref[...] = jnp.zeros_like(dg_ref)

        tv = t_ref[...]
        row_loss, vjp = jax.vjp(lambda a, b: f(a, b, tv), x_ref[...], g_ref[...])
        dx, dg = vjp(jnp.ones_like(row_loss))
        dx_ref[...] = dx
        dg_ref[...] += dg
        loss_ref[...] += jnp.broadcast_to(jnp.sum(row_loss, axis=0, keepdims=True), loss_ref.shape)

    blk = pl.BlockSpec((tr, width), lambda i: (i, 0))
    one = pl.BlockSpec((1, width), lambda i: (0, 0))
    return _pc(
        body, name="loss_head", grid=(n_rows // tr,), in_specs=[blk, blk, one],
        out_specs=[pl.BlockSpec((1, 128), lambda i: (0, 0)), blk, one],
        out_shape=[jax.ShapeDtypeStruct((1, 128), F32), jax.ShapeDtypeStruct(x.shape, F32), jax.ShapeDtypeStruct((1, width), F32)],
        sem=("arbitrary",),
    )(x, target, g)


_NT = (((1,), (1,)), ((), ()))
_TN = (((0,), (0,)), ((), ()))


def _tile(n, cap):
    t = min(n, cap)
    while n % t:
        t -= 128
    assert t > 0
    return t


def _mm(name, a, b, *, ta=False, tb=False, out_dtype=F32, a_off=0, a_width=None, res=None):
    assert not (ta and tb)
    if ta:
        kc = a.shape[0]
        m = a.shape[1] - a_off if a_width is None else a_width
        n = b.shape[1]
        assert b.shape[0] == kc
    else:
        m = a.shape[0]
        kc = a.shape[1] - a_off if a_width is None else a_width
        n = b.shape[0] if tb else b.shape[1]
        assert (b.shape[1] if tb else b.shape[0]) == kc
    tm, tn, tk = _tile(m, 512), _tile(n, 1024), _tile(kc, 1024)
    nk = kc // tk
    if ta:
        assert a_off % tm == 0
        a_spec = pl.BlockSpec((tk, tm), lambda i, j, k, o=a_off // tm: (k, i + o))
        dims = _TN
    else:
        assert a_off % tk == 0
        a_spec = pl.BlockSpec((tm, tk), lambda i, j, k, o=a_off // tk: (i, k + o))
        dims = _NT if tb else (((1,), (0,)), ((), ()))
    b_spec = pl.BlockSpec((tn, tk), lambda i, j, k: (j, k)) if tb else pl.BlockSpec((tk, tn), lambda i, j, k: (k, j))
    in_specs, args = [a_spec, b_spec], [a, b]
    has_res = res is not None
    if has_res:
        in_specs.append(pl.BlockSpec((tm, tn), lambda i, j, k: (i, j)))
        args.append(res)

    def body(*refs):
        a_ref, b_ref = refs[0], refs[1]
        o_ref, acc_ref = refs[-2], refs[-1]
        k = pl.program_id(2)

        @pl.when(k == 0)
        def _():
            acc_ref[...] = jnp.zeros_like(acc_ref)

        acc_ref[...] += lax.dot_general(a_ref[...].astype(BF16), b_ref[...].astype(BF16), dims, preferred_element_type=F32)

        @pl.when(k == nk - 1)
        def _():
            acc = acc_ref[...]
            if has_res:
                acc = acc + refs[2][...].astype(F32)
            o_ref[...] = acc.astype(o_ref.dtype)

    return _pc(
        body, name=name, grid=(m // tm, n // tn, nk), in_specs=in_specs, out_specs=pl.BlockSpec((tm, tn), lambda i, j, k: (i, j)),
        out_shape=jax.ShapeDtypeStruct((m, n), out_dtype), scratch_shapes=[pltpu.VMEM((tm, tn), F32)], sem=("parallel", "parallel", "arbitrary"),
    )(*args)


def _head_masks(width, nsub):
    lane = lax.broadcasted_iota(jnp.int32, (1, width), 1)
    hd = width // nsub
    return [(lane >= h * hd) & (lane < (h + 1) * hd) for h in range(nsub)]


def _dilated_log_count(row0, tq, ext):
    delta = (row0 + lax.broadcasted_iota(jnp.int32, (tq, ext), 0)) - lax.broadcasted_iota(jnp.int32, (tq, ext), 1)
    cnt = (delta <= 128).astype(jnp.int32) + (((delta & 3) == 0) & (delta <= 512)).astype(jnp.int32) + ((delta & 15) == 0).astype(jnp.int32)
    logc = jnp.where(cnt == 3, math.log(3.0), jnp.where(cnt == 2, math.log(2.0), 0.0))
    return jnp.where((delta >= 0) & (cnt > 0), logc, NEG)


def _softmax_rows(s):
    m = jnp.max(s, axis=-1, keepdims=True)
    p = jnp.exp(s - m)
    return p / jnp.sum(p, axis=-1, keepdims=True)


def _attn_fwd(name, qa, ka, va, *, qc, kc, vc, width, nblk, nsub, causal, tq, scale, out_dtype):
    sq, t_len = qa.shape[0], ka.shape[0]

    def body(q_ref, k_ref, v_ref, o_ref):
        kb = k_ref[...].astype(BF16)
        vb = v_ref[...].astype(BF16)
        masks = _head_masks(width, nsub)
        for r in range(sq // tq):
            ext = (r + 1) * tq if causal else t_len
            q = q_ref[r * tq:(r + 1) * tq, :].astype(F32)
            ke, ve = kb[:ext], vb[:ext]
            bias = _dilated_log_count(r * tq, tq, ext) if causal else None
            o = None
            for h in range(nsub):
                qm = (jnp.where(masks[h], q, 0.0) if nsub > 1 else q).astype(BF16)
                s = lax.dot_general(qm, ke, _NT, preferred_element_type=F32) * scale
                if causal:
                    s = jnp.where(bias > 0.5 * NEG, s + bias, NEG)
                pn = _softmax_rows(s).astype(BF16)
                oh = jnp.dot(pn, ve, preferred_element_type=F32)
                o = oh if o is None else jnp.where(masks[h], oh, o)
            o_ref[r * tq:(r + 1) * tq, :] = o.astype(o_ref.dtype)

    return _pc(
        body, name=name, grid=(nblk,),
        in_specs=[pl.BlockSpec((sq, width), lambda i, c=qc: (0, c + i)), pl.BlockSpec((t_len, width), lambda i, c=kc: (0, c + i)),
                  pl.BlockSpec((t_len, width), lambda i, c=vc: (0, c + i))],
        out_specs=pl.BlockSpec((sq, width), lambda i: (0, i)),
        out_shape=jax.ShapeDtypeStruct((sq, nblk * width), out_dtype), sem=("parallel",),
    )(qa, ka, va)


def _attn_bwd(name, qa, ka, va, doa, *, qc, kc, vc, width, nblk, nsub, causal, tq, scale, out_dtype):
    sq, t_len = qa.shape[0], ka.shape[0]

    def body(q_ref, k_ref, v_ref, do_ref, dq_ref, dk_ref, dv_ref, dk_acc, dv_acc):
        kb = k_ref[...].astype(BF16)
        vb = v_ref[...].astype(BF16)
        masks = _head_masks(width, nsub)
        dk_acc[...] = jnp.zeros_like(dk_acc)
        dv_acc[...] = jnp.zeros_like(dv_acc)
        for r in range(sq // tq):
            ext = (r + 1) * tq if causal else t_len
            q = q_ref[r * tq:(r + 1) * tq, :].astype(F32)
            do = do_ref[r * tq:(r + 1) * tq, :].astype(F32)
            ke, ve = kb[:ext], vb[:ext]
            bias = _dilated_log_count(r * tq, tq, ext) if causal else None
            dq = None
            for h in range(nsub):
                qm = (jnp.where(masks[h], q, 0.0) if nsub > 1 else q).astype(BF16)
                dom = (jnp.where(masks[h], do, 0.0) if nsub > 1 else do).astype(BF16)
                s = lax.dot_general(qm, ke, _NT, preferred_element_type=F32) * scale
                if causal:
                    s = jnp.where(bias > 0.5 * NEG, s + bias, NEG)
                pn = _softmax_rows(s)
                dpn = lax.dot_general(dom, ve, _NT, preferred_element_type=F32)
                ds = pn * (dpn - jnp.sum(pn * dpn, axis=-1, keepdims=True))
                dsb = (ds * scale).astype(BF16)
                dqh = jnp.dot(dsb, ke, preferred_element_type=F32)
                dq = dqh if dq is None else jnp.where(masks[h], dqh, dq)
                dk_acc[0:ext, :] += lax.dot_general(dsb, qm, _TN, preferred_element_type=F32)
                dv_acc[0:ext, :] += lax.dot_general(pn.astype(BF16), dom, _TN, preferred_element_type=F32)
            dq_ref[r * tq:(r + 1) * tq, :] = dq.astype(dq_ref.dtype)
        dk_ref[...] = dk_acc[...].astype(dk_ref.dtype)
        dv_ref[...] = dv_acc[...].astype(dv_ref.dtype)

    return _pc(
        body, name=name, grid=(nblk,),
        in_specs=[pl.BlockSpec((sq, width), lambda i, c=qc: (0, c + i)), pl.BlockSpec((t_len, width), lambda i, c=kc: (0, c + i)),
                  pl.BlockSpec((t_len, width), lambda i, c=vc: (0, c + i)), pl.BlockSpec((sq, width), lambda i: (0, i))],
        out_specs=[pl.BlockSpec((sq, width), lambda i: (0, i)), pl.BlockSpec((t_len, width), lambda i: (0, i)), pl.BlockSpec((t_len, width), lambda i: (0, i))],
        out_shape=[jax.ShapeDtypeStruct((sq, nblk * width), out_dtype), jax.ShapeDtypeStruct((t_len, nblk * width), out_dtype),
                   jax.ShapeDtypeStruct((t_len, nblk * width), out_dtype)],
        scratch_shapes=[pltpu.VMEM((t_len, width), F32), pltpu.VMEM((t_len, width), F32)], sem=("parallel",),
    )(qa, ka, va, doa)


_SELF = dict(qc=0, kc=8, vc=16, width=128, nblk=8, nsub=2, causal=True, tq=256, scale=A_HEAD_DIM ** -0.5)
_CROSS = dict(qc=0, kc=0, vc=4, width=256, nblk=4, nsub=1, causal=False, tq=512, scale=X_HEAD_DIM ** -0.5)


def _window_sum(x, g, row, backward):
    n = x.shape[0]

    def shift(y, k):
        if backward:
            return jnp.where(row < n - k, pltpu.roll(y, n - k, 0), 0.0)
        return jnp.where(row >= k, pltpu.roll(y, k, 0), 0.0)

    s2 = x + shift(x, 1)
    s4 = s2 + shift(s2, 2)
    s8 = s4 + shift(s4, 4)
    s16 = s8 + shift(s8, 8)
    return jnp.where(g == 0, s2, jnp.where(g == 1, s4, jnp.where(g == 2, s8, s16)))


def _pool(name, arr, c0, backward, out_dtype):
    n = arr.shape[0]
    gw = 256

    def body(v_ref, o_ref):
        g = pl.program_id(0)
        v = v_ref[...].astype(F32)
        row = lax.broadcasted_iota(jnp.int32, v.shape, 0)
        w = jnp.where(g == 0, 2, jnp.where(g == 1, 4, jnp.where(g == 2, 8, 16)))
        cnt = jnp.minimum(row + 1, w).astype(F32)
        if backward:
            o_ref[...] = (_window_sum(v / cnt, g, row, True) - v).astype(o_ref.dtype)
        else:
            o_ref[...] = (_window_sum(v, g, row, False) / cnt - v).astype(o_ref.dtype)

    return _pc(
        body, name=name, grid=(4,), in_specs=[pl.BlockSpec((n, gw), lambda i, c=c0 // gw: (0, c + i))],
        out_specs=pl.BlockSpec((n, gw), lambda i: (0, i)), out_shape=jax.ShapeDtypeStruct((n, 4 * gw), out_dtype), sem=("parallel",),
    )(arr)


_SCAN_ROWS = 256


def _scan_fwd(bu3, a2):
    n = bu3.shape[0]

    def body(bu_ref, a_ref, h_ref, carry):
        @pl.when(pl.program_id(0) == 0)
        def _():
            carry[...] = jnp.zeros_like(carry)

        ar, ai = a_ref[0:16, :], a_ref[16:32, :]

        def step(t, c):
            hr, hi = c
            nr = ar * hr - ai * hi + bu_ref[t, 0:16, :]
            ni = ar * hi + ai * hr + bu_ref[t, 16:32, :]
            h_ref[t, 0:16, :] = nr
            h_ref[t, 16:32, :] = ni
            return nr, ni

        hr, hi = lax.fori_loop(0, _SCAN_ROWS, step, (carry[0:16, :], carry[16:32, :]), unroll=8)
        carry[0:16, :] = hr
        carry[16:32, :] = hi

    blk = pl.BlockSpec((_SCAN_ROWS, 32, 128), lambda i: (i, 0, 0))
    return _pc(
        body, name="s5_scan_fwd", grid=(n // _SCAN_ROWS,), in_specs=[blk, pl.BlockSpec((32, 128), lambda i: (0, 0))], out_specs=blk,
        out_shape=jax.ShapeDtypeStruct(bu3.shape, F32), scratch_shapes=[pltpu.VMEM((32, 128), F32)], sem=("arbitrary",),
    )(bu3, a2)


def _scan_bwd(dh3, h3, a2):
    n = dh3.shape[0]
    nb = n // _SCAN_ROWS

    def body(dh_ref, h_ref, a_ref, dbu_ref, da_ref, carry):
        @pl.when(pl.program_id(0) == 0)
        def _():
            carry[...] = jnp.zeros_like(carry)
            da_ref[...] = jnp.zeros_like(da_ref)

        ar, ai = a_ref[0:16, :], a_ref[16:32, :]

        def step(tt, c):
            gr, gi, dar, dai = c
            t = _SCAN_ROWS - 1 - tt
            hr, hi = h_ref[t, 0:16, :], h_ref[t, 16:32, :]
            dar = dar + gr * hr + gi * hi
            dai = dai - gr * hi + gi * hr
            ngr = dh_ref[t, 0:16, :] + ar * gr + ai * gi
            ngi = dh_ref[t, 16:32, :] - ai * gr + ar * gi
            dbu_ref[t, 0:16, :] = ngr
            dbu_ref[t, 16:32, :] = ngi
            return ngr, ngi, dar, dai

        z = jnp.zeros((16, 128), F32)
        gr, gi, dar, dai = lax.fori_loop(0, _SCAN_ROWS, step, (carry[0:16, :], carry[16:32, :], z, z), unroll=8)
        carry[0:16, :] = gr
        carry[16:32, :] = gi
        da_ref[0:16, :] += dar
        da_ref[16:32, :] += dai

    blk = pl.BlockSpec((_SCAN_ROWS, 32, 128), lambda i: (nb - 1 - i, 0, 0))
    small = pl.BlockSpec((32, 128), lambda i: (0, 0))
    return _pc(
        body, name="s5_scan_bwd", grid=(nb,), in_specs=[blk, blk, small], out_specs=[blk, small],
        out_shape=[jax.ShapeDtypeStruct(dh3.shape, F32), jax.ShapeDtypeStruct((32, 128), F32)],
        scratch_shapes=[pltpu.VMEM((32, 128), F32)], sem=("arbitrary",),
    )(dh3, h3, a2)


def _bdense(bb_re, bb_im):
    eye = jnp.eye(S5_GROUPS, dtype=F32)

    def one(bb):
        return jnp.einsum("gph,gk->ghkp", bb.reshape(S5_GROUPS, S5_STATE, S5_GROUP_DIM), eye).reshape(512, 2048)

    return jnp.concatenate([one(bb_re), one(bb_im)], axis=1)


def _cdense(c_re, c_im):
    eye = jnp.eye(S5_GROUPS, dtype=F32)

    def one(cc):
        return jnp.einsum("ghp,gk->gpkh", cc, eye).reshape(2048, 512)

    return jnp.concatenate([one(c_re), -one(c_im)], axis=0)


def _pool_dense(pw):
    eye = jnp.eye(4, dtype=pw.dtype)
    return jnp.einsum("gcd,gk->gckd", pw, eye).reshape(1024, 1024)


def _row2(v):
    return v.reshape(1, -1)


def _even_fwd(x, W, i):
    hn = _rw_fwd("rms_fwd", _f_rms, [_cols(x)], [_par(_row2(W["norm_ab"][i]))], [(D_MODEL, BF16, 1)], 256)[0]
    z = _mm("mm_in_ab", hn, W["w_in_ab"][i])
    o = _attn_fwd("attn_self_fwd", z, z, z, out_dtype=F32, **_SELF)
    pooled = _pool("pool_fwd", z, 4096, False, BF16)
    wp = _pool_dense(W["pool_w"][i])
    mixed = _mm("mm_pool", pooled, wp)
    scale = _row2(W["pool_scale"][i])
    ab = _rw_fwd("gate_ab_fwd", _f_gate_ab, [_cols(o), _cols(z, 3072, 1024), _cols(mixed), _cols(z, 5120, 1024)], [_par(scale)],
                 [(2048, BF16, 2)], 256)[0]
    x1 = _mm("mm_out_ab", ab, W["w_out_ab"][i], res=x)
    return x1, dict(x=x, hn=hn, z=z, o=o, pooled=pooled, wp=wp, mixed=mixed, ab=ab)


def _even_bwd(dx1, sv, W, G, i, before_last):
    x, hn, z = sv["x"], sv["hn"], sv["z"]
    dab = _mm("mm_out_ab_dx", dx1, W["w_out_ab"][i], tb=True)
    G["w_out_ab"][i] = _mm("mm_out_ab_dw", sv["ab"], dx1, ta=True, out_dtype=BF16)
    scale = _row2(W["pool_scale"][i])
    (do, dga, dmixed, dgb), (dscale,) = _rw_bwd(
        "gate_ab_bwd", _f_gate_ab, [_cols(sv["o"]), _cols(z, 3072, 1024), _cols(sv["mixed"]), _cols(z, 5120, 1024)], [_par(scale)],
        [_cols(dab, nsplit=2)], [(0, F32), (1, BF16), (2, BF16), (3, BF16)], [0], 256)
    G["pool_scale"][i] = dscale.reshape(-1)
    dpooled = _mm("mm_pool_dx", dmixed, sv["wp"], tb=True)
    dwp = _mm("mm_pool_dw", sv["pooled"], dmixed, ta=True, out_dtype=BF16)
    G["pool_w"][i] = jnp.stack([dwp[g * 256:(g + 1) * 256, g * 256:(g + 1) * 256] for g in range(4)])
    dvb = _pool("pool_bwd", dpooled, 0, True, BF16)
    dq, dk, dv = _attn_bwd("attn_self_bwd", z, z, z, do, out_dtype=BF16, **_SELF)
    dz = jnp.concatenate([dq, dk, dv, dga, dvb, dgb], axis=1)
    dhn = _mm("mm_in_ab_dx", dz, W["w_in_ab"][i], tb=True)
    G["w_in_ab"][i] = _mm("mm_in_ab_dw", hn, dz, ta=True, out_dtype=BF16)
    g = _row2(W["norm_ab"][i]) + before_last()
    (dx,), (dg,) = _rw_bwd("rms_bwd", _f_rms_res, [_cols(x)], [_par(g)], [_cols(dx1), _cols(dhn)], [(0, F32)], [0], 256)
    G["norm_ab"][i] = dg.reshape(-1)
    return dx


def _odd_fwd(x, W, i):
    hn = _rw_fwd("rms_fwd", _f_rms, [_cols(x)], [_par(_row2(W["norm_cd"][i]))], [(D_MODEL, BF16, 1)], 256)[0]
    z = _mm("mm_in_cd", hn, W["w_in_cd"][i])
    sgu_p = [_par(_row2(W["sgu_ln_g"][i]), 4), _par(_row2(W["sgu_ln_b"][i]), 4), _par(W["sgu_w"][i], 4), _par(W["sgu_b"][i][..., None], 4)]
    c_out = _rw_fwd("sgu_fwd", _f_sgu, [_cols(z, 0, 1024, 4), _cols(z, 1024, 1024, 4), _cols(z, 2048, 1024, 4)], sgu_p, [(1024, BF16, 4)], 128)[0]
    prep_rows = [_cols(W["s5_a_re"][i]), _cols(W["s5_a_im"][i]), _cols(W["s5_log_dt"][i].reshape(S5_GROUPS, 1))]
    abar_re, abar_im, coef_re, coef_im = _rw_fwd("s5_prep_fwd", _f_s5_prep, prep_rows, [], [(S5_STATE, F32, 1)] * 4, S5_GROUPS)
    bb_rows = [_cols(W["s5_b_re"][i].reshape(2048, 16)), _cols(W["s5_b_im"][i].reshape(2048, 16)), _cols(coef_re.reshape(2048, 1)), _cols(coef_im.reshape(2048, 1))]
    bb_re, bb_im = _rw_fwd("s5_bbar_fwd", _f_bbar, bb_rows, [], [(16, F32, 1)] * 2, 256)
    bd = _bdense(bb_re, bb_im).astype(BF16)
    cf = _cdense(W["s5_c_re"][i], W["s5_c_im"][i]).astype(BF16)
    a2 = jnp.concatenate([abar_re.reshape(16, 128), abar_im.reshape(16, 128)], axis=0)
    bu = _mm("mm_s5_bu", z, bd, a_off=3072, a_width=512)
    h3 = _scan_fwd(bu.reshape(SEQ, 32, 128), a2)
    h2 = h3.reshape(SEQ, 4096)
    yc = _mm("mm_s5_y", h2, cf)
    dpar = _row2(W["s5_d"][i])
    yg = _rw_fwd("gelu_fwd", _f_gelu_y, [_cols(yc), _cols(z, 3072, 512)], [_par(dpar)], [(512, BF16, 1)], 256)[0]
    w12 = jnp.concatenate([W["glu_w1"][i], W["glu_w2"][i]], axis=1)
    t12 = _mm("mm_glu", yg, w12)
    d_out = _rw_fwd("glu_gate_fwd", _f_glu_gate, [_cols(t12, nsplit=2), _cols(z, 3584, 512)], [], [(512, BF16, 1)], 256)[0]
    cd = jnp.concatenate([c_out, d_out], axis=1)
    x1 = _mm("mm_out_cd", cd, W["w_out_cd"][i], res=x)
    sv = dict(x=x, hn=hn, z=z, sgu_p=sgu_p, prep_rows=prep_rows, bb_rows=bb_rows, bb=(bb_re, bb_im), bd=bd, cf=cf, a2=a2,
              h3=h3, h2=h2, yc=yc, dpar=dpar, yg=yg, w12=w12, t12=t12, cd=cd)
    return x1, sv


def _odd_bwd(dx1, sv, W, G, i, before_last):
    x, hn, z = sv["x"], sv["hn"], sv["z"]
    dcd = _mm("mm_out_cd_dx", dx1, W["w_out_cd"][i], tb=True)
    G["w_out_cd"][i] = _mm("mm_out_cd_dw", sv["cd"], dx1, ta=True, out_dtype=BF16)
    (du, dv, dgc), (dlg, dlb, dsw, dsb) = _rw_bwd(
        "sgu_bwd", _f_sgu, [_cols(z, 0, 1024, 4), _cols(z, 1024, 1024, 4), _cols(z, 2048, 1024, 4)], sv["sgu_p"],
        [_cols(dcd, 0, 1024, 4)], [(0, BF16), (1, BF16), (2, BF16)], [0, 1, 2, 3], 128)
    G["sgu_ln_g"][i], G["sgu_ln_b"][i] = dlg.reshape(-1), dlb.reshape(-1)
    G["sgu_w"][i], G["sgu_b"][i] = dsw, dsb[..., 0]
    (dt12, dgd), _ = _rw_bwd("glu_gate_bwd", _f_glu_gate, [_cols(sv["t12"], nsplit=2), _cols(z, 3584, 512)], [], [_cols(dcd, 1024, 512)],
                             [(0, BF16), (1, BF16)], [], 256)
    dyg = _mm("mm_glu_dx", dt12, sv["w12"], tb=True)
    dw12 = _mm("mm_glu_dw", sv["yg"], dt12, ta=True, out_dtype=BF16)
    G["glu_w1"][i], G["glu_w2"][i] = dw12[:, :512], dw12[:, 512:]
    (dyc, duf1), (dd,) = _rw_bwd("gelu_bwd", _f_gelu_y, [_cols(sv["yc"]), _cols(z, 3072, 512)], [_par(sv["dpar"])], [_cols(dyg)],
                                 [(0, BF16), (1, F32)], [0], 256)
    G["s5_d"][i] = dd.reshape(-1)
    dh2 = _mm("mm_s5_y_dx", dyc, sv["cf"], tb=True)
    dcf = _mm("mm_s5_y_dw", sv["h2"], dyc, ta=True)
    _, cvjp = jax.vjp(_cdense, W["s5_c_re"][i], W["s5_c_im"][i])
    G["s5_c_re"][i], G["s5_c_im"][i] = cvjp(dcf)
    dbu3, da2 = _scan_bwd(dh2.reshape(SEQ, 32, 128), sv["h3"], sv["a2"])
    dbu = dbu3.reshape(SEQ, 4096)
    duf2 = _mm("mm_s5_bu_dx", dbu, sv["bd"], tb=True)
    dbd = _mm("mm_s5_bu_dw", z, dbu, ta=True, a_off=3072, a_width=512)
    _, bvjp = jax.vjp(_bdense, *sv["bb"])
    dbb_re, dbb_im = bvjp(dbd)
    (dbr, dbi, dcr, dci), _ = _rw_bwd("s5_bbar_bwd", _f_bbar, sv["bb_rows"], [], [_cols(dbb_re), _cols(dbb_im)],
                                      [(0, F32), (1, F32), (2, F32), (3, F32)], [], 256)
    G["s5_b_re"][i], G["s5_b_im"][i] = dbr.reshape(S5_GROUPS, S5_STATE, S5_GROUP_DIM), dbi.reshape(S5_GROUPS, S5_STATE, S5_GROUP_DIM)
    douts = [_cols(da2[0:16].reshape(S5_GROUPS, S5_STATE)), _cols(da2[16:32].reshape(S5_GROUPS, S5_STATE)),
             _cols(dcr.reshape(S5_GROUPS, S5_STATE)), _cols(dci.reshape(S5_GROUPS, S5_STATE))]
    (dar, dai, dldt), _ = _rw_bwd("s5_prep_bwd", _f_s5_prep, sv["prep_rows"], [], douts, [(0, F32), (1, F32), (2, F32)], [], S5_GROUPS)
    G["s5_a_re"][i], G["s5_a_im"][i], G["s5_log_dt"][i] = dar, dai, dldt.reshape(-1)
    dxd = (duf1 + duf2).astype(BF16)
    dz = jnp.concatenate([du, dv, dgc, dxd, dgd], axis=1)
    dhn = _mm("mm_in_cd_dx", dz, W["w_in_cd"][i], tb=True)
    G["w_in_cd"][i] = _mm("mm_in_cd_dw", hn, dz, ta=True, out_dtype=BF16)
    g = _row2(W["norm_cd"][i]) + before_last()
    (dx,), (dg,) = _rw_bwd("rms_bwd", _f_rms_res, [_cols(x)], [_par(g)], [_cols(dx1), _cols(dhn)], [(0, F32)], [0], 256)
    G["norm_cd"][i] = dg.reshape(-1)
    return dx


def _cross_fwd(x1, mem_n, W, l):
    hx = _rw_fwd("rms_fwd", _f_rms, [_cols(x1)], [_par(_row2(W["norm_x"][l]))], [(D_MODEL, BF16, 1)], 256)[0]
    qx = _mm("mm_xq", hx, W["w_xq"][l], out_dtype=BF16)
    kv = _mm("mm_xkv", mem_n, W["w_xkv"][l], out_dtype=BF16)
    ox = _attn_fwd("attn_cross_fwd", qx, kv, kv, out_dtype=BF16, **_CROSS)
    x2 = _mm("mm_xo", ox, W["w_xo"][l], res=x1)
    return x2, dict(x1=x1, hx=hx, qx=qx, kv=kv, ox=ox)


def _cross_bwd(dx2, dmem_n, sv, mem_n, W, G, l):
    dox = _mm("mm_xo_dx", dx2, W["w_xo"][l], tb=True, out_dtype=BF16)
    G["w_xo"][l] = _mm("mm_xo_dw", sv["ox"], dx2, ta=True, out_dtype=BF16)
    dqx, dk, dv = _attn_bwd("attn_cross_bwd", sv["qx"], sv["kv"], sv["kv"], dox, out_dtype=BF16, **_CROSS)
    dkv = jnp.concatenate([dk, dv], axis=1)
    dhx = _mm("mm_xq_dx", dqx, W["w_xq"][l], tb=True)
    G["w_xq"][l] = _mm("mm_xq_dw", sv["hx"], dqx, ta=True, out_dtype=BF16)
    dmem_n = _mm("mm_xkv_dx", dkv, W["w_xkv"][l], tb=True, res=dmem_n)
    G["w_xkv"][l] = _mm("mm_xkv_dw", mem_n, dkv, ta=True, out_dtype=BF16)
    (dx1,), (dg,) = _rw_bwd("rms_bwd", _f_rms_res, [_cols(sv["x1"])], [_par(_row2(W["norm_x"][l]))], [_cols(dx2), _cols(dhx)], [(0, F32)], [0], 256)
    G["norm_x"][l] = dg.reshape(-1)
    return dx1, dmem_n


_PER_LAYER = ("pool_scale", "norm_ab", "norm_cd", "sgu_ln_g", "sgu_ln_b", "sgu_w", "sgu_b", "s5_d", "s5_c_re", "s5_c_im", "s5_b_re", "s5_b_im",
              "s5_a_re", "s5_a_im", "s5_log_dt", "w_in_ab", "pool_w", "w_out_ab", "w_in_cd", "glu_w1", "glu_w2", "w_out_cd")


def _local_step(x, mem, target, W, weights_of, before_last):
    G = {k: [None, None] for k in _PER_LAYER}
    for k in ("norm_x", "w_xq", "w_xkv", "w_xo"):
        G[k] = [None] * DEPTH
    mem_rows = [_cols(mem)]
    mem_par = [_par(_row2(W["mem_norm"]))]
    mem_n = _rw_fwd("rms_fwd_mem", _f_rms, mem_rows, mem_par, [(D_MODEL, BF16, 1)], 256)[0]
    saved = []
    for layer in range(DEPTH):
        weights_of(layer, x if layer else mem_n)
        x, sv = _even_fwd(x, W, layer // 2) if layer % 2 == 0 else _odd_fwd(x, W, layer // 2)
        x, svx = _cross_fwd(x, mem_n, W, layer)
        saved.append((sv, svx))
    loss, dx, dfinal = _loss_head(x, target, _row2(W["final_norm"]))
    G["final_norm"] = dfinal.reshape(-1)
    dmem_n = None
    for layer in reversed(range(DEPTH)):
        sv, svx = saved[layer]
        dx, dmem_n = _cross_bwd(dx, dmem_n, svx, mem_n, W, G, layer)
        hook = functools.partial(before_last, layer, G)
        dx = _even_bwd(dx, sv, W, G, layer // 2, hook) if layer % 2 == 0 else _odd_bwd(dx, sv, W, G, layer // 2, hook)
    _, (dmn,) = _rw_bwd("rms_bwd_mem", _f_rms, mem_rows, mem_par, [_cols(dmem_n)], [], [0], 256)
    G["mem_norm"] = dmn.reshape(-1)
    return loss, dx, G


_HBM = pl.BlockSpec(memory_space=pltpu.HBM)
_ANY = pl.BlockSpec(memory_space=pl.ANY)
_SEM = pl.BlockSpec(memory_space=pltpu.SEMAPHORE)
_N_PEERS = N_DEV - 1


def _mesh_pos():
    return lax.axis_index("x"), lax.axis_index("y"), lax.axis_index("c")


def _peer(pos, k):
    x, y, c = pos
    return (x ^ ((k >> 2) & 1), y ^ ((k >> 1) & 1), c ^ (k & 1))


def _lin(pos):
    return 4 * pos[0] + 2 * pos[1] + pos[2]


def _part(ref, spec, idx):
    if spec is None:
        return ref
    ax, n = spec
    return ref.at[(slice(None),) * ax + (pl.ds(pl.multiple_of(idx * n, n), n),)]


def _ends(gather, ins, lands, specs, t, sender, receiver):
    if gather:
        return ins[t], _part(lands[t], specs[t], sender)
    return _part(ins[t], specs[t], receiver), lands[t].at[sender]


def _land_shape(gather, arr, spec):
    if gather:
        ax, n = spec
        return arr.shape[:ax] + (n * N_DEV,) + arr.shape[ax + 1:]
    if spec is None:
        return (N_DEV,) + arr.shape
    ax, n = spec
    return (N_DEV,) + arr.shape[:ax] + (n,) + arr.shape[ax + 1:]


def _exchange_place(name, gather, arrs, specs):
    nt = len(arrs)

    def body(*refs):
        ins, lands, sems = refs[:nt], refs[nt:2 * nt], refs[2 * nt]
        me = _lin(_mesh_pos())
        copies = [pltpu.make_async_copy(*_ends(gather, ins, lands, specs, t, me, me), sems.at[t]) for t in range(nt)]
        for cp in copies:
            cp.start()
        for cp in copies:
            cp.wait()

    return _pc(
        body, name=name, in_specs=[_ANY] * nt, out_specs=[_ANY] * nt,
        out_shape=[jax.ShapeDtypeStruct(_land_shape(gather, a, s), a.dtype) for a, s in zip(arrs, specs)],
        scratch_shapes=[pltpu.SemaphoreType.DMA((nt,))],
    )(*arrs)


def _exchange_start(name, gather, arrs, lands, specs):
    nt = len(arrs)

    def body(*refs):
        ins, lnd = refs[:nt], refs[nt:2 * nt]
        send_sems, recv_sems = refs[2 * nt], refs[2 * nt + 1]
        token = refs[-1]
        pos = _mesh_pos()
        me = _lin(pos)
        for k in range(1, N_DEV):
            peer = _peer(pos, k)
            for t in range(nt):
                src, dst = _ends(gather, ins, lnd, specs, t, me, _lin(peer))
                pltpu.make_async_remote_copy(
                    src_ref=src, dst_ref=dst, send_sem=send_sems.at[t * _N_PEERS + k - 1], recv_sem=recv_sems.at[t * _N_PEERS + k - 1],
                    device_id=peer, device_id_type=pl.DeviceIdType.MESH).start()
        token[...] = jnp.zeros_like(token)

    hbm = lambda a: pltpu.HBM(a.shape, a.dtype)
    out = pl.pallas_call(
        body, name=name,
        out_shape=(pltpu.SemaphoreType.DMA((nt * _N_PEERS,)), pltpu.SemaphoreType.DMA((nt * _N_PEERS,)), *[hbm(a) for a in arrs], *[hbm(a) for a in lands],
                   jax.ShapeDtypeStruct((8, 128), F32)),
        in_specs=[_HBM] * (2 * nt), out_specs=(_SEM, _SEM, *[_HBM] * (2 * nt), pl.BlockSpec(memory_space=pltpu.VMEM)),
        input_output_aliases={j: 2 + j for j in range(2 * nt)},
        compiler_params=pltpu.CompilerParams(has_side_effects=pltpu.SideEffectType.DATAFLOW_SIDE_EFFECTING),
        interpret=False,
    )(*[pltpu.with_memory_space_constraint(a, pltpu.HBM) for a in arrs], *[pltpu.with_memory_space_constraint(a, pltpu.HBM) for a in lands])
    return dict(send=out[0], recv=out[1], arrs=list(out[2:2 + nt]), lands=list(out[2 + nt:2 + 2 * nt]), token=out[-1][0, 0], token_arr=out[-1], gather=gather, specs=specs)


def _exchange_wait(name, ex, after):
    nt = len(ex["arrs"])
    gather, specs = ex["gather"], ex["specs"]

    def body(*refs):
        ins, lnd = refs[:nt], refs[nt:2 * nt]
        send_sems, recv_sems = refs[2 * nt], refs[2 * nt + 1]
        pos = _mesh_pos()
        me = _lin(pos)
        for k in range(1, N_DEV):
            peer = _peer(pos, k)
            for t in range(nt):
                src, _ = _ends(gather, ins, lnd, specs, t, me, _lin(peer))
                _, dst = _ends(gather, ins, lnd, specs, t, _lin(peer), me)
                cp = pltpu.make_async_remote_copy(
                    src_ref=src, dst_ref=dst, send_sem=send_sems.at[t * _N_PEERS + k - 1], recv_sem=recv_sems.at[t * _N_PEERS + k - 1],
                    device_id=peer, device_id_type=pl.DeviceIdType.MESH)
                cp.wait_send()
                cp.wait_recv()

    hbm = lambda a: pltpu.HBM(a.shape, a.dtype)
    out = pl.pallas_call(
        body, name=name, out_shape=(*[hbm(a) for a in ex["arrs"]], *[hbm(a) for a in ex["lands"]]),
        in_specs=[_HBM] * (2 * nt) + [_SEM, _SEM, _ANY], out_specs=tuple([_HBM] * (2 * nt)),
        input_output_aliases={j: j for j in range(2 * nt)},
        compiler_params=pltpu.CompilerParams(has_side_effects=pltpu.SideEffectType.DATAFLOW_SIDE_EFFECTING),
        interpret=False,
    )(*ex["arrs"], *ex["lands"], ex["send"], ex["recv"], after)
    return list(out[nt:])


def _exchange_begin(name, gather, arrs, specs):
    lands = _exchange_place(name + "_place", gather, arrs, specs)
    return _exchange_start(name + "_start", gather, arrs, lands, specs)


def _adam(name, w, parts, m, v, layer, bufs):
    r, c = parts.shape[1:]
    tr = _row_block(r, c, 1 << 20)
    nb = r // tr

    def body(w_ref, m_ref, v_ref, p_ref, *rest):
        g_ref, d_ref, nm_ref, nv_ref = rest[-4:]
        g = p_ref[0].astype(F32)
        for k in range(1, N_DEV):
            g = g + p_ref[k].astype(F32)
        mm = ADAM_B1 * m_ref[...] + (1.0 - ADAM_B1) * g
        vv = ADAM_B2 * v_ref[...] + (1.0 - ADAM_B2) * jnp.square(g)
        m_hat = mm / (1.0 - ADAM_B1 ** ADAM_STEP)
        v_hat = vv / (1.0 - ADAM_B2 ** ADAM_STEP)
        g_ref[...] = g
        d_ref[...] = -ADAM_LR * (m_hat / (jnp.sqrt(v_hat) + ADAM_EPS) + ADAM_WD * w_ref[...])
        nm_ref[...] = mm
        nv_ref[...] = vv

    blk = pl.BlockSpec((tr, c), lambda i, o=layer * nb: (o + i, 0))
    in_specs = [blk, blk, blk, pl.BlockSpec((N_DEV, tr, c), lambda i: (0, i, 0))]
    args = [w, m, v, parts]
    aliases = None
    if bufs is not None:
        in_specs += [_ANY] * 4
        args += list(bufs)
        aliases = {4 + j: j for j in range(4)}
    return _pc(
        body, name=name, grid=(nb,), in_specs=in_specs, out_specs=[blk] * 4, out_shape=[jax.ShapeDtypeStruct(w.shape, F32)] * 4,
        aliases=aliases, sem=("parallel",),
    )(*args)


def _pack(arrs):
    out = []
    for a in arrs:
        f = a.reshape(-1)
        out.append(jnp.pad(f, (0, (-f.shape[0]) % 1024)).reshape(-1, 128))
    rows = sum(a.shape[0] for a in out)
    out.append(jnp.zeros(((-rows) % 256, 128), out[0].dtype))
    return jnp.concatenate(out, axis=0)


def _unpack(packed, shapes):
    out, r = [], 0
    for s in shapes:
        n = math.prod(s)
        rows = (n + 1023) // 1024 * 8
        out.append(packed[r:r + rows].reshape(-1)[:n].reshape(s))
        r += rows
    return out


def _row_block(r, c, limit):
    best = None
    for tr in range(16, r + 1, 16):
        if r % tr == 0 and tr * c * 4 <= limit:
            best = tr
    return r if best is None else best


_BIG_AXIS = {"w_in_ab": 1, "pool_w": 1, "w_out_ab": 0, "w_in_cd": 1, "glu_w1": 0, "glu_w2": 0, "w_out_cd": 0, "w_xq": 0, "w_xkv": 1, "w_xo": 0}
_MIXER_BIG = (("w_in_ab", "pool_w", "w_out_ab"), ("w_in_cd", "glu_w1", "glu_w2", "w_out_cd"))
_CROSS_BIG = ("w_xq", "w_xkv", "w_xo")
_SMALL_SPLIT = ["norm_cd", "sgu_ln_g", "sgu_ln_b", "s5_d"]
_REPLICATED = ["norm_ab", "pool_scale", "sgu_w", "sgu_b", "s5_a_re", "s5_a_im", "s5_log_dt", "s5_b_re", "s5_b_im", "s5_c_re", "s5_c_im",
               "norm_x", "mem_norm", "final_norm"]
_WEIGHTS = ["norm_ab", "w_in_ab", "pool_w", "pool_scale", "w_out_ab", "norm_cd", "w_in_cd", "sgu_ln_g", "sgu_ln_b", "sgu_w", "sgu_b", "s5_a_re",
            "s5_a_im", "s5_log_dt", "s5_b_re", "s5_b_im", "s5_c_re", "s5_c_im", "s5_d", "glu_w1", "glu_w2", "w_out_cd", "norm_x", "w_xq",
            "w_xkv", "w_xo", "mem_norm", "final_norm"]


def _layer_big(layer):
    return [(n, layer // 2) for n in _MIXER_BIG[layer % 2]] + [(n, layer) for n in _CROSS_BIG]


def _rows2d(a):
    return a.reshape(-1, a.shape[-1])


def _small_rows(block):
    return jnp.pad(block, ((0, 0), (0, 128 - block.shape[1])))


def kernel(x, mem, norm_ab, w_in_ab, pool_w, pool_scale, w_out_ab, norm_cd, w_in_cd, sgu_ln_g, sgu_ln_b, sgu_w, sgu_b, s5_a_re, s5_a_im, s5_log_dt, s5_b_re, s5_b_im, s5_c_re, s5_c_im, s5_d, glu_w1, glu_w2, w_out_cd, norm_x, w_xq, w_xkv, w_xo, mem_norm, final_norm, loss_target, m_norm_ab, m_w_in_ab, m_pool_w, m_pool_scale, m_w_out_ab, m_norm_cd, m_w_in_cd, m_sgu_ln_g, m_sgu_ln_b, m_sgu_w, m_sgu_b, m_s5_a_re, m_s5_a_im, m_s5_log_dt, m_s5_b_re, m_s5_b_im, m_s5_c_re, m_s5_c_im, m_s5_d, m_glu_w1, m_glu_w2, m_w_out_cd, m_norm_x, m_w_xq, m_w_xkv, m_w_xo, m_mem_norm, m_final_norm, v_norm_ab, v_w_in_ab, v_pool_w, v_pool_scale, v_w_out_ab, v_norm_cd, v_w_in_cd, v_sgu_ln_g, v_sgu_ln_b, v_sgu_w, v_sgu_b, v_s5_a_re, v_s5_a_im, v_s5_log_dt, v_s5_b_re, v_s5_b_im, v_s5_c_re, v_s5_c_im, v_s5_d, v_glu_w1, v_glu_w2, v_w_out_cd, v_norm_x, v_w_xq, v_w_xkv, v_w_xo, v_mem_norm, v_final_norm):
    args = locals()
    w = {n: args[n] for n in _WEIGHTS}
    m = {n: args["m_" + n] for n in _WEIGHTS}
    v = {n: args["v_" + n] for n in _WEIGHTS}

    gathers, zero = [], 0.0
    for layer in range(DEPTH):
        blocks, specs = [], []
        for name, i in _layer_big(layer):
            shp = w[name].shape[1:]
            b2 = _rows2d(w[name])
            r = b2.shape[0] // w[name].shape[0]
            cast = _rw_fwd("cast_" + name, _f_cast, [_cols(b2, r0=i * r)], [], [(b2.shape[1], BF16, 1)], _row_block(r, b2.shape[1], 2 << 20), n_rows=r)[0]
            blocks.append(cast.reshape(shp))
            specs.append((_BIG_AXIS[name], shp[_BIG_AXIS[name]]))
        if layer == 0:
            blocks.append(jnp.concatenate([_small_rows(w[n]) for n in _SMALL_SPLIT], axis=0))
            specs.append((0, 8))
        gathers.append(_exchange_begin("gather%d" % layer, True, blocks, specs))
        zero = zero + gathers[-1]["token"]

    W = {n: w[n] for n in _REPLICATED}
    W["mem_norm"] = w["mem_norm"] + zero
    for name in _BIG_AXIS:
        W[name] = [None] * w[name].shape[0]

    def weights_of(layer, after):
        got = _exchange_wait("gather%d_wait" % layer, gathers[layer], after)
        for (name, i), arr in zip(_layer_big(layer), got):
            W[name][i] = arr
        if layer == 0:
            sm = got[-1].reshape(N_DEV, 4, 2, 128)
            for j, n in enumerate(_SMALL_SPLIT):
                width = w[n].shape[1]
                W[n] = sm[:, j, :, :width].transpose(1, 0, 2).reshape(2, N_DEV * width)

    scatters = [None] * DEPTH

    def before_last(layer, G):
        wholes = [G[name][i] for name, i in _layer_big(layer)]
        specs = [(_BIG_AXIS[name], w[name].shape[1:][_BIG_AXIS[name]]) for name, i in _layer_big(layer)]
        scatters[layer] = _exchange_begin("scatter%d" % layer, False, wholes, specs)
        return scatters[layer]["token"]

    loss, dx, G = _local_step(x[0], mem[0], loss_target[0], W, weights_of, before_last)
    loss = lax.psum(loss[0, 0], MESH_AXES)

    gs = jnp.stack([jnp.stack(G[n]).reshape(2, N_DEV, -1).transpose(1, 0, 2) for n in _SMALL_SPLIT[:3]]
                   + [jnp.pad(jnp.stack(G["s5_d"]).reshape(2, N_DEV, -1).transpose(1, 0, 2), ((0, 0), (0, 0), (0, 64)))], axis=1)
    rep = _pack([G[n] if n in ("mem_norm", "final_norm") else jnp.stack(G[n]) for n in _REPLICATED])
    tail = _exchange_begin("scatter_small", False, [gs.reshape(N_DEV * 8, 128), rep], [(0, 8), None])

    out = {}
    after = tail["token_arr"]
    for layer in reversed(range(DEPTH)):
        got = _exchange_wait("scatter%d_wait" % layer, scatters[layer], after)
        for (name, i), arr in zip(_layer_big(layer), got):
            parts = arr.reshape((N_DEV,) + _rows2d(w[name][i]).shape)
            out[name] = _adam("adam_" + name, _rows2d(w[name]), parts, _rows2d(m[name]), _rows2d(v[name]), i, out.get(name))
        after = out[name][0]
    for name in _BIG_AXIS:
        out[name] = [a.reshape(w[name].shape) for a in out[name]]
    got = _exchange_wait("scatter_small_wait", tail, after)
    pk = lambda d: jnp.concatenate([_small_rows(d[n]) for n in _SMALL_SPLIT], axis=0)
    res = _adam("adam_small", pk(w), got[0], pk(m), pk(v), 0, None)
    for j, n in enumerate(_SMALL_SPLIT):
        out[n] = [a[2 * j:2 * j + 2, :w[n].shape[1]] for a in res]
    pk = lambda d: _pack([d[n] for n in _REPLICATED])
    res = _adam("adam_replicated", pk(w), got[1], pk(m), pk(v), 0, None)
    shapes = [w[n].shape for n in _REPLICATED]
    un = [_unpack(a, shapes) for a in res]
    for j, n in enumerate(_REPLICATED):
        out[n] = [un[q][j] for q in range(4)]

    return (loss, dx[None], *[out[n][0] for n in _WEIGHTS], *[out[n][1] for n in _WEIGHTS], *[out[n][2] for n in _WEIGHTS],
            *[out[n][3] for n in _WEIGHTS])
```

```python
import functools
import math

import jax
import jax.numpy as jnp
from jax import lax
from jax.experimental import pallas as pl
from jax.experimental.pallas import tpu as pltpu

F32 = jnp.float32
BF16 = jnp.bfloat16

SEQ = 2048
D_MODEL = 1024
MEM_LEN = 256
DEPTH = 4
N_DEV = 8
EPS = 1e-6
NEG = -1e30
A_HEAD_DIM = 64
X_HEAD_DIM = 256
S5_GROUPS = 32
S5_STATE = 64
S5_GROUP_DIM = 16

ADAM_LR = 0.001
ADAM_B1 = 0.9
ADAM_B2 = 0.999
ADAM_EPS = 1e-08
ADAM_WD = 0.01
ADAM_STEP = 10

V7X_VMEM_LIMIT_BYTES = 56 * 1024 * 1024
MESH_AXES = ("x", "y", "c")


def _pc(body, *, name, out_shape, grid=None, in_specs=None, out_specs=None, scratch_shapes=(), aliases=None, sem=None):
    kw = {}
    if grid is not None:
        kw["grid"] = grid
    if in_specs is not None:
        kw["in_specs"] = in_specs
    if out_specs is not None:
        kw["out_specs"] = out_specs
    if aliases:
        kw["input_output_aliases"] = aliases
    return pl.pallas_call(
        body,
        name=name,
        out_shape=out_shape,
        scratch_shapes=list(scratch_shapes),
        compiler_params=pltpu.CompilerParams(dimension_semantics=sem, vmem_limit_bytes=V7X_VMEM_LIMIT_BYTES),
        interpret=False,
        **kw,
    )


def _cols(arr, c0=0, width=None, nsplit=1, r0=0):
    width = arr.shape[1] - c0 if width is None else width
    assert c0 % width == 0 and width % nsplit == 0
    return (arr, c0, width, nsplit, r0)


def _par(arr, nsplit=1):
    return (arr, nsplit)


def _ld(ref, nsplit):
    if nsplit == 1:
        return ref[...].astype(F32)
    if len(ref.shape) == 3:
        return tuple(ref[k].astype(F32) for k in range(nsplit))
    w = ref.shape[-1] // nsplit
    return tuple(ref[:, k * w:(k + 1) * w].astype(F32) for k in range(nsplit))


def _st(ref, val, nsplit, accumulate=False):
    if nsplit == 1:
        val = (val,)
    for k in range(nsplit):
        if nsplit == 1:
            idx = (Ellipsis,)
        elif len(ref.shape) == 3:
            idx = (k,)
        else:
            w = ref.shape[-1] // nsplit
            idx = (slice(None), slice(k * w, (k + 1) * w))
        if accumulate:
            ref[idx] += val[k].astype(ref.dtype)
        else:
            ref[idx] = val[k].astype(ref.dtype)


def _row_spec(tr, op):
    _, c0, w, _, r0 = op
    assert r0 % tr == 0
    return pl.BlockSpec((tr, w), lambda i, cb=c0 // w, rb=r0 // tr: (i + rb, cb))


def _full_spec(arr):
    return pl.BlockSpec(arr.shape, lambda i, nd=arr.ndim: (0,) * nd)


def _rw_fwd(name, f, rows, pars, outs, tr, n_rows=None):
    n_rows = rows[0][0].shape[0] if n_rows is None else n_rows
    nr, npar = len(rows), len(pars)

    def body(*refs):
        r = [_ld(refs[i], rows[i][3]) for i in range(nr)]
        p = [_ld(refs[nr + i], pars[i][1]) for i in range(npar)]
        res = f(r, p)
        for k, (_, _, ns) in enumerate(outs):
            _st(refs[nr + npar + k], res[k], ns)

    res = _pc(
        body, name=name, grid=(n_rows // tr,),
        in_specs=[_row_spec(tr, op) for op in rows] + [_full_spec(a) for a, _ in pars],
        out_specs=[pl.BlockSpec((tr, w), lambda i: (i, 0)) for w, _, _ in outs],
        out_shape=[jax.ShapeDtypeStruct((n_rows, w), dt) for w, dt, _ in outs],
        sem=("arbitrary",),
    )(*[op[0] for op in rows], *[a for a, _ in pars])
    return list(res)


def _rw_bwd(name, f, rows, pars, douts, drow, dpar, tr):
    n_rows = rows[0][0].shape[0]
    nr, npar = len(rows), len(pars)
    dgiven = [d for d in douts if d is not None]
    nd = len(dgiven)

    def body(*refs):
        r = [_ld(refs[i], rows[i][3]) for i in range(nr)]
        p = [_ld(refs[nr + i], pars[i][1]) for i in range(npar)]
        d = [_ld(refs[nr + npar + i], dgiven[i][3]) for i in range(nd)]
        orefs = refs[nr + npar + nd:]

        def g(dr, dp):
            rr, pp = list(r), list(p)
            for j, (idx, _) in enumerate(drow):
                rr[idx] = dr[j]
            for j, idx in enumerate(dpar):
                pp[idx] = dp[j]
            return tuple(f(rr, pp))

        out, vjp = jax.vjp(g, [r[idx] for idx, _ in drow], [p[idx] for idx in dpar])
        ct, j = [], 0
        for k, o in enumerate(out):
            if douts[k] is None:
                ct.append(jax.tree.map(jnp.zeros_like, o))
            else:
                ct.append(d[j])
                j += 1
        gdr, gdp = vjp(tuple(ct))
        for j, (idx, _) in enumerate(drow):
            _st(orefs[j], gdr[j], rows[idx][3])

        @pl.when(pl.program_id(0) == 0)
        def _():
            for j in range(len(dpar)):
                orefs[len(drow) + j][...] = jnp.zeros_like(orefs[len(drow) + j])

        for j, idx in enumerate(dpar):
            _st(orefs[len(drow) + j], gdp[j], pars[idx][1], accumulate=True)

    res = _pc(
        body, name=name, grid=(n_rows // tr,),
        in_specs=[_row_spec(tr, op) for op in rows] + [_full_spec(a) for a, _ in pars] + [_row_spec(tr, op) for op in dgiven],
        out_specs=[pl.BlockSpec((tr, rows[idx][2]), lambda i: (i, 0)) for idx, _ in drow] + [_full_spec(pars[idx][0]) for idx in dpar],
        out_shape=[jax.ShapeDtypeStruct((n_rows, rows[idx][2]), dt) for idx, dt in drow]
        + [jax.ShapeDtypeStruct(pars[idx][0].shape, F32) for idx in dpar],
        sem=("arbitrary",),
    )(*[op[0] for op in rows], *[a for a, _ in pars], *[op[0] for op in dgiven])
    res = list(res)
    return res[:len(drow)], res[len(drow):]


def _sigmoid(x):
    return jax.nn.sigmoid(x)


def _silu(x):
    return x * _sigmoid(x)


def _rms(x, g):
    return x * lax.rsqrt(jnp.mean(x * x, axis=-1, keepdims=True) + EPS) * g


def _f_rms(r, p):
    return [_rms(r[0], p[0])]


def _f_rms_res(r, p):
    return [r[0], _rms(r[0], p[0])]


def _f_cast(r, p):
    return [r[0]]


def _f_gate_ab(r, p):
    o, ga, mixed, gb = r
    return [(o * _silu(ga), mixed * p[0] * _silu(gb))]


def _f_sgu(r, p):
    u, v, gc = r
    lg, lb, w, b = p
    n = float(D_MODEL)
    mu = sum(jnp.sum(vk, axis=-1, keepdims=True) for vk in v) / n
    var = sum(jnp.sum(jnp.square(vk - mu), axis=-1, keepdims=True) for vk in v) / n
    rs = lax.rsqrt(var + EPS)
    t = w[0].shape[0]
    tri = lax.broadcasted_iota(jnp.int32, (t, t), 0) >= lax.broadcasted_iota(jnp.int32, (t, t), 1)
    outs = []
    for k in range(len(v)):
        vn = (v[k] - mu) * rs * lg[k] + lb[k]
        mixed = jnp.dot(jnp.where(tri, w[k], 0.0), vn, preferred_element_type=F32) + b[k]
        outs.append(u[k] * mixed * _silu(gc[k]))
    return [tuple(outs)]


def _gelu(x):
    return 0.5 * x * (1.0 + jnp.tanh(math.sqrt(2.0 / math.pi) * (x + 0.044715 * (x * x * x))))


def _f_gelu_y(r, p):
    yc, uf = r
    return [_gelu(yc + p[0] * uf)]


def _f_glu_gate(r, p):
    t12, gd = r
    return [t12[0] * _sigmoid(t12[1]) * _silu(gd)]


def _f_s5_prep(r, p):
    ar, ai, ldt = r
    dt = jnp.exp(ldt)
    mag = jnp.exp(dt * ar)
    abar_re = mag * jnp.cos(dt * ai)
    abar_im = mag * jnp.sin(dt * ai)
    nr, ni = abar_re - 1.0, abar_im
    inv = 1.0 / (ar * ar + ai * ai)
    return [abar_re, abar_im, (nr * ar + ni * ai) * inv, (ni * ar - nr * ai) * inv]


def _f_bbar(r, p):
    br, bi, cr, ci = r
    return [cr * br - ci * bi, cr * bi + ci * br]


def _loss_head(x, target, g):
    tr = 256
    n_rows, width = x.shape

    def f(xv, gv, tv):
        err = jnp.square(_rms(xv, gv) - tv)
        return 0.5 * jnp.mean(err, axis=-1, keepdims=True)

    def body(x_ref, t_ref, g_ref, loss_ref, dx_ref, dg_ref):
        @pl.when(pl.program_id(0) == 0)
        def _():
            loss_ref[...] = jnp.zeros_like(loss_ref)
            dg_ref[...] = jnp.zeros_like(dg_ref)

        tv = t_ref[...]
        row_loss, vjp = jax.vjp(lambda a, b: f(a, b, tv), x_ref[...], g_ref[...])
        dx, dg = vjp(jnp.ones_like(row_loss))
        dx_ref[...] = dx
        dg_ref[...] += dg
        loss_ref[...] += jnp.broadcast_to(jnp.sum(row_loss, axis=0, keepdims=True), loss_ref.shape)

    blk = pl.BlockSpec((tr, width), lambda i: (i, 0))
    one = pl.BlockSpec((1, width), lambda i: (0, 0))
    return _pc(
        body, name="loss_head", grid=(n_rows // tr,), in_specs=[blk, blk, one],
        out_specs=[pl.BlockSpec((1, 128), lambda i: (0, 0)), blk, one],
        out_shape=[jax.ShapeDtypeStruct((1, 128), F32), jax.ShapeDtypeStruct(x.shape, F32), jax.ShapeDtypeStruct((1, width), F32)],
        sem=("arbitrary",),
    )(x, target, g)


_NT = (((1,), (1,)), ((), ()))
_TN = (((0,), (0,)), ((), ()))


def _tile(n, cap):
    t = min(n, cap)
    while n % t:
        t -= 128
    assert t > 0
    return t


def _mm(name, a, b, *, ta=False, tb=False, out_dtype=F32, a_off=0, a_width=None, res=None, out_stack=None):
    assert not (ta and tb)
    stacked = b.ndim == 3
    bk, bn = (b.shape[1], b.shape[0] * b.shape[2]) if stacked else b.shape
    if ta:
        kc = a.shape[0]
        m = a.shape[1] - a_off if a_width is None else a_width
        n = bn
        assert bk == kc and not stacked
    else:
        m = a.shape[0]
        kc = a.shape[1] - a_off if a_width is None else a_width
        n = bk if tb else bn
        assert (bn if tb else bk) == kc
    tm = _tile(m, 512)
    tn = _tile(b.shape[2] if stacked and not tb else (out_stack or n), 1024)
    tk = _tile(b.shape[2] if stacked and tb else kc, 1024)
    nk = kc // tk
    if ta:
        assert a_off % tm == 0
        a_spec = pl.BlockSpec((tk, tm), lambda i, j, k, o=a_off // tm: (k, i + o))
        dims = _TN
    else:
        assert a_off % tk == 0
        a_spec = pl.BlockSpec((tm, tk), lambda i, j, k, o=a_off // tk: (i, k + o))
        dims = _NT if tb else (((1,), (0,)), ((), ()))
    if stacked and tb:
        b_spec = pl.BlockSpec((None, tn, tk), lambda i, j, k, q=b.shape[2] // tk: (k // q, j, k % q))
    elif stacked:
        b_spec = pl.BlockSpec((None, tk, tn), lambda i, j, k, q=b.shape[2] // tn: (j // q, k, j % q))
    else:
        b_spec = pl.BlockSpec((tn, tk), lambda i, j, k: (j, k)) if tb else pl.BlockSpec((tk, tn), lambda i, j, k: (k, j))
    if out_stack:
        out_spec = pl.BlockSpec((None, tm, tn), lambda i, j, k, q=out_stack // tn: (j // q, i, j % q))
        out_shape = jax.ShapeDtypeStruct((n // out_stack, m, out_stack), out_dtype)
    else:
        out_spec = pl.BlockSpec((tm, tn), lambda i, j, k: (i, j))
        out_shape = jax.ShapeDtypeStruct((m, n), out_dtype)
    in_specs, args = [a_spec, b_spec], [a, b]
    has_res = res is not None
    if has_res:
        in_specs.append(pl.BlockSpec((tm, tn), lambda i, j, k: (i, j)))
        args.append(res)

    def body(*refs):
        a_ref, b_ref = refs[0], refs[1]
        o_ref, acc_ref = refs[-2], refs[-1]
        k = pl.program_id(2)

        @pl.when(k == 0)
        def _():
            acc_ref[...] = jnp.zeros_like(acc_ref)

        acc_ref[...] += lax.dot_general(a_ref[...].astype(BF16), b_ref[...].astype(BF16), dims, preferred_element_type=F32)

        @pl.when(k == nk - 1)
        def _():
            acc = acc_ref[...]
            if has_res:
                acc = acc + refs[2][...].astype(F32)
            o_ref[...] = acc.astype(o_ref.dtype)

    return _pc(
        body, name=name, grid=(m // tm, n // tn, nk), in_specs=in_specs, out_specs=out_spec, out_shape=out_shape,
        scratch_shapes=[pltpu.VMEM((tm, tn), F32)], sem=("parallel", "parallel", "arbitrary"),
    )(*args)


def _head_masks(width, nsub):
    lane = lax.broadcasted_iota(jnp.int32, (1, width), 1)
    hd = width // nsub
    return [(lane >= h * hd) & (lane < (h + 1) * hd) for h in range(nsub)]


def _dilated_log_count(row0, tq, ext):
    delta = (row0 + lax.broadcasted_iota(jnp.int32, (tq, ext), 0)) - lax.broadcasted_iota(jnp.int32, (tq, ext), 1)
    cnt = (delta <= 128).astype(jnp.int32) + (((delta & 3) == 0) & (delta <= 512)).astype(jnp.int32) + ((delta & 15) == 0).astype(jnp.int32)
    logc = jnp.where(cnt == 3, math.log(3.0), jnp.where(cnt == 2, math.log(2.0), 0.0))
    return jnp.where((delta >= 0) & (cnt > 0), logc, NEG)


def _softmax_rows(s):
    m = jnp.max(s, axis=-1, keepdims=True)
    p = jnp.exp(s - m)
    return p / jnp.sum(p, axis=-1, keepdims=True)


def _attn_fwd(name, qa, ka, va, *, qc, kc, vc, width, nblk, nsub, causal, tq, scale, out_dtype):
    sq, t_len = qa.shape[0], ka.shape[0]

    def body(q_ref, k_ref, v_ref, o_ref):
        kb = k_ref[...].astype(BF16)
        vb = v_ref[...].astype(BF16)
        masks = _head_masks(width, nsub)
        for r in range(sq // tq):
            ext = (r + 1) * tq if causal else t_len
            q = q_ref[r * tq:(r + 1) * tq, :].astype(F32)
            ke, ve = kb[:ext], vb[:ext]
            bias = _dilated_log_count(r * tq, tq, ext) if causal else None
            o = None
            for h in range(nsub):
                qm = (jnp.where(masks[h], q, 0.0) if nsub > 1 else q).astype(BF16)
                s = lax.dot_general(qm, ke, _NT, preferred_element_type=F32) * scale
                if causal:
                    s = jnp.where(bias > 0.5 * NEG, s + bias, NEG)
                pn = _softmax_rows(s).astype(BF16)
                oh = jnp.dot(pn, ve, preferred_element_type=F32)
                o = oh if o is None else jnp.where(masks[h], oh, o)
            o_ref[r * tq:(r + 1) * tq, :] = o.astype(o_ref.dtype)

    return _pc(
        body, name=name, grid=(nblk,),
        in_specs=[pl.BlockSpec((sq, width), lambda i, c=qc: (0, c + i)), pl.BlockSpec((t_len, width), lambda i, c=kc: (0, c + i)),
                  pl.BlockSpec((t_len, width), lambda i, c=vc: (0, c + i))],
        out_specs=pl.BlockSpec((sq, width), lambda i: (0, i)),
        out_shape=jax.ShapeDtypeStruct((sq, nblk * width), out_dtype), sem=("parallel",),
    )(qa, ka, va)


def _attn_bwd(name, qa, ka, va, doa, *, qc, kc, vc, width, nblk, nsub, causal, tq, scale, out_dtype):
    sq, t_len = qa.shape[0], ka.shape[0]

    def body(q_ref, k_ref, v_ref, do_ref, dq_ref, dk_ref, dv_ref, dk_acc, dv_acc):
        kb = k_ref[...].astype(BF16)
        vb = v_ref[...].astype(BF16)
        masks = _head_masks(width, nsub)
        dk_acc[...] = jnp.zeros_like(dk_acc)
        dv_acc[...] = jnp.zeros_like(dv_acc)
        for r in range(sq // tq):
            ext = (r + 1) * tq if causal else t_len
            q = q_ref[r * tq:(r + 1) * tq, :].astype(F32)
            do = do_ref[r * tq:(r + 1) * tq, :].astype(F32)
            ke, ve = kb[:ext], vb[:ext]
            bias = _dilated_log_count(r * tq, tq, ext) if causal else None
            dq = None
            for h in range(nsub):
                qm = (jnp.where(masks[h], q, 0.0) if nsub > 1 else q).astype(BF16)
                dom = (jnp.where(masks[h], do, 0.0) if nsub > 1 else do).astype(BF16)
                s = lax.dot_general(qm, ke, _NT, preferred_element_type=F32) * scale
                if causal:
                    s = jnp.where(bias > 0.5 * NEG, s + bias, NEG)
                pn = _softmax_rows(s)
                dpn = lax.dot_general(dom, ve, _NT, preferred_element_type=F32)
                ds = pn * (dpn - jnp.sum(pn * dpn, axis=-1, keepdims=True))
                dsb = (ds * scale).astype(BF16)
                dqh = jnp.dot(dsb, ke, preferred_element_type=F32)
                dq = dqh if dq is None else jnp.where(masks[h], dqh, dq)
                dk_acc[0:ext, :] += lax.dot_general(dsb, qm, _TN, preferred_element_type=F32)
                dv_acc[0:ext, :] += lax.dot_general(pn.astype(BF16), dom, _TN, preferred_element_type=F32)
            dq_ref[r * tq:(r + 1) * tq, :] = dq.astype(dq_ref.dtype)
        dk_ref[...] = dk_acc[...].astype(dk_ref.dtype)
        dv_ref[...] = dv_acc[...].astype(dv_ref.dtype)

    return _pc(
        body, name=name, grid=(nblk,),
        in_specs=[pl.BlockSpec((sq, width), lambda i, c=qc: (0, c + i)), pl.BlockSpec((t_len, width), lambda i, c=kc: (0, c + i)),
                  pl.BlockSpec((t_len, width), lambda i, c=vc: (0, c + i)), pl.BlockSpec((sq, width), lambda i: (0, i))],
        out_specs=[pl.BlockSpec((sq, width), lambda i: (0, i)), pl.BlockSpec((t_len, width), lambda i: (0, i)), pl.BlockSpec((t_len, width), lambda i: (0, i))],
        out_shape=[jax.ShapeDtypeStruct((sq, nblk * width), out_dtype), jax.ShapeDtypeStruct((t_len, nblk * width), out_dtype),
                   jax.ShapeDtypeStruct((t_len, nblk * width), out_dtype)],
        scratch_shapes=[pltpu.VMEM((t_len, width), F32), pltpu.VMEM((t_len, width), F32)], sem=("parallel",),
    )(qa, ka, va, doa)


_SELF = dict(qc=0, kc=8, vc=16, width=128, nblk=8, nsub=2, causal=True, tq=256, scale=A_HEAD_DIM ** -0.5)
_CROSS = dict(qc=0, kc=0, vc=4, width=256, nblk=4, nsub=1, causal=False, tq=512, scale=X_HEAD_DIM ** -0.5)


def _window_sum(x, g, row, backward):
    n = x.shape[0]

    def shift(y, k):
        if backward:
            return jnp.where(row < n - k, pltpu.roll(y, n - k, 0), 0.0)
        return jnp.where(row >= k, pltpu.roll(y, k, 0), 0.0)

    s2 = x + shift(x, 1)
    s4 = s2 + shift(s2, 2)
    s8 = s4 + shift(s4, 4)
    s16 = s8 + shift(s8, 8)
    return jnp.where(g == 0, s2, jnp.where(g == 1, s4, jnp.where(g == 2, s8, s16)))


def _pool(name, arr, c0, backward, out_dtype):
    n = arr.shape[0]
    gw = 256

    def body(v_ref, o_ref):
        g = pl.program_id(0)
        v = v_ref[...].astype(F32)
        row = lax.broadcasted_iota(jnp.int32, v.shape, 0)
        w = jnp.where(g == 0, 2, jnp.where(g == 1, 4, jnp.where(g == 2, 8, 16)))
        cnt = jnp.minimum(row + 1, w).astype(F32)
        if backward:
            o_ref[...] = (_window_sum(v / cnt, g, row, True) - v).astype(o_ref.dtype)
        else:
            o_ref[...] = (_window_sum(v, g, row, False) / cnt - v).astype(o_ref.dtype)

    return _pc(
        body, name=name, grid=(4,), in_specs=[pl.BlockSpec((n, gw), lambda i, c=c0 // gw: (0, c + i))],
        out_specs=pl.BlockSpec((n, gw), lambda i: (0, i)), out_shape=jax.ShapeDtypeStruct((n, 4 * gw), out_dtype), sem=("parallel",),
    )(arr)


_SCAN_ROWS = 256


def _scan_fwd(bu3, a2):
    n = bu3.shape[0]

    def body(bu_ref, a_ref, h_ref, carry):
        @pl.when(pl.program_id(0) == 0)
        def _():
            carry[...] = jnp.zeros_like(carry)

        ar, ai = a_ref[0:16, :], a_ref[16:32, :]

        def step(t, c):
            hr, hi = c
            nr = ar * hr - ai * hi + bu_ref[t, 0:16, :]
            ni = ar * hi + ai * hr + bu_ref[t, 16:32, :]
            h_ref[t, 0:16, :] = nr
            h_ref[t, 16:32, :] = ni
            return nr, ni

        hr, hi = lax.fori_loop(0, _SCAN_ROWS, step, (carry[0:16, :], carry[16:32, :]), unroll=8)
        carry[0:16, :] = hr
        carry[16:32, :] = hi

    blk = pl.BlockSpec((_SCAN_ROWS, 32, 128), lambda i: (i, 0, 0))
    return _pc(
        body, name="s5_scan_fwd", grid=(n // _SCAN_ROWS,), in_specs=[blk, pl.BlockSpec((32, 128), lambda i: (0, 0))], out_specs=blk,
        out_shape=jax.ShapeDtypeStruct(bu3.shape, F32), scratch_shapes=[pltpu.VMEM((32, 128), F32)], sem=("arbitrary",),
    )(bu3, a2)


def _scan_bwd(dh3, h3, a2):
    n = dh3.shape[0]
    nb = n // _SCAN_ROWS

    def body(dh_ref, h_ref, a_ref, dbu_ref, da_ref, carry):
        @pl.when(pl.program_id(0) == 0)
        def _():
            carry[...] = jnp.zeros_like(carry)
            da_ref[...] = jnp.zeros_like(da_ref)

        ar, ai = a_ref[0:16, :], a_ref[16:32, :]

        def step(tt, c):
            gr, gi, dar, dai = c
            t = _SCAN_ROWS - 1 - tt
            hr, hi = h_ref[t, 0:16, :], h_ref[t, 16:32, :]
            dar = dar + gr * hr + gi * hi
            dai = dai - gr * hi + gi * hr
            ngr = dh_ref[t, 0:16, :] + ar * gr + ai * gi
            ngi = dh_ref[t, 16:32, :] - ai * gr + ar * gi
            dbu_ref[t, 0:16, :] = ngr
            dbu_ref[t, 16:32, :] = ngi
            return ngr, ngi, dar, dai

        z = jnp.zeros((16, 128), F32)
        gr, gi, dar, dai = lax.fori_loop(0, _SCAN_ROWS, step, (carry[0:16, :], carry[16:32, :], z, z), unroll=8)
        carry[0:16, :] = gr
        carry[16:32, :] = gi
        da_ref[0:16, :] += dar
        da_ref[16:32, :] += dai

    blk = pl.BlockSpec((_SCAN_ROWS, 32, 128), lambda i: (nb - 1 - i, 0, 0))
    small = pl.BlockSpec((32, 128), lambda i: (0, 0))
    return _pc(
        body, name="s5_scan_bwd", grid=(nb,), in_specs=[blk, blk, small], out_specs=[blk, small],
        out_shape=[jax.ShapeDtypeStruct(dh3.shape, F32), jax.ShapeDtypeStruct((32, 128), F32)],
        scratch_shapes=[pltpu.VMEM((32, 128), F32)], sem=("arbitrary",),
    )(dh3, h3, a2)


def _bdense(bb_re, bb_im):
    eye = jnp.eye(S5_GROUPS, dtype=F32)

    def one(bb):
        return jnp.einsum("gph,gk->ghkp", bb.reshape(S5_GROUPS, S5_STATE, S5_GROUP_DIM), eye).reshape(512, 2048)

    return jnp.concatenate([one(bb_re), one(bb_im)], axis=1)


def _cdense(c_re, c_im):
    eye = jnp.eye(S5_GROUPS, dtype=F32)

    def one(cc):
        return jnp.einsum("ghp,gk->gpkh", cc, eye).reshape(2048, 512)

    return jnp.concatenate([one(c_re), -one(c_im)], axis=0)


def _pool_dense(pw):
    eye = jnp.eye(4, dtype=pw.dtype)
    return jnp.einsum("gcd,gk->gckd", pw, eye).reshape(1024, 1024)


def _row2(v):
    return v.reshape(1, -1)


def _even_fwd(x, W, i, zero):
    hn = _rw_fwd("rms_fwd", _f_rms, [_cols(x)], [_par(_row2(W["norm_ab"][i]) + zero)], [(D_MODEL, BF16, 1)], 256)[0]
    z = _mm("mm_in_ab", hn, W["w_in_ab"][i])
    o = _attn_fwd("attn_self_fwd", z, z, z, out_dtype=F32, **_SELF)
    pooled = _pool("pool_fwd", z, 4096, False, BF16)
    wp = _pool_dense(W["pool_w"][i])
    mixed = _mm("mm_pool", pooled, wp)
    scale = _row2(W["pool_scale"][i])
    ab = _rw_fwd("gate_ab_fwd", _f_gate_ab, [_cols(o), _cols(z, 3072, 1024), _cols(mixed), _cols(z, 5120, 1024)], [_par(scale)],
                 [(2048, BF16, 2)], 256)[0]
    x1 = _mm("mm_out_ab", ab, W["w_out_ab"][i], res=x)
    return x1, dict(x=x, hn=hn, z=z, o=o, pooled=pooled, wp=wp, mixed=mixed, ab=ab)


def _even_bwd(dx1, sv, W, G, i, before_last):
    x, hn, z = sv["x"], sv["hn"], sv["z"]
    dab = _mm("mm_out_ab_dx", dx1, W["w_out_ab"][i], tb=True)
    G["w_out_ab"][i] = _mm("mm_out_ab_dw", sv["ab"], dx1, ta=True, out_dtype=BF16)
    scale = _row2(W["pool_scale"][i])
    (do, dga, dmixed, dgb), (dscale,) = _rw_bwd(
        "gate_ab_bwd", _f_gate_ab, [_cols(sv["o"]), _cols(z, 3072, 1024), _cols(sv["mixed"]), _cols(z, 5120, 1024)], [_par(scale)],
        [_cols(dab, nsplit=2)], [(0, F32), (1, BF16), (2, BF16), (3, BF16)], [0], 256)
    G["pool_scale"][i] = dscale.reshape(-1)
    dpooled = _mm("mm_pool_dx", dmixed, sv["wp"], tb=True)
    dwp = _mm("mm_pool_dw", sv["pooled"], dmixed, ta=True, out_dtype=BF16)
    G["pool_w"][i] = jnp.stack([dwp[g * 256:(g + 1) * 256, g * 256:(g + 1) * 256] for g in range(4)])
    dvb = _pool("pool_bwd", dpooled, 0, True, BF16)
    dq, dk, dv = _attn_bwd("attn_self_bwd", z, z, z, do, out_dtype=BF16, **_SELF)
    dz = jnp.concatenate([dq, dk, dv, dga, dvb, dgb], axis=1)
    dhn = _mm("mm_in_ab_dx", dz, W["w_in_ab"][i], tb=True)
    G["w_in_ab"][i] = _mm("mm_in_ab_dw", hn, dz, ta=True, out_dtype=BF16, out_stack=W["w_in_ab"][i].shape[2])
    g = _row2(W["norm_ab"][i]) + before_last()
    (dx,), (dg,) = _rw_bwd("rms_bwd", _f_rms_res, [_cols(x)], [_par(g)], [_cols(dx1), _cols(dhn)], [(0, F32)], [0], 256)
    G["norm_ab"][i] = dg.reshape(-1)
    return dx


def _odd_fwd(x, W, i, zero):
    hn = _rw_fwd("rms_fwd", _f_rms, [_cols(x)], [_par(_row2(W["norm_cd"][i]) + zero)], [(D_MODEL, BF16, 1)], 256)[0]
    z = _mm("mm_in_cd", hn, W["w_in_cd"][i])
    sgu_p = [_par(_row2(W["sgu_ln_g"][i]), 4), _par(_row2(W["sgu_ln_b"][i]), 4), _par(W["sgu_w"][i], 4), _par(W["sgu_b"][i][..., None], 4)]
    c_out = _rw_fwd("sgu_fwd", _f_sgu, [_cols(z, 0, 1024, 4), _cols(z, 1024, 1024, 4), _cols(z, 2048, 1024, 4)], sgu_p, [(1024, BF16, 4)], 128)[0]
    prep_rows = [_cols(W["s5_a_re"][i]), _cols(W["s5_a_im"][i]), _cols(W["s5_log_dt"][i].reshape(S5_GROUPS, 1))]
    abar_re, abar_im, coef_re, coef_im = _rw_fwd("s5_prep_fwd", _f_s5_prep, prep_rows, [], [(S5_STATE, F32, 1)] * 4, S5_GROUPS)
    bb_rows = [_cols(W["s5_b_re"][i].reshape(2048, 16)), _cols(W["s5_b_im"][i].reshape(2048, 16)), _cols(coef_re.reshape(2048, 1)), _cols(coef_im.reshape(2048, 1))]
    bb_re, bb_im = _rw_fwd("s5_bbar_fwd", _f_bbar, bb_rows, [], [(16, F32, 1)] * 2, 256)
    bd = _bdense(bb_re, bb_im).astype(BF16)
    cf = _cdense(W["s5_c_re"][i], W["s5_c_im"][i]).astype(BF16)
    a2 = jnp.concatenate([abar_re.reshape(16, 128), abar_im.reshape(16, 128)], axis=0)
    bu = _mm("mm_s5_bu", z, bd, a_off=3072, a_width=512)
    h3 = _scan_fwd(bu.reshape(SEQ, 32, 128), a2)
    h2 = h3.reshape(SEQ, 4096)
    yc = _mm("mm_s5_y", h2, cf)
    dpar = _row2(W["s5_d"][i])
    yg = _rw_fwd("gelu_fwd", _f_gelu_y, [_cols(yc), _cols(z, 3072, 512)], [_par(dpar)], [(512, BF16, 1)], 256)[0]
    w12 = jnp.concatenate([W["glu_w1"][i], W["glu_w2"][i]], axis=1)
    t12 = _mm("mm_glu", yg, w12)
    d_out = _rw_fwd("glu_gate_fwd", _f_glu_gate, [_cols(t12, nsplit=2), _cols(z, 3584, 512)], [], [(512, BF16, 1)], 256)[0]
    cd = jnp.concatenate([c_out, d_out], axis=1)
    x1 = _mm("mm_out_cd", cd, W["w_out_cd"][i], res=x)
    sv = dict(x=x, hn=hn, z=z, sgu_p=sgu_p, prep_rows=prep_rows, bb_rows=bb_rows, bb=(bb_re, bb_im), bd=bd, cf=cf, a2=a2,
              h3=h3, h2=h2, yc=yc, dpar=dpar, yg=yg, w12=w12, t12=t12, cd=cd)
    return x1, sv


def _odd_bwd(dx1, sv, W, G, i, before_last):
    x, hn, z = sv["x"], sv["hn"], sv["z"]
    dcd = _mm("mm_out_cd_dx", dx1, W["w_out_cd"][i], tb=True)
    G["w_out_cd"][i] = _mm("mm_out_cd_dw", sv["cd"], dx1, ta=True, out_dtype=BF16)
    (du, dv, dgc), (dlg, dlb, dsw, dsb) = _rw_bwd(
        "sgu_bwd", _f_sgu, [_cols(z, 0, 1024, 4), _cols(z, 1024, 1024, 4), _cols(z, 2048, 1024, 4)], sv["sgu_p"],
        [_cols(dcd, 0, 1024, 4)], [(0, BF16), (1, BF16), (2, BF16)], [0, 1, 2, 3], 128)
    G["sgu_ln_g"][i], G["sgu_ln_b"][i] = dlg.reshape(-1), dlb.reshape(-1)
    G["sgu_w"][i], G["sgu_b"][i] = dsw, dsb[..., 0]
    (dt12, dgd), _ = _rw_bwd("glu_gate_bwd", _f_glu_gate, [_cols(sv["t12"], nsplit=2), _cols(z, 3584, 512)], [], [_cols(dcd, 1024, 512)],
                             [(0, BF16), (1, BF16)], [], 256)
    dyg = _mm("mm_glu_dx", dt12, sv["w12"], tb=True)
    dw12 = _mm("mm_glu_dw", sv["yg"], dt12, ta=True, out_dtype=BF16)
    G["glu_w1"][i], G["glu_w2"][i] = dw12[:, :512], dw12[:, 512:]
    (dyc, duf1), (dd,) = _rw_bwd("gelu_bwd", _f_gelu_y, [_cols(sv["yc"]), _cols(z, 3072, 512)], [_par(sv["dpar"])], [_cols(dyg)],
                                 [(0, BF16), (1, F32)], [0], 256)
    G["s5_d"][i] = dd.reshape(-1)
    dh2 = _mm("mm_s5_y_dx", dyc, sv["cf"], tb=True)
    dcf = _mm("mm_s5_y_dw", sv["h2"], dyc, ta=True)
    _, cvjp = jax.vjp(_cdense, W["s5_c_re"][i], W["s5_c_im"][i])
    G["s5_c_re"][i], G["s5_c_im"][i] = cvjp(dcf)
    dbu3, da2 = _scan_bwd(dh2.reshape(SEQ, 32, 128), sv["h3"], sv["a2"])
    dbu = dbu3.reshape(SEQ, 4096)
    duf2 = _mm("mm_s5_bu_dx", dbu, sv["bd"], tb=True)
    dbd = _mm("mm_s5_bu_dw", z, dbu, ta=True, a_off=3072, a_width=512)
    _, bvjp = jax.vjp(_bdense, *sv["bb"])
    dbb_re, dbb_im = bvjp(dbd)
    (dbr, dbi, dcr, dci), _ = _rw_bwd("s5_bbar_bwd", _f_bbar, sv["bb_rows"], [], [_cols(dbb_re), _cols(dbb_im)],
                                      [(0, F32), (1, F32), (2, F32), (3, F32)], [], 256)
    G["s5_b_re"][i], G["s5_b_im"][i] = dbr.reshape(S5_GROUPS, S5_STATE, S5_GROUP_DIM), dbi.reshape(S5_GROUPS, S5_STATE, S5_GROUP_DIM)
    douts = [_cols(da2[0:16].reshape(S5_GROUPS, S5_STATE)), _cols(da2[16:32].reshape(S5_GROUPS, S5_STATE)),
             _cols(dcr.reshape(S5_GROUPS, S5_STATE)), _cols(dci.reshape(S5_GROUPS, S5_STATE))]
    (dar, dai, dldt), _ = _rw_bwd("s5_prep_bwd", _f_s5_prep, sv["prep_rows"], [], douts, [(0, F32), (1, F32), (2, F32)], [], S5_GROUPS)
    G["s5_a_re"][i], G["s5_a_im"][i], G["s5_log_dt"][i] = dar, dai, dldt.reshape(-1)
    dxd = (duf1 + duf2).astype(BF16)
    dz = jnp.concatenate([du, dv, dgc, dxd, dgd], axis=1)
    dhn = _mm("mm_in_cd_dx", dz, W["w_in_cd"][i], tb=True)
    G["w_in_cd"][i] = _mm("mm_in_cd_dw", hn, dz, ta=True, out_dtype=BF16, out_stack=W["w_in_cd"][i].shape[2])
    g = _row2(W["norm_cd"][i]) + before_last()
    (dx,), (dg,) = _rw_bwd("rms_bwd", _f_rms_res, [_cols(x)], [_par(g)], [_cols(dx1), _cols(dhn)], [(0, F32)], [0], 256)
    G["norm_cd"][i] = dg.reshape(-1)
    return dx


def _cross_fwd(x1, mem_n, W, l):
    hx = _rw_fwd("rms_fwd", _f_rms, [_cols(x1)], [_par(_row2(W["norm_x"][l]))], [(D_MODEL, BF16, 1)], 256)[0]
    qx = _mm("mm_xq", hx, W["w_xq"][l], out_dtype=BF16)
    kv = _mm("mm_xkv", mem_n, W["w_xkv"][l], out_dtype=BF16)
    ox = _attn_fwd("attn_cross_fwd", qx, kv, kv, out_dtype=BF16, **_CROSS)
    x2 = _mm("mm_xo", ox, W["w_xo"][l], res=x1)
    return x2, dict(x1=x1, hx=hx, qx=qx, kv=kv, ox=ox)


def _cross_bwd(dx2, dmem_n, sv, mem_n, W, G, l, zero):
    dox = _mm("mm_xo_dx", dx2, W["w_xo"][l], tb=True, out_dtype=BF16)
    G["w_xo"][l] = _mm("mm_xo_dw", sv["ox"], dx2, ta=True, out_dtype=BF16)
    dqx, dk, dv = _attn_bwd("attn_cross_bwd", sv["qx"], sv["kv"], sv["kv"], dox, out_dtype=BF16, **_CROSS)
    dkv = jnp.concatenate([dk, dv], axis=1)
    dhx = _mm("mm_xq_dx", dqx, W["w_xq"][l], tb=True)
    G["w_xq"][l] = _mm("mm_xq_dw", sv["hx"], dqx, ta=True, out_dtype=BF16)
    dmem_n = _mm("mm_xkv_dx", dkv, W["w_xkv"][l], tb=True, res=dmem_n)
    G["w_xkv"][l] = _mm("mm_xkv_dw", mem_n, dkv, ta=True, out_dtype=BF16, out_stack=W["w_xkv"][l].shape[2])
    (dx1,), (dg,) = _rw_bwd("rms_bwd", _f_rms_res, [_cols(sv["x1"])], [_par(_row2(W["norm_x"][l]) + zero)], [_cols(dx2), _cols(dhx)], [(0, F32)], [0], 256)
    G["norm_x"][l] = dg.reshape(-1)
    return dx1, dmem_n


_PER_LAYER = ("pool_scale", "norm_ab", "norm_cd", "sgu_ln_g", "sgu_ln_b", "sgu_w", "sgu_b", "s5_d", "s5_c_re", "s5_c_im", "s5_b_re", "s5_b_im",
              "s5_a_re", "s5_a_im", "s5_log_dt", "w_in_ab", "pool_w", "w_out_ab", "w_in_cd", "glu_w1", "glu_w2", "w_out_cd")


def _local_step(x, mem, target, W, weights_of, before_last, after_layer):
    G = {k: [None, None] for k in _PER_LAYER}
    for k in ("norm_x", "w_xq", "w_xkv", "w_xo"):
        G[k] = [None] * DEPTH
    mem_rows = [_cols(mem)]
    mem_par = [_par(_row2(W["mem_norm"]))]
    mem_n = _rw_fwd("rms_fwd_mem", _f_rms, mem_rows, mem_par, [(D_MODEL, BF16, 1)], 256)[0]
    saved = []
    for layer in range(DEPTH):
        zero = weights_of(layer, x if layer else mem_n)
        x, sv = _even_fwd(x, W, layer // 2, zero) if layer % 2 == 0 else _odd_fwd(x, W, layer // 2, zero)
        x, svx = _cross_fwd(x, mem_n, W, layer)
        saved.append((sv, svx))
    loss, dx, dfinal = _loss_head(x, target, _row2(W["final_norm"]))
    G["final_norm"] = dfinal.reshape(-1)
    dmem_n, zero = None, 0.0
    for layer in reversed(range(DEPTH)):
        sv, svx = saved[layer]
        dx, dmem_n = _cross_bwd(dx, dmem_n, svx, mem_n, W, G, layer, zero)
        hook = functools.partial(before_last, layer, G)
        dx = _even_bwd(dx, sv, W, G, layer // 2, hook) if layer % 2 == 0 else _odd_bwd(dx, sv, W, G, layer // 2, hook)
        zero = after_layer(layer, G)
    _, (dmn,) = _rw_bwd("rms_bwd_mem", _f_rms, mem_rows, mem_par, [_cols(dmem_n)], [], [0], 256)
    G["mem_norm"] = dmn.reshape(-1)
    return loss, dx, G


_HBM = pl.BlockSpec(memory_space=pltpu.HBM)
_ANY = pl.BlockSpec(memory_space=pl.ANY)
_SEM = pl.BlockSpec(memory_space=pltpu.SEMAPHORE)
_N_PEERS = N_DEV - 1


def _mesh_pos():
    return lax.axis_index("x"), lax.axis_index("y"), lax.axis_index("c")


def _peer(pos, k):
    x, y, c = pos
    return (x ^ ((k >> 2) & 1), y ^ ((k >> 1) & 1), c ^ (k & 1))


def _lin(pos):
    return 4 * pos[0] + 2 * pos[1] + pos[2]


def _part(ref, spec, idx):
    if spec is None:
        return ref
    ax, n = spec
    return ref.at[(slice(None),) * ax + (pl.ds(pl.multiple_of(idx * n, n), n),)]


_STACK = "stack"


def _ends(gather, ins, lands, specs, t, sender, receiver):
    spec = specs[t]
    if gather:
        return ins[t], (lands[t].at[sender] if spec == _STACK else _part(lands[t], spec, sender))
    src = ins[t].at[receiver] if spec == _STACK else _part(ins[t], spec, receiver)
    return src, lands[t].at[sender]


def _land_shape(gather, arr, spec):
    if spec == _STACK:
        return (N_DEV,) + arr.shape if gather else arr.shape
    if gather:
        ax, n = spec
        return arr.shape[:ax] + (n * N_DEV,) + arr.shape[ax + 1:]
    if spec is None:
        return (N_DEV,) + arr.shape
    ax, n = spec
    return (N_DEV,) + arr.shape[:ax] + (n,) + arr.shape[ax + 1:]


def _exchange_place(name, gather, arrs, specs, after):
    nt = len(arrs)

    def body(*refs):
        refs = refs[:nt] + refs[len(refs) - nt - 1:]
        ins, lands, sems = refs[:nt], refs[nt:2 * nt], refs[2 * nt]
        me = _lin(_mesh_pos())
        copies = [pltpu.make_async_copy(*_ends(gather, ins, lands, specs, t, me, me), sems.at[t]) for t in range(nt)]
        for cp in copies:
            cp.start()
        for cp in copies:
            cp.wait()

    extra = [] if after is None else [after]
    return _pc(
        body, name=name, in_specs=[_ANY] * (nt + len(extra)), out_specs=[_ANY] * nt,
        out_shape=[jax.ShapeDtypeStruct(_land_shape(gather, a, s), a.dtype) for a, s in zip(arrs, specs)],
        scratch_shapes=[pltpu.SemaphoreType.DMA((nt,))],
    )(*arrs, *extra)


def _exchange_start(name, gather, arrs, lands, specs):
    nt = len(arrs)

    def body(*refs):
        ins, lnd = refs[:nt], refs[nt:2 * nt]
        send_sems, recv_sems = refs[2 * nt], refs[2 * nt + 1]
        token = refs[-1]
        pos = _mesh_pos()
        me = _lin(pos)
        for k in range(1, N_DEV):
            peer = _peer(pos, k)
            for t in range(nt):
                src, dst = _ends(gather, ins, lnd, specs, t, me, _lin(peer))
                pltpu.make_async_remote_copy(
                    src_ref=src, dst_ref=dst, send_sem=send_sems.at[t * _N_PEERS + k - 1], recv_sem=recv_sems.at[t * _N_PEERS + k - 1],
                    device_id=peer, device_id_type=pl.DeviceIdType.MESH).start()
        token[...] = jnp.zeros_like(token)

    hbm = lambda a: pltpu.HBM(a.shape, a.dtype)
    out = pl.pallas_call(
        body, name=name,
        out_shape=(pltpu.SemaphoreType.DMA((nt * _N_PEERS,)), pltpu.SemaphoreType.DMA((nt * _N_PEERS,)), *[hbm(a) for a in arrs], *[hbm(a) for a in lands],
                   jax.ShapeDtypeStruct((8, 128), F32)),
        in_specs=[_HBM] * (2 * nt), out_specs=(_SEM, _SEM, *[_HBM] * (2 * nt), pl.BlockSpec(memory_space=pltpu.VMEM)),
        input_output_aliases={j: 2 + j for j in range(2 * nt)},
        compiler_params=pltpu.CompilerParams(has_side_effects=pltpu.SideEffectType.DATAFLOW_SIDE_EFFECTING),
        interpret=False,
    )(*[pltpu.with_memory_space_constraint(a, pltpu.HBM) for a in arrs], *[pltpu.with_memory_space_constraint(a, pltpu.HBM) for a in lands])
    return dict(send=out[0], recv=out[1], arrs=list(out[2:2 + nt]), lands=list(out[2 + nt:2 + 2 * nt]), token=out[-1][0, 0], token_arr=out[-1], gather=gather, specs=specs)


def _exchange_wait(name, ex, after):
    nt = len(ex["arrs"])
    gather, specs = ex["gather"], ex["specs"]

    def body(*refs):
        ins, lnd = refs[:nt], refs[nt:2 * nt]
        send_sems, recv_sems = refs[2 * nt], refs[2 * nt + 1]
        pos = _mesh_pos()
        me = _lin(pos)
        for k in range(1, N_DEV):
            peer = _peer(pos, k)
            for t in range(nt):
                src, _ = _ends(gather, ins, lnd, specs, t, me, _lin(peer))
                _, dst = _ends(gather, ins, lnd, specs, t, _lin(peer), me)
                cp = pltpu.make_async_remote_copy(
                    src_ref=src, dst_ref=dst, send_sem=send_sems.at[t * _N_PEERS + k - 1], recv_sem=recv_sems.at[t * _N_PEERS + k - 1],
                    device_id=peer, device_id_type=pl.DeviceIdType.MESH)
                cp.wait_send()
                cp.wait_recv()

    hbm = lambda a: pltpu.HBM(a.shape, a.dtype)
    out = pl.pallas_call(
        body, name=name, out_shape=(*[hbm(a) for a in ex["arrs"]], *[hbm(a) for a in ex["lands"]]),
        in_specs=[_HBM] * (2 * nt) + [_SEM, _SEM, _ANY], out_specs=tuple([_HBM] * (2 * nt)),
        input_output_aliases={j: j for j in range(2 * nt)},
        compiler_params=pltpu.CompilerParams(has_side_effects=pltpu.SideEffectType.DATAFLOW_SIDE_EFFECTING),
        interpret=False,
    )(*ex["arrs"], *ex["lands"], ex["send"], ex["recv"], after)
    return list(out[nt:])


def _exchange_begin(name, gather, arrs, specs, after=None):
    lands = _exchange_place(name + "_place", gather, arrs, specs, after)
    return _exchange_start(name + "_start", gather, arrs, lands, specs)


def _adam(name, w, parts, m, v, layer, bufs):
    r, c = parts.shape[1:]
    tr = _row_block(r, c, 1 << 20)
    nb = r // tr

    def body(w_ref, m_ref, v_ref, p_ref, *rest):
        g_ref, d_ref, nm_ref, nv_ref = rest[-4:]
        g = p_ref[0].astype(F32)
        for k in range(1, N_DEV):
            g = g + p_ref[k].astype(F32)
        mm = ADAM_B1 * m_ref[...] + (1.0 - ADAM_B1) * g
        vv = ADAM_B2 * v_ref[...] + (1.0 - ADAM_B2) * jnp.square(g)
        m_hat = mm / (1.0 - ADAM_B1 ** ADAM_STEP)
        v_hat = vv / (1.0 - ADAM_B2 ** ADAM_STEP)
        g_ref[...] = g
        d_ref[...] = -ADAM_LR * (m_hat / (jnp.sqrt(v_hat) + ADAM_EPS) + ADAM_WD * w_ref[...])
        nm_ref[...] = mm
        nv_ref[...] = vv

    blk = pl.BlockSpec((tr, c), lambda i, o=layer * nb: (o + i, 0))
    in_specs = [blk, blk, blk, pl.BlockSpec((N_DEV, tr, c), lambda i: (0, i, 0))]
    args = [w, m, v, parts]
    aliases = None
    if bufs is not None:
        in_specs += [_ANY] * 4
        args += list(bufs)
        aliases = {4 + j: j for j in range(4)}
    return _pc(
        body, name=name, grid=(nb,), in_specs=in_specs, out_specs=[blk] * 4, out_shape=[jax.ShapeDtypeStruct(w.shape, F32)] * 4,
        aliases=aliases, sem=("parallel",),
    )(*args)


def _pack(arrs):
    out = []
    for a in arrs:
        f = a.reshape(-1)
        out.append(jnp.pad(f, (0, (-f.shape[0]) % 1024)).reshape(-1, 128))
    rows = sum(a.shape[0] for a in out)
    out.append(jnp.zeros(((-rows) % 256, 128), out[0].dtype))
    return jnp.concatenate(out, axis=0)


def _unpack(packed, shapes):
    out, r = [], 0
    for s in shapes:
        n = math.prod(s)
        rows = (n + 1023) // 1024 * 8
        out.append(packed[r:r + rows].reshape(-1)[:n].reshape(s))
        r += rows
    return out


def _row_block(r, c, limit):
    best = None
    for tr in range(16, r + 1, 16):
        if r % tr == 0 and tr * c * 4 <= limit:
            best = tr
    return r if best is None else best


_BIG_AXIS = {"w_in_ab": 1, "pool_w": 1, "w_out_ab": 0, "w_in_cd": 1, "glu_w1": 0, "glu_w2": 0, "w_out_cd": 0, "w_xq": 0, "w_xkv": 1, "w_xo": 0}
_MIXER_BIG = (("w_in_ab", "pool_w", "w_out_ab"), ("w_in_cd", "glu_w1", "glu_w2", "w_out_cd"))
_CROSS_BIG = ("w_xq", "w_xkv", "w_xo")
_SMALL_SPLIT = ["norm_cd", "sgu_ln_g", "sgu_ln_b", "s5_d"]
_STACKED = ("w_in_ab", "w_in_cd", "w_xkv")
_REPLICATED_ODD = ["sgu_w", "sgu_b", "s5_a_re", "s5_a_im", "s5_log_dt", "s5_b_re", "s5_b_im", "s5_c_re", "s5_c_im", "final_norm"]
_REPLICATED_EVEN = ["norm_ab", "pool_scale", "norm_x", "mem_norm"]
_REPLICATED = _REPLICATED_ODD + _REPLICATED_EVEN
_WEIGHTS = ["norm_ab", "w_in_ab", "pool_w", "pool_scale", "w_out_ab", "norm_cd", "w_in_cd", "sgu_ln_g", "sgu_ln_b", "sgu_w", "sgu_b", "s5_a_re",
            "s5_a_im", "s5_log_dt", "s5_b_re", "s5_b_im", "s5_c_re", "s5_c_im", "s5_d", "glu_w1", "glu_w2", "w_out_cd", "norm_x", "w_xq",
            "w_xkv", "w_xo", "mem_norm", "final_norm"]


def _layer_big(layer):
    return [(n, layer // 2) for n in _MIXER_BIG[layer % 2]] + [(n, layer) for n in _CROSS_BIG]


def _rows2d(a):
    return a.reshape(-1, a.shape[-1])


def _small_rows(block):
    return jnp.pad(block, ((0, 0), (0, 128 - block.shape[1])))


def kernel(x, mem, norm_ab, w_in_ab, pool_w, pool_scale, w_out_ab, norm_cd, w_in_cd, sgu_ln_g, sgu_ln_b, sgu_w, sgu_b, s5_a_re, s5_a_im, s5_log_dt, s5_b_re, s5_b_im, s5_c_re, s5_c_im, s5_d, glu_w1, glu_w2, w_out_cd, norm_x, w_xq, w_xkv, w_xo, mem_norm, final_norm, loss_target, m_norm_ab, m_w_in_ab, m_pool_w, m_pool_scale, m_w_out_ab, m_norm_cd, m_w_in_cd, m_sgu_ln_g, m_sgu_ln_b, m_sgu_w, m_sgu_b, m_s5_a_re, m_s5_a_im, m_s5_log_dt, m_s5_b_re, m_s5_b_im, m_s5_c_re, m_s5_c_im, m_s5_d, m_glu_w1, m_glu_w2, m_w_out_cd, m_norm_x, m_w_xq, m_w_xkv, m_w_xo, m_mem_norm, m_final_norm, v_norm_ab, v_w_in_ab, v_pool_w, v_pool_scale, v_w_out_ab, v_norm_cd, v_w_in_cd, v_sgu_ln_g, v_sgu_ln_b, v_sgu_w, v_sgu_b, v_s5_a_re, v_s5_a_im, v_s5_log_dt, v_s5_b_re, v_s5_b_im, v_s5_c_re, v_s5_c_im, v_s5_d, v_glu_w1, v_glu_w2, v_w_out_cd, v_norm_x, v_w_xq, v_w_xkv, v_w_xo, v_mem_norm, v_final_norm):
    args = locals()
    w = {n: args[n] for n in _WEIGHTS}
    m = {n: args["m_" + n] for n in _WEIGHTS}
    v = {n: args["v_" + n] for n in _WEIGHTS}

    def spec_of(name):
        shp = w[name].shape[1:]
        return _STACK if name in _STACKED else (_BIG_AXIS[name], shp[_BIG_AXIS[name]])

    def begin_gather(layer, after):
        blocks, specs = [], []
        for name, i in _layer_big(layer):
            b2 = _rows2d(w[name])
            r = b2.shape[0] // w[name].shape[0]
            cast = _rw_fwd("cast_" + name, _f_cast, [_cols(b2, r0=i * r)], [], [(b2.shape[1], BF16, 1)], _row_block(r, b2.shape[1], 2 << 20), n_rows=r)[0]
            blocks.append(cast.reshape(w[name].shape[1:]))
            specs.append(spec_of(name))
        if layer == 0:
            blocks.append(jnp.concatenate([_small_rows(w[n]) for n in _SMALL_SPLIT], axis=0))
            specs.append((0, 8))
        return _exchange_begin("gather%d" % layer, True, blocks, specs, after)

    gathers = [begin_gather(0, None)] + [None] * (DEPTH - 1)
    W = {n: w[n] for n in _REPLICATED}
    W["mem_norm"] = w["mem_norm"] + gathers[0]["token"]
    for name in _BIG_AXIS:
        W[name] = [None] * w[name].shape[0]

    def weights_of(layer, after):
        got = _exchange_wait("gather%d_wait" % layer, gathers[layer], after)
        for (name, i), arr in zip(_layer_big(layer), got):
            W[name][i] = arr
        if layer == 0:
            sm = got[-1].reshape(N_DEV, 4, 2, 128)
            for j, n in enumerate(_SMALL_SPLIT):
                width = w[n].shape[1]
                W[n] = sm[:, j, :, :width].transpose(1, 0, 2).reshape(2, N_DEV * width)
        if layer + 1 == DEPTH:
            return 0.0
        gathers[layer + 1] = begin_gather(layer + 1, got[0])
        return gathers[layer + 1]["token"]

    scatters, small = [None] * DEPTH, {}

    def before_last(layer, G):
        wholes = [G[name][i] for name, i in _layer_big(layer)]
        scatters[layer] = _exchange_begin("scatter%d" % layer, False, wholes, [spec_of(name) for name, i in _layer_big(layer)])
        return scatters[layer]["token"]

    def begin_small(tag, G, names, split):
        wholes, specs = [_pack([G[n] if n in ("mem_norm", "final_norm") else jnp.stack(G[n]) for n in names])], [None]
        if split:
            gs = jnp.stack([jnp.stack(G[n]).reshape(2, N_DEV, -1).transpose(1, 0, 2) for n in _SMALL_SPLIT[:3]]
                           + [jnp.pad(jnp.stack(G["s5_d"]).reshape(2, N_DEV, -1).transpose(1, 0, 2), ((0, 0), (0, 0), (0, 64)))], axis=1)
            wholes.append(gs.reshape(N_DEV * 8, 128))
            specs.append((0, 8))
        small[tag] = _exchange_begin("scatter_small_" + tag, False, wholes, specs)
        return small[tag]["token"]

    def after_layer(layer, G):
        return begin_small("odd", G, _REPLICATED_ODD, True) if layer == 1 else 0.0

    loss, dx, G = _local_step(x[0], mem[0], loss_target[0], W, weights_of, before_last, after_layer)
    loss = lax.psum(loss[0, 0], MESH_AXES)
    begin_small("even", G, _REPLICATED_EVEN, False)

    out = {}
    after = small["even"]["token_arr"]
    for layer in reversed(range(DEPTH)):
        got = _exchange_wait("scatter%d_wait" % layer, scatters[layer], after)
        for (name, i), arr in zip(_layer_big(layer), got):
            parts = arr.reshape((N_DEV,) + _rows2d(w[name][i]).shape)
            out[name] = _adam("adam_" + name, _rows2d(w[name]), parts, _rows2d(m[name]), _rows2d(v[name]), i, out.get(name))
        after = out[name][0]
    for name in _BIG_AXIS:
        out[name] = [a.reshape(w[name].shape) for a in out[name]]
    got = _exchange_wait("scatter_small_odd_wait", small["odd"], after)
    pk = lambda d: jnp.concatenate([_small_rows(d[n]) for n in _SMALL_SPLIT], axis=0)
    res = _adam("adam_small", pk(w), got[1], pk(m), pk(v), 0, None)
    for j, n in enumerate(_SMALL_SPLIT):
        out[n] = [a[2 * j:2 * j + 2, :w[n].shape[1]] for a in res]
    got = got[:1] + _exchange_wait("scatter_small_even_wait", small["even"], res[0])
    for names, parts in zip((_REPLICATED_ODD, _REPLICATED_EVEN), got):
        pk = lambda d: _pack([d[n] for n in names])
        res = _adam("adam_replicated", pk(w), parts, pk(m), pk(v), 0, None)
        shapes = [w[n].shape for n in names]
        un = [_unpack(a, shapes) for a in res]
        for j, n in enumerate(names):
            out[n] = [un[q][j] for q in range(4)]

    return (loss, dx[None], *[out[n][0] for n in _WEIGHTS], *[out[n][1] for n in _WEIGHTS], *[out[n][2] for n in _WEIGHTS],
            *[out[n][3] for n in _WEIGHTS])
```

```python
import functools
import math

import jax
import jax.numpy as jnp
from jax import lax
from jax.experimental import pallas as pl
from jax.experimental.pallas import tpu as pltpu

F32 = jnp.float32
BF16 = jnp.bfloat16

SEQ = 2048
D_MODEL = 1024
MEM_LEN = 256
DEPTH = 4
N_DEV = 8
EPS = 1e-6
NEG = -1e30
A_HEAD_DIM = 64
X_HEAD_DIM = 256
S5_GROUPS = 32
S5_STATE = 64
S5_GROUP_DIM = 16

ADAM_LR = 0.001
ADAM_B1 = 0.9
ADAM_B2 = 0.999
ADAM_EPS = 1e-08
ADAM_WD = 0.01
ADAM_STEP = 10

V7X_VMEM_LIMIT_BYTES = 56 * 1024 * 1024
MESH_AXES = ("x", "y", "c")


def _pc(body, *, name, out_shape, grid=None, in_specs=None, out_specs=None, scratch_shapes=(), aliases=None, sem=None):
    kw = {}
    if grid is not None:
        kw["grid"] = grid
    if in_specs is not None:
        kw["in_specs"] = in_specs
    if out_specs is not None:
        kw["out_specs"] = out_specs
    if aliases:
        kw["input_output_aliases"] = aliases
    return pl.pallas_call(
        body,
        name=name,
        out_shape=out_shape,
        scratch_shapes=list(scratch_shapes),
        compiler_params=pltpu.CompilerParams(dimension_semantics=sem, vmem_limit_bytes=V7X_VMEM_LIMIT_BYTES),
        interpret=False,
        **kw,
    )


def _cols(arr, c0=0, width=None, nsplit=1, r0=0):
    width = arr.shape[1] - c0 if width is None else width
    assert c0 % width == 0 and width % nsplit == 0
    return (arr, c0, width, nsplit, r0)


def _par(arr, nsplit=1):
    return (arr, nsplit)


def _ld(ref, nsplit):
    if nsplit == 1:
        return ref[...].astype(F32)
    if len(ref.shape) == 3:
        return tuple(ref[k].astype(F32) for k in range(nsplit))
    w = ref.shape[-1] // nsplit
    return tuple(ref[:, k * w:(k + 1) * w].astype(F32) for k in range(nsplit))


def _st(ref, val, nsplit, accumulate=False):
    if nsplit == 1:
        val = (val,)
    for k in range(nsplit):
        if nsplit == 1:
            idx = (Ellipsis,)
        elif len(ref.shape) == 3:
            idx = (k,)
        else:
            w = ref.shape[-1] // nsplit
            idx = (slice(None), slice(k * w, (k + 1) * w))
        if accumulate:
            ref[idx] += val[k].astype(ref.dtype)
        else:
            ref[idx] = val[k].astype(ref.dtype)


def _row_spec(tr, op):
    _, c0, w, _, r0 = op
    assert r0 % tr == 0
    return pl.BlockSpec((tr, w), lambda i, cb=c0 // w, rb=r0 // tr: (i + rb, cb))


def _full_spec(arr):
    return pl.BlockSpec(arr.shape, lambda i, nd=arr.ndim: (0,) * nd)


def _rw_fwd(name, f, rows, pars, outs, tr, n_rows=None):
    n_rows = rows[0][0].shape[0] if n_rows is None else n_rows
    nr, npar = len(rows), len(pars)

    def body(*refs):
        r = [_ld(refs[i], rows[i][3]) for i in range(nr)]
        p = [_ld(refs[nr + i], pars[i][1]) for i in range(npar)]
        res = f(r, p)
        for k, (_, _, ns) in enumerate(outs):
            _st(refs[nr + npar + k], res[k], ns)

    res = _pc(
        body, name=name, grid=(n_rows // tr,),
        in_specs=[_row_spec(tr, op) for op in rows] + [_full_spec(a) for a, _ in pars],
        out_specs=[pl.BlockSpec((tr, w), lambda i: (i, 0)) for w, _, _ in outs],
        out_shape=[jax.ShapeDtypeStruct((n_rows, w), dt) for w, dt, _ in outs],
        sem=("arbitrary",),
    )(*[op[0] for op in rows], *[a for a, _ in pars])
    return list(res)


def _rw_bwd(name, f, rows, pars, douts, drow, dpar, tr):
    n_rows = rows[0][0].shape[0]
    nr, npar = len(rows), len(pars)
    dgiven = [d for d in douts if d is not None]
    nd = len(dgiven)

    def body(*refs):
        r = [_ld(refs[i], rows[i][3]) for i in range(nr)]
        p = [_ld(refs[nr + i], pars[i][1]) for i in range(npar)]
        d = [_ld(refs[nr + npar + i], dgiven[i][3]) for i in range(nd)]
        orefs = refs[nr + npar + nd:]

        def g(dr, dp):
            rr, pp = list(r), list(p)
            for j, (idx, _) in enumerate(drow):
                rr[idx] = dr[j]
            for j, idx in enumerate(dpar):
                pp[idx] = dp[j]
            return tuple(f(rr, pp))

        out, vjp = jax.vjp(g, [r[idx] for idx, _ in drow], [p[idx] for idx in dpar])
        ct, j = [], 0
        for k, o in enumerate(out):
            if douts[k] is None:
                ct.append(jax.tree.map(jnp.zeros_like, o))
            else:
                ct.append(d[j])
                j += 1
        gdr, gdp = vjp(tuple(ct))
        for j, (idx, _) in enumerate(drow):
            _st(orefs[j], gdr[j], rows[idx][3])

        @pl.when(pl.program_id(0) == 0)
        def _():
            for j in range(len(dpar)):
                orefs[len(drow) + j][...] = jnp.zeros_like(orefs[len(drow) + j])

        for j, idx in enumerate(dpar):
            _st(orefs[len(drow) + j], gdp[j], pars[idx][1], accumulate=True)

    res = _pc(
        body, name=name, grid=(n_rows // tr,),
        in_specs=[_row_spec(tr, op) for op in rows] + [_full_spec(a) for a, _ in pars] + [_row_spec(tr, op) for op in dgiven],
        out_specs=[pl.BlockSpec((tr, rows[idx][2]), lambda i: (i, 0)) for idx, _ in drow] + [_full_spec(pars[idx][0]) for idx in dpar],
        out_shape=[jax.ShapeDtypeStruct((n_rows, rows[idx][2]), dt) for idx, dt in drow]
        + [jax.ShapeDtypeStruct(pars[idx][0].shape, F32) for idx in dpar],
        sem=("arbitrary",),
    )(*[op[0] for op in rows], *[a for a, _ in pars], *[op[0] for op in dgiven])
    res = list(res)
    return res[:len(drow)], res[len(drow):]


def _sigmoid(x):
    return jax.nn.sigmoid(x)


def _silu(x):
    return x * _sigmoid(x)


def _rms(x, g):
    return x * lax.rsqrt(jnp.mean(x * x, axis=-1, keepdims=True) + EPS) * g


def _f_rms(r, p):
    return [_rms(r[0], p[0])]


def _f_rms_res(r, p):
    return [r[0], _rms(r[0], p[0])]


def _f_cast(r, p):
    return [r[0]]


def _f_gate_ab(r, p):
    o, ga, mixed, gb = r
    return [(o * _silu(ga), mixed * p[0] * _silu(gb))]


def _f_sgu(r, p):
    u, v, gc = r
    lg, lb, w, b = p
    n = float(D_MODEL)
    mu = sum(jnp.sum(vk, axis=-1, keepdims=True) for vk in v) / n
    var = sum(jnp.sum(jnp.square(vk - mu), axis=-1, keepdims=True) for vk in v) / n
    rs = lax.rsqrt(var + EPS)
    t = w[0].shape[0]
    tri = lax.broadcasted_iota(jnp.int32, (t, t), 0) >= lax.broadcasted_iota(jnp.int32, (t, t), 1)
    outs = []
    for k in range(len(v)):
        vn = (v[k] - mu) * rs * lg[k] + lb[k]
        mixed = jnp.dot(jnp.where(tri, w[k], 0.0), vn, preferred_element_type=F32) + b[k]
        outs.append(u[k] * mixed * _silu(gc[k]))
    return [tuple(outs)]


def _gelu(x):
    return 0.5 * x * (1.0 + jnp.tanh(math.sqrt(2.0 / math.pi) * (x + 0.044715 * (x * x * x))))


def _f_gelu_y(r, p):
    yc, uf = r
    return [_gelu(yc + p[0] * uf)]


def _f_glu_gate(r, p):
    t12, gd = r
    return [t12[0] * _sigmoid(t12[1]) * _silu(gd)]


def _f_s5_prep(r, p):
    ar, ai, ldt = r
    dt = jnp.exp(ldt)
    mag = jnp.exp(dt * ar)
    abar_re = mag * jnp.cos(dt * ai)
    abar_im = mag * jnp.sin(dt * ai)
    nr, ni = abar_re - 1.0, abar_im
    inv = 1.0 / (ar * ar + ai * ai)
    return [abar_re, abar_im, (nr * ar + ni * ai) * inv, (ni * ar - nr * ai) * inv]


def _f_bbar(r, p):
    br, bi, cr, ci = r
    return [cr * br - ci * bi, cr * bi + ci * br]


def _loss_head(x, target, g):
    tr = 256
    n_rows, width = x.shape

    def f(xv, gv, tv):
        err = jnp.square(_rms(xv, gv) - tv)
        return 0.5 * jnp.mean(err, axis=-1, keepdims=True)

    def body(x_ref, t_ref, g_ref, loss_ref, dx_ref, dg_ref):
        @pl.when(pl.program_id(0) == 0)
        def _():
            loss_ref[...] = jnp.zeros_like(loss_ref)
            dg_ref[...] = jnp.zeros_like(dg_ref)

        tv = t_ref[...]
        row_loss, vjp = jax.vjp(lambda a, b: f(a, b, tv), x_ref[...], g_ref[...])
        dx, dg = vjp(jnp.ones_like(row_loss))
        dx_ref[...] = dx
        dg_ref[...] += dg
        loss_ref[...] += jnp.broadcast_to(jnp.sum(row_loss, axis=0, keepdims=True), loss_ref.shape)

    blk = pl.BlockSpec((tr, width), lambda i: (i, 0))
    one = pl.BlockSpec((1, width), lambda i: (0, 0))
    return _pc(
        body, name="loss_head", grid=(n_rows // tr,), in_specs=[blk, blk, one],
        out_specs=[pl.BlockSpec((1, 128), lambda i: (0, 0)), blk, one],
        out_shape=[jax.ShapeDtypeStruct((1, 128), F32), jax.ShapeDtypeStruct(x.shape, F32), jax.ShapeDtypeStruct((1, width), F32)],
        sem=("arbitrary",),
    )(x, target, g)


_NT = (((1,), (1,)), ((), ()))
_TN = (((0,), (0,)), ((), ()))


def _tile(n, cap):
    t = min(n, cap)
    while n % t:
        t -= 128
    assert t > 0
    return t


def _mm(name, a, b, *, ta=False, tb=False, out_dtype=F32, a_off=0, a_width=None, res=None, out_stack=None):
    assert not (ta and tb)
    stacked = b.ndim == 3
    bk, bn = (b.shape[1], b.shape[0] * b.shape[2]) if stacked else b.shape
    if ta:
        kc = a.shape[0]
        m = a.shape[1] - a_off if a_width is None else a_width
        n = bn
        assert bk == kc and not stacked
    else:
        m = a.shape[0]
        kc = a.shape[1] - a_off if a_width is None else a_width
        n = bk if tb else bn
        assert (bn if tb else bk) == kc
    tm = _tile(m, 512)
    tn = _tile(b.shape[2] if stacked and not tb else (out_stack or n), 1024)
    tk = _tile(b.shape[2] if stacked and tb else kc, 1024)
    nk = kc // tk
    if ta:
        assert a_off % tm == 0
        a_spec = pl.BlockSpec((tk, tm), lambda i, j, k, o=a_off // tm: (k, i + o))
        dims = _TN
    else:
        assert a_off % tk == 0
        a_spec = pl.BlockSpec((tm, tk), lambda i, j, k, o=a_off // tk: (i, k + o))
        dims = _NT if tb else (((1,), (0,)), ((), ()))
    if stacked and tb:
        b_spec = pl.BlockSpec((None, tn, tk), lambda i, j, k, q=b.shape[2] // tk: (k // q, j, k % q))
    elif stacked:
        b_spec = pl.BlockSpec((None, tk, tn), lambda i, j, k, q=b.shape[2] // tn: (j // q, k, j % q))
    else:
        b_spec = pl.BlockSpec((tn, tk), lambda i, j, k: (j, k)) if tb else pl.BlockSpec((tk, tn), lambda i, j, k: (k, j))
    if out_stack:
        out_spec = pl.BlockSpec((None, tm, tn), lambda i, j, k, q=out_stack // tn: (j // q, i, j % q))
        out_shape = jax.ShapeDtypeStruct((n // out_stack, m, out_stack), out_dtype)
    else:
        out_spec = pl.BlockSpec((tm, tn), lambda i, j, k: (i, j))
        out_shape = jax.ShapeDtypeStruct((m, n), out_dtype)
    in_specs, args = [a_spec, b_spec], [a, b]
    has_res = res is not None
    if has_res:
        in_specs.append(pl.BlockSpec((tm, tn), lambda i, j, k: (i, j)))
        args.append(res)

    def body(*refs):
        a_ref, b_ref = refs[0], refs[1]
        o_ref, acc_ref = refs[-2], refs[-1]
        k = pl.program_id(2)

        @pl.when(k == 0)
        def _():
            acc_ref[...] = jnp.zeros_like(acc_ref)

        acc_ref[...] += lax.dot_general(a_ref[...].astype(BF16), b_ref[...].astype(BF16), dims, preferred_element_type=F32)

        @pl.when(k == nk - 1)
        def _():
            acc = acc_ref[...]
            if has_res:
                acc = acc + refs[2][...].astype(F32)
            o_ref[...] = acc.astype(o_ref.dtype)

    return _pc(
        body, name=name, grid=(m // tm, n // tn, nk), in_specs=in_specs, out_specs=out_spec, out_shape=out_shape,
        scratch_shapes=[pltpu.VMEM((tm, tn), F32)], sem=("parallel", "parallel", "arbitrary"),
    )(*args)


def _head_masks(width, nsub):
    lane = lax.broadcasted_iota(jnp.int32, (1, width), 1)
    hd = width // nsub
    return [(lane >= h * hd) & (lane < (h + 1) * hd) for h in range(nsub)]


def _dilated_log_count(row0, tq, ext):
    delta = (row0 + lax.broadcasted_iota(jnp.int32, (tq, ext), 0)) - lax.broadcasted_iota(jnp.int32, (tq, ext), 1)
    cnt = (delta <= 128).astype(jnp.int32) + (((delta & 3) == 0) & (delta <= 512)).astype(jnp.int32) + ((delta & 15) == 0).astype(jnp.int32)
    logc = jnp.where(cnt == 3, math.log(3.0), jnp.where(cnt == 2, math.log(2.0), 0.0))
    return jnp.where((delta >= 0) & (cnt > 0), logc, NEG)


def _softmax_rows(s):
    m = jnp.max(s, axis=-1, keepdims=True)
    p = jnp.exp(s - m)
    return p / jnp.sum(p, axis=-1, keepdims=True)


def _attn_fwd(name, qa, ka, va, *, qc, kc, vc, width, nblk, nsub, causal, tq, scale, out_dtype):
    sq, t_len = qa.shape[0], ka.shape[0]

    def body(q_ref, k_ref, v_ref, o_ref):
        kb = k_ref[...].astype(BF16)
        vb = v_ref[...].astype(BF16)
        masks = _head_masks(width, nsub)
        for r in range(sq // tq):
            ext = (r + 1) * tq if causal else t_len
            q = q_ref[r * tq:(r + 1) * tq, :].astype(F32)
            ke, ve = kb[:ext], vb[:ext]
            bias = _dilated_log_count(r * tq, tq, ext) if causal else None
            o = None
            for h in range(nsub):
                qm = (jnp.where(masks[h], q, 0.0) if nsub > 1 else q).astype(BF16)
                s = lax.dot_general(qm, ke, _NT, preferred_element_type=F32) * scale
                if causal:
                    s = jnp.where(bias > 0.5 * NEG, s + bias, NEG)
                pn = _softmax_rows(s).astype(BF16)
                oh = jnp.dot(pn, ve, preferred_element_type=F32)
                o = oh if o is None else jnp.where(masks[h], oh, o)
            o_ref[r * tq:(r + 1) * tq, :] = o.astype(o_ref.dtype)

    return _pc(
        body, name=name, grid=(nblk,),
        in_specs=[pl.BlockSpec((sq, width), lambda i, c=qc: (0, c + i)), pl.BlockSpec((t_len, width), lambda i, c=kc: (0, c + i)),
                  pl.BlockSpec((t_len, width), lambda i, c=vc: (0, c + i))],
        out_specs=pl.BlockSpec((sq, width), lambda i: (0, i)),
        out_shape=jax.ShapeDtypeStruct((sq, nblk * width), out_dtype), sem=("parallel",),
    )(qa, ka, va)


def _attn_bwd(name, qa, ka, va, doa, *, qc, kc, vc, width, nblk, nsub, causal, tq, scale, out_dtype):
    sq, t_len = qa.shape[0], ka.shape[0]

    def body(q_ref, k_ref, v_ref, do_ref, dq_ref, dk_ref, dv_ref, dk_acc, dv_acc):
        kb = k_ref[...].astype(BF16)
        vb = v_ref[...].astype(BF16)
        masks = _head_masks(width, nsub)
        dk_acc[...] = jnp.zeros_like(dk_acc)
        dv_acc[...] = jnp.zeros_like(dv_acc)
        for r in range(sq // tq):
            ext = (r + 1) * tq if causal else t_len
            q = q_ref[r * tq:(r + 1) * tq, :].astype(F32)
            do = do_ref[r * tq:(r + 1) * tq, :].astype(F32)
            ke, ve = kb[:ext], vb[:ext]
            bias = _dilated_log_count(r * tq, tq, ext) if causal else None
            dq = None
            for h in range(nsub):
                qm = (jnp.where(masks[h], q, 0.0) if nsub > 1 else q).astype(BF16)
                dom = (jnp.where(masks[h], do, 0.0) if nsub > 1 else do).astype(BF16)
                s = lax.dot_general(qm, ke, _NT, preferred_element_type=F32) * scale
                if causal:
                    s = jnp.where(bias > 0.5 * NEG, s + bias, NEG)
                pn = _softmax_rows(s)
                dpn = lax.dot_general(dom, ve, _NT, preferred_element_type=F32)
                ds = pn * (dpn - jnp.sum(pn * dpn, axis=-1, keepdims=True))
                dsb = (ds * scale).astype(BF16)
                dqh = jnp.dot(dsb, ke, preferred_element_type=F32)
                dq = dqh if dq is None else jnp.where(masks[h], dqh, dq)
                dk_acc[0:ext, :] += lax.dot_general(dsb, qm, _TN, preferred_element_type=F32)
                dv_acc[0:ext, :] += lax.dot_general(pn.astype(BF16), dom, _TN, preferred_element_type=F32)
            dq_ref[r * tq:(r + 1) * tq, :] = dq.astype(dq_ref.dtype)
        dk_ref[...] = dk_acc[...].astype(dk_ref.dtype)
        dv_ref[...] = dv_acc[...].astype(dv_ref.dtype)

    return _pc(
        body, name=name, grid=(nblk,),
        in_specs=[pl.BlockSpec((sq, width), lambda i, c=qc: (0, c + i)), pl.BlockSpec((t_len, width), lambda i, c=kc: (0, c + i)),
                  pl.BlockSpec((t_len, width), lambda i, c=vc: (0, c + i)), pl.BlockSpec((sq, width), lambda i: (0, i))],
        out_specs=[pl.BlockSpec((sq, width), lambda i: (0, i)), pl.BlockSpec((t_len, width), lambda i: (0, i)), pl.BlockSpec((t_len, width), lambda i: (0, i))],
        out_shape=[jax.ShapeDtypeStruct((sq, nblk * width), out_dtype), jax.ShapeDtypeStruct((t_len, nblk * width), out_dtype),
                   jax.ShapeDtypeStruct((t_len, nblk * width), out_dtype)],
        scratch_shapes=[pltpu.VMEM((t_len, width), F32), pltpu.VMEM((t_len, width), F32)], sem=("parallel",),
    )(qa, ka, va, doa)


_SELF = dict(qc=0, kc=8, vc=16, width=128, nblk=8, nsub=2, causal=True, tq=256, scale=A_HEAD_DIM ** -0.5)
_CROSS = dict(qc=0, kc=0, vc=4, width=256, nblk=4, nsub=1, causal=False, tq=512, scale=X_HEAD_DIM ** -0.5)


def _window_sum(x, g, row, backward):
    n = x.shape[0]

    def shift(y, k):
        if backward:
            return jnp.where(row < n - k, pltpu.roll(y, n - k, 0), 0.0)
        return jnp.where(row >= k, pltpu.roll(y, k, 0), 0.0)

    s2 = x + shift(x, 1)
    s4 = s2 + shift(s2, 2)
    s8 = s4 + shift(s4, 4)
    s16 = s8 + shift(s8, 8)
    return jnp.where(g == 0, s2, jnp.where(g == 1, s4, jnp.where(g == 2, s8, s16)))


def _pool(name, arr, c0, backward, out_dtype):
    n = arr.shape[0]
    gw = 256

    def body(v_ref, o_ref):
        g = pl.program_id(0)
        v = v_ref[...].astype(F32)
        row = lax.broadcasted_iota(jnp.int32, v.shape, 0)
        w = jnp.where(g == 0, 2, jnp.where(g == 1, 4, jnp.where(g == 2, 8, 16)))
        cnt = jnp.minimum(row + 1, w).astype(F32)
        if backward:
            o_ref[...] = (_window_sum(v / cnt, g, row, True) - v).astype(o_ref.dtype)
        else:
            o_ref[...] = (_window_sum(v, g, row, False) / cnt - v).astype(o_ref.dtype)

    return _pc(
        body, name=name, grid=(4,), in_specs=[pl.BlockSpec((n, gw), lambda i, c=c0 // gw: (0, c + i))],
        out_specs=pl.BlockSpec((n, gw), lambda i: (0, i)), out_shape=jax.ShapeDtypeStruct((n, 4 * gw), out_dtype), sem=("parallel",),
    )(arr)


_SCAN_ROWS = 256


def _scan_fwd(bu3, a2):
    n = bu3.shape[0]

    def body(bu_ref, a_ref, h_ref, carry):
        @pl.when(pl.program_id(0) == 0)
        def _():
            carry[...] = jnp.zeros_like(carry)

        ar, ai = a_ref[0:16, :], a_ref[16:32, :]

        def step(t, c):
            hr, hi = c
            nr = ar * hr - ai * hi + bu_ref[t, 0:16, :]
            ni = ar * hi + ai * hr + bu_ref[t, 16:32, :]
            h_ref[t, 0:16, :] = nr
            h_ref[t, 16:32, :] = ni
            return nr, ni

        hr, hi = lax.fori_loop(0, _SCAN_ROWS, step, (carry[0:16, :], carry[16:32, :]), unroll=8)
        carry[0:16, :] = hr
        carry[16:32, :] = hi

    blk = pl.BlockSpec((_SCAN_ROWS, 32, 128), lambda i: (i, 0, 0))
    return _pc(
        body, name="s5_scan_fwd", grid=(n // _SCAN_ROWS,), in_specs=[blk, pl.BlockSpec((32, 128), lambda i: (0, 0))], out_specs=blk,
        out_shape=jax.ShapeDtypeStruct(bu3.shape, F32), scratch_shapes=[pltpu.VMEM((32, 128), F32)], sem=("arbitrary",),
    )(bu3, a2)


def _scan_bwd(dh3, h3, a2):
    n = dh3.shape[0]
    nb = n // _SCAN_ROWS

    def body(dh_ref, h_ref, a_ref, dbu_ref, da_ref, carry):
        @pl.when(pl.program_id(0) == 0)
        def _():
            carry[...] = jnp.zeros_like(carry)
            da_ref[...] = jnp.zeros_like(da_ref)

        ar, ai = a_ref[0:16, :], a_ref[16:32, :]

        def step(tt, c):
            gr, gi, dar, dai = c
            t = _SCAN_ROWS - 1 - tt
            hr, hi = h_ref[t, 0:16, :], h_ref[t, 16:32, :]
            dar = dar + gr * hr + gi * hi
            dai = dai - gr * hi + gi * hr
            ngr = dh_ref[t, 0:16, :] + ar * gr + ai * gi
            ngi = dh_ref[t, 16:32, :] - ai * gr + ar * gi
            dbu_ref[t, 0:16, :] = ngr
            dbu_ref[t, 16:32, :] = ngi
            return ngr, ngi, dar, dai

        z = jnp.zeros((16, 128), F32)
        gr, gi, dar, dai = lax.fori_loop(0, _SCAN_ROWS, step, (carry[0:16, :], carry[16:32, :], z, z), unroll=8)
        carry[0:16, :] = gr
        carry[16:32, :] = gi
        da_ref[0:16, :] += dar
        da_ref[16:32, :] += dai

    blk = pl.BlockSpec((_SCAN_ROWS, 32, 128), lambda i: (nb - 1 - i, 0, 0))
    small = pl.BlockSpec((32, 128), lambda i: (0, 0))
    return _pc(
        body, name="s5_scan_bwd", grid=(nb,), in_specs=[blk, blk, small], out_specs=[blk, small],
        out_shape=[jax.ShapeDtypeStruct(dh3.shape, F32), jax.ShapeDtypeStruct((32, 128), F32)],
        scratch_shapes=[pltpu.VMEM((32, 128), F32)], sem=("arbitrary",),
    )(dh3, h3, a2)


def _bdense(bb_re, bb_im):
    eye = jnp.eye(S5_GROUPS, dtype=F32)

    def one(bb):
        return jnp.einsum("gph,gk->ghkp", bb.reshape(S5_GROUPS, S5_STATE, S5_GROUP_DIM), eye).reshape(512, 2048)

    return jnp.concatenate([one(bb_re), one(bb_im)], axis=1)


def _cdense(c_re, c_im):
    eye = jnp.eye(S5_GROUPS, dtype=F32)

    def one(cc):
        return jnp.einsum("ghp,gk->gpkh", cc, eye).reshape(2048, 512)

    return jnp.concatenate([one(c_re), -one(c_im)], axis=0)


def _pool_dense(pw):
    eye = jnp.eye(4, dtype=pw.dtype)
    return jnp.einsum("gcd,gk->gckd", pw, eye).reshape(1024, 1024)


def _row2(v):
    return v.reshape(1, -1)


def _even_fwd(x, W, i, zero, rest_of_weights):
    hn = _rw_fwd("rms_fwd", _f_rms, [_cols(x)], [_par(_row2(W["norm_ab"][i]) + zero)], [(D_MODEL, BF16, 1)], 256)[0]
    z = _mm("mm_in_ab", hn, W["w_in_ab"][i])
    o = _attn_fwd("attn_self_fwd", z, z, z, out_dtype=F32, **_SELF)
    zero = rest_of_weights(o)
    pooled = _pool("pool_fwd", z, 4096, False, BF16)
    wp = _pool_dense(W["pool_w"][i])
    mixed = _mm("mm_pool", pooled, wp)
    scale = _row2(W["pool_scale"][i]) + zero
    ab = _rw_fwd("gate_ab_fwd", _f_gate_ab, [_cols(o), _cols(z, 3072, 1024), _cols(mixed), _cols(z, 5120, 1024)], [_par(scale)],
                 [(2048, BF16, 2)], 256)[0]
    x1 = _mm("mm_out_ab", ab, W["w_out_ab"][i], res=x)
    return x1, dict(x=x, hn=hn, z=z, o=o, pooled=pooled, wp=wp, mixed=mixed, ab=ab)


def _even_bwd(dx1, sv, W, G, i, before_last):
    x, hn, z = sv["x"], sv["hn"], sv["z"]
    dab = _mm("mm_out_ab_dx", dx1, W["w_out_ab"][i], tb=True)
    G["w_out_ab"][i] = _mm("mm_out_ab_dw", sv["ab"], dx1, ta=True, out_dtype=BF16)
    scale = _row2(W["pool_scale"][i])
    (do, dga, dmixed, dgb), (dscale,) = _rw_bwd(
        "gate_ab_bwd", _f_gate_ab, [_cols(sv["o"]), _cols(z, 3072, 1024), _cols(sv["mixed"]), _cols(z, 5120, 1024)], [_par(scale)],
        [_cols(dab, nsplit=2)], [(0, F32), (1, BF16), (2, BF16), (3, BF16)], [0], 256)
    G["pool_scale"][i] = dscale.reshape(-1)
    dpooled = _mm("mm_pool_dx", dmixed, sv["wp"], tb=True)
    dwp = _mm("mm_pool_dw", sv["pooled"], dmixed, ta=True, out_dtype=BF16)
    G["pool_w"][i] = jnp.stack([dwp[g * 256:(g + 1) * 256, g * 256:(g + 1) * 256] for g in range(4)])
    dvb = _pool("pool_bwd", dpooled, 0, True, BF16)
    dq, dk, dv = _attn_bwd("attn_self_bwd", z, z, z, do, out_dtype=BF16, **_SELF)
    dz = jnp.concatenate([dq, dk, dv, dga, dvb, dgb], axis=1)
    dhn = _mm("mm_in_ab_dx", dz, W["w_in_ab"][i], tb=True)
    G["w_in_ab"][i] = _mm("mm_in_ab_dw", hn, dz, ta=True, out_dtype=BF16, out_stack=W["w_in_ab"][i].shape[2])
    g = _row2(W["norm_ab"][i]) + before_last()
    (dx,), (dg,) = _rw_bwd("rms_bwd", _f_rms_res, [_cols(x)], [_par(g)], [_cols(dx1), _cols(dhn)], [(0, F32)], [0], 256)
    G["norm_ab"][i] = dg.reshape(-1)
    return dx


def _odd_fwd(x, W, i, zero):
    hn = _rw_fwd("rms_fwd", _f_rms, [_cols(x)], [_par(_row2(W["norm_cd"][i]) + zero)], [(D_MODEL, BF16, 1)], 256)[0]
    z = _mm("mm_in_cd", hn, W["w_in_cd"][i])
    sgu_p = [_par(_row2(W["sgu_ln_g"][i]), 4), _par(_row2(W["sgu_ln_b"][i]), 4), _par(W["sgu_w"][i], 4), _par(W["sgu_b"][i][..., None], 4)]
    c_out = _rw_fwd("sgu_fwd", _f_sgu, [_cols(z, 0, 1024, 4), _cols(z, 1024, 1024, 4), _cols(z, 2048, 1024, 4)], sgu_p, [(1024, BF16, 4)], 128)[0]
    prep_rows = [_cols(W["s5_a_re"][i]), _cols(W["s5_a_im"][i]), _cols(W["s5_log_dt"][i].reshape(S5_GROUPS, 1))]
    abar_re, abar_im, coef_re, coef_im = _rw_fwd("s5_prep_fwd", _f_s5_prep, prep_rows, [], [(S5_STATE, F32, 1)] * 4, S5_GROUPS)
    bb_rows = [_cols(W["s5_b_re"][i].reshape(2048, 16)), _cols(W["s5_b_im"][i].reshape(2048, 16)), _cols(coef_re.reshape(2048, 1)), _cols(coef_im.reshape(2048, 1))]
    bb_re, bb_im = _rw_fwd("s5_bbar_fwd", _f_bbar, bb_rows, [], [(16, F32, 1)] * 2, 256)
    bd = _bdense(bb_re, bb_im).astype(BF16)
    cf = _cdense(W["s5_c_re"][i], W["s5_c_im"][i]).astype(BF16)
    a2 = jnp.concatenate([abar_re.reshape(16, 128), abar_im.reshape(16, 128)], axis=0)
    bu = _mm("mm_s5_bu", z, bd, a_off=3072, a_width=512)
    h3 = _scan_fwd(bu.reshape(SEQ, 32, 128), a2)
    h2 = h3.reshape(SEQ, 4096)
    yc = _mm("mm_s5_y", h2, cf)
    dpar = _row2(W["s5_d"][i])
    yg = _rw_fwd("gelu_fwd", _f_gelu_y, [_cols(yc), _cols(z, 3072, 512)], [_par(dpar)], [(512, BF16, 1)], 256)[0]
    w12 = jnp.concatenate([W["glu_w1"][i], W["glu_w2"][i]], axis=1)
    t12 = _mm("mm_glu", yg, w12)
    d_out = _rw_fwd("glu_gate_fwd", _f_glu_gate, [_cols(t12, nsplit=2), _cols(z, 3584, 512)], [], [(512, BF16, 1)], 256)[0]
    cd = jnp.concatenate([c_out, d_out], axis=1)
    x1 = _mm("mm_out_cd", cd, W["w_out_cd"][i], res=x)
    sv = dict(x=x, hn=hn, z=z, sgu_p=sgu_p, prep_rows=prep_rows, bb_rows=bb_rows, bb=(bb_re, bb_im), bd=bd, cf=cf, a2=a2,
              h3=h3, h2=h2, yc=yc, dpar=dpar, yg=yg, w12=w12, t12=t12, cd=cd)
    return x1, sv


def _odd_bwd(dx1, sv, W, G, i, before_last):
    x, hn, z = sv["x"], sv["hn"], sv["z"]
    dcd = _mm("mm_out_cd_dx", dx1, W["w_out_cd"][i], tb=True)
    G["w_out_cd"][i] = _mm("mm_out_cd_dw", sv["cd"], dx1, ta=True, out_dtype=BF16)
    (du, dv, dgc), (dlg, dlb, dsw, dsb) = _rw_bwd(
        "sgu_bwd", _f_sgu, [_cols(z, 0, 1024, 4), _cols(z, 1024, 1024, 4), _cols(z, 2048, 1024, 4)], sv["sgu_p"],
        [_cols(dcd, 0, 1024, 4)], [(0, BF16), (1, BF16), (2, BF16)], [0, 1, 2, 3], 128)
    G["sgu_ln_g"][i], G["sgu_ln_b"][i] = dlg.reshape(-1), dlb.reshape(-1)
    G["sgu_w"][i], G["sgu_b"][i] = dsw, dsb[..., 0]
    (dt12, dgd), _ = _rw_bwd("glu_gate_bwd", _f_glu_gate, [_cols(sv["t12"], nsplit=2), _cols(z, 3584, 512)], [], [_cols(dcd, 1024, 512)],
                             [(0, BF16), (1, BF16)], [], 256)
    dyg = _mm("mm_glu_dx", dt12, sv["w12"], tb=True)
    dw12 = _mm("mm_glu_dw", sv["yg"], dt12, ta=True, out_dtype=BF16)
    G["glu_w1"][i], G["glu_w2"][i] = dw12[:, :512], dw12[:, 512:]
    (dyc, duf1), (dd,) = _rw_bwd("gelu_bwd", _f_gelu_y, [_cols(sv["yc"]), _cols(z, 3072, 512)], [_par(sv["dpar"])], [_cols(dyg)],
                                 [(0, BF16), (1, F32)], [0], 256)
    G["s5_d"][i] = dd.reshape(-1)
    dh2 = _mm("mm_s5_y_dx", dyc, sv["cf"], tb=True)
    dcf = _mm("mm_s5_y_dw", sv["h2"], dyc, ta=True)
    _, cvjp = jax.vjp(_cdense, W["s5_c_re"][i], W["s5_c_im"][i])
    G["s5_c_re"][i], G["s5_c_im"][i] = cvjp(dcf)
    dbu3, da2 = _scan_bwd(dh2.reshape(SEQ, 32, 128), sv["h3"], sv["a2"])
    dbu = dbu3.reshape(SEQ, 4096)
    duf2 = _mm("mm_s5_bu_dx", dbu, sv["bd"], tb=True)
    dbd = _mm("mm_s5_bu_dw", z, dbu, ta=True, a_off=3072, a_width=512)
    _, bvjp = jax.vjp(_bdense, *sv["bb"])
    dbb_re, dbb_im = bvjp(dbd)
    (dbr, dbi, dcr, dci), _ = _rw_bwd("s5_bbar_bwd", _f_bbar, sv["bb_rows"], [], [_cols(dbb_re), _cols(dbb_im)],
                                      [(0, F32), (1, F32), (2, F32), (3, F32)], [], 256)
    G["s5_b_re"][i], G["s5_b_im"][i] = dbr.reshape(S5_GROUPS, S5_STATE, S5_GROUP_DIM), dbi.reshape(S5_GROUPS, S5_STATE, S5_GROUP_DIM)
    douts = [_cols(da2[0:16].reshape(S5_GROUPS, S5_STATE)), _cols(da2[16:32].reshape(S5_GROUPS, S5_STATE)),
             _cols(dcr.reshape(S5_GROUPS, S5_STATE)), _cols(dci.reshape(S5_GROUPS, S5_STATE))]
    (dar, dai, dldt), _ = _rw_bwd("s5_prep_bwd", _f_s5_prep, sv["prep_rows"], [], douts, [(0, F32), (1, F32), (2, F32)], [], S5_GROUPS)
    G["s5_a_re"][i], G["s5_a_im"][i], G["s5_log_dt"][i] = dar, dai, dldt.reshape(-1)
    dxd = (duf1 + duf2).astype(BF16)
    dz = jnp.concatenate([du, dv, dgc, dxd, dgd], axis=1)
    dhn = _mm("mm_in_cd_dx", dz, W["w_in_cd"][i], tb=True)
    G["w_in_cd"][i] = _mm("mm_in_cd_dw", hn, dz, ta=True, out_dtype=BF16, out_stack=W["w_in_cd"][i].shape[2])
    g = _row2(W["norm_cd"][i]) + before_last()
    (dx,), (dg,) = _rw_bwd("rms_bwd", _f_rms_res, [_cols(x)], [_par(g)], [_cols(dx1), _cols(dhn)], [(0, F32)], [0], 256)
    G["norm_cd"][i] = dg.reshape(-1)
    return dx


def _cross_fwd(x1, mem_n, W, l):
    hx = _rw_fwd("rms_fwd", _f_rms, [_cols(x1)], [_par(_row2(W["norm_x"][l]))], [(D_MODEL, BF16, 1)], 256)[0]
    qx = _mm("mm_xq", hx, W["w_xq"][l], out_dtype=BF16)
    kv = _mm("mm_xkv", mem_n, W["w_xkv"][l], out_dtype=BF16)
    ox = _attn_fwd("attn_cross_fwd", qx, kv, kv, out_dtype=BF16, **_CROSS)
    x2 = _mm("mm_xo", ox, W["w_xo"][l], res=x1)
    return x2, dict(x1=x1, hx=hx, qx=qx, kv=kv, ox=ox)


def _cross_bwd(dx2, dmem_n, sv, mem_n, W, G, l, zero):
    dox = _mm("mm_xo_dx", dx2, W["w_xo"][l], tb=True, out_dtype=BF16)
    G["w_xo"][l] = _mm("mm_xo_dw", sv["ox"], dx2, ta=True, out_dtype=BF16)
    dqx, dk, dv = _attn_bwd("attn_cross_bwd", sv["qx"], sv["kv"], sv["kv"], dox, out_dtype=BF16, **_CROSS)
    dkv = jnp.concatenate([dk, dv], axis=1)
    dhx = _mm("mm_xq_dx", dqx, W["w_xq"][l], tb=True)
    G["w_xq"][l] = _mm("mm_xq_dw", sv["hx"], dqx, ta=True, out_dtype=BF16)
    dmem_n = _mm("mm_xkv_dx", dkv, W["w_xkv"][l], tb=True, res=dmem_n)
    G["w_xkv"][l] = _mm("mm_xkv_dw", mem_n, dkv, ta=True, out_dtype=BF16, out_stack=W["w_xkv"][l].shape[2])
    (dx1,), (dg,) = _rw_bwd("rms_bwd", _f_rms_res, [_cols(sv["x1"])], [_par(_row2(W["norm_x"][l]) + zero)], [_cols(dx2), _cols(dhx)], [(0, F32)], [0], 256)
    G["norm_x"][l] = dg.reshape(-1)
    return dx1, dmem_n


_PER_LAYER = ("pool_scale", "norm_ab", "norm_cd", "sgu_ln_g", "sgu_ln_b", "sgu_w", "sgu_b", "s5_d", "s5_c_re", "s5_c_im", "s5_b_re", "s5_b_im",
              "s5_a_re", "s5_a_im", "s5_log_dt", "w_in_ab", "pool_w", "w_out_ab", "w_in_cd", "glu_w1", "glu_w2", "w_out_cd")


def _local_step(x, mem, target, W, weights_of, before_last, after_layer):
    G = {k: [None, None] for k in _PER_LAYER}
    for k in ("norm_x", "w_xq", "w_xkv", "w_xo"):
        G[k] = [None] * DEPTH
    mem_rows = [_cols(mem)]
    mem_par = [_par(_row2(W["mem_norm"]))]
    mem_n = _rw_fwd("rms_fwd_mem", _f_rms, mem_rows, mem_par, [(D_MODEL, BF16, 1)], 256)[0]
    saved = []
    for layer in range(DEPTH):
        zero = weights_of(layer, 0, x if layer else mem_n)
        if layer % 2 == 0:
            x, sv = _even_fwd(x, W, layer // 2, zero, functools.partial(weights_of, layer, 1))
        else:
            x, sv = _odd_fwd(x, W, layer // 2, zero)
        x, svx = _cross_fwd(x, mem_n, W, layer)
        saved.append((sv, svx))
    loss, dx, dfinal = _loss_head(x, target, _row2(W["final_norm"]))
    G["final_norm"] = dfinal.reshape(-1)
    dmem_n, zero = None, 0.0
    for layer in reversed(range(DEPTH)):
        sv, svx = saved[layer]
        dx, dmem_n = _cross_bwd(dx, dmem_n, svx, mem_n, W, G, layer, zero)
        hook = functools.partial(before_last, layer, G)
        dx = _even_bwd(dx, sv, W, G, layer // 2, hook) if layer % 2 == 0 else _odd_bwd(dx, sv, W, G, layer // 2, hook)
        zero = after_layer(layer, G)
    _, (dmn,) = _rw_bwd("rms_bwd_mem", _f_rms, mem_rows, mem_par, [_cols(dmem_n)], [], [0], 256)
    G["mem_norm"] = dmn.reshape(-1)
    return loss, dx, G


_HBM = pl.BlockSpec(memory_space=pltpu.HBM)
_ANY = pl.BlockSpec(memory_space=pl.ANY)
_SEM = pl.BlockSpec(memory_space=pltpu.SEMAPHORE)
_N_PEERS = N_DEV - 1


def _mesh_pos():
    return lax.axis_index("x"), lax.axis_index("y"), lax.axis_index("c")


def _peer(pos, k):
    x, y, c = pos
    return (x ^ ((k >> 2) & 1), y ^ ((k >> 1) & 1), c ^ (k & 1))


def _lin(pos):
    return 4 * pos[0] + 2 * pos[1] + pos[2]


def _ends(gather, srcs, lands, t, sender, receiver):
    if gather:
        return lands[t].at[sender], lands[t].at[sender]
    whole = len(srcs[t].shape) != len(lands[t].shape)
    return (srcs[t] if whole else srcs[t].at[receiver]), lands[t].at[sender]


def _into_slot(name, b2, r0, r, me, dtype, after):
    c = b2.shape[1]
    tr = _row_block(r, c, 2 << 20)
    assert r0 % tr == 0

    def body(me_ref, x_ref, *rest):
        rest[-1][...] = x_ref[...].astype(dtype)

    extra = [] if after is None else [after]
    grid_spec = pltpu.PrefetchScalarGridSpec(
        num_scalar_prefetch=1, grid=(r // tr,),
        in_specs=[pl.BlockSpec((tr, c), lambda i, me, o=r0 // tr: (o + i, 0))] + [_ANY] * len(extra),
        out_specs=pl.BlockSpec((None, tr, c), lambda i, me: (me[0], i, 0)))
    return pl.pallas_call(
        body, name=name, grid_spec=grid_spec, out_shape=jax.ShapeDtypeStruct((N_DEV, r, c), dtype),
        compiler_params=pltpu.CompilerParams(dimension_semantics=("arbitrary",), vmem_limit_bytes=V7X_VMEM_LIMIT_BYTES),
        interpret=False,
    )(me, b2, *extra)


def _exchange_start(name, gather, srcs, lands):
    ns, nt = len(srcs), len(lands)
    arrs = list(srcs) + list(lands)

    def body(*refs):
        ins, lnd = refs[:ns], refs[ns:ns + nt]
        send_sems, recv_sems = refs[ns + nt], refs[ns + nt + 1]
        token = refs[-1]
        pos = _mesh_pos()
        me = _lin(pos)
        for k in range(1, N_DEV):
            peer = _peer(pos, k)
            for t in range(nt):
                src, dst = _ends(gather, ins, lnd, t, me, _lin(peer))
                pltpu.make_async_remote_copy(
                    src_ref=src, dst_ref=dst, send_sem=send_sems.at[t * _N_PEERS + k - 1], recv_sem=recv_sems.at[t * _N_PEERS + k - 1],
                    device_id=peer, device_id_type=pl.DeviceIdType.MESH).start()
        token[...] = jnp.zeros_like(token)

    out = pl.pallas_call(
        body, name=name,
        out_shape=(pltpu.SemaphoreType.DMA((nt * _N_PEERS,)), pltpu.SemaphoreType.DMA((nt * _N_PEERS,)), *[pltpu.HBM(a.shape, a.dtype) for a in arrs],
                   jax.ShapeDtypeStruct((8, 128), F32)),
        in_specs=[_HBM] * (ns + nt), out_specs=(_SEM, _SEM, *[_HBM] * (ns + nt), pl.BlockSpec(memory_space=pltpu.VMEM)),
        input_output_aliases={j: 2 + j for j in range(ns + nt)},
        compiler_params=pltpu.CompilerParams(has_side_effects=pltpu.SideEffectType.DATAFLOW_SIDE_EFFECTING),
        interpret=False,
    )(*[pltpu.with_memory_space_constraint(a, pltpu.HBM) for a in arrs])
    return dict(send=out[0], recv=out[1], srcs=list(out[2:2 + ns]), lands=list(out[2 + ns:2 + ns + nt]), token=out[-1][0, 0], token_arr=out[-1], gather=gather)


def _exchange_wait(name, ex, after):
    ns, nt = len(ex["srcs"]), len(ex["lands"])
    gather = ex["gather"]
    arrs = ex["srcs"] + ex["lands"]

    def body(*refs):
        ins, lnd = refs[:ns], refs[ns:ns + nt]
        send_sems, recv_sems = refs[ns + nt], refs[ns + nt + 1]
        pos = _mesh_pos()
        me = _lin(pos)
        for k in range(1, N_DEV):
            peer = _peer(pos, k)
            for t in range(nt):
                src, _ = _ends(gather, ins, lnd, t, me, _lin(peer))
                _, dst = _ends(gather, ins, lnd, t, _lin(peer), me)
                cp = pltpu.make_async_remote_copy(
                    src_ref=src, dst_ref=dst, send_sem=send_sems.at[t * _N_PEERS + k - 1], recv_sem=recv_sems.at[t * _N_PEERS + k - 1],
                    device_id=peer, device_id_type=pl.DeviceIdType.MESH)
                cp.wait_send()
                cp.wait_recv()

    out = pl.pallas_call(
        body, name=name, out_shape=tuple(pltpu.HBM(a.shape, a.dtype) for a in arrs),
        in_specs=[_HBM] * (ns + nt) + [_SEM, _SEM, _ANY], out_specs=tuple([_HBM] * (ns + nt)),
        input_output_aliases={j: j for j in range(ns + nt)},
        compiler_params=pltpu.CompilerParams(has_side_effects=pltpu.SideEffectType.DATAFLOW_SIDE_EFFECTING),
        interpret=False,
    )(*arrs, ex["send"], ex["recv"], after)
    return list(out[:ns]), list(out[ns:])


def _scatter_begin(name, srcs):
    lands = [lax.empty(s.shape if s.ndim == 3 else (N_DEV,) + s.shape, s.dtype) for s in srcs]
    return _exchange_start(name, False, srcs, lands)


def _adam(name, w, m, v, parts, own, me, layer, bufs):
    r, c = parts.shape[1:]
    tr = _row_block(r, c, 1 << 20)
    nb = r // tr

    def body(me_ref, w_ref, m_ref, v_ref, p_ref, own_ref, *rest):
        g_ref, d_ref, nm_ref, nv_ref, acc = rest[-5:]
        acc[...] = jnp.zeros_like(acc)
        for k in range(N_DEV):
            @pl.when(me_ref[0] == k)
            def _():
                acc[...] += own_ref[...].astype(F32)

            @pl.when(me_ref[0] != k)
            def _(k=k):
                acc[...] += p_ref[k].astype(F32)

        g = acc[...]
        mm = ADAM_B1 * m_ref[...] + (1.0 - ADAM_B1) * g
        vv = ADAM_B2 * v_ref[...] + (1.0 - ADAM_B2) * jnp.square(g)
        m_hat = mm / (1.0 - ADAM_B1 ** ADAM_STEP)
        v_hat = vv / (1.0 - ADAM_B2 ** ADAM_STEP)
        g_ref[...] = g
        d_ref[...] = -ADAM_LR * (m_hat / (jnp.sqrt(v_hat) + ADAM_EPS) + ADAM_WD * w_ref[...])
        nm_ref[...] = mm
        nv_ref[...] = vv

    blk = pl.BlockSpec((tr, c), lambda i, me, o=layer * nb: (o + i, 0))
    own_spec = pl.BlockSpec((None, tr, c), lambda i, me: (me[0], i, 0)) if own.ndim == 3 else pl.BlockSpec((tr, c), lambda i, me: (i, 0))
    in_specs = [blk, blk, blk, pl.BlockSpec((N_DEV, tr, c), lambda i, me: (0, i, 0)), own_spec]
    args = [me, w, m, v, parts, own]
    aliases = {}
    if bufs is not None:
        in_specs += [_ANY] * 4
        aliases = {len(args) + j: j for j in range(4)}
        args += list(bufs)
    grid_spec = pltpu.PrefetchScalarGridSpec(
        num_scalar_prefetch=1, grid=(nb,), in_specs=in_specs, out_specs=[blk] * 4, scratch_shapes=[pltpu.VMEM((tr, c), F32)])
    return pl.pallas_call(
        body, name=name, grid_spec=grid_spec, out_shape=[jax.ShapeDtypeStruct(w.shape, F32)] * 4, input_output_aliases=aliases,
        compiler_params=pltpu.CompilerParams(dimension_semantics=("arbitrary",), vmem_limit_bytes=V7X_VMEM_LIMIT_BYTES),
        interpret=False,
    )(*args)


def _pack(arrs):
    out = []
    for a in arrs:
        f = a.reshape(-1)
        out.append(jnp.pad(f, (0, (-f.shape[0]) % 1024)).reshape(-1, 128))
    rows = sum(a.shape[0] for a in out)
    out.append(jnp.zeros(((-rows) % 256, 128), out[0].dtype))
    return jnp.concatenate(out, axis=0)


def _unpack(packed, shapes):
    out, r = [], 0
    for s in shapes:
        n = math.prod(s)
        rows = (n + 1023) // 1024 * 8
        out.append(packed[r:r + rows].reshape(-1)[:n].reshape(s))
        r += rows
    return out


def _row_block(r, c, limit):
    best = None
    for tr in range(16, r + 1, 16):
        if r % tr == 0 and tr * c * 4 <= limit:
            best = tr
    return r if best is None else best


_BIG = ("w_in_ab", "pool_w", "w_out_ab", "w_in_cd", "glu_w1", "glu_w2", "w_out_cd", "w_xq", "w_xkv", "w_xo")
_STACKED = ("w_in_ab", "w_in_cd", "w_xkv")
_MIXER_BIG = (("w_in_ab", "pool_w", "w_out_ab"), ("w_in_cd", "glu_w1", "glu_w2", "w_out_cd"))
_CROSS_BIG = ("w_xq", "w_xkv", "w_xo")
_SMALL_SPLIT = ["norm_cd", "sgu_ln_g", "sgu_ln_b", "s5_d"]
_REPLICATED_ODD = ["sgu_w", "sgu_b", "s5_a_re", "s5_a_im", "s5_log_dt", "s5_b_re", "s5_b_im", "s5_c_re", "s5_c_im", "final_norm"]
_REPLICATED_EVEN = ["norm_ab", "pool_scale", "norm_x", "mem_norm"]
_REPLICATED = _REPLICATED_ODD + _REPLICATED_EVEN
_WEIGHTS = ["norm_ab", "w_in_ab", "pool_w", "pool_scale", "w_out_ab", "norm_cd", "w_in_cd", "sgu_ln_g", "sgu_ln_b", "sgu_w", "sgu_b", "s5_a_re",
            "s5_a_im", "s5_log_dt", "s5_b_re", "s5_b_im", "s5_c_re", "s5_c_im", "s5_d", "glu_w1", "glu_w2", "w_out_cd", "norm_x", "w_xq",
            "w_xkv", "w_xo", "mem_norm", "final_norm"]


def _layer_big(layer):
    return [(n, layer // 2) for n in _MIXER_BIG[layer % 2]] + [(n, layer) for n in _CROSS_BIG]


def _gather_parts(layer):
    big = _layer_big(layer)
    return [big[:1], big[1:]] if layer % 2 == 0 else [big]


def _from_slots(name, a):
    if name in _STACKED:
        return a
    if name == "pool_w":
        return a.reshape(N_DEV, 4, 32, 256).transpose(1, 0, 2, 3).reshape(4, 256, 256)
    return a.reshape(-1, a.shape[-1])


def _to_slots(name, g):
    if name in _STACKED:
        return g
    if name == "pool_w":
        return g.reshape(4, N_DEV, 32, 256).transpose(1, 0, 2, 3).reshape(N_DEV, 128, 256)
    return g.reshape(N_DEV, -1, g.shape[-1])


def _rows2d(a):
    return a.reshape(-1, a.shape[-1])


def _small_rows(block):
    return jnp.pad(block, ((0, 0), (0, 128 - block.shape[1])))


def kernel(x, mem, norm_ab, w_in_ab, pool_w, pool_scale, w_out_ab, norm_cd, w_in_cd, sgu_ln_g, sgu_ln_b, sgu_w, sgu_b, s5_a_re, s5_a_im, s5_log_dt, s5_b_re, s5_b_im, s5_c_re, s5_c_im, s5_d, glu_w1, glu_w2, w_out_cd, norm_x, w_xq, w_xkv, w_xo, mem_norm, final_norm, loss_target, m_norm_ab, m_w_in_ab, m_pool_w, m_pool_scale, m_w_out_ab, m_norm_cd, m_w_in_cd, m_sgu_ln_g, m_sgu_ln_b, m_sgu_w, m_sgu_b, m_s5_a_re, m_s5_a_im, m_s5_log_dt, m_s5_b_re, m_s5_b_im, m_s5_c_re, m_s5_c_im, m_s5_d, m_glu_w1, m_glu_w2, m_w_out_cd, m_norm_x, m_w_xq, m_w_xkv, m_w_xo, m_mem_norm, m_final_norm, v_norm_ab, v_w_in_ab, v_pool_w, v_pool_scale, v_w_out_ab, v_norm_cd, v_w_in_cd, v_sgu_ln_g, v_sgu_ln_b, v_sgu_w, v_sgu_b, v_s5_a_re, v_s5_a_im, v_s5_log_dt, v_s5_b_re, v_s5_b_im, v_s5_c_re, v_s5_c_im, v_s5_d, v_glu_w1, v_glu_w2, v_w_out_cd, v_norm_x, v_w_xq, v_w_xkv, v_w_xo, v_mem_norm, v_final_norm):
    args = locals()
    w = {n: args[n] for n in _WEIGHTS}
    m = {n: args["m_" + n] for n in _WEIGHTS}
    v = {n: args["v_" + n] for n in _WEIGHTS}

    me = jnp.reshape(_lin(_mesh_pos()), (1,)).astype(jnp.int32)

    order = [(layer, p) for layer in range(DEPTH) for p in range(len(_gather_parts(layer)))]
    gathers = {}

    def begin_gather(key, after):
        lands = []
        for name, i in _gather_parts(key[0])[key[1]]:
            b2 = _rows2d(w[name])
            r = b2.shape[0] // w[name].shape[0]
            lands.append(_into_slot("cast_" + name, b2, i * r, r, me, BF16, None if lands else after))
        if key == order[0]:
            small_blocks = jnp.concatenate([_small_rows(w[n]) for n in _SMALL_SPLIT], axis=0)
            lands.append(_into_slot("cast_small", small_blocks, 0, 8, me, F32, None))
        gathers[key] = _exchange_start("gather%d%s_start" % (key[0], "ab"[key[1]]), True, [], lands)
        return gathers[key]["token"]

    W = {n: w[n] for n in _REPLICATED}
    W["mem_norm"] = w["mem_norm"] + begin_gather(order[0], None)
    for name in _BIG:
        W[name] = [None] * w[name].shape[0]

    def weights_of(layer, part, after):
        key = (layer, part)
        if key not in gathers:
            return 0.0
        _, got = _exchange_wait("gather%d%s_wait" % (layer, "ab"[part]), gathers[key], after)
        for (name, i), arr in zip(_gather_parts(layer)[part], got):
            W[name][i] = _from_slots(name, arr)
        if key == order[0]:
            sm = got[-1].reshape(N_DEV, 4, 2, 128)
            for j, n in enumerate(_SMALL_SPLIT):
                width = w[n].shape[1]
                W[n] = sm[:, j, :, :width].transpose(1, 0, 2).reshape(2, N_DEV * width)
        nxt = order.index(key) + 1
        return begin_gather(order[nxt], got[0]) if nxt < len(order) else 0.0

    scatters, small = [None] * DEPTH, {}

    def before_last(layer, G):
        srcs = [_to_slots(name, G[name][i]) for name, i in _layer_big(layer)]
        scatters[layer] = _scatter_begin("scatter%d_start" % layer, srcs)
        return scatters[layer]["token"]

    def begin_small(tag, G, names, split):
        srcs = [_pack([G[n] if n in ("mem_norm", "final_norm") else jnp.stack(G[n]) for n in names])]
        if split:
            gs = jnp.stack([jnp.stack(G[n]).reshape(2, N_DEV, -1).transpose(1, 0, 2) for n in _SMALL_SPLIT[:3]]
                           + [jnp.pad(jnp.stack(G["s5_d"]).reshape(2, N_DEV, -1).transpose(1, 0, 2), ((0, 0), (0, 0), (0, 64)))], axis=1)
            srcs.append(gs.reshape(N_DEV, 8, 128))
        small[tag] = _scatter_begin("scatter_small_%s_start" % tag, srcs)
        return small[tag]["token"]

    def after_layer(layer, G):
        return begin_small("odd", G, _REPLICATED_ODD, True) if layer == 1 else 0.0

    loss, dx, G = _local_step(x[0], mem[0], loss_target[0], W, weights_of, before_last, after_layer)
    loss = lax.psum(loss[0, 0], MESH_AXES)
    begin_small("even", G, _REPLICATED_EVEN, False)

    out = {}
    after = small["even"]["token_arr"]
    for layer in reversed(range(DEPTH)):
        own, got = _exchange_wait("scatter%d_wait" % layer, scatters[layer], after)
        for (name, i), mine, parts in zip(_layer_big(layer), own, got):
            out[name] = _adam("adam_" + name, _rows2d(w[name]), _rows2d(m[name]), _rows2d(v[name]), parts, mine, me, i, out.get(name))
        after = out[name][0]
    for name in _BIG:
        out[name] = [a.reshape(w[name].shape) for a in out[name]]
    own, got = _exchange_wait("scatter_small_odd_wait", small["odd"], after)
    pk = lambda d: jnp.concatenate([_small_rows(d[n]) for n in _SMALL_SPLIT], axis=0)
    res = _adam("adam_small", pk(w), pk(m), pk(v), got[1], own[1], me, 0, None)
    for j, n in enumerate(_SMALL_SPLIT):
        out[n] = [a[2 * j:2 * j + 2, :w[n].shape[1]] for a in res]
    own_even, got_even = _exchange_wait("scatter_small_even_wait", small["even"], res[0])
    for names, mine, parts in zip((_REPLICATED_ODD, _REPLICATED_EVEN), (own[0], own_even[0]), (got[0], got_even[0])):
        pk = lambda d: _pack([d[n] for n in names])
        res = _adam("adam_replicated", pk(w), pk(m), pk(v), parts, mine, me, 0, None)
        shapes = [w[n].shape for n in names]
        un = [_unpack(a, shapes) for a in res]
        for j, n in enumerate(names):
            out[n] = [un[q][j] for q in range(4)]

    return (loss, dx[None], *[out[n][0] for n in _WEIGHTS], *[out[n][1] for n in _WEIGHTS], *[out[n][2] for n in _WEIGHTS],
            *[out[n][3] for n in _WEIGHTS])
```

```python
import functools
import math

import jax
import jax.numpy as jnp
from jax import lax
from jax.experimental import pallas as pl
from jax.experimental.pallas import tpu as pltpu

F32 = jnp.float32
BF16 = jnp.bfloat16

SEQ = 2048
D_MODEL = 1024
MEM_LEN = 256
DEPTH = 4
N_DEV = 8
EPS = 1e-6
NEG = -1e30
A_HEAD_DIM = 64
X_HEAD_DIM = 256
S5_GROUPS = 32
S5_STATE = 64
S5_GROUP_DIM = 16

ADAM_LR = 0.001
ADAM_B1 = 0.9
ADAM_B2 = 0.999
ADAM_EPS = 1e-08
ADAM_WD = 0.01
ADAM_STEP = 10

V7X_VMEM_LIMIT_BYTES = 56 * 1024 * 1024
MESH_AXES = ("x", "y", "c")


def _pc(body, *, name, out_shape, grid=None, in_specs=None, out_specs=None, scratch_shapes=(), aliases=None, sem=None):
    kw = {}
    if grid is not None:
        kw["grid"] = grid
    if in_specs is not None:
        kw["in_specs"] = in_specs
    if out_specs is not None:
        kw["out_specs"] = out_specs
    if aliases:
        kw["input_output_aliases"] = aliases
    return pl.pallas_call(
        body,
        name=name,
        out_shape=out_shape,
        scratch_shapes=list(scratch_shapes),
        compiler_params=pltpu.CompilerParams(dimension_semantics=sem, vmem_limit_bytes=V7X_VMEM_LIMIT_BYTES),
        interpret=False,
        **kw,
    )


def _cols(arr, c0=0, width=None, nsplit=1, r0=0):
    width = arr.shape[1] - c0 if width is None else width
    assert c0 % width == 0 and width % nsplit == 0
    return (arr, c0, width, nsplit, r0)


def _par(arr, nsplit=1):
    return (arr, nsplit)


def _ld(ref, nsplit):
    if nsplit == 1:
        return ref[...].astype(F32)
    if len(ref.shape) == 3:
        return tuple(ref[k].astype(F32) for k in range(nsplit))
    w = ref.shape[-1] // nsplit
    return tuple(ref[:, k * w:(k + 1) * w].astype(F32) for k in range(nsplit))


def _st(ref, val, nsplit, accumulate=False):
    if nsplit == 1:
        val = (val,)
    for k in range(nsplit):
        if nsplit == 1:
            idx = (Ellipsis,)
        elif len(ref.shape) == 3:
            idx = (k,)
        else:
            w = ref.shape[-1] // nsplit
            idx = (slice(None), slice(k * w, (k + 1) * w))
        if accumulate:
            ref[idx] += val[k].astype(ref.dtype)
        else:
            ref[idx] = val[k].astype(ref.dtype)


def _row_spec(tr, op):
    _, c0, w, _, r0 = op
    assert r0 % tr == 0
    return pl.BlockSpec((tr, w), lambda i, cb=c0 // w, rb=r0 // tr: (i + rb, cb))


def _full_spec(arr):
    return pl.BlockSpec(arr.shape, lambda i, nd=arr.ndim: (0,) * nd)


def _rw_fwd(name, f, rows, pars, outs, tr, n_rows=None):
    n_rows = rows[0][0].shape[0] if n_rows is None else n_rows
    nr, npar = len(rows), len(pars)

    def body(*refs):
        r = [_ld(refs[i], rows[i][3]) for i in range(nr)]
        p = [_ld(refs[nr + i], pars[i][1]) for i in range(npar)]
        res = f(r, p)
        for k, (_, _, ns) in enumerate(outs):
            _st(refs[nr + npar + k], res[k], ns)

    res = _pc(
        body, name=name, grid=(n_rows // tr,),
        in_specs=[_row_spec(tr, op) for op in rows] + [_full_spec(a) for a, _ in pars],
        out_specs=[pl.BlockSpec((tr, w), lambda i: (i, 0)) for w, _, _ in outs],
        out_shape=[jax.ShapeDtypeStruct((n_rows, w), dt) for w, dt, _ in outs],
        sem=("arbitrary",),
    )(*[op[0] for op in rows], *[a for a, _ in pars])
    return list(res)


def _rw_bwd(name, f, rows, pars, douts, drow, dpar, tr):
    n_rows = rows[0][0].shape[0]
    nr, npar = len(rows), len(pars)
    dgiven = [d for d in douts if d is not None]
    nd = len(dgiven)

    def body(*refs):
        r = [_ld(refs[i], rows[i][3]) for i in range(nr)]
        p = [_ld(refs[nr + i], pars[i][1]) for i in range(npar)]
        d = [_ld(refs[nr + npar + i], dgiven[i][3]) for i in range(nd)]
        orefs = refs[nr + npar + nd:]

        def g(dr, dp):
            rr, pp = list(r), list(p)
            for j, (idx, _) in enumerate(drow):
                rr[idx] = dr[j]
            for j, idx in enumerate(dpar):
                pp[idx] = dp[j]
            return tuple(f(rr, pp))

        out, vjp = jax.vjp(g, [r[idx] for idx, _ in drow], [p[idx] for idx in dpar])
        ct, j = [], 0
        for k, o in enumerate(out):
            if douts[k] is None:
                ct.append(jax.tree.map(jnp.zeros_like, o))
            else:
                ct.append(d[j])
                j += 1
        gdr, gdp = vjp(tuple(ct))
        for j, (idx, _) in enumerate(drow):
            _st(orefs[j], gdr[j], rows[idx][3])

        @pl.when(pl.program_id(0) == 0)
        def _():
            for j in range(len(dpar)):
                orefs[len(drow) + j][...] = jnp.zeros_like(orefs[len(drow) + j])

        for j, idx in enumerate(dpar):
            _st(orefs[len(drow) + j], gdp[j], pars[idx][1], accumulate=True)

    res = _pc(
        body, name=name, grid=(n_rows // tr,),
        in_specs=[_row_spec(tr, op) for op in rows] + [_full_spec(a) for a, _ in pars] + [_row_spec(tr, op) for op in dgiven],
        out_specs=[pl.BlockSpec((tr, rows[idx][2]), lambda i: (i, 0)) for idx, _ in drow] + [_full_spec(pars[idx][0]) for idx in dpar],
        out_shape=[jax.ShapeDtypeStruct((n_rows, rows[idx][2]), dt) for idx, dt in drow]
        + [jax.ShapeDtypeStruct(pars[idx][0].shape, F32) for idx in dpar],
        sem=("arbitrary",),
    )(*[op[0] for op in rows], *[a for a, _ in pars], *[op[0] for op in dgiven])
    res = list(res)
    return res[:len(drow)], res[len(drow):]


def _sigmoid(x):
    return jax.nn.sigmoid(x)


def _silu(x):
    return x * _sigmoid(x)


def _rms(x, g):
    return x * lax.rsqrt(jnp.mean(x * x, axis=-1, keepdims=True) + EPS) * g


def _f_rms(r, p):
    return [_rms(r[0], p[0])]


def _f_rms_res(r, p):
    return [r[0], _rms(r[0], p[0])]


def _f_gate_ab(r, p):
    o, ga, mixed, gb = r
    return [(o * _silu(ga), mixed * p[0] * _silu(gb))]


def _f_sgu(r, p):
    u, v, gc = r
    lg, lb, w, b = p
    n = float(D_MODEL)
    mu = sum(jnp.sum(vk, axis=-1, keepdims=True) for vk in v) / n
    var = sum(jnp.sum(jnp.square(vk - mu), axis=-1, keepdims=True) for vk in v) / n
    rs = lax.rsqrt(var + EPS)
    t = w[0].shape[0]
    tri = lax.broadcasted_iota(jnp.int32, (t, t), 0) >= lax.broadcasted_iota(jnp.int32, (t, t), 1)
    outs = []
    for k in range(len(v)):
        vn = (v[k] - mu) * rs * lg[k] + lb[k]
        mixed = jnp.dot(jnp.where(tri, w[k], 0.0), vn, preferred_element_type=F32) + b[k]
        outs.append(u[k] * mixed * _silu(gc[k]))
    return [tuple(outs)]


def _gelu(x):
    return 0.5 * x * (1.0 + jnp.tanh(math.sqrt(2.0 / math.pi) * (x + 0.044715 * (x * x * x))))


def _f_gelu_y(r, p):
    yc, uf = r
    return [_gelu(yc + p[0] * uf)]


def _f_glu_gate(r, p):
    t12, gd = r
    return [t12[0] * _sigmoid(t12[1]) * _silu(gd)]


def _f_s5_prep(r, p):
    ar, ai, ldt = r
    dt = jnp.exp(ldt)
    mag = jnp.exp(dt * ar)
    abar_re = mag * jnp.cos(dt * ai)
    abar_im = mag * jnp.sin(dt * ai)
    nr, ni = abar_re - 1.0, abar_im
    inv = 1.0 / (ar * ar + ai * ai)
    return [abar_re, abar_im, (nr * ar + ni * ai) * inv, (ni * ar - nr * ai) * inv]


def _f_bbar(r, p):
    br, bi, cr, ci = r
    return [cr * br - ci * bi, cr * bi + ci * br]


def _loss_head(x, target, g):
    tr = 256
    n_rows, width = x.shape

    def f(xv, gv, tv):
        err = jnp.square(_rms(xv, gv) - tv)
        return 0.5 * jnp.mean(err, axis=-1, keepdims=True)

    def body(x_ref, t_ref, g_ref, loss_ref, dx_ref, dg_ref):
        @pl.when(pl.program_id(0) == 0)
        def _():
            loss_ref[...] = jnp.zeros_like(loss_ref)
            dg_ref[...] = jnp.zeros_like(dg_ref)

        tv = t_ref[...]
        row_loss, vjp = jax.vjp(lambda a, b: f(a, b, tv), x_ref[...], g_ref[...])
        dx, dg = vjp(jnp.ones_like(row_loss))
        dx_ref[...] = dx
        dg_ref[...] += dg
        loss_ref[...] += jnp.broadcast_to(jnp.sum(row_loss, axis=0, keepdims=True), loss_ref.shape)

    blk = pl.BlockSpec((tr, width), lambda i: (i, 0))
    one = pl.BlockSpec((1, width), lambda i: (0, 0))
    return _pc(
        body, name="loss_head", grid=(n_rows // tr,), in_specs=[blk, blk, one],
        out_specs=[pl.BlockSpec((1, 128), lambda i: (0, 0)), blk, one],
        out_shape=[jax.ShapeDtypeStruct((1, 128), F32), jax.ShapeDtypeStruct(x.shape, F32), jax.ShapeDtypeStruct((1, width), F32)],
        sem=("arbitrary",),
    )(x, target, g)


_NT = (((1,), (1,)), ((), ()))
_TN = (((0,), (0,)), ((), ()))


def _tile(n, cap):
    t = min(n, cap)
    while n % t:
        t -= 128
    assert t > 0
    return t


def _mm(name, a, b, *, ta=False, tb=False, out_dtype=F32, a_off=0, a_width=None, res=None, out_stack=None):
    assert not (ta and tb)
    stacked = b.ndim == 3
    bk, bn = (b.shape[1], b.shape[0] * b.shape[2]) if stacked else b.shape
    if ta:
        kc = a.shape[0]
        m = a.shape[1] - a_off if a_width is None else a_width
        n = bn
        assert bk == kc and not stacked
    else:
        m = a.shape[0]
        kc = a.shape[1] - a_off if a_width is None else a_width
        n = bk if tb else bn
        assert (bn if tb else bk) == kc
    tm = _tile(m, 512)
    tn = _tile(b.shape[2] if stacked and not tb else (out_stack or n), 1024)
    tk = _tile(b.shape[2] if stacked and tb else kc, 1024)
    nk = kc // tk
    if ta:
        assert a_off % tm == 0
        a_spec = pl.BlockSpec((tk, tm), lambda i, j, k, o=a_off // tm: (k, i + o))
        dims = _TN
    else:
        assert a_off % tk == 0
        a_spec = pl.BlockSpec((tm, tk), lambda i, j, k, o=a_off // tk: (i, k + o))
        dims = _NT if tb else (((1,), (0,)), ((), ()))
    if stacked and tb:
        b_spec = pl.BlockSpec((None, tn, tk), lambda i, j, k, q=b.shape[2] // tk: (k // q, j, k % q))
    elif stacked:
        b_spec = pl.BlockSpec((None, tk, tn), lambda i, j, k, q=b.shape[2] // tn: (j // q, k, j % q))
    else:
        b_spec = pl.BlockSpec((tn, tk), lambda i, j, k: (j, k)) if tb else pl.BlockSpec((tk, tn), lambda i, j, k: (k, j))
    if out_stack:
        out_spec = pl.BlockSpec((None, tm, tn), lambda i, j, k, q=out_stack // tn: (j // q, i, j % q))
        out_shape = jax.ShapeDtypeStruct((n // out_stack, m, out_stack), out_dtype)
    else:
        out_spec = pl.BlockSpec((tm, tn), lambda i, j, k: (i, j))
        out_shape = jax.ShapeDtypeStruct((m, n), out_dtype)
    in_specs, args = [a_spec, b_spec], [a, b]
    has_res = res is not None
    if has_res:
        in_specs.append(pl.BlockSpec((tm, tn), lambda i, j, k: (i, j)))
        args.append(res)

    def body(*refs):
        a_ref, b_ref = refs[0], refs[1]
        o_ref, acc_ref = refs[-2], refs[-1]
        k = pl.program_id(2)

        @pl.when(k == 0)
        def _():
            acc_ref[...] = jnp.zeros_like(acc_ref)

        acc_ref[...] += lax.dot_general(a_ref[...].astype(BF16), b_ref[...].astype(BF16), dims, preferred_element_type=F32)

        @pl.when(k == nk - 1)
        def _():
            acc = acc_ref[...]
            if has_res:
                acc = acc + refs[2][...].astype(F32)
            o_ref[...] = acc.astype(o_ref.dtype)

    return _pc(
        body, name=name, grid=(m // tm, n // tn, nk), in_specs=in_specs, out_specs=out_spec, out_shape=out_shape,
        scratch_shapes=[pltpu.VMEM((tm, tn), F32)], sem=("parallel", "parallel", "arbitrary"),
    )(*args)


def _mm_blocks(name, a, b, *, grid, out_blk, a_blk, a_idx, b_blk, b_idx, dims, out_dtype=F32):
    gi, gj, nk = grid

    def body(a_ref, b_ref, o_ref, acc_ref):
        k = pl.program_id(2)

        @pl.when(k == 0)
        def _():
            acc_ref[...] = jnp.zeros_like(acc_ref)

        acc_ref[...] += lax.dot_general(a_ref[...].astype(BF16), b_ref[...].astype(BF16), dims, preferred_element_type=F32)

        @pl.when(k == nk - 1)
        def _():
            o_ref[...] = acc_ref[...].astype(o_ref.dtype)

    return _pc(
        body, name=name, grid=grid, in_specs=[pl.BlockSpec(a_blk, a_idx), pl.BlockSpec(b_blk, b_idx)],
        out_specs=pl.BlockSpec(out_blk, lambda i, j, k: (i, j)), out_shape=jax.ShapeDtypeStruct((gi * out_blk[0], gj * out_blk[1]), out_dtype),
        scratch_shapes=[pltpu.VMEM(out_blk, F32)], sem=("parallel", "parallel", "arbitrary"),
    )(a, b)


def _head_masks(width, nsub):
    lane = lax.broadcasted_iota(jnp.int32, (1, width), 1)
    hd = width // nsub
    return [(lane >= h * hd) & (lane < (h + 1) * hd) for h in range(nsub)]


def _dilated_log_count(row0, tq, ext):
    delta = (row0 + lax.broadcasted_iota(jnp.int32, (tq, ext), 0)) - lax.broadcasted_iota(jnp.int32, (tq, ext), 1)
    cnt = (delta <= 128).astype(jnp.int32) + (((delta & 3) == 0) & (delta <= 512)).astype(jnp.int32) + ((delta & 15) == 0).astype(jnp.int32)
    logc = jnp.where(cnt == 3, math.log(3.0), jnp.where(cnt == 2, math.log(2.0), 0.0))
    return jnp.where((delta >= 0) & (cnt > 0), logc, NEG)


def _softmax_rows(s):
    m = jnp.max(s, axis=-1, keepdims=True)
    p = jnp.exp(s - m)
    return p * (1.0 / jnp.sum(p, axis=-1, keepdims=True))


def _attn_fwd(name, qa, ka, va, *, qc, kc, vc, width, nblk, nsub, causal, tq, scale, out_dtype):
    sq, t_len = qa.shape[0], ka.shape[0]

    def body(q_ref, k_ref, v_ref, o_ref):
        kb = k_ref[...].astype(BF16)
        vb = v_ref[...].astype(BF16)
        masks = _head_masks(width, nsub)
        for r in range(sq // tq):
            ext = (r + 1) * tq if causal else t_len
            q = q_ref[r * tq:(r + 1) * tq, :].astype(F32)
            ke, ve = kb[:ext], vb[:ext]
            bias = _dilated_log_count(r * tq, tq, ext) if causal else None
            o = None
            for h in range(nsub):
                qm = (jnp.where(masks[h], q, 0.0) if nsub > 1 else q).astype(BF16)
                s = lax.dot_general(qm, ke, _NT, preferred_element_type=F32) * scale
                if causal:
                    s = jnp.where(bias > 0.5 * NEG, s + bias, NEG)
                pn = _softmax_rows(s).astype(BF16)
                oh = jnp.dot(pn, ve, preferred_element_type=F32)
                o = oh if o is None else jnp.where(masks[h], oh, o)
            o_ref[r * tq:(r + 1) * tq, :] = o.astype(o_ref.dtype)

    return _pc(
        body, name=name, grid=(nblk,),
        in_specs=[pl.BlockSpec((sq, width), lambda i, c=qc: (0, c + i)), pl.BlockSpec((t_len, width), lambda i, c=kc: (0, c + i)),
                  pl.BlockSpec((t_len, width), lambda i, c=vc: (0, c + i))],
        out_specs=pl.BlockSpec((sq, width), lambda i: (0, i)),
        out_shape=jax.ShapeDtypeStruct((sq, nblk * width), out_dtype), sem=("parallel",),
    )(qa, ka, va)


def _attn_bwd(name, qa, ka, va, doa, *, qc, kc, vc, width, nblk, nsub, causal, tq, scale, out_dtype):
    sq, t_len = qa.shape[0], ka.shape[0]

    def body(q_ref, k_ref, v_ref, do_ref, dq_ref, dk_ref, dv_ref, dk_acc, dv_acc):
        kb = k_ref[...].astype(BF16)
        vb = v_ref[...].astype(BF16)
        masks = _head_masks(width, nsub)
        dk_acc[...] = jnp.zeros_like(dk_acc)
        dv_acc[...] = jnp.zeros_like(dv_acc)
        for r in range(sq // tq):
            ext = (r + 1) * tq if causal else t_len
            q = q_ref[r * tq:(r + 1) * tq, :].astype(F32)
            do = do_ref[r * tq:(r + 1) * tq, :].astype(F32)
            ke, ve = kb[:ext], vb[:ext]
            bias = _dilated_log_count(r * tq, tq, ext) if causal else None
            dq = None
            for h in range(nsub):
                qm = (jnp.where(masks[h], q, 0.0) if nsub > 1 else q).astype(BF16)
                dom = (jnp.where(masks[h], do, 0.0) if nsub > 1 else do).astype(BF16)
                s = lax.dot_general(qm, ke, _NT, preferred_element_type=F32) * scale
                if causal:
                    s = jnp.where(bias > 0.5 * NEG, s + bias, NEG)
                pn = _softmax_rows(s)
                dpn = lax.dot_general(dom, ve, _NT, preferred_element_type=F32)
                ds = pn * (dpn - jnp.sum(pn * dpn, axis=-1, keepdims=True))
                dsb = (ds * scale).astype(BF16)
                dqh = jnp.dot(dsb, ke, preferred_element_type=F32)
                dq = dqh if dq is None else jnp.where(masks[h], dqh, dq)
                dk_acc[0:ext, :] += lax.dot_general(dsb, qm, _TN, preferred_element_type=F32)
                dv_acc[0:ext, :] += lax.dot_general(pn.astype(BF16), dom, _TN, preferred_element_type=F32)
            dq_ref[r * tq:(r + 1) * tq, :] = dq.astype(dq_ref.dtype)
        dk_ref[...] = dk_acc[...].astype(dk_ref.dtype)
        dv_ref[...] = dv_acc[...].astype(dv_ref.dtype)

    return _pc(
        body, name=name, grid=(nblk,),
        in_specs=[pl.BlockSpec((sq, width), lambda i, c=qc: (0, c + i)), pl.BlockSpec((t_len, width), lambda i, c=kc: (0, c + i)),
                  pl.BlockSpec((t_len, width), lambda i, c=vc: (0, c + i)), pl.BlockSpec((sq, width), lambda i: (0, i))],
        out_specs=[pl.BlockSpec((sq, width), lambda i: (0, i)), pl.BlockSpec((t_len, width), lambda i: (0, i)), pl.BlockSpec((t_len, width), lambda i: (0, i))],
        out_shape=[jax.ShapeDtypeStruct((sq, nblk * width), out_dtype), jax.ShapeDtypeStruct((t_len, nblk * width), out_dtype),
                   jax.ShapeDtypeStruct((t_len, nblk * width), out_dtype)],
        scratch_shapes=[pltpu.VMEM((t_len, width), F32), pltpu.VMEM((t_len, width), F32)], sem=("parallel",),
    )(qa, ka, va, doa)


_SELF = dict(qc=0, kc=8, vc=16, width=128, nblk=8, nsub=2, causal=True, tq=256, scale=A_HEAD_DIM ** -0.5)
_CROSS = dict(qc=0, kc=0, vc=4, width=256, nblk=4, nsub=1, causal=False, tq=512, scale=X_HEAD_DIM ** -0.5)


def _window_sum(x, g, row, backward):
    n = x.shape[0]

    def shift(y, k):
        if backward:
            return jnp.where(row < n - k, pltpu.roll(y, n - k, 0), 0.0)
        return jnp.where(row >= k, pltpu.roll(y, k, 0), 0.0)

    s2 = x + shift(x, 1)
    s4 = s2 + shift(s2, 2)
    s8 = s4 + shift(s4, 4)
    s16 = s8 + shift(s8, 8)
    return jnp.where(g == 0, s2, jnp.where(g == 1, s4, jnp.where(g == 2, s8, s16)))


def _pool(name, arr, c0, backward, out_dtype):
    n = arr.shape[0]
    gw = 256

    def body(v_ref, o_ref):
        g = pl.program_id(0)
        v = v_ref[...].astype(F32)
        row = lax.broadcasted_iota(jnp.int32, v.shape, 0)
        w = jnp.where(g == 0, 2, jnp.where(g == 1, 4, jnp.where(g == 2, 8, 16)))
        cnt = jnp.minimum(row + 1, w).astype(F32)
        if backward:
            o_ref[...] = (_window_sum(v / cnt, g, row, True) - v).astype(o_ref.dtype)
        else:
            o_ref[...] = (_window_sum(v, g, row, False) / cnt - v).astype(o_ref.dtype)

    return _pc(
        body, name=name, grid=(4,), in_specs=[pl.BlockSpec((n, gw), lambda i, c=c0 // gw: (0, c + i))],
        out_specs=pl.BlockSpec((n, gw), lambda i: (0, i)), out_shape=jax.ShapeDtypeStruct((n, 4 * gw), out_dtype), sem=("parallel",),
    )(arr)


_SCAN_ROWS = 256


def _scan_fwd(bu3, a2):
    n = bu3.shape[0]

    def body(bu_ref, a_ref, h_ref, carry):
        @pl.when(pl.program_id(0) == 0)
        def _():
            carry[...] = jnp.zeros_like(carry)

        ar, ai = a_ref[0:16, :], a_ref[16:32, :]

        def step(t, c):
            hr, hi = c
            nr = ar * hr - ai * hi + bu_ref[t, 0:16, :]
            ni = ar * hi + ai * hr + bu_ref[t, 16:32, :]
            h_ref[t, 0:16, :] = nr
            h_ref[t, 16:32, :] = ni
            return nr, ni

        hr, hi = lax.fori_loop(0, _SCAN_ROWS, step, (carry[0:16, :], carry[16:32, :]), unroll=8)
        carry[0:16, :] = hr
        carry[16:32, :] = hi

    blk = pl.BlockSpec((_SCAN_ROWS, 32, 128), lambda i: (i, 0, 0))
    return _pc(
        body, name="s5_scan_fwd", grid=(n // _SCAN_ROWS,), in_specs=[blk, pl.BlockSpec((32, 128), lambda i: (0, 0))], out_specs=blk,
        out_shape=jax.ShapeDtypeStruct(bu3.shape, F32), scratch_shapes=[pltpu.VMEM((32, 128), F32)], sem=("arbitrary",),
    )(bu3, a2)


def _scan_bwd(dh3, h3, a2):
    n = dh3.shape[0]
    nb = n // _SCAN_ROWS

    def body(dh_ref, h_ref, a_ref, dbu_ref, da_ref, carry):
        @pl.when(pl.program_id(0) == 0)
        def _():
            carry[...] = jnp.zeros_like(carry)
            da_ref[...] = jnp.zeros_like(da_ref)

        ar, ai = a_ref[0:16, :], a_ref[16:32, :]

        def step(tt, c):
            gr, gi, dar, dai = c
            t = _SCAN_ROWS - 1 - tt
            hr, hi = h_ref[t, 0:16, :], h_ref[t, 16:32, :]
            dar = dar + gr * hr + gi * hi
            dai = dai - gr * hi + gi * hr
            ngr = dh_ref[t, 0:16, :] + ar * gr + ai * gi
            ngi = dh_ref[t, 16:32, :] - ai * gr + ar * gi
            dbu_ref[t, 0:16, :] = ngr
            dbu_ref[t, 16:32, :] = ngi
            return ngr, ngi, dar, dai

        z = jnp.zeros((16, 128), F32)
        gr, gi, dar, dai = lax.fori_loop(0, _SCAN_ROWS, step, (carry[0:16, :], carry[16:32, :], z, z), unroll=8)
        carry[0:16, :] = gr
        carry[16:32, :] = gi
        da_ref[0:16, :] += dar
        da_ref[16:32, :] += dai

    blk = pl.BlockSpec((_SCAN_ROWS, 32, 128), lambda i: (nb - 1 - i, 0, 0))
    small = pl.BlockSpec((32, 128), lambda i: (0, 0))
    return _pc(
        body, name="s5_scan_bwd", grid=(nb,), in_specs=[blk, blk, small], out_specs=[blk, small],
        out_shape=[jax.ShapeDtypeStruct(dh3.shape, F32), jax.ShapeDtypeStruct((32, 128), F32)],
        scratch_shapes=[pltpu.VMEM((32, 128), F32)], sem=("arbitrary",),
    )(dh3, h3, a2)


def _bdense(bb_re, bb_im):
    eye = jnp.eye(8, dtype=F32)

    def one(bb):
        return jnp.einsum("sgph,gk->sghkp", bb.reshape(4, 8, S5_STATE, S5_GROUP_DIM), eye).reshape(512, 512)

    return jnp.concatenate([one(bb_re), one(bb_im)], axis=1)


def _cdense(c_re, c_im):
    eye = jnp.eye(8, dtype=F32)

    def one(cc):
        return jnp.einsum("sghp,gk->sgpkh", cc.reshape(4, 8, S5_GROUP_DIM, S5_STATE), eye).reshape(2048, 128)

    return jnp.concatenate([one(c_re), -one(c_im)], axis=0)


_NN = (((1,), (0,)), ((), ()))
_UF_BLOCK = 3072 // 128


def _s5_bu(z, bd):
    return _mm_blocks("mm_s5_bu", z, bd, grid=(1, 8, 1), out_blk=(SEQ, 512), a_blk=(SEQ, 128), a_idx=lambda i, j, k: (0, _UF_BLOCK + j % 4),
                      b_blk=(128, 512), b_idx=lambda i, j, k: (j % 4, j // 4), dims=_NN)


def _s5_bu_dx(dbu, bd):
    return _mm_blocks("mm_s5_bu_dx", dbu, bd, grid=(1, 4, 2), out_blk=(SEQ, 128), a_blk=(SEQ, 512), a_idx=lambda i, j, k: (0, 4 * k + j),
                      b_blk=(128, 512), b_idx=lambda i, j, k: (j, k), dims=_NT)


def _s5_bu_dw(z, dbu):
    return _mm_blocks("mm_s5_bu_dw", z, dbu, grid=(4, 2, 2), out_blk=(128, 512), a_blk=(1024, 128), a_idx=lambda i, j, k: (k, _UF_BLOCK + i),
                      b_blk=(1024, 512), b_idx=lambda i, j, k: (k, 4 * j + i), dims=_TN)


def _s5_y(h2, cf):
    return _mm_blocks("mm_s5_y", h2, cf, grid=(1, 4, 2), out_blk=(SEQ, 128), a_blk=(SEQ, 512), a_idx=lambda i, j, k: (0, 4 * k + j),
                      b_blk=(512, 128), b_idx=lambda i, j, k: (4 * k + j, 0), dims=_NN)


def _s5_y_dx(dyc, cf):
    return _mm_blocks("mm_s5_y_dx", dyc, cf, grid=(1, 8, 1), out_blk=(SEQ, 512), a_blk=(SEQ, 128), a_idx=lambda i, j, k: (0, j % 4),
                      b_blk=(512, 128), b_idx=lambda i, j, k: (j, 0), dims=_NT)


def _s5_y_dw(h2, dyc):
    return _mm_blocks("mm_s5_y_dw", h2, dyc, grid=(8, 1, 2), out_blk=(512, 128), a_blk=(1024, 512), a_idx=lambda i, j, k: (k, i),
                      b_blk=(1024, 128), b_idx=lambda i, j, k: (k, i % 4), dims=_TN)


def _pool_dense(pw):
    eye = jnp.eye(4, dtype=pw.dtype)
    return jnp.einsum("gcd,gk->gckd", pw, eye).reshape(1024, 1024)


def _row2(v):
    return v.reshape(1, -1)


def _even_fwd(x, W, i, zero, rest_of_weights):
    hn = _rw_fwd("rms_fwd", _f_rms, [_cols(x)], [_par(_row2(W["norm_ab"][i]) + zero)], [(D_MODEL, BF16, 1)], 256)[0]
    z = _mm("mm_in_ab", hn, W["w_in_ab"][i])
    o = _attn_fwd("attn_self_fwd", z, z, z, out_dtype=F32, **_SELF)
    zero = rest_of_weights(o)
    pooled = _pool("pool_fwd", z, 4096, False, BF16)
    wp = _pool_dense(W["pool_w"][i])
    mixed = _mm("mm_pool", pooled, wp)
    scale = _row2(W["pool_scale"][i]) + zero
    ab = _rw_fwd("gate_ab_fwd", _f_gate_ab, [_cols(o), _cols(z, 3072, 1024), _cols(mixed), _cols(z, 5120, 1024)], [_par(scale)],
                 [(2048, BF16, 2)], 256)[0]
    x1 = _mm("mm_out_ab", ab, W["w_out_ab"][i], res=x)
    return x1, dict(x=x, hn=hn, z=z, o=o, pooled=pooled, wp=wp, mixed=mixed, ab=ab)


def _even_bwd(dx1, sv, W, G, i, send):
    x, hn, z = sv["x"], sv["hn"], sv["z"]
    dab = _mm("mm_out_ab_dx", dx1, W["w_out_ab"][i], tb=True)
    G["w_out_ab"][i] = _mm("mm_out_ab_dw", sv["ab"], dx1, ta=True, out_dtype=BF16)
    scale = _row2(W["pool_scale"][i]) + send(0)
    (do, dga, dmixed, dgb), (dscale,) = _rw_bwd(
        "gate_ab_bwd", _f_gate_ab, [_cols(sv["o"]), _cols(z, 3072, 1024), _cols(sv["mixed"]), _cols(z, 5120, 1024)], [_par(scale)],
        [_cols(dab, nsplit=2)], [(0, F32), (1, BF16), (2, BF16), (3, BF16)], [0], 256)
    G["pool_scale"][i] = dscale.reshape(-1)
    dpooled = _mm("mm_pool_dx", dmixed, sv["wp"], tb=True)
    dwp = _mm("mm_pool_dw", sv["pooled"], dmixed, ta=True, out_dtype=BF16)
    G["pool_w"][i] = jnp.stack([dwp[g * 256:(g + 1) * 256, g * 256:(g + 1) * 256] for g in range(4)])
    dvb = _pool("pool_bwd", dpooled, 0, True, BF16)
    dq, dk, dv = _attn_bwd("attn_self_bwd", z, z, z, do, out_dtype=BF16, **_SELF)
    dz = jnp.concatenate([dq, dk, dv, dga, dvb, dgb], axis=1)
    dhn = _mm("mm_in_ab_dx", dz, W["w_in_ab"][i], tb=True)
    G["w_in_ab"][i] = _mm("mm_in_ab_dw", hn, dz, ta=True, out_dtype=BF16, out_stack=W["w_in_ab"][i].shape[2])
    g = _row2(W["norm_ab"][i]) + send(1)
    (dx,), (dg,) = _rw_bwd("rms_bwd", _f_rms_res, [_cols(x)], [_par(g)], [_cols(dx1), _cols(dhn)], [(0, F32)], [0], 256)
    G["norm_ab"][i] = dg.reshape(-1)
    return dx


def _odd_fwd(x, W, i, zero):
    hn = _rw_fwd("rms_fwd", _f_rms, [_cols(x)], [_par(_row2(W["norm_cd"][i]) + zero)], [(D_MODEL, BF16, 1)], 256)[0]
    z = _mm("mm_in_cd", hn, W["w_in_cd"][i])
    sgu_p = [_par(_row2(W["sgu_ln_g"][i]), 4), _par(_row2(W["sgu_ln_b"][i]), 4), _par(W["sgu_w"][i], 4), _par(W["sgu_b"][i][..., None], 4)]
    c_out = _rw_fwd("sgu_fwd", _f_sgu, [_cols(z, 0, 1024, 4), _cols(z, 1024, 1024, 4), _cols(z, 2048, 1024, 4)], sgu_p, [(1024, BF16, 4)], 128)[0]
    prep_rows = [_cols(W["s5_a_re"][i]), _cols(W["s5_a_im"][i]), _cols(W["s5_log_dt"][i].reshape(S5_GROUPS, 1))]
    abar_re, abar_im, coef_re, coef_im = _rw_fwd("s5_prep_fwd", _f_s5_prep, prep_rows, [], [(S5_STATE, F32, 1)] * 4, S5_GROUPS)
    bb_rows = [_cols(W["s5_b_re"][i].reshape(2048, 16)), _cols(W["s5_b_im"][i].reshape(2048, 16)), _cols(coef_re.reshape(2048, 1)), _cols(coef_im.reshape(2048, 1))]
    bb_re, bb_im = _rw_fwd("s5_bbar_fwd", _f_bbar, bb_rows, [], [(16, F32, 1)] * 2, 256)
    bd = _bdense(bb_re, bb_im).astype(BF16)
    cf = _cdense(W["s5_c_re"][i], W["s5_c_im"][i]).astype(BF16)
    a2 = jnp.concatenate([abar_re.reshape(16, 128), abar_im.reshape(16, 128)], axis=0)
    bu = _s5_bu(z, bd)
    h3 = _scan_fwd(bu.reshape(SEQ, 32, 128), a2)
    h2 = h3.reshape(SEQ, 4096)
    yc = _s5_y(h2, cf)
    dpar = _row2(W["s5_d"][i])
    yg = _rw_fwd("gelu_fwd", _f_gelu_y, [_cols(yc), _cols(z, 3072, 512)], [_par(dpar)], [(512, BF16, 1)], 256)[0]
    w12 = jnp.concatenate([W["glu_w1"][i], W["glu_w2"][i]], axis=1)
    t12 = _mm("mm_glu", yg, w12)
    d_out = _rw_fwd("glu_gate_fwd", _f_glu_gate, [_cols(t12, nsplit=2), _cols(z, 3584, 512)], [], [(512, BF16, 1)], 256)[0]
    cd = jnp.concatenate([c_out, d_out], axis=1)
    x1 = _mm("mm_out_cd", cd, W["w_out_cd"][i], res=x)
    sv = dict(x=x, hn=hn, z=z, sgu_p=sgu_p, prep_rows=prep_rows, bb_rows=bb_rows, bb=(bb_re, bb_im), bd=bd, cf=cf, a2=a2,
              h3=h3, h2=h2, yc=yc, dpar=dpar, yg=yg, w12=w12, t12=t12, cd=cd)
    return x1, sv


def _odd_bwd(dx1, sv, W, G, i, send):
    x, hn, z = sv["x"], sv["hn"], sv["z"]
    dcd = _mm("mm_out_cd_dx", dx1, W["w_out_cd"][i], tb=True)
    G["w_out_cd"][i] = _mm("mm_out_cd_dw", sv["cd"], dx1, ta=True, out_dtype=BF16)
    (du, dv, dgc), (dlg, dlb, dsw, dsb) = _rw_bwd(
        "sgu_bwd", _f_sgu, [_cols(z, 0, 1024, 4), _cols(z, 1024, 1024, 4), _cols(z, 2048, 1024, 4)], sv["sgu_p"],
        [_cols(dcd, 0, 1024, 4)], [(0, BF16), (1, BF16), (2, BF16)], [0, 1, 2, 3], 128)
    G["sgu_ln_g"][i], G["sgu_ln_b"][i] = dlg.reshape(-1), dlb.reshape(-1)
    G["sgu_w"][i], G["sgu_b"][i] = dsw, dsb[..., 0]
    (dt12, dgd), _ = _rw_bwd("glu_gate_bwd", _f_glu_gate, [_cols(sv["t12"], nsplit=2), _cols(z, 3584, 512)], [], [_cols(dcd, 1024, 512)],
                             [(0, BF16), (1, BF16)], [], 256)
    dyg = _mm("mm_glu_dx", dt12, sv["w12"], tb=True)
    dw12 = _mm("mm_glu_dw", sv["yg"], dt12, ta=True, out_dtype=BF16)
    G["glu_w1"][i], G["glu_w2"][i] = dw12[:, :512], dw12[:, 512:]
    (dyc, duf1), (dd,) = _rw_bwd("gelu_bwd", _f_gelu_y, [_cols(sv["yc"]), _cols(z, 3072, 512)], [_par(sv["dpar"])], [_cols(dyg)],
                                 [(0, BF16), (1, F32)], [0], 256)
    G["s5_d"][i] = dd.reshape(-1)
    dh2 = _s5_y_dx(dyc, sv["cf"])
    dcf = _s5_y_dw(sv["h2"], dyc)
    _, cvjp = jax.vjp(_cdense, W["s5_c_re"][i], W["s5_c_im"][i])
    G["s5_c_re"][i], G["s5_c_im"][i] = cvjp(dcf)
    dbu3, da2 = _scan_bwd(dh2.reshape(SEQ, 32, 128), sv["h3"], sv["a2"])
    dbu = dbu3.reshape(SEQ, 4096)
    duf2 = _s5_bu_dx(dbu, sv["bd"])
    dbd = _s5_bu_dw(z, dbu)
    _, bvjp = jax.vjp(_bdense, *sv["bb"])
    dbb_re, dbb_im = bvjp(dbd)
    (dbr, dbi, dcr, dci), _ = _rw_bwd("s5_bbar_bwd", _f_bbar, sv["bb_rows"], [], [_cols(dbb_re), _cols(dbb_im)],
                                      [(0, F32), (1, F32), (2, F32), (3, F32)], [], 256)
    G["s5_b_re"][i], G["s5_b_im"][i] = dbr.reshape(S5_GROUPS, S5_STATE, S5_GROUP_DIM), dbi.reshape(S5_GROUPS, S5_STATE, S5_GROUP_DIM)
    douts = [_cols(da2[0:16].reshape(S5_GROUPS, S5_STATE)), _cols(da2[16:32].reshape(S5_GROUPS, S5_STATE)),
             _cols(dcr.reshape(S5_GROUPS, S5_STATE)), _cols(dci.reshape(S5_GROUPS, S5_STATE))]
    (dar, dai, dldt), _ = _rw_bwd("s5_prep_bwd", _f_s5_prep, sv["prep_rows"], [], douts, [(0, F32), (1, F32), (2, F32)], [], S5_GROUPS)
    G["s5_a_re"][i], G["s5_a_im"][i], G["s5_log_dt"][i] = dar, dai, dldt.reshape(-1)
    dxd = (duf1 + duf2).astype(BF16)
    dz = jnp.concatenate([du, dv, dgc, dxd, dgd], axis=1)
    dhn = _mm("mm_in_cd_dx", dz, W["w_in_cd"][i], tb=True)
    G["w_in_cd"][i] = _mm("mm_in_cd_dw", hn, dz, ta=True, out_dtype=BF16, out_stack=W["w_in_cd"][i].shape[2])
    g = _row2(W["norm_cd"][i]) + send(0)
    (dx,), (dg,) = _rw_bwd("rms_bwd", _f_rms_res, [_cols(x)], [_par(g)], [_cols(dx1), _cols(dhn)], [(0, F32)], [0], 256)
    G["norm_cd"][i] = dg.reshape(-1)
    return dx


def _cross_fwd(x1, mem_n, W, l):
    hx = _rw_fwd("rms_fwd", _f_rms, [_cols(x1)], [_par(_row2(W["norm_x"][l]))], [(D_MODEL, BF16, 1)], 256)[0]
    qx = _mm("mm_xq", hx, W["w_xq"][l], out_dtype=BF16)
    kv = _mm("mm_xkv", mem_n, W["w_xkv"][l], out_dtype=BF16)
    ox = _attn_fwd("attn_cross_fwd", qx, kv, kv, out_dtype=BF16, **_CROSS)
    x2 = _mm("mm_xo", ox, W["w_xo"][l], res=x1)
    return x2, dict(x1=x1, hx=hx, qx=qx, kv=kv, ox=ox)


def _cross_bwd(dx2, dmem_n, sv, mem_n, W, G, l, zero):
    dox = _mm("mm_xo_dx", dx2, W["w_xo"][l], tb=True, out_dtype=BF16)
    G["w_xo"][l] = _mm("mm_xo_dw", sv["ox"], dx2, ta=True, out_dtype=BF16)
    dqx, dk, dv = _attn_bwd("attn_cross_bwd", sv["qx"], sv["kv"], sv["kv"], dox, out_dtype=BF16, **_CROSS)
    dkv = jnp.concatenate([dk, dv], axis=1)
    dhx = _mm("mm_xq_dx", dqx, W["w_xq"][l], tb=True)
    G["w_xq"][l] = _mm("mm_xq_dw", sv["hx"], dqx, ta=True, out_dtype=BF16)
    dmem_n = _mm("mm_xkv_dx", dkv, W["w_xkv"][l], tb=True, res=dmem_n)
    G["w_xkv"][l] = _mm("mm_xkv_dw", mem_n, dkv, ta=True, out_dtype=BF16, out_stack=W["w_xkv"][l].shape[2])
    (dx1,), (dg,) = _rw_bwd("rms_bwd", _f_rms_res, [_cols(sv["x1"])], [_par(_row2(W["norm_x"][l]) + zero)], [_cols(dx2), _cols(dhx)], [(0, F32)], [0], 256)
    G["norm_x"][l] = dg.reshape(-1)
    return dx1, dmem_n


_PER_LAYER = ("pool_scale", "norm_ab", "norm_cd", "sgu_ln_g", "sgu_ln_b", "sgu_w", "sgu_b", "s5_d", "s5_c_re", "s5_c_im", "s5_b_re", "s5_b_im",
              "s5_a_re", "s5_a_im", "s5_log_dt", "w_in_ab", "pool_w", "w_out_ab", "w_in_cd", "glu_w1", "glu_w2", "w_out_cd")


def _local_step(x, mem, target, W, weights_of, send_grads, after_layer):
    G = {k: [None, None] for k in _PER_LAYER}
    for k in ("norm_x", "w_xq", "w_xkv", "w_xo"):
        G[k] = [None] * DEPTH
    mem_rows = [_cols(mem)]
    mem_par = [_par(_row2(W["mem_norm"]))]
    mem_n = _rw_fwd("rms_fwd_mem", _f_rms, mem_rows, mem_par, [(D_MODEL, BF16, 1)], 256)[0]
    saved = []
    for layer in range(DEPTH):
        zero = weights_of(layer, 0, x if layer else mem_n)
        if layer % 2 == 0:
            x, sv = _even_fwd(x, W, layer // 2, zero, functools.partial(weights_of, layer, 1))
        else:
            x, sv = _odd_fwd(x, W, layer // 2, zero)
        x, svx = _cross_fwd(x, mem_n, W, layer)
        saved.append((sv, svx))
    loss, dx, dfinal = _loss_head(x, target, _row2(W["final_norm"]))
    G["final_norm"] = dfinal.reshape(-1)
    dmem_n, zero = None, 0.0
    for layer in reversed(range(DEPTH)):
        sv, svx = saved[layer]
        dx, dmem_n = _cross_bwd(dx, dmem_n, svx, mem_n, W, G, layer, zero)
        hook = functools.partial(send_grads, layer, G)
        dx = _even_bwd(dx, sv, W, G, layer // 2, hook) if layer % 2 == 0 else _odd_bwd(dx, sv, W, G, layer // 2, hook)
        zero = after_layer(layer, G)
    _, (dmn,) = _rw_bwd("rms_bwd_mem", _f_rms, mem_rows, mem_par, [_cols(dmem_n)], [], [0], 256)
    G["mem_norm"] = dmn.reshape(-1)
    return loss, dx, G


_HBM = pl.BlockSpec(memory_space=pltpu.HBM)
_ANY = pl.BlockSpec(memory_space=pl.ANY)
_SEM = pl.BlockSpec(memory_space=pltpu.SEMAPHORE)
_N_PEERS = N_DEV - 1


def _mesh_pos():
    return lax.axis_index("x"), lax.axis_index("y"), lax.axis_index("c")


def _peer(pos, k):
    x, y, c = pos
    return (x ^ ((k >> 2) & 1), y ^ ((k >> 1) & 1), c ^ (k & 1))


def _lin(pos):
    return 4 * pos[0] + 2 * pos[1] + pos[2]


def _ends(gather, srcs, lands, t, sender, receiver):
    if gather:
        return lands[t].at[sender], lands[t].at[sender]
    whole = len(srcs[t].shape) != len(lands[t].shape)
    return (srcs[t] if whole else srcs[t].at[receiver]), lands[t].at[sender]


def _into_slot(name, b2, r0, r, me, dtype, after):
    c = b2.shape[1]
    tr = _row_block(r, c, 2 << 20)
    assert r0 % tr == 0

    def body(me_ref, x_ref, *rest):
        rest[-1][...] = x_ref[...].astype(dtype)

    extra = [] if after is None else [after]
    grid_spec = pltpu.PrefetchScalarGridSpec(
        num_scalar_prefetch=1, grid=(r // tr,),
        in_specs=[pl.BlockSpec((tr, c), lambda i, me, o=r0 // tr: (o + i, 0))] + [_ANY] * len(extra),
        out_specs=pl.BlockSpec((None, tr, c), lambda i, me: (me[0], i, 0)))
    return pl.pallas_call(
        body, name=name, grid_spec=grid_spec, out_shape=jax.ShapeDtypeStruct((N_DEV, r, c), dtype),
        compiler_params=pltpu.CompilerParams(dimension_semantics=("arbitrary",), vmem_limit_bytes=V7X_VMEM_LIMIT_BYTES),
        interpret=False,
    )(me, b2, *extra)


def _exchange_start(name, gather, srcs, lands, after=None):
    ns, nt = len(srcs), len(lands)
    arrs = list(srcs) + list(lands)
    extra = [] if after is None else [after]

    def body(*refs):
        ins, lnd = refs[:ns], refs[ns:ns + nt]
        refs = refs[len(extra):]
        send_sems, recv_sems = refs[ns + nt], refs[ns + nt + 1]
        token = refs[-1]
        pos = _mesh_pos()
        me = _lin(pos)
        for k in range(1, N_DEV):
            peer = _peer(pos, k)
            for t in range(nt):
                src, dst = _ends(gather, ins, lnd, t, me, _lin(peer))
                pltpu.make_async_remote_copy(
                    src_ref=src, dst_ref=dst, send_sem=send_sems.at[t * _N_PEERS + k - 1], recv_sem=recv_sems.at[t * _N_PEERS + k - 1],
                    device_id=peer, device_id_type=pl.DeviceIdType.MESH).start()
        token[...] = jnp.zeros_like(token)

    out = pl.pallas_call(
        body, name=name,
        out_shape=(pltpu.SemaphoreType.DMA((nt * _N_PEERS,)), pltpu.SemaphoreType.DMA((nt * _N_PEERS,)), *[pltpu.HBM(a.shape, a.dtype) for a in arrs],
                   jax.ShapeDtypeStruct((8, 128), F32)),
        in_specs=[_HBM] * (ns + nt) + [_ANY] * len(extra), out_specs=(_SEM, _SEM, *[_HBM] * (ns + nt), pl.BlockSpec(memory_space=pltpu.VMEM)),
        input_output_aliases={j: 2 + j for j in range(ns + nt)},
        compiler_params=pltpu.CompilerParams(has_side_effects=pltpu.SideEffectType.DATAFLOW_SIDE_EFFECTING),
        interpret=False,
    )(*[pltpu.with_memory_space_constraint(a, pltpu.HBM) for a in arrs], *extra)
    return dict(send=out[0], recv=out[1], srcs=list(out[2:2 + ns]), lands=list(out[2 + ns:2 + ns + nt]), token=out[-1][0, 0], token_arr=out[-1], gather=gather)


def _exchange_wait(name, ex, after):
    ns, nt = len(ex["srcs"]), len(ex["lands"])
    gather = ex["gather"]
    arrs = ex["srcs"] + ex["lands"]

    def body(*refs):
        ins, lnd = refs[:ns], refs[ns:ns + nt]
        send_sems, recv_sems = refs[ns + nt], refs[ns + nt + 1]
        pos = _mesh_pos()
        me = _lin(pos)
        for k in range(1, N_DEV):
            peer = _peer(pos, k)
            for t in range(nt):
                src, _ = _ends(gather, ins, lnd, t, me, _lin(peer))
                _, dst = _ends(gather, ins, lnd, t, _lin(peer), me)
                cp = pltpu.make_async_remote_copy(
                    src_ref=src, dst_ref=dst, send_sem=send_sems.at[t * _N_PEERS + k - 1], recv_sem=recv_sems.at[t * _N_PEERS + k - 1],
                    device_id=peer, device_id_type=pl.DeviceIdType.MESH)
                cp.wait_send()
                cp.wait_recv()

    out = pl.pallas_call(
        body, name=name, out_shape=tuple(pltpu.HBM(a.shape, a.dtype) for a in arrs),
        in_specs=[_HBM] * (ns + nt) + [_SEM, _SEM, _ANY], out_specs=tuple([_HBM] * (ns + nt)),
        input_output_aliases={j: j for j in range(ns + nt)},
        compiler_params=pltpu.CompilerParams(has_side_effects=pltpu.SideEffectType.DATAFLOW_SIDE_EFFECTING),
        interpret=False,
    )(*arrs, ex["send"], ex["recv"], after)
    return list(out[:ns]), list(out[ns:])


def _scatter_begin(name, srcs):
    lands = [lax.empty(s.shape if s.ndim == 3 else (N_DEV,) + s.shape, s.dtype) for s in srcs]
    return _exchange_start(name, False, srcs, lands)


def _adam(name, w, m, v, parts, own, me, layer, bufs):
    r, c = parts.shape[1:]
    tr = _row_block(r, c, 1 << 20)
    nb = r // tr

    def body(me_ref, w_ref, m_ref, v_ref, p_ref, own_ref, *rest):
        g_ref, d_ref, nm_ref, nv_ref, acc = rest[-5:]
        acc[...] = jnp.zeros_like(acc)
        for k in range(N_DEV):
            @pl.when(me_ref[0] == k)
            def _():
                acc[...] += own_ref[...].astype(F32)

            @pl.when(me_ref[0] != k)
            def _(k=k):
                acc[...] += p_ref[k].astype(F32)

        g = acc[...]
        mm = ADAM_B1 * m_ref[...] + (1.0 - ADAM_B1) * g
        vv = ADAM_B2 * v_ref[...] + (1.0 - ADAM_B2) * jnp.square(g)
        m_hat = mm / (1.0 - ADAM_B1 ** ADAM_STEP)
        v_hat = vv / (1.0 - ADAM_B2 ** ADAM_STEP)
        g_ref[...] = g
        d_ref[...] = -ADAM_LR * (m_hat / (jnp.sqrt(v_hat) + ADAM_EPS) + ADAM_WD * w_ref[...])
        nm_ref[...] = mm
        nv_ref[...] = vv

    blk = pl.BlockSpec((tr, c), lambda i, me, o=layer * nb: (o + i, 0))
    own_spec = pl.BlockSpec((None, tr, c), lambda i, me: (me[0], i, 0)) if own.ndim == 3 else pl.BlockSpec((tr, c), lambda i, me: (i, 0))
    in_specs = [blk, blk, blk, pl.BlockSpec((N_DEV, tr, c), lambda i, me: (0, i, 0)), own_spec]
    args = [me, w, m, v, parts, own]
    aliases = {}
    if bufs is not None:
        in_specs += [_ANY] * 4
        aliases = {len(args) + j: j for j in range(4)}
        args += list(bufs)
    grid_spec = pltpu.PrefetchScalarGridSpec(
        num_scalar_prefetch=1, grid=(nb,), in_specs=in_specs, out_specs=[blk] * 4, scratch_shapes=[pltpu.VMEM((tr, c), F32)])
    return pl.pallas_call(
        body, name=name, grid_spec=grid_spec, out_shape=[jax.ShapeDtypeStruct(w.shape, F32)] * 4, input_output_aliases=aliases,
        compiler_params=pltpu.CompilerParams(dimension_semantics=("arbitrary",), vmem_limit_bytes=V7X_VMEM_LIMIT_BYTES),
        interpret=False,
    )(*args)


def _pack(arrs):
    out = []
    for a in arrs:
        f = a.reshape(-1)
        out.append(jnp.pad(f, (0, (-f.shape[0]) % 1024)).reshape(-1, 128))
    rows = sum(a.shape[0] for a in out)
    out.append(jnp.zeros(((-rows) % 256, 128), out[0].dtype))
    return jnp.concatenate(out, axis=0)


def _unpack(packed, shapes):
    out, r = [], 0
    for s in shapes:
        n = math.prod(s)
        rows = (n + 1023) // 1024 * 8
        out.append(packed[r:r + rows].reshape(-1)[:n].reshape(s))
        r += rows
    return out


def _row_block(r, c, limit):
    best = None
    for tr in range(16, r + 1, 16):
        if r % tr == 0 and tr * c * 4 <= limit:
            best = tr
    return r if best is None else best


_BIG = ("w_in_ab", "pool_w", "w_out_ab", "w_in_cd", "glu_w1", "glu_w2", "w_out_cd", "w_xq", "w_xkv", "w_xo")
_STACKED = ("w_in_ab", "w_in_cd", "w_xkv")
_MIXER_BIG = (("w_in_ab", "pool_w", "w_out_ab"), ("w_in_cd", "glu_w1", "glu_w2", "w_out_cd"))
_CROSS_BIG = ("w_xq", "w_xkv", "w_xo")
_SMALL_SPLIT = ["norm_cd", "sgu_ln_g", "sgu_ln_b", "s5_d"]
_REPLICATED_ODD = ["sgu_w", "sgu_b", "s5_a_re", "s5_a_im", "s5_log_dt", "s5_b_re", "s5_b_im", "s5_c_re", "s5_c_im", "final_norm"]
_REPLICATED_EVEN = ["norm_ab", "pool_scale", "norm_x", "mem_norm"]
_REPLICATED = _REPLICATED_ODD + _REPLICATED_EVEN
_WEIGHTS = ["norm_ab", "w_in_ab", "pool_w", "pool_scale", "w_out_ab", "norm_cd", "w_in_cd", "sgu_ln_g", "sgu_ln_b", "sgu_w", "sgu_b", "s5_a_re",
            "s5_a_im", "s5_log_dt", "s5_b_re", "s5_b_im", "s5_c_re", "s5_c_im", "s5_d", "glu_w1", "glu_w2", "w_out_cd", "norm_x", "w_xq",
            "w_xkv", "w_xo", "mem_norm", "final_norm"]


def _layer_big(layer):
    return [(n, layer // 2) for n in _MIXER_BIG[layer % 2]] + [(n, layer) for n in _CROSS_BIG]


def _gather_parts(layer):
    big = _layer_big(layer)
    return [big[:1], big[1:]] if layer % 2 == 0 else [big]


def _scatter_parts(layer):
    big = _layer_big(layer)
    return [big[2:], big[:2]] if layer % 2 == 0 else [big]


def _from_slots(name, a):
    if name in _STACKED:
        return a
    if name == "pool_w":
        return a.reshape(N_DEV, 4, 32, 256).transpose(1, 0, 2, 3).reshape(4, 256, 256)
    return a.reshape(-1, a.shape[-1])


def _to_slots(name, g):
    if name in _STACKED:
        return g
    if name == "pool_w":
        return g.reshape(4, N_DEV, 32, 256).transpose(1, 0, 2, 3).reshape(N_DEV, 128, 256)
    return g.reshape(N_DEV, -1, g.shape[-1])


def _rows2d(a):
    return a.reshape(-1, a.shape[-1])


def _small_rows(block):
    return jnp.pad(block, ((0, 0), (0, 128 - block.shape[1])))


def kernel(x, mem, norm_ab, w_in_ab, pool_w, pool_scale, w_out_ab, norm_cd, w_in_cd, sgu_ln_g, sgu_ln_b, sgu_w, sgu_b, s5_a_re, s5_a_im, s5_log_dt, s5_b_re, s5_b_im, s5_c_re, s5_c_im, s5_d, glu_w1, glu_w2, w_out_cd, norm_x, w_xq, w_xkv, w_xo, mem_norm, final_norm, loss_target, m_norm_ab, m_w_in_ab, m_pool_w, m_pool_scale, m_w_out_ab, m_norm_cd, m_w_in_cd, m_sgu_ln_g, m_sgu_ln_b, m_sgu_w, m_sgu_b, m_s5_a_re, m_s5_a_im, m_s5_log_dt, m_s5_b_re, m_s5_b_im, m_s5_c_re, m_s5_c_im, m_s5_d, m_glu_w1, m_glu_w2, m_w_out_cd, m_norm_x, m_w_xq, m_w_xkv, m_w_xo, m_mem_norm, m_final_norm, v_norm_ab, v_w_in_ab, v_pool_w, v_pool_scale, v_w_out_ab, v_norm_cd, v_w_in_cd, v_sgu_ln_g, v_sgu_ln_b, v_sgu_w, v_sgu_b, v_s5_a_re, v_s5_a_im, v_s5_log_dt, v_s5_b_re, v_s5_b_im, v_s5_c_re, v_s5_c_im, v_s5_d, v_glu_w1, v_glu_w2, v_w_out_cd, v_norm_x, v_w_xq, v_w_xkv, v_w_xo, v_mem_norm, v_final_norm):
    args = locals()
    w = {n: args[n] for n in _WEIGHTS}
    m = {n: args["m_" + n] for n in _WEIGHTS}
    v = {n: args["v_" + n] for n in _WEIGHTS}

    me = jnp.reshape(_lin(_mesh_pos()), (1,)).astype(jnp.int32)

    order = [(layer, p) for layer in range(DEPTH) for p in range(len(_gather_parts(layer)))]
    lands, gathers = {}, {}
    for key in order:
        lands[key] = []
        for name, i in _gather_parts(key[0])[key[1]]:
            b2 = _rows2d(w[name])
            r = b2.shape[0] // w[name].shape[0]
            lands[key].append(_into_slot("cast_" + name, b2, i * r, r, me, BF16, None))
    small_blocks = jnp.concatenate([_small_rows(w[n]) for n in _SMALL_SPLIT], axis=0)
    lands[order[0]].append(_into_slot("cast_small", small_blocks, 0, 8, me, F32, None))

    def begin_gather(key, after):
        gathers[key] = _exchange_start("gather%d%s_start" % (key[0], "ab"[key[1]]), True, [], lands[key], after)
        return gathers[key]["token"]

    W = {n: w[n] for n in _REPLICATED}
    W["mem_norm"] = w["mem_norm"] + begin_gather(order[0], None)
    for name in _BIG:
        W[name] = [None] * w[name].shape[0]

    def weights_of(layer, part, after):
        key = (layer, part)
        if key not in gathers:
            return 0.0
        _, got = _exchange_wait("gather%d%s_wait" % (layer, "ab"[part]), gathers[key], after)
        for (name, i), arr in zip(_gather_parts(layer)[part], got):
            W[name][i] = _from_slots(name, arr)
        if key == order[0]:
            sm = got[-1].reshape(N_DEV, 4, 2, 128)
            for j, n in enumerate(_SMALL_SPLIT):
                width = w[n].shape[1]
                W[n] = sm[:, j, :, :width].transpose(1, 0, 2).reshape(2, N_DEV * width)
        k = order.index(key)
        return sum([begin_gather(nxt, got[0]) for nxt in (order[1:3] if k == 0 else order[k + 2:k + 3])], 0.0)

    scatters, small = {}, {}

    def send_grads(layer, G, part):
        srcs = [_to_slots(name, G[name][i]) for name, i in _scatter_parts(layer)[part]]
        scatters[layer, part] = _scatter_begin("scatter%d%s_start" % (layer, "ab"[part]), srcs)
        return scatters[layer, part]["token"]

    def begin_small(tag, G, names, split):
        srcs = [_pack([G[n] if n in ("mem_norm", "final_norm") else jnp.stack(G[n]) for n in names])]
        if split:
            gs = jnp.stack([jnp.stack(G[n]).reshape(2, N_DEV, -1).transpose(1, 0, 2) for n in _SMALL_SPLIT[:3]]
                           + [jnp.pad(jnp.stack(G["s5_d"]).reshape(2, N_DEV, -1).transpose(1, 0, 2), ((0, 0), (0, 0), (0, 64)))], axis=1)
            srcs.append(gs.reshape(N_DEV, 8, 128))
        small[tag] = _scatter_begin("scatter_small_%s_start" % tag, srcs)
        return small[tag]["token"]

    def after_layer(layer, G):
        return begin_small("odd", G, _REPLICATED_ODD, True) if layer == 1 else 0.0

    loss, dx, G = _local_step(x[0], mem[0], loss_target[0], W, weights_of, send_grads, after_layer)
    loss = lax.psum(loss[0, 0], MESH_AXES)
    begin_small("even", G, _REPLICATED_EVEN, False)

    out = {}
    after = small["even"]["token_arr"]
    for layer, part in scatters:
        own, got = _exchange_wait("scatter%d%s_wait" % (layer, "ab"[part]), scatters[layer, part], after)
        for (name, i), mine, parts in zip(_scatter_parts(layer)[part], own, got):
            out[name] = _adam("adam_" + name, _rows2d(w[name]), _rows2d(m[name]), _rows2d(v[name]), parts, mine, me, i, out.get(name))
        after = out[name][0]
    for name in _BIG:
        out[name] = [a.reshape(w[name].shape) for a in out[name]]
    own, got = _exchange_wait("scatter_small_odd_wait", small["odd"], after)
    pk = lambda d: jnp.concatenate([_small_rows(d[n]) for n in _SMALL_SPLIT], axis=0)
    res = _adam("adam_small", pk(w), pk(m), pk(v), got[1], own[1], me, 0, None)
    for j, n in enumerate(_SMALL_SPLIT):
        out[n] = [a[2 * j:2 * j + 2, :w[n].shape[1]] for a in res]
    own_even, got_even = _exchange_wait("scatter_small_even_wait", small["even"], res[0])
    for names, mine, parts in zip((_REPLICATED_ODD, _REPLICATED_EVEN), (own[0], own_even[0]), (got[0], got_even[0])):
        pk = lambda d: _pack([d[n] for n in names])
        res = _adam("adam_replicated", pk(w), pk(m), pk(v), parts, mine, me, 0, None)
        shapes = [w[n].shape for n in names]
        un = [_unpack(a, shapes) for a in res]
        for j, n in enumerate(names):
            out[n] = [un[q][j] for q in range(4)]

    return (loss, dx[None], *[out[n][0] for n in _WEIGHTS], *[out[n][1] for n in _WEIGHTS], *[out[n][2] for n in _WEIGHTS],
            *[out[n][3] for n in _WEIGHTS])
```

```python
import functools
import math

import jax
import jax.numpy as jnp
from jax import lax
from jax.experimental import pallas as pl
from jax.experimental.pallas import tpu as pltpu

F32 = jnp.float32
BF16 = jnp.bfloat16

SEQ = 2048
D_MODEL = 1024
MEM_LEN = 256
DEPTH = 4
N_DEV = 8
EPS = 1e-6
NEG = -1e30
A_HEAD_DIM = 64
X_HEAD_DIM = 256
S5_GROUPS = 32
S5_STATE = 64
S5_GROUP_DIM = 16

ADAM_LR = 0.001
ADAM_B1 = 0.9
ADAM_B2 = 0.999
ADAM_EPS = 1e-08
ADAM_WD = 0.01
ADAM_STEP = 10

V7X_VMEM_LIMIT_BYTES = 56 * 1024 * 1024
_MM_VMEM_BYTES = 36 * 1024 * 1024
MESH_AXES = ("x", "y", "c")


def _pc(body, *, name, out_shape, grid=None, in_specs=None, out_specs=None, scratch_shapes=(), aliases=None, sem=None):
    kw = {}
    if grid is not None:
        kw["grid"] = grid
    if in_specs is not None:
        kw["in_specs"] = in_specs
    if out_specs is not None:
        kw["out_specs"] = out_specs
    if aliases:
        kw["input_output_aliases"] = aliases
    return pl.pallas_call(
        body,
        name=name,
        out_shape=out_shape,
        scratch_shapes=list(scratch_shapes),
        compiler_params=pltpu.CompilerParams(dimension_semantics=sem, vmem_limit_bytes=V7X_VMEM_LIMIT_BYTES),
        interpret=False,
        **kw,
    )


def _cols(arr, c0=0, width=None, nsplit=1, r0=0):
    width = arr.shape[1] - c0 if width is None else width
    assert c0 % width == 0 and width % nsplit == 0
    return (arr, c0, width, nsplit, r0)


def _par(arr, nsplit=1):
    return (arr, nsplit)


def _ld(ref, nsplit):
    if nsplit == 1:
        return ref[...].astype(F32)
    if len(ref.shape) == 3:
        return tuple(ref[k].astype(F32) for k in range(nsplit))
    w = ref.shape[-1] // nsplit
    return tuple(ref[:, k * w:(k + 1) * w].astype(F32) for k in range(nsplit))


def _st(ref, val, nsplit, accumulate=False):
    if nsplit == 1:
        val = (val,)
    for k in range(nsplit):
        if nsplit == 1:
            idx = (Ellipsis,)
        elif len(ref.shape) == 3:
            idx = (k,)
        else:
            w = ref.shape[-1] // nsplit
            idx = (slice(None), slice(k * w, (k + 1) * w))
        if accumulate:
            ref[idx] += val[k].astype(ref.dtype)
        else:
            ref[idx] = val[k].astype(ref.dtype)


def _row_spec(tr, op):
    _, c0, w, _, r0 = op
    assert r0 % tr == 0
    return pl.BlockSpec((tr, w), lambda i, cb=c0 // w, rb=r0 // tr: (i + rb, cb))


def _full_spec(arr):
    return pl.BlockSpec(arr.shape, lambda i, nd=arr.ndim: (0,) * nd)


def _rw_fwd(name, f, rows, pars, outs, tr, n_rows=None):
    n_rows = rows[0][0].shape[0] if n_rows is None else n_rows
    nr, npar = len(rows), len(pars)

    def body(*refs):
        r = [_ld(refs[i], rows[i][3]) for i in range(nr)]
        p = [_ld(refs[nr + i], pars[i][1]) for i in range(npar)]
        res = f(r, p)
        for k, (_, _, ns) in enumerate(outs):
            _st(refs[nr + npar + k], res[k], ns)

    res = _pc(
        body, name=name, grid=(n_rows // tr,),
        in_specs=[_row_spec(tr, op) for op in rows] + [_full_spec(a) for a, _ in pars],
        out_specs=[pl.BlockSpec((tr, w), lambda i: (i, 0)) for w, _, _ in outs],
        out_shape=[jax.ShapeDtypeStruct((n_rows, w), dt) for w, dt, _ in outs],
        sem=("arbitrary",),
    )(*[op[0] for op in rows], *[a for a, _ in pars])
    return list(res)


def _rw_bwd(name, f, rows, pars, douts, drow, dpar, tr):
    n_rows = rows[0][0].shape[0]
    nr, npar = len(rows), len(pars)
    dgiven = [d for d in douts if d is not None]
    nd = len(dgiven)

    def body(*refs):
        r = [_ld(refs[i], rows[i][3]) for i in range(nr)]
        p = [_ld(refs[nr + i], pars[i][1]) for i in range(npar)]
        d = [_ld(refs[nr + npar + i], dgiven[i][3]) for i in range(nd)]
        orefs = refs[nr + npar + nd:]

        def g(dr, dp):
            rr, pp = list(r), list(p)
            for j, (idx, _) in enumerate(drow):
                rr[idx] = dr[j]
            for j, idx in enumerate(dpar):
                pp[idx] = dp[j]
            return tuple(f(rr, pp))

        out, vjp = jax.vjp(g, [r[idx] for idx, _ in drow], [p[idx] for idx in dpar])
        ct, j = [], 0
        for k, o in enumerate(out):
            if douts[k] is None:
                ct.append(jax.tree.map(jnp.zeros_like, o))
            else:
                ct.append(d[j])
                j += 1
        gdr, gdp = vjp(tuple(ct))
        for j, (idx, _) in enumerate(drow):
            _st(orefs[j], gdr[j], rows[idx][3])

        @pl.when(pl.program_id(0) == 0)
        def _():
            for j in range(len(dpar)):
                orefs[len(drow) + j][...] = jnp.zeros_like(orefs[len(drow) + j])

        for j, idx in enumerate(dpar):
            _st(orefs[len(drow) + j], gdp[j], pars[idx][1], accumulate=True)

    res = _pc(
        body, name=name, grid=(n_rows // tr,),
        in_specs=[_row_spec(tr, op) for op in rows] + [_full_spec(a) for a, _ in pars] + [_row_spec(tr, op) for op in dgiven],
        out_specs=[pl.BlockSpec((tr, rows[idx][2]), lambda i: (i, 0)) for idx, _ in drow] + [_full_spec(pars[idx][0]) for idx in dpar],
        out_shape=[jax.ShapeDtypeStruct((n_rows, rows[idx][2]), dt) for idx, dt in drow]
        + [jax.ShapeDtypeStruct(pars[idx][0].shape, F32) for idx in dpar],
        sem=("arbitrary",),
    )(*[op[0] for op in rows], *[a for a, _ in pars], *[op[0] for op in dgiven])
    res = list(res)
    return res[:len(drow)], res[len(drow):]


def _sigmoid(x):
    return jax.nn.sigmoid(x)


def _silu(x):
    return x * _sigmoid(x)


def _rms(x, g):
    return x * lax.rsqrt(jnp.mean(x * x, axis=-1, keepdims=True) + EPS) * g


def _f_rms(r, p):
    return [_rms(r[0], p[0])]


def _f_rms_res(r, p):
    return [r[0], _rms(r[0], p[0])]


def _f_gate_ab(r, p):
    o, ga, mixed, gb = r
    return [(o * _silu(ga), mixed * p[0] * _silu(gb))]


def _f_sgu(r, p):
    u, v, gc = r
    lg, lb, w, b = p
    n = float(D_MODEL)
    mu = sum(jnp.sum(vk, axis=-1, keepdims=True) for vk in v) / n
    var = sum(jnp.sum(jnp.square(vk - mu), axis=-1, keepdims=True) for vk in v) / n
    rs = lax.rsqrt(var + EPS)
    t = w[0].shape[0]
    tri = lax.broadcasted_iota(jnp.int32, (t, t), 0) >= lax.broadcasted_iota(jnp.int32, (t, t), 1)
    outs = []
    for k in range(len(v)):
        vn = (v[k] - mu) * rs * lg[k] + lb[k]
        mixed = jnp.dot(jnp.where(tri, w[k], 0.0), vn, preferred_element_type=F32) + b[k]
        outs.append(u[k] * mixed * _silu(gc[k]))
    return [tuple(outs)]


def _gelu(x):
    return 0.5 * x * (1.0 + jnp.tanh(math.sqrt(2.0 / math.pi) * (x + 0.044715 * (x * x * x))))


def _f_gelu_y(r, p):
    yc, uf = r
    return [_gelu(yc + p[0] * uf)]


def _f_glu_gate(r, p):
    t12, gd = r
    return [t12[0] * _sigmoid(t12[1]) * _silu(gd)]


def _f_s5_prep(r, p):
    ar, ai, ldt = r
    dt = jnp.exp(ldt)
    mag = jnp.exp(dt * ar)
    abar_re = mag * jnp.cos(dt * ai)
    abar_im = mag * jnp.sin(dt * ai)
    nr, ni = abar_re - 1.0, abar_im
    inv = 1.0 / (ar * ar + ai * ai)
    return [abar_re, abar_im, (nr * ar + ni * ai) * inv, (ni * ar - nr * ai) * inv]


def _f_bbar(r, p):
    br, bi, cr, ci = r
    return [cr * br - ci * bi, cr * bi + ci * br]


def _loss_head(x, target, g):
    tr = 256
    n_rows, width = x.shape

    def f(xv, gv, tv):
        err = jnp.square(_rms(xv, gv) - tv)
        return 0.5 * jnp.mean(err, axis=-1, keepdims=True)

    def body(x_ref, t_ref, g_ref, loss_ref, dx_ref, dg_ref):
        @pl.when(pl.program_id(0) == 0)
        def _():
            loss_ref[...] = jnp.zeros_like(loss_ref)
            dg_ref[...] = jnp.zeros_like(dg_ref)

        tv = t_ref[...]
        row_loss, vjp = jax.vjp(lambda a, b: f(a, b, tv), x_ref[...], g_ref[...])
        dx, dg = vjp(jnp.ones_like(row_loss))
        dx_ref[...] = dx
        dg_ref[...] += dg
        loss_ref[...] += jnp.broadcast_to(jnp.sum(row_loss, axis=0, keepdims=True), loss_ref.shape)

    blk = pl.BlockSpec((tr, width), lambda i: (i, 0))
    one = pl.BlockSpec((1, width), lambda i: (0, 0))
    return _pc(
        body, name="loss_head", grid=(n_rows // tr,), in_specs=[blk, blk, one],
        out_specs=[pl.BlockSpec((1, 128), lambda i: (0, 0)), blk, one],
        out_shape=[jax.ShapeDtypeStruct((1, 128), F32), jax.ShapeDtypeStruct(x.shape, F32), jax.ShapeDtypeStruct((1, width), F32)],
        sem=("arbitrary",),
    )(x, target, g)


_NT = (((1,), (1,)), ((), ()))
_TN = (((0,), (0,)), ((), ()))


def _tile(n, cap):
    t = min(n, cap)
    while n % t:
        t -= 128
    assert t > 0
    return t


def _mm(name, a, b, *, ta=False, tb=False, out_dtype=F32, a_off=0, a_width=None, res=None, out_stack=None):
    assert not (ta and tb)
    stacked = b.ndim == 3
    bk, bn = (b.shape[1], b.shape[0] * b.shape[2]) if stacked else b.shape
    if ta:
        kc = a.shape[0]
        m = a.shape[1] - a_off if a_width is None else a_width
        n = bn
        assert bk == kc and not stacked
    else:
        m = a.shape[0]
        kc = a.shape[1] - a_off if a_width is None else a_width
        n = bk if tb else bn
        assert (bn if tb else bk) == kc
    tn = _tile(b.shape[2] if stacked and not tb else (out_stack or n), 1024)
    size = lambda dt: jnp.dtype(dt).itemsize
    for tk_cap, tm_cap in ((2048, 2048), (2048, 1024), (1024, 1024), (1024, 512), (1024, 256)):
        tm, tk = _tile(m, tm_cap), _tile(b.shape[2] if stacked and tb else kc, tk_cap)
        nk = kc // tk
        vmem = 2 * tm * tk * size(a.dtype) + 2 * tk * tn * size(b.dtype) + tm * tn * (2 * size(out_dtype) + (4 if nk > 1 else 0) + (8 if res is not None else 0))
        if vmem <= _MM_VMEM_BYTES:
            break
    if ta:
        assert a_off % tm == 0
        a_spec = pl.BlockSpec((tk, tm), lambda i, j, k, o=a_off // tm: (k, i + o))
        dims = _TN
    else:
        assert a_off % tk == 0
        a_spec = pl.BlockSpec((tm, tk), lambda i, j, k, o=a_off // tk: (i, k + o))
        dims = _NT if tb else (((1,), (0,)), ((), ()))
    if stacked and tb:
        b_spec = pl.BlockSpec((None, tn, tk), lambda i, j, k, q=b.shape[2] // tk: (k // q, j, k % q))
    elif stacked:
        b_spec = pl.BlockSpec((None, tk, tn), lambda i, j, k, q=b.shape[2] // tn: (j // q, k, j % q))
    else:
        b_spec = pl.BlockSpec((tn, tk), lambda i, j, k: (j, k)) if tb else pl.BlockSpec((tk, tn), lambda i, j, k: (k, j))
    if out_stack:
        out_spec = pl.BlockSpec((None, tm, tn), lambda i, j, k, q=out_stack // tn: (j // q, i, j % q))
        out_shape = jax.ShapeDtypeStruct((n // out_stack, m, out_stack), out_dtype)
    else:
        out_spec = pl.BlockSpec((tm, tn), lambda i, j, k: (i, j))
        out_shape = jax.ShapeDtypeStruct((m, n), out_dtype)
    in_specs, args = [a_spec, b_spec], [a, b]
    has_res = res is not None
    if has_res:
        in_specs.append(pl.BlockSpec((tm, tn), lambda i, j, k: (i, j)))
        args.append(res)

    def finish(refs, acc):
        if has_res:
            acc = acc + refs[2][...].astype(F32)
        refs[3 if has_res else 2][...] = acc.astype(out_dtype)

    def body_one(*refs):
        finish(refs, lax.dot_general(refs[0][...].astype(BF16), refs[1][...].astype(BF16), dims, preferred_element_type=F32))

    def body(*refs):
        acc_ref = refs[-1]
        k = pl.program_id(2)

        @pl.when(k == 0)
        def _():
            acc_ref[...] = jnp.zeros_like(acc_ref)

        acc_ref[...] += lax.dot_general(refs[0][...].astype(BF16), refs[1][...].astype(BF16), dims, preferred_element_type=F32)

        @pl.when(k == nk - 1)
        def _():
            finish(refs, acc_ref[...])

    return _pc(
        body_one if nk == 1 else body, name=name, grid=(m // tm, n // tn, nk), in_specs=in_specs, out_specs=out_spec, out_shape=out_shape,
        scratch_shapes=[] if nk == 1 else [pltpu.VMEM((tm, tn), F32)], sem=("parallel", "parallel", "arbitrary"),
    )(*args)


def _mm_blocks(name, a, b, *, grid, out_blk, a_blk, a_idx, b_blk, b_idx, dims, out_dtype=F32):
    gi, gj, nk = grid

    def body(a_ref, b_ref, o_ref, acc_ref):
        k = pl.program_id(2)

        @pl.when(k == 0)
        def _():
            acc_ref[...] = jnp.zeros_like(acc_ref)

        acc_ref[...] += lax.dot_general(a_ref[...].astype(BF16), b_ref[...].astype(BF16), dims, preferred_element_type=F32)

        @pl.when(k == nk - 1)
        def _():
            o_ref[...] = acc_ref[...].astype(o_ref.dtype)

    return _pc(
        body, name=name, grid=grid, in_specs=[pl.BlockSpec(a_blk, a_idx), pl.BlockSpec(b_blk, b_idx)],
        out_specs=pl.BlockSpec(out_blk, lambda i, j, k: (i, j)), out_shape=jax.ShapeDtypeStruct((gi * out_blk[0], gj * out_blk[1]), out_dtype),
        scratch_shapes=[pltpu.VMEM(out_blk, F32)], sem=("parallel", "parallel", "arbitrary"),
    )(a, b)


def _head_masks(width, nsub):
    lane = lax.broadcasted_iota(jnp.int32, (1, width), 1)
    hd = width // nsub
    return [(lane >= h * hd) & (lane < (h + 1) * hd) for h in range(nsub)]


def _dilated_log_count(row0, tq, ext):
    delta = (row0 + lax.broadcasted_iota(jnp.int32, (tq, ext), 0)) - lax.broadcasted_iota(jnp.int32, (tq, ext), 1)
    cnt = (delta <= 128).astype(jnp.int32) + (((delta & 3) == 0) & (delta <= 512)).astype(jnp.int32) + ((delta & 15) == 0).astype(jnp.int32)
    logc = jnp.where(cnt == 3, math.log(3.0), jnp.where(cnt == 2, math.log(2.0), 0.0))
    return jnp.where((delta >= 0) & (cnt > 0), logc, NEG)


def _bias_table(tab, nq, tq):
    @pl.when(pl.program_id(0) == 0)
    def _():
        for d in range(nq):
            tab[d] = _dilated_log_count(d * tq, tq, tq)


def _scores(q, ke, tab, r, masks, h):
    qm = (jnp.where(masks[h], q, 0.0) if len(masks) > 1 else q).astype(BF16)
    s = lax.dot_general(qm, ke, _NT, preferred_element_type=F32)
    if tab is not None:
        s = s + jnp.concatenate([tab[r - c] for c in range(r + 1)], axis=1)
    p = jnp.exp(s - jnp.max(s, axis=-1, keepdims=True))
    return qm, p, 1.0 / jnp.sum(p, axis=-1, keepdims=True)


def _attn_fwd(name, qa, ka, va, *, qc, kc, vc, width, nblk, nsub, causal, tq, scale, out_dtype):
    sq, t_len = qa.shape[0], ka.shape[0]
    nq = sq // tq

    def body(q_ref, k_ref, v_ref, o_ref, *scratch):
        tab = scratch[0] if causal else None
        if causal:
            _bias_table(tab, nq, tq)
        kb = k_ref[...].astype(BF16)
        vb = v_ref[...].astype(BF16)
        masks = _head_masks(width, nsub)
        for r in range(nq):
            ext = (r + 1) * tq if causal else t_len
            q = q_ref[r * tq:(r + 1) * tq, :].astype(F32) * scale
            ke, ve = kb[:ext], vb[:ext]
            o = None
            for h in range(nsub):
                _, p, inv = _scores(q, ke, tab, r, masks, h)
                oh = jnp.dot(p.astype(BF16), ve, preferred_element_type=F32) * inv
                o = oh if o is None else jnp.where(masks[h], oh, o)
            o_ref[r * tq:(r + 1) * tq, :] = o.astype(o_ref.dtype)

    return _pc(
        body, name=name, grid=(nblk,),
        in_specs=[pl.BlockSpec((sq, width), lambda i, c=qc: (0, c + i)), pl.BlockSpec((t_len, width), lambda i, c=kc: (0, c + i)),
                  pl.BlockSpec((t_len, width), lambda i, c=vc: (0, c + i))],
        out_specs=pl.BlockSpec((sq, width), lambda i: (0, i)),
        out_shape=jax.ShapeDtypeStruct((sq, nblk * width), out_dtype),
        scratch_shapes=[pltpu.VMEM((nq, tq, tq), F32)] if causal else [], sem=("arbitrary",),
    )(qa, ka, va)


def _attn_bwd(name, qa, ka, va, doa, *, qc, kc, vc, width, nblk, nsub, causal, tq, scale, out_dtype):
    sq, t_len = qa.shape[0], ka.shape[0]

    nq = sq // tq

    def body(q_ref, k_ref, v_ref, do_ref, dq_ref, dk_ref, dv_ref, dk_acc, dv_acc, *scratch):
        tab = scratch[0] if causal else None
        if causal:
            _bias_table(tab, nq, tq)
        kb = k_ref[...].astype(BF16)
        vb = v_ref[...].astype(BF16)
        masks = _head_masks(width, nsub)
        dk_acc[...] = jnp.zeros_like(dk_acc)
        dv_acc[...] = jnp.zeros_like(dv_acc)
        for r in range(nq):
            ext = (r + 1) * tq if causal else t_len
            q = q_ref[r * tq:(r + 1) * tq, :].astype(F32) * scale
            do = do_ref[r * tq:(r + 1) * tq, :].astype(F32)
            ke, ve = kb[:ext], vb[:ext]
            dq = None
            for h in range(nsub):
                qm, p, inv = _scores(q, ke, tab, r, masks, h)
                dom = (jnp.where(masks[h], do, 0.0) if nsub > 1 else do).astype(BF16)
                pn = p * inv
                dpn = lax.dot_general(dom, ve, _NT, preferred_element_type=F32)
                dsb = (pn * (dpn - jnp.sum(pn * dpn, axis=-1, keepdims=True))).astype(BF16)
                dqh = jnp.dot(dsb, ke, preferred_element_type=F32)
                dq = dqh if dq is None else jnp.where(masks[h], dqh, dq)
                dk_acc[0:ext, :] += lax.dot_general(dsb, qm, _TN, preferred_element_type=F32)
                dv_acc[0:ext, :] += lax.dot_general(pn.astype(BF16), dom, _TN, preferred_element_type=F32)
            dq_ref[r * tq:(r + 1) * tq, :] = (dq * scale).astype(dq_ref.dtype)
        dk_ref[...] = dk_acc[...].astype(dk_ref.dtype)
        dv_ref[...] = dv_acc[...].astype(dv_ref.dtype)

    return _pc(
        body, name=name, grid=(nblk,),
        in_specs=[pl.BlockSpec((sq, width), lambda i, c=qc: (0, c + i)), pl.BlockSpec((t_len, width), lambda i, c=kc: (0, c + i)),
                  pl.BlockSpec((t_len, width), lambda i, c=vc: (0, c + i)), pl.BlockSpec((sq, width), lambda i: (0, i))],
        out_specs=[pl.BlockSpec((sq, width), lambda i: (0, i)), pl.BlockSpec((t_len, width), lambda i: (0, i)), pl.BlockSpec((t_len, width), lambda i: (0, i))],
        out_shape=[jax.ShapeDtypeStruct((sq, nblk * width), out_dtype), jax.ShapeDtypeStruct((t_len, nblk * width), out_dtype),
                   jax.ShapeDtypeStruct((t_len, nblk * width), out_dtype)],
        scratch_shapes=[pltpu.VMEM((t_len, width), F32), pltpu.VMEM((t_len, width), F32)] + ([pltpu.VMEM((nq, tq, tq), F32)] if causal else []),
        sem=("arbitrary",),
    )(qa, ka, va, doa)


_SELF = dict(qc=0, kc=8, vc=16, width=128, nblk=8, nsub=2, causal=True, tq=256, scale=A_HEAD_DIM ** -0.5)
_CROSS = dict(qc=0, kc=0, vc=4, width=256, nblk=4, nsub=1, causal=False, tq=512, scale=X_HEAD_DIM ** -0.5)


def _window_sum(x, g, row, backward):
    n = x.shape[0]

    def shift(y, k):
        if backward:
            return jnp.where(row < n - k, pltpu.roll(y, n - k, 0), 0.0)
        return jnp.where(row >= k, pltpu.roll(y, k, 0), 0.0)

    s2 = x + shift(x, 1)
    s4 = s2 + shift(s2, 2)
    s8 = s4 + shift(s4, 4)
    s16 = s8 + shift(s8, 8)
    return jnp.where(g == 0, s2, jnp.where(g == 1, s4, jnp.where(g == 2, s8, s16)))


def _pool(name, arr, c0, backward, out_dtype):
    n = arr.shape[0]
    gw = 256

    def body(v_ref, o_ref):
        g = pl.program_id(0)
        v = v_ref[...].astype(F32)
        row = lax.broadcasted_iota(jnp.int32, v.shape, 0)
        w = jnp.where(g == 0, 2, jnp.where(g == 1, 4, jnp.where(g == 2, 8, 16)))
        cnt = jnp.minimum(row + 1, w).astype(F32)
        if backward:
            o_ref[...] = (_window_sum(v / cnt, g, row, True) - v).astype(o_ref.dtype)
        else:
            o_ref[...] = (_window_sum(v, g, row, False) / cnt - v).astype(o_ref.dtype)

    return _pc(
        body, name=name, grid=(4,), in_specs=[pl.BlockSpec((n, gw), lambda i, c=c0 // gw: (0, c + i))],
        out_specs=pl.BlockSpec((n, gw), lambda i: (0, i)), out_shape=jax.ShapeDtypeStruct((n, 4 * gw), out_dtype), sem=("parallel",),
    )(arr)


_SCAN_ROWS = 256


def _scan_fwd(bu3, a2):
    n = bu3.shape[0]

    def body(bu_ref, a_ref, h_ref, carry):
        @pl.when(pl.program_id(0) == 0)
        def _():
            carry[...] = jnp.zeros_like(carry)

        ar, ai = a_ref[0:16, :], a_ref[16:32, :]

        def step(t, c):
            hr, hi = c
            nr = ar * hr - ai * hi + bu_ref[t, 0:16, :]
            ni = ar * hi + ai * hr + bu_ref[t, 16:32, :]
            h_ref[t, 0:16, :] = nr
            h_ref[t, 16:32, :] = ni
            return nr, ni

        hr, hi = lax.fori_loop(0, _SCAN_ROWS, step, (carry[0:16, :], carry[16:32, :]), unroll=8)
        carry[0:16, :] = hr
        carry[16:32, :] = hi

    blk = pl.BlockSpec((_SCAN_ROWS, 32, 128), lambda i: (i, 0, 0))
    return _pc(
        body, name="s5_scan_fwd", grid=(n // _SCAN_ROWS,), in_specs=[blk, pl.BlockSpec((32, 128), lambda i: (0, 0))], out_specs=blk,
        out_shape=jax.ShapeDtypeStruct(bu3.shape, F32), scratch_shapes=[pltpu.VMEM((32, 128), F32)], sem=("arbitrary",),
    )(bu3, a2)


def _scan_bwd(dh3, h3, a2):
    n = dh3.shape[0]
    nb = n // _SCAN_ROWS

    def body(dh_ref, h_ref, a_ref, dbu_ref, da_ref, carry):
        @pl.when(pl.program_id(0) == 0)
        def _():
            carry[...] = jnp.zeros_like(carry)
            da_ref[...] = jnp.zeros_like(da_ref)

        ar, ai = a_ref[0:16, :], a_ref[16:32, :]

        def step(tt, c):
            gr, gi, dar, dai = c
            t = _SCAN_ROWS - 1 - tt
            hr, hi = h_ref[t, 0:16, :], h_ref[t, 16:32, :]
            dar = dar + gr * hr + gi * hi
            dai = dai - gr * hi + gi * hr
            ngr = dh_ref[t, 0:16, :] + ar * gr + ai * gi
            ngi = dh_ref[t, 16:32, :] - ai * gr + ar * gi
            dbu_ref[t, 0:16, :] = ngr
            dbu_ref[t, 16:32, :] = ngi
            return ngr, ngi, dar, dai

        z = jnp.zeros((16, 128), F32)
        gr, gi, dar, dai = lax.fori_loop(0, _SCAN_ROWS, step, (carry[0:16, :], carry[16:32, :], z, z), unroll=8)
        carry[0:16, :] = gr
        carry[16:32, :] = gi
        da_ref[0:16, :] += dar
        da_ref[16:32, :] += dai

    blk = pl.BlockSpec((_SCAN_ROWS, 32, 128), lambda i: (nb - 1 - i, 0, 0))
    small = pl.BlockSpec((32, 128), lambda i: (0, 0))
    return _pc(
        body, name="s5_scan_bwd", grid=(nb,), in_specs=[blk, blk, small], out_specs=[blk, small],
        out_shape=[jax.ShapeDtypeStruct(dh3.shape, F32), jax.ShapeDtypeStruct((32, 128), F32)],
        scratch_shapes=[pltpu.VMEM((32, 128), F32)], sem=("arbitrary",),
    )(dh3, h3, a2)


def _bdense(bb_re, bb_im):
    eye = jnp.eye(8, dtype=F32)

    def one(bb):
        return jnp.einsum("sgph,gk->sghkp", bb.reshape(4, 8, S5_STATE, S5_GROUP_DIM), eye).reshape(512, 512)

    return jnp.concatenate([one(bb_re), one(bb_im)], axis=1)


def _cdense(c_re, c_im):
    eye = jnp.eye(8, dtype=F32)

    def one(cc):
        return jnp.einsum("sghp,gk->sgpkh", cc.reshape(4, 8, S5_GROUP_DIM, S5_STATE), eye).reshape(2048, 128)

    return jnp.concatenate([one(c_re), -one(c_im)], axis=0)


_NN = (((1,), (0,)), ((), ()))
_UF_BLOCK = 3072 // 128


def _s5_bu(z, bd):
    return _mm_blocks("mm_s5_bu", z, bd, grid=(1, 8, 1), out_blk=(SEQ, 512), a_blk=(SEQ, 128), a_idx=lambda i, j, k: (0, _UF_BLOCK + j % 4),
                      b_blk=(128, 512), b_idx=lambda i, j, k: (j % 4, j // 4), dims=_NN)


def _s5_bu_dx(dbu, bd):
    return _mm_blocks("mm_s5_bu_dx", dbu, bd, grid=(1, 4, 2), out_blk=(SEQ, 128), a_blk=(SEQ, 512), a_idx=lambda i, j, k: (0, 4 * k + j),
                      b_blk=(128, 512), b_idx=lambda i, j, k: (j, k), dims=_NT)


def _s5_bu_dw(z, dbu):
    return _mm_blocks("mm_s5_bu_dw", z, dbu, grid=(4, 2, 2), out_blk=(128, 512), a_blk=(1024, 128), a_idx=lambda i, j, k: (k, _UF_BLOCK + i),
                      b_blk=(1024, 512), b_idx=lambda i, j, k: (k, 4 * j + i), dims=_TN)


def _s5_y(h2, cf):
    return _mm_blocks("mm_s5_y", h2, cf, grid=(1, 4, 2), out_blk=(SEQ, 128), a_blk=(SEQ, 512), a_idx=lambda i, j, k: (0, 4 * k + j),
                      b_blk=(512, 128), b_idx=lambda i, j, k: (4 * k + j, 0), dims=_NN)


def _s5_y_dx(dyc, cf):
    return _mm_blocks("mm_s5_y_dx", dyc, cf, grid=(1, 8, 1), out_blk=(SEQ, 512), a_blk=(SEQ, 128), a_idx=lambda i, j, k: (0, j % 4),
                      b_blk=(512, 128), b_idx=lambda i, j, k: (j, 0), dims=_NT)


def _s5_y_dw(h2, dyc):
    return _mm_blocks("mm_s5_y_dw", h2, dyc, grid=(8, 1, 2), out_blk=(512, 128), a_blk=(1024, 512), a_idx=lambda i, j, k: (k, i),
                      b_blk=(1024, 128), b_idx=lambda i, j, k: (k, i % 4), dims=_TN)


def _pool_dense(pw):
    eye = jnp.eye(4, dtype=pw.dtype)
    return jnp.einsum("gcd,gk->gckd", pw, eye).reshape(1024, 1024)


def _row2(v):
    return v.reshape(1, -1)


def _even_fwd(x, W, i, zero, rest_of_weights):
    hn = _rw_fwd("rms_fwd", _f_rms, [_cols(x)], [_par(_row2(W["norm_ab"][i]) + zero)], [(D_MODEL, BF16, 1)], 256)[0]
    z = _mm("mm_in_ab", hn, W["w_in_ab"][i])
    o = _attn_fwd("attn_self_fwd", z, z, z, out_dtype=F32, **_SELF)
    zero = rest_of_weights(o)
    pooled = _pool("pool_fwd", z, 4096, False, BF16)
    wp = _pool_dense(W["pool_w"][i])
    mixed = _mm("mm_pool", pooled, wp)
    scale = _row2(W["pool_scale"][i]) + zero
    ab = _rw_fwd("gate_ab_fwd", _f_gate_ab, [_cols(o), _cols(z, 3072, 1024), _cols(mixed), _cols(z, 5120, 1024)], [_par(scale)],
                 [(2048, BF16, 2)], 256)[0]
    x1 = _mm("mm_out_ab", ab, W["w_out_ab"][i], res=x)
    return x1, dict(x=x, hn=hn, z=z, o=o, pooled=pooled, wp=wp, mixed=mixed, ab=ab)


def _even_bwd(dx1, sv, W, G, i, send):
    x, hn, z = sv["x"], sv["hn"], sv["z"]
    dab = _mm("mm_out_ab_dx", dx1, W["w_out_ab"][i], tb=True)
    G["w_out_ab"][i] = _mm("mm_out_ab_dw", sv["ab"], dx1, ta=True, out_dtype=BF16)
    scale = _row2(W["pool_scale"][i]) + send(0)
    (do, dga, dmixed, dgb), (dscale,) = _rw_bwd(
        "gate_ab_bwd", _f_gate_ab, [_cols(sv["o"]), _cols(z, 3072, 1024), _cols(sv["mixed"]), _cols(z, 5120, 1024)], [_par(scale)],
        [_cols(dab, nsplit=2)], [(0, F32), (1, BF16), (2, BF16), (3, BF16)], [0], 256)
    G["pool_scale"][i] = dscale.reshape(-1)
    dpooled = _mm("mm_pool_dx", dmixed, sv["wp"], tb=True)
    dwp = _mm("mm_pool_dw", sv["pooled"], dmixed, ta=True, out_dtype=BF16)
    G["pool_w"][i] = jnp.stack([dwp[g * 256:(g + 1) * 256, g * 256:(g + 1) * 256] for g in range(4)])
    dvb = _pool("pool_bwd", dpooled, 0, True, BF16)
    dq, dk, dv = _attn_bwd("attn_self_bwd", z, z, z, do, out_dtype=BF16, **_SELF)
    dz = jnp.concatenate([dq, dk, dv, dga, dvb, dgb], axis=1)
    dhn = _mm("mm_in_ab_dx", dz, W["w_in_ab"][i], tb=True)
    G["w_in_ab"][i] = _mm("mm_in_ab_dw", hn, dz, ta=True, out_dtype=BF16, out_stack=W["w_in_ab"][i].shape[2])
    g = _row2(W["norm_ab"][i]) + send(1)
    (dx,), (dg,) = _rw_bwd("rms_bwd", _f_rms_res, [_cols(x)], [_par(g)], [_cols(dx1), _cols(dhn)], [(0, F32)], [0], 256)
    G["norm_ab"][i] = dg.reshape(-1)
    return dx


def _odd_fwd(x, W, i, zero):
    hn = _rw_fwd("rms_fwd", _f_rms, [_cols(x)], [_par(_row2(W["norm_cd"][i]) + zero)], [(D_MODEL, BF16, 1)], 256)[0]
    z = _mm("mm_in_cd", hn, W["w_in_cd"][i])
    sgu_p = [_par(_row2(W["sgu_ln_g"][i]), 4), _par(_row2(W["sgu_ln_b"][i]), 4), _par(W["sgu_w"][i], 4), _par(W["sgu_b"][i][..., None], 4)]
    c_out = _rw_fwd("sgu_fwd", _f_sgu, [_cols(z, 0, 1024, 4), _cols(z, 1024, 1024, 4), _cols(z, 2048, 1024, 4)], sgu_p, [(1024, BF16, 4)], 128)[0]
    prep_rows = [_cols(W["s5_a_re"][i]), _cols(W["s5_a_im"][i]), _cols(W["s5_log_dt"][i].reshape(S5_GROUPS, 1))]
    abar_re, abar_im, coef_re, coef_im = _rw_fwd("s5_prep_fwd", _f_s5_prep, prep_rows, [], [(S5_STATE, F32, 1)] * 4, S5_GROUPS)
    bb_rows = [_cols(W["s5_b_re"][i].reshape(2048, 16)), _cols(W["s5_b_im"][i].reshape(2048, 16)), _cols(coef_re.reshape(2048, 1)), _cols(coef_im.reshape(2048, 1))]
    bb_re, bb_im = _rw_fwd("s5_bbar_fwd", _f_bbar, bb_rows, [], [(16, F32, 1)] * 2, 256)
    bd = _bdense(bb_re, bb_im).astype(BF16)
    cf = _cdense(W["s5_c_re"][i], W["s5_c_im"][i]).astype(BF16)
    a2 = jnp.concatenate([abar_re.reshape(16, 128), abar_im.reshape(16, 128)], axis=0)
    bu = _s5_bu(z, bd)
    h3 = _scan_fwd(bu.reshape(SEQ, 32, 128), a2)
    h2 = h3.reshape(SEQ, 4096)
    yc = _s5_y(h2, cf)
    dpar = _row2(W["s5_d"][i])
    yg = _rw_fwd("gelu_fwd", _f_gelu_y, [_cols(yc), _cols(z, 3072, 512)], [_par(dpar)], [(512, BF16, 1)], 256)[0]
    w12 = jnp.concatenate([W["glu_w1"][i], W["glu_w2"][i]], axis=1)
    t12 = _mm("mm_glu", yg, w12)
    d_out = _rw_fwd("glu_gate_fwd", _f_glu_gate, [_cols(t12, nsplit=2), _cols(z, 3584, 512)], [], [(512, BF16, 1)], 256)[0]
    cd = jnp.concatenate([c_out, d_out], axis=1)
    x1 = _mm("mm_out_cd", cd, W["w_out_cd"][i], res=x)
    sv = dict(x=x, hn=hn, z=z, sgu_p=sgu_p, prep_rows=prep_rows, bb_rows=bb_rows, bb=(bb_re, bb_im), bd=bd, cf=cf, a2=a2,
              h3=h3, h2=h2, yc=yc, dpar=dpar, yg=yg, w12=w12, t12=t12, cd=cd)
    return x1, sv


def _odd_bwd(dx1, sv, W, G, i, send):
    x, hn, z = sv["x"], sv["hn"], sv["z"]
    dcd = _mm("mm_out_cd_dx", dx1, W["w_out_cd"][i], tb=True)
    G["w_out_cd"][i] = _mm("mm_out_cd_dw", sv["cd"], dx1, ta=True, out_dtype=BF16)
    (du, dv, dgc), (dlg, dlb, dsw, dsb) = _rw_bwd(
        "sgu_bwd", _f_sgu, [_cols(z, 0, 1024, 4), _cols(z, 1024, 1024, 4), _cols(z, 2048, 1024, 4)], sv["sgu_p"],
        [_cols(dcd, 0, 1024, 4)], [(0, BF16), (1, BF16), (2, BF16)], [0, 1, 2, 3], 128)
    G["sgu_ln_g"][i], G["sgu_ln_b"][i] = dlg.reshape(-1), dlb.reshape(-1)
    G["sgu_w"][i], G["sgu_b"][i] = dsw, dsb[..., 0]
    (dt12, dgd), _ = _rw_bwd("glu_gate_bwd", _f_glu_gate, [_cols(sv["t12"], nsplit=2), _cols(z, 3584, 512)], [], [_cols(dcd, 1024, 512)],
                             [(0, BF16), (1, BF16)], [], 256)
    dyg = _mm("mm_glu_dx", dt12, sv["w12"], tb=True)
    dw12 = _mm("mm_glu_dw", sv["yg"], dt12, ta=True, out_dtype=BF16)
    G["glu_w1"][i], G["glu_w2"][i] = dw12[:, :512], dw12[:, 512:]
    (dyc, duf1), (dd,) = _rw_bwd("gelu_bwd", _f_gelu_y, [_cols(sv["yc"]), _cols(z, 3072, 512)], [_par(sv["dpar"])], [_cols(dyg)],
                                 [(0, BF16), (1, F32)], [0], 256)
    G["s5_d"][i] = dd.reshape(-1)
    dh2 = _s5_y_dx(dyc, sv["cf"])
    dcf = _s5_y_dw(sv["h2"], dyc)
    _, cvjp = jax.vjp(_cdense, W["s5_c_re"][i], W["s5_c_im"][i])
    G["s5_c_re"][i], G["s5_c_im"][i] = cvjp(dcf)
    dbu3, da2 = _scan_bwd(dh2.reshape(SEQ, 32, 128), sv["h3"], sv["a2"])
    dbu = dbu3.reshape(SEQ, 4096)
    duf2 = _s5_bu_dx(dbu, sv["bd"])
    dbd = _s5_bu_dw(z, dbu)
    _, bvjp = jax.vjp(_bdense, *sv["bb"])
    dbb_re, dbb_im = bvjp(dbd)
    (dbr, dbi, dcr, dci), _ = _rw_bwd("s5_bbar_bwd", _f_bbar, sv["bb_rows"], [], [_cols(dbb_re), _cols(dbb_im)],
                                      [(0, F32), (1, F32), (2, F32), (3, F32)], [], 256)
    G["s5_b_re"][i], G["s5_b_im"][i] = dbr.reshape(S5_GROUPS, S5_STATE, S5_GROUP_DIM), dbi.reshape(S5_GROUPS, S5_STATE, S5_GROUP_DIM)
    douts = [_cols(da2[0:16].reshape(S5_GROUPS, S5_STATE)), _cols(da2[16:32].reshape(S5_GROUPS, S5_STATE)),
             _cols(dcr.reshape(S5_GROUPS, S5_STATE)), _cols(dci.reshape(S5_GROUPS, S5_STATE))]
    (dar, dai, dldt), _ = _rw_bwd("s5_prep_bwd", _f_s5_prep, sv["prep_rows"], [], douts, [(0, F32), (1, F32), (2, F32)], [], S5_GROUPS)
    G["s5_a_re"][i], G["s5_a_im"][i], G["s5_log_dt"][i] = dar, dai, dldt.reshape(-1)
    dxd = (duf1 + duf2).astype(BF16)
    dz = jnp.concatenate([du, dv, dgc, dxd, dgd], axis=1)
    dhn = _mm("mm_in_cd_dx", dz, W["w_in_cd"][i], tb=True)
    G["w_in_cd"][i] = _mm("mm_in_cd_dw", hn, dz, ta=True, out_dtype=BF16, out_stack=W["w_in_cd"][i].shape[2])
    g = _row2(W["norm_cd"][i]) + send(0)
    (dx,), (dg,) = _rw_bwd("rms_bwd", _f_rms_res, [_cols(x)], [_par(g)], [_cols(dx1), _cols(dhn)], [(0, F32)], [0], 256)
    G["norm_cd"][i] = dg.reshape(-1)
    return dx


def _cross_fwd(x1, mem_n, W, l):
    hx = _rw_fwd("rms_fwd", _f_rms, [_cols(x1)], [_par(_row2(W["norm_x"][l]))], [(D_MODEL, BF16, 1)], 256)[0]
    qx = _mm("mm_xq", hx, W["w_xq"][l], out_dtype=BF16)
    kv = _mm("mm_xkv", mem_n, W["w_xkv"][l], out_dtype=BF16)
    ox = _attn_fwd("attn_cross_fwd", qx, kv, kv, out_dtype=BF16, **_CROSS)
    x2 = _mm("mm_xo", ox, W["w_xo"][l], res=x1)
    return x2, dict(x1=x1, hx=hx, qx=qx, kv=kv, ox=ox)


def _cross_bwd(dx2, dmem_n, sv, mem_n, W, G, l, zero):
    dox = _mm("mm_xo_dx", dx2, W["w_xo"][l], tb=True, out_dtype=BF16)
    G["w_xo"][l] = _mm("mm_xo_dw", sv["ox"], dx2, ta=True, out_dtype=BF16)
    dqx, dk, dv = _attn_bwd("attn_cross_bwd", sv["qx"], sv["kv"], sv["kv"], dox, out_dtype=BF16, **_CROSS)
    dkv = jnp.concatenate([dk, dv], axis=1)
    dhx = _mm("mm_xq_dx", dqx, W["w_xq"][l], tb=True)
    G["w_xq"][l] = _mm("mm_xq_dw", sv["hx"], dqx, ta=True, out_dtype=BF16)
    dmem_n = _mm("mm_xkv_dx", dkv, W["w_xkv"][l], tb=True, res=dmem_n)
    G["w_xkv"][l] = _mm("mm_xkv_dw", mem_n, dkv, ta=True, out_dtype=BF16, out_stack=W["w_xkv"][l].shape[2])
    (dx1,), (dg,) = _rw_bwd("rms_bwd", _f_rms_res, [_cols(sv["x1"])], [_par(_row2(W["norm_x"][l]) + zero)], [_cols(dx2), _cols(dhx)], [(0, F32)], [0], 256)
    G["norm_x"][l] = dg.reshape(-1)
    return dx1, dmem_n


_PER_LAYER = ("pool_scale", "norm_ab", "norm_cd", "sgu_ln_g", "sgu_ln_b", "sgu_w", "sgu_b", "s5_d", "s5_c_re", "s5_c_im", "s5_b_re", "s5_b_im",
              "s5_a_re", "s5_a_im", "s5_log_dt", "w_in_ab", "pool_w", "w_out_ab", "w_in_cd", "glu_w1", "glu_w2", "w_out_cd")


def _local_step(x, mem, target, W, weights_of, send_grads, after_layer):
    G = {k: [None, None] for k in _PER_LAYER}
    for k in ("norm_x", "w_xq", "w_xkv", "w_xo"):
        G[k] = [None] * DEPTH
    mem_rows = [_cols(mem)]
    mem_par = [_par(_row2(W["mem_norm"]))]
    mem_n = _rw_fwd("rms_fwd_mem", _f_rms, mem_rows, mem_par, [(D_MODEL, BF16, 1)], 256)[0]
    saved = []
    for layer in range(DEPTH):
        zero = weights_of(layer, 0, x if layer else mem_n)
        if layer % 2 == 0:
            x, sv = _even_fwd(x, W, layer // 2, zero, functools.partial(weights_of, layer, 1))
        else:
            x, sv = _odd_fwd(x, W, layer // 2, zero)
        x, svx = _cross_fwd(x, mem_n, W, layer)
        saved.append((sv, svx))
    loss, dx, dfinal = _loss_head(x, target, _row2(W["final_norm"]))
    G["final_norm"] = dfinal.reshape(-1)
    dmem_n, zero = None, 0.0
    for layer in reversed(range(DEPTH)):
        sv, svx = saved[layer]
        dx, dmem_n = _cross_bwd(dx, dmem_n, svx, mem_n, W, G, layer, zero)
        hook = functools.partial(send_grads, layer, G)
        dx = _even_bwd(dx, sv, W, G, layer // 2, hook) if layer % 2 == 0 else _odd_bwd(dx, sv, W, G, layer // 2, hook)
        zero = after_layer(layer, G)
    _, (dmn,) = _rw_bwd("rms_bwd_mem", _f_rms, mem_rows, mem_par, [_cols(dmem_n)], [], [0], 256)
    G["mem_norm"] = dmn.reshape(-1)
    return loss, dx, G


_HBM = pl.BlockSpec(memory_space=pltpu.HBM)
_ANY = pl.BlockSpec(memory_space=pl.ANY)
_SEM = pl.BlockSpec(memory_space=pltpu.SEMAPHORE)
_N_PEERS = N_DEV - 1


def _mesh_pos():
    return lax.axis_index("x"), lax.axis_index("y"), lax.axis_index("c")


def _peer(pos, k):
    x, y, c = pos
    return (x ^ ((k >> 2) & 1), y ^ ((k >> 1) & 1), c ^ (k & 1))


def _lin(pos):
    return 4 * pos[0] + 2 * pos[1] + pos[2]


def _ends(gather, srcs, lands, t, sender, receiver):
    if gather:
        return lands[t].at[sender], lands[t].at[sender]
    whole = len(srcs[t].shape) != len(lands[t].shape)
    return (srcs[t] if whole else srcs[t].at[receiver]), lands[t].at[sender]


def _into_slot(name, b2, r0, r, me, dtype, after):
    c = b2.shape[1]
    tr = _row_block(r, c, 2 << 20)
    assert r0 % tr == 0

    def body(me_ref, x_ref, *rest):
        rest[-1][...] = x_ref[...].astype(dtype)

    extra = [] if after is None else [after]
    grid_spec = pltpu.PrefetchScalarGridSpec(
        num_scalar_prefetch=1, grid=(r // tr,),
        in_specs=[pl.BlockSpec((tr, c), lambda i, me, o=r0 // tr: (o + i, 0))] + [_ANY] * len(extra),
        out_specs=pl.BlockSpec((None, tr, c), lambda i, me: (me[0], i, 0)))
    return pl.pallas_call(
        body, name=name, grid_spec=grid_spec, out_shape=jax.ShapeDtypeStruct((N_DEV, r, c), dtype),
        compiler_params=pltpu.CompilerParams(dimension_semantics=("arbitrary",), vmem_limit_bytes=V7X_VMEM_LIMIT_BYTES),
        interpret=False,
    )(me, b2, *extra)


def _exchange_start(name, gather, srcs, lands, after=None):
    ns, nt = len(srcs), len(lands)
    arrs = list(srcs) + list(lands)
    extra = [] if after is None else [after]

    def body(*refs):
        ins, lnd = refs[:ns], refs[ns:ns + nt]
        refs = refs[len(extra):]
        send_sems, recv_sems = refs[ns + nt], refs[ns + nt + 1]
        token = refs[-1]
        pos = _mesh_pos()
        me = _lin(pos)
        for k in range(1, N_DEV):
            peer = _peer(pos, k)
            for t in range(nt):
                src, dst = _ends(gather, ins, lnd, t, me, _lin(peer))
                pltpu.make_async_remote_copy(
                    src_ref=src, dst_ref=dst, send_sem=send_sems.at[t * _N_PEERS + k - 1], recv_sem=recv_sems.at[t * _N_PEERS + k - 1],
                    device_id=peer, device_id_type=pl.DeviceIdType.MESH).start()
        token[...] = jnp.zeros_like(token)

    out = pl.pallas_call(
        body, name=name,
        out_shape=(pltpu.SemaphoreType.DMA((nt * _N_PEERS,)), pltpu.SemaphoreType.DMA((nt * _N_PEERS,)), *[pltpu.HBM(a.shape, a.dtype) for a in arrs],
                   jax.ShapeDtypeStruct((8, 128), F32)),
        in_specs=[_HBM] * (ns + nt) + [_ANY] * len(extra), out_specs=(_SEM, _SEM, *[_HBM] * (ns + nt), pl.BlockSpec(memory_space=pltpu.VMEM)),
        input_output_aliases={j: 2 + j for j in range(ns + nt)},
        compiler_params=pltpu.CompilerParams(has_side_effects=pltpu.SideEffectType.DATAFLOW_SIDE_EFFECTING),
        interpret=False,
    )(*[pltpu.with_memory_space_constraint(a, pltpu.HBM) for a in arrs], *extra)
    return dict(send=out[0], recv=out[1], srcs=list(out[2:2 + ns]), lands=list(out[2 + ns:2 + ns + nt]), token=out[-1][0, 0], token_arr=out[-1], gather=gather)


def _exchange_wait(name, ex, after):
    ns, nt = len(ex["srcs"]), len(ex["lands"])
    gather = ex["gather"]
    arrs = ex["srcs"] + ex["lands"]

    def body(*refs):
        ins, lnd = refs[:ns], refs[ns:ns + nt]
        send_sems, recv_sems = refs[ns + nt], refs[ns + nt + 1]
        pos = _mesh_pos()
        me = _lin(pos)
        for k in range(1, N_DEV):
            peer = _peer(pos, k)
            for t in range(nt):
                src, _ = _ends(gather, ins, lnd, t, me, _lin(peer))
                _, dst = _ends(gather, ins, lnd, t, _lin(peer), me)
                cp = pltpu.make_async_remote_copy(
                    src_ref=src, dst_ref=dst, send_sem=send_sems.at[t * _N_PEERS + k - 1], recv_sem=recv_sems.at[t * _N_PEERS + k - 1],
                    device_id=peer, device_id_type=pl.DeviceIdType.MESH)
                cp.wait_send()
                cp.wait_recv()

    out = pl.pallas_call(
        body, name=name, out_shape=tuple(pltpu.HBM(a.shape, a.dtype) for a in arrs),
        in_specs=[_HBM] * (ns + nt) + [_SEM, _SEM, _ANY], out_specs=tuple([_HBM] * (ns + nt)),
        input_output_aliases={j: j for j in range(ns + nt)},
        compiler_params=pltpu.CompilerParams(has_side_effects=pltpu.SideEffectType.DATAFLOW_SIDE_EFFECTING),
        interpret=False,
    )(*arrs, ex["send"], ex["recv"], after)
    return list(out[:ns]), list(out[ns:])


def _scatter_begin(name, srcs):
    lands = [lax.empty(s.shape if s.ndim == 3 else (N_DEV,) + s.shape, s.dtype) for s in srcs]
    return _exchange_start(name, False, srcs, lands)


def _adam(name, w, m, v, parts, own, me, layer, bufs):
    r, c = parts.shape[1:]
    tr = _row_block(r, c, 1 << 20)
    nb = r // tr

    def body(me_ref, w_ref, m_ref, v_ref, p_ref, own_ref, *rest):
        g_ref, d_ref, nm_ref, nv_ref, acc = rest[-5:]
        acc[...] = jnp.zeros_like(acc)
        for k in range(N_DEV):
            @pl.when(me_ref[0] == k)
            def _():
                acc[...] += own_ref[...].astype(F32)

            @pl.when(me_ref[0] != k)
            def _(k=k):
                acc[...] += p_ref[k].astype(F32)

        g = acc[...]
        mm = ADAM_B1 * m_ref[...] + (1.0 - ADAM_B1) * g
        vv = ADAM_B2 * v_ref[...] + (1.0 - ADAM_B2) * jnp.square(g)
        m_hat = mm / (1.0 - ADAM_B1 ** ADAM_STEP)
        v_hat = vv / (1.0 - ADAM_B2 ** ADAM_STEP)
        g_ref[...] = g
        d_ref[...] = -ADAM_LR * (m_hat / (jnp.sqrt(v_hat) + ADAM_EPS) + ADAM_WD * w_ref[...])
        nm_ref[...] = mm
        nv_ref[...] = vv

    blk = pl.BlockSpec((tr, c), lambda i, me, o=layer * nb: (o + i, 0))
    own_spec = pl.BlockSpec((None, tr, c), lambda i, me: (me[0], i, 0)) if own.ndim == 3 else pl.BlockSpec((tr, c), lambda i, me: (i, 0))
    in_specs = [blk, blk, blk, pl.BlockSpec((N_DEV, tr, c), lambda i, me: (0, i, 0)), own_spec]
    args = [me, w, m, v, parts, own]
    aliases = {}
    if bufs is not None:
        in_specs += [_ANY] * 4
        aliases = {len(args) + j: j for j in range(4)}
        args += list(bufs)
    grid_spec = pltpu.PrefetchScalarGridSpec(
        num_scalar_prefetch=1, grid=(nb,), in_specs=in_specs, out_specs=[blk] * 4, scratch_shapes=[pltpu.VMEM((tr, c), F32)])
    return pl.pallas_call(
        body, name=name, grid_spec=grid_spec, out_shape=[jax.ShapeDtypeStruct(w.shape, F32)] * 4, input_output_aliases=aliases,
        compiler_params=pltpu.CompilerParams(dimension_semantics=("arbitrary",), vmem_limit_bytes=V7X_VMEM_LIMIT_BYTES),
        interpret=False,
    )(*args)


def _pack(arrs):
    out = []
    for a in arrs:
        f = a.reshape(-1)
        out.append(jnp.pad(f, (0, (-f.shape[0]) % 1024)).reshape(-1, 128))
    rows = sum(a.shape[0] for a in out)
    out.append(jnp.zeros(((-rows) % 256, 128), out[0].dtype))
    return jnp.concatenate(out, axis=0)


def _unpack(packed, shapes):
    out, r = [], 0
    for s in shapes:
        n = math.prod(s)
        rows = (n + 1023) // 1024 * 8
        out.append(packed[r:r + rows].reshape(-1)[:n].reshape(s))
        r += rows
    return out


def _row_block(r, c, limit):
    best = None
    for tr in range(16, r + 1, 16):
        if r % tr == 0 and tr * c * 4 <= limit:
            best = tr
    return r if best is None else best


_BIG = ("w_in_ab", "pool_w", "w_out_ab", "w_in_cd", "glu_w1", "glu_w2", "w_out_cd", "w_xq", "w_xkv", "w_xo")
_STACKED = ("w_in_ab", "w_in_cd", "w_xkv")
_MIXER_BIG = (("w_in_ab", "pool_w", "w_out_ab"), ("w_in_cd", "glu_w1", "glu_w2", "w_out_cd"))
_CROSS_BIG = ("w_xq", "w_xkv", "w_xo")
_SMALL_SPLIT = ["norm_cd", "sgu_ln_g", "sgu_ln_b", "s5_d"]
_REPLICATED_ODD = ["sgu_w", "sgu_b", "s5_a_re", "s5_a_im", "s5_log_dt", "s5_b_re", "s5_b_im", "s5_c_re", "s5_c_im", "final_norm"]
_REPLICATED_EVEN = ["norm_ab", "pool_scale", "norm_x", "mem_norm"]
_REPLICATED = _REPLICATED_ODD + _REPLICATED_EVEN
_WEIGHTS = ["norm_ab", "w_in_ab", "pool_w", "pool_scale", "w_out_ab", "norm_cd", "w_in_cd", "sgu_ln_g", "sgu_ln_b", "sgu_w", "sgu_b", "s5_a_re",
            "s5_a_im", "s5_log_dt", "s5_b_re", "s5_b_im", "s5_c_re", "s5_c_im", "s5_d", "glu_w1", "glu_w2", "w_out_cd", "norm_x", "w_xq",
            "w_xkv", "w_xo", "mem_norm", "final_norm"]


def _layer_big(layer):
    return [(n, layer // 2) for n in _MIXER_BIG[layer % 2]] + [(n, layer) for n in _CROSS_BIG]


def _gather_parts(layer):
    big = _layer_big(layer)
    return [big[:1], big[1:]] if layer % 2 == 0 else [big]


def _scatter_parts(layer):
    big = _layer_big(layer)
    return [big[2:], big[:2]] if layer % 2 == 0 else [big]


def _from_slots(name, a):
    if name in _STACKED:
        return a
    if name == "pool_w":
        return a.reshape(N_DEV, 4, 32, 256).transpose(1, 0, 2, 3).reshape(4, 256, 256)
    return a.reshape(-1, a.shape[-1])


def _to_slots(name, g):
    if name in _STACKED:
        return g
    if name == "pool_w":
        return g.reshape(4, N_DEV, 32, 256).transpose(1, 0, 2, 3).reshape(N_DEV, 128, 256)
    return g.reshape(N_DEV, -1, g.shape[-1])


def _rows2d(a):
    return a.reshape(-1, a.shape[-1])


def _small_rows(block):
    return jnp.pad(block, ((0, 0), (0, 128 - block.shape[1])))


def kernel(x, mem, norm_ab, w_in_ab, pool_w, pool_scale, w_out_ab, norm_cd, w_in_cd, sgu_ln_g, sgu_ln_b, sgu_w, sgu_b, s5_a_re, s5_a_im, s5_log_dt, s5_b_re, s5_b_im, s5_c_re, s5_c_im, s5_d, glu_w1, glu_w2, w_out_cd, norm_x, w_xq, w_xkv, w_xo, mem_norm, final_norm, loss_target, m_norm_ab, m_w_in_ab, m_pool_w, m_pool_scale, m_w_out_ab, m_norm_cd, m_w_in_cd, m_sgu_ln_g, m_sgu_ln_b, m_sgu_w, m_sgu_b, m_s5_a_re, m_s5_a_im, m_s5_log_dt, m_s5_b_re, m_s5_b_im, m_s5_c_re, m_s5_c_im, m_s5_d, m_glu_w1, m_glu_w2, m_w_out_cd, m_norm_x, m_w_xq, m_w_xkv, m_w_xo, m_mem_norm, m_final_norm, v_norm_ab, v_w_in_ab, v_pool_w, v_pool_scale, v_w_out_ab, v_norm_cd, v_w_in_cd, v_sgu_ln_g, v_sgu_ln_b, v_sgu_w, v_sgu_b, v_s5_a_re, v_s5_a_im, v_s5_log_dt, v_s5_b_re, v_s5_b_im, v_s5_c_re, v_s5_c_im, v_s5_d, v_glu_w1, v_glu_w2, v_w_out_cd, v_norm_x, v_w_xq, v_w_xkv, v_w_xo, v_mem_norm, v_final_norm):
    args = locals()
    w = {n: args[n] for n in _WEIGHTS}
    m = {n: args["m_" + n] for n in _WEIGHTS}
    v = {n: args["v_" + n] for n in _WEIGHTS}

    me = jnp.reshape(_lin(_mesh_pos()), (1,)).astype(jnp.int32)

    order = [(layer, p) for layer in range(DEPTH) for p in range(len(_gather_parts(layer)))]
    lands, gathers = {}, {}
    for key in order:
        lands[key] = []
        for name, i in _gather_parts(key[0])[key[1]]:
            b2 = _rows2d(w[name])
            r = b2.shape[0] // w[name].shape[0]
            lands[key].append(_into_slot("cast_" + name, b2, i * r, r, me, BF16, None))
    small_blocks = jnp.concatenate([_small_rows(w[n]) for n in _SMALL_SPLIT], axis=0)
    lands[order[0]].append(_into_slot("cast_small", small_blocks, 0, 8, me, F32, None))

    def begin_gather(key, after):
        gathers[key] = _exchange_start("gather%d%s_start" % (key[0], "ab"[key[1]]), True, [], lands[key], after)
        return gathers[key]["token"]

    W = {n: w[n] for n in _REPLICATED}
    W["mem_norm"] = w["mem_norm"] + begin_gather(order[0], None)
    for name in _BIG:
        W[name] = [None] * w[name].shape[0]

    def weights_of(layer, part, after):
        key = (layer, part)
        if key not in gathers:
            return 0.0
        _, got = _exchange_wait("gather%d%s_wait" % (layer, "ab"[part]), gathers[key], after)
        for (name, i), arr in zip(_gather_parts(layer)[part], got):
            W[name][i] = _from_slots(name, arr)
        if key == order[0]:
            sm = got[-1].reshape(N_DEV, 4, 2, 128)
            for j, n in enumerate(_SMALL_SPLIT):
                width = w[n].shape[1]
                W[n] = sm[:, j, :, :width].transpose(1, 0, 2).reshape(2, N_DEV * width)
        k = order.index(key)
        return sum([begin_gather(nxt, got[0]) for nxt in (order[1:3] if k == 0 else order[k + 2:k + 3])], 0.0)

    scatters, small = {}, {}

    def send_grads(layer, G, part):
        srcs = [_to_slots(name, G[name][i]) for name, i in _scatter_parts(layer)[part]]
        scatters[layer, part] = _scatter_begin("scatter%d%s_start" % (layer, "ab"[part]), srcs)
        return scatters[layer, part]["token"]

    def begin_small(tag, G, names, split):
        srcs = [_pack([G[n] if n in ("mem_norm", "final_norm") else jnp.stack(G[n]) for n in names])]
        if split:
            gs = jnp.stack([jnp.stack(G[n]).reshape(2, N_DEV, -1).transpose(1, 0, 2) for n in _SMALL_SPLIT[:3]]
                           + [jnp.pad(jnp.stack(G["s5_d"]).reshape(2, N_DEV, -1).transpose(1, 0, 2), ((0, 0), (0, 0), (0, 64)))], axis=1)
            srcs.append(gs.reshape(N_DEV, 8, 128))
        small[tag] = _scatter_begin("scatter_small_%s_start" % tag, srcs)
        return small[tag]["token"]

    def after_layer(layer, G):
        return begin_small("odd", G, _REPLICATED_ODD, True) if layer == 1 else 0.0

    loss, dx, G = _local_step(x[0], mem[0], loss_target[0], W, weights_of, send_grads, after_layer)
    loss = lax.psum(loss[0, 0], MESH_AXES)
    begin_small("even", G, _REPLICATED_EVEN, False)

    out = {}
    after = small["even"]["token_arr"]
    for layer, part in scatters:
        own, got = _exchange_wait("scatter%d%s_wait" % (layer, "ab"[part]), scatters[layer, part], after)
        for (name, i), mine, parts in zip(_scatter_parts(layer)[part], own, got):
            out[name] = _adam("adam_" + name, _rows2d(w[name]), _rows2d(m[name]), _rows2d(v[name]), parts, mine, me, i, out.get(name))
        after = out[name][0]
    for name in _BIG:
        out[name] = [a.reshape(w[name].shape) for a in out[name]]
    own, got = _exchange_wait("scatter_small_odd_wait", small["odd"], after)
    pk = lambda d: jnp.concatenate([_small_rows(d[n]) for n in _SMALL_SPLIT], axis=0)
    res = _adam("adam_small", pk(w), pk(m), pk(v), got[1], own[1], me, 0, None)
    for j, n in enumerate(_SMALL_SPLIT):
        out[n] = [a[2 * j:2 * j + 2, :w[n].shape[1]] for a in res]
    own_even, got_even = _exchange_wait("scatter_small_even_wait", small["even"], res[0])
    for names, mine, parts in zip((_REPLICATED_ODD, _REPLICATED_EVEN), (own[0], own_even[0]), (got[0], got_even[0])):
        pk = lambda d: _pack([d[n] for n in names])
        res = _adam("adam_replicated", pk(w), pk(m), pk(v), parts, mine, me, 0, None)
        shapes = [w[n].shape for n in names]
        un = [_unpack(a, shapes) for a in res]
        for j, n in enumerate(names):
            out[n] = [un[q][j] for q in range(4)]

    return (loss, dx[None], *[out[n][0] for n in _WEIGHTS], *[out[n][1] for n in _WEIGHTS], *[out[n][2] for n in _WEIGHTS],
            *[out[n][3] for n in _WEIGHTS])
```

```python
import functools
import math

import jax
import jax.numpy as jnp
from jax import lax
from jax.experimental import pallas as pl
from jax.experimental.pallas import tpu as pltpu

F32 = jnp.float32
BF16 = jnp.bfloat16

SEQ = 2048
D_MODEL = 1024
MEM_LEN = 256
DEPTH = 4
N_DEV = 8
EPS = 1e-6
NEG = -1e30
A_HEAD_DIM = 64
X_HEAD_DIM = 256
S5_GROUPS = 32
S5_STATE = 64
S5_GROUP_DIM = 16

ADAM_LR = 0.001
ADAM_B1 = 0.9
ADAM_B2 = 0.999
ADAM_EPS = 1e-08
ADAM_WD = 0.01
ADAM_STEP = 10

V7X_VMEM_LIMIT_BYTES = 56 * 1024 * 1024
_MM_VMEM_BYTES = 36 * 1024 * 1024
MESH_AXES = ("x", "y", "c")


def _pc(body, *, name, out_shape, grid=None, in_specs=None, out_specs=None, scratch_shapes=(), aliases=None, sem=None):
    kw = {}
    if grid is not None:
        kw["grid"] = grid
    if in_specs is not None:
        kw["in_specs"] = in_specs
    if out_specs is not None:
        kw["out_specs"] = out_specs
    if aliases:
        kw["input_output_aliases"] = aliases
    return pl.pallas_call(
        body,
        name=name,
        out_shape=out_shape,
        scratch_shapes=list(scratch_shapes),
        compiler_params=pltpu.CompilerParams(dimension_semantics=sem, vmem_limit_bytes=V7X_VMEM_LIMIT_BYTES),
        interpret=False,
        **kw,
    )


def _cols(arr, c0=0, width=None, nsplit=1, r0=0):
    width = arr.shape[1] - c0 if width is None else width
    assert c0 % width == 0 and width % nsplit == 0
    return (arr, c0, width, nsplit, r0)


def _par(arr, nsplit=1):
    return (arr, nsplit)


def _ld(ref, nsplit):
    if nsplit == 1:
        return ref[...].astype(F32)
    if len(ref.shape) == 3:
        return tuple(ref[k].astype(F32) for k in range(nsplit))
    w = ref.shape[-1] // nsplit
    return tuple(ref[:, k * w:(k + 1) * w].astype(F32) for k in range(nsplit))


def _st(ref, val, nsplit, accumulate=False):
    if nsplit == 1:
        val = (val,)
    for k in range(nsplit):
        if nsplit == 1:
            idx = (Ellipsis,)
        elif len(ref.shape) == 3:
            idx = (k,)
        else:
            w = ref.shape[-1] // nsplit
            idx = (slice(None), slice(k * w, (k + 1) * w))
        if accumulate:
            ref[idx] += val[k].astype(ref.dtype)
        else:
            ref[idx] = val[k].astype(ref.dtype)


def _row_spec(tr, op):
    _, c0, w, _, r0 = op
    assert r0 % tr == 0
    return pl.BlockSpec((tr, w), lambda i, cb=c0 // w, rb=r0 // tr: (i + rb, cb))


def _full_spec(arr):
    return pl.BlockSpec(arr.shape, lambda i, nd=arr.ndim: (0,) * nd)


def _rw_fwd(name, f, rows, pars, outs, tr, n_rows=None):
    n_rows = rows[0][0].shape[0] if n_rows is None else n_rows
    nr, npar = len(rows), len(pars)

    def body(*refs):
        r = [_ld(refs[i], rows[i][3]) for i in range(nr)]
        p = [_ld(refs[nr + i], pars[i][1]) for i in range(npar)]
        res = f(r, p)
        for k, (_, _, ns) in enumerate(outs):
            _st(refs[nr + npar + k], res[k], ns)

    res = _pc(
        body, name=name, grid=(n_rows // tr,),
        in_specs=[_row_spec(tr, op) for op in rows] + [_full_spec(a) for a, _ in pars],
        out_specs=[pl.BlockSpec((tr, w), lambda i: (i, 0)) for w, _, _ in outs],
        out_shape=[jax.ShapeDtypeStruct((n_rows, w), dt) for w, dt, _ in outs],
        sem=("arbitrary",),
    )(*[op[0] for op in rows], *[a for a, _ in pars])
    return list(res)


def _rw_bwd(name, f, rows, pars, douts, drow, dpar, tr):
    n_rows = rows[0][0].shape[0]
    nr, npar = len(rows), len(pars)
    dgiven = [d for d in douts if d is not None]
    nd = len(dgiven)

    def body(*refs):
        r = [_ld(refs[i], rows[i][3]) for i in range(nr)]
        p = [_ld(refs[nr + i], pars[i][1]) for i in range(npar)]
        d = [_ld(refs[nr + npar + i], dgiven[i][3]) for i in range(nd)]
        orefs = refs[nr + npar + nd:]

        def g(dr, dp):
            rr, pp = list(r), list(p)
            for j, (idx, _) in enumerate(drow):
                rr[idx] = dr[j]
            for j, idx in enumerate(dpar):
                pp[idx] = dp[j]
            return tuple(f(rr, pp))

        out, vjp = jax.vjp(g, [r[idx] for idx, _ in drow], [p[idx] for idx in dpar])
        ct, j = [], 0
        for k, o in enumerate(out):
            if douts[k] is None:
                ct.append(jax.tree.map(jnp.zeros_like, o))
            else:
                ct.append(d[j])
                j += 1
        gdr, gdp = vjp(tuple(ct))
        for j, (idx, _) in enumerate(drow):
            _st(orefs[j], gdr[j], rows[idx][3])

        @pl.when(pl.program_id(0) == 0)
        def _():
            for j in range(len(dpar)):
                orefs[len(drow) + j][...] = jnp.zeros_like(orefs[len(drow) + j])

        for j, idx in enumerate(dpar):
            _st(orefs[len(drow) + j], gdp[j], pars[idx][1], accumulate=True)

    res = _pc(
        body, name=name, grid=(n_rows // tr,),
        in_specs=[_row_spec(tr, op) for op in rows] + [_full_spec(a) for a, _ in pars] + [_row_spec(tr, op) for op in dgiven],
        out_specs=[pl.BlockSpec((tr, rows[idx][2]), lambda i: (i, 0)) for idx, _ in drow] + [_full_spec(pars[idx][0]) for idx in dpar],
        out_shape=[jax.ShapeDtypeStruct((n_rows, rows[idx][2]), dt) for idx, dt in drow]
        + [jax.ShapeDtypeStruct(pars[idx][0].shape, F32) for idx in dpar],
        sem=("arbitrary",),
    )(*[op[0] for op in rows], *[a for a, _ in pars], *[op[0] for op in dgiven])
    res = list(res)
    return res[:len(drow)], res[len(drow):]


def _sigmoid(x):
    return jax.nn.sigmoid(x)


def _silu(x):
    return x * _sigmoid(x)


def _rms(x, g):
    return x * lax.rsqrt(jnp.mean(x * x, axis=-1, keepdims=True) + EPS) * g


def _f_rms(r, p):
    return [_rms(r[0], p[0])]


def _f_rms_res(r, p):
    return [r[0], _rms(r[0], p[0])]


def _f_gate_ab(r, p):
    o, ga, mixed, gb = r
    return [(o * _silu(ga), mixed * p[0] * _silu(gb))]


def _f_sgu(r, p):
    u, v, gc = r
    lg, lb, w, b = p
    n = float(D_MODEL)
    mu = sum(jnp.sum(vk, axis=-1, keepdims=True) for vk in v) / n
    var = sum(jnp.sum(jnp.square(vk - mu), axis=-1, keepdims=True) for vk in v) / n
    rs = lax.rsqrt(var + EPS)
    t = w[0].shape[0]
    tri = lax.broadcasted_iota(jnp.int32, (t, t), 0) >= lax.broadcasted_iota(jnp.int32, (t, t), 1)
    outs = []
    for k in range(len(v)):
        vn = (v[k] - mu) * rs * lg[k] + lb[k]
        mixed = jnp.dot(jnp.where(tri, w[k], 0.0), vn, preferred_element_type=F32) + b[k]
        outs.append(u[k] * mixed * _silu(gc[k]))
    return [tuple(outs)]


def _gelu(x):
    return 0.5 * x * (1.0 + jnp.tanh(math.sqrt(2.0 / math.pi) * (x + 0.044715 * (x * x * x))))


def _f_gelu_y(r, p):
    yc, uf = r
    return [_gelu(yc + p[0] * uf)]


def _f_glu_gate(r, p):
    t12, gd = r
    return [t12[0] * _sigmoid(t12[1]) * _silu(gd)]


def _f_s5_prep(r, p):
    ar, ai, ldt = r
    dt = jnp.exp(ldt)
    mag = jnp.exp(dt * ar)
    abar_re = mag * jnp.cos(dt * ai)
    abar_im = mag * jnp.sin(dt * ai)
    nr, ni = abar_re - 1.0, abar_im
    inv = 1.0 / (ar * ar + ai * ai)
    return [abar_re, abar_im, (nr * ar + ni * ai) * inv, (ni * ar - nr * ai) * inv]


def _f_bbar(r, p):
    br, bi, cr, ci = r
    return [cr * br - ci * bi, cr * bi + ci * br]


def _loss_head(x, target, g):
    tr = 256
    n_rows, width = x.shape

    def f(xv, gv, tv):
        err = jnp.square(_rms(xv, gv) - tv)
        return 0.5 * jnp.mean(err, axis=-1, keepdims=True)

    def body(x_ref, t_ref, g_ref, loss_ref, dx_ref, dg_ref):
        @pl.when(pl.program_id(0) == 0)
        def _():
            loss_ref[...] = jnp.zeros_like(loss_ref)
            dg_ref[...] = jnp.zeros_like(dg_ref)

        tv = t_ref[...]
        row_loss, vjp = jax.vjp(lambda a, b: f(a, b, tv), x_ref[...], g_ref[...])
        dx, dg = vjp(jnp.ones_like(row_loss))
        dx_ref[...] = dx
        dg_ref[...] += dg
        loss_ref[...] += jnp.broadcast_to(jnp.sum(row_loss, axis=0, keepdims=True), loss_ref.shape)

    blk = pl.BlockSpec((tr, width), lambda i: (i, 0))
    one = pl.BlockSpec((1, width), lambda i: (0, 0))
    return _pc(
        body, name="loss_head", grid=(n_rows // tr,), in_specs=[blk, blk, one],
        out_specs=[pl.BlockSpec((1, 128), lambda i: (0, 0)), blk, one],
        out_shape=[jax.ShapeDtypeStruct((1, 128), F32), jax.ShapeDtypeStruct(x.shape, F32), jax.ShapeDtypeStruct((1, width), F32)],
        sem=("arbitrary",),
    )(x, target, g)


_NT = (((1,), (1,)), ((), ()))
_TN = (((0,), (0,)), ((), ()))


def _tile(n, cap):
    t = min(n, cap)
    while n % t:
        t -= 128
    assert t > 0
    return t


def _mm(name, a, b, *, ta=False, tb=False, out_dtype=F32, a_off=0, a_width=None, res=None, out_stack=None):
    assert not (ta and tb)
    stacked = b.ndim == 3
    bk, bn = (b.shape[1], b.shape[0] * b.shape[2]) if stacked else b.shape
    if ta:
        kc = a.shape[0]
        m = a.shape[1] - a_off if a_width is None else a_width
        n = bn
        assert bk == kc and not stacked
    else:
        m = a.shape[0]
        kc = a.shape[1] - a_off if a_width is None else a_width
        n = bk if tb else bn
        assert (bn if tb else bk) == kc
    tn = _tile(b.shape[2] if stacked and not tb else (out_stack or n), 1024)
    size = lambda dt: jnp.dtype(dt).itemsize
    for tk_cap, tm_cap in ((2048, 2048), (2048, 1024), (1024, 1024), (1024, 512), (1024, 256)):
        tm, tk = _tile(m, tm_cap), _tile(b.shape[2] if stacked and tb else kc, tk_cap)
        nk = kc // tk
        vmem = 2 * tm * tk * size(a.dtype) + 2 * tk * tn * size(b.dtype) + tm * tn * (2 * size(out_dtype) + (4 if nk > 1 else 0) + (8 if res is not None else 0))
        if vmem <= _MM_VMEM_BYTES:
            break
    if ta:
        assert a_off % tm == 0
        a_spec = pl.BlockSpec((tk, tm), lambda i, j, k, o=a_off // tm: (k, i + o))
        dims = _TN
    else:
        assert a_off % tk == 0
        a_spec = pl.BlockSpec((tm, tk), lambda i, j, k, o=a_off // tk: (i, k + o))
        dims = _NT if tb else (((1,), (0,)), ((), ()))
    if stacked and tb:
        b_spec = pl.BlockSpec((None, tn, tk), lambda i, j, k, q=b.shape[2] // tk: (k // q, j, k % q))
    elif stacked:
        b_spec = pl.BlockSpec((None, tk, tn), lambda i, j, k, q=b.shape[2] // tn: (j // q, k, j % q))
    else:
        b_spec = pl.BlockSpec((tn, tk), lambda i, j, k: (j, k)) if tb else pl.BlockSpec((tk, tn), lambda i, j, k: (k, j))
    if out_stack:
        out_spec = pl.BlockSpec((None, tm, tn), lambda i, j, k, q=out_stack // tn: (j // q, i, j % q))
        out_shape = jax.ShapeDtypeStruct((n // out_stack, m, out_stack), out_dtype)
    else:
        out_spec = pl.BlockSpec((tm, tn), lambda i, j, k: (i, j))
        out_shape = jax.ShapeDtypeStruct((m, n), out_dtype)
    in_specs, args = [a_spec, b_spec], [a, b]
    has_res = res is not None
    if has_res:
        in_specs.append(pl.BlockSpec((tm, tn), lambda i, j, k: (i, j)))
        args.append(res)

    def finish(refs, acc):
        if has_res:
            acc = acc + refs[2][...].astype(F32)
        refs[3 if has_res else 2][...] = acc.astype(out_dtype)

    def body_one(*refs):
        finish(refs, lax.dot_general(refs[0][...].astype(BF16), refs[1][...].astype(BF16), dims, preferred_element_type=F32))

    def body(*refs):
        acc_ref = refs[-1]
        k = pl.program_id(2)

        @pl.when(k == 0)
        def _():
            acc_ref[...] = jnp.zeros_like(acc_ref)

        acc_ref[...] += lax.dot_general(refs[0][...].astype(BF16), refs[1][...].astype(BF16), dims, preferred_element_type=F32)

        @pl.when(k == nk - 1)
        def _():
            finish(refs, acc_ref[...])

    return _pc(
        body_one if nk == 1 else body, name=name, grid=(m // tm, n // tn, nk), in_specs=in_specs, out_specs=out_spec, out_shape=out_shape,
        scratch_shapes=[] if nk == 1 else [pltpu.VMEM((tm, tn), F32)], sem=("parallel", "parallel", "arbitrary"),
    )(*args)


def _mm_blocks(name, a, b, *, grid, out_blk, a_blk, a_idx, b_blk, b_idx, dims, out_dtype=F32):
    gi, gj, nk = grid

    def body(a_ref, b_ref, o_ref, acc_ref):
        k = pl.program_id(2)

        @pl.when(k == 0)
        def _():
            acc_ref[...] = jnp.zeros_like(acc_ref)

        acc_ref[...] += lax.dot_general(a_ref[...].astype(BF16), b_ref[...].astype(BF16), dims, preferred_element_type=F32)

        @pl.when(k == nk - 1)
        def _():
            o_ref[...] = acc_ref[...].astype(o_ref.dtype)

    return _pc(
        body, name=name, grid=grid, in_specs=[pl.BlockSpec(a_blk, a_idx), pl.BlockSpec(b_blk, b_idx)],
        out_specs=pl.BlockSpec(out_blk, lambda i, j, k: (i, j)), out_shape=jax.ShapeDtypeStruct((gi * out_blk[0], gj * out_blk[1]), out_dtype),
        scratch_shapes=[pltpu.VMEM(out_blk, F32)], sem=("parallel", "parallel", "arbitrary"),
    )(a, b)


def _head_masks(width, nsub):
    lane = lax.broadcasted_iota(jnp.int32, (1, width), 1)
    hd = width // nsub
    return [(lane >= h * hd) & (lane < (h + 1) * hd) for h in range(nsub)]


def _dilated_log_count(row0, tq, ext):
    delta = (row0 + lax.broadcasted_iota(jnp.int32, (tq, ext), 0)) - lax.broadcasted_iota(jnp.int32, (tq, ext), 1)
    cnt = (delta <= 128).astype(jnp.int32) + (((delta & 3) == 0) & (delta <= 512)).astype(jnp.int32) + ((delta & 15) == 0).astype(jnp.int32)
    logc = jnp.where(cnt == 3, math.log(3.0), jnp.where(cnt == 2, math.log(2.0), 0.0))
    return jnp.where((delta >= 0) & (cnt > 0), logc, NEG)


def _bias_table(tab, nq, tq):
    @pl.when(pl.program_id(0) == 0)
    def _():
        for d in range(nq):
            tab[d] = _dilated_log_count(d * tq, tq, tq)


def _scores(q, ke, tab, r, masks, h):
    qm = (jnp.where(masks[h], q, 0.0) if len(masks) > 1 else q).astype(BF16)
    s = lax.dot_general(qm, ke, _NT, preferred_element_type=F32)
    if tab is not None:
        s = s + jnp.concatenate([tab[r - c] for c in range(r + 1)], axis=1)
    p = jnp.exp(s - jnp.max(s, axis=-1, keepdims=True))
    return qm, p, 1.0 / jnp.sum(p, axis=-1, keepdims=True)


def _attn_fwd(name, qa, ka, va, *, qc, kc, vc, width, nblk, nsub, causal, tq, scale, out_dtype):
    sq, t_len = qa.shape[0], ka.shape[0]
    nq = sq // tq

    def body(q_ref, k_ref, v_ref, o_ref, *scratch):
        tab = scratch[0] if causal else None
        if causal:
            _bias_table(tab, nq, tq)
        kb = k_ref[...].astype(BF16)
        vb = v_ref[...].astype(BF16)
        masks = _head_masks(width, nsub)
        for r in range(nq):
            ext = (r + 1) * tq if causal else t_len
            q = q_ref[r * tq:(r + 1) * tq, :].astype(F32) * scale
            ke, ve = kb[:ext], vb[:ext]
            o = None
            for h in range(nsub):
                _, p, inv = _scores(q, ke, tab, r, masks, h)
                oh = jnp.dot(p.astype(BF16), ve, preferred_element_type=F32) * inv
                o = oh if o is None else jnp.where(masks[h], oh, o)
            o_ref[r * tq:(r + 1) * tq, :] = o.astype(o_ref.dtype)

    return _pc(
        body, name=name, grid=(nblk,),
        in_specs=[pl.BlockSpec((sq, width), lambda i, c=qc: (0, c + i)), pl.BlockSpec((t_len, width), lambda i, c=kc: (0, c + i)),
                  pl.BlockSpec((t_len, width), lambda i, c=vc: (0, c + i))],
        out_specs=pl.BlockSpec((sq, width), lambda i: (0, i)),
        out_shape=jax.ShapeDtypeStruct((sq, nblk * width), out_dtype),
        scratch_shapes=[pltpu.VMEM((nq, tq, tq), F32)] if causal else [], sem=("arbitrary",),
    )(qa, ka, va)


def _attn_bwd(name, qa, ka, va, doa, *, qc, kc, vc, width, nblk, nsub, causal, tq, scale, out_dtype):
    sq, t_len = qa.shape[0], ka.shape[0]

    nq = sq // tq

    def body(q_ref, k_ref, v_ref, do_ref, dq_ref, dk_ref, dv_ref, dk_acc, dv_acc, *scratch):
        tab = scratch[0] if causal else None
        if causal:
            _bias_table(tab, nq, tq)
        kb = k_ref[...].astype(BF16)
        vb = v_ref[...].astype(BF16)
        masks = _head_masks(width, nsub)
        dk_acc[...] = jnp.zeros_like(dk_acc)
        dv_acc[...] = jnp.zeros_like(dv_acc)
        for r in range(nq):
            ext = (r + 1) * tq if causal else t_len
            q = q_ref[r * tq:(r + 1) * tq, :].astype(F32) * scale
            do = do_ref[r * tq:(r + 1) * tq, :].astype(F32)
            ke, ve = kb[:ext], vb[:ext]
            dq = None
            for h in range(nsub):
                qm, p, inv = _scores(q, ke, tab, r, masks, h)
                dom = (jnp.where(masks[h], do, 0.0) if nsub > 1 else do).astype(BF16)
                pn = p * inv
                dpn = lax.dot_general(dom, ve, _NT, preferred_element_type=F32)
                dsb = (pn * (dpn - jnp.sum(pn * dpn, axis=-1, keepdims=True))).astype(BF16)
                dqh = jnp.dot(dsb, ke, preferred_element_type=F32)
                dq = dqh if dq is None else jnp.where(masks[h], dqh, dq)
                dk_acc[0:ext, :] += lax.dot_general(dsb, qm, _TN, preferred_element_type=F32)
                dv_acc[0:ext, :] += lax.dot_general(pn.astype(BF16), dom, _TN, preferred_element_type=F32)
            dq_ref[r * tq:(r + 1) * tq, :] = (dq * scale).astype(dq_ref.dtype)
        dk_ref[...] = dk_acc[...].astype(dk_ref.dtype)
        dv_ref[...] = dv_acc[...].astype(dv_ref.dtype)

    return _pc(
        body, name=name, grid=(nblk,),
        in_specs=[pl.BlockSpec((sq, width), lambda i, c=qc: (0, c + i)), pl.BlockSpec((t_len, width), lambda i, c=kc: (0, c + i)),
                  pl.BlockSpec((t_len, width), lambda i, c=vc: (0, c + i)), pl.BlockSpec((sq, width), lambda i: (0, i))],
        out_specs=[pl.BlockSpec((sq, width), lambda i: (0, i)), pl.BlockSpec((t_len, width), lambda i: (0, i)), pl.BlockSpec((t_len, width), lambda i: (0, i))],
        out_shape=[jax.ShapeDtypeStruct((sq, nblk * width), out_dtype), jax.ShapeDtypeStruct((t_len, nblk * width), out_dtype),
                   jax.ShapeDtypeStruct((t_len, nblk * width), out_dtype)],
        scratch_shapes=[pltpu.VMEM((t_len, width), F32), pltpu.VMEM((t_len, width), F32)] + ([pltpu.VMEM((nq, tq, tq), F32)] if causal else []),
        sem=("arbitrary",),
    )(qa, ka, va, doa)


_SELF = dict(qc=0, kc=8, vc=16, width=128, nblk=8, nsub=2, causal=True, tq=256, scale=A_HEAD_DIM ** -0.5)
_CROSS = dict(qc=0, kc=0, vc=4, width=256, nblk=4, nsub=1, causal=False, tq=512, scale=X_HEAD_DIM ** -0.5)


def _window_sum(x, g, row, backward):
    n = x.shape[0]

    def shift(y, k):
        if backward:
            return jnp.where(row < n - k, pltpu.roll(y, n - k, 0), 0.0)
        return jnp.where(row >= k, pltpu.roll(y, k, 0), 0.0)

    s2 = x + shift(x, 1)
    s4 = s2 + shift(s2, 2)
    s8 = s4 + shift(s4, 4)
    s16 = s8 + shift(s8, 8)
    return jnp.where(g == 0, s2, jnp.where(g == 1, s4, jnp.where(g == 2, s8, s16)))


def _pool(name, arr, c0, backward, out_dtype):
    n = arr.shape[0]
    gw = 256

    def body(v_ref, o_ref):
        g = pl.program_id(0)
        v = v_ref[...].astype(F32)
        row = lax.broadcasted_iota(jnp.int32, v.shape, 0)
        w = jnp.where(g == 0, 2, jnp.where(g == 1, 4, jnp.where(g == 2, 8, 16)))
        cnt = jnp.minimum(row + 1, w).astype(F32)
        if backward:
            o_ref[...] = (_window_sum(v / cnt, g, row, True) - v).astype(o_ref.dtype)
        else:
            o_ref[...] = (_window_sum(v, g, row, False) / cnt - v).astype(o_ref.dtype)

    return _pc(
        body, name=name, grid=(4,), in_specs=[pl.BlockSpec((n, gw), lambda i, c=c0 // gw: (0, c + i))],
        out_specs=pl.BlockSpec((n, gw), lambda i: (0, i)), out_shape=jax.ShapeDtypeStruct((n, 4 * gw), out_dtype), sem=("parallel",),
    )(arr)


_SCAN_ROWS = 256


def _scan_fwd(bu3, a2):
    n = bu3.shape[0]

    def body(bu_ref, a_ref, h_ref, carry):
        @pl.when(pl.program_id(0) == 0)
        def _():
            carry[...] = jnp.zeros_like(carry)

        ar, ai = a_ref[0:16, :], a_ref[16:32, :]

        def step(t, c):
            hr, hi = c
            nr = ar * hr - ai * hi + bu_ref[t, 0:16, :]
            ni = ar * hi + ai * hr + bu_ref[t, 16:32, :]
            h_ref[t, 0:16, :] = nr
            h_ref[t, 16:32, :] = ni
            return nr, ni

        hr, hi = lax.fori_loop(0, _SCAN_ROWS, step, (carry[0:16, :], carry[16:32, :]), unroll=8)
        carry[0:16, :] = hr
        carry[16:32, :] = hi

    blk = pl.BlockSpec((_SCAN_ROWS, 32, 128), lambda i: (i, 0, 0))
    return _pc(
        body, name="s5_scan_fwd", grid=(n // _SCAN_ROWS,), in_specs=[blk, pl.BlockSpec((32, 128), lambda i: (0, 0))], out_specs=blk,
        out_shape=jax.ShapeDtypeStruct(bu3.shape, F32), scratch_shapes=[pltpu.VMEM((32, 128), F32)], sem=("arbitrary",),
    )(bu3, a2)


def _scan_bwd(dh3, h3, a2):
    n = dh3.shape[0]
    nb = n // _SCAN_ROWS

    def body(dh_ref, h_ref, a_ref, dbu_ref, da_ref, carry):
        @pl.when(pl.program_id(0) == 0)
        def _():
            carry[...] = jnp.zeros_like(carry)
            da_ref[...] = jnp.zeros_like(da_ref)

        ar, ai = a_ref[0:16, :], a_ref[16:32, :]

        def step(tt, c):
            gr, gi, dar, dai = c
            t = _SCAN_ROWS - 1 - tt
            hr, hi = h_ref[t, 0:16, :], h_ref[t, 16:32, :]
            dar = dar + gr * hr + gi * hi
            dai = dai - gr * hi + gi * hr
            ngr = dh_ref[t, 0:16, :] + ar * gr + ai * gi
            ngi = dh_ref[t, 16:32, :] - ai * gr + ar * gi
            dbu_ref[t, 0:16, :] = ngr
            dbu_ref[t, 16:32, :] = ngi
            return ngr, ngi, dar, dai

        z = jnp.zeros((16, 128), F32)
        gr, gi, dar, dai = lax.fori_loop(0, _SCAN_ROWS, step, (carry[0:16, :], carry[16:32, :], z, z), unroll=8)
        carry[0:16, :] = gr
        carry[16:32, :] = gi
        da_ref[0:16, :] += dar
        da_ref[16:32, :] += dai

    blk = pl.BlockSpec((_SCAN_ROWS, 32, 128), lambda i: (nb - 1 - i, 0, 0))
    small = pl.BlockSpec((32, 128), lambda i: (0, 0))
    return _pc(
        body, name="s5_scan_bwd", grid=(nb,), in_specs=[blk, blk, small], out_specs=[blk, small],
        out_shape=[jax.ShapeDtypeStruct(dh3.shape, F32), jax.ShapeDtypeStruct((32, 128), F32)],
        scratch_shapes=[pltpu.VMEM((32, 128), F32)], sem=("arbitrary",),
    )(dh3, h3, a2)


def _bdense(bb_re, bb_im):
    eye = jnp.eye(8, dtype=F32)

    def one(bb):
        return jnp.einsum("sgph,gk->sghkp", bb.reshape(4, 8, S5_STATE, S5_GROUP_DIM), eye).reshape(512, 512)

    return jnp.concatenate([one(bb_re), one(bb_im)], axis=1)


def _cdense(c_re, c_im):
    eye = jnp.eye(8, dtype=F32)

    def one(cc):
        return jnp.einsum("sghp,gk->sgpkh", cc.reshape(4, 8, S5_GROUP_DIM, S5_STATE), eye).reshape(2048, 128)

    return jnp.concatenate([one(c_re), -one(c_im)], axis=0)


_NN = (((1,), (0,)), ((), ()))
_UF_BLOCK = 3072 // 128


def _s5_bu(z, bd):
    return _mm_blocks("mm_s5_bu", z, bd, grid=(1, 8, 1), out_blk=(SEQ, 512), a_blk=(SEQ, 128), a_idx=lambda i, j, k: (0, _UF_BLOCK + j % 4),
                      b_blk=(128, 512), b_idx=lambda i, j, k: (j % 4, j // 4), dims=_NN)


def _s5_bu_dx(dbu, bd):
    return _mm_blocks("mm_s5_bu_dx", dbu, bd, grid=(1, 4, 2), out_blk=(SEQ, 128), a_blk=(SEQ, 512), a_idx=lambda i, j, k: (0, 4 * k + j),
                      b_blk=(128, 512), b_idx=lambda i, j, k: (j, k), dims=_NT)


def _s5_bu_dw(z, dbu):
    return _mm_blocks("mm_s5_bu_dw", z, dbu, grid=(4, 2, 2), out_blk=(128, 512), a_blk=(1024, 128), a_idx=lambda i, j, k: (k, _UF_BLOCK + i),
                      b_blk=(1024, 512), b_idx=lambda i, j, k: (k, 4 * j + i), dims=_TN)


def _s5_y(h2, cf):
    return _mm_blocks("mm_s5_y", h2, cf, grid=(1, 4, 2), out_blk=(SEQ, 128), a_blk=(SEQ, 512), a_idx=lambda i, j, k: (0, 4 * k + j),
                      b_blk=(512, 128), b_idx=lambda i, j, k: (4 * k + j, 0), dims=_NN)


def _s5_y_dx(dyc, cf):
    return _mm_blocks("mm_s5_y_dx", dyc, cf, grid=(1, 8, 1), out_blk=(SEQ, 512), a_blk=(SEQ, 128), a_idx=lambda i, j, k: (0, j % 4),
                      b_blk=(512, 128), b_idx=lambda i, j, k: (j, 0), dims=_NT)


def _s5_y_dw(h2, dyc):
    return _mm_blocks("mm_s5_y_dw", h2, dyc, grid=(8, 1, 2), out_blk=(512, 128), a_blk=(1024, 512), a_idx=lambda i, j, k: (k, i),
                      b_blk=(1024, 128), b_idx=lambda i, j, k: (k, i % 4), dims=_TN)


def _pool_dense(pw):
    eye = jnp.eye(4, dtype=pw.dtype)
    return jnp.einsum("gcd,gk->gckd", pw, eye).reshape(1024, 1024)


def _row2(v):
    return v.reshape(1, -1)


def _even_fwd(x, W, i, zero, rest_of_weights):
    hn = _rw_fwd("rms_fwd", _f_rms, [_cols(x)], [_par(_row2(W["norm_ab"][i]) + zero)], [(D_MODEL, BF16, 1)], 256)[0]
    z = _mm("mm_in_ab", hn, W["w_in_ab"][i])
    o = _attn_fwd("attn_self_fwd", z, z, z, out_dtype=F32, **_SELF)
    zero = rest_of_weights(o)
    pooled = _pool("pool_fwd", z, 4096, False, BF16)
    wp = _pool_dense(W["pool_w"][i])
    mixed = _mm("mm_pool", pooled, wp)
    scale = _row2(W["pool_scale"][i]) + zero
    ab = _rw_fwd("gate_ab_fwd", _f_gate_ab, [_cols(o), _cols(z, 3072, 1024), _cols(mixed), _cols(z, 5120, 1024)], [_par(scale)],
                 [(2048, BF16, 2)], 256)[0]
    x1 = _mm("mm_out_ab", ab, W["w_out_ab"][i], res=x)
    return x1, dict(x=x, hn=hn, z=z, o=o, pooled=pooled, wp=wp, mixed=mixed, ab=ab), 0.0


def _even_bwd(dx1, sv, W, G, i, send):
    x, hn, z = sv["x"], sv["hn"], sv["z"]
    dab = _mm("mm_out_ab_dx", dx1, W["w_out_ab"][i], tb=True)
    G["w_out_ab"][i] = _mm("mm_out_ab_dw", sv["ab"], dx1, ta=True, out_dtype=BF16)
    scale = _row2(W["pool_scale"][i]) + send(0)
    (do, dga, dmixed, dgb), (dscale,) = _rw_bwd(
        "gate_ab_bwd", _f_gate_ab, [_cols(sv["o"]), _cols(z, 3072, 1024), _cols(sv["mixed"]), _cols(z, 5120, 1024)], [_par(scale)],
        [_cols(dab, nsplit=2)], [(0, F32), (1, BF16), (2, BF16), (3, BF16)], [0], 256)
    G["pool_scale"][i] = dscale.reshape(-1)
    dpooled = _mm("mm_pool_dx", dmixed, sv["wp"], tb=True)
    dwp = _mm("mm_pool_dw", sv["pooled"], dmixed, ta=True, out_dtype=BF16)
    G["pool_w"][i] = jnp.stack([dwp[g * 256:(g + 1) * 256, g * 256:(g + 1) * 256] for g in range(4)])
    dvb = _pool("pool_bwd", dpooled, 0, True, BF16)
    dq, dk, dv = _attn_bwd("attn_self_bwd", z, z, z, do, out_dtype=BF16, **_SELF)
    dz = jnp.concatenate([dq, dk, dv, dga, dvb, dgb], axis=1)
    dhn = _mm("mm_in_ab_dx", dz, W["w_in_ab"][i], tb=True)
    G["w_in_ab"][i] = _mm("mm_in_ab_dw", hn, dz, ta=True, out_dtype=BF16, out_stack=W["w_in_ab"][i].shape[2])
    g = _row2(W["norm_ab"][i]) + send(1)
    (dx,), (dg,) = _rw_bwd("rms_bwd", _f_rms_res, [_cols(x)], [_par(g)], [_cols(dx1), _cols(dhn)], [(0, F32)], [0], 256)
    G["norm_ab"][i] = dg.reshape(-1)
    return dx


def _odd_fwd(x, W, i, zero, rest_of_weights):
    hn = _rw_fwd("rms_fwd", _f_rms, [_cols(x)], [_par(_row2(W["norm_cd"][i]) + zero)], [(D_MODEL, BF16, 1)], 256)[0]
    z = _mm("mm_in_cd", hn, W["w_in_cd"][i])
    sgu_p = [_par(_row2(W["sgu_ln_g"][i]), 4), _par(_row2(W["sgu_ln_b"][i]), 4), _par(W["sgu_w"][i], 4), _par(W["sgu_b"][i][..., None], 4)]
    c_out = _rw_fwd("sgu_fwd", _f_sgu, [_cols(z, 0, 1024, 4), _cols(z, 1024, 1024, 4), _cols(z, 2048, 1024, 4)], sgu_p, [(1024, BF16, 4)], 128)[0]
    prep_rows = [_cols(W["s5_a_re"][i]), _cols(W["s5_a_im"][i]), _cols(W["s5_log_dt"][i].reshape(S5_GROUPS, 1))]
    abar_re, abar_im, coef_re, coef_im = _rw_fwd("s5_prep_fwd", _f_s5_prep, prep_rows, [], [(S5_STATE, F32, 1)] * 4, S5_GROUPS)
    bb_rows = [_cols(W["s5_b_re"][i].reshape(2048, 16)), _cols(W["s5_b_im"][i].reshape(2048, 16)), _cols(coef_re.reshape(2048, 1)), _cols(coef_im.reshape(2048, 1))]
    bb_re, bb_im = _rw_fwd("s5_bbar_fwd", _f_bbar, bb_rows, [], [(16, F32, 1)] * 2, 256)
    bd = _bdense(bb_re, bb_im).astype(BF16)
    cf = _cdense(W["s5_c_re"][i], W["s5_c_im"][i]).astype(BF16)
    a2 = jnp.concatenate([abar_re.reshape(16, 128), abar_im.reshape(16, 128)], axis=0)
    bu = _s5_bu(z, bd)
    h3 = _scan_fwd(bu.reshape(SEQ, 32, 128), a2)
    h2 = h3.reshape(SEQ, 4096)
    yc = _s5_y(h2, cf)
    dpar = _row2(W["s5_d"][i])
    yg = _rw_fwd("gelu_fwd", _f_gelu_y, [_cols(yc), _cols(z, 3072, 512)], [_par(dpar)], [(512, BF16, 1)], 256)[0]
    zero = rest_of_weights(yg)
    w12 = jnp.concatenate([W["glu_w1"][i], W["glu_w2"][i]], axis=1)
    t12 = _mm("mm_glu", yg, w12)
    d_out = _rw_fwd("glu_gate_fwd", _f_glu_gate, [_cols(t12, nsplit=2), _cols(z, 3584, 512)], [], [(512, BF16, 1)], 256)[0]
    cd = jnp.concatenate([c_out, d_out], axis=1)
    x1 = _mm("mm_out_cd", cd, W["w_out_cd"][i], res=x)
    sv = dict(x=x, hn=hn, z=z, sgu_p=sgu_p, prep_rows=prep_rows, bb_rows=bb_rows, bb=(bb_re, bb_im), bd=bd, cf=cf, a2=a2,
              h3=h3, h2=h2, yc=yc, dpar=dpar, yg=yg, w12=w12, t12=t12, cd=cd)
    return x1, sv, zero


def _odd_bwd(dx1, sv, W, G, i, send):
    x, hn, z = sv["x"], sv["hn"], sv["z"]
    dcd = _mm("mm_out_cd_dx", dx1, W["w_out_cd"][i], tb=True)
    G["w_out_cd"][i] = _mm("mm_out_cd_dw", sv["cd"], dx1, ta=True, out_dtype=BF16)
    (du, dv, dgc), (dlg, dlb, dsw, dsb) = _rw_bwd(
        "sgu_bwd", _f_sgu, [_cols(z, 0, 1024, 4), _cols(z, 1024, 1024, 4), _cols(z, 2048, 1024, 4)], sv["sgu_p"],
        [_cols(dcd, 0, 1024, 4)], [(0, BF16), (1, BF16), (2, BF16)], [0, 1, 2, 3], 128)
    G["sgu_ln_g"][i], G["sgu_ln_b"][i] = dlg.reshape(-1), dlb.reshape(-1)
    G["sgu_w"][i], G["sgu_b"][i] = dsw, dsb[..., 0]
    (dt12, dgd), _ = _rw_bwd("glu_gate_bwd", _f_glu_gate, [_cols(sv["t12"], nsplit=2), _cols(z, 3584, 512)], [], [_cols(dcd, 1024, 512)],
                             [(0, BF16), (1, BF16)], [], 256)
    dyg = _mm("mm_glu_dx", dt12, sv["w12"], tb=True)
    dw12 = _mm("mm_glu_dw", sv["yg"], dt12, ta=True, out_dtype=BF16)
    G["glu_w1"][i], G["glu_w2"][i] = dw12[:, :512], dw12[:, 512:]
    (dyc, duf1), (dd,) = _rw_bwd("gelu_bwd", _f_gelu_y, [_cols(sv["yc"]), _cols(z, 3072, 512)], [_par(sv["dpar"])], [_cols(dyg)],
                                 [(0, BF16), (1, F32)], [0], 256)
    G["s5_d"][i] = dd.reshape(-1)
    dh2 = _s5_y_dx(dyc, sv["cf"])
    dcf = _s5_y_dw(sv["h2"], dyc)
    _, cvjp = jax.vjp(_cdense, W["s5_c_re"][i], W["s5_c_im"][i])
    G["s5_c_re"][i], G["s5_c_im"][i] = cvjp(dcf)
    dbu3, da2 = _scan_bwd(dh2.reshape(SEQ, 32, 128), sv["h3"], sv["a2"])
    dbu = dbu3.reshape(SEQ, 4096)
    duf2 = _s5_bu_dx(dbu, sv["bd"])
    dbd = _s5_bu_dw(z, dbu)
    _, bvjp = jax.vjp(_bdense, *sv["bb"])
    dbb_re, dbb_im = bvjp(dbd)
    (dbr, dbi, dcr, dci), _ = _rw_bwd("s5_bbar_bwd", _f_bbar, sv["bb_rows"], [], [_cols(dbb_re), _cols(dbb_im)],
                                      [(0, F32), (1, F32), (2, F32), (3, F32)], [], 256)
    G["s5_b_re"][i], G["s5_b_im"][i] = dbr.reshape(S5_GROUPS, S5_STATE, S5_GROUP_DIM), dbi.reshape(S5_GROUPS, S5_STATE, S5_GROUP_DIM)
    douts = [_cols(da2[0:16].reshape(S5_GROUPS, S5_STATE)), _cols(da2[16:32].reshape(S5_GROUPS, S5_STATE)),
             _cols(dcr.reshape(S5_GROUPS, S5_STATE)), _cols(dci.reshape(S5_GROUPS, S5_STATE))]
    (dar, dai, dldt), _ = _rw_bwd("s5_prep_bwd", _f_s5_prep, sv["prep_rows"], [], douts, [(0, F32), (1, F32), (2, F32)], [], S5_GROUPS)
    G["s5_a_re"][i], G["s5_a_im"][i], G["s5_log_dt"][i] = dar, dai, dldt.reshape(-1)
    dxd = (duf1 + duf2).astype(BF16)
    dz = jnp.concatenate([du, dv, dgc, dxd, dgd], axis=1)
    dhn = _mm("mm_in_cd_dx", dz, W["w_in_cd"][i], tb=True)
    G["w_in_cd"][i] = _mm("mm_in_cd_dw", hn, dz, ta=True, out_dtype=BF16, out_stack=W["w_in_cd"][i].shape[2])
    g = _row2(W["norm_cd"][i]) + send(0)
    (dx,), (dg,) = _rw_bwd("rms_bwd", _f_rms_res, [_cols(x)], [_par(g)], [_cols(dx1), _cols(dhn)], [(0, F32)], [0], 256)
    G["norm_cd"][i] = dg.reshape(-1)
    return dx


def _cross_fwd(x1, mem_n, W, l, zero):
    hx = _rw_fwd("rms_fwd", _f_rms, [_cols(x1)], [_par(_row2(W["norm_x"][l]) + zero)], [(D_MODEL, BF16, 1)], 256)[0]
    qx = _mm("mm_xq", hx, W["w_xq"][l], out_dtype=BF16)
    kv = _mm("mm_xkv", mem_n, W["w_xkv"][l], out_dtype=BF16)
    ox = _attn_fwd("attn_cross_fwd", qx, kv, kv, out_dtype=BF16, **_CROSS)
    x2 = _mm("mm_xo", ox, W["w_xo"][l], res=x1)
    return x2, dict(x1=x1, hx=hx, qx=qx, kv=kv, ox=ox)


def _cross_bwd(dx2, dmem_n, sv, mem_n, W, G, l, zero):
    dox = _mm("mm_xo_dx", dx2, W["w_xo"][l], tb=True, out_dtype=BF16)
    G["w_xo"][l] = _mm("mm_xo_dw", sv["ox"], dx2, ta=True, out_dtype=BF16)
    dqx, dk, dv = _attn_bwd("attn_cross_bwd", sv["qx"], sv["kv"], sv["kv"], dox, out_dtype=BF16, **_CROSS)
    dkv = jnp.concatenate([dk, dv], axis=1)
    dhx = _mm("mm_xq_dx", dqx, W["w_xq"][l], tb=True)
    G["w_xq"][l] = _mm("mm_xq_dw", sv["hx"], dqx, ta=True, out_dtype=BF16)
    dmem_n = _mm("mm_xkv_dx", dkv, W["w_xkv"][l], tb=True, res=dmem_n)
    G["w_xkv"][l] = _mm("mm_xkv_dw", mem_n, dkv, ta=True, out_dtype=BF16, out_stack=W["w_xkv"][l].shape[2])
    (dx1,), (dg,) = _rw_bwd("rms_bwd", _f_rms_res, [_cols(sv["x1"])], [_par(_row2(W["norm_x"][l]) + zero)], [_cols(dx2), _cols(dhx)], [(0, F32)], [0], 256)
    G["norm_x"][l] = dg.reshape(-1)
    return dx1, dmem_n


_PER_LAYER = ("pool_scale", "norm_ab", "norm_cd", "sgu_ln_g", "sgu_ln_b", "sgu_w", "sgu_b", "s5_d", "s5_c_re", "s5_c_im", "s5_b_re", "s5_b_im",
              "s5_a_re", "s5_a_im", "s5_log_dt", "w_in_ab", "pool_w", "w_out_ab", "w_in_cd", "glu_w1", "glu_w2", "w_out_cd")


def _local_step(x, mem, target, W, weights_of, send_grads, after_layer):
    G = {k: [None, None] for k in _PER_LAYER}
    for k in ("norm_x", "w_xq", "w_xkv", "w_xo"):
        G[k] = [None] * DEPTH
    mem_rows = [_cols(mem)]
    mem_par = [_par(_row2(W["mem_norm"]))]
    mem_n = _rw_fwd("rms_fwd_mem", _f_rms, mem_rows, mem_par, [(D_MODEL, BF16, 1)], 256)[0]
    saved = []
    for layer in range(DEPTH):
        zero = weights_of(layer, 0, x if layer else mem_n)
        mixer = _even_fwd if layer % 2 == 0 else _odd_fwd
        x, sv, zero = mixer(x, W, layer // 2, zero, functools.partial(weights_of, layer, 1))
        x, svx = _cross_fwd(x, mem_n, W, layer, zero)
        saved.append((sv, svx))
    loss, dx, dfinal = _loss_head(x, target, _row2(W["final_norm"]))
    G["final_norm"] = dfinal.reshape(-1)
    dmem_n, zero = None, 0.0
    for layer in reversed(range(DEPTH)):
        sv, svx = saved[layer]
        dx, dmem_n = _cross_bwd(dx, dmem_n, svx, mem_n, W, G, layer, zero)
        hook = functools.partial(send_grads, layer, G)
        dx = _even_bwd(dx, sv, W, G, layer // 2, hook) if layer % 2 == 0 else _odd_bwd(dx, sv, W, G, layer // 2, hook)
        zero = after_layer(layer, G)
    _, (dmn,) = _rw_bwd("rms_bwd_mem", _f_rms, mem_rows, mem_par, [_cols(dmem_n)], [], [0], 256)
    G["mem_norm"] = dmn.reshape(-1)
    return loss, dx, G


_HBM = pl.BlockSpec(memory_space=pltpu.HBM)
_ANY = pl.BlockSpec(memory_space=pl.ANY)
_SEM = pl.BlockSpec(memory_space=pltpu.SEMAPHORE)
_N_PEERS = N_DEV - 1


def _mesh_pos():
    return lax.axis_index("x"), lax.axis_index("y"), lax.axis_index("c")


def _peer(pos, k):
    x, y, c = pos
    return (x ^ ((k >> 2) & 1), y ^ ((k >> 1) & 1), c ^ (k & 1))


def _lin(pos):
    return 4 * pos[0] + 2 * pos[1] + pos[2]


def _ends(gather, srcs, lands, t, sender, receiver):
    if gather:
        return lands[t].at[sender], lands[t].at[sender]
    whole = len(srcs[t].shape) != len(lands[t].shape)
    return (srcs[t] if whole else srcs[t].at[receiver]), lands[t].at[sender]


def _into_slot(name, b2, r0, r, me, dtype, after):
    c = b2.shape[1]
    tr = _row_block(r, c, 2 << 20)
    assert r0 % tr == 0

    def body(me_ref, x_ref, *rest):
        rest[-1][...] = x_ref[...].astype(dtype)

    extra = [] if after is None else [after]
    grid_spec = pltpu.PrefetchScalarGridSpec(
        num_scalar_prefetch=1, grid=(r // tr,),
        in_specs=[pl.BlockSpec((tr, c), lambda i, me, o=r0 // tr: (o + i, 0))] + [_ANY] * len(extra),
        out_specs=pl.BlockSpec((None, tr, c), lambda i, me: (me[0], i, 0)))
    return pl.pallas_call(
        body, name=name, grid_spec=grid_spec, out_shape=jax.ShapeDtypeStruct((N_DEV, r, c), dtype),
        compiler_params=pltpu.CompilerParams(dimension_semantics=("arbitrary",), vmem_limit_bytes=V7X_VMEM_LIMIT_BYTES),
        interpret=False,
    )(me, b2, *extra)


def _exchange_start(name, gather, srcs, lands, after=None):
    ns, nt = len(srcs), len(lands)
    arrs = list(srcs) + list(lands)
    extra = [] if after is None else [after]

    def body(*refs):
        ins, lnd = refs[:ns], refs[ns:ns + nt]
        refs = refs[len(extra):]
        send_sems, recv_sems = refs[ns + nt], refs[ns + nt + 1]
        token = refs[-1]
        pos = _mesh_pos()
        me = _lin(pos)
        for k in range(1, N_DEV):
            peer = _peer(pos, k)
            for t in range(nt):
                src, dst = _ends(gather, ins, lnd, t, me, _lin(peer))
                pltpu.make_async_remote_copy(
                    src_ref=src, dst_ref=dst, send_sem=send_sems.at[t * _N_PEERS + k - 1], recv_sem=recv_sems.at[t * _N_PEERS + k - 1],
                    device_id=peer, device_id_type=pl.DeviceIdType.MESH).start()
        token[...] = jnp.zeros_like(token)

    out = pl.pallas_call(
        body, name=name,
        out_shape=(pltpu.SemaphoreType.DMA((nt * _N_PEERS,)), pltpu.SemaphoreType.DMA((nt * _N_PEERS,)), *[pltpu.HBM(a.shape, a.dtype) for a in arrs],
                   jax.ShapeDtypeStruct((8, 128), F32)),
        in_specs=[_HBM] * (ns + nt) + [_ANY] * len(extra), out_specs=(_SEM, _SEM, *[_HBM] * (ns + nt), pl.BlockSpec(memory_space=pltpu.VMEM)),
        input_output_aliases={j: 2 + j for j in range(ns + nt)},
        compiler_params=pltpu.CompilerParams(has_side_effects=pltpu.SideEffectType.DATAFLOW_SIDE_EFFECTING),
        interpret=False,
    )(*[pltpu.with_memory_space_constraint(a, pltpu.HBM) for a in arrs], *extra)
    return dict(send=out[0], recv=out[1], srcs=list(out[2:2 + ns]), lands=list(out[2 + ns:2 + ns + nt]), token=out[-1][0, 0], token_arr=out[-1], gather=gather)


def _exchange_wait(name, ex, after):
    ns, nt = len(ex["srcs"]), len(ex["lands"])
    gather = ex["gather"]
    arrs = ex["srcs"] + ex["lands"]
    after = list(after) if isinstance(after, (list, tuple)) else [after]

    def body(*refs):
        ins, lnd = refs[:ns], refs[ns:ns + nt]
        send_sems, recv_sems = refs[ns + nt], refs[ns + nt + 1]
        pos = _mesh_pos()
        me = _lin(pos)
        for k in range(1, N_DEV):
            peer = _peer(pos, k)
            for t in range(nt):
                src, _ = _ends(gather, ins, lnd, t, me, _lin(peer))
                _, dst = _ends(gather, ins, lnd, t, _lin(peer), me)
                cp = pltpu.make_async_remote_copy(
                    src_ref=src, dst_ref=dst, send_sem=send_sems.at[t * _N_PEERS + k - 1], recv_sem=recv_sems.at[t * _N_PEERS + k - 1],
                    device_id=peer, device_id_type=pl.DeviceIdType.MESH)
                cp.wait_send()
                cp.wait_recv()

    out = pl.pallas_call(
        body, name=name, out_shape=tuple(pltpu.HBM(a.shape, a.dtype) for a in arrs),
        in_specs=[_HBM] * (ns + nt) + [_SEM, _SEM] + [_ANY] * len(after), out_specs=tuple([_HBM] * (ns + nt)),
        input_output_aliases={j: j for j in range(ns + nt)},
        compiler_params=pltpu.CompilerParams(has_side_effects=pltpu.SideEffectType.DATAFLOW_SIDE_EFFECTING),
        interpret=False,
    )(*arrs, ex["send"], ex["recv"], *after)
    return list(out[:ns]), list(out[ns:])


def _scatter_begin(name, srcs):
    lands = [lax.empty(s.shape if s.ndim == 3 else (N_DEV,) + s.shape, s.dtype) for s in srcs]
    return _exchange_start(name, False, srcs, lands)


def _adam(name, w, m, v, parts, own, me, layer, bufs):
    r, c = parts.shape[1:]
    tr = _row_block(r, c, 1 << 20)
    nb = r // tr

    def body(me_ref, w_ref, m_ref, v_ref, p_ref, own_ref, *rest):
        g_ref, d_ref, nm_ref, nv_ref, acc = rest[-5:]
        acc[...] = jnp.zeros_like(acc)
        for k in range(N_DEV):
            @pl.when(me_ref[0] == k)
            def _():
                acc[...] += own_ref[...].astype(F32)

            @pl.when(me_ref[0] != k)
            def _(k=k):
                acc[...] += p_ref[k].astype(F32)

        g = acc[...]
        mm = ADAM_B1 * m_ref[...] + (1.0 - ADAM_B1) * g
        vv = ADAM_B2 * v_ref[...] + (1.0 - ADAM_B2) * jnp.square(g)
        m_hat = mm / (1.0 - ADAM_B1 ** ADAM_STEP)
        v_hat = vv / (1.0 - ADAM_B2 ** ADAM_STEP)
        g_ref[...] = g
        d_ref[...] = -ADAM_LR * (m_hat / (jnp.sqrt(v_hat) + ADAM_EPS) + ADAM_WD * w_ref[...])
        nm_ref[...] = mm
        nv_ref[...] = vv

    blk = pl.BlockSpec((tr, c), lambda i, me, o=layer * nb: (o + i, 0))
    own_spec = pl.BlockSpec((None, tr, c), lambda i, me: (me[0], i, 0)) if own.ndim == 3 else pl.BlockSpec((tr, c), lambda i, me: (i, 0))
    in_specs = [blk, blk, blk, pl.BlockSpec((N_DEV, tr, c), lambda i, me: (0, i, 0)), own_spec]
    args = [me, w, m, v, parts, own]
    aliases = {}
    if bufs is not None:
        in_specs += [_ANY] * 4
        aliases = {len(args) + j: j for j in range(4)}
        args += list(bufs)
    grid_spec = pltpu.PrefetchScalarGridSpec(
        num_scalar_prefetch=1, grid=(nb,), in_specs=in_specs, out_specs=[blk] * 4, scratch_shapes=[pltpu.VMEM((tr, c), F32)])
    return pl.pallas_call(
        body, name=name, grid_spec=grid_spec, out_shape=[jax.ShapeDtypeStruct(w.shape, F32)] * 4, input_output_aliases=aliases,
        compiler_params=pltpu.CompilerParams(dimension_semantics=("arbitrary",), vmem_limit_bytes=V7X_VMEM_LIMIT_BYTES),
        interpret=False,
    )(*args)


def _pack(arrs):
    out = []
    for a in arrs:
        f = a.reshape(-1)
        out.append(jnp.pad(f, (0, (-f.shape[0]) % 1024)).reshape(-1, 128))
    rows = sum(a.shape[0] for a in out)
    out.append(jnp.zeros(((-rows) % 256, 128), out[0].dtype))
    return jnp.concatenate(out, axis=0)


def _unpack(packed, shapes):
    out, r = [], 0
    for s in shapes:
        n = math.prod(s)
        rows = (n + 1023) // 1024 * 8
        out.append(packed[r:r + rows].reshape(-1)[:n].reshape(s))
        r += rows
    return out


def _row_block(r, c, limit):
    best = None
    for tr in range(16, r + 1, 16):
        if r % tr == 0 and tr * c * 4 <= limit:
            best = tr
    return r if best is None else best


_BIG = ("w_in_ab", "pool_w", "w_out_ab", "w_in_cd", "glu_w1", "glu_w2", "w_out_cd", "w_xq", "w_xkv", "w_xo")
_STACKED = ("w_in_ab", "w_in_cd", "w_xkv")
_MIXER_BIG = (("w_in_ab", "pool_w", "w_out_ab"), ("w_in_cd", "glu_w1", "glu_w2", "w_out_cd"))
_CROSS_BIG = ("w_xq", "w_xkv", "w_xo")
_SMALL_SPLIT = ["norm_cd", "sgu_ln_g", "sgu_ln_b", "s5_d"]
_REPLICATED_ODD = ["sgu_w", "sgu_b", "s5_a_re", "s5_a_im", "s5_log_dt", "s5_b_re", "s5_b_im", "s5_c_re", "s5_c_im", "final_norm"]
_REPLICATED_EVEN = ["norm_ab", "pool_scale", "norm_x", "mem_norm"]
_REPLICATED = _REPLICATED_ODD + _REPLICATED_EVEN
_WEIGHTS = ["norm_ab", "w_in_ab", "pool_w", "pool_scale", "w_out_ab", "norm_cd", "w_in_cd", "sgu_ln_g", "sgu_ln_b", "sgu_w", "sgu_b", "s5_a_re",
            "s5_a_im", "s5_log_dt", "s5_b_re", "s5_b_im", "s5_c_re", "s5_c_im", "s5_d", "glu_w1", "glu_w2", "w_out_cd", "norm_x", "w_xq",
            "w_xkv", "w_xo", "mem_norm", "final_norm"]


def _layer_big(layer):
    return [(n, layer // 2) for n in _MIXER_BIG[layer % 2]] + [(n, layer) for n in _CROSS_BIG]


def _gather_parts(layer):
    big = _layer_big(layer)
    return [big[:1], big[1:]]


def _scatter_parts(layer):
    big = _layer_big(layer)
    return [big[2:], big[:2]] if layer % 2 == 0 else [big]


def _from_slots(name, a):
    if name in _STACKED:
        return a
    if name == "pool_w":
        return a.reshape(N_DEV, 4, 32, 256).transpose(1, 0, 2, 3).reshape(4, 256, 256)
    return a.reshape(-1, a.shape[-1])


def _to_slots(name, g):
    if name in _STACKED:
        return g
    if name == "pool_w":
        return g.reshape(4, N_DEV, 32, 256).transpose(1, 0, 2, 3).reshape(N_DEV, 128, 256)
    return g.reshape(N_DEV, -1, g.shape[-1])


def _rows2d(a):
    return a.reshape(-1, a.shape[-1])


def _small_rows(block):
    return jnp.pad(block, ((0, 0), (0, 128 - block.shape[1])))


def kernel(x, mem, norm_ab, w_in_ab, pool_w, pool_scale, w_out_ab, norm_cd, w_in_cd, sgu_ln_g, sgu_ln_b, sgu_w, sgu_b, s5_a_re, s5_a_im, s5_log_dt, s5_b_re, s5_b_im, s5_c_re, s5_c_im, s5_d, glu_w1, glu_w2, w_out_cd, norm_x, w_xq, w_xkv, w_xo, mem_norm, final_norm, loss_target, m_norm_ab, m_w_in_ab, m_pool_w, m_pool_scale, m_w_out_ab, m_norm_cd, m_w_in_cd, m_sgu_ln_g, m_sgu_ln_b, m_sgu_w, m_sgu_b, m_s5_a_re, m_s5_a_im, m_s5_log_dt, m_s5_b_re, m_s5_b_im, m_s5_c_re, m_s5_c_im, m_s5_d, m_glu_w1, m_glu_w2, m_w_out_cd, m_norm_x, m_w_xq, m_w_xkv, m_w_xo, m_mem_norm, m_final_norm, v_norm_ab, v_w_in_ab, v_pool_w, v_pool_scale, v_w_out_ab, v_norm_cd, v_w_in_cd, v_sgu_ln_g, v_sgu_ln_b, v_sgu_w, v_sgu_b, v_s5_a_re, v_s5_a_im, v_s5_log_dt, v_s5_b_re, v_s5_b_im, v_s5_c_re, v_s5_c_im, v_s5_d, v_glu_w1, v_glu_w2, v_w_out_cd, v_norm_x, v_w_xq, v_w_xkv, v_w_xo, v_mem_norm, v_final_norm):
    args = locals()
    w = {n: args[n] for n in _WEIGHTS}
    m = {n: args["m_" + n] for n in _WEIGHTS}
    v = {n: args["v_" + n] for n in _WEIGHTS}

    me = jnp.reshape(_lin(_mesh_pos()), (1,)).astype(jnp.int32)

    order = [(layer, p) for layer in range(DEPTH) for p in range(len(_gather_parts(layer)))]
    lands, gathers = {}, {}
    for key in order:
        lands[key] = []
        for name, i in _gather_parts(key[0])[key[1]]:
            b2 = _rows2d(w[name])
            r = b2.shape[0] // w[name].shape[0]
            lands[key].append(_into_slot("cast_" + name, b2, i * r, r, me, BF16, None))
    small_blocks = jnp.concatenate([_small_rows(w[n]) for n in _SMALL_SPLIT], axis=0)
    lands[order[0]].append(_into_slot("cast_small", small_blocks, 0, 8, me, F32, None))

    def begin_gather(key, after):
        gathers[key] = _exchange_start("gather%d%s_start" % (key[0], "ab"[key[1]]), True, [], lands[key], after)
        return gathers[key]["token"]

    W = {n: w[n] for n in _REPLICATED}
    W["mem_norm"] = w["mem_norm"] + begin_gather(order[0], None)
    for name in _BIG:
        W[name] = [None] * w[name].shape[0]

    def weights_of(layer, part, after):
        key = (layer, part)
        if key not in gathers:
            return 0.0
        _, got = _exchange_wait("gather%d%s_wait" % (layer, "ab"[part]), gathers[key], after)
        for (name, i), arr in zip(_gather_parts(layer)[part], got):
            W[name][i] = _from_slots(name, arr)
        if key == order[0]:
            sm = got[-1].reshape(N_DEV, 4, 2, 128)
            for j, n in enumerate(_SMALL_SPLIT):
                width = w[n].shape[1]
                W[n] = sm[:, j, :, :width].transpose(1, 0, 2).reshape(2, N_DEV * width)
        nxt = order.index(key) + 1
        return begin_gather(order[nxt], got[0]) if nxt < len(order) else 0.0

    scatters, small = {}, {}

    def send_grads(layer, G, part):
        srcs = [_to_slots(name, G[name][i]) for name, i in _scatter_parts(layer)[part]]
        scatters[layer, part] = _scatter_begin("scatter%d%s_start" % (layer, "ab"[part]), srcs)
        return scatters[layer, part]["token"]

    def begin_small(tag, G, names, split):
        srcs = [_pack([G[n] if n in ("mem_norm", "final_norm") else jnp.stack(G[n]) for n in names])]
        if split:
            gs = jnp.stack([jnp.stack(G[n]).reshape(2, N_DEV, -1).transpose(1, 0, 2) for n in _SMALL_SPLIT[:3]]
                           + [jnp.pad(jnp.stack(G["s5_d"]).reshape(2, N_DEV, -1).transpose(1, 0, 2), ((0, 0), (0, 0), (0, 64)))], axis=1)
            srcs.append(gs.reshape(N_DEV, 8, 128))
        small[tag] = _scatter_begin("scatter_small_%s_start" % tag, srcs)
        return small[tag]["token"]

    def after_layer(layer, G):
        return begin_small("odd", G, _REPLICATED_ODD, True) if layer == 1 else 0.0

    loss, dx, G = _local_step(x[0], mem[0], loss_target[0], W, weights_of, send_grads, after_layer)
    loss = lax.psum(loss[0, 0], MESH_AXES)
    begin_small("even", G, _REPLICATED_EVEN, False)

    out = {}
    after = small["even"]["token_arr"]
    for layer, part in scatters:
        own, got = _exchange_wait("scatter%d%s_wait" % (layer, "ab"[part]), scatters[layer, part], after)
        for (name, i), mine, parts in zip(_scatter_parts(layer)[part], own, got):
            out[name] = _adam("adam_" + name, _rows2d(w[name]), _rows2d(m[name]), _rows2d(v[name]), parts, mine, me, i, out.get(name))
        after = [out[name][0] for name, i in _scatter_parts(layer)[part]]
    for name in _BIG:
        out[name] = [a.reshape(w[name].shape) for a in out[name]]
    own, got = _exchange_wait("scatter_small_odd_wait", small["odd"], after)
    pk = lambda d: jnp.concatenate([_small_rows(d[n]) for n in _SMALL_SPLIT], axis=0)
    res = _adam("adam_small", pk(w), pk(m), pk(v), got[1], own[1], me, 0, None)
    for j, n in enumerate(_SMALL_SPLIT):
        out[n] = [a[2 * j:2 * j + 2, :w[n].shape[1]] for a in res]
    own_even, got_even = _exchange_wait("scatter_small_even_wait", small["even"], res[0])
    for names, mine, parts in zip((_REPLICATED_ODD, _REPLICATED_EVEN), (own[0], own_even[0]), (got[0], got_even[0])):
        pk = lambda d: _pack([d[n] for n in names])
        res = _adam("adam_replicated", pk(w), pk(m), pk(v), parts, mine, me, 0, None)
        shapes = [w[n].shape for n in names]
        un = [_unpack(a, shapes) for a in res]
        for j, n in enumerate(names):
            out[n] = [un[q][j] for q in range(4)]

    return (loss, dx[None], *[out[n][0] for n in _WEIGHTS], *[out[n][1] for n in _WEIGHTS], *[out[n][2] for n in _WEIGHTS],
            *[out[n][3] for n in _WEIGHTS])
```

```python
import functools
import math

import jax
import jax.numpy as jnp
from jax import lax
from jax.experimental import pallas as pl
from jax.experimental.pallas import tpu as pltpu

F32 = jnp.float32
BF16 = jnp.bfloat16

SEQ = 2048
D_MODEL = 1024
MEM_LEN = 256
DEPTH = 4
N_DEV = 8
EPS = 1e-6
NEG = -1e30
A_HEAD_DIM = 64
X_HEAD_DIM = 256
S5_GROUPS = 32
S5_STATE = 64
S5_GROUP_DIM = 16

ADAM_LR = 0.001
ADAM_B1 = 0.9
ADAM_B2 = 0.999
ADAM_EPS = 1e-08
ADAM_WD = 0.01
ADAM_STEP = 10

V7X_VMEM_LIMIT_BYTES = 56 * 1024 * 1024
_MM_VMEM_BYTES = 36 * 1024 * 1024
_MM_MIN_STEPS = 8
MESH_AXES = ("x", "y", "c")


def _pc(body, *, name, out_shape, grid=None, in_specs=None, out_specs=None, scratch_shapes=(), aliases=None, sem=None):
    kw = {}
    if grid is not None:
        kw["grid"] = grid
    if in_specs is not None:
        kw["in_specs"] = in_specs
    if out_specs is not None:
        kw["out_specs"] = out_specs
    if aliases:
        kw["input_output_aliases"] = aliases
    return pl.pallas_call(
        body,
        name=name,
        out_shape=out_shape,
        scratch_shapes=list(scratch_shapes),
        compiler_params=pltpu.CompilerParams(dimension_semantics=sem, vmem_limit_bytes=V7X_VMEM_LIMIT_BYTES),
        interpret=False,
        **kw,
    )


def _cols(arr, c0=0, width=None, nsplit=1, r0=0):
    width = arr.shape[1] - c0 if width is None else width
    assert c0 % width == 0 and width % nsplit == 0
    return (arr, c0, width, nsplit, r0)


def _par(arr, nsplit=1):
    return (arr, nsplit)


def _ld(ref, nsplit):
    if nsplit == 1:
        return ref[...].astype(F32)
    if len(ref.shape) == 3:
        return tuple(ref[k].astype(F32) for k in range(nsplit))
    w = ref.shape[-1] // nsplit
    return tuple(ref[:, k * w:(k + 1) * w].astype(F32) for k in range(nsplit))


def _st(ref, val, nsplit, accumulate=False):
    if nsplit == 1:
        val = (val,)
    for k in range(nsplit):
        if nsplit == 1:
            idx = (Ellipsis,)
        elif len(ref.shape) == 3:
            idx = (k,)
        else:
            w = ref.shape[-1] // nsplit
            idx = (slice(None), slice(k * w, (k + 1) * w))
        if accumulate:
            ref[idx] += val[k].astype(ref.dtype)
        else:
            ref[idx] = val[k].astype(ref.dtype)


def _row_spec(tr, op):
    _, c0, w, _, r0 = op
    assert r0 % tr == 0
    return pl.BlockSpec((tr, w), lambda i, cb=c0 // w, rb=r0 // tr: (i + rb, cb))


def _full_spec(arr):
    return pl.BlockSpec(arr.shape, lambda i, nd=arr.ndim: (0,) * nd)


def _rw_fwd(name, f, rows, pars, outs, tr, n_rows=None):
    n_rows = rows[0][0].shape[0] if n_rows is None else n_rows
    nr, npar = len(rows), len(pars)

    def body(*refs):
        r = [_ld(refs[i], rows[i][3]) for i in range(nr)]
        p = [_ld(refs[nr + i], pars[i][1]) for i in range(npar)]
        res = f(r, p)
        for k, (_, _, ns) in enumerate(outs):
            _st(refs[nr + npar + k], res[k], ns)

    res = _pc(
        body, name=name, grid=(n_rows // tr,),
        in_specs=[_row_spec(tr, op) for op in rows] + [_full_spec(a) for a, _ in pars],
        out_specs=[pl.BlockSpec((tr, w), lambda i: (i, 0)) for w, _, _ in outs],
        out_shape=[jax.ShapeDtypeStruct((n_rows, w), dt) for w, dt, _ in outs],
        sem=("arbitrary",),
    )(*[op[0] for op in rows], *[a for a, _ in pars])
    return list(res)


def _rw_bwd(name, f, rows, pars, douts, drow, dpar, tr):
    n_rows = rows[0][0].shape[0]
    nr, npar = len(rows), len(pars)
    dgiven = [d for d in douts if d is not None]
    nd = len(dgiven)

    def body(*refs):
        r = [_ld(refs[i], rows[i][3]) for i in range(nr)]
        p = [_ld(refs[nr + i], pars[i][1]) for i in range(npar)]
        d = [_ld(refs[nr + npar + i], dgiven[i][3]) for i in range(nd)]
        orefs = refs[nr + npar + nd:]

        def g(dr, dp):
            rr, pp = list(r), list(p)
            for j, (idx, _) in enumerate(drow):
                rr[idx] = dr[j]
            for j, idx in enumerate(dpar):
                pp[idx] = dp[j]
            return tuple(f(rr, pp))

        out, vjp = jax.vjp(g, [r[idx] for idx, _ in drow], [p[idx] for idx in dpar])
        ct, j = [], 0
        for k, o in enumerate(out):
            if douts[k] is None:
                ct.append(jax.tree.map(jnp.zeros_like, o))
            else:
                ct.append(d[j])
                j += 1
        gdr, gdp = vjp(tuple(ct))
        for j, (idx, _) in enumerate(drow):
            _st(orefs[j], gdr[j], rows[idx][3])

        @pl.when(pl.program_id(0) == 0)
        def _():
            for j in range(len(dpar)):
                orefs[len(drow) + j][...] = jnp.zeros_like(orefs[len(drow) + j])

        for j, idx in enumerate(dpar):
            _st(orefs[len(drow) + j], gdp[j], pars[idx][1], accumulate=True)

    res = _pc(
        body, name=name, grid=(n_rows // tr,),
        in_specs=[_row_spec(tr, op) for op in rows] + [_full_spec(a) for a, _ in pars] + [_row_spec(tr, op) for op in dgiven],
        out_specs=[pl.BlockSpec((tr, rows[idx][2]), lambda i: (i, 0)) for idx, _ in drow] + [_full_spec(pars[idx][0]) for idx in dpar],
        out_shape=[jax.ShapeDtypeStruct((n_rows, rows[idx][2]), dt) for idx, dt in drow]
        + [jax.ShapeDtypeStruct(pars[idx][0].shape, F32) for idx in dpar],
        sem=("arbitrary",),
    )(*[op[0] for op in rows], *[a for a, _ in pars], *[op[0] for op in dgiven])
    res = list(res)
    return res[:len(drow)], res[len(drow):]


def _sigmoid(x):
    return jax.nn.sigmoid(x)


def _silu(x):
    return x * _sigmoid(x)


def _rms(x, g):
    return x * lax.rsqrt(jnp.mean(x * x, axis=-1, keepdims=True) + EPS) * g


def _f_rms(r, p):
    return [_rms(r[0], p[0])]


def _f_rms_res(r, p):
    return [r[0], _rms(r[0], p[0])]


def _f_gate_ab(r, p):
    o, ga, mixed, gb = r
    return [(o * _silu(ga), mixed * p[0] * _silu(gb))]


def _f_sgu(r, p):
    u, v, gc = r
    lg, lb, w, b = p
    n = float(D_MODEL)
    mu = sum(jnp.sum(vk, axis=-1, keepdims=True) for vk in v) / n
    var = sum(jnp.sum(jnp.square(vk - mu), axis=-1, keepdims=True) for vk in v) / n
    rs = lax.rsqrt(var + EPS)
    t = w[0].shape[0]
    tri = lax.broadcasted_iota(jnp.int32, (t, t), 0) >= lax.broadcasted_iota(jnp.int32, (t, t), 1)
    outs = []
    for k in range(len(v)):
        vn = (v[k] - mu) * rs * lg[k] + lb[k]
        mixed = jnp.dot(jnp.where(tri, w[k], 0.0), vn, preferred_element_type=F32) + b[k]
        outs.append(u[k] * mixed * _silu(gc[k]))
    return [tuple(outs)]


def _gelu(x):
    return 0.5 * x * (1.0 + jnp.tanh(math.sqrt(2.0 / math.pi) * (x + 0.044715 * (x * x * x))))


def _f_gelu_y(r, p):
    yc, uf = r
    return [_gelu(yc + p[0] * uf)]


def _f_glu_gate(r, p):
    t12, gd = r
    return [t12[0] * _sigmoid(t12[1]) * _silu(gd)]


def _f_s5_prep(r, p):
    ar, ai, ldt = r
    dt = jnp.exp(ldt)
    mag = jnp.exp(dt * ar)
    abar_re = mag * jnp.cos(dt * ai)
    abar_im = mag * jnp.sin(dt * ai)
    nr, ni = abar_re - 1.0, abar_im
    inv = 1.0 / (ar * ar + ai * ai)
    return [abar_re, abar_im, (nr * ar + ni * ai) * inv, (ni * ar - nr * ai) * inv]


def _f_bbar(r, p):
    br, bi, cr, ci = r
    return [cr * br - ci * bi, cr * bi + ci * br]


def _loss_head(x, target, g):
    tr = 256
    n_rows, width = x.shape

    def f(xv, gv, tv):
        err = jnp.square(_rms(xv, gv) - tv)
        return 0.5 * jnp.mean(err, axis=-1, keepdims=True)

    def body(x_ref, t_ref, g_ref, loss_ref, dx_ref, dg_ref):
        @pl.when(pl.program_id(0) == 0)
        def _():
            loss_ref[...] = jnp.zeros_like(loss_ref)
            dg_ref[...] = jnp.zeros_like(dg_ref)

        tv = t_ref[...]
        row_loss, vjp = jax.vjp(lambda a, b: f(a, b, tv), x_ref[...], g_ref[...])
        dx, dg = vjp(jnp.ones_like(row_loss))
        dx_ref[...] = dx
        dg_ref[...] += dg
        loss_ref[...] += jnp.broadcast_to(jnp.sum(row_loss, axis=0, keepdims=True), loss_ref.shape)

    blk = pl.BlockSpec((tr, width), lambda i: (i, 0))
    one = pl.BlockSpec((1, width), lambda i: (0, 0))
    return _pc(
        body, name="loss_head", grid=(n_rows // tr,), in_specs=[blk, blk, one],
        out_specs=[pl.BlockSpec((1, 128), lambda i: (0, 0)), blk, one],
        out_shape=[jax.ShapeDtypeStruct((1, 128), F32), jax.ShapeDtypeStruct(x.shape, F32), jax.ShapeDtypeStruct((1, width), F32)],
        sem=("arbitrary",),
    )(x, target, g)


_NT = (((1,), (1,)), ((), ()))
_TN = (((0,), (0,)), ((), ()))


def _tile(n, cap):
    t = min(n, cap)
    while n % t:
        t -= 128
    assert t > 0
    return t


def _mm(name, a, b, *, ta=False, tb=False, out_dtype=F32, a_off=0, a_width=None, res=None, out_stack=None):
    assert not (ta and tb)
    stacked = b.ndim == 3
    bk, bn = (b.shape[1], b.shape[0] * b.shape[2]) if stacked else b.shape
    if ta:
        kc = a.shape[0]
        m = a.shape[1] - a_off if a_width is None else a_width
        n = bn
        assert bk == kc and not stacked
    else:
        m = a.shape[0]
        kc = a.shape[1] - a_off if a_width is None else a_width
        n = bk if tb else bn
        assert (bn if tb else bk) == kc
    tn = _tile(b.shape[2] if stacked and not tb else (out_stack or n), 1024)
    size = lambda dt: jnp.dtype(dt).itemsize
    best = None
    for tk_cap in (2048, 1024, 512):
        for tm_cap in (2048, 1024, 512, 256):
            tm, tk = _tile(m, tm_cap), _tile(b.shape[2] if stacked and tb else kc, tk_cap)
            nk = kc // tk
            vmem = 2 * tm * tk * size(a.dtype) + 2 * tk * tn * size(b.dtype) + tm * tn * (2 * size(out_dtype) + (4 if nk > 1 else 0) + (8 if res is not None else 0))
            key = (vmem <= _MM_VMEM_BYTES, min((m // tm) * (n // tn) * nk, _MM_MIN_STEPS), tk, tm)
            if best is None or key > best[0]:
                best = (key, tm, tk, nk)
    _, tm, tk, nk = best
    if ta:
        assert a_off % tm == 0
        a_spec = pl.BlockSpec((tk, tm), lambda i, j, k, o=a_off // tm: (k, i + o))
        dims = _TN
    else:
        assert a_off % tk == 0
        a_spec = pl.BlockSpec((tm, tk), lambda i, j, k, o=a_off // tk: (i, k + o))
        dims = _NT if tb else (((1,), (0,)), ((), ()))
    if stacked and tb:
        b_spec = pl.BlockSpec((None, tn, tk), lambda i, j, k, q=b.shape[2] // tk: (k // q, j, k % q))
    elif stacked:
        b_spec = pl.BlockSpec((None, tk, tn), lambda i, j, k, q=b.shape[2] // tn: (j // q, k, j % q))
    else:
        b_spec = pl.BlockSpec((tn, tk), lambda i, j, k: (j, k)) if tb else pl.BlockSpec((tk, tn), lambda i, j, k: (k, j))
    if out_stack:
        out_spec = pl.BlockSpec((None, tm, tn), lambda i, j, k, q=out_stack // tn: (j // q, i, j % q))
        out_shape = jax.ShapeDtypeStruct((n // out_stack, m, out_stack), out_dtype)
    else:
        out_spec = pl.BlockSpec((tm, tn), lambda i, j, k: (i, j))
        out_shape = jax.ShapeDtypeStruct((m, n), out_dtype)
    in_specs, args = [a_spec, b_spec], [a, b]
    has_res = res is not None
    if has_res:
        in_specs.append(pl.BlockSpec((tm, tn), lambda i, j, k: (i, j)))
        args.append(res)

    def finish(refs, acc):
        if has_res:
            acc = acc + refs[2][...].astype(F32)
        refs[3 if has_res else 2][...] = acc.astype(out_dtype)

    def body_one(*refs):
        finish(refs, lax.dot_general(refs[0][...].astype(BF16), refs[1][...].astype(BF16), dims, preferred_element_type=F32))

    def body(*refs):
        acc_ref = refs[-1]
        k = pl.program_id(2)

        @pl.when(k == 0)
        def _():
            acc_ref[...] = jnp.zeros_like(acc_ref)

        acc_ref[...] += lax.dot_general(refs[0][...].astype(BF16), refs[1][...].astype(BF16), dims, preferred_element_type=F32)

        @pl.when(k == nk - 1)
        def _():
            finish(refs, acc_ref[...])

    return _pc(
        body_one if nk == 1 else body, name=name, grid=(m // tm, n // tn, nk), in_specs=in_specs, out_specs=out_spec, out_shape=out_shape,
        scratch_shapes=[] if nk == 1 else [pltpu.VMEM((tm, tn), F32)], sem=("parallel", "parallel", "arbitrary"),
    )(*args)


def _mm_blocks(name, a, b, *, grid, out_blk, a_blk, a_idx, b_blk, b_idx, dims, out_dtype=F32):
    gi, gj, nk = grid

    def body(a_ref, b_ref, o_ref, acc_ref):
        k = pl.program_id(2)

        @pl.when(k == 0)
        def _():
            acc_ref[...] = jnp.zeros_like(acc_ref)

        acc_ref[...] += lax.dot_general(a_ref[...].astype(BF16), b_ref[...].astype(BF16), dims, preferred_element_type=F32)

        @pl.when(k == nk - 1)
        def _():
            o_ref[...] = acc_ref[...].astype(o_ref.dtype)

    return _pc(
        body, name=name, grid=grid, in_specs=[pl.BlockSpec(a_blk, a_idx), pl.BlockSpec(b_blk, b_idx)],
        out_specs=pl.BlockSpec(out_blk, lambda i, j, k: (i, j)), out_shape=jax.ShapeDtypeStruct((gi * out_blk[0], gj * out_blk[1]), out_dtype),
        scratch_shapes=[pltpu.VMEM(out_blk, F32)], sem=("parallel", "parallel", "arbitrary"),
    )(a, b)


def _head_masks(width, nsub):
    lane = lax.broadcasted_iota(jnp.int32, (1, width), 1)
    hd = width // nsub
    return [(lane >= h * hd) & (lane < (h + 1) * hd) for h in range(nsub)]


def _dilated_log_count(row0, tq, ext):
    delta = (row0 + lax.broadcasted_iota(jnp.int32, (tq, ext), 0)) - lax.broadcasted_iota(jnp.int32, (tq, ext), 1)
    cnt = (delta <= 128).astype(jnp.int32) + (((delta & 3) == 0) & (delta <= 512)).astype(jnp.int32) + ((delta & 15) == 0).astype(jnp.int32)
    logc = jnp.where(cnt == 3, math.log(3.0), jnp.where(cnt == 2, math.log(2.0), 0.0))
    return jnp.where((delta >= 0) & (cnt > 0), logc, NEG)


def _bias_table(tab, nq, tq):
    @pl.when(pl.program_id(0) == 0)
    def _():
        for d in range(nq):
            tab[d] = _dilated_log_count(d * tq, tq, tq)


def _scores(q, ke, tab, r, masks, h):
    qm = (jnp.where(masks[h], q, 0.0) if len(masks) > 1 else q).astype(BF16)
    s = lax.dot_general(qm, ke, _NT, preferred_element_type=F32)
    if tab is not None:
        s = s + jnp.concatenate([tab[r - c] for c in range(r + 1)], axis=1)
    p = jnp.exp(s - jnp.max(s, axis=-1, keepdims=True))
    return qm, p, 1.0 / jnp.sum(p, axis=-1, keepdims=True)


def _attn_fwd(name, qa, ka, va, *, qc, kc, vc, width, nblk, nsub, causal, tq, scale, out_dtype):
    sq, t_len = qa.shape[0], ka.shape[0]
    nq = sq // tq

    def body(q_ref, k_ref, v_ref, o_ref, *scratch):
        tab = scratch[0] if causal else None
        if causal:
            _bias_table(tab, nq, tq)
        kb = k_ref[...].astype(BF16)
        vb = v_ref[...].astype(BF16)
        masks = _head_masks(width, nsub)
        for r in range(nq):
            ext = (r + 1) * tq if causal else t_len
            q = q_ref[r * tq:(r + 1) * tq, :].astype(F32) * scale
            ke, ve = kb[:ext], vb[:ext]
            o = None
            for h in range(nsub):
                _, p, inv = _scores(q, ke, tab, r, masks, h)
                oh = jnp.dot(p.astype(BF16), ve, preferred_element_type=F32) * inv
                o = oh if o is None else jnp.where(masks[h], oh, o)
            o_ref[r * tq:(r + 1) * tq, :] = o.astype(o_ref.dtype)

    return _pc(
        body, name=name, grid=(nblk,),
        in_specs=[pl.BlockSpec((sq, width), lambda i, c=qc: (0, c + i)), pl.BlockSpec((t_len, width), lambda i, c=kc: (0, c + i)),
                  pl.BlockSpec((t_len, width), lambda i, c=vc: (0, c + i))],
        out_specs=pl.BlockSpec((sq, width), lambda i: (0, i)),
        out_shape=jax.ShapeDtypeStruct((sq, nblk * width), out_dtype),
        scratch_shapes=[pltpu.VMEM((nq, tq, tq), F32)] if causal else [], sem=("arbitrary",),
    )(qa, ka, va)


def _attn_bwd(name, qa, ka, va, doa, *, qc, kc, vc, width, nblk, nsub, causal, tq, scale, out_dtype):
    sq, t_len = qa.shape[0], ka.shape[0]

    nq = sq // tq

    def body(q_ref, k_ref, v_ref, do_ref, dq_ref, dk_ref, dv_ref, dk_acc, dv_acc, *scratch):
        tab = scratch[0] if causal else None
        if causal:
            _bias_table(tab, nq, tq)
        kb = k_ref[...].astype(BF16)
        vb = v_ref[...].astype(BF16)
        masks = _head_masks(width, nsub)
        dk_acc[...] = jnp.zeros_like(dk_acc)
        dv_acc[...] = jnp.zeros_like(dv_acc)
        for r in range(nq):
            ext = (r + 1) * tq if causal else t_len
            q = q_ref[r * tq:(r + 1) * tq, :].astype(F32) * scale
            do = do_ref[r * tq:(r + 1) * tq, :].astype(F32)
            ke, ve = kb[:ext], vb[:ext]
            dq = None
            for h in range(nsub):
                qm, p, inv = _scores(q, ke, tab, r, masks, h)
                dom = (jnp.where(masks[h], do, 0.0) if nsub > 1 else do).astype(BF16)
                pn = p * inv
                dpn = lax.dot_general(dom, ve, _NT, preferred_element_type=F32)
                dsb = (pn * (dpn - jnp.sum(pn * dpn, axis=-1, keepdims=True))).astype(BF16)
                dqh = jnp.dot(dsb, ke, preferred_element_type=F32)
                dq = dqh if dq is None else jnp.where(masks[h], dqh, dq)
                dk_acc[0:ext, :] += lax.dot_general(dsb, qm, _TN, preferred_element_type=F32)
                dv_acc[0:ext, :] += lax.dot_general(pn.astype(BF16), dom, _TN, preferred_element_type=F32)
            dq_ref[r * tq:(r + 1) * tq, :] = (dq * scale).astype(dq_ref.dtype)
        dk_ref[...] = dk_acc[...].astype(dk_ref.dtype)
        dv_ref[...] = dv_acc[...].astype(dv_ref.dtype)

    return _pc(
        body, name=name, grid=(nblk,),
        in_specs=[pl.BlockSpec((sq, width), lambda i, c=qc: (0, c + i)), pl.BlockSpec((t_len, width), lambda i, c=kc: (0, c + i)),
                  pl.BlockSpec((t_len, width), lambda i, c=vc: (0, c + i)), pl.BlockSpec((sq, width), lambda i: (0, i))],
        out_specs=[pl.BlockSpec((sq, width), lambda i: (0, i)), pl.BlockSpec((t_len, width), lambda i: (0, i)), pl.BlockSpec((t_len, width), lambda i: (0, i))],
        out_shape=[jax.ShapeDtypeStruct((sq, nblk * width), out_dtype), jax.ShapeDtypeStruct((t_len, nblk * width), out_dtype),
                   jax.ShapeDtypeStruct((t_len, nblk * width), out_dtype)],
        scratch_shapes=[pltpu.VMEM((t_len, width), F32), pltpu.VMEM((t_len, width), F32)] + ([pltpu.VMEM((nq, tq, tq), F32)] if causal else []),
        sem=("arbitrary",),
    )(qa, ka, va, doa)


_SELF = dict(qc=0, kc=8, vc=16, width=128, nblk=8, nsub=2, causal=True, tq=256, scale=A_HEAD_DIM ** -0.5)
_CROSS = dict(qc=0, kc=0, vc=4, width=256, nblk=4, nsub=1, causal=False, tq=512, scale=X_HEAD_DIM ** -0.5)


def _window_sum(x, g, row, backward):
    n = x.shape[0]

    def shift(y, k):
        if backward:
            return jnp.where(row < n - k, pltpu.roll(y, n - k, 0), 0.0)
        return jnp.where(row >= k, pltpu.roll(y, k, 0), 0.0)

    s2 = x + shift(x, 1)
    s4 = s2 + shift(s2, 2)
    s8 = s4 + shift(s4, 4)
    s16 = s8 + shift(s8, 8)
    return jnp.where(g == 0, s2, jnp.where(g == 1, s4, jnp.where(g == 2, s8, s16)))


def _pool(name, arr, c0, backward, out_dtype):
    n = arr.shape[0]
    gw = 256

    def body(v_ref, o_ref):
        g = pl.program_id(0)
        v = v_ref[...].astype(F32)
        row = lax.broadcasted_iota(jnp.int32, v.shape, 0)
        w = jnp.where(g == 0, 2, jnp.where(g == 1, 4, jnp.where(g == 2, 8, 16)))
        cnt = jnp.minimum(row + 1, w).astype(F32)
        if backward:
            o_ref[...] = (_window_sum(v / cnt, g, row, True) - v).astype(o_ref.dtype)
        else:
            o_ref[...] = (_window_sum(v, g, row, False) / cnt - v).astype(o_ref.dtype)

    return _pc(
        body, name=name, grid=(4,), in_specs=[pl.BlockSpec((n, gw), lambda i, c=c0 // gw: (0, c + i))],
        out_specs=pl.BlockSpec((n, gw), lambda i: (0, i)), out_shape=jax.ShapeDtypeStruct((n, 4 * gw), out_dtype), sem=("parallel",),
    )(arr)


_SCAN_ROWS = 256


def _scan_fwd(bu3, a2):
    n = bu3.shape[0]

    def body(bu_ref, a_ref, h_ref, carry):
        @pl.when(pl.program_id(0) == 0)
        def _():
            carry[...] = jnp.zeros_like(carry)

        ar, ai = a_ref[0:16, :], a_ref[16:32, :]

        def step(t, c):
            hr, hi = c
            nr = ar * hr - ai * hi + bu_ref[t, 0:16, :]
            ni = ar * hi + ai * hr + bu_ref[t, 16:32, :]
            h_ref[t, 0:16, :] = nr
            h_ref[t, 16:32, :] = ni
            return nr, ni

        hr, hi = lax.fori_loop(0, _SCAN_ROWS, step, (carry[0:16, :], carry[16:32, :]), unroll=8)
        carry[0:16, :] = hr
        carry[16:32, :] = hi

    blk = pl.BlockSpec((_SCAN_ROWS, 32, 128), lambda i: (i, 0, 0))
    return _pc(
        body, name="s5_scan_fwd", grid=(n // _SCAN_ROWS,), in_specs=[blk, pl.BlockSpec((32, 128), lambda i: (0, 0))], out_specs=blk,
        out_shape=jax.ShapeDtypeStruct(bu3.shape, F32), scratch_shapes=[pltpu.VMEM((32, 128), F32)], sem=("arbitrary",),
    )(bu3, a2)


def _scan_bwd(dh3, h3, a2):
    n = dh3.shape[0]
    nb = n // _SCAN_ROWS

    def body(dh_ref, h_ref, a_ref, dbu_ref, da_ref, carry):
        @pl.when(pl.program_id(0) == 0)
        def _():
            carry[...] = jnp.zeros_like(carry)
            da_ref[...] = jnp.zeros_like(da_ref)

        ar, ai = a_ref[0:16, :], a_ref[16:32, :]

        def step(tt, c):
            gr, gi, dar, dai = c
            t = _SCAN_ROWS - 1 - tt
            hr, hi = h_ref[t, 0:16, :], h_ref[t, 16:32, :]
            dar = dar + gr * hr + gi * hi
            dai = dai - gr * hi + gi * hr
            ngr = dh_ref[t, 0:16, :] + ar * gr + ai * gi
            ngi = dh_ref[t, 16:32, :] - ai * gr + ar * gi
            dbu_ref[t, 0:16, :] = ngr
            dbu_ref[t, 16:32, :] = ngi
            return ngr, ngi, dar, dai

        z = jnp.zeros((16, 128), F32)
        gr, gi, dar, dai = lax.fori_loop(0, _SCAN_ROWS, step, (carry[0:16, :], carry[16:32, :], z, z), unroll=8)
        carry[0:16, :] = gr
        carry[16:32, :] = gi
        da_ref[0:16, :] += dar
        da_ref[16:32, :] += dai

    blk = pl.BlockSpec((_SCAN_ROWS, 32, 128), lambda i: (nb - 1 - i, 0, 0))
    small = pl.BlockSpec((32, 128), lambda i: (0, 0))
    return _pc(
        body, name="s5_scan_bwd", grid=(nb,), in_specs=[blk, blk, small], out_specs=[blk, small],
        out_shape=[jax.ShapeDtypeStruct(dh3.shape, F32), jax.ShapeDtypeStruct((32, 128), F32)],
        scratch_shapes=[pltpu.VMEM((32, 128), F32)], sem=("arbitrary",),
    )(dh3, h3, a2)


def _bdense(bb_re, bb_im):
    eye = jnp.eye(8, dtype=F32)

    def one(bb):
        return jnp.einsum("sgph,gk->sghkp", bb.reshape(4, 8, S5_STATE, S5_GROUP_DIM), eye).reshape(512, 512)

    return jnp.concatenate([one(bb_re), one(bb_im)], axis=1)


def _cdense(c_re, c_im):
    eye = jnp.eye(8, dtype=F32)

    def one(cc):
        return jnp.einsum("sghp,gk->sgpkh", cc.reshape(4, 8, S5_GROUP_DIM, S5_STATE), eye).reshape(2048, 128)

    return jnp.concatenate([one(c_re), -one(c_im)], axis=0)


_NN = (((1,), (0,)), ((), ()))
_UF_BLOCK = 3072 // 128


def _s5_bu(z, bd):
    return _mm_blocks("mm_s5_bu", z, bd, grid=(1, 8, 1), out_blk=(SEQ, 512), a_blk=(SEQ, 128), a_idx=lambda i, j, k: (0, _UF_BLOCK + j % 4),
                      b_blk=(128, 512), b_idx=lambda i, j, k: (j % 4, j // 4), dims=_NN)


def _s5_bu_dx(dbu, bd):
    return _mm_blocks("mm_s5_bu_dx", dbu, bd, grid=(1, 4, 2), out_blk=(SEQ, 128), a_blk=(SEQ, 512), a_idx=lambda i, j, k: (0, 4 * k + j),
                      b_blk=(128, 512), b_idx=lambda i, j, k: (j, k), dims=_NT)


def _s5_bu_dw(z, dbu):
    return _mm_blocks("mm_s5_bu_dw", z, dbu, grid=(4, 2, 2), out_blk=(128, 512), a_blk=(1024, 128), a_idx=lambda i, j, k: (k, _UF_BLOCK + i),
                      b_blk=(1024, 512), b_idx=lambda i, j, k: (k, 4 * j + i), dims=_TN)


def _s5_y(h2, cf):
    return _mm_blocks("mm_s5_y", h2, cf, grid=(1, 4, 2), out_blk=(SEQ, 128), a_blk=(SEQ, 512), a_idx=lambda i, j, k: (0, 4 * k + j),
                      b_blk=(512, 128), b_idx=lambda i, j, k: (4 * k + j, 0), dims=_NN)


def _s5_y_dx(dyc, cf):
    return _mm_blocks("mm_s5_y_dx", dyc, cf, grid=(1, 8, 1), out_blk=(SEQ, 512), a_blk=(SEQ, 128), a_idx=lambda i, j, k: (0, j % 4),
                      b_blk=(512, 128), b_idx=lambda i, j, k: (j, 0), dims=_NT)


def _s5_y_dw(h2, dyc):
    return _mm_blocks("mm_s5_y_dw", h2, dyc, grid=(8, 1, 2), out_blk=(512, 128), a_blk=(1024, 512), a_idx=lambda i, j, k: (k, i),
                      b_blk=(1024, 128), b_idx=lambda i, j, k: (k, i % 4), dims=_TN)


def _pool_dense(pw):
    eye = jnp.eye(4, dtype=pw.dtype)
    return jnp.einsum("gcd,gk->gckd", pw, eye).reshape(1024, 1024)


def _row2(v):
    return v.reshape(1, -1)


def _even_fwd(x, W, i, zero, rest_of_weights):
    hn = _rw_fwd("rms_fwd", _f_rms, [_cols(x)], [_par(_row2(W["norm_ab"][i]) + zero)], [(D_MODEL, BF16, 1)], 256)[0]
    z = _mm("mm_in_ab", hn, W["w_in_ab"][i])
    o = _attn_fwd("attn_self_fwd", z, z, z, out_dtype=F32, **_SELF)
    zero = rest_of_weights(o)
    pooled = _pool("pool_fwd", z, 4096, False, BF16)
    wp = _pool_dense(W["pool_w"][i])
    mixed = _mm("mm_pool", pooled, wp)
    scale = _row2(W["pool_scale"][i]) + zero
    ab = _rw_fwd("gate_ab_fwd", _f_gate_ab, [_cols(o), _cols(z, 3072, 1024), _cols(mixed), _cols(z, 5120, 1024)], [_par(scale)],
                 [(2048, BF16, 2)], 256)[0]
    x1 = _mm("mm_out_ab", ab, W["w_out_ab"][i], res=x)
    return x1, dict(x=x, hn=hn, z=z, o=o, pooled=pooled, wp=wp, mixed=mixed, ab=ab), 0.0


def _even_bwd(dx1, sv, W, G, i, send):
    x, hn, z = sv["x"], sv["hn"], sv["z"]
    dab = _mm("mm_out_ab_dx", dx1, W["w_out_ab"][i], tb=True)
    G["w_out_ab"][i] = _mm("mm_out_ab_dw", sv["ab"], dx1, ta=True, out_dtype=BF16)
    scale = _row2(W["pool_scale"][i]) + send(0)
    (do, dga, dmixed, dgb), (dscale,) = _rw_bwd(
        "gate_ab_bwd", _f_gate_ab, [_cols(sv["o"]), _cols(z, 3072, 1024), _cols(sv["mixed"]), _cols(z, 5120, 1024)], [_par(scale)],
        [_cols(dab, nsplit=2)], [(0, F32), (1, BF16), (2, BF16), (3, BF16)], [0], 256)
    G["pool_scale"][i] = dscale.reshape(-1)
    dpooled = _mm("mm_pool_dx", dmixed, sv["wp"], tb=True)
    dwp = _mm("mm_pool_dw", sv["pooled"], dmixed, ta=True, out_dtype=BF16)
    G["pool_w"][i] = jnp.stack([dwp[g * 256:(g + 1) * 256, g * 256:(g + 1) * 256] for g in range(4)])
    dvb = _pool("pool_bwd", dpooled, 0, True, BF16)
    dq, dk, dv = _attn_bwd("attn_self_bwd", z, z, z, do, out_dtype=BF16, **_SELF)
    dz = jnp.concatenate([dq, dk, dv, dga, dvb, dgb], axis=1)
    dhn = _mm("mm_in_ab_dx", dz, W["w_in_ab"][i], tb=True)
    G["w_in_ab"][i] = _mm("mm_in_ab_dw", hn, dz, ta=True, out_dtype=BF16, out_stack=W["w_in_ab"][i].shape[2])
    g = _row2(W["norm_ab"][i]) + send(1)
    (dx,), (dg,) = _rw_bwd("rms_bwd", _f_rms_res, [_cols(x)], [_par(g)], [_cols(dx1), _cols(dhn)], [(0, F32)], [0], 256)
    G["norm_ab"][i] = dg.reshape(-1)
    return dx


def _odd_fwd(x, W, i, zero, rest_of_weights):
    hn = _rw_fwd("rms_fwd", _f_rms, [_cols(x)], [_par(_row2(W["norm_cd"][i]) + zero)], [(D_MODEL, BF16, 1)], 256)[0]
    z = _mm("mm_in_cd", hn, W["w_in_cd"][i])
    sgu_p = [_par(_row2(W["sgu_ln_g"][i]), 4), _par(_row2(W["sgu_ln_b"][i]), 4), _par(W["sgu_w"][i], 4), _par(W["sgu_b"][i][..., None], 4)]
    c_out = _rw_fwd("sgu_fwd", _f_sgu, [_cols(z, 0, 1024, 4), _cols(z, 1024, 1024, 4), _cols(z, 2048, 1024, 4)], sgu_p, [(1024, BF16, 4)], 128)[0]
    prep_rows = [_cols(W["s5_a_re"][i]), _cols(W["s5_a_im"][i]), _cols(W["s5_log_dt"][i].reshape(S5_GROUPS, 1))]
    abar_re, abar_im, coef_re, coef_im = _rw_fwd("s5_prep_fwd", _f_s5_prep, prep_rows, [], [(S5_STATE, F32, 1)] * 4, S5_GROUPS)
    bb_rows = [_cols(W["s5_b_re"][i].reshape(2048, 16)), _cols(W["s5_b_im"][i].reshape(2048, 16)), _cols(coef_re.reshape(2048, 1)), _cols(coef_im.reshape(2048, 1))]
    bb_re, bb_im = _rw_fwd("s5_bbar_fwd", _f_bbar, bb_rows, [], [(16, F32, 1)] * 2, 256)
    bd = _bdense(bb_re, bb_im).astype(BF16)
    cf = _cdense(W["s5_c_re"][i], W["s5_c_im"][i]).astype(BF16)
    a2 = jnp.concatenate([abar_re.reshape(16, 128), abar_im.reshape(16, 128)], axis=0)
    bu = _s5_bu(z, bd)
    h3 = _scan_fwd(bu.reshape(SEQ, 32, 128), a2)
    h2 = h3.reshape(SEQ, 4096)
    yc = _s5_y(h2, cf)
    dpar = _row2(W["s5_d"][i])
    yg = _rw_fwd("gelu_fwd", _f_gelu_y, [_cols(yc), _cols(z, 3072, 512)], [_par(dpar)], [(512, BF16, 1)], 256)[0]
    zero = rest_of_weights(yg)
    w12 = jnp.concatenate([W["glu_w1"][i], W["glu_w2"][i]], axis=1)
    t12 = _mm("mm_glu", yg, w12)
    d_out = _rw_fwd("glu_gate_fwd", _f_glu_gate, [_cols(t12, nsplit=2), _cols(z, 3584, 512)], [], [(512, BF16, 1)], 256)[0]
    cd = jnp.concatenate([c_out, d_out], axis=1)
    x1 = _mm("mm_out_cd", cd, W["w_out_cd"][i], res=x)
    sv = dict(x=x, hn=hn, z=z, sgu_p=sgu_p, prep_rows=prep_rows, bb_rows=bb_rows, bb=(bb_re, bb_im), bd=bd, cf=cf, a2=a2,
              h3=h3, h2=h2, yc=yc, dpar=dpar, yg=yg, w12=w12, t12=t12, cd=cd)
    return x1, sv, zero


def _odd_bwd(dx1, sv, W, G, i, send):
    x, hn, z = sv["x"], sv["hn"], sv["z"]
    dcd = _mm("mm_out_cd_dx", dx1, W["w_out_cd"][i], tb=True)
    G["w_out_cd"][i] = _mm("mm_out_cd_dw", sv["cd"], dx1, ta=True, out_dtype=BF16)
    (du, dv, dgc), (dlg, dlb, dsw, dsb) = _rw_bwd(
        "sgu_bwd", _f_sgu, [_cols(z, 0, 1024, 4), _cols(z, 1024, 1024, 4), _cols(z, 2048, 1024, 4)], sv["sgu_p"],
        [_cols(dcd, 0, 1024, 4)], [(0, BF16), (1, BF16), (2, BF16)], [0, 1, 2, 3], 128)
    G["sgu_ln_g"][i], G["sgu_ln_b"][i] = dlg.reshape(-1), dlb.reshape(-1)
    G["sgu_w"][i], G["sgu_b"][i] = dsw, dsb[..., 0]
    (dt12, dgd), _ = _rw_bwd("glu_gate_bwd", _f_glu_gate, [_cols(sv["t12"], nsplit=2), _cols(z, 3584, 512)], [], [_cols(dcd, 1024, 512)],
                             [(0, BF16), (1, BF16)], [], 256)
    dyg = _mm("mm_glu_dx", dt12, sv["w12"], tb=True)
    dw12 = _mm("mm_glu_dw", sv["yg"], dt12, ta=True, out_dtype=BF16)
    G["glu_w1"][i], G["glu_w2"][i] = dw12[:, :512], dw12[:, 512:]
    (dyc, duf1), (dd,) = _rw_bwd("gelu_bwd", _f_gelu_y, [_cols(sv["yc"]), _cols(z, 3072, 512)], [_par(sv["dpar"])], [_cols(dyg)],
                                 [(0, BF16), (1, F32)], [0], 256)
    G["s5_d"][i] = dd.reshape(-1)
    dh2 = _s5_y_dx(dyc, sv["cf"])
    dcf = _s5_y_dw(sv["h2"], dyc)
    _, cvjp = jax.vjp(_cdense, W["s5_c_re"][i], W["s5_c_im"][i])
    G["s5_c_re"][i], G["s5_c_im"][i] = cvjp(dcf)
    dbu3, da2 = _scan_bwd(dh2.reshape(SEQ, 32, 128), sv["h3"], sv["a2"])
    dbu = dbu3.reshape(SEQ, 4096)
    duf2 = _s5_bu_dx(dbu, sv["bd"])
    dbd = _s5_bu_dw(z, dbu)
    _, bvjp = jax.vjp(_bdense, *sv["bb"])
    dbb_re, dbb_im = bvjp(dbd)
    (dbr, dbi, dcr, dci), _ = _rw_bwd("s5_bbar_bwd", _f_bbar, sv["bb_rows"], [], [_cols(dbb_re), _cols(dbb_im)],
                                      [(0, F32), (1, F32), (2, F32), (3, F32)], [], 256)
    G["s5_b_re"][i], G["s5_b_im"][i] = dbr.reshape(S5_GROUPS, S5_STATE, S5_GROUP_DIM), dbi.reshape(S5_GROUPS, S5_STATE, S5_GROUP_DIM)
    douts = [_cols(da2[0:16].reshape(S5_GROUPS, S5_STATE)), _cols(da2[16:32].reshape(S5_GROUPS, S5_STATE)),
             _cols(dcr.reshape(S5_GROUPS, S5_STATE)), _cols(dci.reshape(S5_GROUPS, S5_STATE))]
    (dar, dai, dldt), _ = _rw_bwd("s5_prep_bwd", _f_s5_prep, sv["prep_rows"], [], douts, [(0, F32), (1, F32), (2, F32)], [], S5_GROUPS)
    G["s5_a_re"][i], G["s5_a_im"][i], G["s5_log_dt"][i] = dar, dai, dldt.reshape(-1)
    dxd = (duf1 + duf2).astype(BF16)
    dz = jnp.concatenate([du, dv, dgc, dxd, dgd], axis=1)
    dhn = _mm("mm_in_cd_dx", dz, W["w_in_cd"][i], tb=True)
    G["w_in_cd"][i] = _mm("mm_in_cd_dw", hn, dz, ta=True, out_dtype=BF16, out_stack=W["w_in_cd"][i].shape[2])
    g = _row2(W["norm_cd"][i]) + send(0)
    (dx,), (dg,) = _rw_bwd("rms_bwd", _f_rms_res, [_cols(x)], [_par(g)], [_cols(dx1), _cols(dhn)], [(0, F32)], [0], 256)
    G["norm_cd"][i] = dg.reshape(-1)
    return dx


def _cross_fwd(x1, mem_n, W, l, zero):
    hx = _rw_fwd("rms_fwd", _f_rms, [_cols(x1)], [_par(_row2(W["norm_x"][l]) + zero)], [(D_MODEL, BF16, 1)], 256)[0]
    qx = _mm("mm_xq", hx, W["w_xq"][l], out_dtype=BF16)
    kv = _mm("mm_xkv", mem_n, W["w_xkv"][l], out_dtype=BF16)
    ox = _attn_fwd("attn_cross_fwd", qx, kv, kv, out_dtype=BF16, **_CROSS)
    x2 = _mm("mm_xo", ox, W["w_xo"][l], res=x1)
    return x2, dict(x1=x1, hx=hx, qx=qx, kv=kv, ox=ox)


def _cross_bwd(dx2, dmem_n, sv, mem_n, W, G, l, zero):
    dox = _mm("mm_xo_dx", dx2, W["w_xo"][l], tb=True, out_dtype=BF16)
    G["w_xo"][l] = _mm("mm_xo_dw", sv["ox"], dx2, ta=True, out_dtype=BF16)
    dqx, dk, dv = _attn_bwd("attn_cross_bwd", sv["qx"], sv["kv"], sv["kv"], dox, out_dtype=BF16, **_CROSS)
    dkv = jnp.concatenate([dk, dv], axis=1)
    dhx = _mm("mm_xq_dx", dqx, W["w_xq"][l], tb=True)
    G["w_xq"][l] = _mm("mm_xq_dw", sv["hx"], dqx, ta=True, out_dtype=BF16)
    dmem_n = _mm("mm_xkv_dx", dkv, W["w_xkv"][l], tb=True, res=dmem_n)
    G["w_xkv"][l] = _mm("mm_xkv_dw", mem_n, dkv, ta=True, out_dtype=BF16, out_stack=W["w_xkv"][l].shape[2])
    (dx1,), (dg,) = _rw_bwd("rms_bwd", _f_rms_res, [_cols(sv["x1"])], [_par(_row2(W["norm_x"][l]) + zero)], [_cols(dx2), _cols(dhx)], [(0, F32)], [0], 256)
    G["norm_x"][l] = dg.reshape(-1)
    return dx1, dmem_n


_PER_LAYER = ("pool_scale", "norm_ab", "norm_cd", "sgu_ln_g", "sgu_ln_b", "sgu_w", "sgu_b", "s5_d", "s5_c_re", "s5_c_im", "s5_b_re", "s5_b_im",
              "s5_a_re", "s5_a_im", "s5_log_dt", "w_in_ab", "pool_w", "w_out_ab", "w_in_cd", "glu_w1", "glu_w2", "w_out_cd")


def _local_step(x, mem, target, W, weights_of, send_grads, after_layer):
    G = {k: [None, None] for k in _PER_LAYER}
    for k in ("norm_x", "w_xq", "w_xkv", "w_xo"):
        G[k] = [None] * DEPTH
    mem_rows = [_cols(mem)]
    mem_par = [_par(_row2(W["mem_norm"]))]
    mem_n = _rw_fwd("rms_fwd_mem", _f_rms, mem_rows, mem_par, [(D_MODEL, BF16, 1)], 256)[0]
    saved = []
    for layer in range(DEPTH):
        zero = weights_of(layer, 0, x if layer else mem_n)
        mixer = _even_fwd if layer % 2 == 0 else _odd_fwd
        x, sv, zero = mixer(x, W, layer // 2, zero, functools.partial(weights_of, layer, 1))
        x, svx = _cross_fwd(x, mem_n, W, layer, zero)
        saved.append((sv, svx))
    loss, dx, dfinal = _loss_head(x, target, _row2(W["final_norm"]))
    G["final_norm"] = dfinal.reshape(-1)
    dmem_n, zero = None, 0.0
    for layer in reversed(range(DEPTH)):
        sv, svx = saved[layer]
        dx, dmem_n = _cross_bwd(dx, dmem_n, svx, mem_n, W, G, layer, zero)
        hook = functools.partial(send_grads, layer, G)
        dx = _even_bwd(dx, sv, W, G, layer // 2, hook) if layer % 2 == 0 else _odd_bwd(dx, sv, W, G, layer // 2, hook)
        zero = after_layer(layer, G)
    _, (dmn,) = _rw_bwd("rms_bwd_mem", _f_rms, mem_rows, mem_par, [_cols(dmem_n)], [], [0], 256)
    G["mem_norm"] = dmn.reshape(-1)
    return loss, dx, G


_HBM = pl.BlockSpec(memory_space=pltpu.HBM)
_ANY = pl.BlockSpec(memory_space=pl.ANY)
_SEM = pl.BlockSpec(memory_space=pltpu.SEMAPHORE)
_N_PEERS = N_DEV - 1


def _mesh_pos():
    return lax.axis_index("x"), lax.axis_index("y"), lax.axis_index("c")


def _peer(pos, k):
    x, y, c = pos
    return (x ^ ((k >> 2) & 1), y ^ ((k >> 1) & 1), c ^ (k & 1))


def _lin(pos):
    return 4 * pos[0] + 2 * pos[1] + pos[2]


def _ends(gather, srcs, lands, t, sender, receiver):
    if gather:
        return lands[t].at[sender], lands[t].at[sender]
    whole = len(srcs[t].shape) != len(lands[t].shape)
    return (srcs[t] if whole else srcs[t].at[receiver]), lands[t].at[sender]


def _into_slot(name, b2, r0, r, me, dtype, after):
    c = b2.shape[1]
    tr = _row_block(r, c, 2 << 20)
    assert r0 % tr == 0

    def body(me_ref, x_ref, *rest):
        rest[-1][...] = x_ref[...].astype(dtype)

    extra = [] if after is None else [after]
    grid_spec = pltpu.PrefetchScalarGridSpec(
        num_scalar_prefetch=1, grid=(r // tr,),
        in_specs=[pl.BlockSpec((tr, c), lambda i, me, o=r0 // tr: (o + i, 0))] + [_ANY] * len(extra),
        out_specs=pl.BlockSpec((None, tr, c), lambda i, me: (me[0], i, 0)))
    return pl.pallas_call(
        body, name=name, grid_spec=grid_spec, out_shape=jax.ShapeDtypeStruct((N_DEV, r, c), dtype),
        compiler_params=pltpu.CompilerParams(dimension_semantics=("arbitrary",), vmem_limit_bytes=V7X_VMEM_LIMIT_BYTES),
        interpret=False,
    )(me, b2, *extra)


def _exchange_start(name, gather, srcs, lands, after=None):
    ns, nt = len(srcs), len(lands)
    arrs = list(srcs) + list(lands)
    extra = [] if after is None else [after]

    def body(*refs):
        ins, lnd = refs[:ns], refs[ns:ns + nt]
        refs = refs[len(extra):]
        send_sems, recv_sems = refs[ns + nt], refs[ns + nt + 1]
        token = refs[-1]
        pos = _mesh_pos()
        me = _lin(pos)
        for k in range(1, N_DEV):
            peer = _peer(pos, k)
            for t in range(nt):
                src, dst = _ends(gather, ins, lnd, t, me, _lin(peer))
                pltpu.make_async_remote_copy(
                    src_ref=src, dst_ref=dst, send_sem=send_sems.at[t * _N_PEERS + k - 1], recv_sem=recv_sems.at[t * _N_PEERS + k - 1],
                    device_id=peer, device_id_type=pl.DeviceIdType.MESH).start()
        token[...] = jnp.zeros_like(token)

    out = pl.pallas_call(
        body, name=name,
        out_shape=(pltpu.SemaphoreType.DMA((nt * _N_PEERS,)), pltpu.SemaphoreType.DMA((nt * _N_PEERS,)), *[pltpu.HBM(a.shape, a.dtype) for a in arrs],
                   jax.ShapeDtypeStruct((8, 128), F32)),
        in_specs=[_HBM] * (ns + nt) + [_ANY] * len(extra), out_specs=(_SEM, _SEM, *[_HBM] * (ns + nt), pl.BlockSpec(memory_space=pltpu.VMEM)),
        input_output_aliases={j: 2 + j for j in range(ns + nt)},
        compiler_params=pltpu.CompilerParams(has_side_effects=pltpu.SideEffectType.DATAFLOW_SIDE_EFFECTING),
        interpret=False,
    )(*[pltpu.with_memory_space_constraint(a, pltpu.HBM) for a in arrs], *extra)
    return dict(send=out[0], recv=out[1], srcs=list(out[2:2 + ns]), lands=list(out[2 + ns:2 + ns + nt]), token=out[-1][0, 0], token_arr=out[-1], gather=gather)


def _exchange_wait(name, ex, after):
    ns, nt = len(ex["srcs"]), len(ex["lands"])
    gather = ex["gather"]
    arrs = ex["srcs"] + ex["lands"]
    after = list(after) if isinstance(after, (list, tuple)) else [after]

    def body(*refs):
        ins, lnd = refs[:ns], refs[ns:ns + nt]
        send_sems, recv_sems = refs[ns + nt], refs[ns + nt + 1]
        pos = _mesh_pos()
        me = _lin(pos)
        for k in range(1, N_DEV):
            peer = _peer(pos, k)
            for t in range(nt):
                src, _ = _ends(gather, ins, lnd, t, me, _lin(peer))
                _, dst = _ends(gather, ins, lnd, t, _lin(peer), me)
                cp = pltpu.make_async_remote_copy(
                    src_ref=src, dst_ref=dst, send_sem=send_sems.at[t * _N_PEERS + k - 1], recv_sem=recv_sems.at[t * _N_PEERS + k - 1],
                    device_id=peer, device_id_type=pl.DeviceIdType.MESH)
                cp.wait_send()
                cp.wait_recv()

    out = pl.pallas_call(
        body, name=name, out_shape=tuple(pltpu.HBM(a.shape, a.dtype) for a in arrs),
        in_specs=[_HBM] * (ns + nt) + [_SEM, _SEM] + [_ANY] * len(after), out_specs=tuple([_HBM] * (ns + nt)),
        input_output_aliases={j: j for j in range(ns + nt)},
        compiler_params=pltpu.CompilerParams(has_side_effects=pltpu.SideEffectType.DATAFLOW_SIDE_EFFECTING),
        interpret=False,
    )(*arrs, ex["send"], ex["recv"], *after)
    return list(out[:ns]), list(out[ns:])


def _scatter_begin(name, srcs):
    lands = [lax.empty(s.shape if s.ndim == 3 else (N_DEV,) + s.shape, s.dtype) for s in srcs]
    return _exchange_start(name, False, srcs, lands)


def _adam(name, w, m, v, parts, own, me, layer, bufs):
    r, c = parts.shape[1:]
    tr = _row_block(r, max(c, 128), 1 << 20)
    nb = r // tr

    def body(me_ref, w_ref, m_ref, v_ref, p_ref, own_ref, *rest):
        g_ref, d_ref, nm_ref, nv_ref, acc = rest[-5:]
        acc[...] = jnp.zeros_like(acc)
        for k in range(N_DEV):
            @pl.when(me_ref[0] == k)
            def _():
                acc[...] += own_ref[...].astype(F32)

            @pl.when(me_ref[0] != k)
            def _(k=k):
                acc[...] += p_ref[k].astype(F32)

        g = acc[...]
        mm = ADAM_B1 * m_ref[...] + (1.0 - ADAM_B1) * g
        vv = ADAM_B2 * v_ref[...] + (1.0 - ADAM_B2) * jnp.square(g)
        m_hat = mm / (1.0 - ADAM_B1 ** ADAM_STEP)
        v_hat = vv / (1.0 - ADAM_B2 ** ADAM_STEP)
        g_ref[...] = g
        d_ref[...] = -ADAM_LR * (m_hat / (jnp.sqrt(v_hat) + ADAM_EPS) + ADAM_WD * w_ref[...])
        nm_ref[...] = mm
        nv_ref[...] = vv

    blk = pl.BlockSpec((tr, c), lambda i, me, o=layer * nb: (o + i, 0))
    own_spec = pl.BlockSpec((None, tr, c), lambda i, me: (me[0], i, 0)) if own.ndim == 3 else pl.BlockSpec((tr, c), lambda i, me: (i, 0))
    in_specs = [blk, blk, blk, pl.BlockSpec((N_DEV, tr, c), lambda i, me: (0, i, 0)), own_spec]
    args = [me, w, m, v, parts, own]
    aliases = {}
    if bufs is not None:
        in_specs += [_ANY] * 4
        aliases = {len(args) + j: j for j in range(4)}
        args += list(bufs)
    grid_spec = pltpu.PrefetchScalarGridSpec(
        num_scalar_prefetch=1, grid=(nb,), in_specs=in_specs, out_specs=[blk] * 4, scratch_shapes=[pltpu.VMEM((tr, c), F32)])
    return pl.pallas_call(
        body, name=name, grid_spec=grid_spec, out_shape=[jax.ShapeDtypeStruct(w.shape, F32)] * 4, input_output_aliases=aliases,
        compiler_params=pltpu.CompilerParams(dimension_semantics=("arbitrary",), vmem_limit_bytes=V7X_VMEM_LIMIT_BYTES),
        interpret=False,
    )(*args)


def _row_block(r, c, limit):
    best = None
    for tr in range(16, r + 1, 16):
        if r % tr == 0 and tr * c * 4 <= limit:
            best = tr
    return r if best is None else best


_BIG = ("w_in_ab", "pool_w", "w_out_ab", "w_in_cd", "glu_w1", "glu_w2", "w_out_cd", "w_xq", "w_xkv", "w_xo")
_STACKED = ("w_in_ab", "w_in_cd", "w_xkv")
_MIXER_BIG = (("w_in_ab", "pool_w", "w_out_ab"), ("w_in_cd", "glu_w1", "glu_w2", "w_out_cd"))
_CROSS_BIG = ("w_xq", "w_xkv", "w_xo")
_SMALL_SPLIT = ["norm_cd", "sgu_ln_g", "sgu_ln_b", "s5_d"]
_REPLICATED_ODD = ["sgu_w", "sgu_b", "s5_a_re", "s5_a_im", "s5_log_dt", "s5_b_re", "s5_b_im", "s5_c_re", "s5_c_im", "final_norm"]
_REPLICATED_EVEN = ["norm_ab", "pool_scale", "norm_x", "mem_norm"]
_REPLICATED = _REPLICATED_ODD + _REPLICATED_EVEN
_WEIGHTS = ["norm_ab", "w_in_ab", "pool_w", "pool_scale", "w_out_ab", "norm_cd", "w_in_cd", "sgu_ln_g", "sgu_ln_b", "sgu_w", "sgu_b", "s5_a_re",
            "s5_a_im", "s5_log_dt", "s5_b_re", "s5_b_im", "s5_c_re", "s5_c_im", "s5_d", "glu_w1", "glu_w2", "w_out_cd", "norm_x", "w_xq",
            "w_xkv", "w_xo", "mem_norm", "final_norm"]


def _layer_big(layer):
    return [(n, layer // 2) for n in _MIXER_BIG[layer % 2]] + [(n, layer) for n in _CROSS_BIG]


def _gather_parts(layer):
    big = _layer_big(layer)
    return [big[:1], big[1:]]


def _scatter_parts(layer):
    big = _layer_big(layer)
    return [big[2:], big[:2]] if layer % 2 == 0 else [big]


def _from_slots(name, a):
    if name in _STACKED:
        return a
    if name == "pool_w":
        return a.reshape(N_DEV, 4, 32, 256).transpose(1, 0, 2, 3).reshape(4, 256, 256)
    return a.reshape(-1, a.shape[-1])


def _to_slots(name, g):
    if name in _STACKED:
        return g
    if name == "pool_w":
        return g.reshape(4, N_DEV, 32, 256).transpose(1, 0, 2, 3).reshape(N_DEV, 128, 256)
    return g.reshape(N_DEV, -1, g.shape[-1])


def _rows2d(a):
    return a.reshape(-1, a.shape[-1])


def _small_rows(block):
    return jnp.pad(block, ((0, 0), (0, 128 - block.shape[1])))


def kernel(x, mem, norm_ab, w_in_ab, pool_w, pool_scale, w_out_ab, norm_cd, w_in_cd, sgu_ln_g, sgu_ln_b, sgu_w, sgu_b, s5_a_re, s5_a_im, s5_log_dt, s5_b_re, s5_b_im, s5_c_re, s5_c_im, s5_d, glu_w1, glu_w2, w_out_cd, norm_x, w_xq, w_xkv, w_xo, mem_norm, final_norm, loss_target, m_norm_ab, m_w_in_ab, m_pool_w, m_pool_scale, m_w_out_ab, m_norm_cd, m_w_in_cd, m_sgu_ln_g, m_sgu_ln_b, m_sgu_w, m_sgu_b, m_s5_a_re, m_s5_a_im, m_s5_log_dt, m_s5_b_re, m_s5_b_im, m_s5_c_re, m_s5_c_im, m_s5_d, m_glu_w1, m_glu_w2, m_w_out_cd, m_norm_x, m_w_xq, m_w_xkv, m_w_xo, m_mem_norm, m_final_norm, v_norm_ab, v_w_in_ab, v_pool_w, v_pool_scale, v_w_out_ab, v_norm_cd, v_w_in_cd, v_sgu_ln_g, v_sgu_ln_b, v_sgu_w, v_sgu_b, v_s5_a_re, v_s5_a_im, v_s5_log_dt, v_s5_b_re, v_s5_b_im, v_s5_c_re, v_s5_c_im, v_s5_d, v_glu_w1, v_glu_w2, v_w_out_cd, v_norm_x, v_w_xq, v_w_xkv, v_w_xo, v_mem_norm, v_final_norm):
    args = locals()
    w = {n: args[n] for n in _WEIGHTS}
    m = {n: args["m_" + n] for n in _WEIGHTS}
    v = {n: args["v_" + n] for n in _WEIGHTS}

    me = jnp.reshape(_lin(_mesh_pos()), (1,)).astype(jnp.int32)

    order = [(layer, p) for layer in range(DEPTH) for p in range(len(_gather_parts(layer)))]
    lands, gathers = {}, {}
    for key in order:
        lands[key] = []
        for name, i in _gather_parts(key[0])[key[1]]:
            b2 = _rows2d(w[name])
            r = b2.shape[0] // w[name].shape[0]
            lands[key].append(_into_slot("cast_" + name, b2, i * r, r, me, BF16, None))
    small_blocks = jnp.concatenate([_small_rows(w[n]) for n in _SMALL_SPLIT], axis=0)
    lands[order[0]].append(_into_slot("cast_small", small_blocks, 0, 8, me, F32, None))

    def begin_gather(key, after):
        gathers[key] = _exchange_start("gather%d%s_start" % (key[0], "ab"[key[1]]), True, [], lands[key], after)
        return gathers[key]["token"]

    W = {n: w[n] for n in _REPLICATED}
    W["mem_norm"] = w["mem_norm"] + begin_gather(order[0], None)
    for name in _BIG:
        W[name] = [None] * w[name].shape[0]

    def weights_of(layer, part, after):
        key = (layer, part)
        if key not in gathers:
            return 0.0
        _, got = _exchange_wait("gather%d%s_wait" % (layer, "ab"[part]), gathers[key], after)
        for (name, i), arr in zip(_gather_parts(layer)[part], got):
            W[name][i] = _from_slots(name, arr)
        if key == order[0]:
            sm = got[-1].reshape(N_DEV, 4, 2, 128)
            for j, n in enumerate(_SMALL_SPLIT):
                width = w[n].shape[1]
                W[n] = sm[:, j, :, :width].transpose(1, 0, 2).reshape(2, N_DEV * width)
        nxt = order.index(key) + 1
        return begin_gather(order[nxt], got[0]) if nxt < len(order) else 0.0

    scatters, small = {}, {}

    def send_grads(layer, G, part):
        srcs = [_to_slots(name, G[name][i]) for name, i in _scatter_parts(layer)[part]]
        scatters[layer, part] = _scatter_begin("scatter%d%s_start" % (layer, "ab"[part]), srcs)
        return scatters[layer, part]["token"]

    def rows(a):
        return a.reshape(1, -1) if a.ndim == 1 else a.reshape(-1, a.shape[-1])

    def begin_small(tag, G, names, split):
        srcs = [rows(G[n] if n in ("mem_norm", "final_norm") else jnp.stack(G[n])) for n in names]
        if split:
            srcs += [jnp.stack(G[n]).reshape(2, N_DEV, -1).transpose(1, 0, 2) for n in _SMALL_SPLIT]
        small[tag] = _scatter_begin("scatter_small_%s_start" % tag, srcs)
        return small[tag]["token"]

    def after_layer(layer, G):
        return begin_small("odd", G, _REPLICATED_ODD, True) if layer == 1 else 0.0

    loss, dx, G = _local_step(x[0], mem[0], loss_target[0], W, weights_of, send_grads, after_layer)
    loss = lax.psum(loss[0, 0], MESH_AXES)
    begin_small("even", G, _REPLICATED_EVEN, False)

    out = {}
    after = small["even"]["token_arr"]
    for layer, part in scatters:
        own, got = _exchange_wait("scatter%d%s_wait" % (layer, "ab"[part]), scatters[layer, part], after)
        for (name, i), mine, parts in zip(_scatter_parts(layer)[part], own, got):
            out[name] = _adam("adam_" + name, _rows2d(w[name]), _rows2d(m[name]), _rows2d(v[name]), parts, mine, me, i, out.get(name))
        after = [out[name][0] for name, i in _scatter_parts(layer)[part]]
    for name in _BIG:
        out[name] = [a.reshape(w[name].shape) for a in out[name]]
    for tag, names in (("odd", _REPLICATED_ODD + _SMALL_SPLIT), ("even", _REPLICATED_EVEN)):
        own, got = _exchange_wait("scatter_small_%s_wait" % tag, small[tag], after)
        for n, mine, parts in zip(names, own, got):
            res = _adam("adam_" + n, rows(w[n]), rows(m[n]), rows(v[n]), parts, mine, me, 0, None)
            out[n] = [a.reshape(w[n].shape) for a in res]
        after = [out[n][0] for n in names]

    return (loss, dx[None], *[out[n][0] for n in _WEIGHTS], *[out[n][1] for n in _WEIGHTS], *[out[n][2] for n in _WEIGHTS],
            *[out[n][3] for n in _WEIGHTS])
```

```python
import functools
import math

import jax
import jax.numpy as jnp
from jax import lax
from jax.experimental import pallas as pl
from jax.experimental.pallas import tpu as pltpu

F32 = jnp.float32
BF16 = jnp.bfloat16

SEQ = 2048
D_MODEL = 1024
MEM_LEN = 256
DEPTH = 4
N_DEV = 8
EPS = 1e-6
NEG = -1e30
A_HEAD_DIM = 64
X_HEAD_DIM = 256
S5_GROUPS = 32
S5_STATE = 64
S5_GROUP_DIM = 16

ADAM_LR = 0.001
ADAM_B1 = 0.9
ADAM_B2 = 0.999
ADAM_EPS = 1e-08
ADAM_WD = 0.01
ADAM_STEP = 10

V7X_VMEM_LIMIT_BYTES = 56 * 1024 * 1024
_MM_VMEM_BYTES = 36 * 1024 * 1024
MESH_AXES = ("x", "y", "c")


def _pc(body, *, name, out_shape, grid=None, in_specs=None, out_specs=None, scratch_shapes=(), aliases=None, sem=None):
    kw = {}
    if grid is not None:
        kw["grid"] = grid
    if in_specs is not None:
        kw["in_specs"] = in_specs
    if out_specs is not None:
        kw["out_specs"] = out_specs
    if aliases:
        kw["input_output_aliases"] = aliases
    return pl.pallas_call(
        body,
        name=name,
        out_shape=out_shape,
        scratch_shapes=list(scratch_shapes),
        compiler_params=pltpu.CompilerParams(dimension_semantics=sem, vmem_limit_bytes=V7X_VMEM_LIMIT_BYTES),
        interpret=False,
        **kw,
    )


def _cols(arr, c0=0, width=None, nsplit=1, r0=0):
    width = arr.shape[1] - c0 if width is None else width
    assert c0 % width == 0 and width % nsplit == 0
    return (arr, c0, width, nsplit, r0)


def _par(arr, nsplit=1):
    return (arr, nsplit)


def _ld(ref, nsplit):
    if nsplit == 1:
        return ref[...].astype(F32)
    if len(ref.shape) == 3:
        return tuple(ref[k].astype(F32) for k in range(nsplit))
    w = ref.shape[-1] // nsplit
    return tuple(ref[:, k * w:(k + 1) * w].astype(F32) for k in range(nsplit))


def _st(ref, val, nsplit, accumulate=False):
    if nsplit == 1:
        val = (val,)
    for k in range(nsplit):
        if nsplit == 1:
            idx = (Ellipsis,)
        elif len(ref.shape) == 3:
            idx = (k,)
        else:
            w = ref.shape[-1] // nsplit
            idx = (slice(None), slice(k * w, (k + 1) * w))
        if accumulate:
            ref[idx] += val[k].astype(ref.dtype)
        else:
            ref[idx] = val[k].astype(ref.dtype)


def _row_spec(tr, op):
    _, c0, w, _, r0 = op
    assert r0 % tr == 0
    return pl.BlockSpec((tr, w), lambda i, cb=c0 // w, rb=r0 // tr: (i + rb, cb))


def _full_spec(arr):
    return pl.BlockSpec(arr.shape, lambda i, nd=arr.ndim: (0,) * nd)


def _rw_fwd(name, f, rows, pars, outs, tr, n_rows=None):
    n_rows = rows[0][0].shape[0] if n_rows is None else n_rows
    nr, npar = len(rows), len(pars)

    def body(*refs):
        r = [_ld(refs[i], rows[i][3]) for i in range(nr)]
        p = [_ld(refs[nr + i], pars[i][1]) for i in range(npar)]
        res = f(r, p)
        for k, (_, _, ns) in enumerate(outs):
            _st(refs[nr + npar + k], res[k], ns)

    res = _pc(
        body, name=name, grid=(n_rows // tr,),
        in_specs=[_row_spec(tr, op) for op in rows] + [_full_spec(a) for a, _ in pars],
        out_specs=[pl.BlockSpec((tr, w), lambda i: (i, 0)) for w, _, _ in outs],
        out_shape=[jax.ShapeDtypeStruct((n_rows, w), dt) for w, dt, _ in outs],
        sem=("arbitrary",),
    )(*[op[0] for op in rows], *[a for a, _ in pars])
    return list(res)


def _rw_bwd(name, f, rows, pars, douts, drow, dpar, tr):
    n_rows = rows[0][0].shape[0]
    nr, npar = len(rows), len(pars)
    dgiven = [d for d in douts if d is not None]
    nd = len(dgiven)

    def body(*refs):
        r = [_ld(refs[i], rows[i][3]) for i in range(nr)]
        p = [_ld(refs[nr + i], pars[i][1]) for i in range(npar)]
        d = [_ld(refs[nr + npar + i], dgiven[i][3]) for i in range(nd)]
        orefs = refs[nr + npar + nd:]

        def g(dr, dp):
            rr, pp = list(r), list(p)
            for j, (idx, _) in enumerate(drow):
                rr[idx] = dr[j]
            for j, idx in enumerate(dpar):
                pp[idx] = dp[j]
            return tuple(f(rr, pp))

        out, vjp = jax.vjp(g, [r[idx] for idx, _ in drow], [p[idx] for idx in dpar])
        ct, j = [], 0
        for k, o in enumerate(out):
            if douts[k] is None:
                ct.append(jax.tree.map(jnp.zeros_like, o))
            else:
                ct.append(d[j])
                j += 1
        gdr, gdp = vjp(tuple(ct))
        for j, (idx, _) in enumerate(drow):
            _st(orefs[j], gdr[j], rows[idx][3])

        @pl.when(pl.program_id(0) == 0)
        def _():
            for j in range(len(dpar)):
                orefs[len(drow) + j][...] = jnp.zeros_like(orefs[len(drow) + j])

        for j, idx in enumerate(dpar):
            _st(orefs[len(drow) + j], gdp[j], pars[idx][1], accumulate=True)

    res = _pc(
        body, name=name, grid=(n_rows // tr,),
        in_specs=[_row_spec(tr, op) for op in rows] + [_full_spec(a) for a, _ in pars] + [_row_spec(tr, op) for op in dgiven],
        out_specs=[pl.BlockSpec((tr, rows[idx][2]), lambda i: (i, 0)) for idx, _ in drow] + [_full_spec(pars[idx][0]) for idx in dpar],
        out_shape=[jax.ShapeDtypeStruct((n_rows, rows[idx][2]), dt) for idx, dt in drow]
        + [jax.ShapeDtypeStruct(pars[idx][0].shape, F32) for idx in dpar],
        sem=("arbitrary",),
    )(*[op[0] for op in rows], *[a for a, _ in pars], *[op[0] for op in dgiven])
    res = list(res)
    return res[:len(drow)], res[len(drow):]


def _sigmoid(x):
    return jax.nn.sigmoid(x)


def _silu(x):
    return x * _sigmoid(x)


def _rms(x, g):
    return x * lax.rsqrt(jnp.mean(x * x, axis=-1, keepdims=True) + EPS) * g


def _f_rms(r, p):
    return [_rms(r[0], p[0])]


def _f_rms_res(r, p):
    return [r[0], _rms(r[0], p[0])]


def _f_gate_ab(r, p):
    o, ga, mixed, gb = r
    return [(o * _silu(ga), mixed * p[0] * _silu(gb))]


def _f_sgu(r, p):
    u, v, gc = r
    lg, lb, w, b = p
    n = float(D_MODEL)
    mu = sum(jnp.sum(vk, axis=-1, keepdims=True) for vk in v) / n
    var = sum(jnp.sum(jnp.square(vk - mu), axis=-1, keepdims=True) for vk in v) / n
    rs = lax.rsqrt(var + EPS)
    t = w[0].shape[0]
    tri = lax.broadcasted_iota(jnp.int32, (t, t), 0) >= lax.broadcasted_iota(jnp.int32, (t, t), 1)
    outs = []
    for k in range(len(v)):
        vn = (v[k] - mu) * rs * lg[k] + lb[k]
        mixed = jnp.dot(jnp.where(tri, w[k], 0.0), vn, preferred_element_type=F32) + b[k]
        outs.append(u[k] * mixed * _silu(gc[k]))
    return [tuple(outs)]


def _gelu(x):
    return 0.5 * x * (1.0 + jnp.tanh(math.sqrt(2.0 / math.pi) * (x + 0.044715 * (x * x * x))))


def _f_gelu_y(r, p):
    yc, uf = r
    return [_gelu(yc + p[0] * uf)]


def _f_glu_gate(r, p):
    t12, gd = r
    return [t12[0] * _sigmoid(t12[1]) * _silu(gd)]


def _f_s5_prep(r, p):
    ar, ai, ldt = r
    dt = jnp.exp(ldt)
    mag = jnp.exp(dt * ar)
    abar_re = mag * jnp.cos(dt * ai)
    abar_im = mag * jnp.sin(dt * ai)
    nr, ni = abar_re - 1.0, abar_im
    inv = 1.0 / (ar * ar + ai * ai)
    return [abar_re, abar_im, (nr * ar + ni * ai) * inv, (ni * ar - nr * ai) * inv]


def _f_bbar(r, p):
    br, bi, cr, ci = r
    return [cr * br - ci * bi, cr * bi + ci * br]


def _loss_head(x, target, g):
    tr = 256
    n_rows, width = x.shape

    def f(xv, gv, tv):
        err = jnp.square(_rms(xv, gv) - tv)
        return 0.5 * jnp.mean(err, axis=-1, keepdims=True)

    def body(x_ref, t_ref, g_ref, loss_ref, dx_ref, dg_ref):
        @pl.when(pl.program_id(0) == 0)
        def _():
            loss_ref[...] = jnp.zeros_like(loss_ref)
            dg_ref[...] = jnp.zeros_like(dg_ref)

        tv = t_ref[...]
        row_loss, vjp = jax.vjp(lambda a, b: f(a, b, tv), x_ref[...], g_ref[...])
        dx, dg = vjp(jnp.ones_like(row_loss))
        dx_ref[...] = dx
        dg_ref[...] += dg
        loss_ref[...] += jnp.broadcast_to(jnp.sum(row_loss, axis=0, keepdims=True), loss_ref.shape)

    blk = pl.BlockSpec((tr, width), lambda i: (i, 0))
    one = pl.BlockSpec((1, width), lambda i: (0, 0))
    return _pc(
        body, name="loss_head", grid=(n_rows // tr,), in_specs=[blk, blk, one],
        out_specs=[pl.BlockSpec((1, 128), lambda i: (0, 0)), blk, one],
        out_shape=[jax.ShapeDtypeStruct((1, 128), F32), jax.ShapeDtypeStruct(x.shape, F32), jax.ShapeDtypeStruct((1, width), F32)],
        sem=("arbitrary",),
    )(x, target, g)


_NT = (((1,), (1,)), ((), ()))
_TN = (((0,), (0,)), ((), ()))


def _tile(n, cap):
    t = min(n, cap)
    while n % t:
        t -= 128
    assert t > 0
    return t


def _mm(name, a, b, *, ta=False, tb=False, out_dtype=F32, a_off=0, a_width=None, res=None, out_stack=None):
    assert not (ta and tb)
    stacked = b.ndim == 3
    bk, bn = (b.shape[1], b.shape[0] * b.shape[2]) if stacked else b.shape
    if ta:
        kc = a.shape[0]
        m = a.shape[1] - a_off if a_width is None else a_width
        n = bn
        assert bk == kc and not stacked
    else:
        m = a.shape[0]
        kc = a.shape[1] - a_off if a_width is None else a_width
        n = bk if tb else bn
        assert (bn if tb else bk) == kc
    tn = _tile(b.shape[2] if stacked and not tb else (out_stack or n), 1024)
    size = lambda dt: jnp.dtype(dt).itemsize
    for tk_cap, tm_cap in ((2048, 2048), (2048, 1024), (1024, 1024), (1024, 512), (1024, 256)):
        tm, tk = _tile(m, tm_cap), _tile(b.shape[2] if stacked and tb else kc, tk_cap)
        nk = kc // tk
        vmem = 2 * tm * tk * size(a.dtype) + 2 * tk * tn * size(b.dtype) + tm * tn * (2 * size(out_dtype) + (4 if nk > 1 else 0) + (8 if res is not None else 0))
        if vmem <= _MM_VMEM_BYTES:
            break
    if ta:
        assert a_off % tm == 0
        a_spec = pl.BlockSpec((tk, tm), lambda i, j, k, o=a_off // tm: (k, i + o))
        dims = _TN
    else:
        assert a_off % tk == 0
        a_spec = pl.BlockSpec((tm, tk), lambda i, j, k, o=a_off // tk: (i, k + o))
        dims = _NT if tb else (((1,), (0,)), ((), ()))
    if stacked and tb:
        b_spec = pl.BlockSpec((None, tn, tk), lambda i, j, k, q=b.shape[2] // tk: (k // q, j, k % q))
    elif stacked:
        b_spec = pl.BlockSpec((None, tk, tn), lambda i, j, k, q=b.shape[2] // tn: (j // q, k, j % q))
    else:
        b_spec = pl.BlockSpec((tn, tk), lambda i, j, k: (j, k)) if tb else pl.BlockSpec((tk, tn), lambda i, j, k: (k, j))
    if out_stack:
        out_spec = pl.BlockSpec((None, tm, tn), lambda i, j, k, q=out_stack // tn: (j // q, i, j % q))
        out_shape = jax.ShapeDtypeStruct((n // out_stack, m, out_stack), out_dtype)
    else:
        out_spec = pl.BlockSpec((tm, tn), lambda i, j, k: (i, j))
        out_shape = jax.ShapeDtypeStruct((m, n), out_dtype)
    in_specs, args = [a_spec, b_spec], [a, b]
    has_res = res is not None
    if has_res:
        in_specs.append(pl.BlockSpec((tm, tn), lambda i, j, k: (i, j)))
        args.append(res)

    def finish(refs, acc):
        if has_res:
            acc = acc + refs[2][...].astype(F32)
        refs[3 if has_res else 2][...] = acc.astype(out_dtype)

    def body_one(*refs):
        finish(refs, lax.dot_general(refs[0][...].astype(BF16), refs[1][...].astype(BF16), dims, preferred_element_type=F32))

    def body(*refs):
        acc_ref = refs[-1]
        k = pl.program_id(2)

        @pl.when(k == 0)
        def _():
            acc_ref[...] = jnp.zeros_like(acc_ref)

        acc_ref[...] += lax.dot_general(refs[0][...].astype(BF16), refs[1][...].astype(BF16), dims, preferred_element_type=F32)

        @pl.when(k == nk - 1)
        def _():
            finish(refs, acc_ref[...])

    return _pc(
        body_one if nk == 1 else body, name=name, grid=(m // tm, n // tn, nk), in_specs=in_specs, out_specs=out_spec, out_shape=out_shape,
        scratch_shapes=[] if nk == 1 else [pltpu.VMEM((tm, tn), F32)], sem=("parallel", "parallel", "arbitrary"),
    )(*args)


def _mm_blocks(name, a, b, *, grid, out_blk, a_blk, a_idx, b_blk, b_idx, dims, out_dtype=F32):
    gi, gj, nk = grid

    def body(a_ref, b_ref, o_ref, acc_ref):
        k = pl.program_id(2)

        @pl.when(k == 0)
        def _():
            acc_ref[...] = jnp.zeros_like(acc_ref)

        acc_ref[...] += lax.dot_general(a_ref[...].astype(BF16), b_ref[...].astype(BF16), dims, preferred_element_type=F32)

        @pl.when(k == nk - 1)
        def _():
            o_ref[...] = acc_ref[...].astype(o_ref.dtype)

    return _pc(
        body, name=name, grid=grid, in_specs=[pl.BlockSpec(a_blk, a_idx), pl.BlockSpec(b_blk, b_idx)],
        out_specs=pl.BlockSpec(out_blk, lambda i, j, k: (i, j)), out_shape=jax.ShapeDtypeStruct((gi * out_blk[0], gj * out_blk[1]), out_dtype),
        scratch_shapes=[pltpu.VMEM(out_blk, F32)], sem=("parallel", "parallel", "arbitrary"),
    )(a, b)


def _head_masks(width, nsub):
    lane = lax.broadcasted_iota(jnp.int32, (1, width), 1)
    hd = width // nsub
    return [(lane >= h * hd) & (lane < (h + 1) * hd) for h in range(nsub)]


def _dilated_log_count(row0, tq, ext):
    delta = (row0 + lax.broadcasted_iota(jnp.int32, (tq, ext), 0)) - lax.broadcasted_iota(jnp.int32, (tq, ext), 1)
    cnt = (delta <= 128).astype(jnp.int32) + (((delta & 3) == 0) & (delta <= 512)).astype(jnp.int32) + ((delta & 15) == 0).astype(jnp.int32)
    logc = jnp.where(cnt == 3, math.log(3.0), jnp.where(cnt == 2, math.log(2.0), 0.0))
    return jnp.where((delta >= 0) & (cnt > 0), logc, NEG)


def _bias_table(tab, nq, tq):
    @pl.when(pl.program_id(0) == 0)
    def _():
        for d in range(nq):
            tab[d] = _dilated_log_count(d * tq, tq, tq)


def _scores(q, ke, tab, r, masks, h):
    qm = (jnp.where(masks[h], q, 0.0) if len(masks) > 1 else q).astype(BF16)
    s = lax.dot_general(qm, ke, _NT, preferred_element_type=F32)
    if tab is not None:
        s = s + jnp.concatenate([tab[r - c] for c in range(r + 1)], axis=1)
    p = jnp.exp(s - jnp.max(s, axis=-1, keepdims=True))
    return qm, p, 1.0 / jnp.sum(p, axis=-1, keepdims=True)


def _attn_fwd(name, qa, ka, va, *, qc, kc, vc, width, nblk, nsub, causal, tq, scale, out_dtype):
    sq, t_len = qa.shape[0], ka.shape[0]
    nq = sq // tq

    def body(q_ref, k_ref, v_ref, o_ref, *scratch):
        tab = scratch[0] if causal else None
        if causal:
            _bias_table(tab, nq, tq)
        kb = k_ref[...].astype(BF16)
        vb = v_ref[...].astype(BF16)
        masks = _head_masks(width, nsub)
        for r in range(nq):
            ext = (r + 1) * tq if causal else t_len
            q = q_ref[r * tq:(r + 1) * tq, :].astype(F32) * scale
            ke, ve = kb[:ext], vb[:ext]
            o = None
            for h in range(nsub):
                _, p, inv = _scores(q, ke, tab, r, masks, h)
                oh = jnp.dot(p.astype(BF16), ve, preferred_element_type=F32) * inv
                o = oh if o is None else jnp.where(masks[h], oh, o)
            o_ref[r * tq:(r + 1) * tq, :] = o.astype(o_ref.dtype)

    return _pc(
        body, name=name, grid=(nblk,),
        in_specs=[pl.BlockSpec((sq, width), lambda i, c=qc: (0, c + i)), pl.BlockSpec((t_len, width), lambda i, c=kc: (0, c + i)),
                  pl.BlockSpec((t_len, width), lambda i, c=vc: (0, c + i))],
        out_specs=pl.BlockSpec((sq, width), lambda i: (0, i)),
        out_shape=jax.ShapeDtypeStruct((sq, nblk * width), out_dtype),
        scratch_shapes=[pltpu.VMEM((nq, tq, tq), F32)] if causal else [], sem=("arbitrary",),
    )(qa, ka, va)


def _attn_bwd(name, qa, ka, va, doa, *, qc, kc, vc, width, nblk, nsub, causal, tq, scale, out_dtype):
    sq, t_len = qa.shape[0], ka.shape[0]

    nq = sq // tq

    def body(q_ref, k_ref, v_ref, do_ref, dq_ref, dk_ref, dv_ref, dk_acc, dv_acc, *scratch):
        tab = scratch[0] if causal else None
        if causal:
            _bias_table(tab, nq, tq)
        kb = k_ref[...].astype(BF16)
        vb = v_ref[...].astype(BF16)
        masks = _head_masks(width, nsub)
        dk_acc[...] = jnp.zeros_like(dk_acc)
        dv_acc[...] = jnp.zeros_like(dv_acc)
        for r in range(nq):
            ext = (r + 1) * tq if causal else t_len
            q = q_ref[r * tq:(r + 1) * tq, :].astype(F32) * scale
            do = do_ref[r * tq:(r + 1) * tq, :].astype(F32)
            ke, ve = kb[:ext], vb[:ext]
            dq = None
            for h in range(nsub):
                qm, p, inv = _scores(q, ke, tab, r, masks, h)
                dom = (jnp.where(masks[h], do, 0.0) if nsub > 1 else do).astype(BF16)
                pn = p * inv
                dpn = lax.dot_general(dom, ve, _NT, preferred_element_type=F32)
                dsb = (pn * (dpn - jnp.sum(pn * dpn, axis=-1, keepdims=True))).astype(BF16)
                dqh = jnp.dot(dsb, ke, preferred_element_type=F32)
                dq = dqh if dq is None else jnp.where(masks[h], dqh, dq)
                dk_acc[0:ext, :] += lax.dot_general(dsb, qm, _TN, preferred_element_type=F32)
                dv_acc[0:ext, :] += lax.dot_general(pn.astype(BF16), dom, _TN, preferred_element_type=F32)
            dq_ref[r * tq:(r + 1) * tq, :] = (dq * scale).astype(dq_ref.dtype)
        dk_ref[...] = dk_acc[...].astype(dk_ref.dtype)
        dv_ref[...] = dv_acc[...].astype(dv_ref.dtype)

    return _pc(
        body, name=name, grid=(nblk,),
        in_specs=[pl.BlockSpec((sq, width), lambda i, c=qc: (0, c + i)), pl.BlockSpec((t_len, width), lambda i, c=kc: (0, c + i)),
                  pl.BlockSpec((t_len, width), lambda i, c=vc: (0, c + i)), pl.BlockSpec((sq, width), lambda i: (0, i))],
        out_specs=[pl.BlockSpec((sq, width), lambda i: (0, i)), pl.BlockSpec((t_len, width), lambda i: (0, i)), pl.BlockSpec((t_len, width), lambda i: (0, i))],
        out_shape=[jax.ShapeDtypeStruct((sq, nblk * width), out_dtype), jax.ShapeDtypeStruct((t_len, nblk * width), out_dtype),
                   jax.ShapeDtypeStruct((t_len, nblk * width), out_dtype)],
        scratch_shapes=[pltpu.VMEM((t_len, width), F32), pltpu.VMEM((t_len, width), F32)] + ([pltpu.VMEM((nq, tq, tq), F32)] if causal else []),
        sem=("arbitrary",),
    )(qa, ka, va, doa)


_SELF = dict(qc=0, kc=8, vc=16, width=128, nblk=8, nsub=2, causal=True, tq=256, scale=A_HEAD_DIM ** -0.5)
_CROSS = dict(qc=0, kc=0, vc=4, width=256, nblk=4, nsub=1, causal=False, tq=512, scale=X_HEAD_DIM ** -0.5)


def _window_sum(x, g, row, backward):
    n = x.shape[0]

    def shift(y, k):
        if backward:
            return jnp.where(row < n - k, pltpu.roll(y, n - k, 0), 0.0)
        return jnp.where(row >= k, pltpu.roll(y, k, 0), 0.0)

    s2 = x + shift(x, 1)
    s4 = s2 + shift(s2, 2)
    s8 = s4 + shift(s4, 4)
    s16 = s8 + shift(s8, 8)
    return jnp.where(g == 0, s2, jnp.where(g == 1, s4, jnp.where(g == 2, s8, s16)))


def _pool(name, arr, c0, backward, out_dtype):
    n = arr.shape[0]
    gw = 256

    def body(v_ref, o_ref):
        g = pl.program_id(0)
        v = v_ref[...].astype(F32)
        row = lax.broadcasted_iota(jnp.int32, v.shape, 0)
        w = jnp.where(g == 0, 2, jnp.where(g == 1, 4, jnp.where(g == 2, 8, 16)))
        cnt = jnp.minimum(row + 1, w).astype(F32)
        if backward:
            o_ref[...] = (_window_sum(v / cnt, g, row, True) - v).astype(o_ref.dtype)
        else:
            o_ref[...] = (_window_sum(v, g, row, False) / cnt - v).astype(o_ref.dtype)

    return _pc(
        body, name=name, grid=(4,), in_specs=[pl.BlockSpec((n, gw), lambda i, c=c0 // gw: (0, c + i))],
        out_specs=pl.BlockSpec((n, gw), lambda i: (0, i)), out_shape=jax.ShapeDtypeStruct((n, 4 * gw), out_dtype), sem=("parallel",),
    )(arr)


_SCAN_ROWS = 256


def _scan_fwd(bu3, a2):
    n = bu3.shape[0]

    def body(bu_ref, a_ref, h_ref, carry):
        @pl.when(pl.program_id(0) == 0)
        def _():
            carry[...] = jnp.zeros_like(carry)

        ar, ai = a_ref[0:16, :], a_ref[16:32, :]

        def step(t, c):
            hr, hi = c
            nr = ar * hr - ai * hi + bu_ref[t, 0:16, :]
            ni = ar * hi + ai * hr + bu_ref[t, 16:32, :]
            h_ref[t, 0:16, :] = nr
            h_ref[t, 16:32, :] = ni
            return nr, ni

        hr, hi = lax.fori_loop(0, _SCAN_ROWS, step, (carry[0:16, :], carry[16:32, :]), unroll=8)
        carry[0:16, :] = hr
        carry[16:32, :] = hi

    blk = pl.BlockSpec((_SCAN_ROWS, 32, 128), lambda i: (i, 0, 0))
    return _pc(
        body, name="s5_scan_fwd", grid=(n // _SCAN_ROWS,), in_specs=[blk, pl.BlockSpec((32, 128), lambda i: (0, 0))], out_specs=blk,
        out_shape=jax.ShapeDtypeStruct(bu3.shape, F32), scratch_shapes=[pltpu.VMEM((32, 128), F32)], sem=("arbitrary",),
    )(bu3, a2)


def _scan_bwd(dh3, h3, a2):
    n = dh3.shape[0]
    nb = n // _SCAN_ROWS

    def body(dh_ref, h_ref, a_ref, dbu_ref, da_ref, carry):
        @pl.when(pl.program_id(0) == 0)
        def _():
            carry[...] = jnp.zeros_like(carry)
            da_ref[...] = jnp.zeros_like(da_ref)

        ar, ai = a_ref[0:16, :], a_ref[16:32, :]

        def step(tt, c):
            gr, gi, dar, dai = c
            t = _SCAN_ROWS - 1 - tt
            hr, hi = h_ref[t, 0:16, :], h_ref[t, 16:32, :]
            dar = dar + gr * hr + gi * hi
            dai = dai - gr * hi + gi * hr
            ngr = dh_ref[t, 0:16, :] + ar * gr + ai * gi
            ngi = dh_ref[t, 16:32, :] - ai * gr + ar * gi
            dbu_ref[t, 0:16, :] = ngr
            dbu_ref[t, 16:32, :] = ngi
            return ngr, ngi, dar, dai

        z = jnp.zeros((16, 128), F32)
        gr, gi, dar, dai = lax.fori_loop(0, _SCAN_ROWS, step, (carry[0:16, :], carry[16:32, :], z, z), unroll=8)
        carry[0:16, :] = gr
        carry[16:32, :] = gi
        da_ref[0:16, :] += dar
        da_ref[16:32, :] += dai

    blk = pl.BlockSpec((_SCAN_ROWS, 32, 128), lambda i: (nb - 1 - i, 0, 0))
    small = pl.BlockSpec((32, 128), lambda i: (0, 0))
    return _pc(
        body, name="s5_scan_bwd", grid=(nb,), in_specs=[blk, blk, small], out_specs=[blk, small],
        out_shape=[jax.ShapeDtypeStruct(dh3.shape, F32), jax.ShapeDtypeStruct((32, 128), F32)],
        scratch_shapes=[pltpu.VMEM((32, 128), F32)], sem=("arbitrary",),
    )(dh3, h3, a2)


def _bdense(bb_re, bb_im):
    eye = jnp.eye(8, dtype=F32)

    def one(bb):
        return jnp.einsum("sgph,gk->sghkp", bb.reshape(4, 8, S5_STATE, S5_GROUP_DIM), eye).reshape(512, 512)

    return jnp.concatenate([one(bb_re), one(bb_im)], axis=1)


def _cdense(c_re, c_im):
    eye = jnp.eye(8, dtype=F32)

    def one(cc):
        return jnp.einsum("sghp,gk->sgpkh", cc.reshape(4, 8, S5_GROUP_DIM, S5_STATE), eye).reshape(2048, 128)

    return jnp.concatenate([one(c_re), -one(c_im)], axis=0)


_NN = (((1,), (0,)), ((), ()))
_UF_BLOCK = 3072 // 128


def _s5_bu(z, bd):
    return _mm_blocks("mm_s5_bu", z, bd, grid=(1, 8, 1), out_blk=(SEQ, 512), a_blk=(SEQ, 128), a_idx=lambda i, j, k: (0, _UF_BLOCK + j % 4),
                      b_blk=(128, 512), b_idx=lambda i, j, k: (j % 4, j // 4), dims=_NN)


def _s5_bu_dx(dbu, bd):
    return _mm_blocks("mm_s5_bu_dx", dbu, bd, grid=(1, 4, 2), out_blk=(SEQ, 128), a_blk=(SEQ, 512), a_idx=lambda i, j, k: (0, 4 * k + j),
                      b_blk=(128, 512), b_idx=lambda i, j, k: (j, k), dims=_NT)


def _s5_bu_dw(z, dbu):
    return _mm_blocks("mm_s5_bu_dw", z, dbu, grid=(4, 2, 2), out_blk=(128, 512), a_blk=(1024, 128), a_idx=lambda i, j, k: (k, _UF_BLOCK + i),
                      b_blk=(1024, 512), b_idx=lambda i, j, k: (k, 4 * j + i), dims=_TN)


def _s5_y(h2, cf):
    return _mm_blocks("mm_s5_y", h2, cf, grid=(1, 4, 2), out_blk=(SEQ, 128), a_blk=(SEQ, 512), a_idx=lambda i, j, k: (0, 4 * k + j),
                      b_blk=(512, 128), b_idx=lambda i, j, k: (4 * k + j, 0), dims=_NN)


def _s5_y_dx(dyc, cf):
    return _mm_blocks("mm_s5_y_dx", dyc, cf, grid=(1, 8, 1), out_blk=(SEQ, 512), a_blk=(SEQ, 128), a_idx=lambda i, j, k: (0, j % 4),
                      b_blk=(512, 128), b_idx=lambda i, j, k: (j, 0), dims=_NT)


def _s5_y_dw(h2, dyc):
    return _mm_blocks("mm_s5_y_dw", h2, dyc, grid=(8, 1, 2), out_blk=(512, 128), a_blk=(1024, 512), a_idx=lambda i, j, k: (k, i),
                      b_blk=(1024, 128), b_idx=lambda i, j, k: (k, i % 4), dims=_TN)


def _pool_dense(pw):
    eye = jnp.eye(4, dtype=pw.dtype)
    return jnp.einsum("gcd,gk->gckd", pw, eye).reshape(1024, 1024)


def _row2(v):
    return v.reshape(1, -1)


def _even_fwd(x, W, i, zero, rest_of_weights, start_ahead):
    hn = _rw_fwd("rms_fwd", _f_rms, [_cols(x)], [_par(_row2(W["norm_ab"][i]) + zero)], [(D_MODEL, BF16, 1)], 256)[0]
    z = _mm("mm_in_ab", hn, W["w_in_ab"][i])
    o = _attn_fwd("attn_self_fwd", z, z, z, out_dtype=F32, **_SELF)
    zero = rest_of_weights(o)
    pooled = _pool("pool_fwd", z, 4096, False, BF16)
    wp = _pool_dense(W["pool_w"][i])
    mixed = _mm("mm_pool", pooled, wp)
    scale = _row2(W["pool_scale"][i]) + zero
    ab = _rw_fwd("gate_ab_fwd", _f_gate_ab, [_cols(o), _cols(z, 3072, 1024), _cols(mixed), _cols(z, 5120, 1024)], [_par(scale)],
                 [(2048, BF16, 2)], 256)[0]
    x1 = _mm("mm_out_ab", ab, W["w_out_ab"][i], res=x)
    return x1, dict(x=x, hn=hn, z=z, o=o, pooled=pooled, wp=wp, mixed=mixed, ab=ab), 0.0


def _even_bwd(dx1, sv, W, G, i, send):
    x, hn, z = sv["x"], sv["hn"], sv["z"]
    dab = _mm("mm_out_ab_dx", dx1, W["w_out_ab"][i], tb=True)
    G["w_out_ab"][i] = _mm("mm_out_ab_dw", sv["ab"], dx1, ta=True, out_dtype=BF16)
    scale = _row2(W["pool_scale"][i]) + send(0)
    (do, dga, dmixed, dgb), (dscale,) = _rw_bwd(
        "gate_ab_bwd", _f_gate_ab, [_cols(sv["o"]), _cols(z, 3072, 1024), _cols(sv["mixed"]), _cols(z, 5120, 1024)], [_par(scale)],
        [_cols(dab, nsplit=2)], [(0, F32), (1, BF16), (2, BF16), (3, BF16)], [0], 256)
    G["pool_scale"][i] = dscale.reshape(-1)
    dpooled = _mm("mm_pool_dx", dmixed, sv["wp"], tb=True)
    dwp = _mm("mm_pool_dw", sv["pooled"], dmixed, ta=True, out_dtype=BF16)
    G["pool_w"][i] = jnp.stack([dwp[g * 256:(g + 1) * 256, g * 256:(g + 1) * 256] for g in range(4)])
    dvb = _pool("pool_bwd", dpooled, 0, True, BF16)
    dq, dk, dv = _attn_bwd("attn_self_bwd", z, z, z, do, out_dtype=BF16, **_SELF)
    dz = jnp.concatenate([dq, dk, dv, dga, dvb, dgb], axis=1)
    dhn = _mm("mm_in_ab_dx", dz, W["w_in_ab"][i], tb=True)
    G["w_in_ab"][i] = _mm("mm_in_ab_dw", hn, dz, ta=True, out_dtype=BF16, out_stack=W["w_in_ab"][i].shape[2])
    g = _row2(W["norm_ab"][i]) + send(1)
    (dx,), (dg,) = _rw_bwd("rms_bwd", _f_rms_res, [_cols(x)], [_par(g)], [_cols(dx1), _cols(dhn)], [(0, F32)], [0], 256)
    G["norm_ab"][i] = dg.reshape(-1)
    return dx


def _odd_fwd(x, W, i, zero, rest_of_weights, start_ahead):
    hn = _rw_fwd("rms_fwd", _f_rms, [_cols(x)], [_par(_row2(W["norm_cd"][i]) + zero)], [(D_MODEL, BF16, 1)], 256)[0]
    z = _mm("mm_in_cd", hn, W["w_in_cd"][i])
    sgu_p = [_par(_row2(W["sgu_ln_g"][i]), 4), _par(_row2(W["sgu_ln_b"][i]), 4), _par(W["sgu_w"][i], 4), _par(W["sgu_b"][i][..., None], 4)]
    c_out = _rw_fwd("sgu_fwd", _f_sgu, [_cols(z, 0, 1024, 4), _cols(z, 1024, 1024, 4), _cols(z, 2048, 1024, 4)], sgu_p, [(1024, BF16, 4)], 128)[0]
    prep_rows = [_cols(W["s5_a_re"][i]), _cols(W["s5_a_im"][i]), _cols(W["s5_log_dt"][i].reshape(S5_GROUPS, 1))]
    abar_re, abar_im, coef_re, coef_im = _rw_fwd("s5_prep_fwd", _f_s5_prep, prep_rows, [], [(S5_STATE, F32, 1)] * 4, S5_GROUPS)
    bb_rows = [_cols(W["s5_b_re"][i].reshape(2048, 16)), _cols(W["s5_b_im"][i].reshape(2048, 16)), _cols(coef_re.reshape(2048, 1)), _cols(coef_im.reshape(2048, 1))]
    bb_re, bb_im = _rw_fwd("s5_bbar_fwd", _f_bbar, bb_rows, [], [(16, F32, 1)] * 2, 256)
    bd = _bdense(bb_re, bb_im).astype(BF16)
    cf = _cdense(W["s5_c_re"][i], W["s5_c_im"][i]).astype(BF16)
    a2 = jnp.concatenate([abar_re.reshape(16, 128), abar_im.reshape(16, 128)], axis=0)
    bu = _s5_bu(z, bd)
    h3 = _scan_fwd(bu.reshape(SEQ, 32, 128), a2)
    h2 = h3.reshape(SEQ, 4096)
    yc = _s5_y(h2, cf)
    dpar = _row2(W["s5_d"][i]) + start_ahead(h3)
    yg = _rw_fwd("gelu_fwd", _f_gelu_y, [_cols(yc), _cols(z, 3072, 512)], [_par(dpar)], [(512, BF16, 1)], 256)[0]
    zero = rest_of_weights(yg)
    w12 = jnp.concatenate([W["glu_w1"][i], W["glu_w2"][i]], axis=1)
    t12 = _mm("mm_glu", yg, w12)
    d_out = _rw_fwd("glu_gate_fwd", _f_glu_gate, [_cols(t12, nsplit=2), _cols(z, 3584, 512)], [], [(512, BF16, 1)], 256)[0]
    cd = jnp.concatenate([c_out, d_out], axis=1)
    x1 = _mm("mm_out_cd", cd, W["w_out_cd"][i], res=x)
    sv = dict(x=x, hn=hn, z=z, sgu_p=sgu_p, prep_rows=prep_rows, bb_rows=bb_rows, bb=(bb_re, bb_im), bd=bd, cf=cf, a2=a2,
              h3=h3, h2=h2, yc=yc, dpar=dpar, yg=yg, w12=w12, t12=t12, cd=cd)
    return x1, sv, zero


def _odd_bwd(dx1, sv, W, G, i, send):
    x, hn, z = sv["x"], sv["hn"], sv["z"]
    dcd = _mm("mm_out_cd_dx", dx1, W["w_out_cd"][i], tb=True)
    G["w_out_cd"][i] = _mm("mm_out_cd_dw", sv["cd"], dx1, ta=True, out_dtype=BF16)
    (du, dv, dgc), (dlg, dlb, dsw, dsb) = _rw_bwd(
        "sgu_bwd", _f_sgu, [_cols(z, 0, 1024, 4), _cols(z, 1024, 1024, 4), _cols(z, 2048, 1024, 4)], sv["sgu_p"],
        [_cols(dcd, 0, 1024, 4)], [(0, BF16), (1, BF16), (2, BF16)], [0, 1, 2, 3], 128)
    G["sgu_ln_g"][i], G["sgu_ln_b"][i] = dlg.reshape(-1), dlb.reshape(-1)
    G["sgu_w"][i], G["sgu_b"][i] = dsw, dsb[..., 0]
    (dt12, dgd), _ = _rw_bwd("glu_gate_bwd", _f_glu_gate, [_cols(sv["t12"], nsplit=2), _cols(z, 3584, 512)], [], [_cols(dcd, 1024, 512)],
                             [(0, BF16), (1, BF16)], [], 256)
    dyg = _mm("mm_glu_dx", dt12, sv["w12"], tb=True)
    dw12 = _mm("mm_glu_dw", sv["yg"], dt12, ta=True, out_dtype=BF16)
    G["glu_w1"][i], G["glu_w2"][i] = dw12[:, :512], dw12[:, 512:]
    (dyc, duf1), (dd,) = _rw_bwd("gelu_bwd", _f_gelu_y, [_cols(sv["yc"]), _cols(z, 3072, 512)], [_par(sv["dpar"])], [_cols(dyg)],
                                 [(0, BF16), (1, F32)], [0], 256)
    G["s5_d"][i] = dd.reshape(-1)
    dh2 = _s5_y_dx(dyc, sv["cf"])
    dcf = _s5_y_dw(sv["h2"], dyc)
    _, cvjp = jax.vjp(_cdense, W["s5_c_re"][i], W["s5_c_im"][i])
    G["s5_c_re"][i], G["s5_c_im"][i] = cvjp(dcf)
    dbu3, da2 = _scan_bwd(dh2.reshape(SEQ, 32, 128), sv["h3"], sv["a2"])
    dbu = dbu3.reshape(SEQ, 4096)
    duf2 = _s5_bu_dx(dbu, sv["bd"])
    dbd = _s5_bu_dw(z, dbu)
    _, bvjp = jax.vjp(_bdense, *sv["bb"])
    dbb_re, dbb_im = bvjp(dbd)
    (dbr, dbi, dcr, dci), _ = _rw_bwd("s5_bbar_bwd", _f_bbar, sv["bb_rows"], [], [_cols(dbb_re), _cols(dbb_im)],
                                      [(0, F32), (1, F32), (2, F32), (3, F32)], [], 256)
    G["s5_b_re"][i], G["s5_b_im"][i] = dbr.reshape(S5_GROUPS, S5_STATE, S5_GROUP_DIM), dbi.reshape(S5_GROUPS, S5_STATE, S5_GROUP_DIM)
    douts = [_cols(da2[0:16].reshape(S5_GROUPS, S5_STATE)), _cols(da2[16:32].reshape(S5_GROUPS, S5_STATE)),
             _cols(dcr.reshape(S5_GROUPS, S5_STATE)), _cols(dci.reshape(S5_GROUPS, S5_STATE))]
    (dar, dai, dldt), _ = _rw_bwd("s5_prep_bwd", _f_s5_prep, sv["prep_rows"], [], douts, [(0, F32), (1, F32), (2, F32)], [], S5_GROUPS)
    G["s5_a_re"][i], G["s5_a_im"][i], G["s5_log_dt"][i] = dar, dai, dldt.reshape(-1)
    dxd = (duf1 + duf2).astype(BF16)
    dz = jnp.concatenate([du, dv, dgc, dxd, dgd], axis=1)
    dhn = _mm("mm_in_cd_dx", dz, W["w_in_cd"][i], tb=True)
    G["w_in_cd"][i] = _mm("mm_in_cd_dw", hn, dz, ta=True, out_dtype=BF16, out_stack=W["w_in_cd"][i].shape[2])
    g = _row2(W["norm_cd"][i]) + send(0)
    (dx,), (dg,) = _rw_bwd("rms_bwd", _f_rms_res, [_cols(x)], [_par(g)], [_cols(dx1), _cols(dhn)], [(0, F32)], [0], 256)
    G["norm_cd"][i] = dg.reshape(-1)
    return dx


def _cross_fwd(x1, mem_n, W, l, zero):
    hx = _rw_fwd("rms_fwd", _f_rms, [_cols(x1)], [_par(_row2(W["norm_x"][l]) + zero)], [(D_MODEL, BF16, 1)], 256)[0]
    qx = _mm("mm_xq", hx, W["w_xq"][l], out_dtype=BF16)
    kv = _mm("mm_xkv", mem_n, W["w_xkv"][l], out_dtype=BF16)
    ox = _attn_fwd("attn_cross_fwd", qx, kv, kv, out_dtype=BF16, **_CROSS)
    x2 = _mm("mm_xo", ox, W["w_xo"][l], res=x1)
    return x2, dict(x1=x1, hx=hx, qx=qx, kv=kv, ox=ox)


def _cross_bwd(dx2, dmem_n, sv, mem_n, W, G, l, zero):
    dox = _mm("mm_xo_dx", dx2, W["w_xo"][l], tb=True, out_dtype=BF16)
    G["w_xo"][l] = _mm("mm_xo_dw", sv["ox"], dx2, ta=True, out_dtype=BF16)
    dqx, dk, dv = _attn_bwd("attn_cross_bwd", sv["qx"], sv["kv"], sv["kv"], dox, out_dtype=BF16, **_CROSS)
    dkv = jnp.concatenate([dk, dv], axis=1)
    dhx = _mm("mm_xq_dx", dqx, W["w_xq"][l], tb=True)
    G["w_xq"][l] = _mm("mm_xq_dw", sv["hx"], dqx, ta=True, out_dtype=BF16)
    dmem_n = _mm("mm_xkv_dx", dkv, W["w_xkv"][l], tb=True, res=dmem_n)
    G["w_xkv"][l] = _mm("mm_xkv_dw", mem_n, dkv, ta=True, out_dtype=BF16, out_stack=W["w_xkv"][l].shape[2])
    (dx1,), (dg,) = _rw_bwd("rms_bwd", _f_rms_res, [_cols(sv["x1"])], [_par(_row2(W["norm_x"][l]) + zero)], [_cols(dx2), _cols(dhx)], [(0, F32)], [0], 256)
    G["norm_x"][l] = dg.reshape(-1)
    return dx1, dmem_n


_PER_LAYER = ("pool_scale", "norm_ab", "norm_cd", "sgu_ln_g", "sgu_ln_b", "sgu_w", "sgu_b", "s5_d", "s5_c_re", "s5_c_im", "s5_b_re", "s5_b_im",
              "s5_a_re", "s5_a_im", "s5_log_dt", "w_in_ab", "pool_w", "w_out_ab", "w_in_cd", "glu_w1", "glu_w2", "w_out_cd")


def _local_step(x, mem, target, W, weights_of, start_ahead, send_grads, after_layer):
    G = {k: [None, None] for k in _PER_LAYER}
    for k in ("norm_x", "w_xq", "w_xkv", "w_xo"):
        G[k] = [None] * DEPTH
    mem_rows = [_cols(mem)]
    mem_par = [_par(_row2(W["mem_norm"]))]
    mem_n = _rw_fwd("rms_fwd_mem", _f_rms, mem_rows, mem_par, [(D_MODEL, BF16, 1)], 256)[0]
    saved = []
    for layer in range(DEPTH):
        zero = weights_of(layer, 0, x if layer else mem_n)
        mixer = _even_fwd if layer % 2 == 0 else _odd_fwd
        x, sv, zero = mixer(x, W, layer // 2, zero, functools.partial(weights_of, layer, 1), functools.partial(start_ahead, layer))
        x, svx = _cross_fwd(x, mem_n, W, layer, zero)
        saved.append((sv, svx))
    loss, dx, dfinal = _loss_head(x, target, _row2(W["final_norm"]))
    G["final_norm"] = dfinal.reshape(-1)
    dmem_n, zero = None, 0.0
    for layer in reversed(range(DEPTH)):
        sv, svx = saved[layer]
        dx, dmem_n = _cross_bwd(dx, dmem_n, svx, mem_n, W, G, layer, zero)
        hook = functools.partial(send_grads, layer, G)
        dx = _even_bwd(dx, sv, W, G, layer // 2, hook) if layer % 2 == 0 else _odd_bwd(dx, sv, W, G, layer // 2, hook)
        zero = after_layer(layer, G)
    _, (dmn,) = _rw_bwd("rms_bwd_mem", _f_rms, mem_rows, mem_par, [_cols(dmem_n)], [], [0], 256)
    G["mem_norm"] = dmn.reshape(-1)
    return loss, dx, G


_HBM = pl.BlockSpec(memory_space=pltpu.HBM)
_ANY = pl.BlockSpec(memory_space=pl.ANY)
_SEM = pl.BlockSpec(memory_space=pltpu.SEMAPHORE)
_N_PEERS = N_DEV - 1


def _mesh_pos():
    return lax.axis_index("x"), lax.axis_index("y"), lax.axis_index("c")


def _peer(pos, k):
    x, y, c = pos
    return (x ^ ((k >> 2) & 1), y ^ ((k >> 1) & 1), c ^ (k & 1))


def _lin(pos):
    return 4 * pos[0] + 2 * pos[1] + pos[2]


def _ends(gather, srcs, lands, t, sender, receiver):
    if gather:
        return lands[t].at[sender], lands[t].at[sender]
    whole = len(srcs[t].shape) != len(lands[t].shape)
    return (srcs[t] if whole else srcs[t].at[receiver]), lands[t].at[sender]


def _into_slot(name, b2, r0, r, me, dtype, after):
    c = b2.shape[1]
    tr = _row_block(r, c, 2 << 20)
    assert r0 % tr == 0

    def body(me_ref, x_ref, *rest):
        rest[-1][...] = x_ref[...].astype(dtype)

    extra = [] if after is None else [after]
    grid_spec = pltpu.PrefetchScalarGridSpec(
        num_scalar_prefetch=1, grid=(r // tr,),
        in_specs=[pl.BlockSpec((tr, c), lambda i, me, o=r0 // tr: (o + i, 0))] + [_ANY] * len(extra),
        out_specs=pl.BlockSpec((None, tr, c), lambda i, me: (me[0], i, 0)))
    return pl.pallas_call(
        body, name=name, grid_spec=grid_spec, out_shape=jax.ShapeDtypeStruct((N_DEV, r, c), dtype),
        compiler_params=pltpu.CompilerParams(dimension_semantics=("arbitrary",), vmem_limit_bytes=V7X_VMEM_LIMIT_BYTES),
        interpret=False,
    )(me, b2, *extra)


def _exchange_start(name, gather, srcs, lands, after=None):
    ns, nt = len(srcs), len(lands)
    arrs = list(srcs) + list(lands)
    extra = [] if after is None else [after]

    def body(*refs):
        ins, lnd = refs[:ns], refs[ns:ns + nt]
        refs = refs[len(extra):]
        send_sems, recv_sems = refs[ns + nt], refs[ns + nt + 1]
        token = refs[-1]
        pos = _mesh_pos()
        me = _lin(pos)
        for k in range(1, N_DEV):
            peer = _peer(pos, k)
            for t in range(nt):
                src, dst = _ends(gather, ins, lnd, t, me, _lin(peer))
                pltpu.make_async_remote_copy(
                    src_ref=src, dst_ref=dst, send_sem=send_sems.at[t * _N_PEERS + k - 1], recv_sem=recv_sems.at[t * _N_PEERS + k - 1],
                    device_id=peer, device_id_type=pl.DeviceIdType.MESH).start()
        token[...] = jnp.zeros_like(token)

    out = pl.pallas_call(
        body, name=name,
        out_shape=(pltpu.SemaphoreType.DMA((nt * _N_PEERS,)), pltpu.SemaphoreType.DMA((nt * _N_PEERS,)), *[pltpu.HBM(a.shape, a.dtype) for a in arrs],
                   jax.ShapeDtypeStruct((8, 128), F32)),
        in_specs=[_HBM] * (ns + nt) + [_ANY] * len(extra), out_specs=(_SEM, _SEM, *[_HBM] * (ns + nt), pl.BlockSpec(memory_space=pltpu.VMEM)),
        input_output_aliases={j: 2 + j for j in range(ns + nt)},
        compiler_params=pltpu.CompilerParams(has_side_effects=pltpu.SideEffectType.DATAFLOW_SIDE_EFFECTING),
        interpret=False,
    )(*[pltpu.with_memory_space_constraint(a, pltpu.HBM) for a in arrs], *extra)
    return dict(send=out[0], recv=out[1], srcs=list(out[2:2 + ns]), lands=list(out[2 + ns:2 + ns + nt]), token=out[-1][0, 0], token_arr=out[-1], gather=gather)


def _exchange_wait(name, ex, after):
    ns, nt = len(ex["srcs"]), len(ex["lands"])
    gather = ex["gather"]
    arrs = ex["srcs"] + ex["lands"]
    after = list(after) if isinstance(after, (list, tuple)) else [after]

    def body(*refs):
        ins, lnd = refs[:ns], refs[ns:ns + nt]
        send_sems, recv_sems = refs[ns + nt], refs[ns + nt + 1]
        pos = _mesh_pos()
        me = _lin(pos)
        for k in range(1, N_DEV):
            peer = _peer(pos, k)
            for t in range(nt):
                src, _ = _ends(gather, ins, lnd, t, me, _lin(peer))
                _, dst = _ends(gather, ins, lnd, t, _lin(peer), me)
                cp = pltpu.make_async_remote_copy(
                    src_ref=src, dst_ref=dst, send_sem=send_sems.at[t * _N_PEERS + k - 1], recv_sem=recv_sems.at[t * _N_PEERS + k - 1],
                    device_id=peer, device_id_type=pl.DeviceIdType.MESH)
                cp.wait_send()
                cp.wait_recv()

    out = pl.pallas_call(
        body, name=name, out_shape=tuple(pltpu.HBM(a.shape, a.dtype) for a in arrs),
        in_specs=[_HBM] * (ns + nt) + [_SEM, _SEM] + [_ANY] * len(after), out_specs=tuple([_HBM] * (ns + nt)),
        input_output_aliases={j: j for j in range(ns + nt)},
        compiler_params=pltpu.CompilerParams(has_side_effects=pltpu.SideEffectType.DATAFLOW_SIDE_EFFECTING),
        interpret=False,
    )(*arrs, ex["send"], ex["recv"], *after)
    return list(out[:ns]), list(out[ns:])


def _scatter_begin(name, srcs):
    lands = [lax.empty(s.shape if s.ndim == 3 else (N_DEV,) + s.shape, s.dtype) for s in srcs]
    return _exchange_start(name, False, srcs, lands)


def _adam(name, w, m, v, parts, own, me, layer, bufs):
    r, c = parts.shape[1:]
    tr = _row_block(r, max(c, 128), 1 << 20)
    nb = r // tr

    def body(me_ref, w_ref, m_ref, v_ref, p_ref, own_ref, *rest):
        g_ref, d_ref, nm_ref, nv_ref, acc = rest[-5:]
        acc[...] = jnp.zeros_like(acc)
        for k in range(N_DEV):
            @pl.when(me_ref[0] == k)
            def _():
                acc[...] += own_ref[...].astype(F32)

            @pl.when(me_ref[0] != k)
            def _(k=k):
                acc[...] += p_ref[k].astype(F32)

        g = acc[...]
        mm = ADAM_B1 * m_ref[...] + (1.0 - ADAM_B1) * g
        vv = ADAM_B2 * v_ref[...] + (1.0 - ADAM_B2) * jnp.square(g)
        m_hat = mm / (1.0 - ADAM_B1 ** ADAM_STEP)
        v_hat = vv / (1.0 - ADAM_B2 ** ADAM_STEP)
        g_ref[...] = g
        d_ref[...] = -ADAM_LR * (m_hat / (jnp.sqrt(v_hat) + ADAM_EPS) + ADAM_WD * w_ref[...])
        nm_ref[...] = mm
        nv_ref[...] = vv

    blk = pl.BlockSpec((tr, c), lambda i, me, o=layer * nb: (o + i, 0))
    own_spec = pl.BlockSpec((None, tr, c), lambda i, me: (me[0], i, 0)) if own.ndim == 3 else pl.BlockSpec((tr, c), lambda i, me: (i, 0))
    in_specs = [blk, blk, blk, pl.BlockSpec((N_DEV, tr, c), lambda i, me: (0, i, 0)), own_spec]
    args = [me, w, m, v, parts, own]
    aliases = {}
    if bufs is not None:
        in_specs += [_ANY] * 4
        aliases = {len(args) + j: j for j in range(4)}
        args += list(bufs)
    grid_spec = pltpu.PrefetchScalarGridSpec(
        num_scalar_prefetch=1, grid=(nb,), in_specs=in_specs, out_specs=[blk] * 4, scratch_shapes=[pltpu.VMEM((tr, c), F32)])
    return pl.pallas_call(
        body, name=name, grid_spec=grid_spec, out_shape=[jax.ShapeDtypeStruct(w.shape, F32)] * 4, input_output_aliases=aliases,
        compiler_params=pltpu.CompilerParams(dimension_semantics=("arbitrary",), vmem_limit_bytes=V7X_VMEM_LIMIT_BYTES),
        interpret=False,
    )(*args)


def _row_block(r, c, limit):
    best = None
    for tr in range(16, r + 1, 16):
        if r % tr == 0 and tr * c * 4 <= limit:
            best = tr
    return r if best is None else best


_BIG = ("w_in_ab", "pool_w", "w_out_ab", "w_in_cd", "glu_w1", "glu_w2", "w_out_cd", "w_xq", "w_xkv", "w_xo")
_STACKED = ("w_in_ab", "w_in_cd", "w_xkv")
_MIXER_BIG = (("w_in_ab", "pool_w", "w_out_ab"), ("w_in_cd", "glu_w1", "glu_w2", "w_out_cd"))
_CROSS_BIG = ("w_xq", "w_xkv", "w_xo")
_SMALL_SPLIT = ["norm_cd", "sgu_ln_g", "sgu_ln_b", "s5_d"]
_REPLICATED_ODD = ["sgu_w", "sgu_b", "s5_a_re", "s5_a_im", "s5_log_dt", "s5_b_re", "s5_b_im", "s5_c_re", "s5_c_im", "final_norm"]
_REPLICATED_EVEN = ["norm_ab", "pool_scale", "norm_x", "mem_norm"]
_REPLICATED = _REPLICATED_ODD + _REPLICATED_EVEN
_WEIGHTS = ["norm_ab", "w_in_ab", "pool_w", "pool_scale", "w_out_ab", "norm_cd", "w_in_cd", "sgu_ln_g", "sgu_ln_b", "sgu_w", "sgu_b", "s5_a_re",
            "s5_a_im", "s5_log_dt", "s5_b_re", "s5_b_im", "s5_c_re", "s5_c_im", "s5_d", "glu_w1", "glu_w2", "w_out_cd", "norm_x", "w_xq",
            "w_xkv", "w_xo", "mem_norm", "final_norm"]


def _layer_big(layer):
    return [(n, layer // 2) for n in _MIXER_BIG[layer % 2]] + [(n, layer) for n in _CROSS_BIG]


def _gather_parts(layer):
    big = _layer_big(layer)
    return [big[:1], big[1:]]


def _scatter_parts(layer):
    big = _layer_big(layer)
    return [big[2:], big[:2]] if layer % 2 == 0 else [big]


def _from_slots(name, a):
    if name in _STACKED:
        return a
    if name == "pool_w":
        return a.reshape(N_DEV, 4, 32, 256).transpose(1, 0, 2, 3).reshape(4, 256, 256)
    return a.reshape(-1, a.shape[-1])


def _to_slots(name, g):
    if name in _STACKED:
        return g
    if name == "pool_w":
        return g.reshape(4, N_DEV, 32, 256).transpose(1, 0, 2, 3).reshape(N_DEV, 128, 256)
    return g.reshape(N_DEV, -1, g.shape[-1])


def _rows2d(a):
    return a.reshape(-1, a.shape[-1])


def _small_rows(block):
    return jnp.pad(block, ((0, 0), (0, 128 - block.shape[1])))


def kernel(x, mem, norm_ab, w_in_ab, pool_w, pool_scale, w_out_ab, norm_cd, w_in_cd, sgu_ln_g, sgu_ln_b, sgu_w, sgu_b, s5_a_re, s5_a_im, s5_log_dt, s5_b_re, s5_b_im, s5_c_re, s5_c_im, s5_d, glu_w1, glu_w2, w_out_cd, norm_x, w_xq, w_xkv, w_xo, mem_norm, final_norm, loss_target, m_norm_ab, m_w_in_ab, m_pool_w, m_pool_scale, m_w_out_ab, m_norm_cd, m_w_in_cd, m_sgu_ln_g, m_sgu_ln_b, m_sgu_w, m_sgu_b, m_s5_a_re, m_s5_a_im, m_s5_log_dt, m_s5_b_re, m_s5_b_im, m_s5_c_re, m_s5_c_im, m_s5_d, m_glu_w1, m_glu_w2, m_w_out_cd, m_norm_x, m_w_xq, m_w_xkv, m_w_xo, m_mem_norm, m_final_norm, v_norm_ab, v_w_in_ab, v_pool_w, v_pool_scale, v_w_out_ab, v_norm_cd, v_w_in_cd, v_sgu_ln_g, v_sgu_ln_b, v_sgu_w, v_sgu_b, v_s5_a_re, v_s5_a_im, v_s5_log_dt, v_s5_b_re, v_s5_b_im, v_s5_c_re, v_s5_c_im, v_s5_d, v_glu_w1, v_glu_w2, v_w_out_cd, v_norm_x, v_w_xq, v_w_xkv, v_w_xo, v_mem_norm, v_final_norm):
    args = locals()
    w = {n: args[n] for n in _WEIGHTS}
    m = {n: args["m_" + n] for n in _WEIGHTS}
    v = {n: args["v_" + n] for n in _WEIGHTS}

    me = jnp.reshape(_lin(_mesh_pos()), (1,)).astype(jnp.int32)

    order = [(layer, p) for layer in range(DEPTH) for p in range(len(_gather_parts(layer)))]
    lands, gathers = {}, {}
    for key in order:
        lands[key] = []
        for name, i in _gather_parts(key[0])[key[1]]:
            b2 = _rows2d(w[name])
            r = b2.shape[0] // w[name].shape[0]
            lands[key].append(_into_slot("cast_" + name, b2, i * r, r, me, BF16, None))
    small_blocks = jnp.concatenate([_small_rows(w[n]) for n in _SMALL_SPLIT], axis=0)
    lands[order[0]].append(_into_slot("cast_small", small_blocks, 0, 8, me, F32, None))

    def begin_gather(key, after):
        gathers[key] = _exchange_start("gather%d%s_start" % (key[0], "ab"[key[1]]), True, [], lands[key], after)
        return gathers[key]["token"]

    W = {n: w[n] for n in _REPLICATED}
    W["mem_norm"] = w["mem_norm"] + begin_gather(order[0], None)
    for name in _BIG:
        W[name] = [None] * w[name].shape[0]

    def weights_of(layer, part, after):
        key = (layer, part)
        if key not in gathers:
            return 0.0
        _, got = _exchange_wait("gather%d%s_wait" % (layer, "ab"[part]), gathers[key], after)
        for (name, i), arr in zip(_gather_parts(layer)[part], got):
            W[name][i] = _from_slots(name, arr)
        if key == order[0]:
            sm = got[-1].reshape(N_DEV, 4, 2, 128)
            for j, n in enumerate(_SMALL_SPLIT):
                width = w[n].shape[1]
                W[n] = sm[:, j, :, :width].transpose(1, 0, 2).reshape(2, N_DEV * width)
        nxt = order.index(key) + 1
        return begin_gather(order[nxt], got[0]) if nxt < len(order) and order[nxt] not in gathers else 0.0

    def start_ahead(layer, after):
        return begin_gather((layer + 1, 0), after) if layer + 1 < DEPTH else 0.0

    scatters, small = {}, {}

    def send_grads(layer, G, part):
        srcs = [_to_slots(name, G[name][i]) for name, i in _scatter_parts(layer)[part]]
        scatters[layer, part] = _scatter_begin("scatter%d%s_start" % (layer, "ab"[part]), srcs)
        return scatters[layer, part]["token"]

    def rows(a):
        if a.ndim == 1:
            return a.reshape(1, -1)
        return a.reshape(-1, 128) if a.shape[-1] < 128 and a.size % 128 == 0 else a.reshape(-1, a.shape[-1])

    def begin_small(tag, G, names, split):
        srcs = [rows(G[n] if n in ("mem_norm", "final_norm") else jnp.stack(G[n])) for n in names]
        if split:
            srcs += [jnp.stack(G[n]).reshape(2, N_DEV, -1).transpose(1, 0, 2) for n in _SMALL_SPLIT]
        small[tag] = _scatter_begin("scatter_small_%s_start" % tag, srcs)
        return small[tag]["token"]

    def after_layer(layer, G):
        return begin_small("odd", G, _REPLICATED_ODD, True) if layer == 1 else 0.0

    loss, dx, G = _local_step(x[0], mem[0], loss_target[0], W, weights_of, start_ahead, send_grads, after_layer)
    loss = lax.psum(loss[0, 0], MESH_AXES)
    begin_small("even", G, _REPLICATED_EVEN, False)

    out = {}
    after = small["even"]["token_arr"]
    for layer, part in scatters:
        own, got = _exchange_wait("scatter%d%s_wait" % (layer, "ab"[part]), scatters[layer, part], after)
        for (name, i), mine, parts in zip(_scatter_parts(layer)[part], own, got):
            out[name] = _adam("adam_" + name, _rows2d(w[name]), _rows2d(m[name]), _rows2d(v[name]), parts, mine, me, i, out.get(name))
        after = [out[name][0] for name, i in _scatter_parts(layer)[part]]
    for name in _BIG:
        out[name] = [a.reshape(w[name].shape) for a in out[name]]
    for tag, names in (("odd", _REPLICATED_ODD + _SMALL_SPLIT), ("even", _REPLICATED_EVEN)):
        own, got = _exchange_wait("scatter_small_%s_wait" % tag, small[tag], after)
        for n, mine, parts in zip(names, own, got):
            as2d = (lambda a: a) if n in _SMALL_SPLIT else rows
            res = _adam("adam_" + n, as2d(w[n]), as2d(m[n]), as2d(v[n]), parts, mine, me, 0, None)
            out[n] = [a.reshape(w[n].shape) for a in res]
        after = [out[n][0] for n in names]

    return (loss, dx[None], *[out[n][0] for n in _WEIGHTS], *[out[n][1] for n in _WEIGHTS], *[out[n][2] for n in _WEIGHTS],
            *[out[n][3] for n in _WEIGHTS])
```

```python
import functools
import math

import jax
import jax.numpy as jnp
from jax import lax
from jax.experimental import pallas as pl
from jax.experimental.pallas import tpu as pltpu

F32 = jnp.float32
BF16 = jnp.bfloat16

SEQ = 2048
D_MODEL = 1024
MEM_LEN = 256
DEPTH = 4
N_DEV = 8
EPS = 1e-6
NEG = -1e30
A_HEAD_DIM = 64
X_HEAD_DIM = 256
S5_GROUPS = 32
S5_STATE = 64
S5_GROUP_DIM = 16

ADAM_LR = 0.001
ADAM_B1 = 0.9
ADAM_B2 = 0.999
ADAM_EPS = 1e-08
ADAM_WD = 0.01
ADAM_STEP = 10

V7X_VMEM_LIMIT_BYTES = 56 * 1024 * 1024
_MM_VMEM_BYTES = 36 * 1024 * 1024
MESH_AXES = ("x", "y", "c")


def _pc(body, *, name, out_shape, grid=None, in_specs=None, out_specs=None, scratch_shapes=(), aliases=None, sem=None):
    kw = {}
    if grid is not None:
        kw["grid"] = grid
    if in_specs is not None:
        kw["in_specs"] = in_specs
    if out_specs is not None:
        kw["out_specs"] = out_specs
    if aliases:
        kw["input_output_aliases"] = aliases
    return pl.pallas_call(
        body,
        name=name,
        out_shape=out_shape,
        scratch_shapes=list(scratch_shapes),
        compiler_params=pltpu.CompilerParams(dimension_semantics=sem, vmem_limit_bytes=V7X_VMEM_LIMIT_BYTES),
        interpret=False,
        **kw,
    )


def _cols(arr, c0=0, width=None, nsplit=1, r0=0):
    width = arr.shape[1] - c0 if width is None else width
    assert c0 % width == 0 and width % nsplit == 0
    return (arr, c0, width, nsplit, r0)


def _par(arr, nsplit=1):
    return (arr, nsplit)


def _ld(ref, nsplit):
    if nsplit == 1:
        return ref[...].astype(F32)
    if len(ref.shape) == 3:
        return tuple(ref[k].astype(F32) for k in range(nsplit))
    w = ref.shape[-1] // nsplit
    return tuple(ref[:, k * w:(k + 1) * w].astype(F32) for k in range(nsplit))


def _st(ref, val, nsplit, accumulate=False):
    if nsplit == 1:
        val = (val,)
    for k in range(nsplit):
        if nsplit == 1:
            idx = (Ellipsis,)
        elif len(ref.shape) == 3:
            idx = (k,)
        else:
            w = ref.shape[-1] // nsplit
            idx = (slice(None), slice(k * w, (k + 1) * w))
        if accumulate:
            ref[idx] += val[k].astype(ref.dtype)
        else:
            ref[idx] = val[k].astype(ref.dtype)


def _row_spec(tr, op):
    _, c0, w, _, r0 = op
    assert r0 % tr == 0
    return pl.BlockSpec((tr, w), lambda i, cb=c0 // w, rb=r0 // tr: (i + rb, cb))


def _full_spec(arr):
    return pl.BlockSpec(arr.shape, lambda i, nd=arr.ndim: (0,) * nd)


def _rw_fwd(name, f, rows, pars, outs, tr, n_rows=None):
    n_rows = rows[0][0].shape[0] if n_rows is None else n_rows
    nr, npar = len(rows), len(pars)

    def body(*refs):
        r = [_ld(refs[i], rows[i][3]) for i in range(nr)]
        p = [_ld(refs[nr + i], pars[i][1]) for i in range(npar)]
        res = f(r, p)
        for k, (_, _, ns) in enumerate(outs):
            _st(refs[nr + npar + k], res[k], ns)

    res = _pc(
        body, name=name, grid=(n_rows // tr,),
        in_specs=[_row_spec(tr, op) for op in rows] + [_full_spec(a) for a, _ in pars],
        out_specs=[pl.BlockSpec((tr, w), lambda i: (i, 0)) for w, _, _ in outs],
        out_shape=[jax.ShapeDtypeStruct((n_rows, w), dt) for w, dt, _ in outs],
        sem=("arbitrary",),
    )(*[op[0] for op in rows], *[a for a, _ in pars])
    return list(res)


def _rw_bwd(name, f, rows, pars, douts, drow, dpar, tr):
    n_rows = rows[0][0].shape[0]
    nr, npar = len(rows), len(pars)
    dgiven = [d for d in douts if d is not None]
    nd = len(dgiven)

    def body(*refs):
        r = [_ld(refs[i], rows[i][3]) for i in range(nr)]
        p = [_ld(refs[nr + i], pars[i][1]) for i in range(npar)]
        d = [_ld(refs[nr + npar + i], dgiven[i][3]) for i in range(nd)]
        orefs = refs[nr + npar + nd:]

        def g(dr, dp):
            rr, pp = list(r), list(p)
            for j, (idx, _) in enumerate(drow):
                rr[idx] = dr[j]
            for j, idx in enumerate(dpar):
                pp[idx] = dp[j]
            return tuple(f(rr, pp))

        out, vjp = jax.vjp(g, [r[idx] for idx, _ in drow], [p[idx] for idx in dpar])
        ct, j = [], 0
        for k, o in enumerate(out):
            if douts[k] is None:
                ct.append(jax.tree.map(jnp.zeros_like, o))
            else:
                ct.append(d[j])
                j += 1
        gdr, gdp = vjp(tuple(ct))
        for j, (idx, _) in enumerate(drow):
            _st(orefs[j], gdr[j], rows[idx][3])

        @pl.when(pl.program_id(0) == 0)
        def _():
            for j in range(len(dpar)):
                orefs[len(drow) + j][...] = jnp.zeros_like(orefs[len(drow) + j])

        for j, idx in enumerate(dpar):
            _st(orefs[len(drow) + j], gdp[j], pars[idx][1], accumulate=True)

    res = _pc(
        body, name=name, grid=(n_rows // tr,),
        in_specs=[_row_spec(tr, op) for op in rows] + [_full_spec(a) for a, _ in pars] + [_row_spec(tr, op) for op in dgiven],
        out_specs=[pl.BlockSpec((tr, rows[idx][2]), lambda i: (i, 0)) for idx, _ in drow] + [_full_spec(pars[idx][0]) for idx in dpar],
        out_shape=[jax.ShapeDtypeStruct((n_rows, rows[idx][2]), dt) for idx, dt in drow]
        + [jax.ShapeDtypeStruct(pars[idx][0].shape, F32) for idx in dpar],
        sem=("arbitrary",),
    )(*[op[0] for op in rows], *[a for a, _ in pars], *[op[0] for op in dgiven])
    res = list(res)
    return res[:len(drow)], res[len(drow):]


def _sigmoid(x):
    return jax.nn.sigmoid(x)


def _silu(x):
    return x * _sigmoid(x)


def _rms(x, g):
    return x * lax.rsqrt(jnp.mean(x * x, axis=-1, keepdims=True) + EPS) * g


def _f_rms(r, p):
    return [_rms(r[0], p[0])]


def _f_rms_res(r, p):
    return [r[0], _rms(r[0], p[0])]


def _f_gate_ab(r, p):
    o, ga, mixed, gb = r
    return [(o * _silu(ga), mixed * p[0] * _silu(gb))]


def _f_sgu(r, p):
    u, v, gc = r
    lg, lb, w, b = p
    n = float(D_MODEL)
    mu = sum(jnp.sum(vk, axis=-1, keepdims=True) for vk in v) / n
    var = sum(jnp.sum(jnp.square(vk - mu), axis=-1, keepdims=True) for vk in v) / n
    rs = lax.rsqrt(var + EPS)
    t = w[0].shape[0]
    tri = lax.broadcasted_iota(jnp.int32, (t, t), 0) >= lax.broadcasted_iota(jnp.int32, (t, t), 1)
    outs = []
    for k in range(len(v)):
        vn = (v[k] - mu) * rs * lg[k] + lb[k]
        mixed = jnp.dot(jnp.where(tri, w[k], 0.0), vn, preferred_element_type=F32) + b[k]
        outs.append(u[k] * mixed * _silu(gc[k]))
    return [tuple(outs)]


def _gelu(x):
    return 0.5 * x * (1.0 + jnp.tanh(math.sqrt(2.0 / math.pi) * (x + 0.044715 * (x * x * x))))


def _f_gelu_y(r, p):
    yc, uf = r
    return [_gelu(yc + p[0] * uf)]


def _f_glu_gate(r, p):
    t12, gd = r
    return [t12[0] * _sigmoid(t12[1]) * _silu(gd)]


def _f_s5_prep(r, p):
    ar, ai, ldt = r
    dt = jnp.exp(ldt)
    mag = jnp.exp(dt * ar)
    abar_re = mag * jnp.cos(dt * ai)
    abar_im = mag * jnp.sin(dt * ai)
    nr, ni = abar_re - 1.0, abar_im
    inv = 1.0 / (ar * ar + ai * ai)
    return [abar_re, abar_im, (nr * ar + ni * ai) * inv, (ni * ar - nr * ai) * inv]


def _f_bbar(r, p):
    br, bi, cr, ci = r
    return [cr * br - ci * bi, cr * bi + ci * br]


def _loss_head(x, target, g):
    tr = 256
    n_rows, width = x.shape

    def f(xv, gv, tv):
        err = jnp.square(_rms(xv, gv) - tv)
        return 0.5 * jnp.mean(err, axis=-1, keepdims=True)

    def body(x_ref, t_ref, g_ref, loss_ref, dx_ref, dg_ref):
        @pl.when(pl.program_id(0) == 0)
        def _():
            loss_ref[...] = jnp.zeros_like(loss_ref)
            dg_ref[...] = jnp.zeros_like(dg_ref)

        tv = t_ref[...]
        row_loss, vjp = jax.vjp(lambda a, b: f(a, b, tv), x_ref[...], g_ref[...])
        dx, dg = vjp(jnp.ones_like(row_loss))
        dx_ref[...] = dx
        dg_ref[...] += dg
        loss_ref[...] += jnp.broadcast_to(jnp.sum(row_loss, axis=0, keepdims=True), loss_ref.shape)

    blk = pl.BlockSpec((tr, width), lambda i: (i, 0))
    one = pl.BlockSpec((1, width), lambda i: (0, 0))
    return _pc(
        body, name="loss_head", grid=(n_rows // tr,), in_specs=[blk, blk, one],
        out_specs=[pl.BlockSpec((1, 128), lambda i: (0, 0)), blk, one],
        out_shape=[jax.ShapeDtypeStruct((1, 128), F32), jax.ShapeDtypeStruct(x.shape, F32), jax.ShapeDtypeStruct((1, width), F32)],
        sem=("arbitrary",),
    )(x, target, g)


_NT = (((1,), (1,)), ((), ()))
_TN = (((0,), (0,)), ((), ()))


def _tile(n, cap):
    t = min(n, cap)
    while n % t:
        t -= 128
    assert t > 0
    return t


def _mm(name, a, b, *, ta=False, tb=False, out_dtype=F32, a_off=0, a_width=None, res=None, out_stack=None):
    assert not (ta and tb)
    stacked = b.ndim == 3
    bk, bn = (b.shape[1], b.shape[0] * b.shape[2]) if stacked else b.shape
    if ta:
        kc = a.shape[0]
        m = a.shape[1] - a_off if a_width is None else a_width
        n = bn
        assert bk == kc and not stacked
    else:
        m = a.shape[0]
        kc = a.shape[1] - a_off if a_width is None else a_width
        n = bk if tb else bn
        assert (bn if tb else bk) == kc
    tn = _tile(b.shape[2] if stacked and not tb else (out_stack or n), 1024)
    size = lambda dt: jnp.dtype(dt).itemsize
    for tk_cap, tm_cap in ((2048, 2048), (2048, 1024), (1024, 1024), (1024, 512), (1024, 256)):
        tm, tk = _tile(m, tm_cap), _tile(b.shape[2] if stacked and tb else kc, tk_cap)
        nk = kc // tk
        vmem = 2 * tm * tk * size(a.dtype) + 2 * tk * tn * size(b.dtype) + tm * tn * (2 * size(out_dtype) + (4 if nk > 1 else 0) + (8 if res is not None else 0))
        if vmem <= _MM_VMEM_BYTES:
            break
    if ta:
        assert a_off % tm == 0
        a_spec = pl.BlockSpec((tk, tm), lambda i, j, k, o=a_off // tm: (k, i + o))
        dims = _TN
    else:
        assert a_off % tk == 0
        a_spec = pl.BlockSpec((tm, tk), lambda i, j, k, o=a_off // tk: (i, k + o))
        dims = _NT if tb else (((1,), (0,)), ((), ()))
    if stacked and tb:
        b_spec = pl.BlockSpec((None, tn, tk), lambda i, j, k, q=b.shape[2] // tk: (k // q, j, k % q))
    elif stacked:
        b_spec = pl.BlockSpec((None, tk, tn), lambda i, j, k, q=b.shape[2] // tn: (j // q, k, j % q))
    else:
        b_spec = pl.BlockSpec((tn, tk), lambda i, j, k: (j, k)) if tb else pl.BlockSpec((tk, tn), lambda i, j, k: (k, j))
    if out_stack:
        out_spec = pl.BlockSpec((None, tm, tn), lambda i, j, k, q=out_stack // tn: (j // q, i, j % q))
        out_shape = jax.ShapeDtypeStruct((n // out_stack, m, out_stack), out_dtype)
    else:
        out_spec = pl.BlockSpec((tm, tn), lambda i, j, k: (i, j))
        out_shape = jax.ShapeDtypeStruct((m, n), out_dtype)
    in_specs, args = [a_spec, b_spec], [a, b]
    has_res = res is not None
    if has_res:
        in_specs.append(pl.BlockSpec((tm, tn), lambda i, j, k: (i, j)))
        args.append(res)

    def finish(refs, acc):
        if has_res:
            acc = acc + refs[2][...].astype(F32)
        refs[3 if has_res else 2][...] = acc.astype(out_dtype)

    def body_one(*refs):
        finish(refs, lax.dot_general(refs[0][...].astype(BF16), refs[1][...].astype(BF16), dims, preferred_element_type=F32))

    def body(*refs):
        acc_ref = refs[-1]
        k = pl.program_id(2)

        @pl.when(k == 0)
        def _():
            acc_ref[...] = jnp.zeros_like(acc_ref)

        acc_ref[...] += lax.dot_general(refs[0][...].astype(BF16), refs[1][...].astype(BF16), dims, preferred_element_type=F32)

        @pl.when(k == nk - 1)
        def _():
            finish(refs, acc_ref[...])

    return _pc(
        body_one if nk == 1 else body, name=name, grid=(m // tm, n // tn, nk), in_specs=in_specs, out_specs=out_spec, out_shape=out_shape,
        scratch_shapes=[] if nk == 1 else [pltpu.VMEM((tm, tn), F32)], sem=("parallel", "parallel", "arbitrary"),
    )(*args)


def _mm_blocks(name, a, b, *, grid, out_blk, a_blk, a_idx, b_blk, b_idx, dims, out_dtype=F32):
    gi, gj, nk = grid

    def body(a_ref, b_ref, o_ref, acc_ref):
        k = pl.program_id(2)

        @pl.when(k == 0)
        def _():
            acc_ref[...] = jnp.zeros_like(acc_ref)

        acc_ref[...] += lax.dot_general(a_ref[...].astype(BF16), b_ref[...].astype(BF16), dims, preferred_element_type=F32)

        @pl.when(k == nk - 1)
        def _():
            o_ref[...] = acc_ref[...].astype(o_ref.dtype)

    return _pc(
        body, name=name, grid=grid, in_specs=[pl.BlockSpec(a_blk, a_idx), pl.BlockSpec(b_blk, b_idx)],
        out_specs=pl.BlockSpec(out_blk, lambda i, j, k: (i, j)), out_shape=jax.ShapeDtypeStruct((gi * out_blk[0], gj * out_blk[1]), out_dtype),
        scratch_shapes=[pltpu.VMEM(out_blk, F32)], sem=("parallel", "parallel", "arbitrary"),
    )(a, b)


def _head_masks(width, nsub):
    lane = lax.broadcasted_iota(jnp.int32, (1, width), 1)
    hd = width // nsub
    return [(lane >= h * hd) & (lane < (h + 1) * hd) for h in range(nsub)]


def _dilated_log_count(row0, tq, ext):
    delta = (row0 + lax.broadcasted_iota(jnp.int32, (tq, ext), 0)) - lax.broadcasted_iota(jnp.int32, (tq, ext), 1)
    cnt = (delta <= 128).astype(jnp.int32) + (((delta & 3) == 0) & (delta <= 512)).astype(jnp.int32) + ((delta & 15) == 0).astype(jnp.int32)
    logc = jnp.where(cnt == 3, math.log(3.0), jnp.where(cnt == 2, math.log(2.0), 0.0))
    return jnp.where((delta >= 0) & (cnt > 0), logc, NEG)


def _bias_table(tab, nq, tq):
    @pl.when(pl.program_id(0) == 0)
    def _():
        for d in range(nq):
            tab[d] = _dilated_log_count(d * tq, tq, tq)


def _scores(q, ke, tab, r, masks, h):
    qm = (jnp.where(masks[h], q, 0.0) if len(masks) > 1 else q).astype(BF16)
    s = lax.dot_general(qm, ke, _NT, preferred_element_type=F32)
    if tab is not None:
        s = s + jnp.concatenate([tab[r - c] for c in range(r + 1)], axis=1)
    return qm, s


def _head_value(masks, h, tile, reduce, fill):
    return reduce(jnp.where(masks[h], tile, fill) if len(masks) > 1 else tile, axis=-1, keepdims=True)


def _attn_fwd(name, qa, ka, va, *, qc, kc, vc, width, nblk, nsub, causal, tq, scale, out_dtype):
    sq, t_len = qa.shape[0], ka.shape[0]
    nq = sq // tq

    def body(q_ref, k_ref, v_ref, o_ref, lse_ref, *scratch):
        tab = scratch[0] if causal else None
        if causal:
            _bias_table(tab, nq, tq)
        kb = k_ref[...].astype(BF16)
        vb = v_ref[...].astype(BF16)
        masks = _head_masks(width, nsub)
        for r in range(nq):
            ext = (r + 1) * tq if causal else t_len
            q = q_ref[r * tq:(r + 1) * tq, :].astype(F32) * scale
            ke, ve = kb[:ext], vb[:ext]
            o = lse = None
            for h in range(nsub):
                _, s = _scores(q, ke, tab, r, masks, h)
                m = jnp.max(s, axis=-1, keepdims=True)
                p = jnp.exp(s - m)
                l = jnp.sum(p, axis=-1, keepdims=True)
                oh = jnp.dot(p.astype(BF16), ve, preferred_element_type=F32) * (1.0 / l)
                lh = jnp.broadcast_to(m + jnp.log(l), (tq, width))
                o = oh if o is None else jnp.where(masks[h], oh, o)
                lse = lh if lse is None else jnp.where(masks[h], lh, lse)
            o_ref[r * tq:(r + 1) * tq, :] = o.astype(o_ref.dtype)
            lse_ref[r * tq:(r + 1) * tq, :] = lse

    blk = pl.BlockSpec((sq, width), lambda i: (0, i))
    return _pc(
        body, name=name, grid=(nblk,),
        in_specs=[pl.BlockSpec((sq, width), lambda i, c=qc: (0, c + i)), pl.BlockSpec((t_len, width), lambda i, c=kc: (0, c + i)),
                  pl.BlockSpec((t_len, width), lambda i, c=vc: (0, c + i))],
        out_specs=[blk, blk],
        out_shape=[jax.ShapeDtypeStruct((sq, nblk * width), out_dtype), jax.ShapeDtypeStruct((sq, nblk * width), F32)],
        scratch_shapes=[pltpu.VMEM((nq, tq, tq), F32)] if causal else [], sem=("arbitrary",),
    )(qa, ka, va)


def _attn_bwd(name, qa, ka, va, doa, oa, lsea, *, qc, kc, vc, width, nblk, nsub, causal, tq, scale, out_dtype):
    sq, t_len = qa.shape[0], ka.shape[0]
    nq = sq // tq

    def body(q_ref, k_ref, v_ref, do_ref, o_ref, lse_ref, dq_ref, dk_ref, dv_ref, dk_acc, dv_acc, *scratch):
        tab = scratch[0] if causal else None
        if causal:
            _bias_table(tab, nq, tq)
        kb = k_ref[...].astype(BF16)
        vb = v_ref[...].astype(BF16)
        masks = _head_masks(width, nsub)
        dk_acc[...] = jnp.zeros_like(dk_acc)
        dv_acc[...] = jnp.zeros_like(dv_acc)
        for r in range(nq):
            ext = (r + 1) * tq if causal else t_len
            q = q_ref[r * tq:(r + 1) * tq, :].astype(F32) * scale
            do = do_ref[r * tq:(r + 1) * tq, :].astype(F32)
            do_o = do * o_ref[r * tq:(r + 1) * tq, :].astype(F32)
            lse = lse_ref[r * tq:(r + 1) * tq, :]
            ke, ve = kb[:ext], vb[:ext]
            dq = None
            for h in range(nsub):
                qm, s = _scores(q, ke, tab, r, masks, h)
                dom = (jnp.where(masks[h], do, 0.0) if nsub > 1 else do).astype(BF16)
                pn = jnp.exp(s - _head_value(masks, h, lse, jnp.max, -jnp.inf))
                dpn = lax.dot_general(dom, ve, _NT, preferred_element_type=F32)
                dsb = (pn * (dpn - _head_value(masks, h, do_o, jnp.sum, 0.0))).astype(BF16)
                dqh = jnp.dot(dsb, ke, preferred_element_type=F32)
                dq = dqh if dq is None else jnp.where(masks[h], dqh, dq)
                dk_acc[0:ext, :] += lax.dot_general(dsb, qm, _TN, preferred_element_type=F32)
                dv_acc[0:ext, :] += lax.dot_general(pn.astype(BF16), dom, _TN, preferred_element_type=F32)
            dq_ref[r * tq:(r + 1) * tq, :] = (dq * scale).astype(dq_ref.dtype)
        dk_ref[...] = dk_acc[...].astype(dk_ref.dtype)
        dv_ref[...] = dv_acc[...].astype(dv_ref.dtype)

    return _pc(
        body, name=name, grid=(nblk,),
        in_specs=[pl.BlockSpec((sq, width), lambda i, c=qc: (0, c + i)), pl.BlockSpec((t_len, width), lambda i, c=kc: (0, c + i)),
                  pl.BlockSpec((t_len, width), lambda i, c=vc: (0, c + i))] + [pl.BlockSpec((sq, width), lambda i: (0, i))] * 3,
        out_specs=[pl.BlockSpec((sq, width), lambda i: (0, i)), pl.BlockSpec((t_len, width), lambda i: (0, i)), pl.BlockSpec((t_len, width), lambda i: (0, i))],
        out_shape=[jax.ShapeDtypeStruct((sq, nblk * width), out_dtype), jax.ShapeDtypeStruct((t_len, nblk * width), out_dtype),
                   jax.ShapeDtypeStruct((t_len, nblk * width), out_dtype)],
        scratch_shapes=[pltpu.VMEM((t_len, width), F32), pltpu.VMEM((t_len, width), F32)] + ([pltpu.VMEM((nq, tq, tq), F32)] if causal else []),
        sem=("arbitrary",),
    )(qa, ka, va, doa, oa, lsea)


_SELF = dict(qc=0, kc=8, vc=16, width=128, nblk=8, nsub=2, causal=True, tq=256, scale=A_HEAD_DIM ** -0.5)
_CROSS = dict(qc=0, kc=0, vc=4, width=256, nblk=4, nsub=1, causal=False, tq=512, scale=X_HEAD_DIM ** -0.5)


def _window_sum(x, g, row, backward):
    n = x.shape[0]

    def shift(y, k):
        if backward:
            return jnp.where(row < n - k, pltpu.roll(y, n - k, 0), 0.0)
        return jnp.where(row >= k, pltpu.roll(y, k, 0), 0.0)

    s2 = x + shift(x, 1)
    s4 = s2 + shift(s2, 2)
    s8 = s4 + shift(s4, 4)
    s16 = s8 + shift(s8, 8)
    return jnp.where(g == 0, s2, jnp.where(g == 1, s4, jnp.where(g == 2, s8, s16)))


def _pool(name, arr, c0, backward, out_dtype):
    n = arr.shape[0]
    gw = 256

    def body(v_ref, o_ref):
        g = pl.program_id(0)
        v = v_ref[...].astype(F32)
        row = lax.broadcasted_iota(jnp.int32, v.shape, 0)
        w = jnp.where(g == 0, 2, jnp.where(g == 1, 4, jnp.where(g == 2, 8, 16)))
        cnt = jnp.minimum(row + 1, w).astype(F32)
        if backward:
            o_ref[...] = (_window_sum(v / cnt, g, row, True) - v).astype(o_ref.dtype)
        else:
            o_ref[...] = (_window_sum(v, g, row, False) / cnt - v).astype(o_ref.dtype)

    return _pc(
        body, name=name, grid=(4,), in_specs=[pl.BlockSpec((n, gw), lambda i, c=c0 // gw: (0, c + i))],
        out_specs=pl.BlockSpec((n, gw), lambda i: (0, i)), out_shape=jax.ShapeDtypeStruct((n, 4 * gw), out_dtype), sem=("parallel",),
    )(arr)


_SCAN_ROWS = 256


def _scan_fwd(bu3, a2):
    n = bu3.shape[0]

    def body(bu_ref, a_ref, h_ref, carry):
        @pl.when(pl.program_id(0) == 0)
        def _():
            carry[...] = jnp.zeros_like(carry)

        ar, ai = a_ref[0:16, :], a_ref[16:32, :]

        def step(t, c):
            hr, hi = c
            nr = ar * hr - ai * hi + bu_ref[t, 0:16, :]
            ni = ar * hi + ai * hr + bu_ref[t, 16:32, :]
            h_ref[t, 0:16, :] = nr
            h_ref[t, 16:32, :] = ni
            return nr, ni

        hr, hi = lax.fori_loop(0, _SCAN_ROWS, step, (carry[0:16, :], carry[16:32, :]), unroll=8)
        carry[0:16, :] = hr
        carry[16:32, :] = hi

    blk = pl.BlockSpec((_SCAN_ROWS, 32, 128), lambda i: (i, 0, 0))
    return _pc(
        body, name="s5_scan_fwd", grid=(n // _SCAN_ROWS,), in_specs=[blk, pl.BlockSpec((32, 128), lambda i: (0, 0))], out_specs=blk,
        out_shape=jax.ShapeDtypeStruct(bu3.shape, F32), scratch_shapes=[pltpu.VMEM((32, 128), F32)], sem=("arbitrary",),
    )(bu3, a2)


def _scan_bwd(dh3, h3, a2):
    n = dh3.shape[0]
    nb = n // _SCAN_ROWS

    def body(dh_ref, h_ref, a_ref, dbu_ref, da_ref, carry):
        @pl.when(pl.program_id(0) == 0)
        def _():
            carry[...] = jnp.zeros_like(carry)
            da_ref[...] = jnp.zeros_like(da_ref)

        ar, ai = a_ref[0:16, :], a_ref[16:32, :]

        def step(tt, c):
            gr, gi, dar, dai = c
            t = _SCAN_ROWS - 1 - tt
            hr, hi = h_ref[t, 0:16, :], h_ref[t, 16:32, :]
            dar = dar + gr * hr + gi * hi
            dai = dai - gr * hi + gi * hr
            ngr = dh_ref[t, 0:16, :] + ar * gr + ai * gi
            ngi = dh_ref[t, 16:32, :] - ai * gr + ar * gi
            dbu_ref[t, 0:16, :] = ngr
            dbu_ref[t, 16:32, :] = ngi
            return ngr, ngi, dar, dai

        z = jnp.zeros((16, 128), F32)
        gr, gi, dar, dai = lax.fori_loop(0, _SCAN_ROWS, step, (carry[0:16, :], carry[16:32, :], z, z), unroll=8)
        carry[0:16, :] = gr
        carry[16:32, :] = gi
        da_ref[0:16, :] += dar
        da_ref[16:32, :] += dai

    blk = pl.BlockSpec((_SCAN_ROWS, 32, 128), lambda i: (nb - 1 - i, 0, 0))
    small = pl.BlockSpec((32, 128), lambda i: (0, 0))
    return _pc(
        body, name="s5_scan_bwd", grid=(nb,), in_specs=[blk, blk, small], out_specs=[blk, small],
        out_shape=[jax.ShapeDtypeStruct(dh3.shape, F32), jax.ShapeDtypeStruct((32, 128), F32)],
        scratch_shapes=[pltpu.VMEM((32, 128), F32)], sem=("arbitrary",),
    )(dh3, h3, a2)


def _bdense(bb_re, bb_im):
    eye = jnp.eye(8, dtype=F32)

    def one(bb):
        return jnp.einsum("sgph,gk->sghkp", bb.reshape(4, 8, S5_STATE, S5_GROUP_DIM), eye).reshape(512, 512)

    return jnp.concatenate([one(bb_re), one(bb_im)], axis=1)


def _cdense(c_re, c_im):
    eye = jnp.eye(8, dtype=F32)

    def one(cc):
        return jnp.einsum("sghp,gk->sgpkh", cc.reshape(4, 8, S5_GROUP_DIM, S5_STATE), eye).reshape(2048, 128)

    return jnp.concatenate([one(c_re), -one(c_im)], axis=0)


_NN = (((1,), (0,)), ((), ()))
_UF_BLOCK = 3072 // 128


def _s5_bu(z, bd):
    return _mm_blocks("mm_s5_bu", z, bd, grid=(1, 8, 1), out_blk=(SEQ, 512), a_blk=(SEQ, 128), a_idx=lambda i, j, k: (0, _UF_BLOCK + j % 4),
                      b_blk=(128, 512), b_idx=lambda i, j, k: (j % 4, j // 4), dims=_NN)


def _s5_bu_dx(dbu, bd):
    return _mm_blocks("mm_s5_bu_dx", dbu, bd, grid=(1, 4, 2), out_blk=(SEQ, 128), a_blk=(SEQ, 512), a_idx=lambda i, j, k: (0, 4 * k + j),
                      b_blk=(128, 512), b_idx=lambda i, j, k: (j, k), dims=_NT)


def _s5_bu_dw(z, dbu):
    return _mm_blocks("mm_s5_bu_dw", z, dbu, grid=(4, 2, 2), out_blk=(128, 512), a_blk=(1024, 128), a_idx=lambda i, j, k: (k, _UF_BLOCK + i),
                      b_blk=(1024, 512), b_idx=lambda i, j, k: (k, 4 * j + i), dims=_TN)


def _s5_y(h2, cf):
    return _mm_blocks("mm_s5_y", h2, cf, grid=(1, 4, 2), out_blk=(SEQ, 128), a_blk=(SEQ, 512), a_idx=lambda i, j, k: (0, 4 * k + j),
                      b_blk=(512, 128), b_idx=lambda i, j, k: (4 * k + j, 0), dims=_NN)


def _s5_y_dx(dyc, cf):
    return _mm_blocks("mm_s5_y_dx", dyc, cf, grid=(1, 8, 1), out_blk=(SEQ, 512), a_blk=(SEQ, 128), a_idx=lambda i, j, k: (0, j % 4),
                      b_blk=(512, 128), b_idx=lambda i, j, k: (j, 0), dims=_NT)


def _s5_y_dw(h2, dyc):
    return _mm_blocks("mm_s5_y_dw", h2, dyc, grid=(8, 1, 2), out_blk=(512, 128), a_blk=(1024, 512), a_idx=lambda i, j, k: (k, i),
                      b_blk=(1024, 128), b_idx=lambda i, j, k: (k, i % 4), dims=_TN)


def _pool_dense(pw):
    eye = jnp.eye(4, dtype=pw.dtype)
    return jnp.einsum("gcd,gk->gckd", pw, eye).reshape(1024, 1024)


def _row2(v):
    return v.reshape(1, -1)


def _even_fwd(x, W, i, zero, rest_of_weights, start_ahead):
    hn = _rw_fwd("rms_fwd", _f_rms, [_cols(x)], [_par(_row2(W["norm_ab"][i]) + zero)], [(D_MODEL, BF16, 1)], 256)[0]
    z = _mm("mm_in_ab", hn, W["w_in_ab"][i])
    o, lse = _attn_fwd("attn_self_fwd", z, z, z, out_dtype=F32, **_SELF)
    zero = rest_of_weights(o)
    pooled = _pool("pool_fwd", z, 4096, False, BF16)
    wp = _pool_dense(W["pool_w"][i])
    mixed = _mm("mm_pool", pooled, wp)
    scale = _row2(W["pool_scale"][i]) + zero
    ab = _rw_fwd("gate_ab_fwd", _f_gate_ab, [_cols(o), _cols(z, 3072, 1024), _cols(mixed), _cols(z, 5120, 1024)], [_par(scale)],
                 [(2048, BF16, 2)], 256)[0]
    x1 = _mm("mm_out_ab", ab, W["w_out_ab"][i], res=x)
    return x1, dict(x=x, hn=hn, z=z, o=o, lse=lse, pooled=pooled, wp=wp, mixed=mixed, ab=ab), 0.0


def _even_bwd(dx1, sv, W, G, i, send):
    x, hn, z = sv["x"], sv["hn"], sv["z"]
    dab = _mm("mm_out_ab_dx", dx1, W["w_out_ab"][i], tb=True)
    G["w_out_ab"][i] = _mm("mm_out_ab_dw", sv["ab"], dx1, ta=True, out_dtype=BF16)
    scale = _row2(W["pool_scale"][i]) + send(0)
    (do, dga, dmixed, dgb), (dscale,) = _rw_bwd(
        "gate_ab_bwd", _f_gate_ab, [_cols(sv["o"]), _cols(z, 3072, 1024), _cols(sv["mixed"]), _cols(z, 5120, 1024)], [_par(scale)],
        [_cols(dab, nsplit=2)], [(0, F32), (1, BF16), (2, BF16), (3, BF16)], [0], 256)
    G["pool_scale"][i] = dscale.reshape(-1)
    dpooled = _mm("mm_pool_dx", dmixed, sv["wp"], tb=True)
    dwp = _mm("mm_pool_dw", sv["pooled"], dmixed, ta=True, out_dtype=BF16)
    G["pool_w"][i] = jnp.stack([dwp[g * 256:(g + 1) * 256, g * 256:(g + 1) * 256] for g in range(4)])
    dvb = _pool("pool_bwd", dpooled, 0, True, BF16)
    dq, dk, dv = _attn_bwd("attn_self_bwd", z, z, z, do, sv["o"], sv["lse"], out_dtype=BF16, **_SELF)
    dz = jnp.concatenate([dq, dk, dv, dga, dvb, dgb], axis=1)
    dhn = _mm("mm_in_ab_dx", dz, W["w_in_ab"][i], tb=True)
    G["w_in_ab"][i] = _mm("mm_in_ab_dw", hn, dz, ta=True, out_dtype=BF16, out_stack=W["w_in_ab"][i].shape[2])
    g = _row2(W["norm_ab"][i]) + send(1)
    (dx,), (dg,) = _rw_bwd("rms_bwd", _f_rms_res, [_cols(x)], [_par(g)], [_cols(dx1), _cols(dhn)], [(0, F32)], [0], 256)
    G["norm_ab"][i] = dg.reshape(-1)
    return dx


def _odd_fwd(x, W, i, zero, rest_of_weights, start_ahead):
    hn = _rw_fwd("rms_fwd", _f_rms, [_cols(x)], [_par(_row2(W["norm_cd"][i]) + zero)], [(D_MODEL, BF16, 1)], 256)[0]
    z = _mm("mm_in_cd", hn, W["w_in_cd"][i])
    sgu_p = [_par(_row2(W["sgu_ln_g"][i]), 4), _par(_row2(W["sgu_ln_b"][i]), 4), _par(W["sgu_w"][i], 4), _par(W["sgu_b"][i][..., None], 4)]
    c_out = _rw_fwd("sgu_fwd", _f_sgu, [_cols(z, 0, 1024, 4), _cols(z, 1024, 1024, 4), _cols(z, 2048, 1024, 4)], sgu_p, [(1024, BF16, 4)], 128)[0]
    prep_rows = [_cols(W["s5_a_re"][i]), _cols(W["s5_a_im"][i]), _cols(W["s5_log_dt"][i].reshape(S5_GROUPS, 1))]
    abar_re, abar_im, coef_re, coef_im = _rw_fwd("s5_prep_fwd", _f_s5_prep, prep_rows, [], [(S5_STATE, F32, 1)] * 4, S5_GROUPS)
    bb_rows = [_cols(W["s5_b_re"][i].reshape(2048, 16)), _cols(W["s5_b_im"][i].reshape(2048, 16)), _cols(coef_re.reshape(2048, 1)), _cols(coef_im.reshape(2048, 1))]
    bb_re, bb_im = _rw_fwd("s5_bbar_fwd", _f_bbar, bb_rows, [], [(16, F32, 1)] * 2, 256)
    bd = _bdense(bb_re, bb_im).astype(BF16)
    cf = _cdense(W["s5_c_re"][i], W["s5_c_im"][i]).astype(BF16)
    a2 = jnp.concatenate([abar_re.reshape(16, 128), abar_im.reshape(16, 128)], axis=0)
    bu = _s5_bu(z, bd)
    h3 = _scan_fwd(bu.reshape(SEQ, 32, 128), a2)
    h2 = h3.reshape(SEQ, 4096)
    yc = _s5_y(h2, cf)
    dpar = _row2(W["s5_d"][i]) + start_ahead(h3)
    yg = _rw_fwd("gelu_fwd", _f_gelu_y, [_cols(yc), _cols(z, 3072, 512)], [_par(dpar)], [(512, BF16, 1)], 256)[0]
    zero = rest_of_weights(yg)
    w12 = jnp.concatenate([W["glu_w1"][i], W["glu_w2"][i]], axis=1)
    t12 = _mm("mm_glu", yg, w12)
    d_out = _rw_fwd("glu_gate_fwd", _f_glu_gate, [_cols(t12, nsplit=2), _cols(z, 3584, 512)], [], [(512, BF16, 1)], 256)[0]
    cd = jnp.concatenate([c_out, d_out], axis=1)
    x1 = _mm("mm_out_cd", cd, W["w_out_cd"][i], res=x)
    sv = dict(x=x, hn=hn, z=z, sgu_p=sgu_p, prep_rows=prep_rows, bb_rows=bb_rows, bb=(bb_re, bb_im), bd=bd, cf=cf, a2=a2,
              h3=h3, h2=h2, yc=yc, dpar=dpar, yg=yg, w12=w12, t12=t12, cd=cd)
    return x1, sv, zero


def _odd_bwd(dx1, sv, W, G, i, send):
    x, hn, z = sv["x"], sv["hn"], sv["z"]
    dcd = _mm("mm_out_cd_dx", dx1, W["w_out_cd"][i], tb=True)
    G["w_out_cd"][i] = _mm("mm_out_cd_dw", sv["cd"], dx1, ta=True, out_dtype=BF16)
    (du, dv, dgc), (dlg, dlb, dsw, dsb) = _rw_bwd(
        "sgu_bwd", _f_sgu, [_cols(z, 0, 1024, 4), _cols(z, 1024, 1024, 4), _cols(z, 2048, 1024, 4)], sv["sgu_p"],
        [_cols(dcd, 0, 1024, 4)], [(0, BF16), (1, BF16), (2, BF16)], [0, 1, 2, 3], 128)
    G["sgu_ln_g"][i], G["sgu_ln_b"][i] = dlg.reshape(-1), dlb.reshape(-1)
    G["sgu_w"][i], G["sgu_b"][i] = dsw, dsb[..., 0]
    (dt12, dgd), _ = _rw_bwd("glu_gate_bwd", _f_glu_gate, [_cols(sv["t12"], nsplit=2), _cols(z, 3584, 512)], [], [_cols(dcd, 1024, 512)],
                             [(0, BF16), (1, BF16)], [], 256)
    dyg = _mm("mm_glu_dx", dt12, sv["w12"], tb=True)
    dw12 = _mm("mm_glu_dw", sv["yg"], dt12, ta=True, out_dtype=BF16)
    G["glu_w1"][i], G["glu_w2"][i] = dw12[:, :512], dw12[:, 512:]
    (dyc, duf1), (dd,) = _rw_bwd("gelu_bwd", _f_gelu_y, [_cols(sv["yc"]), _cols(z, 3072, 512)], [_par(sv["dpar"])], [_cols(dyg)],
                                 [(0, BF16), (1, F32)], [0], 256)
    G["s5_d"][i] = dd.reshape(-1)
    dh2 = _s5_y_dx(dyc, sv["cf"])
    dcf = _s5_y_dw(sv["h2"], dyc)
    _, cvjp = jax.vjp(_cdense, W["s5_c_re"][i], W["s5_c_im"][i])
    G["s5_c_re"][i], G["s5_c_im"][i] = cvjp(dcf)
    dbu3, da2 = _scan_bwd(dh2.reshape(SEQ, 32, 128), sv["h3"], sv["a2"])
    dbu = dbu3.reshape(SEQ, 4096)
    duf2 = _s5_bu_dx(dbu, sv["bd"])
    dbd = _s5_bu_dw(z, dbu)
    _, bvjp = jax.vjp(_bdense, *sv["bb"])
    dbb_re, dbb_im = bvjp(dbd)
    (dbr, dbi, dcr, dci), _ = _rw_bwd("s5_bbar_bwd", _f_bbar, sv["bb_rows"], [], [_cols(dbb_re), _cols(dbb_im)],
                                      [(0, F32), (1, F32), (2, F32), (3, F32)], [], 256)
    G["s5_b_re"][i], G["s5_b_im"][i] = dbr.reshape(S5_GROUPS, S5_STATE, S5_GROUP_DIM), dbi.reshape(S5_GROUPS, S5_STATE, S5_GROUP_DIM)
    douts = [_cols(da2[0:16].reshape(S5_GROUPS, S5_STATE)), _cols(da2[16:32].reshape(S5_GROUPS, S5_STATE)),
             _cols(dcr.reshape(S5_GROUPS, S5_STATE)), _cols(dci.reshape(S5_GROUPS, S5_STATE))]
    (dar, dai, dldt), _ = _rw_bwd("s5_prep_bwd", _f_s5_prep, sv["prep_rows"], [], douts, [(0, F32), (1, F32), (2, F32)], [], S5_GROUPS)
    G["s5_a_re"][i], G["s5_a_im"][i], G["s5_log_dt"][i] = dar, dai, dldt.reshape(-1)
    dxd = (duf1 + duf2).astype(BF16)
    dz = jnp.concatenate([du, dv, dgc, dxd, dgd], axis=1)
    dhn = _mm("mm_in_cd_dx", dz, W["w_in_cd"][i], tb=True)
    G["w_in_cd"][i] = _mm("mm_in_cd_dw", hn, dz, ta=True, out_dtype=BF16, out_stack=W["w_in_cd"][i].shape[2])
    g = _row2(W["norm_cd"][i]) + send(0)
    (dx,), (dg,) = _rw_bwd("rms_bwd", _f_rms_res, [_cols(x)], [_par(g)], [_cols(dx1), _cols(dhn)], [(0, F32)], [0], 256)
    G["norm_cd"][i] = dg.reshape(-1)
    return dx


def _cross_fwd(x1, mem_n, W, l, zero):
    hx = _rw_fwd("rms_fwd", _f_rms, [_cols(x1)], [_par(_row2(W["norm_x"][l]) + zero)], [(D_MODEL, BF16, 1)], 256)[0]
    qx = _mm("mm_xq", hx, W["w_xq"][l], out_dtype=BF16)
    kv = _mm("mm_xkv", mem_n, W["w_xkv"][l], out_dtype=BF16)
    ox, lse = _attn_fwd("attn_cross_fwd", qx, kv, kv, out_dtype=BF16, **_CROSS)
    x2 = _mm("mm_xo", ox, W["w_xo"][l], res=x1)
    return x2, dict(x1=x1, hx=hx, qx=qx, kv=kv, ox=ox, lse=lse)


def _cross_bwd(dx2, dmem_n, sv, mem_n, W, G, l, zero):
    dox = _mm("mm_xo_dx", dx2, W["w_xo"][l], tb=True, out_dtype=BF16)
    G["w_xo"][l] = _mm("mm_xo_dw", sv["ox"], dx2, ta=True, out_dtype=BF16)
    dqx, dk, dv = _attn_bwd("attn_cross_bwd", sv["qx"], sv["kv"], sv["kv"], dox, sv["ox"], sv["lse"], out_dtype=BF16, **_CROSS)
    dkv = jnp.concatenate([dk, dv], axis=1)
    dhx = _mm("mm_xq_dx", dqx, W["w_xq"][l], tb=True)
    G["w_xq"][l] = _mm("mm_xq_dw", sv["hx"], dqx, ta=True, out_dtype=BF16)
    dmem_n = _mm("mm_xkv_dx", dkv, W["w_xkv"][l], tb=True, res=dmem_n)
    G["w_xkv"][l] = _mm("mm_xkv_dw", mem_n, dkv, ta=True, out_dtype=BF16, out_stack=W["w_xkv"][l].shape[2])
    (dx1,), (dg,) = _rw_bwd("rms_bwd", _f_rms_res, [_cols(sv["x1"])], [_par(_row2(W["norm_x"][l]) + zero)], [_cols(dx2), _cols(dhx)], [(0, F32)], [0], 256)
    G["norm_x"][l] = dg.reshape(-1)
    return dx1, dmem_n


_PER_LAYER = ("pool_scale", "norm_ab", "norm_cd", "sgu_ln_g", "sgu_ln_b", "sgu_w", "sgu_b", "s5_d", "s5_c_re", "s5_c_im", "s5_b_re", "s5_b_im",
              "s5_a_re", "s5_a_im", "s5_log_dt", "w_in_ab", "pool_w", "w_out_ab", "w_in_cd", "glu_w1", "glu_w2", "w_out_cd")


def _local_step(x, mem, target, W, weights_of, start_ahead, send_grads, after_layer):
    G = {k: [None, None] for k in _PER_LAYER}
    for k in ("norm_x", "w_xq", "w_xkv", "w_xo"):
        G[k] = [None] * DEPTH
    mem_rows = [_cols(mem)]
    mem_par = [_par(_row2(W["mem_norm"]))]
    mem_n = _rw_fwd("rms_fwd_mem", _f_rms, mem_rows, mem_par, [(D_MODEL, BF16, 1)], 256)[0]
    saved = []
    for layer in range(DEPTH):
        zero = weights_of(layer, 0, x if layer else mem_n)
        mixer = _even_fwd if layer % 2 == 0 else _odd_fwd
        x, sv, zero = mixer(x, W, layer // 2, zero, functools.partial(weights_of, layer, 1), functools.partial(start_ahead, layer))
        x, svx = _cross_fwd(x, mem_n, W, layer, zero)
        saved.append((sv, svx))
    loss, dx, dfinal = _loss_head(x, target, _row2(W["final_norm"]))
    G["final_norm"] = dfinal.reshape(-1)
    dmem_n, zero = None, 0.0
    for layer in reversed(range(DEPTH)):
        sv, svx = saved[layer]
        dx, dmem_n = _cross_bwd(dx, dmem_n, svx, mem_n, W, G, layer, zero)
        hook = functools.partial(send_grads, layer, G)
        dx = _even_bwd(dx, sv, W, G, layer // 2, hook) if layer % 2 == 0 else _odd_bwd(dx, sv, W, G, layer // 2, hook)
        zero = after_layer(layer, G)
    _, (dmn,) = _rw_bwd("rms_bwd_mem", _f_rms, mem_rows, mem_par, [_cols(dmem_n)], [], [0], 256)
    G["mem_norm"] = dmn.reshape(-1)
    return loss, dx, G


_HBM = pl.BlockSpec(memory_space=pltpu.HBM)
_ANY = pl.BlockSpec(memory_space=pl.ANY)
_SEM = pl.BlockSpec(memory_space=pltpu.SEMAPHORE)
_N_PEERS = N_DEV - 1


def _mesh_pos():
    return lax.axis_index("x"), lax.axis_index("y"), lax.axis_index("c")


def _peer(pos, k):
    x, y, c = pos
    return (x ^ ((k >> 2) & 1), y ^ ((k >> 1) & 1), c ^ (k & 1))


def _lin(pos):
    return 4 * pos[0] + 2 * pos[1] + pos[2]


def _ends(gather, srcs, lands, t, sender, receiver):
    if gather:
        return lands[t].at[sender], lands[t].at[sender]
    whole = len(srcs[t].shape) != len(lands[t].shape)
    return (srcs[t] if whole else srcs[t].at[receiver]), lands[t].at[sender]


def _into_slot(name, b2, r0, r, me, dtype, after):
    c = b2.shape[1]
    tr = _row_block(r, c, 2 << 20)
    assert r0 % tr == 0

    def body(me_ref, x_ref, *rest):
        rest[-1][...] = x_ref[...].astype(dtype)

    extra = [] if after is None else [after]
    grid_spec = pltpu.PrefetchScalarGridSpec(
        num_scalar_prefetch=1, grid=(r // tr,),
        in_specs=[pl.BlockSpec((tr, c), lambda i, me, o=r0 // tr: (o + i, 0))] + [_ANY] * len(extra),
        out_specs=pl.BlockSpec((None, tr, c), lambda i, me: (me[0], i, 0)))
    return pl.pallas_call(
        body, name=name, grid_spec=grid_spec, out_shape=jax.ShapeDtypeStruct((N_DEV, r, c), dtype),
        compiler_params=pltpu.CompilerParams(dimension_semantics=("arbitrary",), vmem_limit_bytes=V7X_VMEM_LIMIT_BYTES),
        interpret=False,
    )(me, b2, *extra)


def _exchange_start(name, gather, srcs, lands, after=None):
    ns, nt = len(srcs), len(lands)
    arrs = list(srcs) + list(lands)
    extra = [] if after is None else [after]

    def body(*refs):
        ins, lnd = refs[:ns], refs[ns:ns + nt]
        refs = refs[len(extra):]
        send_sems, recv_sems = refs[ns + nt], refs[ns + nt + 1]
        token = refs[-1]
        pos = _mesh_pos()
        me = _lin(pos)
        for k in range(1, N_DEV):
            peer = _peer(pos, k)
            for t in range(nt):
                src, dst = _ends(gather, ins, lnd, t, me, _lin(peer))
                pltpu.make_async_remote_copy(
                    src_ref=src, dst_ref=dst, send_sem=send_sems.at[t * _N_PEERS + k - 1], recv_sem=recv_sems.at[t * _N_PEERS + k - 1],
                    device_id=peer, device_id_type=pl.DeviceIdType.MESH).start()
        token[...] = jnp.zeros_like(token)

    out = pl.pallas_call(
        body, name=name,
        out_shape=(pltpu.SemaphoreType.DMA((nt * _N_PEERS,)), pltpu.SemaphoreType.DMA((nt * _N_PEERS,)), *[pltpu.HBM(a.shape, a.dtype) for a in arrs],
                   jax.ShapeDtypeStruct((8, 128), F32)),
        in_specs=[_HBM] * (ns + nt) + [_ANY] * len(extra), out_specs=(_SEM, _SEM, *[_HBM] * (ns + nt), pl.BlockSpec(memory_space=pltpu.VMEM)),
        input_output_aliases={j: 2 + j for j in range(ns + nt)},
        compiler_params=pltpu.CompilerParams(has_side_effects=pltpu.SideEffectType.DATAFLOW_SIDE_EFFECTING),
        interpret=False,
    )(*[pltpu.with_memory_space_constraint(a, pltpu.HBM) for a in arrs], *extra)
    return dict(send=out[0], recv=out[1], srcs=list(out[2:2 + ns]), lands=list(out[2 + ns:2 + ns + nt]), token=out[-1][0, 0], token_arr=out[-1], gather=gather)


def _exchange_wait(name, ex, after):
    ns, nt = len(ex["srcs"]), len(ex["lands"])
    gather = ex["gather"]
    arrs = ex["srcs"] + ex["lands"]
    after = list(after) if isinstance(after, (list, tuple)) else [after]

    def body(*refs):
        ins, lnd = refs[:ns], refs[ns:ns + nt]
        send_sems, recv_sems = refs[ns + nt], refs[ns + nt + 1]
        pos = _mesh_pos()
        me = _lin(pos)
        for k in range(1, N_DEV):
            peer = _peer(pos, k)
            for t in range(nt):
                src, _ = _ends(gather, ins, lnd, t, me, _lin(peer))
                _, dst = _ends(gather, ins, lnd, t, _lin(peer), me)
                cp = pltpu.make_async_remote_copy(
                    src_ref=src, dst_ref=dst, send_sem=send_sems.at[t * _N_PEERS + k - 1], recv_sem=recv_sems.at[t * _N_PEERS + k - 1],
                    device_id=peer, device_id_type=pl.DeviceIdType.MESH)
                cp.wait_send()
                cp.wait_recv()

    out = pl.pallas_call(
        body, name=name, out_shape=tuple(pltpu.HBM(a.shape, a.dtype) for a in arrs),
        in_specs=[_HBM] * (ns + nt) + [_SEM, _SEM] + [_ANY] * len(after), out_specs=tuple([_HBM] * (ns + nt)),
        input_output_aliases={j: j for j in range(ns + nt)},
        compiler_params=pltpu.CompilerParams(has_side_effects=pltpu.SideEffectType.DATAFLOW_SIDE_EFFECTING),
        interpret=False,
    )(*arrs, ex["send"], ex["recv"], *after)
    return list(out[:ns]), list(out[ns:])


def _scatter_begin(name, srcs):
    lands = [lax.empty(s.shape if s.ndim == 3 else (N_DEV,) + s.shape, s.dtype) for s in srcs]
    return _exchange_start(name, False, srcs, lands)


def _adam(name, w, m, v, parts, own, me, layer, bufs):
    r, c = parts.shape[1:]
    tr = _row_block(r, max(c, 128), 1 << 20)
    nb = r // tr

    def body(me_ref, w_ref, m_ref, v_ref, p_ref, own_ref, *rest):
        g_ref, d_ref, nm_ref, nv_ref, acc = rest[-5:]
        acc[...] = jnp.zeros_like(acc)
        for k in range(N_DEV):
            @pl.when(me_ref[0] == k)
            def _():
                acc[...] += own_ref[...].astype(F32)

            @pl.when(me_ref[0] != k)
            def _(k=k):
                acc[...] += p_ref[k].astype(F32)

        g = acc[...]
        mm = ADAM_B1 * m_ref[...] + (1.0 - ADAM_B1) * g
        vv = ADAM_B2 * v_ref[...] + (1.0 - ADAM_B2) * jnp.square(g)
        m_hat = mm / (1.0 - ADAM_B1 ** ADAM_STEP)
        v_hat = vv / (1.0 - ADAM_B2 ** ADAM_STEP)
        g_ref[...] = g
        d_ref[...] = -ADAM_LR * (m_hat / (jnp.sqrt(v_hat) + ADAM_EPS) + ADAM_WD * w_ref[...])
        nm_ref[...] = mm
        nv_ref[...] = vv

    blk = pl.BlockSpec((tr, c), lambda i, me, o=layer * nb: (o + i, 0))
    own_spec = pl.BlockSpec((None, tr, c), lambda i, me: (me[0], i, 0)) if own.ndim == 3 else pl.BlockSpec((tr, c), lambda i, me: (i, 0))
    in_specs = [blk, blk, blk, pl.BlockSpec((N_DEV, tr, c), lambda i, me: (0, i, 0)), own_spec]
    args = [me, w, m, v, parts, own]
    aliases = {}
    if bufs is not None:
        in_specs += [_ANY] * 4
        aliases = {len(args) + j: j for j in range(4)}
        args += list(bufs)
    grid_spec = pltpu.PrefetchScalarGridSpec(
        num_scalar_prefetch=1, grid=(nb,), in_specs=in_specs, out_specs=[blk] * 4, scratch_shapes=[pltpu.VMEM((tr, c), F32)])
    return pl.pallas_call(
        body, name=name, grid_spec=grid_spec, out_shape=[jax.ShapeDtypeStruct(w.shape, F32)] * 4, input_output_aliases=aliases,
        compiler_params=pltpu.CompilerParams(dimension_semantics=("arbitrary",), vmem_limit_bytes=V7X_VMEM_LIMIT_BYTES),
        interpret=False,
    )(*args)


def _row_block(r, c, limit):
    best = None
    for tr in range(16, r + 1, 16):
        if r % tr == 0 and tr * c * 4 <= limit:
            best = tr
    return r if best is None else best


_BIG = ("w_in_ab", "pool_w", "w_out_ab", "w_in_cd", "glu_w1", "glu_w2", "w_out_cd", "w_xq", "w_xkv", "w_xo")
_STACKED = ("w_in_ab", "w_in_cd", "w_xkv")
_MIXER_BIG = (("w_in_ab", "pool_w", "w_out_ab"), ("w_in_cd", "glu_w1", "glu_w2", "w_out_cd"))
_CROSS_BIG = ("w_xq", "w_xkv", "w_xo")
_SMALL_SPLIT = ["norm_cd", "sgu_ln_g", "sgu_ln_b", "s5_d"]
_REPLICATED_ODD = ["sgu_w", "sgu_b", "s5_a_re", "s5_a_im", "s5_log_dt", "s5_b_re", "s5_b_im", "s5_c_re", "s5_c_im", "final_norm"]
_REPLICATED_EVEN = ["norm_ab", "pool_scale", "norm_x", "mem_norm"]
_REPLICATED = _REPLICATED_ODD + _REPLICATED_EVEN
_WEIGHTS = ["norm_ab", "w_in_ab", "pool_w", "pool_scale", "w_out_ab", "norm_cd", "w_in_cd", "sgu_ln_g", "sgu_ln_b", "sgu_w", "sgu_b", "s5_a_re",
            "s5_a_im", "s5_log_dt", "s5_b_re", "s5_b_im", "s5_c_re", "s5_c_im", "s5_d", "glu_w1", "glu_w2", "w_out_cd", "norm_x", "w_xq",
            "w_xkv", "w_xo", "mem_norm", "final_norm"]


def _layer_big(layer):
    return [(n, layer // 2) for n in _MIXER_BIG[layer % 2]] + [(n, layer) for n in _CROSS_BIG]


def _gather_parts(layer):
    big = _layer_big(layer)
    return [big[:1], big[1:]]


def _scatter_parts(layer):
    big = _layer_big(layer)
    return [big[2:], big[:2]] if layer % 2 == 0 else [big]


def _from_slots(name, a):
    if name in _STACKED:
        return a
    if name == "pool_w":
        return a.reshape(N_DEV, 4, 32, 256).transpose(1, 0, 2, 3).reshape(4, 256, 256)
    return a.reshape(-1, a.shape[-1])


def _to_slots(name, g):
    if name in _STACKED:
        return g
    if name == "pool_w":
        return g.reshape(4, N_DEV, 32, 256).transpose(1, 0, 2, 3).reshape(N_DEV, 128, 256)
    return g.reshape(N_DEV, -1, g.shape[-1])


def _rows2d(a):
    return a.reshape(-1, a.shape[-1])


def _small_rows(block):
    return jnp.pad(block, ((0, 0), (0, 128 - block.shape[1])))


def kernel(x, mem, norm_ab, w_in_ab, pool_w, pool_scale, w_out_ab, norm_cd, w_in_cd, sgu_ln_g, sgu_ln_b, sgu_w, sgu_b, s5_a_re, s5_a_im, s5_log_dt, s5_b_re, s5_b_im, s5_c_re, s5_c_im, s5_d, glu_w1, glu_w2, w_out_cd, norm_x, w_xq, w_xkv, w_xo, mem_norm, final_norm, loss_target, m_norm_ab, m_w_in_ab, m_pool_w, m_pool_scale, m_w_out_ab, m_norm_cd, m_w_in_cd, m_sgu_ln_g, m_sgu_ln_b, m_sgu_w, m_sgu_b, m_s5_a_re, m_s5_a_im, m_s5_log_dt, m_s5_b_re, m_s5_b_im, m_s5_c_re, m_s5_c_im, m_s5_d, m_glu_w1, m_glu_w2, m_w_out_cd, m_norm_x, m_w_xq, m_w_xkv, m_w_xo, m_mem_norm, m_final_norm, v_norm_ab, v_w_in_ab, v_pool_w, v_pool_scale, v_w_out_ab, v_norm_cd, v_w_in_cd, v_sgu_ln_g, v_sgu_ln_b, v_sgu_w, v_sgu_b, v_s5_a_re, v_s5_a_im, v_s5_log_dt, v_s5_b_re, v_s5_b_im, v_s5_c_re, v_s5_c_im, v_s5_d, v_glu_w1, v_glu_w2, v_w_out_cd, v_norm_x, v_w_xq, v_w_xkv, v_w_xo, v_mem_norm, v_final_norm):
    args = locals()
    w = {n: args[n] for n in _WEIGHTS}
    m = {n: args["m_" + n] for n in _WEIGHTS}
    v = {n: args["v_" + n] for n in _WEIGHTS}

    me = jnp.reshape(_lin(_mesh_pos()), (1,)).astype(jnp.int32)

    order = [(layer, p) for layer in range(DEPTH) for p in range(len(_gather_parts(layer)))]
    lands, gathers = {}, {}
    for key in order:
        lands[key] = []
        for name, i in _gather_parts(key[0])[key[1]]:
            b2 = _rows2d(w[name])
            r = b2.shape[0] // w[name].shape[0]
            lands[key].append(_into_slot("cast_" + name, b2, i * r, r, me, BF16, None))
    small_blocks = jnp.concatenate([_small_rows(w[n]) for n in _SMALL_SPLIT], axis=0)
    lands[order[0]].append(_into_slot("cast_small", small_blocks, 0, 8, me, F32, None))

    def begin_gather(key, after):
        gathers[key] = _exchange_start("gather%d%s_start" % (key[0], "ab"[key[1]]), True, [], lands[key], after)
        return gathers[key]["token"]

    W = {n: w[n] for n in _REPLICATED}
    W["mem_norm"] = w["mem_norm"] + begin_gather(order[0], None)
    for name in _BIG:
        W[name] = [None] * w[name].shape[0]

    def weights_of(layer, part, after):
        key = (layer, part)
        if key not in gathers:
            return 0.0
        if key == order[0]:
            after = [after] + [a for k in order[1:] for a in lands[k]]
        _, got = _exchange_wait("gather%d%s_wait" % (layer, "ab"[part]), gathers[key], after)
        for (name, i), arr in zip(_gather_parts(layer)[part], got):
            W[name][i] = _from_slots(name, arr)
        if key == order[0]:
            sm = got[-1].reshape(N_DEV, 4, 2, 128)
            for j, n in enumerate(_SMALL_SPLIT):
                width = w[n].shape[1]
                W[n] = sm[:, j, :, :width].transpose(1, 0, 2).reshape(2, N_DEV * width)
        nxt = order.index(key) + 1
        return begin_gather(order[nxt], got[0]) if nxt < len(order) and order[nxt] not in gathers else 0.0

    def start_ahead(layer, after):
        return begin_gather((layer + 1, 0), after) if layer + 1 < DEPTH else 0.0

    scatters, small = {}, {}

    def send_grads(layer, G, part):
        srcs = [_to_slots(name, G[name][i]) for name, i in _scatter_parts(layer)[part]]
        scatters[layer, part] = _scatter_begin("scatter%d%s_start" % (layer, "ab"[part]), srcs)
        return scatters[layer, part]["token"]

    def rows(a):
        if a.ndim == 1:
            return a.reshape(1, -1)
        return a.reshape(-1, 128) if a.shape[-1] < 128 and a.size % 128 == 0 else a.reshape(-1, a.shape[-1])

    def begin_small(tag, G, names, split):
        srcs = [rows(G[n] if n in ("mem_norm", "final_norm") else jnp.stack(G[n])) for n in names]
        if split:
            srcs += [jnp.stack(G[n]).reshape(2, N_DEV, -1).transpose(1, 0, 2) for n in _SMALL_SPLIT]
        small[tag] = _scatter_begin("scatter_small_%s_start" % tag, srcs)
        return small[tag]["token"]

    def after_layer(layer, G):
        return begin_small("odd", G, _REPLICATED_ODD, True) if layer == 1 else 0.0

    loss, dx, G = _local_step(x[0], mem[0], loss_target[0], W, weights_of, start_ahead, send_grads, after_layer)
    loss = lax.psum(loss[0, 0], MESH_AXES)
    begin_small("even", G, _REPLICATED_EVEN, False)

    out = {}
    after = small["even"]["token_arr"]
    for layer, part in scatters:
        own, got = _exchange_wait("scatter%d%s_wait" % (layer, "ab"[part]), scatters[layer, part], after)
        for (name, i), mine, parts in zip(_scatter_parts(layer)[part], own, got):
            out[name] = _adam("adam_" + name, _rows2d(w[name]), _rows2d(m[name]), _rows2d(v[name]), parts, mine, me, i, out.get(name))
        after = [out[name][0] for name, i in _scatter_parts(layer)[part]]
    for name in _BIG:
        out[name] = [a.reshape(w[name].shape) for a in out[name]]
    for tag, names in (("odd", _REPLICATED_ODD + _SMALL_SPLIT), ("even", _REPLICATED_EVEN)):
        own, got = _exchange_wait("scatter_small_%s_wait" % tag, small[tag], after)
        for n, mine, parts in zip(names, own, got):
            as2d = (lambda a: a) if n in _SMALL_SPLIT else rows
            res = _adam("adam_" + n, as2d(w[n]), as2d(m[n]), as2d(v[n]), parts, mine, me, 0, None)
            out[n] = [a.reshape(w[n].shape) for a in res]
        after = [out[n][0] for n in names]

    return (loss, dx[None], *[out[n][0] for n in _WEIGHTS], *[out[n][1] for n in _WEIGHTS], *[out[n][2] for n in _WEIGHTS],
            *[out[n][3] for n in _WEIGHTS])
```

```python
import functools
import math

import jax
import jax.numpy as jnp
from jax import lax
from jax.experimental import pallas as pl
from jax.experimental.pallas import tpu as pltpu

F32 = jnp.float32
BF16 = jnp.bfloat16

SEQ = 2048
D_MODEL = 1024
MEM_LEN = 256
DEPTH = 4
N_DEV = 8
EPS = 1e-6
NEG = -1e30
A_HEAD_DIM = 64
X_HEAD_DIM = 256
S5_GROUPS = 32
S5_STATE = 64
S5_GROUP_DIM = 16

ADAM_LR = 0.001
ADAM_B1 = 0.9
ADAM_B2 = 0.999
ADAM_EPS = 1e-08
ADAM_WD = 0.01
ADAM_STEP = 10

V7X_VMEM_LIMIT_BYTES = 56 * 1024 * 1024
_MM_VMEM_BYTES = 36 * 1024 * 1024
MESH_AXES = ("x", "y", "c")


def _pc(body, *, name, out_shape, grid=None, in_specs=None, out_specs=None, scratch_shapes=(), aliases=None, sem=None):
    kw = {}
    if grid is not None:
        kw["grid"] = grid
    if in_specs is not None:
        kw["in_specs"] = in_specs
    if out_specs is not None:
        kw["out_specs"] = out_specs
    if aliases:
        kw["input_output_aliases"] = aliases
    return pl.pallas_call(
        body,
        name=name,
        out_shape=out_shape,
        scratch_shapes=list(scratch_shapes),
        compiler_params=pltpu.CompilerParams(dimension_semantics=sem, vmem_limit_bytes=V7X_VMEM_LIMIT_BYTES),
        interpret=False,
        **kw,
    )


def _cols(arr, c0=0, width=None, nsplit=1, r0=0):
    width = arr.shape[1] - c0 if width is None else width
    assert c0 % width == 0 and width % nsplit == 0
    return (arr, c0, width, nsplit, r0)


def _par(arr, nsplit=1):
    return (arr, nsplit)


def _ld(ref, nsplit):
    if nsplit == 1:
        return ref[...].astype(F32)
    if len(ref.shape) == 3:
        return tuple(ref[k].astype(F32) for k in range(nsplit))
    w = ref.shape[-1] // nsplit
    return tuple(ref[:, k * w:(k + 1) * w].astype(F32) for k in range(nsplit))


def _st(ref, val, nsplit, accumulate=False):
    if nsplit == 1:
        val = (val,)
    for k in range(nsplit):
        if nsplit == 1:
            idx = (Ellipsis,)
        elif len(ref.shape) == 3:
            idx = (k,)
        else:
            w = ref.shape[-1] // nsplit
            idx = (slice(None), slice(k * w, (k + 1) * w))
        if accumulate:
            ref[idx] += val[k].astype(ref.dtype)
        else:
            ref[idx] = val[k].astype(ref.dtype)


def _row_spec(tr, op):
    _, c0, w, _, r0 = op
    assert r0 % tr == 0
    return pl.BlockSpec((tr, w), lambda i, cb=c0 // w, rb=r0 // tr: (i + rb, cb))


def _full_spec(arr):
    return pl.BlockSpec(arr.shape, lambda i, nd=arr.ndim: (0,) * nd)


def _rw_fwd(name, f, rows, pars, outs, tr, n_rows=None):
    n_rows = rows[0][0].shape[0] if n_rows is None else n_rows
    nr, npar = len(rows), len(pars)

    def body(*refs):
        r = [_ld(refs[i], rows[i][3]) for i in range(nr)]
        p = [_ld(refs[nr + i], pars[i][1]) for i in range(npar)]
        res = f(r, p)
        for k, (_, _, ns) in enumerate(outs):
            _st(refs[nr + npar + k], res[k], ns)

    res = _pc(
        body, name=name, grid=(n_rows // tr,),
        in_specs=[_row_spec(tr, op) for op in rows] + [_full_spec(a) for a, _ in pars],
        out_specs=[pl.BlockSpec((tr, w), lambda i: (i, 0)) for w, _, _ in outs],
        out_shape=[jax.ShapeDtypeStruct((n_rows, w), dt) for w, dt, _ in outs],
        sem=("arbitrary",),
    )(*[op[0] for op in rows], *[a for a, _ in pars])
    return list(res)


def _rw_bwd(name, f, rows, pars, douts, drow, dpar, tr):
    n_rows = rows[0][0].shape[0]
    nr, npar = len(rows), len(pars)
    dgiven = [d for d in douts if d is not None]
    nd = len(dgiven)

    def body(*refs):
        r = [_ld(refs[i], rows[i][3]) for i in range(nr)]
        p = [_ld(refs[nr + i], pars[i][1]) for i in range(npar)]
        d = [_ld(refs[nr + npar + i], dgiven[i][3]) for i in range(nd)]
        orefs = refs[nr + npar + nd:]

        def g(dr, dp):
            rr, pp = list(r), list(p)
            for j, (idx, _) in enumerate(drow):
                rr[idx] = dr[j]
            for j, idx in enumerate(dpar):
                pp[idx] = dp[j]
            return tuple(f(rr, pp))

        out, vjp = jax.vjp(g, [r[idx] for idx, _ in drow], [p[idx] for idx in dpar])
        ct, j = [], 0
        for k, o in enumerate(out):
            if douts[k] is None:
                ct.append(jax.tree.map(jnp.zeros_like, o))
            else:
                ct.append(d[j])
                j += 1
        gdr, gdp = vjp(tuple(ct))
        for j, (idx, _) in enumerate(drow):
            _st(orefs[j], gdr[j], rows[idx][3])

        @pl.when(pl.program_id(0) == 0)
        def _():
            for j in range(len(dpar)):
                orefs[len(drow) + j][...] = jnp.zeros_like(orefs[len(drow) + j])

        for j, idx in enumerate(dpar):
            _st(orefs[len(drow) + j], gdp[j], pars[idx][1], accumulate=True)

    res = _pc(
        body, name=name, grid=(n_rows // tr,),
        in_specs=[_row_spec(tr, op) for op in rows] + [_full_spec(a) for a, _ in pars] + [_row_spec(tr, op) for op in dgiven],
        out_specs=[pl.BlockSpec((tr, rows[idx][2]), lambda i: (i, 0)) for idx, _ in drow] + [_full_spec(pars[idx][0]) for idx in dpar],
        out_shape=[jax.ShapeDtypeStruct((n_rows, rows[idx][2]), dt) for idx, dt in drow]
        + [jax.ShapeDtypeStruct(pars[idx][0].shape, F32) for idx in dpar],
        sem=("arbitrary",),
    )(*[op[0] for op in rows], *[a for a, _ in pars], *[op[0] for op in dgiven])
    res = list(res)
    return res[:len(drow)], res[len(drow):]


def _sigmoid(x):
    return jax.nn.sigmoid(x)


def _silu(x):
    return x * _sigmoid(x)


def _rms(x, g):
    return x * lax.rsqrt(jnp.mean(x * x, axis=-1, keepdims=True) + EPS) * g


def _f_rms(r, p):
    return [_rms(r[0], p[0])]


def _f_rms_res(r, p):
    return [r[0], _rms(r[0], p[0])]


def _f_gate_ab(r, p):
    o, ga, mixed, gb = r
    return [(o * _silu(ga), mixed * p[0] * _silu(gb))]


def _f_sgu(r, p):
    u, v, gc = r
    lg, lb, w, b = p
    n = float(D_MODEL)
    mu = sum(jnp.sum(vk, axis=-1, keepdims=True) for vk in v) / n
    var = sum(jnp.sum(jnp.square(vk - mu), axis=-1, keepdims=True) for vk in v) / n
    rs = lax.rsqrt(var + EPS)
    t = w[0].shape[0]
    tri = lax.broadcasted_iota(jnp.int32, (t, t), 0) >= lax.broadcasted_iota(jnp.int32, (t, t), 1)
    outs = []
    for k in range(len(v)):
        vn = (v[k] - mu) * rs * lg[k] + lb[k]
        mixed = jnp.dot(jnp.where(tri, w[k], 0.0), vn, preferred_element_type=F32) + b[k]
        outs.append(u[k] * mixed * _silu(gc[k]))
    return [tuple(outs)]


def _gelu(x):
    return 0.5 * x * (1.0 + jnp.tanh(math.sqrt(2.0 / math.pi) * (x + 0.044715 * (x * x * x))))


def _f_gelu_y(r, p):
    yc, uf = r
    return [_gelu(yc + p[0] * uf)]


def _f_glu_gate(r, p):
    t12, gd = r
    return [t12[0] * _sigmoid(t12[1]) * _silu(gd)]


def _f_s5_prep(r, p):
    ar, ai, ldt = r
    dt = jnp.exp(ldt)
    mag = jnp.exp(dt * ar)
    abar_re = mag * jnp.cos(dt * ai)
    abar_im = mag * jnp.sin(dt * ai)
    nr, ni = abar_re - 1.0, abar_im
    inv = 1.0 / (ar * ar + ai * ai)
    return [abar_re, abar_im, (nr * ar + ni * ai) * inv, (ni * ar - nr * ai) * inv]


def _f_bbar(r, p):
    br, bi, cr, ci = r
    return [cr * br - ci * bi, cr * bi + ci * br]


def _loss_head(x, target, g):
    tr = 256
    n_rows, width = x.shape

    def f(xv, gv, tv):
        err = jnp.square(_rms(xv, gv) - tv)
        return 0.5 * jnp.mean(err, axis=-1, keepdims=True)

    def body(x_ref, t_ref, g_ref, loss_ref, dx_ref, dg_ref):
        @pl.when(pl.program_id(0) == 0)
        def _():
            loss_ref[...] = jnp.zeros_like(loss_ref)
            dg_ref[...] = jnp.zeros_like(dg_ref)

        tv = t_ref[...]
        row_loss, vjp = jax.vjp(lambda a, b: f(a, b, tv), x_ref[...], g_ref[...])
        dx, dg = vjp(jnp.ones_like(row_loss))
        dx_ref[...] = dx
        dg_ref[...] += dg
        loss_ref[...] += jnp.broadcast_to(jnp.sum(row_loss, axis=0, keepdims=True), loss_ref.shape)

    blk = pl.BlockSpec((tr, width), lambda i: (i, 0))
    one = pl.BlockSpec((1, width), lambda i: (0, 0))
    return _pc(
        body, name="loss_head", grid=(n_rows // tr,), in_specs=[blk, blk, one],
        out_specs=[pl.BlockSpec((1, 128), lambda i: (0, 0)), blk, one],
        out_shape=[jax.ShapeDtypeStruct((1, 128), F32), jax.ShapeDtypeStruct(x.shape, F32), jax.ShapeDtypeStruct((1, width), F32)],
        sem=("arbitrary",),
    )(x, target, g)


_NT = (((1,), (1,)), ((), ()))
_TN = (((0,), (0,)), ((), ()))


def _tile(n, cap):
    t = min(n, cap)
    while n % t:
        t -= 128
    assert t > 0
    return t


def _mm(name, a, b, *, ta=False, tb=False, out_dtype=F32, a_off=0, a_width=None, res=None, out_stack=None):
    assert not (ta and tb)
    stacked = b.ndim == 3
    bk, bn = (b.shape[1], b.shape[0] * b.shape[2]) if stacked else b.shape
    if ta:
        kc = a.shape[0]
        m = a.shape[1] - a_off if a_width is None else a_width
        n = bn
        assert bk == kc and not stacked
    else:
        m = a.shape[0]
        kc = a.shape[1] - a_off if a_width is None else a_width
        n = bk if tb else bn
        assert (bn if tb else bk) == kc
    tn = _tile(b.shape[2] if stacked and not tb else (out_stack or n), 1024)
    size = lambda dt: jnp.dtype(dt).itemsize
    for tk_cap, tm_cap in ((2048, 2048), (2048, 1024), (1024, 1024), (1024, 512), (1024, 256)):
        tm, tk = _tile(m, tm_cap), _tile(b.shape[2] if stacked and tb else kc, tk_cap)
        nk = kc // tk
        vmem = 2 * tm * tk * size(a.dtype) + 2 * tk * tn * size(b.dtype) + tm * tn * (2 * size(out_dtype) + (4 if nk > 1 else 0) + (8 if res is not None else 0))
        if vmem <= _MM_VMEM_BYTES:
            break
    if ta:
        assert a_off % tm == 0
        a_spec = pl.BlockSpec((tk, tm), lambda i, j, k, o=a_off // tm: (k, i + o))
        dims = _TN
    else:
        assert a_off % tk == 0
        a_spec = pl.BlockSpec((tm, tk), lambda i, j, k, o=a_off // tk: (i, k + o))
        dims = _NT if tb else (((1,), (0,)), ((), ()))
    if stacked and tb:
        b_spec = pl.BlockSpec((None, tn, tk), lambda i, j, k, q=b.shape[2] // tk: (k // q, j, k % q))
    elif stacked:
        b_spec = pl.BlockSpec((None, tk, tn), lambda i, j, k, q=b.shape[2] // tn: (j // q, k, j % q))
    else:
        b_spec = pl.BlockSpec((tn, tk), lambda i, j, k: (j, k)) if tb else pl.BlockSpec((tk, tn), lambda i, j, k: (k, j))
    if out_stack:
        out_spec = pl.BlockSpec((None, tm, tn), lambda i, j, k, q=out_stack // tn: (j // q, i, j % q))
        out_shape = jax.ShapeDtypeStruct((n // out_stack, m, out_stack), out_dtype)
    else:
        out_spec = pl.BlockSpec((tm, tn), lambda i, j, k: (i, j))
        out_shape = jax.ShapeDtypeStruct((m, n), out_dtype)
    in_specs, args = [a_spec, b_spec], [a, b]
    has_res = res is not None
    if has_res:
        in_specs.append(pl.BlockSpec((tm, tn), lambda i, j, k: (i, j)))
        args.append(res)

    def finish(refs, acc):
        if has_res:
            acc = acc + refs[2][...].astype(F32)
        refs[3 if has_res else 2][...] = acc.astype(out_dtype)

    def body_one(*refs):
        finish(refs, lax.dot_general(refs[0][...].astype(BF16), refs[1][...].astype(BF16), dims, preferred_element_type=F32))

    def body(*refs):
        acc_ref = refs[-1]
        k = pl.program_id(2)

        @pl.when(k == 0)
        def _():
            acc_ref[...] = jnp.zeros_like(acc_ref)

        acc_ref[...] += lax.dot_general(refs[0][...].astype(BF16), refs[1][...].astype(BF16), dims, preferred_element_type=F32)

        @pl.when(k == nk - 1)
        def _():
            finish(refs, acc_ref[...])

    return _pc(
        body_one if nk == 1 else body, name=name, grid=(m // tm, n // tn, nk), in_specs=in_specs, out_specs=out_spec, out_shape=out_shape,
        scratch_shapes=[] if nk == 1 else [pltpu.VMEM((tm, tn), F32)], sem=("parallel", "parallel", "arbitrary"),
    )(*args)


def _mm_blocks(name, a, b, *, grid, out_blk, a_blk, a_idx, b_blk, b_idx, dims, out_dtype=F32):
    gi, gj, nk = grid

    def body(a_ref, b_ref, o_ref, acc_ref):
        k = pl.program_id(2)

        @pl.when(k == 0)
        def _():
            acc_ref[...] = jnp.zeros_like(acc_ref)

        acc_ref[...] += lax.dot_general(a_ref[...].astype(BF16), b_ref[...].astype(BF16), dims, preferred_element_type=F32)

        @pl.when(k == nk - 1)
        def _():
            o_ref[...] = acc_ref[...].astype(o_ref.dtype)

    return _pc(
        body, name=name, grid=grid, in_specs=[pl.BlockSpec(a_blk, a_idx), pl.BlockSpec(b_blk, b_idx)],
        out_specs=pl.BlockSpec(out_blk, lambda i, j, k: (i, j)), out_shape=jax.ShapeDtypeStruct((gi * out_blk[0], gj * out_blk[1]), out_dtype),
        scratch_shapes=[pltpu.VMEM(out_blk, F32)], sem=("parallel", "parallel", "arbitrary"),
    )(a, b)


def _head_masks(width, nsub):
    lane = lax.broadcasted_iota(jnp.int32, (1, width), 1)
    hd = width // nsub
    return [(lane >= h * hd) & (lane < (h + 1) * hd) for h in range(nsub)]


def _dilated_log_count(row0, tq, ext):
    delta = (row0 + lax.broadcasted_iota(jnp.int32, (tq, ext), 0)) - lax.broadcasted_iota(jnp.int32, (tq, ext), 1)
    cnt = (delta <= 128).astype(jnp.int32) + (((delta & 3) == 0) & (delta <= 512)).astype(jnp.int32) + ((delta & 15) == 0).astype(jnp.int32)
    logc = jnp.where(cnt == 3, math.log(3.0), jnp.where(cnt == 2, math.log(2.0), 0.0))
    return jnp.where((delta >= 0) & (cnt > 0), logc, NEG)


def _bias_table(tab, nq, tq):
    @pl.when(pl.program_id(0) == 0)
    def _():
        for d in range(nq):
            tab[d] = _dilated_log_count(d * tq, tq, tq)


def _scores(q, ke, tab, r, masks, h):
    qm = (jnp.where(masks[h], q, 0.0) if len(masks) > 1 else q).astype(BF16)
    s = lax.dot_general(qm, ke, _NT, preferred_element_type=F32)
    if tab is not None:
        s = s + jnp.concatenate([tab[r - c] for c in range(r + 1)], axis=1)
    return qm, s


def _head_value(masks, h, tile, reduce, fill):
    return reduce(jnp.where(masks[h], tile, fill) if len(masks) > 1 else tile, axis=-1, keepdims=True)


def _attn_fwd(name, qa, ka, va, *, qc, kc, vc, width, nblk, nsub, causal, tq, scale, out_dtype):
    sq, t_len = qa.shape[0], ka.shape[0]
    nq = sq // tq

    def body(q_ref, k_ref, v_ref, o_ref, lse_ref, *scratch):
        tab = scratch[0] if causal else None
        if causal:
            _bias_table(tab, nq, tq)
        kb = k_ref[...].astype(BF16)
        vb = v_ref[...].astype(BF16)
        masks = _head_masks(width, nsub)
        for r in range(nq):
            ext = (r + 1) * tq if causal else t_len
            q = q_ref[r * tq:(r + 1) * tq, :].astype(F32) * scale
            ke, ve = kb[:ext], vb[:ext]
            o = lse = None
            for h in range(nsub):
                _, s = _scores(q, ke, tab, r, masks, h)
                m = jnp.max(s, axis=-1, keepdims=True)
                p = jnp.exp(s - m)
                l = jnp.sum(p, axis=-1, keepdims=True)
                oh = jnp.dot(p.astype(BF16), ve, preferred_element_type=F32) * (1.0 / l)
                lh = jnp.broadcast_to(m + jnp.log(l), (tq, width))
                o = oh if o is None else jnp.where(masks[h], oh, o)
                lse = lh if lse is None else jnp.where(masks[h], lh, lse)
            o_ref[r * tq:(r + 1) * tq, :] = o.astype(o_ref.dtype)
            lse_ref[r * tq:(r + 1) * tq, :] = lse

    blk = pl.BlockSpec((sq, width), lambda i: (0, i))
    return _pc(
        body, name=name, grid=(nblk,),
        in_specs=[pl.BlockSpec((sq, width), lambda i, c=qc: (0, c + i)), pl.BlockSpec((t_len, width), lambda i, c=kc: (0, c + i)),
                  pl.BlockSpec((t_len, width), lambda i, c=vc: (0, c + i))],
        out_specs=[blk, blk],
        out_shape=[jax.ShapeDtypeStruct((sq, nblk * width), out_dtype), jax.ShapeDtypeStruct((sq, nblk * width), F32)],
        scratch_shapes=[pltpu.VMEM((nq, tq, tq), F32)] if causal else [], sem=("arbitrary",),
    )(qa, ka, va)


def _attn_bwd(name, qa, ka, va, doa, oa, lsea, *, qc, kc, vc, width, nblk, nsub, causal, tq, scale, out_dtype):
    sq, t_len = qa.shape[0], ka.shape[0]
    nq = sq // tq

    def body(q_ref, k_ref, v_ref, do_ref, o_ref, lse_ref, dq_ref, dk_ref, dv_ref, dk_acc, dv_acc, *scratch):
        tab = scratch[0] if causal else None
        if causal:
            _bias_table(tab, nq, tq)
        kb = k_ref[...].astype(BF16)
        vb = v_ref[...].astype(BF16)
        masks = _head_masks(width, nsub)
        dk_acc[...] = jnp.zeros_like(dk_acc)
        dv_acc[...] = jnp.zeros_like(dv_acc)
        for r in range(nq):
            ext = (r + 1) * tq if causal else t_len
            q = q_ref[r * tq:(r + 1) * tq, :].astype(F32) * scale
            do = do_ref[r * tq:(r + 1) * tq, :].astype(F32)
            do_o = do * o_ref[r * tq:(r + 1) * tq, :].astype(F32)
            lse = lse_ref[r * tq:(r + 1) * tq, :]
            ke, ve = kb[:ext], vb[:ext]
            dq = None
            for h in range(nsub):
                qm, s = _scores(q, ke, tab, r, masks, h)
                dom = (jnp.where(masks[h], do, 0.0) if nsub > 1 else do).astype(BF16)
                pn = jnp.exp(s - _head_value(masks, h, lse, jnp.max, -jnp.inf))
                dpn = lax.dot_general(dom, ve, _NT, preferred_element_type=F32)
                dsb = (pn * (dpn - _head_value(masks, h, do_o, jnp.sum, 0.0))).astype(BF16)
                dqh = jnp.dot(dsb, ke, preferred_element_type=F32)
                dq = dqh if dq is None else jnp.where(masks[h], dqh, dq)
                dk_acc[0:ext, :] += lax.dot_general(dsb, qm, _TN, preferred_element_type=F32)
                dv_acc[0:ext, :] += lax.dot_general(pn.astype(BF16), dom, _TN, preferred_element_type=F32)
            dq_ref[r * tq:(r + 1) * tq, :] = (dq * scale).astype(dq_ref.dtype)
        dk_ref[...] = dk_acc[...].astype(dk_ref.dtype)
        dv_ref[...] = dv_acc[...].astype(dv_ref.dtype)

    return _pc(
        body, name=name, grid=(nblk,),
        in_specs=[pl.BlockSpec((sq, width), lambda i, c=qc: (0, c + i)), pl.BlockSpec((t_len, width), lambda i, c=kc: (0, c + i)),
                  pl.BlockSpec((t_len, width), lambda i, c=vc: (0, c + i))] + [pl.BlockSpec((sq, width), lambda i: (0, i))] * 3,
        out_specs=[pl.BlockSpec((sq, width), lambda i: (0, i)), pl.BlockSpec((t_len, width), lambda i: (0, i)), pl.BlockSpec((t_len, width), lambda i: (0, i))],
        out_shape=[jax.ShapeDtypeStruct((sq, nblk * width), out_dtype), jax.ShapeDtypeStruct((t_len, nblk * width), out_dtype),
                   jax.ShapeDtypeStruct((t_len, nblk * width), out_dtype)],
        scratch_shapes=[pltpu.VMEM((t_len, width), F32), pltpu.VMEM((t_len, width), F32)] + ([pltpu.VMEM((nq, tq, tq), F32)] if causal else []),
        sem=("arbitrary",),
    )(qa, ka, va, doa, oa, lsea)


_SELF = dict(qc=0, kc=8, vc=16, width=128, nblk=8, nsub=2, causal=True, tq=256, scale=A_HEAD_DIM ** -0.5)
_CROSS = dict(qc=0, kc=0, vc=4, width=256, nblk=4, nsub=1, causal=False, tq=512, scale=X_HEAD_DIM ** -0.5)


def _window_sum(x, g, row, backward):
    n = x.shape[0]

    def shift(y, k):
        if backward:
            return jnp.where(row < n - k, pltpu.roll(y, n - k, 0), 0.0)
        return jnp.where(row >= k, pltpu.roll(y, k, 0), 0.0)

    s2 = x + shift(x, 1)
    s4 = s2 + shift(s2, 2)
    s8 = s4 + shift(s4, 4)
    s16 = s8 + shift(s8, 8)
    return jnp.where(g == 0, s2, jnp.where(g == 1, s4, jnp.where(g == 2, s8, s16)))


def _pool(name, arr, c0, backward, out_dtype):
    n = arr.shape[0]
    gw = 256

    def body(v_ref, o_ref):
        g = pl.program_id(0)
        v = v_ref[...].astype(F32)
        row = lax.broadcasted_iota(jnp.int32, v.shape, 0)
        w = jnp.where(g == 0, 2, jnp.where(g == 1, 4, jnp.where(g == 2, 8, 16)))
        cnt = jnp.minimum(row + 1, w).astype(F32)
        if backward:
            o_ref[...] = (_window_sum(v / cnt, g, row, True) - v).astype(o_ref.dtype)
        else:
            o_ref[...] = (_window_sum(v, g, row, False) / cnt - v).astype(o_ref.dtype)

    return _pc(
        body, name=name, grid=(4,), in_specs=[pl.BlockSpec((n, gw), lambda i, c=c0 // gw: (0, c + i))],
        out_specs=pl.BlockSpec((n, gw), lambda i: (0, i)), out_shape=jax.ShapeDtypeStruct((n, 4 * gw), out_dtype), sem=("parallel",),
    )(arr)


_SCAN_ROWS = 256


def _scan_fwd(bu3, a2):
    n = bu3.shape[0]

    def body(bu_ref, a_ref, h_ref, carry):
        @pl.when(pl.program_id(0) == 0)
        def _():
            carry[...] = jnp.zeros_like(carry)

        ar, ai = a_ref[0:16, :], a_ref[16:32, :]

        def step(t, c):
            hr, hi = c
            nr = ar * hr - ai * hi + bu_ref[t, 0:16, :]
            ni = ar * hi + ai * hr + bu_ref[t, 16:32, :]
            h_ref[t, 0:16, :] = nr
            h_ref[t, 16:32, :] = ni
            return nr, ni

        hr, hi = lax.fori_loop(0, _SCAN_ROWS, step, (carry[0:16, :], carry[16:32, :]), unroll=8)
        carry[0:16, :] = hr
        carry[16:32, :] = hi

    blk = pl.BlockSpec((_SCAN_ROWS, 32, 128), lambda i: (i, 0, 0))
    return _pc(
        body, name="s5_scan_fwd", grid=(n // _SCAN_ROWS,), in_specs=[blk, pl.BlockSpec((32, 128), lambda i: (0, 0))], out_specs=blk,
        out_shape=jax.ShapeDtypeStruct(bu3.shape, F32), scratch_shapes=[pltpu.VMEM((32, 128), F32)], sem=("arbitrary",),
    )(bu3, a2)


def _scan_bwd(dh3, h3, a2):
    n = dh3.shape[0]
    nb = n // _SCAN_ROWS

    def body(dh_ref, h_ref, a_ref, dbu_ref, da_ref, carry):
        @pl.when(pl.program_id(0) == 0)
        def _():
            carry[...] = jnp.zeros_like(carry)
            da_ref[...] = jnp.zeros_like(da_ref)

        ar, ai = a_ref[0:16, :], a_ref[16:32, :]

        def step(tt, c):
            gr, gi, dar, dai = c
            t = _SCAN_ROWS - 1 - tt
            hr, hi = h_ref[t, 0:16, :], h_ref[t, 16:32, :]
            dar = dar + gr * hr + gi * hi
            dai = dai - gr * hi + gi * hr
            ngr = dh_ref[t, 0:16, :] + ar * gr + ai * gi
            ngi = dh_ref[t, 16:32, :] - ai * gr + ar * gi
            dbu_ref[t, 0:16, :] = ngr
            dbu_ref[t, 16:32, :] = ngi
            return ngr, ngi, dar, dai

        z = jnp.zeros((16, 128), F32)
        gr, gi, dar, dai = lax.fori_loop(0, _SCAN_ROWS, step, (carry[0:16, :], carry[16:32, :], z, z), unroll=8)
        carry[0:16, :] = gr
        carry[16:32, :] = gi
        da_ref[0:16, :] += dar
        da_ref[16:32, :] += dai

    blk = pl.BlockSpec((_SCAN_ROWS, 32, 128), lambda i: (nb - 1 - i, 0, 0))
    small = pl.BlockSpec((32, 128), lambda i: (0, 0))
    return _pc(
        body, name="s5_scan_bwd", grid=(nb,), in_specs=[blk, blk, small], out_specs=[blk, small],
        out_shape=[jax.ShapeDtypeStruct(dh3.shape, F32), jax.ShapeDtypeStruct((32, 128), F32)],
        scratch_shapes=[pltpu.VMEM((32, 128), F32)], sem=("arbitrary",),
    )(dh3, h3, a2)


def _bdense(bb_re, bb_im):
    eye = jnp.eye(8, dtype=F32)

    def one(bb):
        return jnp.einsum("sgph,gk->sghkp", bb.reshape(4, 8, S5_STATE, S5_GROUP_DIM), eye).reshape(512, 512)

    return jnp.concatenate([one(bb_re), one(bb_im)], axis=1)


def _cdense(c_re, c_im):
    eye = jnp.eye(8, dtype=F32)

    def one(cc):
        return jnp.einsum("sghp,gk->sgpkh", cc.reshape(4, 8, S5_GROUP_DIM, S5_STATE), eye).reshape(2048, 128)

    return jnp.concatenate([one(c_re), -one(c_im)], axis=0)


_NN = (((1,), (0,)), ((), ()))
_UF_BLOCK = 3072 // 128


def _s5_bu(z, bd):
    return _mm_blocks("mm_s5_bu", z, bd, grid=(1, 8, 1), out_blk=(SEQ, 512), a_blk=(SEQ, 128), a_idx=lambda i, j, k: (0, _UF_BLOCK + j % 4),
                      b_blk=(128, 512), b_idx=lambda i, j, k: (j % 4, j // 4), dims=_NN)


def _s5_bu_dx(dbu, bd):
    return _mm_blocks("mm_s5_bu_dx", dbu, bd, grid=(1, 4, 2), out_blk=(SEQ, 128), a_blk=(SEQ, 512), a_idx=lambda i, j, k: (0, 4 * k + j),
                      b_blk=(128, 512), b_idx=lambda i, j, k: (j, k), dims=_NT)


def _s5_bu_dw(z, dbu):
    return _mm_blocks("mm_s5_bu_dw", z, dbu, grid=(4, 2, 2), out_blk=(128, 512), a_blk=(1024, 128), a_idx=lambda i, j, k: (k, _UF_BLOCK + i),
                      b_blk=(1024, 512), b_idx=lambda i, j, k: (k, 4 * j + i), dims=_TN)


def _s5_y(h2, cf):
    return _mm_blocks("mm_s5_y", h2, cf, grid=(1, 4, 2), out_blk=(SEQ, 128), a_blk=(SEQ, 512), a_idx=lambda i, j, k: (0, 4 * k + j),
                      b_blk=(512, 128), b_idx=lambda i, j, k: (4 * k + j, 0), dims=_NN)


def _s5_y_dx(dyc, cf):
    return _mm_blocks("mm_s5_y_dx", dyc, cf, grid=(1, 8, 1), out_blk=(SEQ, 512), a_blk=(SEQ, 128), a_idx=lambda i, j, k: (0, j % 4),
                      b_blk=(512, 128), b_idx=lambda i, j, k: (j, 0), dims=_NT)


def _s5_y_dw(h2, dyc):
    return _mm_blocks("mm_s5_y_dw", h2, dyc, grid=(8, 1, 2), out_blk=(512, 128), a_blk=(1024, 512), a_idx=lambda i, j, k: (k, i),
                      b_blk=(1024, 128), b_idx=lambda i, j, k: (k, i % 4), dims=_TN)


def _pool_dense(pw):
    eye = jnp.eye(4, dtype=pw.dtype)
    return jnp.einsum("gcd,gk->gckd", pw, eye).reshape(1024, 1024)


def _row2(v):
    return v.reshape(1, -1)


def _even_fwd(x, W, i, zero, rest_of_weights, start_ahead):
    hn = _rw_fwd("rms_fwd", _f_rms, [_cols(x)], [_par(_row2(W["norm_ab"][i]) + zero)], [(D_MODEL, BF16, 1)], 256)[0]
    z = _mm("mm_in_ab", hn, W["w_in_ab"][i], out_dtype=BF16)
    o, lse = _attn_fwd("attn_self_fwd", z, z, z, out_dtype=F32, **_SELF)
    zero = rest_of_weights(o)
    pooled = _pool("pool_fwd", z, 4096, False, BF16)
    wp = _pool_dense(W["pool_w"][i])
    mixed = _mm("mm_pool", pooled, wp)
    scale = _row2(W["pool_scale"][i]) + zero
    ab = _rw_fwd("gate_ab_fwd", _f_gate_ab, [_cols(o), _cols(z, 3072, 1024), _cols(mixed), _cols(z, 5120, 1024)], [_par(scale)],
                 [(2048, BF16, 2)], 256)[0]
    x1 = _mm("mm_out_ab", ab, W["w_out_ab"][i], res=x)
    return x1, dict(x=x, hn=hn, z=z, o=o, lse=lse, pooled=pooled, wp=wp, mixed=mixed, ab=ab), 0.0


def _even_bwd(dx1, sv, W, G, i, send):
    x, hn, z = sv["x"], sv["hn"], sv["z"]
    dab = _mm("mm_out_ab_dx", dx1, W["w_out_ab"][i], tb=True)
    G["w_out_ab"][i] = _mm("mm_out_ab_dw", sv["ab"], dx1, ta=True, out_dtype=BF16)
    scale = _row2(W["pool_scale"][i]) + send(0)
    (do, dga, dmixed, dgb), (dscale,) = _rw_bwd(
        "gate_ab_bwd", _f_gate_ab, [_cols(sv["o"]), _cols(z, 3072, 1024), _cols(sv["mixed"]), _cols(z, 5120, 1024)], [_par(scale)],
        [_cols(dab, nsplit=2)], [(0, F32), (1, BF16), (2, BF16), (3, BF16)], [0], 256)
    G["pool_scale"][i] = dscale.reshape(-1)
    dpooled = _mm("mm_pool_dx", dmixed, sv["wp"], tb=True)
    dwp = _mm("mm_pool_dw", sv["pooled"], dmixed, ta=True, out_dtype=BF16)
    G["pool_w"][i] = jnp.stack([dwp[g * 256:(g + 1) * 256, g * 256:(g + 1) * 256] for g in range(4)])
    dvb = _pool("pool_bwd", dpooled, 0, True, BF16)
    dq, dk, dv = _attn_bwd("attn_self_bwd", z, z, z, do, sv["o"], sv["lse"], out_dtype=BF16, **_SELF)
    dz = jnp.concatenate([dq, dk, dv, dga, dvb, dgb], axis=1)
    dhn = _mm("mm_in_ab_dx", dz, W["w_in_ab"][i], tb=True)
    G["w_in_ab"][i] = _mm("mm_in_ab_dw", hn, dz, ta=True, out_dtype=BF16, out_stack=W["w_in_ab"][i].shape[2])
    g = _row2(W["norm_ab"][i]) + send(1)
    (dx,), (dg,) = _rw_bwd("rms_bwd", _f_rms_res, [_cols(x)], [_par(g)], [_cols(dx1), _cols(dhn)], [(0, F32)], [0], 256)
    G["norm_ab"][i] = dg.reshape(-1)
    return dx


def _odd_fwd(x, W, i, zero, rest_of_weights, start_ahead):
    hn = _rw_fwd("rms_fwd", _f_rms, [_cols(x)], [_par(_row2(W["norm_cd"][i]) + zero)], [(D_MODEL, BF16, 1)], 256)[0]
    z = _mm("mm_in_cd", hn, W["w_in_cd"][i], out_dtype=BF16)
    sgu_p = [_par(_row2(W["sgu_ln_g"][i]), 4), _par(_row2(W["sgu_ln_b"][i]), 4), _par(W["sgu_w"][i], 4), _par(W["sgu_b"][i][..., None], 4)]
    c_out = _rw_fwd("sgu_fwd", _f_sgu, [_cols(z, 0, 1024, 4), _cols(z, 1024, 1024, 4), _cols(z, 2048, 1024, 4)], sgu_p, [(1024, BF16, 4)], 128)[0]
    prep_rows = [_cols(W["s5_a_re"][i]), _cols(W["s5_a_im"][i]), _cols(W["s5_log_dt"][i].reshape(S5_GROUPS, 1))]
    abar_re, abar_im, coef_re, coef_im = _rw_fwd("s5_prep_fwd", _f_s5_prep, prep_rows, [], [(S5_STATE, F32, 1)] * 4, S5_GROUPS)
    bb_rows = [_cols(W["s5_b_re"][i].reshape(2048, 16)), _cols(W["s5_b_im"][i].reshape(2048, 16)), _cols(coef_re.reshape(2048, 1)), _cols(coef_im.reshape(2048, 1))]
    bb_re, bb_im = _rw_fwd("s5_bbar_fwd", _f_bbar, bb_rows, [], [(16, F32, 1)] * 2, 256)
    bd = _bdense(bb_re, bb_im).astype(BF16)
    cf = _cdense(W["s5_c_re"][i], W["s5_c_im"][i]).astype(BF16)
    a2 = jnp.concatenate([abar_re.reshape(16, 128), abar_im.reshape(16, 128)], axis=0)
    bu = _s5_bu(z, bd)
    h3 = _scan_fwd(bu.reshape(SEQ, 32, 128), a2)
    h2 = h3.reshape(SEQ, 4096)
    yc = _s5_y(h2, cf)
    dpar = _row2(W["s5_d"][i]) + start_ahead(h3)
    yg = _rw_fwd("gelu_fwd", _f_gelu_y, [_cols(yc), _cols(z, 3072, 512)], [_par(dpar)], [(512, BF16, 1)], 256)[0]
    zero = rest_of_weights(yg)
    w12 = jnp.concatenate([W["glu_w1"][i], W["glu_w2"][i]], axis=1)
    t12 = _mm("mm_glu", yg, w12)
    d_out = _rw_fwd("glu_gate_fwd", _f_glu_gate, [_cols(t12, nsplit=2), _cols(z, 3584, 512)], [], [(512, BF16, 1)], 256)[0]
    cd = jnp.concatenate([c_out, d_out], axis=1)
    x1 = _mm("mm_out_cd", cd, W["w_out_cd"][i], res=x)
    sv = dict(x=x, hn=hn, z=z, sgu_p=sgu_p, prep_rows=prep_rows, bb_rows=bb_rows, bb=(bb_re, bb_im), bd=bd, cf=cf, a2=a2,
              h3=h3, h2=h2, yc=yc, dpar=dpar, yg=yg, w12=w12, t12=t12, cd=cd)
    return x1, sv, zero


def _odd_bwd(dx1, sv, W, G, i, send):
    x, hn, z = sv["x"], sv["hn"], sv["z"]
    dcd = _mm("mm_out_cd_dx", dx1, W["w_out_cd"][i], tb=True)
    G["w_out_cd"][i] = _mm("mm_out_cd_dw", sv["cd"], dx1, ta=True, out_dtype=BF16)
    (du, dv, dgc), (dlg, dlb, dsw, dsb) = _rw_bwd(
        "sgu_bwd", _f_sgu, [_cols(z, 0, 1024, 4), _cols(z, 1024, 1024, 4), _cols(z, 2048, 1024, 4)], sv["sgu_p"],
        [_cols(dcd, 0, 1024, 4)], [(0, BF16), (1, BF16), (2, BF16)], [0, 1, 2, 3], 128)
    G["sgu_ln_g"][i], G["sgu_ln_b"][i] = dlg.reshape(-1), dlb.reshape(-1)
    G["sgu_w"][i], G["sgu_b"][i] = dsw, dsb[..., 0]
    (dt12, dgd), _ = _rw_bwd("glu_gate_bwd", _f_glu_gate, [_cols(sv["t12"], nsplit=2), _cols(z, 3584, 512)], [], [_cols(dcd, 1024, 512)],
                             [(0, BF16), (1, BF16)], [], 256)
    dyg = _mm("mm_glu_dx", dt12, sv["w12"], tb=True)
    dw12 = _mm("mm_glu_dw", sv["yg"], dt12, ta=True, out_dtype=BF16)
    G["glu_w1"][i], G["glu_w2"][i] = dw12[:, :512], dw12[:, 512:]
    (dyc, duf1), (dd,) = _rw_bwd("gelu_bwd", _f_gelu_y, [_cols(sv["yc"]), _cols(z, 3072, 512)], [_par(sv["dpar"])], [_cols(dyg)],
                                 [(0, BF16), (1, F32)], [0], 256)
    G["s5_d"][i] = dd.reshape(-1)
    dh2 = _s5_y_dx(dyc, sv["cf"])
    dcf = _s5_y_dw(sv["h2"], dyc)
    _, cvjp = jax.vjp(_cdense, W["s5_c_re"][i], W["s5_c_im"][i])
    G["s5_c_re"][i], G["s5_c_im"][i] = cvjp(dcf)
    dbu3, da2 = _scan_bwd(dh2.reshape(SEQ, 32, 128), sv["h3"], sv["a2"])
    dbu = dbu3.reshape(SEQ, 4096)
    duf2 = _s5_bu_dx(dbu, sv["bd"])
    dbd = _s5_bu_dw(z, dbu)
    _, bvjp = jax.vjp(_bdense, *sv["bb"])
    dbb_re, dbb_im = bvjp(dbd)
    (dbr, dbi, dcr, dci), _ = _rw_bwd("s5_bbar_bwd", _f_bbar, sv["bb_rows"], [], [_cols(dbb_re), _cols(dbb_im)],
                                      [(0, F32), (1, F32), (2, F32), (3, F32)], [], 256)
    G["s5_b_re"][i], G["s5_b_im"][i] = dbr.reshape(S5_GROUPS, S5_STATE, S5_GROUP_DIM), dbi.reshape(S5_GROUPS, S5_STATE, S5_GROUP_DIM)
    douts = [_cols(da2[0:16].reshape(S5_GROUPS, S5_STATE)), _cols(da2[16:32].reshape(S5_GROUPS, S5_STATE)),
             _cols(dcr.reshape(S5_GROUPS, S5_STATE)), _cols(dci.reshape(S5_GROUPS, S5_STATE))]
    (dar, dai, dldt), _ = _rw_bwd("s5_prep_bwd", _f_s5_prep, sv["prep_rows"], [], douts, [(0, F32), (1, F32), (2, F32)], [], S5_GROUPS)
    G["s5_a_re"][i], G["s5_a_im"][i], G["s5_log_dt"][i] = dar, dai, dldt.reshape(-1)
    dxd = (duf1 + duf2).astype(BF16)
    dz = jnp.concatenate([du, dv, dgc, dxd, dgd], axis=1)
    dhn = _mm("mm_in_cd_dx", dz, W["w_in_cd"][i], tb=True)
    G["w_in_cd"][i] = _mm("mm_in_cd_dw", hn, dz, ta=True, out_dtype=BF16, out_stack=W["w_in_cd"][i].shape[2])
    g = _row2(W["norm_cd"][i]) + send(0)
    (dx,), (dg,) = _rw_bwd("rms_bwd", _f_rms_res, [_cols(x)], [_par(g)], [_cols(dx1), _cols(dhn)], [(0, F32)], [0], 256)
    G["norm_cd"][i] = dg.reshape(-1)
    return dx


def _cross_fwd(x1, mem_n, W, l, zero):
    hx = _rw_fwd("rms_fwd", _f_rms, [_cols(x1)], [_par(_row2(W["norm_x"][l]) + zero)], [(D_MODEL, BF16, 1)], 256)[0]
    qx = _mm("mm_xq", hx, W["w_xq"][l], out_dtype=BF16)
    kv = _mm("mm_xkv", mem_n, W["w_xkv"][l], out_dtype=BF16)
    ox, lse = _attn_fwd("attn_cross_fwd", qx, kv, kv, out_dtype=BF16, **_CROSS)
    x2 = _mm("mm_xo", ox, W["w_xo"][l], res=x1)
    return x2, dict(x1=x1, hx=hx, qx=qx, kv=kv, ox=ox, lse=lse)


def _cross_bwd(dx2, dmem_n, sv, mem_n, W, G, l, zero):
    dox = _mm("mm_xo_dx", dx2, W["w_xo"][l], tb=True, out_dtype=BF16)
    G["w_xo"][l] = _mm("mm_xo_dw", sv["ox"], dx2, ta=True, out_dtype=BF16)
    dqx, dk, dv = _attn_bwd("attn_cross_bwd", sv["qx"], sv["kv"], sv["kv"], dox, sv["ox"], sv["lse"], out_dtype=BF16, **_CROSS)
    dkv = jnp.concatenate([dk, dv], axis=1)
    dhx = _mm("mm_xq_dx", dqx, W["w_xq"][l], tb=True)
    G["w_xq"][l] = _mm("mm_xq_dw", sv["hx"], dqx, ta=True, out_dtype=BF16)
    dmem_n = _mm("mm_xkv_dx", dkv, W["w_xkv"][l], tb=True, res=dmem_n)
    G["w_xkv"][l] = _mm("mm_xkv_dw", mem_n, dkv, ta=True, out_dtype=BF16, out_stack=W["w_xkv"][l].shape[2])
    (dx1,), (dg,) = _rw_bwd("rms_bwd", _f_rms_res, [_cols(sv["x1"])], [_par(_row2(W["norm_x"][l]) + zero)], [_cols(dx2), _cols(dhx)], [(0, F32)], [0], 256)
    G["norm_x"][l] = dg.reshape(-1)
    return dx1, dmem_n


_PER_LAYER = ("pool_scale", "norm_ab", "norm_cd", "sgu_ln_g", "sgu_ln_b", "sgu_w", "sgu_b", "s5_d", "s5_c_re", "s5_c_im", "s5_b_re", "s5_b_im",
              "s5_a_re", "s5_a_im", "s5_log_dt", "w_in_ab", "pool_w", "w_out_ab", "w_in_cd", "glu_w1", "glu_w2", "w_out_cd")


def _local_step(x, mem, target, W, weights_of, start_ahead, send_grads, after_layer):
    G = {k: [None, None] for k in _PER_LAYER}
    for k in ("norm_x", "w_xq", "w_xkv", "w_xo"):
        G[k] = [None] * DEPTH
    mem_rows = [_cols(mem)]
    mem_par = [_par(_row2(W["mem_norm"]))]
    mem_n = _rw_fwd("rms_fwd_mem", _f_rms, mem_rows, mem_par, [(D_MODEL, BF16, 1)], 256)[0]
    saved = []
    for layer in range(DEPTH):
        zero = weights_of(layer, 0, x if layer else mem_n)
        mixer = _even_fwd if layer % 2 == 0 else _odd_fwd
        x, sv, zero = mixer(x, W, layer // 2, zero, functools.partial(weights_of, layer, 1), functools.partial(start_ahead, layer))
        x, svx = _cross_fwd(x, mem_n, W, layer, zero)
        saved.append((sv, svx))
    loss, dx, dfinal = _loss_head(x, target, _row2(W["final_norm"]))
    G["final_norm"] = dfinal.reshape(-1)
    dmem_n, zero = None, 0.0
    for layer in reversed(range(DEPTH)):
        sv, svx = saved[layer]
        dx, dmem_n = _cross_bwd(dx, dmem_n, svx, mem_n, W, G, layer, zero)
        hook = functools.partial(send_grads, layer, G)
        dx = _even_bwd(dx, sv, W, G, layer // 2, hook) if layer % 2 == 0 else _odd_bwd(dx, sv, W, G, layer // 2, hook)
        zero = after_layer(layer, G)
    _, (dmn,) = _rw_bwd("rms_bwd_mem", _f_rms, mem_rows, mem_par, [_cols(dmem_n)], [], [0], 256)
    G["mem_norm"] = dmn.reshape(-1)
    return loss, dx, G


_HBM = pl.BlockSpec(memory_space=pltpu.HBM)
_ANY = pl.BlockSpec(memory_space=pl.ANY)
_SEM = pl.BlockSpec(memory_space=pltpu.SEMAPHORE)
_N_PEERS = N_DEV - 1


def _mesh_pos():
    return lax.axis_index("x"), lax.axis_index("y"), lax.axis_index("c")


def _peer(pos, k):
    x, y, c = pos
    return (x ^ ((k >> 2) & 1), y ^ ((k >> 1) & 1), c ^ (k & 1))


def _lin(pos):
    return 4 * pos[0] + 2 * pos[1] + pos[2]


def _ends(gather, srcs, lands, t, sender, receiver):
    if gather:
        return lands[t].at[sender], lands[t].at[sender]
    whole = len(srcs[t].shape) != len(lands[t].shape)
    return (srcs[t] if whole else srcs[t].at[receiver]), lands[t].at[sender]


def _into_slot(name, b2, r0, r, me, dtype, after):
    c = b2.shape[1]
    tr = _row_block(r, c, 2 << 20)
    assert r0 % tr == 0

    def body(me_ref, x_ref, *rest):
        rest[-1][...] = x_ref[...].astype(dtype)

    extra = [] if after is None else [after]
    grid_spec = pltpu.PrefetchScalarGridSpec(
        num_scalar_prefetch=1, grid=(r // tr,),
        in_specs=[pl.BlockSpec((tr, c), lambda i, me, o=r0 // tr: (o + i, 0))] + [_ANY] * len(extra),
        out_specs=pl.BlockSpec((None, tr, c), lambda i, me: (me[0], i, 0)))
    return pl.pallas_call(
        body, name=name, grid_spec=grid_spec, out_shape=jax.ShapeDtypeStruct((N_DEV, r, c), dtype),
        compiler_params=pltpu.CompilerParams(dimension_semantics=("arbitrary",), vmem_limit_bytes=V7X_VMEM_LIMIT_BYTES),
        interpret=False,
    )(me, b2, *extra)


def _exchange_start(name, gather, srcs, lands, after=None):
    ns, nt = len(srcs), len(lands)
    arrs = list(srcs) + list(lands)
    extra = [] if after is None else [after]

    def body(*refs):
        ins, lnd = refs[:ns], refs[ns:ns + nt]
        refs = refs[len(extra):]
        send_sems, recv_sems = refs[ns + nt], refs[ns + nt + 1]
        token = refs[-1]
        pos = _mesh_pos()
        me = _lin(pos)
        for k in range(1, N_DEV):
            peer = _peer(pos, k)
            for t in range(nt):
                src, dst = _ends(gather, ins, lnd, t, me, _lin(peer))
                pltpu.make_async_remote_copy(
                    src_ref=src, dst_ref=dst, send_sem=send_sems.at[t * _N_PEERS + k - 1], recv_sem=recv_sems.at[t * _N_PEERS + k - 1],
                    device_id=peer, device_id_type=pl.DeviceIdType.MESH).start()
        token[...] = jnp.zeros_like(token)

    out = pl.pallas_call(
        body, name=name,
        out_shape=(pltpu.SemaphoreType.DMA((nt * _N_PEERS,)), pltpu.SemaphoreType.DMA((nt * _N_PEERS,)), *[pltpu.HBM(a.shape, a.dtype) for a in arrs],
                   jax.ShapeDtypeStruct((8, 128), F32)),
        in_specs=[_HBM] * (ns + nt) + [_ANY] * len(extra), out_specs=(_SEM, _SEM, *[_HBM] * (ns + nt), pl.BlockSpec(memory_space=pltpu.VMEM)),
        input_output_aliases={j: 2 + j for j in range(ns + nt)},
        compiler_params=pltpu.CompilerParams(has_side_effects=pltpu.SideEffectType.DATAFLOW_SIDE_EFFECTING),
        interpret=False,
    )(*[pltpu.with_memory_space_constraint(a, pltpu.HBM) for a in arrs], *extra)
    return dict(send=out[0], recv=out[1], srcs=list(out[2:2 + ns]), lands=list(out[2 + ns:2 + ns + nt]), token=out[-1][0, 0], token_arr=out[-1], gather=gather)


def _exchange_wait(name, ex, after):
    ns, nt = len(ex["srcs"]), len(ex["lands"])
    gather = ex["gather"]
    arrs = ex["srcs"] + ex["lands"]
    after = list(after) if isinstance(after, (list, tuple)) else [after]

    def body(*refs):
        ins, lnd = refs[:ns], refs[ns:ns + nt]
        send_sems, recv_sems = refs[ns + nt], refs[ns + nt + 1]
        pos = _mesh_pos()
        me = _lin(pos)
        for k in range(1, N_DEV):
            peer = _peer(pos, k)
            for t in range(nt):
                src, _ = _ends(gather, ins, lnd, t, me, _lin(peer))
                _, dst = _ends(gather, ins, lnd, t, _lin(peer), me)
                cp = pltpu.make_async_remote_copy(
                    src_ref=src, dst_ref=dst, send_sem=send_sems.at[t * _N_PEERS + k - 1], recv_sem=recv_sems.at[t * _N_PEERS + k - 1],
                    device_id=peer, device_id_type=pl.DeviceIdType.MESH)
                cp.wait_send()
                cp.wait_recv()

    out = pl.pallas_call(
        body, name=name, out_shape=tuple(pltpu.HBM(a.shape, a.dtype) for a in arrs),
        in_specs=[_HBM] * (ns + nt) + [_SEM, _SEM] + [_ANY] * len(after), out_specs=tuple([_HBM] * (ns + nt)),
        input_output_aliases={j: j for j in range(ns + nt)},
        compiler_params=pltpu.CompilerParams(has_side_effects=pltpu.SideEffectType.DATAFLOW_SIDE_EFFECTING),
        interpret=False,
    )(*arrs, ex["send"], ex["recv"], *after)
    return list(out[:ns]), list(out[ns:])


def _scatter_begin(name, srcs):
    lands = [lax.empty(s.shape if s.ndim == 3 else (N_DEV,) + s.shape, s.dtype) for s in srcs]
    return _exchange_start(name, False, srcs, lands)


def _adam(name, w, m, v, parts, own, me, layer, bufs):
    r, c = parts.shape[1:]
    tr = _row_block(r, max(c, 128), 1 << 20)
    nb = r // tr

    def body(me_ref, w_ref, m_ref, v_ref, p_ref, own_ref, *rest):
        g_ref, d_ref, nm_ref, nv_ref, acc = rest[-5:]
        acc[...] = jnp.zeros_like(acc)
        for k in range(N_DEV):
            @pl.when(me_ref[0] == k)
            def _():
                acc[...] += own_ref[...].astype(F32)

            @pl.when(me_ref[0] != k)
            def _(k=k):
                acc[...] += p_ref[k].astype(F32)

        g = acc[...]
        mm = ADAM_B1 * m_ref[...] + (1.0 - ADAM_B1) * g
        vv = ADAM_B2 * v_ref[...] + (1.0 - ADAM_B2) * jnp.square(g)
        m_hat = mm / (1.0 - ADAM_B1 ** ADAM_STEP)
        v_hat = vv / (1.0 - ADAM_B2 ** ADAM_STEP)
        g_ref[...] = g
        d_ref[...] = -ADAM_LR * (m_hat / (jnp.sqrt(v_hat) + ADAM_EPS) + ADAM_WD * w_ref[...])
        nm_ref[...] = mm
        nv_ref[...] = vv

    blk = pl.BlockSpec((tr, c), lambda i, me, o=layer * nb: (o + i, 0))
    own_spec = pl.BlockSpec((None, tr, c), lambda i, me: (me[0], i, 0)) if own.ndim == 3 else pl.BlockSpec((tr, c), lambda i, me: (i, 0))
    in_specs = [blk, blk, blk, pl.BlockSpec((N_DEV, tr, c), lambda i, me: (0, i, 0)), own_spec]
    args = [me, w, m, v, parts, own]
    aliases = {}
    if bufs is not None:
        in_specs += [_ANY] * 4
        aliases = {len(args) + j: j for j in range(4)}
        args += list(bufs)
    grid_spec = pltpu.PrefetchScalarGridSpec(
        num_scalar_prefetch=1, grid=(nb,), in_specs=in_specs, out_specs=[blk] * 4, scratch_shapes=[pltpu.VMEM((tr, c), F32)])
    return pl.pallas_call(
        body, name=name, grid_spec=grid_spec, out_shape=[jax.ShapeDtypeStruct(w.shape, F32)] * 4, input_output_aliases=aliases,
        compiler_params=pltpu.CompilerParams(dimension_semantics=("arbitrary",), vmem_limit_bytes=V7X_VMEM_LIMIT_BYTES),
        interpret=False,
    )(*args)


def _row_block(r, c, limit):
    best = None
    for tr in range(16, r + 1, 16):
        if r % tr == 0 and tr * c * 4 <= limit:
            best = tr
    return r if best is None else best


_BIG = ("w_in_ab", "pool_w", "w_out_ab", "w_in_cd", "glu_w1", "glu_w2", "w_out_cd", "w_xq", "w_xkv", "w_xo")
_STACKED = ("w_in_ab", "w_in_cd", "w_xkv")
_MIXER_BIG = (("w_in_ab", "pool_w", "w_out_ab"), ("w_in_cd", "glu_w1", "glu_w2", "w_out_cd"))
_CROSS_BIG = ("w_xq", "w_xkv", "w_xo")
_SMALL_SPLIT = ["norm_cd", "sgu_ln_g", "sgu_ln_b", "s5_d"]
_REPLICATED_ODD = ["sgu_w", "sgu_b", "s5_a_re", "s5_a_im", "s5_log_dt", "s5_b_re", "s5_b_im", "s5_c_re", "s5_c_im", "final_norm"]
_REPLICATED_EVEN = ["norm_ab", "pool_scale", "norm_x", "mem_norm"]
_REPLICATED = _REPLICATED_ODD + _REPLICATED_EVEN
_WEIGHTS = ["norm_ab", "w_in_ab", "pool_w", "pool_scale", "w_out_ab", "norm_cd", "w_in_cd", "sgu_ln_g", "sgu_ln_b", "sgu_w", "sgu_b", "s5_a_re",
            "s5_a_im", "s5_log_dt", "s5_b_re", "s5_b_im", "s5_c_re", "s5_c_im", "s5_d", "glu_w1", "glu_w2", "w_out_cd", "norm_x", "w_xq",
            "w_xkv", "w_xo", "mem_norm", "final_norm"]


def _layer_big(layer):
    return [(n, layer // 2) for n in _MIXER_BIG[layer % 2]] + [(n, layer) for n in _CROSS_BIG]


def _gather_parts(layer):
    big = _layer_big(layer)
    return [big[:1], big[1:]]


def _scatter_parts(layer):
    big = _layer_big(layer)
    return [big[2:], big[:2]] if layer % 2 == 0 else [big]


def _from_slots(name, a):
    if name in _STACKED:
        return a
    if name == "pool_w":
        return a.reshape(N_DEV, 4, 32, 256).transpose(1, 0, 2, 3).reshape(4, 256, 256)
    return a.reshape(-1, a.shape[-1])


def _to_slots(name, g):
    if name in _STACKED:
        return g
    if name == "pool_w":
        return g.reshape(4, N_DEV, 32, 256).transpose(1, 0, 2, 3).reshape(N_DEV, 128, 256)
    return g.reshape(N_DEV, -1, g.shape[-1])


def _rows2d(a):
    return a.reshape(-1, a.shape[-1])


def _small_rows(block):
    return jnp.pad(block, ((0, 0), (0, 128 - block.shape[1])))


def kernel(x, mem, norm_ab, w_in_ab, pool_w, pool_scale, w_out_ab, norm_cd, w_in_cd, sgu_ln_g, sgu_ln_b, sgu_w, sgu_b, s5_a_re, s5_a_im, s5_log_dt, s5_b_re, s5_b_im, s5_c_re, s5_c_im, s5_d, glu_w1, glu_w2, w_out_cd, norm_x, w_xq, w_xkv, w_xo, mem_norm, final_norm, loss_target, m_norm_ab, m_w_in_ab, m_pool_w, m_pool_scale, m_w_out_ab, m_norm_cd, m_w_in_cd, m_sgu_ln_g, m_sgu_ln_b, m_sgu_w, m_sgu_b, m_s5_a_re, m_s5_a_im, m_s5_log_dt, m_s5_b_re, m_s5_b_im, m_s5_c_re, m_s5_c_im, m_s5_d, m_glu_w1, m_glu_w2, m_w_out_cd, m_norm_x, m_w_xq, m_w_xkv, m_w_xo, m_mem_norm, m_final_norm, v_norm_ab, v_w_in_ab, v_pool_w, v_pool_scale, v_w_out_ab, v_norm_cd, v_w_in_cd, v_sgu_ln_g, v_sgu_ln_b, v_sgu_w, v_sgu_b, v_s5_a_re, v_s5_a_im, v_s5_log_dt, v_s5_b_re, v_s5_b_im, v_s5_c_re, v_s5_c_im, v_s5_d, v_glu_w1, v_glu_w2, v_w_out_cd, v_norm_x, v_w_xq, v_w_xkv, v_w_xo, v_mem_norm, v_final_norm):
    args = locals()
    w = {n: args[n] for n in _WEIGHTS}
    m = {n: args["m_" + n] for n in _WEIGHTS}
    v = {n: args["v_" + n] for n in _WEIGHTS}

    me = jnp.reshape(_lin(_mesh_pos()), (1,)).astype(jnp.int32)

    order = [(layer, p) for layer in range(DEPTH) for p in range(len(_gather_parts(layer)))]
    lands, gathers = {}, {}
    for key in order:
        lands[key] = []
        for name, i in _gather_parts(key[0])[key[1]]:
            b2 = _rows2d(w[name])
            r = b2.shape[0] // w[name].shape[0]
            lands[key].append(_into_slot("cast_" + name, b2, i * r, r, me, BF16, None))
    small_blocks = jnp.concatenate([_small_rows(w[n]) for n in _SMALL_SPLIT], axis=0)
    lands[order[0]].append(_into_slot("cast_small", small_blocks, 0, 8, me, F32, None))

    def begin_gather(key, after):
        gathers[key] = _exchange_start("gather%d%s_start" % (key[0], "ab"[key[1]]), True, [], lands[key], after)
        return gathers[key]["token"]

    W = {n: w[n] for n in _REPLICATED}
    W["mem_norm"] = w["mem_norm"] + begin_gather(order[0], None)
    for name in _BIG:
        W[name] = [None] * w[name].shape[0]

    def weights_of(layer, part, after):
        key = (layer, part)
        if key not in gathers:
            return 0.0
        if key == order[0]:
            after = [after] + [a for k in order[1:] for a in lands[k]]
        _, got = _exchange_wait("gather%d%s_wait" % (layer, "ab"[part]), gathers[key], after)
        for (name, i), arr in zip(_gather_parts(layer)[part], got):
            W[name][i] = _from_slots(name, arr)
        if key == order[0]:
            sm = got[-1].reshape(N_DEV, 4, 2, 128)
            for j, n in enumerate(_SMALL_SPLIT):
                width = w[n].shape[1]
                W[n] = sm[:, j, :, :width].transpose(1, 0, 2).reshape(2, N_DEV * width)
        nxt = order.index(key) + 1
        return begin_gather(order[nxt], got[0]) if nxt < len(order) and order[nxt] not in gathers else 0.0

    def start_ahead(layer, after):
        return begin_gather((layer + 1, 0), after) if layer + 1 < DEPTH else 0.0

    scatters, small = {}, {}

    def send_grads(layer, G, part):
        srcs = [_to_slots(name, G[name][i]) for name, i in _scatter_parts(layer)[part]]
        scatters[layer, part] = _scatter_begin("scatter%d%s_start" % (layer, "ab"[part]), srcs)
        return scatters[layer, part]["token"]

    def rows(a):
        if a.ndim == 1:
            return a.reshape(1, -1)
        return a.reshape(-1, 128) if a.shape[-1] < 128 and a.size % 128 == 0 else a.reshape(-1, a.shape[-1])

    def begin_small(tag, G, names, split):
        srcs = [rows(G[n] if n in ("mem_norm", "final_norm") else jnp.stack(G[n])) for n in names]
        if split:
            srcs += [jnp.stack(G[n]).reshape(2, N_DEV, -1).transpose(1, 0, 2) for n in _SMALL_SPLIT]
        small[tag] = _scatter_begin("scatter_small_%s_start" % tag, srcs)
        return small[tag]["token"]

    def after_layer(layer, G):
        return begin_small("odd", G, _REPLICATED_ODD, True) if layer == 1 else 0.0

    loss, dx, G = _local_step(x[0], mem[0], loss_target[0], W, weights_of, start_ahead, send_grads, after_layer)
    loss = lax.psum(loss[0, 0], MESH_AXES)
    begin_small("even", G, _REPLICATED_EVEN, False)

    out = {}
    after = small["even"]["token_arr"]
    for layer, part in scatters:
        own, got = _exchange_wait("scatter%d%s_wait" % (layer, "ab"[part]), scatters[layer, part], after)
        for (name, i), mine, parts in zip(_scatter_parts(layer)[part], own, got):
            out[name] = _adam("adam_" + name, _rows2d(w[name]), _rows2d(m[name]), _rows2d(v[name]), parts, mine, me, i, out.get(name))
        after = [out[name][0] for name, i in _scatter_parts(layer)[part]]
    for name in _BIG:
        out[name] = [a.reshape(w[name].shape) for a in out[name]]
    for tag, names in (("odd", _REPLICATED_ODD + _SMALL_SPLIT), ("even", _REPLICATED_EVEN)):
        own, got = _exchange_wait("scatter_small_%s_wait" % tag, small[tag], after)
        for n, mine, parts in zip(names, own, got):
            as2d = (lambda a: a) if n in _SMALL_SPLIT else rows
            res = _adam("adam_" + n, as2d(w[n]), as2d(m[n]), as2d(v[n]), parts, mine, me, 0, None)
            out[n] = [a.reshape(w[n].shape) for a in res]
        after = [out[n][0] for n in names]

    return (loss, dx[None], *[out[n][0] for n in _WEIGHTS], *[out[n][1] for n in _WEIGHTS], *[out[n][2] for n in _WEIGHTS],
            *[out[n][3] for n in _WEIGHTS])
```

```python
import functools
import math

import jax
import jax.numpy as jnp
from jax import lax
from jax.experimental import pallas as pl
from jax.experimental.pallas import tpu as pltpu

F32 = jnp.float32
BF16 = jnp.bfloat16

SEQ = 2048
D_MODEL = 1024
MEM_LEN = 256
DEPTH = 4
N_DEV = 8
EPS = 1e-6
NEG = -1e30
A_HEAD_DIM = 64
X_HEAD_DIM = 256
S5_GROUPS = 32
S5_STATE = 64
S5_GROUP_DIM = 16

ADAM_LR = 0.001
ADAM_B1 = 0.9
ADAM_B2 = 0.999
ADAM_EPS = 1e-08
ADAM_WD = 0.01
ADAM_STEP = 10

V7X_VMEM_LIMIT_BYTES = 56 * 1024 * 1024
_MM_VMEM_BYTES = 36 * 1024 * 1024
MESH_AXES = ("x", "y", "c")


def _pc(body, *, name, out_shape, grid=None, in_specs=None, out_specs=None, scratch_shapes=(), aliases=None, sem=None):
    kw = {}
    if grid is not None:
        kw["grid"] = grid
    if in_specs is not None:
        kw["in_specs"] = in_specs
    if out_specs is not None:
        kw["out_specs"] = out_specs
    if aliases:
        kw["input_output_aliases"] = aliases
    return pl.pallas_call(
        body,
        name=name,
        out_shape=out_shape,
        scratch_shapes=list(scratch_shapes),
        compiler_params=pltpu.CompilerParams(dimension_semantics=sem, vmem_limit_bytes=V7X_VMEM_LIMIT_BYTES),
        interpret=False,
        **kw,
    )


def _cols(arr, c0=0, width=None, nsplit=1, r0=0):
    width = arr.shape[1] - c0 if width is None else width
    assert c0 % width == 0 and width % nsplit == 0
    return (arr, c0, width, nsplit, r0)


def _par(arr, nsplit=1):
    return (arr, nsplit)


def _ld(ref, nsplit):
    if nsplit == 1:
        return ref[...].astype(F32)
    if len(ref.shape) == 3:
        return tuple(ref[k].astype(F32) for k in range(nsplit))
    w = ref.shape[-1] // nsplit
    return tuple(ref[:, k * w:(k + 1) * w].astype(F32) for k in range(nsplit))


def _st(ref, val, nsplit, accumulate=False):
    if nsplit == 1:
        val = (val,)
    for k in range(nsplit):
        if nsplit == 1:
            idx = (Ellipsis,)
        elif len(ref.shape) == 3:
            idx = (k,)
        else:
            w = ref.shape[-1] // nsplit
            idx = (slice(None), slice(k * w, (k + 1) * w))
        if accumulate:
            ref[idx] += val[k].astype(ref.dtype)
        else:
            ref[idx] = val[k].astype(ref.dtype)


def _row_spec(tr, op):
    _, c0, w, _, r0 = op
    assert r0 % tr == 0
    return pl.BlockSpec((tr, w), lambda i, cb=c0 // w, rb=r0 // tr: (i + rb, cb))


def _full_spec(arr):
    return pl.BlockSpec(arr.shape, lambda i, nd=arr.ndim: (0,) * nd)


def _rw_fwd(name, f, rows, pars, outs, tr, n_rows=None):
    n_rows = rows[0][0].shape[0] if n_rows is None else n_rows
    nr, npar = len(rows), len(pars)

    def body(*refs):
        r = [_ld(refs[i], rows[i][3]) for i in range(nr)]
        p = [_ld(refs[nr + i], pars[i][1]) for i in range(npar)]
        res = f(r, p)
        for k, (_, _, ns) in enumerate(outs):
            _st(refs[nr + npar + k], res[k], ns)

    res = _pc(
        body, name=name, grid=(n_rows // tr,),
        in_specs=[_row_spec(tr, op) for op in rows] + [_full_spec(a) for a, _ in pars],
        out_specs=[pl.BlockSpec((tr, w), lambda i: (i, 0)) for w, _, _ in outs],
        out_shape=[jax.ShapeDtypeStruct((n_rows, w), dt) for w, dt, _ in outs],
        sem=("arbitrary",),
    )(*[op[0] for op in rows], *[a for a, _ in pars])
    return list(res)


def _rw_bwd(name, f, rows, pars, douts, drow, dpar, tr):
    n_rows = rows[0][0].shape[0]
    nr, npar = len(rows), len(pars)
    dgiven = [d for d in douts if d is not None]
    nd = len(dgiven)

    def body(*refs):
        r = [_ld(refs[i], rows[i][3]) for i in range(nr)]
        p = [_ld(refs[nr + i], pars[i][1]) for i in range(npar)]
        d = [_ld(refs[nr + npar + i], dgiven[i][3]) for i in range(nd)]
        orefs = refs[nr + npar + nd:]

        def g(dr, dp):
            rr, pp = list(r), list(p)
            for j, (idx, _) in enumerate(drow):
                rr[idx] = dr[j]
            for j, idx in enumerate(dpar):
                pp[idx] = dp[j]
            return tuple(f(rr, pp))

        out, vjp = jax.vjp(g, [r[idx] for idx, _ in drow], [p[idx] for idx in dpar])
        ct, j = [], 0
        for k, o in enumerate(out):
            if douts[k] is None:
                ct.append(jax.tree.map(jnp.zeros_like, o))
            else:
                ct.append(d[j])
                j += 1
        gdr, gdp = vjp(tuple(ct))
        for j, (idx, _) in enumerate(drow):
            _st(orefs[j], gdr[j], rows[idx][3])

        @pl.when(pl.program_id(0) == 0)
        def _():
            for j in range(len(dpar)):
                orefs[len(drow) + j][...] = jnp.zeros_like(orefs[len(drow) + j])

        for j, idx in enumerate(dpar):
            _st(orefs[len(drow) + j], gdp[j], pars[idx][1], accumulate=True)

    res = _pc(
        body, name=name, grid=(n_rows // tr,),
        in_specs=[_row_spec(tr, op) for op in rows] + [_full_spec(a) for a, _ in pars] + [_row_spec(tr, op) for op in dgiven],
        out_specs=[pl.BlockSpec((tr, rows[idx][2]), lambda i: (i, 0)) for idx, _ in drow] + [_full_spec(pars[idx][0]) for idx in dpar],
        out_shape=[jax.ShapeDtypeStruct((n_rows, rows[idx][2]), dt) for idx, dt in drow]
        + [jax.ShapeDtypeStruct(pars[idx][0].shape, F32) for idx in dpar],
        sem=("arbitrary",),
    )(*[op[0] for op in rows], *[a for a, _ in pars], *[op[0] for op in dgiven])
    res = list(res)
    return res[:len(drow)], res[len(drow):]


_NORM_ROWS = 512


def _sigmoid(x):
    return jax.nn.sigmoid(x)


def _silu(x):
    return x * _sigmoid(x)


def _rms(x, g):
    return x * lax.rsqrt(jnp.mean(x * x, axis=-1, keepdims=True) + EPS) * g


def _f_rms(r, p):
    return [_rms(r[0], p[0])]


def _f_rms_res(r, p):
    return [r[0], _rms(r[0], p[0])]


def _f_gate_ab(r, p):
    o, ga, mixed, gb = r
    return [(o * _silu(ga), mixed * p[0] * _silu(gb))]


def _f_sgu(r, p):
    u, v, gc = r
    lg, lb, w, b = p
    n = float(D_MODEL)
    mu = sum(jnp.sum(vk, axis=-1, keepdims=True) for vk in v) / n
    var = sum(jnp.sum(jnp.square(vk - mu), axis=-1, keepdims=True) for vk in v) / n
    rs = lax.rsqrt(var + EPS)
    t = w[0].shape[0]
    tri = lax.broadcasted_iota(jnp.int32, (t, t), 0) >= lax.broadcasted_iota(jnp.int32, (t, t), 1)
    outs = []
    for k in range(len(v)):
        vn = (v[k] - mu) * rs * lg[k] + lb[k]
        mixed = jnp.dot(jnp.where(tri, w[k], 0.0), vn, preferred_element_type=F32) + b[k]
        outs.append(u[k] * mixed * _silu(gc[k]))
    return [tuple(outs)]


def _gelu(x):
    return 0.5 * x * (1.0 + jnp.tanh(math.sqrt(2.0 / math.pi) * (x + 0.044715 * (x * x * x))))


def _f_gelu_y(r, p):
    yc, uf = r
    return [_gelu(yc + p[0] * uf)]


def _f_glu_gate(r, p):
    t12, gd = r
    return [t12[0] * _sigmoid(t12[1]) * _silu(gd)]


def _f_s5_prep(r, p):
    ar, ai, ldt = r
    dt = jnp.exp(ldt)
    mag = jnp.exp(dt * ar)
    abar_re = mag * jnp.cos(dt * ai)
    abar_im = mag * jnp.sin(dt * ai)
    nr, ni = abar_re - 1.0, abar_im
    inv = 1.0 / (ar * ar + ai * ai)
    return [abar_re, abar_im, (nr * ar + ni * ai) * inv, (ni * ar - nr * ai) * inv]


def _f_bbar(r, p):
    br, bi, cr, ci = r
    return [cr * br - ci * bi, cr * bi + ci * br]


def _loss_head(x, target, g):
    tr = 256
    n_rows, width = x.shape

    def f(xv, gv, tv):
        err = jnp.square(_rms(xv, gv) - tv)
        return 0.5 * jnp.mean(err, axis=-1, keepdims=True)

    def body(x_ref, t_ref, g_ref, loss_ref, dx_ref, dg_ref):
        @pl.when(pl.program_id(0) == 0)
        def _():
            loss_ref[...] = jnp.zeros_like(loss_ref)
            dg_ref[...] = jnp.zeros_like(dg_ref)

        tv = t_ref[...]
        row_loss, vjp = jax.vjp(lambda a, b: f(a, b, tv), x_ref[...], g_ref[...])
        dx, dg = vjp(jnp.ones_like(row_loss))
        dx_ref[...] = dx
        dg_ref[...] += dg
        loss_ref[...] += jnp.broadcast_to(jnp.sum(row_loss, axis=0, keepdims=True), loss_ref.shape)

    blk = pl.BlockSpec((tr, width), lambda i: (i, 0))
    one = pl.BlockSpec((1, width), lambda i: (0, 0))
    return _pc(
        body, name="loss_head", grid=(n_rows // tr,), in_specs=[blk, blk, one],
        out_specs=[pl.BlockSpec((1, 128), lambda i: (0, 0)), blk, one],
        out_shape=[jax.ShapeDtypeStruct((1, 128), F32), jax.ShapeDtypeStruct(x.shape, F32), jax.ShapeDtypeStruct((1, width), F32)],
        sem=("arbitrary",),
    )(x, target, g)


_NT = (((1,), (1,)), ((), ()))
_TN = (((0,), (0,)), ((), ()))


def _tile(n, cap):
    t = min(n, cap)
    while n % t:
        t -= 128
    assert t > 0
    return t


def _mm(name, a, b, *, ta=False, tb=False, out_dtype=F32, a_off=0, a_width=None, res=None, out_stack=None):
    assert not (ta and tb)
    stacked = b.ndim == 3
    bk, bn = (b.shape[1], b.shape[0] * b.shape[2]) if stacked else b.shape
    if ta:
        kc = a.shape[0]
        m = a.shape[1] - a_off if a_width is None else a_width
        n = bn
        assert bk == kc and not stacked
    else:
        m = a.shape[0]
        kc = a.shape[1] - a_off if a_width is None else a_width
        n = bk if tb else bn
        assert (bn if tb else bk) == kc
    tn = _tile(b.shape[2] if stacked and not tb else (out_stack or n), 1024)
    size = lambda dt: jnp.dtype(dt).itemsize
    for tk_cap, tm_cap in ((2048, 2048), (2048, 1024), (1024, 1024), (1024, 512), (1024, 256)):
        tm, tk = _tile(m, tm_cap), _tile(b.shape[2] if stacked and tb else kc, tk_cap)
        nk = kc // tk
        vmem = 2 * tm * tk * size(a.dtype) + 2 * tk * tn * size(b.dtype) + tm * tn * (2 * size(out_dtype) + (4 if nk > 1 else 0) + (8 if res is not None else 0))
        if vmem <= _MM_VMEM_BYTES:
            break
    if ta:
        assert a_off % tm == 0
        a_spec = pl.BlockSpec((tk, tm), lambda i, j, k, o=a_off // tm: (k, i + o))
        dims = _TN
    else:
        assert a_off % tk == 0
        a_spec = pl.BlockSpec((tm, tk), lambda i, j, k, o=a_off // tk: (i, k + o))
        dims = _NT if tb else (((1,), (0,)), ((), ()))
    if stacked and tb:
        b_spec = pl.BlockSpec((None, tn, tk), lambda i, j, k, q=b.shape[2] // tk: (k // q, j, k % q))
    elif stacked:
        b_spec = pl.BlockSpec((None, tk, tn), lambda i, j, k, q=b.shape[2] // tn: (j // q, k, j % q))
    else:
        b_spec = pl.BlockSpec((tn, tk), lambda i, j, k: (j, k)) if tb else pl.BlockSpec((tk, tn), lambda i, j, k: (k, j))
    if out_stack:
        out_spec = pl.BlockSpec((None, tm, tn), lambda i, j, k, q=out_stack // tn: (j // q, i, j % q))
        out_shape = jax.ShapeDtypeStruct((n // out_stack, m, out_stack), out_dtype)
    else:
        out_spec = pl.BlockSpec((tm, tn), lambda i, j, k: (i, j))
        out_shape = jax.ShapeDtypeStruct((m, n), out_dtype)
    in_specs, args = [a_spec, b_spec], [a, b]
    has_res = res is not None
    if has_res:
        in_specs.append(pl.BlockSpec((tm, tn), lambda i, j, k: (i, j)))
        args.append(res)

    def finish(refs, acc):
        if has_res:
            acc = acc + refs[2][...].astype(F32)
        refs[3 if has_res else 2][...] = acc.astype(out_dtype)

    def body_one(*refs):
        finish(refs, lax.dot_general(refs[0][...].astype(BF16), refs[1][...].astype(BF16), dims, preferred_element_type=F32))

    def body(*refs):
        acc_ref = refs[-1]
        k = pl.program_id(2)

        @pl.when(k == 0)
        def _():
            acc_ref[...] = jnp.zeros_like(acc_ref)

        acc_ref[...] += lax.dot_general(refs[0][...].astype(BF16), refs[1][...].astype(BF16), dims, preferred_element_type=F32)

        @pl.when(k == nk - 1)
        def _():
            finish(refs, acc_ref[...])

    return _pc(
        body_one if nk == 1 else body, name=name, grid=(m // tm, n // tn, nk), in_specs=in_specs, out_specs=out_spec, out_shape=out_shape,
        scratch_shapes=[] if nk == 1 else [pltpu.VMEM((tm, tn), F32)], sem=("parallel", "parallel", "arbitrary"),
    )(*args)


def _mm_blocks(name, a, b, *, grid, out_blk, a_blk, a_idx, b_blk, b_idx, dims, out_dtype=F32):
    gi, gj, nk = grid

    def body(a_ref, b_ref, o_ref, acc_ref):
        k = pl.program_id(2)

        @pl.when(k == 0)
        def _():
            acc_ref[...] = jnp.zeros_like(acc_ref)

        acc_ref[...] += lax.dot_general(a_ref[...].astype(BF16), b_ref[...].astype(BF16), dims, preferred_element_type=F32)

        @pl.when(k == nk - 1)
        def _():
            o_ref[...] = acc_ref[...].astype(o_ref.dtype)

    return _pc(
        body, name=name, grid=grid, in_specs=[pl.BlockSpec(a_blk, a_idx), pl.BlockSpec(b_blk, b_idx)],
        out_specs=pl.BlockSpec(out_blk, lambda i, j, k: (i, j)), out_shape=jax.ShapeDtypeStruct((gi * out_blk[0], gj * out_blk[1]), out_dtype),
        scratch_shapes=[pltpu.VMEM(out_blk, F32)], sem=("parallel", "parallel", "arbitrary"),
    )(a, b)


def _head_masks(width, nsub):
    lane = lax.broadcasted_iota(jnp.int32, (1, width), 1)
    hd = width // nsub
    return [(lane >= h * hd) & (lane < (h + 1) * hd) for h in range(nsub)]


def _dilated_log_count(row0, tq, ext):
    delta = (row0 + lax.broadcasted_iota(jnp.int32, (tq, ext), 0)) - lax.broadcasted_iota(jnp.int32, (tq, ext), 1)
    cnt = (delta <= 128).astype(jnp.int32) + (((delta & 3) == 0) & (delta <= 512)).astype(jnp.int32) + ((delta & 15) == 0).astype(jnp.int32)
    logc = jnp.where(cnt == 3, math.log(3.0), jnp.where(cnt == 2, math.log(2.0), 0.0))
    return jnp.where((delta >= 0) & (cnt > 0), logc, NEG)


def _bias_table(tab, nq, tq):
    @pl.when(pl.program_id(0) == 0)
    def _():
        for d in range(nq):
            tab[d] = _dilated_log_count(d * tq, tq, tq)


def _scores(q, ke, tab, r, masks, h):
    qm = (jnp.where(masks[h], q, 0.0) if len(masks) > 1 else q).astype(BF16)
    s = lax.dot_general(qm, ke, _NT, preferred_element_type=F32)
    if tab is not None:
        s = s + jnp.concatenate([tab[r - c] for c in range(r + 1)], axis=1)
    return qm, s


def _head_value(masks, h, tile, reduce, fill):
    return reduce(jnp.where(masks[h], tile, fill) if len(masks) > 1 else tile, axis=-1, keepdims=True)


def _attn_fwd(name, qa, ka, va, *, qc, kc, vc, width, nblk, nsub, causal, tq, scale, out_dtype):
    sq, t_len = qa.shape[0], ka.shape[0]
    nq = sq // tq

    def body(q_ref, k_ref, v_ref, o_ref, lse_ref, *scratch):
        tab = scratch[0] if causal else None
        if causal:
            _bias_table(tab, nq, tq)
        kb = k_ref[...].astype(BF16)
        vb = v_ref[...].astype(BF16)
        masks = _head_masks(width, nsub)
        for r in range(nq):
            ext = (r + 1) * tq if causal else t_len
            q = q_ref[r * tq:(r + 1) * tq, :].astype(F32) * scale
            ke, ve = kb[:ext], vb[:ext]
            o = lse = None
            for h in range(nsub):
                _, s = _scores(q, ke, tab, r, masks, h)
                m = jnp.max(s, axis=-1, keepdims=True)
                p = jnp.exp(s - m)
                l = jnp.sum(p, axis=-1, keepdims=True)
                oh = jnp.dot(p.astype(BF16), ve, preferred_element_type=F32) * (1.0 / l)
                lh = jnp.broadcast_to(m + jnp.log(l), (tq, width))
                o = oh if o is None else jnp.where(masks[h], oh, o)
                lse = lh if lse is None else jnp.where(masks[h], lh, lse)
            o_ref[r * tq:(r + 1) * tq, :] = o.astype(o_ref.dtype)
            lse_ref[r * tq:(r + 1) * tq, :] = lse

    blk = pl.BlockSpec((sq, width), lambda i: (0, i))
    return _pc(
        body, name=name, grid=(nblk,),
        in_specs=[pl.BlockSpec((sq, width), lambda i, c=qc: (0, c + i)), pl.BlockSpec((t_len, width), lambda i, c=kc: (0, c + i)),
                  pl.BlockSpec((t_len, width), lambda i, c=vc: (0, c + i))],
        out_specs=[blk, blk],
        out_shape=[jax.ShapeDtypeStruct((sq, nblk * width), out_dtype), jax.ShapeDtypeStruct((sq, nblk * width), F32)],
        scratch_shapes=[pltpu.VMEM((nq, tq, tq), F32)] if causal else [], sem=("arbitrary",),
    )(qa, ka, va)


def _attn_bwd(name, qa, ka, va, doa, oa, lsea, *, qc, kc, vc, width, nblk, nsub, causal, tq, scale, out_dtype):
    sq, t_len = qa.shape[0], ka.shape[0]
    nq = sq // tq

    def body(q_ref, k_ref, v_ref, do_ref, o_ref, lse_ref, dq_ref, dk_ref, dv_ref, dk_acc, dv_acc, *scratch):
        tab = scratch[0] if causal else None
        if causal:
            _bias_table(tab, nq, tq)
        kb = k_ref[...].astype(BF16)
        vb = v_ref[...].astype(BF16)
        masks = _head_masks(width, nsub)
        dk_acc[...] = jnp.zeros_like(dk_acc)
        dv_acc[...] = jnp.zeros_like(dv_acc)
        for r in range(nq):
            ext = (r + 1) * tq if causal else t_len
            q = q_ref[r * tq:(r + 1) * tq, :].astype(F32) * scale
            do = do_ref[r * tq:(r + 1) * tq, :].astype(F32)
            do_o = do * o_ref[r * tq:(r + 1) * tq, :].astype(F32)
            lse = lse_ref[r * tq:(r + 1) * tq, :]
            ke, ve = kb[:ext], vb[:ext]
            dq = None
            for h in range(nsub):
                qm, s = _scores(q, ke, tab, r, masks, h)
                dom = (jnp.where(masks[h], do, 0.0) if nsub > 1 else do).astype(BF16)
                pn = jnp.exp(s - _head_value(masks, h, lse, jnp.max, -jnp.inf))
                dpn = lax.dot_general(dom, ve, _NT, preferred_element_type=F32)
                dsb = (pn * (dpn - _head_value(masks, h, do_o, jnp.sum, 0.0))).astype(BF16)
                dqh = jnp.dot(dsb, ke, preferred_element_type=F32)
                dq = dqh if dq is None else jnp.where(masks[h], dqh, dq)
                dk_acc[0:ext, :] += lax.dot_general(dsb, qm, _TN, preferred_element_type=F32)
                dv_acc[0:ext, :] += lax.dot_general(pn.astype(BF16), dom, _TN, preferred_element_type=F32)
            dq_ref[r * tq:(r + 1) * tq, :] = (dq * scale).astype(dq_ref.dtype)
        dk_ref[...] = dk_acc[...].astype(dk_ref.dtype)
        dv_ref[...] = dv_acc[...].astype(dv_ref.dtype)

    return _pc(
        body, name=name, grid=(nblk,),
        in_specs=[pl.BlockSpec((sq, width), lambda i, c=qc: (0, c + i)), pl.BlockSpec((t_len, width), lambda i, c=kc: (0, c + i)),
                  pl.BlockSpec((t_len, width), lambda i, c=vc: (0, c + i))] + [pl.BlockSpec((sq, width), lambda i: (0, i))] * 3,
        out_specs=[pl.BlockSpec((sq, width), lambda i: (0, i)), pl.BlockSpec((t_len, width), lambda i: (0, i)), pl.BlockSpec((t_len, width), lambda i: (0, i))],
        out_shape=[jax.ShapeDtypeStruct((sq, nblk * width), out_dtype), jax.ShapeDtypeStruct((t_len, nblk * width), out_dtype),
                   jax.ShapeDtypeStruct((t_len, nblk * width), out_dtype)],
        scratch_shapes=[pltpu.VMEM((t_len, width), F32), pltpu.VMEM((t_len, width), F32)] + ([pltpu.VMEM((nq, tq, tq), F32)] if causal else []),
        sem=("arbitrary",),
    )(qa, ka, va, doa, oa, lsea)


_SELF = dict(qc=0, kc=8, vc=16, width=128, nblk=8, nsub=2, causal=True, tq=256, scale=A_HEAD_DIM ** -0.5)
_CROSS = dict(qc=0, kc=0, vc=4, width=256, nblk=4, nsub=1, causal=False, tq=512, scale=X_HEAD_DIM ** -0.5)


def _window_sum(x, g, row, backward):
    n = x.shape[0]

    def shift(y, k):
        if backward:
            return jnp.where(row < n - k, pltpu.roll(y, n - k, 0), 0.0)
        return jnp.where(row >= k, pltpu.roll(y, k, 0), 0.0)

    s2 = x + shift(x, 1)
    s4 = s2 + shift(s2, 2)
    s8 = s4 + shift(s4, 4)
    s16 = s8 + shift(s8, 8)
    return jnp.where(g == 0, s2, jnp.where(g == 1, s4, jnp.where(g == 2, s8, s16)))


def _pool(name, arr, c0, backward, out_dtype):
    n = arr.shape[0]
    gw = 256

    def body(v_ref, o_ref):
        g = pl.program_id(0)
        v = v_ref[...].astype(F32)
        row = lax.broadcasted_iota(jnp.int32, v.shape, 0)
        w = jnp.where(g == 0, 2, jnp.where(g == 1, 4, jnp.where(g == 2, 8, 16)))
        cnt = jnp.minimum(row + 1, w).astype(F32)
        if backward:
            o_ref[...] = (_window_sum(v / cnt, g, row, True) - v).astype(o_ref.dtype)
        else:
            o_ref[...] = (_window_sum(v, g, row, False) / cnt - v).astype(o_ref.dtype)

    return _pc(
        body, name=name, grid=(4,), in_specs=[pl.BlockSpec((n, gw), lambda i, c=c0 // gw: (0, c + i))],
        out_specs=pl.BlockSpec((n, gw), lambda i: (0, i)), out_shape=jax.ShapeDtypeStruct((n, 4 * gw), out_dtype), sem=("parallel",),
    )(arr)


_SCAN_ROWS = 256


def _scan_fwd(bu3, a2):
    n = bu3.shape[0]

    def body(bu_ref, a_ref, h_ref, carry):
        @pl.when(pl.program_id(0) == 0)
        def _():
            carry[...] = jnp.zeros_like(carry)

        ar, ai = a_ref[0:16, :], a_ref[16:32, :]

        def step(t, c):
            hr, hi = c
            nr = ar * hr - ai * hi + bu_ref[t, 0:16, :]
            ni = ar * hi + ai * hr + bu_ref[t, 16:32, :]
            h_ref[t, 0:16, :] = nr
            h_ref[t, 16:32, :] = ni
            return nr, ni

        hr, hi = lax.fori_loop(0, _SCAN_ROWS, step, (carry[0:16, :], carry[16:32, :]), unroll=8)
        carry[0:16, :] = hr
        carry[16:32, :] = hi

    blk = pl.BlockSpec((_SCAN_ROWS, 32, 128), lambda i: (i, 0, 0))
    return _pc(
        body, name="s5_scan_fwd", grid=(n // _SCAN_ROWS,), in_specs=[blk, pl.BlockSpec((32, 128), lambda i: (0, 0))], out_specs=blk,
        out_shape=jax.ShapeDtypeStruct(bu3.shape, F32), scratch_shapes=[pltpu.VMEM((32, 128), F32)], sem=("arbitrary",),
    )(bu3, a2)


def _scan_bwd(dh3, h3, a2):
    n = dh3.shape[0]
    nb = n // _SCAN_ROWS

    def body(dh_ref, h_ref, a_ref, dbu_ref, da_ref, carry):
        @pl.when(pl.program_id(0) == 0)
        def _():
            carry[...] = jnp.zeros_like(carry)
            da_ref[...] = jnp.zeros_like(da_ref)

        ar, ai = a_ref[0:16, :], a_ref[16:32, :]

        def step(tt, c):
            gr, gi, dar, dai = c
            t = _SCAN_ROWS - 1 - tt
            hr, hi = h_ref[t, 0:16, :], h_ref[t, 16:32, :]
            dar = dar + gr * hr + gi * hi
            dai = dai - gr * hi + gi * hr
            ngr = dh_ref[t, 0:16, :] + ar * gr + ai * gi
            ngi = dh_ref[t, 16:32, :] - ai * gr + ar * gi
            dbu_ref[t, 0:16, :] = ngr
            dbu_ref[t, 16:32, :] = ngi
            return ngr, ngi, dar, dai

        z = jnp.zeros((16, 128), F32)
        gr, gi, dar, dai = lax.fori_loop(0, _SCAN_ROWS, step, (carry[0:16, :], carry[16:32, :], z, z), unroll=8)
        carry[0:16, :] = gr
        carry[16:32, :] = gi
        da_ref[0:16, :] += dar
        da_ref[16:32, :] += dai

    blk = pl.BlockSpec((_SCAN_ROWS, 32, 128), lambda i: (nb - 1 - i, 0, 0))
    small = pl.BlockSpec((32, 128), lambda i: (0, 0))
    return _pc(
        body, name="s5_scan_bwd", grid=(nb,), in_specs=[blk, blk, small], out_specs=[blk, small],
        out_shape=[jax.ShapeDtypeStruct(dh3.shape, F32), jax.ShapeDtypeStruct((32, 128), F32)],
        scratch_shapes=[pltpu.VMEM((32, 128), F32)], sem=("arbitrary",),
    )(dh3, h3, a2)


def _bdense(bb_re, bb_im):
    eye = jnp.eye(8, dtype=F32)

    def one(bb):
        return jnp.einsum("sgph,gk->sghkp", bb.reshape(4, 8, S5_STATE, S5_GROUP_DIM), eye).reshape(512, 512)

    return jnp.concatenate([one(bb_re), one(bb_im)], axis=1)


def _cdense(c_re, c_im):
    eye = jnp.eye(8, dtype=F32)

    def one(cc):
        return jnp.einsum("sghp,gk->sgpkh", cc.reshape(4, 8, S5_GROUP_DIM, S5_STATE), eye).reshape(2048, 128)

    return jnp.concatenate([one(c_re), -one(c_im)], axis=0)


_NN = (((1,), (0,)), ((), ()))
_UF_BLOCK = 3072 // 128


def _s5_bu(z, bd):
    return _mm_blocks("mm_s5_bu", z, bd, grid=(1, 8, 1), out_blk=(SEQ, 512), a_blk=(SEQ, 128), a_idx=lambda i, j, k: (0, _UF_BLOCK + j % 4),
                      b_blk=(128, 512), b_idx=lambda i, j, k: (j % 4, j // 4), dims=_NN)


def _s5_bu_dx(dbu, bd):
    return _mm_blocks("mm_s5_bu_dx", dbu, bd, grid=(1, 4, 2), out_blk=(SEQ, 128), a_blk=(SEQ, 512), a_idx=lambda i, j, k: (0, 4 * k + j),
                      b_blk=(128, 512), b_idx=lambda i, j, k: (j, k), dims=_NT)


def _s5_bu_dw(z, dbu):
    return _mm_blocks("mm_s5_bu_dw", z, dbu, grid=(4, 2, 2), out_blk=(128, 512), a_blk=(1024, 128), a_idx=lambda i, j, k: (k, _UF_BLOCK + i),
                      b_blk=(1024, 512), b_idx=lambda i, j, k: (k, 4 * j + i), dims=_TN)


def _s5_y(h2, cf):
    return _mm_blocks("mm_s5_y", h2, cf, grid=(1, 4, 2), out_blk=(SEQ, 128), a_blk=(SEQ, 512), a_idx=lambda i, j, k: (0, 4 * k + j),
                      b_blk=(512, 128), b_idx=lambda i, j, k: (4 * k + j, 0), dims=_NN)


def _s5_y_dx(dyc, cf):
    return _mm_blocks("mm_s5_y_dx", dyc, cf, grid=(1, 8, 1), out_blk=(SEQ, 512), a_blk=(SEQ, 128), a_idx=lambda i, j, k: (0, j % 4),
                      b_blk=(512, 128), b_idx=lambda i, j, k: (j, 0), dims=_NT)


def _s5_y_dw(h2, dyc):
    return _mm_blocks("mm_s5_y_dw", h2, dyc, grid=(8, 1, 2), out_blk=(512, 128), a_blk=(1024, 512), a_idx=lambda i, j, k: (k, i),
                      b_blk=(1024, 128), b_idx=lambda i, j, k: (k, i % 4), dims=_TN)


def _pool_dense(pw):
    eye = jnp.eye(4, dtype=pw.dtype)
    return jnp.einsum("gcd,gk->gckd", pw, eye).reshape(1024, 1024)


def _row2(v):
    return v.reshape(1, -1)


def _even_fwd(x, W, i, zero, rest_of_weights, start_ahead):
    hn = _rw_fwd("rms_fwd", _f_rms, [_cols(x)], [_par(_row2(W["norm_ab"][i]) + zero)], [(D_MODEL, BF16, 1)], _NORM_ROWS)[0]
    z = _mm("mm_in_ab", hn, W["w_in_ab"][i], out_dtype=BF16)
    o, lse = _attn_fwd("attn_self_fwd", z, z, z, out_dtype=F32, **_SELF)
    zero = rest_of_weights(o)
    pooled = _pool("pool_fwd", z, 4096, False, BF16)
    wp = _pool_dense(W["pool_w"][i])
    mixed = _mm("mm_pool", pooled, wp)
    scale = _row2(W["pool_scale"][i]) + zero
    ab = _rw_fwd("gate_ab_fwd", _f_gate_ab, [_cols(o), _cols(z, 3072, 1024), _cols(mixed), _cols(z, 5120, 1024)], [_par(scale)],
                 [(2048, BF16, 2)], 256)[0]
    x1 = _mm("mm_out_ab", ab, W["w_out_ab"][i], res=x)
    return x1, dict(x=x, hn=hn, z=z, o=o, lse=lse, pooled=pooled, wp=wp, mixed=mixed, ab=ab), 0.0


def _even_bwd(dx1, sv, W, G, i, send):
    x, hn, z = sv["x"], sv["hn"], sv["z"]
    dab = _mm("mm_out_ab_dx", dx1, W["w_out_ab"][i], tb=True)
    G["w_out_ab"][i] = _mm("mm_out_ab_dw", sv["ab"], dx1, ta=True, out_dtype=BF16)
    scale = _row2(W["pool_scale"][i]) + send(0)
    (do, dga, dmixed, dgb), (dscale,) = _rw_bwd(
        "gate_ab_bwd", _f_gate_ab, [_cols(sv["o"]), _cols(z, 3072, 1024), _cols(sv["mixed"]), _cols(z, 5120, 1024)], [_par(scale)],
        [_cols(dab, nsplit=2)], [(0, F32), (1, BF16), (2, BF16), (3, BF16)], [0], 256)
    G["pool_scale"][i] = dscale.reshape(-1)
    dpooled = _mm("mm_pool_dx", dmixed, sv["wp"], tb=True)
    dwp = _mm("mm_pool_dw", sv["pooled"], dmixed, ta=True, out_dtype=BF16)
    G["pool_w"][i] = jnp.stack([dwp[g * 256:(g + 1) * 256, g * 256:(g + 1) * 256] for g in range(4)])
    dvb = _pool("pool_bwd", dpooled, 0, True, BF16)
    dq, dk, dv = _attn_bwd("attn_self_bwd", z, z, z, do, sv["o"], sv["lse"], out_dtype=BF16, **_SELF)
    dz = jnp.concatenate([dq, dk, dv, dga, dvb, dgb], axis=1)
    dhn = _mm("mm_in_ab_dx", dz, W["w_in_ab"][i], tb=True)
    G["w_in_ab"][i] = _mm("mm_in_ab_dw", hn, dz, ta=True, out_dtype=BF16, out_stack=W["w_in_ab"][i].shape[2])
    g = _row2(W["norm_ab"][i]) + send(1)
    (dx,), (dg,) = _rw_bwd("rms_bwd", _f_rms_res, [_cols(x)], [_par(g)], [_cols(dx1), _cols(dhn)], [(0, F32)], [0], _NORM_ROWS)
    G["norm_ab"][i] = dg.reshape(-1)
    return dx


def _odd_fwd(x, W, i, zero, rest_of_weights, start_ahead):
    hn = _rw_fwd("rms_fwd", _f_rms, [_cols(x)], [_par(_row2(W["norm_cd"][i]) + zero)], [(D_MODEL, BF16, 1)], _NORM_ROWS)[0]
    z = _mm("mm_in_cd", hn, W["w_in_cd"][i], out_dtype=BF16)
    sgu_p = [_par(_row2(W["sgu_ln_g"][i]), 4), _par(_row2(W["sgu_ln_b"][i]), 4), _par(W["sgu_w"][i], 4), _par(W["sgu_b"][i][..., None], 4)]
    c_out = _rw_fwd("sgu_fwd", _f_sgu, [_cols(z, 0, 1024, 4), _cols(z, 1024, 1024, 4), _cols(z, 2048, 1024, 4)], sgu_p, [(1024, BF16, 4)], 128)[0]
    prep_rows = [_cols(W["s5_a_re"][i]), _cols(W["s5_a_im"][i]), _cols(W["s5_log_dt"][i].reshape(S5_GROUPS, 1))]
    abar_re, abar_im, coef_re, coef_im = _rw_fwd("s5_prep_fwd", _f_s5_prep, prep_rows, [], [(S5_STATE, F32, 1)] * 4, S5_GROUPS)
    bb_rows = [_cols(W["s5_b_re"][i].reshape(2048, 16)), _cols(W["s5_b_im"][i].reshape(2048, 16)), _cols(coef_re.reshape(2048, 1)), _cols(coef_im.reshape(2048, 1))]
    bb_re, bb_im = _rw_fwd("s5_bbar_fwd", _f_bbar, bb_rows, [], [(16, F32, 1)] * 2, 256)
    bd = _bdense(bb_re, bb_im).astype(BF16)
    cf = _cdense(W["s5_c_re"][i], W["s5_c_im"][i]).astype(BF16)
    a2 = jnp.concatenate([abar_re.reshape(16, 128), abar_im.reshape(16, 128)], axis=0)
    bu = _s5_bu(z, bd)
    h3 = _scan_fwd(bu.reshape(SEQ, 32, 128), a2)
    h2 = h3.reshape(SEQ, 4096)
    yc = _s5_y(h2, cf)
    dpar = _row2(W["s5_d"][i]) + start_ahead(h3)
    yg = _rw_fwd("gelu_fwd", _f_gelu_y, [_cols(yc), _cols(z, 3072, 512)], [_par(dpar)], [(512, BF16, 1)], 256)[0]
    zero = rest_of_weights(yg)
    w12 = jnp.concatenate([W["glu_w1"][i], W["glu_w2"][i]], axis=1)
    t12 = _mm("mm_glu", yg, w12)
    d_out = _rw_fwd("glu_gate_fwd", _f_glu_gate, [_cols(t12, nsplit=2), _cols(z, 3584, 512)], [], [(512, BF16, 1)], 256)[0]
    cd = jnp.concatenate([c_out, d_out], axis=1)
    x1 = _mm("mm_out_cd", cd, W["w_out_cd"][i], res=x)
    sv = dict(x=x, hn=hn, z=z, sgu_p=sgu_p, prep_rows=prep_rows, bb_rows=bb_rows, bb=(bb_re, bb_im), bd=bd, cf=cf, a2=a2,
              h3=h3, h2=h2, yc=yc, dpar=dpar, yg=yg, w12=w12, t12=t12, cd=cd)
    return x1, sv, zero


def _odd_bwd(dx1, sv, W, G, i, send):
    x, hn, z = sv["x"], sv["hn"], sv["z"]
    dcd = _mm("mm_out_cd_dx", dx1, W["w_out_cd"][i], tb=True)
    G["w_out_cd"][i] = _mm("mm_out_cd_dw", sv["cd"], dx1, ta=True, out_dtype=BF16)
    (du, dv, dgc), (dlg, dlb, dsw, dsb) = _rw_bwd(
        "sgu_bwd", _f_sgu, [_cols(z, 0, 1024, 4), _cols(z, 1024, 1024, 4), _cols(z, 2048, 1024, 4)], sv["sgu_p"],
        [_cols(dcd, 0, 1024, 4)], [(0, BF16), (1, BF16), (2, BF16)], [0, 1, 2, 3], 128)
    G["sgu_ln_g"][i], G["sgu_ln_b"][i] = dlg.reshape(-1), dlb.reshape(-1)
    G["sgu_w"][i], G["sgu_b"][i] = dsw, dsb[..., 0]
    (dt12, dgd), _ = _rw_bwd("glu_gate_bwd", _f_glu_gate, [_cols(sv["t12"], nsplit=2), _cols(z, 3584, 512)], [], [_cols(dcd, 1024, 512)],
                             [(0, BF16), (1, BF16)], [], 256)
    dyg = _mm("mm_glu_dx", dt12, sv["w12"], tb=True)
    dw12 = _mm("mm_glu_dw", sv["yg"], dt12, ta=True, out_dtype=BF16)
    G["glu_w1"][i], G["glu_w2"][i] = dw12[:, :512], dw12[:, 512:]
    (dyc, duf1), (dd,) = _rw_bwd("gelu_bwd", _f_gelu_y, [_cols(sv["yc"]), _cols(z, 3072, 512)], [_par(sv["dpar"])], [_cols(dyg)],
                                 [(0, BF16), (1, F32)], [0], 256)
    G["s5_d"][i] = dd.reshape(-1)
    dh2 = _s5_y_dx(dyc, sv["cf"])
    dcf = _s5_y_dw(sv["h2"], dyc)
    _, cvjp = jax.vjp(_cdense, W["s5_c_re"][i], W["s5_c_im"][i])
    G["s5_c_re"][i], G["s5_c_im"][i] = cvjp(dcf)
    dbu3, da2 = _scan_bwd(dh2.reshape(SEQ, 32, 128), sv["h3"], sv["a2"])
    dbu = dbu3.reshape(SEQ, 4096)
    duf2 = _s5_bu_dx(dbu, sv["bd"])
    dbd = _s5_bu_dw(z, dbu)
    _, bvjp = jax.vjp(_bdense, *sv["bb"])
    dbb_re, dbb_im = bvjp(dbd)
    (dbr, dbi, dcr, dci), _ = _rw_bwd("s5_bbar_bwd", _f_bbar, sv["bb_rows"], [], [_cols(dbb_re), _cols(dbb_im)],
                                      [(0, F32), (1, F32), (2, F32), (3, F32)], [], 256)
    G["s5_b_re"][i], G["s5_b_im"][i] = dbr.reshape(S5_GROUPS, S5_STATE, S5_GROUP_DIM), dbi.reshape(S5_GROUPS, S5_STATE, S5_GROUP_DIM)
    douts = [_cols(da2[0:16].reshape(S5_GROUPS, S5_STATE)), _cols(da2[16:32].reshape(S5_GROUPS, S5_STATE)),
             _cols(dcr.reshape(S5_GROUPS, S5_STATE)), _cols(dci.reshape(S5_GROUPS, S5_STATE))]
    (dar, dai, dldt), _ = _rw_bwd("s5_prep_bwd", _f_s5_prep, sv["prep_rows"], [], douts, [(0, F32), (1, F32), (2, F32)], [], S5_GROUPS)
    G["s5_a_re"][i], G["s5_a_im"][i], G["s5_log_dt"][i] = dar, dai, dldt.reshape(-1)
    dxd = (duf1 + duf2).astype(BF16)
    dz = jnp.concatenate([du, dv, dgc, dxd, dgd], axis=1)
    dhn = _mm("mm_in_cd_dx", dz, W["w_in_cd"][i], tb=True)
    G["w_in_cd"][i] = _mm("mm_in_cd_dw", hn, dz, ta=True, out_dtype=BF16, out_stack=W["w_in_cd"][i].shape[2])
    g = _row2(W["norm_cd"][i]) + send(0)
    (dx,), (dg,) = _rw_bwd("rms_bwd", _f_rms_res, [_cols(x)], [_par(g)], [_cols(dx1), _cols(dhn)], [(0, F32)], [0], _NORM_ROWS)
    G["norm_cd"][i] = dg.reshape(-1)
    return dx


def _cross_fwd(x1, mem_n, W, l, zero):
    hx = _rw_fwd("rms_fwd", _f_rms, [_cols(x1)], [_par(_row2(W["norm_x"][l]) + zero)], [(D_MODEL, BF16, 1)], _NORM_ROWS)[0]
    qx = _mm("mm_xq", hx, W["w_xq"][l], out_dtype=BF16)
    kv = _mm("mm_xkv", mem_n, W["w_xkv"][l], out_dtype=BF16)
    ox, lse = _attn_fwd("attn_cross_fwd", qx, kv, kv, out_dtype=BF16, **_CROSS)
    x2 = _mm("mm_xo", ox, W["w_xo"][l], res=x1)
    return x2, dict(x1=x1, hx=hx, qx=qx, kv=kv, ox=ox, lse=lse)


def _cross_bwd(dx2, dmem_n, sv, mem_n, W, G, l, zero):
    dox = _mm("mm_xo_dx", dx2, W["w_xo"][l], tb=True, out_dtype=BF16)
    G["w_xo"][l] = _mm("mm_xo_dw", sv["ox"], dx2, ta=True, out_dtype=BF16)
    dqx, dk, dv = _attn_bwd("attn_cross_bwd", sv["qx"], sv["kv"], sv["kv"], dox, sv["ox"], sv["lse"], out_dtype=BF16, **_CROSS)
    dkv = jnp.concatenate([dk, dv], axis=1)
    dhx = _mm("mm_xq_dx", dqx, W["w_xq"][l], tb=True)
    G["w_xq"][l] = _mm("mm_xq_dw", sv["hx"], dqx, ta=True, out_dtype=BF16)
    dmem_n = _mm("mm_xkv_dx", dkv, W["w_xkv"][l], tb=True, res=dmem_n)
    G["w_xkv"][l] = _mm("mm_xkv_dw", mem_n, dkv, ta=True, out_dtype=BF16, out_stack=W["w_xkv"][l].shape[2])
    (dx1,), (dg,) = _rw_bwd("rms_bwd", _f_rms_res, [_cols(sv["x1"])], [_par(_row2(W["norm_x"][l]) + zero)], [_cols(dx2), _cols(dhx)], [(0, F32)], [0], _NORM_ROWS)
    G["norm_x"][l] = dg.reshape(-1)
    return dx1, dmem_n


_PER_LAYER = ("pool_scale", "norm_ab", "norm_cd", "sgu_ln_g", "sgu_ln_b", "sgu_w", "sgu_b", "s5_d", "s5_c_re", "s5_c_im", "s5_b_re", "s5_b_im",
              "s5_a_re", "s5_a_im", "s5_log_dt", "w_in_ab", "pool_w", "w_out_ab", "w_in_cd", "glu_w1", "glu_w2", "w_out_cd")


def _local_step(x, mem, target, W, weights_of, start_ahead, send_grads, after_layer):
    G = {k: [None, None] for k in _PER_LAYER}
    for k in ("norm_x", "w_xq", "w_xkv", "w_xo"):
        G[k] = [None] * DEPTH
    mem_rows = [_cols(mem)]
    mem_par = [_par(_row2(W["mem_norm"]))]
    mem_n = _rw_fwd("rms_fwd_mem", _f_rms, mem_rows, mem_par, [(D_MODEL, BF16, 1)], MEM_LEN)[0]
    saved = []
    for layer in range(DEPTH):
        zero = weights_of(layer, 0, x if layer else mem_n)
        mixer = _even_fwd if layer % 2 == 0 else _odd_fwd
        x, sv, zero = mixer(x, W, layer // 2, zero, functools.partial(weights_of, layer, 1), functools.partial(start_ahead, layer))
        x, svx = _cross_fwd(x, mem_n, W, layer, zero)
        saved.append((sv, svx))
    loss, dx, dfinal = _loss_head(x, target, _row2(W["final_norm"]))
    G["final_norm"] = dfinal.reshape(-1)
    dmem_n, zero = None, 0.0
    for layer in reversed(range(DEPTH)):
        sv, svx = saved[layer]
        dx, dmem_n = _cross_bwd(dx, dmem_n, svx, mem_n, W, G, layer, zero)
        hook = functools.partial(send_grads, layer, G)
        dx = _even_bwd(dx, sv, W, G, layer // 2, hook) if layer % 2 == 0 else _odd_bwd(dx, sv, W, G, layer // 2, hook)
        zero = after_layer(layer, G)
    _, (dmn,) = _rw_bwd("rms_bwd_mem", _f_rms, mem_rows, mem_par, [_cols(dmem_n)], [], [0], 256)
    G["mem_norm"] = dmn.reshape(-1)
    return loss, dx, G


_HBM = pl.BlockSpec(memory_space=pltpu.HBM)
_ANY = pl.BlockSpec(memory_space=pl.ANY)
_SEM = pl.BlockSpec(memory_space=pltpu.SEMAPHORE)
_N_PEERS = N_DEV - 1


def _mesh_pos():
    return lax.axis_index("x"), lax.axis_index("y"), lax.axis_index("c")


def _peer(pos, k):
    x, y, c = pos
    return (x ^ ((k >> 2) & 1), y ^ ((k >> 1) & 1), c ^ (k & 1))


def _lin(pos):
    return 4 * pos[0] + 2 * pos[1] + pos[2]


def _ends(gather, srcs, lands, t, sender, receiver):
    if gather:
        return lands[t].at[sender], lands[t].at[sender]
    whole = len(srcs[t].shape) != len(lands[t].shape)
    return (srcs[t] if whole else srcs[t].at[receiver]), lands[t].at[sender]


def _into_slot(name, b2, r0, r, me, dtype, after):
    c = b2.shape[1]
    tr = _row_block(r, c, 2 << 20)
    assert r0 % tr == 0

    def body(me_ref, x_ref, *rest):
        rest[-1][...] = x_ref[...].astype(dtype)

    extra = [] if after is None else [after]
    grid_spec = pltpu.PrefetchScalarGridSpec(
        num_scalar_prefetch=1, grid=(r // tr,),
        in_specs=[pl.BlockSpec((tr, c), lambda i, me, o=r0 // tr: (o + i, 0))] + [_ANY] * len(extra),
        out_specs=pl.BlockSpec((None, tr, c), lambda i, me: (me[0], i, 0)))
    return pl.pallas_call(
        body, name=name, grid_spec=grid_spec, out_shape=jax.ShapeDtypeStruct((N_DEV, r, c), dtype),
        compiler_params=pltpu.CompilerParams(dimension_semantics=("arbitrary",), vmem_limit_bytes=V7X_VMEM_LIMIT_BYTES),
        interpret=False,
    )(me, b2, *extra)


def _exchange_start(name, gather, srcs, lands, after=None):
    ns, nt = len(srcs), len(lands)
    arrs = list(srcs) + list(lands)
    extra = [] if after is None else [after]

    def body(*refs):
        ins, lnd = refs[:ns], refs[ns:ns + nt]
        refs = refs[len(extra):]
        send_sems, recv_sems = refs[ns + nt], refs[ns + nt + 1]
        token = refs[-1]
        pos = _mesh_pos()
        me = _lin(pos)
        for k in range(1, N_DEV):
            peer = _peer(pos, k)
            for t in range(nt):
                src, dst = _ends(gather, ins, lnd, t, me, _lin(peer))
                pltpu.make_async_remote_copy(
                    src_ref=src, dst_ref=dst, send_sem=send_sems.at[t * _N_PEERS + k - 1], recv_sem=recv_sems.at[t * _N_PEERS + k - 1],
                    device_id=peer, device_id_type=pl.DeviceIdType.MESH).start()
        token[...] = jnp.zeros_like(token)

    out = pl.pallas_call(
        body, name=name,
        out_shape=(pltpu.SemaphoreType.DMA((nt * _N_PEERS,)), pltpu.SemaphoreType.DMA((nt * _N_PEERS,)), *[pltpu.HBM(a.shape, a.dtype) for a in arrs],
                   jax.ShapeDtypeStruct((8, 128), F32)),
        in_specs=[_HBM] * (ns + nt) + [_ANY] * len(extra), out_specs=(_SEM, _SEM, *[_HBM] * (ns + nt), pl.BlockSpec(memory_space=pltpu.VMEM)),
        input_output_aliases={j: 2 + j for j in range(ns + nt)},
        compiler_params=pltpu.CompilerParams(has_side_effects=pltpu.SideEffectType.DATAFLOW_SIDE_EFFECTING),
        interpret=False,
    )(*[pltpu.with_memory_space_constraint(a, pltpu.HBM) for a in arrs], *extra)
    return dict(send=out[0], recv=out[1], srcs=list(out[2:2 + ns]), lands=list(out[2 + ns:2 + ns + nt]), token=out[-1][0, 0], token_arr=out[-1], gather=gather)


def _exchange_wait(name, ex, after):
    ns, nt = len(ex["srcs"]), len(ex["lands"])
    gather = ex["gather"]
    arrs = ex["srcs"] + ex["lands"]
    after = list(after) if isinstance(after, (list, tuple)) else [after]

    def body(*refs):
        ins, lnd = refs[:ns], refs[ns:ns + nt]
        send_sems, recv_sems = refs[ns + nt], refs[ns + nt + 1]
        pos = _mesh_pos()
        me = _lin(pos)
        for k in range(1, N_DEV):
            peer = _peer(pos, k)
            for t in range(nt):
                src, _ = _ends(gather, ins, lnd, t, me, _lin(peer))
                _, dst = _ends(gather, ins, lnd, t, _lin(peer), me)
                cp = pltpu.make_async_remote_copy(
                    src_ref=src, dst_ref=dst, send_sem=send_sems.at[t * _N_PEERS + k - 1], recv_sem=recv_sems.at[t * _N_PEERS + k - 1],
                    device_id=peer, device_id_type=pl.DeviceIdType.MESH)
                cp.wait_send()
                cp.wait_recv()

    out = pl.pallas_call(
        body, name=name, out_shape=tuple(pltpu.HBM(a.shape, a.dtype) for a in arrs),
        in_specs=[_HBM] * (ns + nt) + [_SEM, _SEM] + [_ANY] * len(after), out_specs=tuple([_HBM] * (ns + nt)),
        input_output_aliases={j: j for j in range(ns + nt)},
        compiler_params=pltpu.CompilerParams(has_side_effects=pltpu.SideEffectType.DATAFLOW_SIDE_EFFECTING),
        interpret=False,
    )(*arrs, ex["send"], ex["recv"], *after)
    return list(out[:ns]), list(out[ns:])


def _scatter_begin(name, srcs):
    lands = [lax.empty(s.shape if s.ndim == 3 else (N_DEV,) + s.shape, s.dtype) for s in srcs]
    return _exchange_start(name, False, srcs, lands)


def _adam(name, w, m, v, parts, own, me, layer, bufs):
    r, c = parts.shape[1:]
    tr = _row_block(r, max(c, 128), 2 << 20)
    nb = r // tr

    def body(me_ref, w_ref, m_ref, v_ref, p_ref, own_ref, *rest):
        g_ref, d_ref, nm_ref, nv_ref, acc = rest[-5:]
        acc[...] = jnp.zeros_like(acc)
        for k in range(N_DEV):
            @pl.when(me_ref[0] == k)
            def _():
                acc[...] += own_ref[...].astype(F32)

            @pl.when(me_ref[0] != k)
            def _(k=k):
                acc[...] += p_ref[k].astype(F32)

        g = acc[...]
        mm = ADAM_B1 * m_ref[...] + (1.0 - ADAM_B1) * g
        vv = ADAM_B2 * v_ref[...] + (1.0 - ADAM_B2) * jnp.square(g)
        m_hat = mm / (1.0 - ADAM_B1 ** ADAM_STEP)
        v_hat = vv / (1.0 - ADAM_B2 ** ADAM_STEP)
        g_ref[...] = g
        d_ref[...] = -ADAM_LR * (m_hat / (jnp.sqrt(v_hat) + ADAM_EPS) + ADAM_WD * w_ref[...])
        nm_ref[...] = mm
        nv_ref[...] = vv

    blk = pl.BlockSpec((tr, c), lambda i, me, o=layer * nb: (o + i, 0))
    own_spec = pl.BlockSpec((None, tr, c), lambda i, me: (me[0], i, 0)) if own.ndim == 3 else pl.BlockSpec((tr, c), lambda i, me: (i, 0))
    in_specs = [blk, blk, blk, pl.BlockSpec((N_DEV, tr, c), lambda i, me: (0, i, 0)), own_spec]
    args = [me, w, m, v, parts, own]
    aliases = {}
    if bufs is not None:
        in_specs += [_ANY] * 4
        aliases = {len(args) + j: j for j in range(4)}
        args += list(bufs)
    grid_spec = pltpu.PrefetchScalarGridSpec(
        num_scalar_prefetch=1, grid=(nb,), in_specs=in_specs, out_specs=[blk] * 4, scratch_shapes=[pltpu.VMEM((tr, c), F32)])
    return pl.pallas_call(
        body, name=name, grid_spec=grid_spec, out_shape=[jax.ShapeDtypeStruct(w.shape, F32)] * 4, input_output_aliases=aliases,
        compiler_params=pltpu.CompilerParams(dimension_semantics=("arbitrary",), vmem_limit_bytes=V7X_VMEM_LIMIT_BYTES),
        interpret=False,
    )(*args)


def _row_block(r, c, limit):
    best = None
    for tr in range(16, r + 1, 16):
        if r % tr == 0 and tr * c * 4 <= limit:
            best = tr
    return r if best is None else best


_BIG = ("w_in_ab", "pool_w", "w_out_ab", "w_in_cd", "glu_w1", "glu_w2", "w_out_cd", "w_xq", "w_xkv", "w_xo")
_STACKED = ("w_in_ab", "w_in_cd", "w_xkv")
_MIXER_BIG = (("w_in_ab", "pool_w", "w_out_ab"), ("w_in_cd", "glu_w1", "glu_w2", "w_out_cd"))
_CROSS_BIG = ("w_xq", "w_xkv", "w_xo")
_SMALL_SPLIT = ["norm_cd", "sgu_ln_g", "sgu_ln_b", "s5_d"]
_REPLICATED_ODD = ["sgu_w", "sgu_b", "s5_a_re", "s5_a_im", "s5_log_dt", "s5_b_re", "s5_b_im", "s5_c_re", "s5_c_im", "final_norm"]
_REPLICATED_EVEN = ["norm_ab", "pool_scale", "norm_x", "mem_norm"]
_REPLICATED = _REPLICATED_ODD + _REPLICATED_EVEN
_WEIGHTS = ["norm_ab", "w_in_ab", "pool_w", "pool_scale", "w_out_ab", "norm_cd", "w_in_cd", "sgu_ln_g", "sgu_ln_b", "sgu_w", "sgu_b", "s5_a_re",
            "s5_a_im", "s5_log_dt", "s5_b_re", "s5_b_im", "s5_c_re", "s5_c_im", "s5_d", "glu_w1", "glu_w2", "w_out_cd", "norm_x", "w_xq",
            "w_xkv", "w_xo", "mem_norm", "final_norm"]


def _layer_big(layer):
    return [(n, layer // 2) for n in _MIXER_BIG[layer % 2]] + [(n, layer) for n in _CROSS_BIG]


def _gather_parts(layer):
    big = _layer_big(layer)
    return [big[:1], big[1:]]


def _scatter_parts(layer):
    big = _layer_big(layer)
    return [big[2:], big[:2]] if layer % 2 == 0 else [big]


def _from_slots(name, a):
    if name in _STACKED:
        return a
    if name == "pool_w":
        return a.reshape(N_DEV, 4, 32, 256).transpose(1, 0, 2, 3).reshape(4, 256, 256)
    return a.reshape(-1, a.shape[-1])


def _to_slots(name, g):
    if name in _STACKED:
        return g
    if name == "pool_w":
        return g.reshape(4, N_DEV, 32, 256).transpose(1, 0, 2, 3).reshape(N_DEV, 128, 256)
    return g.reshape(N_DEV, -1, g.shape[-1])


def _rows2d(a):
    return a.reshape(-1, a.shape[-1])


def _small_rows(block):
    return jnp.pad(block, ((0, 0), (0, 128 - block.shape[1])))


def kernel(x, mem, norm_ab, w_in_ab, pool_w, pool_scale, w_out_ab, norm_cd, w_in_cd, sgu_ln_g, sgu_ln_b, sgu_w, sgu_b, s5_a_re, s5_a_im, s5_log_dt, s5_b_re, s5_b_im, s5_c_re, s5_c_im, s5_d, glu_w1, glu_w2, w_out_cd, norm_x, w_xq, w_xkv, w_xo, mem_norm, final_norm, loss_target, m_norm_ab, m_w_in_ab, m_pool_w, m_pool_scale, m_w_out_ab, m_norm_cd, m_w_in_cd, m_sgu_ln_g, m_sgu_ln_b, m_sgu_w, m_sgu_b, m_s5_a_re, m_s5_a_im, m_s5_log_dt, m_s5_b_re, m_s5_b_im, m_s5_c_re, m_s5_c_im, m_s5_d, m_glu_w1, m_glu_w2, m_w_out_cd, m_norm_x, m_w_xq, m_w_xkv, m_w_xo, m_mem_norm, m_final_norm, v_norm_ab, v_w_in_ab, v_pool_w, v_pool_scale, v_w_out_ab, v_norm_cd, v_w_in_cd, v_sgu_ln_g, v_sgu_ln_b, v_sgu_w, v_sgu_b, v_s5_a_re, v_s5_a_im, v_s5_log_dt, v_s5_b_re, v_s5_b_im, v_s5_c_re, v_s5_c_im, v_s5_d, v_glu_w1, v_glu_w2, v_w_out_cd, v_norm_x, v_w_xq, v_w_xkv, v_w_xo, v_mem_norm, v_final_norm):
    args = locals()
    w = {n: args[n] for n in _WEIGHTS}
    m = {n: args["m_" + n] for n in _WEIGHTS}
    v = {n: args["v_" + n] for n in _WEIGHTS}

    me = jnp.reshape(_lin(_mesh_pos()), (1,)).astype(jnp.int32)

    order = [(layer, p) for layer in range(DEPTH) for p in range(len(_gather_parts(layer)))]
    lands, gathers = {}, {}
    for key in order:
        lands[key] = []
        for name, i in _gather_parts(key[0])[key[1]]:
            b2 = _rows2d(w[name])
            r = b2.shape[0] // w[name].shape[0]
            lands[key].append(_into_slot("cast_" + name, b2, i * r, r, me, BF16, None))
    small_blocks = jnp.concatenate([_small_rows(w[n]) for n in _SMALL_SPLIT], axis=0)
    lands[order[0]].append(_into_slot("cast_small", small_blocks, 0, 8, me, F32, None))

    def begin_gather(key, after):
        gathers[key] = _exchange_start("gather%d%s_start" % (key[0], "ab"[key[1]]), True, [], lands[key], after)
        return gathers[key]["token"]

    W = {n: w[n] for n in _REPLICATED}
    W["mem_norm"] = w["mem_norm"] + begin_gather(order[0], None)
    for name in _BIG:
        W[name] = [None] * w[name].shape[0]

    def weights_of(layer, part, after):
        key = (layer, part)
        if key not in gathers:
            return 0.0
        if key == order[0]:
            after = [after] + [a for k in order[1:] for a in lands[k]]
        _, got = _exchange_wait("gather%d%s_wait" % (layer, "ab"[part]), gathers[key], after)
        for (name, i), arr in zip(_gather_parts(layer)[part], got):
            W[name][i] = _from_slots(name, arr)
        if key == order[0]:
            sm = got[-1].reshape(N_DEV, 4, 2, 128)
            for j, n in enumerate(_SMALL_SPLIT):
                width = w[n].shape[1]
                W[n] = sm[:, j, :, :width].transpose(1, 0, 2).reshape(2, N_DEV * width)
        nxt = order.index(key) + 1
        return begin_gather(order[nxt], got[0]) if nxt < len(order) and order[nxt] not in gathers else 0.0

    def start_ahead(layer, after):
        return begin_gather((layer + 1, 0), after) if layer + 1 < DEPTH else 0.0

    scatters, small = {}, {}

    def send_grads(layer, G, part):
        srcs = [_to_slots(name, G[name][i]) for name, i in _scatter_parts(layer)[part]]
        scatters[layer, part] = _scatter_begin("scatter%d%s_start" % (layer, "ab"[part]), srcs)
        return scatters[layer, part]["token"]

    def rows(a):
        if a.ndim == 1:
            return a.reshape(1, -1)
        return a.reshape(-1, 128) if a.shape[-1] < 128 and a.size % 128 == 0 else a.reshape(-1, a.shape[-1])

    def begin_small(tag, G, names, split):
        srcs = [rows(G[n] if n in ("mem_norm", "final_norm") else jnp.stack(G[n])) for n in names]
        if split:
            srcs += [jnp.stack(G[n]).reshape(2, N_DEV, -1).transpose(1, 0, 2) for n in _SMALL_SPLIT]
        small[tag] = _scatter_begin("scatter_small_%s_start" % tag, srcs)
        return small[tag]["token"]

    def after_layer(layer, G):
        return begin_small("odd", G, _REPLICATED_ODD, True) if layer == 1 else 0.0

    loss, dx, G = _local_step(x[0], mem[0], loss_target[0], W, weights_of, start_ahead, send_grads, after_layer)
    loss = lax.psum(loss[0, 0], MESH_AXES)
    begin_small("even", G, _REPLICATED_EVEN, False)

    out = {}
    after = small["even"]["token_arr"]
    for layer, part in scatters:
        own, got = _exchange_wait("scatter%d%s_wait" % (layer, "ab"[part]), scatters[layer, part], after)
        for (name, i), mine, parts in zip(_scatter_parts(layer)[part], own, got):
            out[name] = _adam("adam_" + name, _rows2d(w[name]), _rows2d(m[name]), _rows2d(v[name]), parts, mine, me, i, out.get(name))
        after = [out[name][0] for name, i in _scatter_parts(layer)[part]]
    for name in _BIG:
        out[name] = [a.reshape(w[name].shape) for a in out[name]]
    for tag, names in (("odd", _REPLICATED_ODD + _SMALL_SPLIT), ("even", _REPLICATED_EVEN)):
        own, got = _exchange_wait("scatter_small_%s_wait" % tag, small[tag], after)
        for n, mine, parts in zip(names, own, got):
            as2d = (lambda a: a) if n in _SMALL_SPLIT else rows
            res = _adam("adam_" + n, as2d(w[n]), as2d(m[n]), as2d(v[n]), parts, mine, me, 0, None)
            out[n] = [a.reshape(w[n].shape) for a in res]
        after = [out[n][0] for n in names]

    return (loss, dx[None], *[out[n][0] for n in _WEIGHTS], *[out[n][1] for n in _WEIGHTS], *[out[n][2] for n in _WEIGHTS],
            *[out[n][3] for n in _WEIGHTS])
```

```python
import functools
import math

import jax
import jax.numpy as jnp
from jax import lax
from jax.experimental import pallas as pl
from jax.experimental.pallas import tpu as pltpu

F32 = jnp.float32
BF16 = jnp.bfloat16

SEQ = 2048
D_MODEL = 1024
MEM_LEN = 256
DEPTH = 4
N_DEV = 8
EPS = 1e-6
NEG = -1e30
A_HEAD_DIM = 64
X_HEAD_DIM = 256
S5_GROUPS = 32
S5_STATE = 64
S5_GROUP_DIM = 16

ADAM_LR = 0.001
ADAM_B1 = 0.9
ADAM_B2 = 0.999
ADAM_EPS = 1e-08
ADAM_WD = 0.01
ADAM_STEP = 10

V7X_VMEM_LIMIT_BYTES = 56 * 1024 * 1024
_MM_VMEM_BYTES = 36 * 1024 * 1024
MESH_AXES = ("x", "y", "c")


def _pc(body, *, name, out_shape, grid=None, in_specs=None, out_specs=None, scratch_shapes=(), aliases=None, sem=None):
    kw = {}
    if grid is not None:
        kw["grid"] = grid
    if in_specs is not None:
        kw["in_specs"] = in_specs
    if out_specs is not None:
        kw["out_specs"] = out_specs
    if aliases:
        kw["input_output_aliases"] = aliases
    return pl.pallas_call(
        body,
        name=name,
        out_shape=out_shape,
        scratch_shapes=list(scratch_shapes),
        compiler_params=pltpu.CompilerParams(dimension_semantics=sem, vmem_limit_bytes=V7X_VMEM_LIMIT_BYTES),
        interpret=False,
        **kw,
    )


def _cols(arr, c0=0, width=None, nsplit=1, r0=0):
    width = arr.shape[1] - c0 if width is None else width
    assert c0 % width == 0 and width % nsplit == 0
    return (arr, c0, width, nsplit, r0)


def _par(arr, nsplit=1):
    return (arr, nsplit)


def _ld(ref, nsplit):
    if nsplit == 1:
        return ref[...].astype(F32)
    if len(ref.shape) == 3:
        return tuple(ref[k].astype(F32) for k in range(nsplit))
    w = ref.shape[-1] // nsplit
    return tuple(ref[:, k * w:(k + 1) * w].astype(F32) for k in range(nsplit))


def _st(ref, val, nsplit, accumulate=False):
    if nsplit == 1:
        val = (val,)
    for k in range(nsplit):
        if nsplit == 1:
            idx = (Ellipsis,)
        elif len(ref.shape) == 3:
            idx = (k,)
        else:
            w = ref.shape[-1] // nsplit
            idx = (slice(None), slice(k * w, (k + 1) * w))
        if accumulate:
            ref[idx] += val[k].astype(ref.dtype)
        else:
            ref[idx] = val[k].astype(ref.dtype)


def _row_spec(tr, op):
    _, c0, w, _, r0 = op
    assert r0 % tr == 0
    return pl.BlockSpec((tr, w), lambda i, cb=c0 // w, rb=r0 // tr: (i + rb, cb))


def _full_spec(arr):
    return pl.BlockSpec(arr.shape, lambda i, nd=arr.ndim: (0,) * nd)


def _rw_fwd(name, f, rows, pars, outs, tr, n_rows=None):
    n_rows = rows[0][0].shape[0] if n_rows is None else n_rows
    nr, npar = len(rows), len(pars)

    def body(*refs):
        r = [_ld(refs[i], rows[i][3]) for i in range(nr)]
        p = [_ld(refs[nr + i], pars[i][1]) for i in range(npar)]
        res = f(r, p)
        for k, (_, _, ns) in enumerate(outs):
            _st(refs[nr + npar + k], res[k], ns)

    res = _pc(
        body, name=name, grid=(n_rows // tr,),
        in_specs=[_row_spec(tr, op) for op in rows] + [_full_spec(a) for a, _ in pars],
        out_specs=[pl.BlockSpec((tr, w), lambda i: (i, 0)) for w, _, _ in outs],
        out_shape=[jax.ShapeDtypeStruct((n_rows, w), dt) for w, dt, _ in outs],
        sem=("arbitrary",),
    )(*[op[0] for op in rows], *[a for a, _ in pars])
    return list(res)


def _rw_bwd(name, f, rows, pars, douts, drow, dpar, tr):
    n_rows = rows[0][0].shape[0]
    nr, npar = len(rows), len(pars)
    dgiven = [d for d in douts if d is not None]
    nd = len(dgiven)

    def body(*refs):
        r = [_ld(refs[i], rows[i][3]) for i in range(nr)]
        p = [_ld(refs[nr + i], pars[i][1]) for i in range(npar)]
        d = [_ld(refs[nr + npar + i], dgiven[i][3]) for i in range(nd)]
        orefs = refs[nr + npar + nd:]

        def g(dr, dp):
            rr, pp = list(r), list(p)
            for j, (idx, _) in enumerate(drow):
                rr[idx] = dr[j]
            for j, idx in enumerate(dpar):
                pp[idx] = dp[j]
            return tuple(f(rr, pp))

        out, vjp = jax.vjp(g, [r[idx] for idx, _ in drow], [p[idx] for idx in dpar])
        ct, j = [], 0
        for k, o in enumerate(out):
            if douts[k] is None:
                ct.append(jax.tree.map(jnp.zeros_like, o))
            else:
                ct.append(d[j])
                j += 1
        gdr, gdp = vjp(tuple(ct))
        for j, (idx, _) in enumerate(drow):
            _st(orefs[j], gdr[j], rows[idx][3])

        @pl.when(pl.program_id(0) == 0)
        def _():
            for j in range(len(dpar)):
                orefs[len(drow) + j][...] = jnp.zeros_like(orefs[len(drow) + j])

        for j, idx in enumerate(dpar):
            _st(orefs[len(drow) + j], gdp[j], pars[idx][1], accumulate=True)

    res = _pc(
        body, name=name, grid=(n_rows // tr,),
        in_specs=[_row_spec(tr, op) for op in rows] + [_full_spec(a) for a, _ in pars] + [_row_spec(tr, op) for op in dgiven],
        out_specs=[pl.BlockSpec((tr, rows[idx][2]), lambda i: (i, 0)) for idx, _ in drow] + [_full_spec(pars[idx][0]) for idx in dpar],
        out_shape=[jax.ShapeDtypeStruct((n_rows, rows[idx][2]), dt) for idx, dt in drow]
        + [jax.ShapeDtypeStruct(pars[idx][0].shape, F32) for idx in dpar],
        sem=("arbitrary",),
    )(*[op[0] for op in rows], *[a for a, _ in pars], *[op[0] for op in dgiven])
    res = list(res)
    return res[:len(drow)], res[len(drow):]


_NORM_ROWS = 512


def _sigmoid(x):
    return jax.nn.sigmoid(x)


def _silu(x):
    return x * _sigmoid(x)


def _rms(x, g):
    return x * lax.rsqrt(jnp.mean(x * x, axis=-1, keepdims=True) + EPS) * g


def _f_rms(r, p):
    return [_rms(r[0], p[0])]


def _f_rms_res(r, p):
    return [r[0], _rms(r[0], p[0])]


def _f_gate_ab(r, p):
    o, ga, mixed, gb = r
    return [(o * _silu(ga), mixed * p[0] * _silu(gb))]


def _f_sgu(r, p):
    u, v, gc = r
    lg, lb, w, b = p
    n = float(D_MODEL)
    mu = sum(jnp.sum(vk, axis=-1, keepdims=True) for vk in v) / n
    var = sum(jnp.sum(jnp.square(vk - mu), axis=-1, keepdims=True) for vk in v) / n
    rs = lax.rsqrt(var + EPS)
    t = w[0].shape[0]
    tri = lax.broadcasted_iota(jnp.int32, (t, t), 0) >= lax.broadcasted_iota(jnp.int32, (t, t), 1)
    outs = []
    for k in range(len(v)):
        vn = (v[k] - mu) * rs * lg[k] + lb[k]
        mixed = jnp.dot(jnp.where(tri, w[k], 0.0), vn, preferred_element_type=F32) + b[k]
        outs.append(u[k] * mixed * _silu(gc[k]))
    return [tuple(outs)]


def _gelu(x):
    return 0.5 * x * (1.0 + jnp.tanh(math.sqrt(2.0 / math.pi) * (x + 0.044715 * (x * x * x))))


def _f_gelu_y(r, p):
    yc, uf = r
    return [_gelu(yc + p[0] * uf)]


def _f_glu_gate(r, p):
    t12, gd = r
    return [t12[0] * _sigmoid(t12[1]) * _silu(gd)]


def _f_s5_prep(r, p):
    ar, ai, ldt = r
    dt = jnp.exp(ldt)
    mag = jnp.exp(dt * ar)
    abar_re = mag * jnp.cos(dt * ai)
    abar_im = mag * jnp.sin(dt * ai)
    nr, ni = abar_re - 1.0, abar_im
    inv = 1.0 / (ar * ar + ai * ai)
    return [abar_re, abar_im, (nr * ar + ni * ai) * inv, (ni * ar - nr * ai) * inv]


def _f_bbar(r, p):
    br, bi, cr, ci = r
    return [cr * br - ci * bi, cr * bi + ci * br]


def _loss_head(x, target, g):
    tr = 256
    n_rows, width = x.shape

    def f(xv, gv, tv):
        err = jnp.square(_rms(xv, gv) - tv)
        return 0.5 * jnp.mean(err, axis=-1, keepdims=True)

    def body(x_ref, t_ref, g_ref, loss_ref, dx_ref, dg_ref):
        @pl.when(pl.program_id(0) == 0)
        def _():
            loss_ref[...] = jnp.zeros_like(loss_ref)
            dg_ref[...] = jnp.zeros_like(dg_ref)

        tv = t_ref[...]
        row_loss, vjp = jax.vjp(lambda a, b: f(a, b, tv), x_ref[...], g_ref[...])
        dx, dg = vjp(jnp.ones_like(row_loss))
        dx_ref[...] = dx
        dg_ref[...] += dg
        loss_ref[...] += jnp.broadcast_to(jnp.sum(row_loss, axis=0, keepdims=True), loss_ref.shape)

    blk = pl.BlockSpec((tr, width), lambda i: (i, 0))
    one = pl.BlockSpec((1, width), lambda i: (0, 0))
    return _pc(
        body, name="loss_head", grid=(n_rows // tr,), in_specs=[blk, blk, one],
        out_specs=[pl.BlockSpec((1, 128), lambda i: (0, 0)), blk, one],
        out_shape=[jax.ShapeDtypeStruct((1, 128), F32), jax.ShapeDtypeStruct(x.shape, F32), jax.ShapeDtypeStruct((1, width), F32)],
        sem=("arbitrary",),
    )(x, target, g)


_NT = (((1,), (1,)), ((), ()))
_TN = (((0,), (0,)), ((), ()))


def _tile(n, cap):
    t = min(n, cap)
    while n % t:
        t -= 128
    assert t > 0
    return t


def _mm(name, a, b, *, ta=False, tb=False, out_dtype=F32, a_off=0, a_width=None, res=None, out_stack=None):
    assert not (ta and tb)
    stacked = b.ndim == 3
    bk, bn = (b.shape[1], b.shape[0] * b.shape[2]) if stacked else b.shape
    if ta:
        kc = a.shape[0]
        m = a.shape[1] - a_off if a_width is None else a_width
        n = bn
        assert bk == kc and not stacked
    else:
        m = a.shape[0]
        kc = a.shape[1] - a_off if a_width is None else a_width
        n = bk if tb else bn
        assert (bn if tb else bk) == kc
    tn = _tile(b.shape[2] if stacked and not tb else (out_stack or n), 1024)
    size = lambda dt: jnp.dtype(dt).itemsize
    for tk_cap, tm_cap in ((2048, 2048), (2048, 1024), (1024, 1024), (1024, 512), (1024, 256)):
        tm, tk = _tile(m, tm_cap), _tile(b.shape[2] if stacked and tb else kc, tk_cap)
        nk = kc // tk
        vmem = 2 * tm * tk * size(a.dtype) + 2 * tk * tn * size(b.dtype) + tm * tn * (2 * size(out_dtype) + (4 if nk > 1 else 0) + (8 if res is not None else 0))
        if vmem <= _MM_VMEM_BYTES:
            break
    if ta:
        assert a_off % tm == 0
        a_spec = pl.BlockSpec((tk, tm), lambda i, j, k, o=a_off // tm: (k, i + o))
        dims = _TN
    else:
        assert a_off % tk == 0
        a_spec = pl.BlockSpec((tm, tk), lambda i, j, k, o=a_off // tk: (i, k + o))
        dims = _NT if tb else (((1,), (0,)), ((), ()))
    if stacked and tb:
        b_spec = pl.BlockSpec((None, tn, tk), lambda i, j, k, q=b.shape[2] // tk: (k // q, j, k % q))
    elif stacked:
        b_spec = pl.BlockSpec((None, tk, tn), lambda i, j, k, q=b.shape[2] // tn: (j // q, k, j % q))
    else:
        b_spec = pl.BlockSpec((tn, tk), lambda i, j, k: (j, k)) if tb else pl.BlockSpec((tk, tn), lambda i, j, k: (k, j))
    if out_stack:
        out_spec = pl.BlockSpec((None, tm, tn), lambda i, j, k, q=out_stack // tn: (j // q, i, j % q))
        out_shape = jax.ShapeDtypeStruct((n // out_stack, m, out_stack), out_dtype)
    else:
        out_spec = pl.BlockSpec((tm, tn), lambda i, j, k: (i, j))
        out_shape = jax.ShapeDtypeStruct((m, n), out_dtype)
    in_specs, args = [a_spec, b_spec], [a, b]
    has_res = res is not None
    if has_res:
        in_specs.append(pl.BlockSpec((tm, tn), lambda i, j, k: (i, j)))
        args.append(res)

    def finish(refs, acc):
        if has_res:
            acc = acc + refs[2][...].astype(F32)
        refs[3 if has_res else 2][...] = acc.astype(out_dtype)

    def body_one(*refs):
        finish(refs, lax.dot_general(refs[0][...].astype(BF16), refs[1][...].astype(BF16), dims, preferred_element_type=F32))

    def body(*refs):
        acc_ref = refs[-1]
        k = pl.program_id(2)

        @pl.when(k == 0)
        def _():
            acc_ref[...] = jnp.zeros_like(acc_ref)

        acc_ref[...] += lax.dot_general(refs[0][...].astype(BF16), refs[1][...].astype(BF16), dims, preferred_element_type=F32)

        @pl.when(k == nk - 1)
        def _():
            finish(refs, acc_ref[...])

    return _pc(
        body_one if nk == 1 else body, name=name, grid=(m // tm, n // tn, nk), in_specs=in_specs, out_specs=out_spec, out_shape=out_shape,
        scratch_shapes=[] if nk == 1 else [pltpu.VMEM((tm, tn), F32)], sem=("parallel", "parallel", "arbitrary"),
    )(*args)


def _mm_blocks(name, a, b, *, grid, out_blk, a_blk, a_idx, b_blk, b_idx, dims, out_dtype=F32):
    gi, gj, nk = grid

    def body(a_ref, b_ref, o_ref, acc_ref):
        k = pl.program_id(2)

        @pl.when(k == 0)
        def _():
            acc_ref[...] = jnp.zeros_like(acc_ref)

        acc_ref[...] += lax.dot_general(a_ref[...].astype(BF16), b_ref[...].astype(BF16), dims, preferred_element_type=F32)

        @pl.when(k == nk - 1)
        def _():
            o_ref[...] = acc_ref[...].astype(o_ref.dtype)

    return _pc(
        body, name=name, grid=grid, in_specs=[pl.BlockSpec(a_blk, a_idx), pl.BlockSpec(b_blk, b_idx)],
        out_specs=pl.BlockSpec(out_blk, lambda i, j, k: (i, j)), out_shape=jax.ShapeDtypeStruct((gi * out_blk[0], gj * out_blk[1]), out_dtype),
        scratch_shapes=[pltpu.VMEM(out_blk, F32)], sem=("parallel", "parallel", "arbitrary"),
    )(a, b)


def _head_masks(width, nsub):
    lane = lax.broadcasted_iota(jnp.int32, (1, width), 1)
    hd = width // nsub
    return [(lane >= h * hd) & (lane < (h + 1) * hd) for h in range(nsub)]


def _dilated_log_count(row0, tq, ext):
    delta = (row0 + lax.broadcasted_iota(jnp.int32, (tq, ext), 0)) - lax.broadcasted_iota(jnp.int32, (tq, ext), 1)
    cnt = (delta <= 128).astype(jnp.int32) + (((delta & 3) == 0) & (delta <= 512)).astype(jnp.int32) + ((delta & 15) == 0).astype(jnp.int32)
    logc = jnp.where(cnt == 3, math.log(3.0), jnp.where(cnt == 2, math.log(2.0), 0.0))
    return jnp.where((delta >= 0) & (cnt > 0), logc, NEG)


def _bias_table(tab, nq, tq):
    @pl.when(pl.program_id(0) == 0)
    def _():
        for d in range(nq):
            tab[d] = _dilated_log_count(d * tq, tq, tq)


def _scores(q, ke, tab, r, masks, h, c_lo=0, c_hi=None):
    qm = (jnp.where(masks[h], q, 0.0) if len(masks) > 1 else q).astype(BF16)
    s = lax.dot_general(qm, ke, _NT, preferred_element_type=F32)
    if tab is not None:
        s = s + jnp.concatenate([tab[r - c] for c in range(c_lo, r + 1 if c_hi is None else c_hi)], axis=1)
    return qm, s


def _head_value(masks, h, tile, reduce, fill):
    return reduce(jnp.where(masks[h], tile, fill) if len(masks) > 1 else tile, axis=-1, keepdims=True)


def _attn_fwd(name, qa, ka, va, *, qc, kc, vc, width, nblk, nsub, causal, tq, scale, out_dtype):
    sq, t_len = qa.shape[0], ka.shape[0]
    nq = sq // tq

    def body(q_ref, k_ref, v_ref, o_ref, lse_ref, *scratch):
        tab = scratch[0] if causal else None
        if causal:
            _bias_table(tab, nq, tq)
        kb = k_ref[...].astype(BF16)
        vb = v_ref[...].astype(BF16)
        masks = _head_masks(width, nsub)
        for r in range(nq):
            ext = (r + 1) * tq if causal else t_len
            q = q_ref[r * tq:(r + 1) * tq, :].astype(F32) * scale
            ke, ve = kb[:ext], vb[:ext]
            o = lse = None
            for h in range(nsub):
                _, s = _scores(q, ke, tab, r, masks, h)
                m = jnp.max(s, axis=-1, keepdims=True)
                p = jnp.exp(s - m)
                l = jnp.sum(p, axis=-1, keepdims=True)
                oh = jnp.dot(p.astype(BF16), ve, preferred_element_type=F32) * (1.0 / l)
                lh = jnp.broadcast_to(m + jnp.log(l), (tq, width))
                o = oh if o is None else jnp.where(masks[h], oh, o)
                lse = lh if lse is None else jnp.where(masks[h], lh, lse)
            o_ref[r * tq:(r + 1) * tq, :] = o.astype(o_ref.dtype)
            lse_ref[r * tq:(r + 1) * tq, :] = lse

    blk = pl.BlockSpec((sq, width), lambda i: (0, i))
    return _pc(
        body, name=name, grid=(nblk,),
        in_specs=[pl.BlockSpec((sq, width), lambda i, c=qc: (0, c + i)), pl.BlockSpec((t_len, width), lambda i, c=kc: (0, c + i)),
                  pl.BlockSpec((t_len, width), lambda i, c=vc: (0, c + i))],
        out_specs=[blk, blk],
        out_shape=[jax.ShapeDtypeStruct((sq, nblk * width), out_dtype), jax.ShapeDtypeStruct((sq, nblk * width), F32)],
        scratch_shapes=[pltpu.VMEM((nq, tq, tq), F32)] if causal else [], sem=("arbitrary",),
    )(qa, ka, va)


def _attn_bwd(name, qa, ka, va, doa, oa, lsea, *, qc, kc, vc, width, nblk, nsub, causal, tq, scale, out_dtype):
    sq, t_len = qa.shape[0], ka.shape[0]
    nq = sq // tq

    def body(q_ref, k_ref, v_ref, do_ref, o_ref, lse_ref, dq_ref, dk_ref, dv_ref, dk_acc, dv_acc, *scratch):
        tab = scratch[0] if causal else None
        if causal:
            _bias_table(tab, nq, tq)
        kb = k_ref[...].astype(BF16)
        vb = v_ref[...].astype(BF16)
        masks = _head_masks(width, nsub)
        dk_acc[...] = jnp.zeros_like(dk_acc)
        dv_acc[...] = jnp.zeros_like(dv_acc)
        for r in range(nq):
            ext = (r + 1) * tq if causal else t_len
            q = q_ref[r * tq:(r + 1) * tq, :].astype(F32) * scale
            do = do_ref[r * tq:(r + 1) * tq, :].astype(F32)
            do_o = do * o_ref[r * tq:(r + 1) * tq, :].astype(F32)
            lse = lse_ref[r * tq:(r + 1) * tq, :]
            dq = None
            for h in range(nsub):
                dom = (jnp.where(masks[h], do, 0.0) if nsub > 1 else do).astype(BF16)
                lse_h = _head_value(masks, h, lse, jnp.max, -jnp.inf)
                delta_h = _head_value(masks, h, do_o, jnp.sum, 0.0)
                dqh = None
                for c0 in range(0, ext, _BWD_KEYS):
                    c1 = min(c0 + _BWD_KEYS, ext)
                    ke, ve = kb[c0:c1], vb[c0:c1]
                    qm, s = _scores(q, ke, tab, r, masks, h, c0 // tq, c1 // tq)
                    pn = jnp.exp(s - lse_h)
                    dpn = lax.dot_general(dom, ve, _NT, preferred_element_type=F32)
                    dsb = (pn * (dpn - delta_h)).astype(BF16)
                    part = jnp.dot(dsb, ke, preferred_element_type=F32)
                    dqh = part if dqh is None else dqh + part
                    dk_acc[c0:c1, :] += lax.dot_general(dsb, qm, _TN, preferred_element_type=F32)
                    dv_acc[c0:c1, :] += lax.dot_general(pn.astype(BF16), dom, _TN, preferred_element_type=F32)
                dq = dqh if dq is None else jnp.where(masks[h], dqh, dq)
            dq_ref[r * tq:(r + 1) * tq, :] = (dq * scale).astype(dq_ref.dtype)
        dk_ref[...] = dk_acc[...].astype(dk_ref.dtype)
        dv_ref[...] = dv_acc[...].astype(dv_ref.dtype)

    return _pc(
        body, name=name, grid=(nblk,),
        in_specs=[pl.BlockSpec((sq, width), lambda i, c=qc: (0, c + i)), pl.BlockSpec((t_len, width), lambda i, c=kc: (0, c + i)),
                  pl.BlockSpec((t_len, width), lambda i, c=vc: (0, c + i))] + [pl.BlockSpec((sq, width), lambda i: (0, i))] * 3,
        out_specs=[pl.BlockSpec((sq, width), lambda i: (0, i)), pl.BlockSpec((t_len, width), lambda i: (0, i)), pl.BlockSpec((t_len, width), lambda i: (0, i))],
        out_shape=[jax.ShapeDtypeStruct((sq, nblk * width), out_dtype), jax.ShapeDtypeStruct((t_len, nblk * width), out_dtype),
                   jax.ShapeDtypeStruct((t_len, nblk * width), out_dtype)],
        scratch_shapes=[pltpu.VMEM((t_len, width), F32), pltpu.VMEM((t_len, width), F32)] + ([pltpu.VMEM((nq, tq, tq), F32)] if causal else []),
        sem=("arbitrary",),
    )(qa, ka, va, doa, oa, lsea)


_BWD_KEYS = 512
_SELF = dict(qc=0, kc=8, vc=16, width=128, nblk=8, nsub=2, causal=True, tq=256, scale=A_HEAD_DIM ** -0.5)
_CROSS = dict(qc=0, kc=0, vc=4, width=256, nblk=4, nsub=1, causal=False, tq=512, scale=X_HEAD_DIM ** -0.5)


def _window_sum(x, g, row, backward):
    n = x.shape[0]

    def shift(y, k):
        if backward:
            return jnp.where(row < n - k, pltpu.roll(y, n - k, 0), 0.0)
        return jnp.where(row >= k, pltpu.roll(y, k, 0), 0.0)

    s2 = x + shift(x, 1)
    s4 = s2 + shift(s2, 2)
    s8 = s4 + shift(s4, 4)
    s16 = s8 + shift(s8, 8)
    return jnp.where(g == 0, s2, jnp.where(g == 1, s4, jnp.where(g == 2, s8, s16)))


def _pool(name, arr, c0, backward, out_dtype):
    n = arr.shape[0]
    gw = 256

    def body(v_ref, o_ref):
        g = pl.program_id(0)
        v = v_ref[...].astype(F32)
        row = lax.broadcasted_iota(jnp.int32, v.shape, 0)
        w = jnp.where(g == 0, 2, jnp.where(g == 1, 4, jnp.where(g == 2, 8, 16)))
        cnt = jnp.minimum(row + 1, w).astype(F32)
        if backward:
            o_ref[...] = (_window_sum(v / cnt, g, row, True) - v).astype(o_ref.dtype)
        else:
            o_ref[...] = (_window_sum(v, g, row, False) / cnt - v).astype(o_ref.dtype)

    return _pc(
        body, name=name, grid=(4,), in_specs=[pl.BlockSpec((n, gw), lambda i, c=c0 // gw: (0, c + i))],
        out_specs=pl.BlockSpec((n, gw), lambda i: (0, i)), out_shape=jax.ShapeDtypeStruct((n, 4 * gw), out_dtype), sem=("parallel",),
    )(arr)


_SCAN_ROWS = 256


def _scan_fwd(bu3, a2):
    n = bu3.shape[0]

    def body(bu_ref, a_ref, h_ref, carry):
        @pl.when(pl.program_id(0) == 0)
        def _():
            carry[...] = jnp.zeros_like(carry)

        ar, ai = a_ref[0:16, :], a_ref[16:32, :]

        def step(t, c):
            hr, hi = c
            nr = ar * hr - ai * hi + bu_ref[t, 0:16, :]
            ni = ar * hi + ai * hr + bu_ref[t, 16:32, :]
            h_ref[t, 0:16, :] = nr
            h_ref[t, 16:32, :] = ni
            return nr, ni

        hr, hi = lax.fori_loop(0, _SCAN_ROWS, step, (carry[0:16, :], carry[16:32, :]), unroll=8)
        carry[0:16, :] = hr
        carry[16:32, :] = hi

    blk = pl.BlockSpec((_SCAN_ROWS, 32, 128), lambda i: (i, 0, 0))
    return _pc(
        body, name="s5_scan_fwd", grid=(n // _SCAN_ROWS,), in_specs=[blk, pl.BlockSpec((32, 128), lambda i: (0, 0))], out_specs=blk,
        out_shape=jax.ShapeDtypeStruct(bu3.shape, F32), scratch_shapes=[pltpu.VMEM((32, 128), F32)], sem=("arbitrary",),
    )(bu3, a2)


def _scan_bwd(dh3, h3, a2):
    n = dh3.shape[0]
    nb = n // _SCAN_ROWS

    def body(dh_ref, h_ref, a_ref, dbu_ref, da_ref, carry):
        @pl.when(pl.program_id(0) == 0)
        def _():
            carry[...] = jnp.zeros_like(carry)
            da_ref[...] = jnp.zeros_like(da_ref)

        ar, ai = a_ref[0:16, :], a_ref[16:32, :]

        def step(tt, c):
            gr, gi, dar, dai = c
            t = _SCAN_ROWS - 1 - tt
            hr, hi = h_ref[t, 0:16, :], h_ref[t, 16:32, :]
            dar = dar + gr * hr + gi * hi
            dai = dai - gr * hi + gi * hr
            ngr = dh_ref[t, 0:16, :] + ar * gr + ai * gi
            ngi = dh_ref[t, 16:32, :] - ai * gr + ar * gi
            dbu_ref[t, 0:16, :] = ngr
            dbu_ref[t, 16:32, :] = ngi
            return ngr, ngi, dar, dai

        z = jnp.zeros((16, 128), F32)
        gr, gi, dar, dai = lax.fori_loop(0, _SCAN_ROWS, step, (carry[0:16, :], carry[16:32, :], z, z), unroll=8)
        carry[0:16, :] = gr
        carry[16:32, :] = gi
        da_ref[0:16, :] += dar
        da_ref[16:32, :] += dai

    blk = pl.BlockSpec((_SCAN_ROWS, 32, 128), lambda i: (nb - 1 - i, 0, 0))
    small = pl.BlockSpec((32, 128), lambda i: (0, 0))
    return _pc(
        body, name="s5_scan_bwd", grid=(nb,), in_specs=[blk, blk, small], out_specs=[blk, small],
        out_shape=[jax.ShapeDtypeStruct(dh3.shape, F32), jax.ShapeDtypeStruct((32, 128), F32)],
        scratch_shapes=[pltpu.VMEM((32, 128), F32)], sem=("arbitrary",),
    )(dh3, h3, a2)


def _bdense(bb_re, bb_im):
    eye = jnp.eye(8, dtype=F32)

    def one(bb):
        return jnp.einsum("sgph,gk->sghkp", bb.reshape(4, 8, S5_STATE, S5_GROUP_DIM), eye).reshape(512, 512)

    return jnp.concatenate([one(bb_re), one(bb_im)], axis=1)


def _cdense(c_re, c_im):
    eye = jnp.eye(8, dtype=F32)

    def one(cc):
        return jnp.einsum("sghp,gk->sgpkh", cc.reshape(4, 8, S5_GROUP_DIM, S5_STATE), eye).reshape(2048, 128)

    return jnp.concatenate([one(c_re), -one(c_im)], axis=0)


_NN = (((1,), (0,)), ((), ()))
_UF_BLOCK = 3072 // 128


def _s5_bu(z, bd):
    return _mm_blocks("mm_s5_bu", z, bd, grid=(1, 8, 1), out_blk=(SEQ, 512), a_blk=(SEQ, 128), a_idx=lambda i, j, k: (0, _UF_BLOCK + j % 4),
                      b_blk=(128, 512), b_idx=lambda i, j, k: (j % 4, j // 4), dims=_NN)


def _s5_bu_dx(dbu, bd):
    return _mm_blocks("mm_s5_bu_dx", dbu, bd, grid=(1, 4, 2), out_blk=(SEQ, 128), a_blk=(SEQ, 512), a_idx=lambda i, j, k: (0, 4 * k + j),
                      b_blk=(128, 512), b_idx=lambda i, j, k: (j, k), dims=_NT)


def _s5_bu_dw(z, dbu):
    return _mm_blocks("mm_s5_bu_dw", z, dbu, grid=(4, 2, 2), out_blk=(128, 512), a_blk=(1024, 128), a_idx=lambda i, j, k: (k, _UF_BLOCK + i),
                      b_blk=(1024, 512), b_idx=lambda i, j, k: (k, 4 * j + i), dims=_TN)


def _s5_y(h2, cf):
    return _mm_blocks("mm_s5_y", h2, cf, grid=(1, 4, 2), out_blk=(SEQ, 128), a_blk=(SEQ, 512), a_idx=lambda i, j, k: (0, 4 * k + j),
                      b_blk=(512, 128), b_idx=lambda i, j, k: (4 * k + j, 0), dims=_NN)


def _s5_y_dx(dyc, cf):
    return _mm_blocks("mm_s5_y_dx", dyc, cf, grid=(1, 8, 1), out_blk=(SEQ, 512), a_blk=(SEQ, 128), a_idx=lambda i, j, k: (0, j % 4),
                      b_blk=(512, 128), b_idx=lambda i, j, k: (j, 0), dims=_NT)


def _s5_y_dw(h2, dyc):
    return _mm_blocks("mm_s5_y_dw", h2, dyc, grid=(8, 1, 2), out_blk=(512, 128), a_blk=(1024, 512), a_idx=lambda i, j, k: (k, i),
                      b_blk=(1024, 128), b_idx=lambda i, j, k: (k, i % 4), dims=_TN)


def _pool_dense(pw):
    eye = jnp.eye(4, dtype=pw.dtype)
    return jnp.einsum("gcd,gk->gckd", pw, eye).reshape(1024, 1024)


def _row2(v):
    return v.reshape(1, -1)


def _even_fwd(x, W, i, zero, rest_of_weights, start_ahead):
    hn = _rw_fwd("rms_fwd", _f_rms, [_cols(x)], [_par(_row2(W["norm_ab"][i]) + zero)], [(D_MODEL, BF16, 1)], _NORM_ROWS)[0]
    z = _mm("mm_in_ab", hn, W["w_in_ab"][i], out_dtype=BF16)
    o, lse = _attn_fwd("attn_self_fwd", z, z, z, out_dtype=F32, **_SELF)
    zero = rest_of_weights(o)
    pooled = _pool("pool_fwd", z, 4096, False, BF16)
    wp = _pool_dense(W["pool_w"][i])
    mixed = _mm("mm_pool", pooled, wp)
    scale = _row2(W["pool_scale"][i]) + zero
    ab = _rw_fwd("gate_ab_fwd", _f_gate_ab, [_cols(o), _cols(z, 3072, 1024), _cols(mixed), _cols(z, 5120, 1024)], [_par(scale)],
                 [(2048, BF16, 2)], 256)[0]
    x1 = _mm("mm_out_ab", ab, W["w_out_ab"][i], res=x)
    return x1, dict(x=x, hn=hn, z=z, o=o, lse=lse, pooled=pooled, wp=wp, mixed=mixed, ab=ab), 0.0


def _even_bwd(dx1, sv, W, G, i, send):
    x, hn, z = sv["x"], sv["hn"], sv["z"]
    dab = _mm("mm_out_ab_dx", dx1, W["w_out_ab"][i], tb=True)
    G["w_out_ab"][i] = _mm("mm_out_ab_dw", sv["ab"], dx1, ta=True, out_dtype=BF16)
    scale = _row2(W["pool_scale"][i]) + send(0)
    (do, dga, dmixed, dgb), (dscale,) = _rw_bwd(
        "gate_ab_bwd", _f_gate_ab, [_cols(sv["o"]), _cols(z, 3072, 1024), _cols(sv["mixed"]), _cols(z, 5120, 1024)], [_par(scale)],
        [_cols(dab, nsplit=2)], [(0, F32), (1, BF16), (2, BF16), (3, BF16)], [0], 256)
    G["pool_scale"][i] = dscale.reshape(-1)
    dpooled = _mm("mm_pool_dx", dmixed, sv["wp"], tb=True)
    dwp = _mm("mm_pool_dw", sv["pooled"], dmixed, ta=True, out_dtype=BF16)
    G["pool_w"][i] = jnp.stack([dwp[g * 256:(g + 1) * 256, g * 256:(g + 1) * 256] for g in range(4)])
    dvb = _pool("pool_bwd", dpooled, 0, True, BF16)
    dq, dk, dv = _attn_bwd("attn_self_bwd", z, z, z, do, sv["o"], sv["lse"], out_dtype=BF16, **_SELF)
    dz = jnp.concatenate([dq, dk, dv, dga, dvb, dgb], axis=1)
    dhn = _mm("mm_in_ab_dx", dz, W["w_in_ab"][i], tb=True)
    G["w_in_ab"][i] = _mm("mm_in_ab_dw", hn, dz, ta=True, out_dtype=BF16, out_stack=W["w_in_ab"][i].shape[2])
    g = _row2(W["norm_ab"][i]) + send(1)
    (dx,), (dg,) = _rw_bwd("rms_bwd", _f_rms_res, [_cols(x)], [_par(g)], [_cols(dx1), _cols(dhn)], [(0, F32)], [0], _NORM_ROWS)
    G["norm_ab"][i] = dg.reshape(-1)
    return dx


def _odd_fwd(x, W, i, zero, rest_of_weights, start_ahead):
    hn = _rw_fwd("rms_fwd", _f_rms, [_cols(x)], [_par(_row2(W["norm_cd"][i]) + zero)], [(D_MODEL, BF16, 1)], _NORM_ROWS)[0]
    z = _mm("mm_in_cd", hn, W["w_in_cd"][i], out_dtype=BF16)
    sgu_p = [_par(_row2(W["sgu_ln_g"][i]), 4), _par(_row2(W["sgu_ln_b"][i]), 4), _par(W["sgu_w"][i], 4), _par(W["sgu_b"][i][..., None], 4)]
    c_out = _rw_fwd("sgu_fwd", _f_sgu, [_cols(z, 0, 1024, 4), _cols(z, 1024, 1024, 4), _cols(z, 2048, 1024, 4)], sgu_p, [(1024, BF16, 4)], 128)[0]
    prep_rows = [_cols(W["s5_a_re"][i]), _cols(W["s5_a_im"][i]), _cols(W["s5_log_dt"][i].reshape(S5_GROUPS, 1))]
    abar_re, abar_im, coef_re, coef_im = _rw_fwd("s5_prep_fwd", _f_s5_prep, prep_rows, [], [(S5_STATE, F32, 1)] * 4, S5_GROUPS)
    bb_rows = [_cols(W["s5_b_re"][i].reshape(2048, 16)), _cols(W["s5_b_im"][i].reshape(2048, 16)), _cols(coef_re.reshape(2048, 1)), _cols(coef_im.reshape(2048, 1))]
    bb_re, bb_im = _rw_fwd("s5_bbar_fwd", _f_bbar, bb_rows, [], [(16, F32, 1)] * 2, 256)
    bd = _bdense(bb_re, bb_im).astype(BF16)
    cf = _cdense(W["s5_c_re"][i], W["s5_c_im"][i]).astype(BF16)
    a2 = jnp.concatenate([abar_re.reshape(16, 128), abar_im.reshape(16, 128)], axis=0)
    bu = _s5_bu(z, bd)
    h3 = _scan_fwd(bu.reshape(SEQ, 32, 128), a2)
    h2 = h3.reshape(SEQ, 4096)
    yc = _s5_y(h2, cf)
    dpar = _row2(W["s5_d"][i]) + start_ahead(h3)
    yg = _rw_fwd("gelu_fwd", _f_gelu_y, [_cols(yc), _cols(z, 3072, 512)], [_par(dpar)], [(512, BF16, 1)], 256)[0]
    zero = rest_of_weights(yg)
    w12 = jnp.concatenate([W["glu_w1"][i], W["glu_w2"][i]], axis=1)
    t12 = _mm("mm_glu", yg, w12)
    d_out = _rw_fwd("glu_gate_fwd", _f_glu_gate, [_cols(t12, nsplit=2), _cols(z, 3584, 512)], [], [(512, BF16, 1)], 256)[0]
    cd = jnp.concatenate([c_out, d_out], axis=1)
    x1 = _mm("mm_out_cd", cd, W["w_out_cd"][i], res=x)
    sv = dict(x=x, hn=hn, z=z, sgu_p=sgu_p, prep_rows=prep_rows, bb_rows=bb_rows, bb=(bb_re, bb_im), bd=bd, cf=cf, a2=a2,
              h3=h3, h2=h2, yc=yc, dpar=dpar, yg=yg, w12=w12, t12=t12, cd=cd)
    return x1, sv, zero


def _odd_bwd(dx1, sv, W, G, i, send):
    x, hn, z = sv["x"], sv["hn"], sv["z"]
    dcd = _mm("mm_out_cd_dx", dx1, W["w_out_cd"][i], tb=True)
    G["w_out_cd"][i] = _mm("mm_out_cd_dw", sv["cd"], dx1, ta=True, out_dtype=BF16)
    (du, dv, dgc), (dlg, dlb, dsw, dsb) = _rw_bwd(
        "sgu_bwd", _f_sgu, [_cols(z, 0, 1024, 4), _cols(z, 1024, 1024, 4), _cols(z, 2048, 1024, 4)], sv["sgu_p"],
        [_cols(dcd, 0, 1024, 4)], [(0, BF16), (1, BF16), (2, BF16)], [0, 1, 2, 3], 128)
    G["sgu_ln_g"][i], G["sgu_ln_b"][i] = dlg.reshape(-1), dlb.reshape(-1)
    G["sgu_w"][i], G["sgu_b"][i] = dsw, dsb[..., 0]
    (dt12, dgd), _ = _rw_bwd("glu_gate_bwd", _f_glu_gate, [_cols(sv["t12"], nsplit=2), _cols(z, 3584, 512)], [], [_cols(dcd, 1024, 512)],
                             [(0, BF16), (1, BF16)], [], 256)
    dyg = _mm("mm_glu_dx", dt12, sv["w12"], tb=True)
    dw12 = _mm("mm_glu_dw", sv["yg"], dt12, ta=True, out_dtype=BF16)
    G["glu_w1"][i], G["glu_w2"][i] = dw12[:, :512], dw12[:, 512:]
    (dyc, duf1), (dd,) = _rw_bwd("gelu_bwd", _f_gelu_y, [_cols(sv["yc"]), _cols(z, 3072, 512)], [_par(sv["dpar"])], [_cols(dyg)],
                                 [(0, BF16), (1, F32)], [0], 256)
    G["s5_d"][i] = dd.reshape(-1)
    dh2 = _s5_y_dx(dyc, sv["cf"])
    dcf = _s5_y_dw(sv["h2"], dyc)
    _, cvjp = jax.vjp(_cdense, W["s5_c_re"][i], W["s5_c_im"][i])
    G["s5_c_re"][i], G["s5_c_im"][i] = cvjp(dcf)
    dbu3, da2 = _scan_bwd(dh2.reshape(SEQ, 32, 128), sv["h3"], sv["a2"])
    dbu = dbu3.reshape(SEQ, 4096)
    duf2 = _s5_bu_dx(dbu, sv["bd"])
    dbd = _s5_bu_dw(z, dbu)
    _, bvjp = jax.vjp(_bdense, *sv["bb"])
    dbb_re, dbb_im = bvjp(dbd)
    (dbr, dbi, dcr, dci), _ = _rw_bwd("s5_bbar_bwd", _f_bbar, sv["bb_rows"], [], [_cols(dbb_re), _cols(dbb_im)],
                                      [(0, F32), (1, F32), (2, F32), (3, F32)], [], 256)
    G["s5_b_re"][i], G["s5_b_im"][i] = dbr.reshape(S5_GROUPS, S5_STATE, S5_GROUP_DIM), dbi.reshape(S5_GROUPS, S5_STATE, S5_GROUP_DIM)
    douts = [_cols(da2[0:16].reshape(S5_GROUPS, S5_STATE)), _cols(da2[16:32].reshape(S5_GROUPS, S5_STATE)),
             _cols(dcr.reshape(S5_GROUPS, S5_STATE)), _cols(dci.reshape(S5_GROUPS, S5_STATE))]
    (dar, dai, dldt), _ = _rw_bwd("s5_prep_bwd", _f_s5_prep, sv["prep_rows"], [], douts, [(0, F32), (1, F32), (2, F32)], [], S5_GROUPS)
    G["s5_a_re"][i], G["s5_a_im"][i], G["s5_log_dt"][i] = dar, dai, dldt.reshape(-1)
    dxd = (duf1 + duf2).astype(BF16)
    dz = jnp.concatenate([du, dv, dgc, dxd, dgd], axis=1)
    dhn = _mm("mm_in_cd_dx", dz, W["w_in_cd"][i], tb=True)
    G["w_in_cd"][i] = _mm("mm_in_cd_dw", hn, dz, ta=True, out_dtype=BF16, out_stack=W["w_in_cd"][i].shape[2])
    g = _row2(W["norm_cd"][i]) + send(0)
    (dx,), (dg,) = _rw_bwd("rms_bwd", _f_rms_res, [_cols(x)], [_par(g)], [_cols(dx1), _cols(dhn)], [(0, F32)], [0], _NORM_ROWS)
    G["norm_cd"][i] = dg.reshape(-1)
    return dx


def _cross_fwd(x1, mem_n, W, l, zero):
    hx = _rw_fwd("rms_fwd", _f_rms, [_cols(x1)], [_par(_row2(W["norm_x"][l]) + zero)], [(D_MODEL, BF16, 1)], _NORM_ROWS)[0]
    qx = _mm("mm_xq", hx, W["w_xq"][l], out_dtype=BF16)
    kv = _mm("mm_xkv", mem_n, W["w_xkv"][l], out_dtype=BF16)
    ox, lse = _attn_fwd("attn_cross_fwd", qx, kv, kv, out_dtype=BF16, **_CROSS)
    x2 = _mm("mm_xo", ox, W["w_xo"][l], res=x1)
    return x2, dict(x1=x1, hx=hx, qx=qx, kv=kv, ox=ox, lse=lse)


def _cross_bwd(dx2, dmem_n, sv, mem_n, W, G, l, zero):
    dox = _mm("mm_xo_dx", dx2, W["w_xo"][l], tb=True, out_dtype=BF16)
    G["w_xo"][l] = _mm("mm_xo_dw", sv["ox"], dx2, ta=True, out_dtype=BF16)
    dqx, dk, dv = _attn_bwd("attn_cross_bwd", sv["qx"], sv["kv"], sv["kv"], dox, sv["ox"], sv["lse"], out_dtype=BF16, **_CROSS)
    dkv = jnp.concatenate([dk, dv], axis=1)
    dhx = _mm("mm_xq_dx", dqx, W["w_xq"][l], tb=True)
    G["w_xq"][l] = _mm("mm_xq_dw", sv["hx"], dqx, ta=True, out_dtype=BF16)
    dmem_n = _mm("mm_xkv_dx", dkv, W["w_xkv"][l], tb=True, res=dmem_n)
    G["w_xkv"][l] = _mm("mm_xkv_dw", mem_n, dkv, ta=True, out_dtype=BF16, out_stack=W["w_xkv"][l].shape[2])
    (dx1,), (dg,) = _rw_bwd("rms_bwd", _f_rms_res, [_cols(sv["x1"])], [_par(_row2(W["norm_x"][l]) + zero)], [_cols(dx2), _cols(dhx)], [(0, F32)], [0], _NORM_ROWS)
    G["norm_x"][l] = dg.reshape(-1)
    return dx1, dmem_n


_PER_LAYER = ("pool_scale", "norm_ab", "norm_cd", "sgu_ln_g", "sgu_ln_b", "sgu_w", "sgu_b", "s5_d", "s5_c_re", "s5_c_im", "s5_b_re", "s5_b_im",
              "s5_a_re", "s5_a_im", "s5_log_dt", "w_in_ab", "pool_w", "w_out_ab", "w_in_cd", "glu_w1", "glu_w2", "w_out_cd")


def _local_step(x, mem, target, W, weights_of, start_ahead, send_grads, after_layer):
    G = {k: [None, None] for k in _PER_LAYER}
    for k in ("norm_x", "w_xq", "w_xkv", "w_xo"):
        G[k] = [None] * DEPTH
    mem_rows = [_cols(mem)]
    mem_par = [_par(_row2(W["mem_norm"]))]
    mem_n = _rw_fwd("rms_fwd_mem", _f_rms, mem_rows, mem_par, [(D_MODEL, BF16, 1)], MEM_LEN)[0]
    saved = []
    for layer in range(DEPTH):
        zero = weights_of(layer, 0, x if layer else mem_n)
        mixer = _even_fwd if layer % 2 == 0 else _odd_fwd
        x, sv, zero = mixer(x, W, layer // 2, zero, functools.partial(weights_of, layer, 1), functools.partial(start_ahead, layer))
        x, svx = _cross_fwd(x, mem_n, W, layer, zero)
        saved.append((sv, svx))
    loss, dx, dfinal = _loss_head(x, target, _row2(W["final_norm"]))
    G["final_norm"] = dfinal.reshape(-1)
    dmem_n, zero = None, 0.0
    for layer in reversed(range(DEPTH)):
        sv, svx = saved[layer]
        dx, dmem_n = _cross_bwd(dx, dmem_n, svx, mem_n, W, G, layer, zero)
        hook = functools.partial(send_grads, layer, G)
        dx = _even_bwd(dx, sv, W, G, layer // 2, hook) if layer % 2 == 0 else _odd_bwd(dx, sv, W, G, layer // 2, hook)
        zero = after_layer(layer, G)
    _, (dmn,) = _rw_bwd("rms_bwd_mem", _f_rms, mem_rows, mem_par, [_cols(dmem_n)], [], [0], 256)
    G["mem_norm"] = dmn.reshape(-1)
    return loss, dx, G


_HBM = pl.BlockSpec(memory_space=pltpu.HBM)
_ANY = pl.BlockSpec(memory_space=pl.ANY)
_SEM = pl.BlockSpec(memory_space=pltpu.SEMAPHORE)
_N_PEERS = N_DEV - 1


def _mesh_pos():
    return lax.axis_index("x"), lax.axis_index("y"), lax.axis_index("c")


def _peer(pos, k):
    x, y, c = pos
    return (x ^ ((k >> 2) & 1), y ^ ((k >> 1) & 1), c ^ (k & 1))


def _lin(pos):
    return 4 * pos[0] + 2 * pos[1] + pos[2]


def _ends(gather, srcs, lands, t, sender, receiver):
    if gather:
        return lands[t].at[sender], lands[t].at[sender]
    whole = len(srcs[t].shape) != len(lands[t].shape)
    return (srcs[t] if whole else srcs[t].at[receiver]), lands[t].at[sender]


def _into_slot(name, b2, r0, r, me, dtype, after):
    c = b2.shape[1]
    tr = _row_block(r, c, 2 << 20)
    assert r0 % tr == 0

    def body(me_ref, x_ref, *rest):
        rest[-1][...] = x_ref[...].astype(dtype)

    extra = [] if after is None else [after]
    grid_spec = pltpu.PrefetchScalarGridSpec(
        num_scalar_prefetch=1, grid=(r // tr,),
        in_specs=[pl.BlockSpec((tr, c), lambda i, me, o=r0 // tr: (o + i, 0))] + [_ANY] * len(extra),
        out_specs=pl.BlockSpec((None, tr, c), lambda i, me: (me[0], i, 0)))
    return pl.pallas_call(
        body, name=name, grid_spec=grid_spec, out_shape=jax.ShapeDtypeStruct((N_DEV, r, c), dtype),
        compiler_params=pltpu.CompilerParams(dimension_semantics=("arbitrary",), vmem_limit_bytes=V7X_VMEM_LIMIT_BYTES),
        interpret=False,
    )(me, b2, *extra)


def _exchange_start(name, gather, srcs, lands, after=None):
    ns, nt = len(srcs), len(lands)
    arrs = list(srcs) + list(lands)
    extra = [] if after is None else [after]

    def body(*refs):
        ins, lnd = refs[:ns], refs[ns:ns + nt]
        refs = refs[len(extra):]
        send_sems, recv_sems = refs[ns + nt], refs[ns + nt + 1]
        token = refs[-1]
        pos = _mesh_pos()
        me = _lin(pos)
        for k in range(1, N_DEV):
            peer = _peer(pos, k)
            for t in range(nt):
                src, dst = _ends(gather, ins, lnd, t, me, _lin(peer))
                pltpu.make_async_remote_copy(
                    src_ref=src, dst_ref=dst, send_sem=send_sems.at[t * _N_PEERS + k - 1], recv_sem=recv_sems.at[t * _N_PEERS + k - 1],
                    device_id=peer, device_id_type=pl.DeviceIdType.MESH).start()
        token[...] = jnp.zeros_like(token)

    out = pl.pallas_call(
        body, name=name,
        out_shape=(pltpu.SemaphoreType.DMA((nt * _N_PEERS,)), pltpu.SemaphoreType.DMA((nt * _N_PEERS,)), *[pltpu.HBM(a.shape, a.dtype) for a in arrs],
                   jax.ShapeDtypeStruct((8, 128), F32)),
        in_specs=[_HBM] * (ns + nt) + [_ANY] * len(extra), out_specs=(_SEM, _SEM, *[_HBM] * (ns + nt), pl.BlockSpec(memory_space=pltpu.VMEM)),
        input_output_aliases={j: 2 + j for j in range(ns + nt)},
        compiler_params=pltpu.CompilerParams(has_side_effects=pltpu.SideEffectType.DATAFLOW_SIDE_EFFECTING),
        interpret=False,
    )(*[pltpu.with_memory_space_constraint(a, pltpu.HBM) for a in arrs], *extra)
    return dict(send=out[0], recv=out[1], srcs=list(out[2:2 + ns]), lands=list(out[2 + ns:2 + ns + nt]), token=out[-1][0, 0], token_arr=out[-1], gather=gather)


def _exchange_wait(name, ex, after):
    ns, nt = len(ex["srcs"]), len(ex["lands"])
    gather = ex["gather"]
    arrs = ex["srcs"] + ex["lands"]
    after = list(after) if isinstance(after, (list, tuple)) else [after]

    def body(*refs):
        ins, lnd = refs[:ns], refs[ns:ns + nt]
        send_sems, recv_sems = refs[ns + nt], refs[ns + nt + 1]
        pos = _mesh_pos()
        me = _lin(pos)
        for k in range(1, N_DEV):
            peer = _peer(pos, k)
            for t in range(nt):
                src, _ = _ends(gather, ins, lnd, t, me, _lin(peer))
                _, dst = _ends(gather, ins, lnd, t, _lin(peer), me)
                cp = pltpu.make_async_remote_copy(
                    src_ref=src, dst_ref=dst, send_sem=send_sems.at[t * _N_PEERS + k - 1], recv_sem=recv_sems.at[t * _N_PEERS + k - 1],
                    device_id=peer, device_id_type=pl.DeviceIdType.MESH)
                cp.wait_send()
                cp.wait_recv()

    out = pl.pallas_call(
        body, name=name, out_shape=tuple(pltpu.HBM(a.shape, a.dtype) for a in arrs),
        in_specs=[_HBM] * (ns + nt) + [_SEM, _SEM] + [_ANY] * len(after), out_specs=tuple([_HBM] * (ns + nt)),
        input_output_aliases={j: j for j in range(ns + nt)},
        compiler_params=pltpu.CompilerParams(has_side_effects=pltpu.SideEffectType.DATAFLOW_SIDE_EFFECTING),
        interpret=False,
    )(*arrs, ex["send"], ex["recv"], *after)
    return list(out[:ns]), list(out[ns:])


def _scatter_begin(name, srcs):
    lands = [lax.empty(s.shape if s.ndim == 3 else (N_DEV,) + s.shape, s.dtype) for s in srcs]
    return _exchange_start(name, False, srcs, lands)


def _adam(name, w, m, v, parts, own, me, layer, bufs):
    r, c = parts.shape[1:]
    tr = _row_block(r, max(c, 128), 2 << 20)
    nb = r // tr

    def body(me_ref, w_ref, m_ref, v_ref, p_ref, own_ref, *rest):
        g_ref, d_ref, nm_ref, nv_ref, acc = rest[-5:]
        acc[...] = jnp.zeros_like(acc)
        for k in range(N_DEV):
            @pl.when(me_ref[0] == k)
            def _():
                acc[...] += own_ref[...].astype(F32)

            @pl.when(me_ref[0] != k)
            def _(k=k):
                acc[...] += p_ref[k].astype(F32)

        g = acc[...]
        mm = ADAM_B1 * m_ref[...] + (1.0 - ADAM_B1) * g
        vv = ADAM_B2 * v_ref[...] + (1.0 - ADAM_B2) * jnp.square(g)
        m_hat = mm / (1.0 - ADAM_B1 ** ADAM_STEP)
        v_hat = vv / (1.0 - ADAM_B2 ** ADAM_STEP)
        g_ref[...] = g
        d_ref[...] = -ADAM_LR * (m_hat / (jnp.sqrt(v_hat) + ADAM_EPS) + ADAM_WD * w_ref[...])
        nm_ref[...] = mm
        nv_ref[...] = vv

    blk = pl.BlockSpec((tr, c), lambda i, me, o=layer * nb: (o + i, 0))
    own_spec = pl.BlockSpec((None, tr, c), lambda i, me: (me[0], i, 0)) if own.ndim == 3 else pl.BlockSpec((tr, c), lambda i, me: (i, 0))
    in_specs = [blk, blk, blk, pl.BlockSpec((N_DEV, tr, c), lambda i, me: (0, i, 0)), own_spec]
    args = [me, w, m, v, parts, own]
    aliases = {}
    if bufs is not None:
        in_specs += [_ANY] * 4
        aliases = {len(args) + j: j for j in range(4)}
        args += list(bufs)
    grid_spec = pltpu.PrefetchScalarGridSpec(
        num_scalar_prefetch=1, grid=(nb,), in_specs=in_specs, out_specs=[blk] * 4, scratch_shapes=[pltpu.VMEM((tr, c), F32)])
    return pl.pallas_call(
        body, name=name, grid_spec=grid_spec, out_shape=[jax.ShapeDtypeStruct(w.shape, F32)] * 4, input_output_aliases=aliases,
        compiler_params=pltpu.CompilerParams(dimension_semantics=("arbitrary",), vmem_limit_bytes=V7X_VMEM_LIMIT_BYTES),
        interpret=False,
    )(*args)


def _row_block(r, c, limit):
    best = None
    for tr in range(16, r + 1, 16):
        if r % tr == 0 and tr * c * 4 <= limit:
            best = tr
    return r if best is None else best


_BIG = ("w_in_ab", "pool_w", "w_out_ab", "w_in_cd", "glu_w1", "glu_w2", "w_out_cd", "w_xq", "w_xkv", "w_xo")
_STACKED = ("w_in_ab", "w_in_cd", "w_xkv")
_MIXER_BIG = (("w_in_ab", "pool_w", "w_out_ab"), ("w_in_cd", "glu_w1", "glu_w2", "w_out_cd"))
_CROSS_BIG = ("w_xq", "w_xkv", "w_xo")
_SMALL_SPLIT = ["norm_cd", "sgu_ln_g", "sgu_ln_b", "s5_d"]
_REPLICATED_ODD = ["sgu_w", "sgu_b", "s5_a_re", "s5_a_im", "s5_log_dt", "s5_b_re", "s5_b_im", "s5_c_re", "s5_c_im", "final_norm"]
_REPLICATED_EVEN = ["norm_ab", "pool_scale", "norm_x", "mem_norm"]
_REPLICATED = _REPLICATED_ODD + _REPLICATED_EVEN
_WEIGHTS = ["norm_ab", "w_in_ab", "pool_w", "pool_scale", "w_out_ab", "norm_cd", "w_in_cd", "sgu_ln_g", "sgu_ln_b", "sgu_w", "sgu_b", "s5_a_re",
            "s5_a_im", "s5_log_dt", "s5_b_re", "s5_b_im", "s5_c_re", "s5_c_im", "s5_d", "glu_w1", "glu_w2", "w_out_cd", "norm_x", "w_xq",
            "w_xkv", "w_xo", "mem_norm", "final_norm"]


def _layer_big(layer):
    return [(n, layer // 2) for n in _MIXER_BIG[layer % 2]] + [(n, layer) for n in _CROSS_BIG]


def _gather_parts(layer):
    big = _layer_big(layer)
    return [big[:1], big[1:]]


def _scatter_parts(layer):
    big = _layer_big(layer)
    return [big[2:], big[:2]] if layer % 2 == 0 else [big]


def _from_slots(name, a):
    if name in _STACKED:
        return a
    if name == "pool_w":
        return a.reshape(N_DEV, 4, 32, 256).transpose(1, 0, 2, 3).reshape(4, 256, 256)
    return a.reshape(-1, a.shape[-1])


def _to_slots(name, g):
    if name in _STACKED:
        return g
    if name == "pool_w":
        return g.reshape(4, N_DEV, 32, 256).transpose(1, 0, 2, 3).reshape(N_DEV, 128, 256)
    return g.reshape(N_DEV, -1, g.shape[-1])


def _rows2d(a):
    return a.reshape(-1, a.shape[-1])


def _small_rows(block):
    return jnp.pad(block, ((0, 0), (0, 128 - block.shape[1])))


def kernel(x, mem, norm_ab, w_in_ab, pool_w, pool_scale, w_out_ab, norm_cd, w_in_cd, sgu_ln_g, sgu_ln_b, sgu_w, sgu_b, s5_a_re, s5_a_im, s5_log_dt, s5_b_re, s5_b_im, s5_c_re, s5_c_im, s5_d, glu_w1, glu_w2, w_out_cd, norm_x, w_xq, w_xkv, w_xo, mem_norm, final_norm, loss_target, m_norm_ab, m_w_in_ab, m_pool_w, m_pool_scale, m_w_out_ab, m_norm_cd, m_w_in_cd, m_sgu_ln_g, m_sgu_ln_b, m_sgu_w, m_sgu_b, m_s5_a_re, m_s5_a_im, m_s5_log_dt, m_s5_b_re, m_s5_b_im, m_s5_c_re, m_s5_c_im, m_s5_d, m_glu_w1, m_glu_w2, m_w_out_cd, m_norm_x, m_w_xq, m_w_xkv, m_w_xo, m_mem_norm, m_final_norm, v_norm_ab, v_w_in_ab, v_pool_w, v_pool_scale, v_w_out_ab, v_norm_cd, v_w_in_cd, v_sgu_ln_g, v_sgu_ln_b, v_sgu_w, v_sgu_b, v_s5_a_re, v_s5_a_im, v_s5_log_dt, v_s5_b_re, v_s5_b_im, v_s5_c_re, v_s5_c_im, v_s5_d, v_glu_w1, v_glu_w2, v_w_out_cd, v_norm_x, v_w_xq, v_w_xkv, v_w_xo, v_mem_norm, v_final_norm):
    args = locals()
    w = {n: args[n] for n in _WEIGHTS}
    m = {n: args["m_" + n] for n in _WEIGHTS}
    v = {n: args["v_" + n] for n in _WEIGHTS}

    me = jnp.reshape(_lin(_mesh_pos()), (1,)).astype(jnp.int32)

    order = [(layer, p) for layer in range(DEPTH) for p in range(len(_gather_parts(layer)))]
    lands, gathers = {}, {}
    for key in order:
        lands[key] = []
        for name, i in _gather_parts(key[0])[key[1]]:
            b2 = _rows2d(w[name])
            r = b2.shape[0] // w[name].shape[0]
            lands[key].append(_into_slot("cast_" + name, b2, i * r, r, me, BF16, None))
    small_blocks = jnp.concatenate([_small_rows(w[n]) for n in _SMALL_SPLIT], axis=0)
    lands[order[0]].append(_into_slot("cast_small", small_blocks, 0, 8, me, F32, None))

    def begin_gather(key, after):
        gathers[key] = _exchange_start("gather%d%s_start" % (key[0], "ab"[key[1]]), True, [], lands[key], after)
        return gathers[key]["token"]

    W = {n: w[n] for n in _REPLICATED}
    W["mem_norm"] = w["mem_norm"] + begin_gather(order[0], None)
    for name in _BIG:
        W[name] = [None] * w[name].shape[0]

    def weights_of(layer, part, after):
        key = (layer, part)
        if key not in gathers:
            return 0.0
        if key == order[0]:
            after = [after] + [a for k in order[1:] for a in lands[k]]
        _, got = _exchange_wait("gather%d%s_wait" % (layer, "ab"[part]), gathers[key], after)
        for (name, i), arr in zip(_gather_parts(layer)[part], got):
            W[name][i] = _from_slots(name, arr)
        if key == order[0]:
            sm = got[-1].reshape(N_DEV, 4, 2, 128)
            for j, n in enumerate(_SMALL_SPLIT):
                width = w[n].shape[1]
                W[n] = sm[:, j, :, :width].transpose(1, 0, 2).reshape(2, N_DEV * width)
        nxt = order.index(key) + 1
        return begin_gather(order[nxt], got[0]) if nxt < len(order) and order[nxt] not in gathers else 0.0

    def start_ahead(layer, after):
        return begin_gather((layer + 1, 0), after) if layer + 1 < DEPTH else 0.0

    scatters, small = {}, {}

    def send_grads(layer, G, part):
        srcs = [_to_slots(name, G[name][i]) for name, i in _scatter_parts(layer)[part]]
        scatters[layer, part] = _scatter_begin("scatter%d%s_start" % (layer, "ab"[part]), srcs)
        return scatters[layer, part]["token"]

    def rows(a):
        if a.ndim == 1:
            return a.reshape(1, -1)
        return a.reshape(-1, 128) if a.shape[-1] < 128 and a.size % 128 == 0 else a.reshape(-1, a.shape[-1])

    def begin_small(tag, G, names, split):
        srcs = [rows(G[n] if n in ("mem_norm", "final_norm") else jnp.stack(G[n])) for n in names]
        if split:
            srcs += [jnp.stack(G[n]).reshape(2, N_DEV, -1).transpose(1, 0, 2) for n in _SMALL_SPLIT]
        small[tag] = _scatter_begin("scatter_small_%s_start" % tag, srcs)
        return small[tag]["token"]

    def after_layer(layer, G):
        return begin_small("odd", G, _REPLICATED_ODD, True) if layer == 1 else 0.0

    loss, dx, G = _local_step(x[0], mem[0], loss_target[0], W, weights_of, start_ahead, send_grads, after_layer)
    loss = lax.psum(loss[0, 0], MESH_AXES)
    begin_small("even", G, _REPLICATED_EVEN, False)

    out = {}
    after = small["even"]["token_arr"]
    for layer, part in scatters:
        own, got = _exchange_wait("scatter%d%s_wait" % (layer, "ab"[part]), scatters[layer, part], after)
        for (name, i), mine, parts in zip(_scatter_parts(layer)[part], own, got):
            out[name] = _adam("adam_" + name, _rows2d(w[name]), _rows2d(m[name]), _rows2d(v[name]), parts, mine, me, i, out.get(name))
        after = [out[name][0] for name, i in _scatter_parts(layer)[part]]
    for name in _BIG:
        out[name] = [a.reshape(w[name].shape) for a in out[name]]
    for tag, names in (("odd", _REPLICATED_ODD + _SMALL_SPLIT), ("even", _REPLICATED_EVEN)):
        own, got = _exchange_wait("scatter_small_%s_wait" % tag, small[tag], after)
        for n, mine, parts in zip(names, own, got):
            as2d = (lambda a: a) if n in _SMALL_SPLIT else rows
            res = _adam("adam_" + n, as2d(w[n]), as2d(m[n]), as2d(v[n]), parts, mine, me, 0, None)
            out[n] = [a.reshape(w[n].shape) for a in res]
        after = [out[n][0] for n in names]

    return (loss, dx[None], *[out[n][0] for n in _WEIGHTS], *[out[n][1] for n in _WEIGHTS], *[out[n][2] for n in _WEIGHTS],
            *[out[n][3] for n in _WEIGHTS])
```

```python
import functools
import math

import jax
import jax.numpy as jnp
from jax import lax
from jax.experimental import pallas as pl
from jax.experimental.pallas import tpu as pltpu

F32 = jnp.float32
BF16 = jnp.bfloat16

SEQ = 2048
D_MODEL = 1024
MEM_LEN = 256
DEPTH = 4
N_DEV = 8
EPS = 1e-6
NEG = -1e30
A_HEAD_DIM = 64
X_HEAD_DIM = 256
S5_GROUPS = 32
S5_STATE = 64
S5_GROUP_DIM = 16

ADAM_LR = 0.001
ADAM_B1 = 0.9
ADAM_B2 = 0.999
ADAM_EPS = 1e-08
ADAM_WD = 0.01
ADAM_STEP = 10

V7X_VMEM_LIMIT_BYTES = 56 * 1024 * 1024
_MM_VMEM_BYTES = 36 * 1024 * 1024
MESH_AXES = ("x", "y", "c")


def _pc(body, *, name, out_shape, grid=None, in_specs=None, out_specs=None, scratch_shapes=(), aliases=None, sem=None):
    kw = {}
    if grid is not None:
        kw["grid"] = grid
    if in_specs is not None:
        kw["in_specs"] = in_specs
    if out_specs is not None:
        kw["out_specs"] = out_specs
    if aliases:
        kw["input_output_aliases"] = aliases
    return pl.pallas_call(
        body,
        name=name,
        out_shape=out_shape,
        scratch_shapes=list(scratch_shapes),
        compiler_params=pltpu.CompilerParams(dimension_semantics=sem, vmem_limit_bytes=V7X_VMEM_LIMIT_BYTES),
        interpret=False,
        **kw,
    )


def _cols(arr, c0=0, width=None, nsplit=1, r0=0):
    width = arr.shape[1] - c0 if width is None else width
    assert c0 % width == 0 and width % nsplit == 0
    return (arr, c0, width, nsplit, r0)


def _par(arr, nsplit=1):
    return (arr, nsplit)


def _ld(ref, nsplit):
    if nsplit == 1:
        return ref[...].astype(F32)
    if len(ref.shape) == 3:
        return tuple(ref[k].astype(F32) for k in range(nsplit))
    w = ref.shape[-1] // nsplit
    return tuple(ref[:, k * w:(k + 1) * w].astype(F32) for k in range(nsplit))


def _st(ref, val, nsplit, accumulate=False):
    if nsplit == 1:
        val = (val,)
    for k in range(nsplit):
        if nsplit == 1:
            idx = (Ellipsis,)
        elif len(ref.shape) == 3:
            idx = (k,)
        else:
            w = ref.shape[-1] // nsplit
            idx = (slice(None), slice(k * w, (k + 1) * w))
        if accumulate:
            ref[idx] += val[k].astype(ref.dtype)
        else:
            ref[idx] = val[k].astype(ref.dtype)


def _row_spec(tr, op):
    _, c0, w, _, r0 = op
    assert r0 % tr == 0
    return pl.BlockSpec((tr, w), lambda i, cb=c0 // w, rb=r0 // tr: (i + rb, cb))


def _full_spec(arr):
    return pl.BlockSpec(arr.shape, lambda i, nd=arr.ndim: (0,) * nd)


def _rw_fwd(name, f, rows, pars, outs, tr, n_rows=None):
    n_rows = rows[0][0].shape[0] if n_rows is None else n_rows
    nr, npar = len(rows), len(pars)

    def body(*refs):
        r = [_ld(refs[i], rows[i][3]) for i in range(nr)]
        p = [_ld(refs[nr + i], pars[i][1]) for i in range(npar)]
        res = f(r, p)
        for k, (_, _, ns) in enumerate(outs):
            _st(refs[nr + npar + k], res[k], ns)

    res = _pc(
        body, name=name, grid=(n_rows // tr,),
        in_specs=[_row_spec(tr, op) for op in rows] + [_full_spec(a) for a, _ in pars],
        out_specs=[pl.BlockSpec((tr, w), lambda i: (i, 0)) for w, _, _ in outs],
        out_shape=[jax.ShapeDtypeStruct((n_rows, w), dt) for w, dt, _ in outs],
        sem=("arbitrary",),
    )(*[op[0] for op in rows], *[a for a, _ in pars])
    return list(res)


def _rw_bwd(name, f, rows, pars, douts, drow, dpar, tr):
    n_rows = rows[0][0].shape[0]
    nr, npar = len(rows), len(pars)
    dgiven = [d for d in douts if d is not None]
    nd = len(dgiven)

    def body(*refs):
        r = [_ld(refs[i], rows[i][3]) for i in range(nr)]
        p = [_ld(refs[nr + i], pars[i][1]) for i in range(npar)]
        d = [_ld(refs[nr + npar + i], dgiven[i][3]) for i in range(nd)]
        orefs = refs[nr + npar + nd:]

        def g(dr, dp):
            rr, pp = list(r), list(p)
            for j, (idx, _) in enumerate(drow):
                rr[idx] = dr[j]
            for j, idx in enumerate(dpar):
                pp[idx] = dp[j]
            return tuple(f(rr, pp))

        out, vjp = jax.vjp(g, [r[idx] for idx, _ in drow], [p[idx] for idx in dpar])
        ct, j = [], 0
        for k, o in enumerate(out):
            if douts[k] is None:
                ct.append(jax.tree.map(jnp.zeros_like, o))
            else:
                ct.append(d[j])
                j += 1
        gdr, gdp = vjp(tuple(ct))
        for j, (idx, _) in enumerate(drow):
            _st(orefs[j], gdr[j], rows[idx][3])

        @pl.when(pl.program_id(0) == 0)
        def _():
            for j in range(len(dpar)):
                orefs[len(drow) + j][...] = jnp.zeros_like(orefs[len(drow) + j])

        for j, idx in enumerate(dpar):
            _st(orefs[len(drow) + j], gdp[j], pars[idx][1], accumulate=True)

    res = _pc(
        body, name=name, grid=(n_rows // tr,),
        in_specs=[_row_spec(tr, op) for op in rows] + [_full_spec(a) for a, _ in pars] + [_row_spec(tr, op) for op in dgiven],
        out_specs=[pl.BlockSpec((tr, rows[idx][2]), lambda i: (i, 0)) for idx, _ in drow] + [_full_spec(pars[idx][0]) for idx in dpar],
        out_shape=[jax.ShapeDtypeStruct((n_rows, rows[idx][2]), dt) for idx, dt in drow]
        + [jax.ShapeDtypeStruct(pars[idx][0].shape, F32) for idx in dpar],
        sem=("arbitrary",),
    )(*[op[0] for op in rows], *[a for a, _ in pars], *[op[0] for op in dgiven])
    res = list(res)
    return res[:len(drow)], res[len(drow):]


_NORM_ROWS = 512


def _sigmoid(x):
    return jax.nn.sigmoid(x)


def _silu(x):
    return x * _sigmoid(x)


def _rms(x, g):
    return x * lax.rsqrt(jnp.mean(x * x, axis=-1, keepdims=True) + EPS) * g


def _f_rms(r, p):
    return [_rms(r[0], p[0])]


def _f_rms_res(r, p):
    return [r[0], _rms(r[0], p[0])]


def _f_gate_ab(r, p):
    o, ga, mixed, gb = r
    return [(o * _silu(ga), mixed * p[0] * _silu(gb))]


def _f_sgu(r, p):
    u, v, gc = r
    lg, lb, w, b = p
    n = float(D_MODEL)
    mu = sum(jnp.sum(vk, axis=-1, keepdims=True) for vk in v) / n
    var = sum(jnp.sum(jnp.square(vk - mu), axis=-1, keepdims=True) for vk in v) / n
    rs = lax.rsqrt(var + EPS)
    t = w[0].shape[0]
    tri = lax.broadcasted_iota(jnp.int32, (t, t), 0) >= lax.broadcasted_iota(jnp.int32, (t, t), 1)
    outs = []
    for k in range(len(v)):
        vn = (v[k] - mu) * rs * lg[k] + lb[k]
        mixed = jnp.dot(jnp.where(tri, w[k], 0.0).astype(BF16), vn.astype(BF16), preferred_element_type=F32) + b[k]
        outs.append(u[k] * mixed * _silu(gc[k]))
    return [tuple(outs)]


def _gelu(x):
    return 0.5 * x * (1.0 + jnp.tanh(math.sqrt(2.0 / math.pi) * (x + 0.044715 * (x * x * x))))


def _f_gelu_y(r, p):
    yc, uf = r
    return [_gelu(yc + p[0] * uf)]


def _f_glu_gate(r, p):
    t12, gd = r
    return [t12[0] * _sigmoid(t12[1]) * _silu(gd)]


def _f_s5_prep(r, p):
    ar, ai, ldt = r
    dt = jnp.exp(ldt)
    mag = jnp.exp(dt * ar)
    abar_re = mag * jnp.cos(dt * ai)
    abar_im = mag * jnp.sin(dt * ai)
    nr, ni = abar_re - 1.0, abar_im
    inv = 1.0 / (ar * ar + ai * ai)
    return [abar_re, abar_im, (nr * ar + ni * ai) * inv, (ni * ar - nr * ai) * inv]


def _f_bbar(r, p):
    br, bi, cr, ci = r
    return [cr * br - ci * bi, cr * bi + ci * br]


def _loss_head(x, target, g):
    tr = 256
    n_rows, width = x.shape

    def f(xv, gv, tv):
        err = jnp.square(_rms(xv, gv) - tv)
        return 0.5 * jnp.mean(err, axis=-1, keepdims=True)

    def body(x_ref, t_ref, g_ref, loss_ref, dx_ref, dg_ref):
        @pl.when(pl.program_id(0) == 0)
        def _():
            loss_ref[...] = jnp.zeros_like(loss_ref)
            dg_ref[...] = jnp.zeros_like(dg_ref)

        tv = t_ref[...]
        row_loss, vjp = jax.vjp(lambda a, b: f(a, b, tv), x_ref[...], g_ref[...])
        dx, dg = vjp(jnp.ones_like(row_loss))
        dx_ref[...] = dx
        dg_ref[...] += dg
        loss_ref[...] += jnp.broadcast_to(jnp.sum(row_loss, axis=0, keepdims=True), loss_ref.shape)

    blk = pl.BlockSpec((tr, width), lambda i: (i, 0))
    one = pl.BlockSpec((1, width), lambda i: (0, 0))
    return _pc(
        body, name="loss_head", grid=(n_rows // tr,), in_specs=[blk, blk, one],
        out_specs=[pl.BlockSpec((1, 128), lambda i: (0, 0)), blk, one],
        out_shape=[jax.ShapeDtypeStruct((1, 128), F32), jax.ShapeDtypeStruct(x.shape, F32), jax.ShapeDtypeStruct((1, width), F32)],
        sem=("arbitrary",),
    )(x, target, g)


_NT = (((1,), (1,)), ((), ()))
_TN = (((0,), (0,)), ((), ()))


def _tile(n, cap):
    t = min(n, cap)
    while n % t:
        t -= 128
    assert t > 0
    return t


def _mm(name, a, b, *, ta=False, tb=False, out_dtype=F32, a_off=0, a_width=None, res=None, out_stack=None):
    assert not (ta and tb)
    stacked = b.ndim == 3
    bk, bn = (b.shape[1], b.shape[0] * b.shape[2]) if stacked else b.shape
    if ta:
        kc = a.shape[0]
        m = a.shape[1] - a_off if a_width is None else a_width
        n = bn
        assert bk == kc and not stacked
    else:
        m = a.shape[0]
        kc = a.shape[1] - a_off if a_width is None else a_width
        n = bk if tb else bn
        assert (bn if tb else bk) == kc
    tn = _tile(b.shape[2] if stacked and not tb else (out_stack or n), 1024)
    size = lambda dt: jnp.dtype(dt).itemsize
    for tk_cap, tm_cap in ((2048, 2048), (2048, 1024), (1024, 1024), (1024, 512), (1024, 256)):
        tm, tk = _tile(m, tm_cap), _tile(b.shape[2] if stacked and tb else kc, tk_cap)
        nk = kc // tk
        vmem = 2 * tm * tk * size(a.dtype) + 2 * tk * tn * size(b.dtype) + tm * tn * (2 * size(out_dtype) + (4 if nk > 1 else 0) + (8 if res is not None else 0))
        if vmem <= _MM_VMEM_BYTES:
            break
    if ta:
        assert a_off % tm == 0
        a_spec = pl.BlockSpec((tk, tm), lambda i, j, k, o=a_off // tm: (k, i + o))
        dims = _TN
    else:
        assert a_off % tk == 0
        a_spec = pl.BlockSpec((tm, tk), lambda i, j, k, o=a_off // tk: (i, k + o))
        dims = _NT if tb else (((1,), (0,)), ((), ()))
    if stacked and tb:
        b_spec = pl.BlockSpec((None, tn, tk), lambda i, j, k, q=b.shape[2] // tk: (k // q, j, k % q))
    elif stacked:
        b_spec = pl.BlockSpec((None, tk, tn), lambda i, j, k, q=b.shape[2] // tn: (j // q, k, j % q))
    else:
        b_spec = pl.BlockSpec((tn, tk), lambda i, j, k: (j, k)) if tb else pl.BlockSpec((tk, tn), lambda i, j, k: (k, j))
    if out_stack:
        out_spec = pl.BlockSpec((None, tm, tn), lambda i, j, k, q=out_stack // tn: (j // q, i, j % q))
        out_shape = jax.ShapeDtypeStruct((n // out_stack, m, out_stack), out_dtype)
    else:
        out_spec = pl.BlockSpec((tm, tn), lambda i, j, k: (i, j))
        out_shape = jax.ShapeDtypeStruct((m, n), out_dtype)
    in_specs, args = [a_spec, b_spec], [a, b]
    has_res = res is not None
    if has_res:
        in_specs.append(pl.BlockSpec((tm, tn), lambda i, j, k: (i, j)))
        args.append(res)

    def finish(refs, acc):
        if has_res:
            acc = acc + refs[2][...].astype(F32)
        refs[3 if has_res else 2][...] = acc.astype(out_dtype)

    def body_one(*refs):
        finish(refs, lax.dot_general(refs[0][...].astype(BF16), refs[1][...].astype(BF16), dims, preferred_element_type=F32))

    def body(*refs):
        acc_ref = refs[-1]
        k = pl.program_id(2)

        @pl.when(k == 0)
        def _():
            acc_ref[...] = jnp.zeros_like(acc_ref)

        acc_ref[...] += lax.dot_general(refs[0][...].astype(BF16), refs[1][...].astype(BF16), dims, preferred_element_type=F32)

        @pl.when(k == nk - 1)
        def _():
            finish(refs, acc_ref[...])

    return _pc(
        body_one if nk == 1 else body, name=name, grid=(m // tm, n // tn, nk), in_specs=in_specs, out_specs=out_spec, out_shape=out_shape,
        scratch_shapes=[] if nk == 1 else [pltpu.VMEM((tm, tn), F32)], sem=("parallel", "parallel", "arbitrary"),
    )(*args)


def _mm_blocks(name, a, b, *, grid, out_blk, a_blk, a_idx, b_blk, b_idx, dims, out_dtype=F32):
    gi, gj, nk = grid

    def body(a_ref, b_ref, o_ref, acc_ref):
        k = pl.program_id(2)

        @pl.when(k == 0)
        def _():
            acc_ref[...] = jnp.zeros_like(acc_ref)

        acc_ref[...] += lax.dot_general(a_ref[...].astype(BF16), b_ref[...].astype(BF16), dims, preferred_element_type=F32)

        @pl.when(k == nk - 1)
        def _():
            o_ref[...] = acc_ref[...].astype(o_ref.dtype)

    return _pc(
        body, name=name, grid=grid, in_specs=[pl.BlockSpec(a_blk, a_idx), pl.BlockSpec(b_blk, b_idx)],
        out_specs=pl.BlockSpec(out_blk, lambda i, j, k: (i, j)), out_shape=jax.ShapeDtypeStruct((gi * out_blk[0], gj * out_blk[1]), out_dtype),
        scratch_shapes=[pltpu.VMEM(out_blk, F32)], sem=("parallel", "parallel", "arbitrary"),
    )(a, b)


def _head_masks(width, nsub):
    lane = lax.broadcasted_iota(jnp.int32, (1, width), 1)
    hd = width // nsub
    return [(lane >= h * hd) & (lane < (h + 1) * hd) for h in range(nsub)]


def _dilated_log_count(row0, tq, ext):
    delta = (row0 + lax.broadcasted_iota(jnp.int32, (tq, ext), 0)) - lax.broadcasted_iota(jnp.int32, (tq, ext), 1)
    cnt = (delta <= 128).astype(jnp.int32) + (((delta & 3) == 0) & (delta <= 512)).astype(jnp.int32) + ((delta & 15) == 0).astype(jnp.int32)
    logc = jnp.where(cnt == 3, math.log(3.0), jnp.where(cnt == 2, math.log(2.0), 0.0))
    return jnp.where((delta >= 0) & (cnt > 0), logc, NEG)


def _bias_table(tab, nq, tq):
    @pl.when(pl.program_id(0) == 0)
    def _():
        for d in range(nq):
            tab[d] = _dilated_log_count(d * tq, tq, tq)


def _scores(q, ke, tab, r, masks, h):
    qm = (jnp.where(masks[h], q, 0.0) if len(masks) > 1 else q).astype(BF16)
    s = lax.dot_general(qm, ke, _NT, preferred_element_type=F32)
    if tab is not None:
        s = s + jnp.concatenate([tab[r - c] for c in range(r + 1)], axis=1)
    return qm, s


def _head_value(masks, h, tile, reduce, fill):
    return reduce(jnp.where(masks[h], tile, fill) if len(masks) > 1 else tile, axis=-1, keepdims=True)


def _attn_fwd(name, qa, ka, va, *, qc, kc, vc, width, nblk, nsub, causal, tq, scale, out_dtype):
    sq, t_len = qa.shape[0], ka.shape[0]
    nq = sq // tq

    def body(q_ref, k_ref, v_ref, o_ref, lse_ref, *scratch):
        tab = scratch[0] if causal else None
        if causal:
            _bias_table(tab, nq, tq)
        kb = k_ref[...].astype(BF16)
        vb = v_ref[...].astype(BF16)
        masks = _head_masks(width, nsub)
        for r in range(nq):
            ext = (r + 1) * tq if causal else t_len
            q = q_ref[r * tq:(r + 1) * tq, :].astype(F32) * scale
            ke, ve = kb[:ext], vb[:ext]
            o = lse = None
            for h in range(nsub):
                _, s = _scores(q, ke, tab, r, masks, h)
                m = jnp.max(s, axis=-1, keepdims=True)
                p = jnp.exp(s - m)
                l = jnp.sum(p, axis=-1, keepdims=True)
                oh = jnp.dot(p.astype(BF16), ve, preferred_element_type=F32) * (1.0 / l)
                lh = jnp.broadcast_to(m + jnp.log(l), (tq, width))
                o = oh if o is None else jnp.where(masks[h], oh, o)
                lse = lh if lse is None else jnp.where(masks[h], lh, lse)
            o_ref[r * tq:(r + 1) * tq, :] = o.astype(o_ref.dtype)
            lse_ref[r * tq:(r + 1) * tq, :] = lse

    blk = pl.BlockSpec((sq, width), lambda i: (0, i))
    return _pc(
        body, name=name, grid=(nblk,),
        in_specs=[pl.BlockSpec((sq, width), lambda i, c=qc: (0, c + i)), pl.BlockSpec((t_len, width), lambda i, c=kc: (0, c + i)),
                  pl.BlockSpec((t_len, width), lambda i, c=vc: (0, c + i))],
        out_specs=[blk, blk],
        out_shape=[jax.ShapeDtypeStruct((sq, nblk * width), out_dtype), jax.ShapeDtypeStruct((sq, nblk * width), F32)],
        scratch_shapes=[pltpu.VMEM((nq, tq, tq), F32)] if causal else [], sem=("arbitrary",),
    )(qa, ka, va)


def _attn_bwd(name, qa, ka, va, doa, oa, lsea, *, qc, kc, vc, width, nblk, nsub, causal, tq, scale, out_dtype):
    sq, t_len = qa.shape[0], ka.shape[0]
    nq = sq // tq

    def body(q_ref, k_ref, v_ref, do_ref, o_ref, lse_ref, dq_ref, dk_ref, dv_ref, dk_acc, dv_acc, *scratch):
        tab = scratch[0] if causal else None
        if causal:
            _bias_table(tab, nq, tq)
        kb = k_ref[...].astype(BF16)
        vb = v_ref[...].astype(BF16)
        masks = _head_masks(width, nsub)
        dk_acc[...] = jnp.zeros_like(dk_acc)
        dv_acc[...] = jnp.zeros_like(dv_acc)
        for r in range(nq):
            ext = (r + 1) * tq if causal else t_len
            q = q_ref[r * tq:(r + 1) * tq, :].astype(F32) * scale
            do = do_ref[r * tq:(r + 1) * tq, :].astype(F32)
            do_o = do * o_ref[r * tq:(r + 1) * tq, :].astype(F32)
            lse = lse_ref[r * tq:(r + 1) * tq, :]
            ke, ve = kb[:ext], vb[:ext]
            dq = None
            for h in range(nsub):
                qm, s = _scores(q, ke, tab, r, masks, h)
                dom = (jnp.where(masks[h], do, 0.0) if nsub > 1 else do).astype(BF16)
                pn = jnp.exp(s - _head_value(masks, h, lse, jnp.max, -jnp.inf))
                dpn = lax.dot_general(dom, ve, _NT, preferred_element_type=F32)
                dsb = (pn * (dpn - _head_value(masks, h, do_o, jnp.sum, 0.0))).astype(BF16)
                dqh = jnp.dot(dsb, ke, preferred_element_type=F32)
                dq = dqh if dq is None else jnp.where(masks[h], dqh, dq)
                dk_acc[0:ext, :] += lax.dot_general(dsb, qm, _TN, preferred_element_type=F32)
                dv_acc[0:ext, :] += lax.dot_general(pn.astype(BF16), dom, _TN, preferred_element_type=F32)
            dq_ref[r * tq:(r + 1) * tq, :] = (dq * scale).astype(dq_ref.dtype)
        dk_ref[...] = dk_acc[...].astype(dk_ref.dtype)
        dv_ref[...] = dv_acc[...].astype(dv_ref.dtype)

    return _pc(
        body, name=name, grid=(nblk,),
        in_specs=[pl.BlockSpec((sq, width), lambda i, c=qc: (0, c + i)), pl.BlockSpec((t_len, width), lambda i, c=kc: (0, c + i)),
                  pl.BlockSpec((t_len, width), lambda i, c=vc: (0, c + i))] + [pl.BlockSpec((sq, width), lambda i: (0, i))] * 3,
        out_specs=[pl.BlockSpec((sq, width), lambda i: (0, i)), pl.BlockSpec((t_len, width), lambda i: (0, i)), pl.BlockSpec((t_len, width), lambda i: (0, i))],
        out_shape=[jax.ShapeDtypeStruct((sq, nblk * width), out_dtype), jax.ShapeDtypeStruct((t_len, nblk * width), out_dtype),
                   jax.ShapeDtypeStruct((t_len, nblk * width), out_dtype)],
        scratch_shapes=[pltpu.VMEM((t_len, width), F32), pltpu.VMEM((t_len, width), F32)] + ([pltpu.VMEM((nq, tq, tq), F32)] if causal else []),
        sem=("arbitrary",),
    )(qa, ka, va, doa, oa, lsea)


_SELF = dict(qc=0, kc=8, vc=16, width=128, nblk=8, nsub=2, causal=True, tq=256, scale=A_HEAD_DIM ** -0.5)
_CROSS = dict(qc=0, kc=0, vc=4, width=256, nblk=4, nsub=1, causal=False, tq=512, scale=X_HEAD_DIM ** -0.5)


def _window_sum(x, g, row, backward):
    n = x.shape[0]

    def shift(y, k):
        if backward:
            return jnp.where(row < n - k, pltpu.roll(y, n - k, 0), 0.0)
        return jnp.where(row >= k, pltpu.roll(y, k, 0), 0.0)

    s2 = x + shift(x, 1)
    s4 = s2 + shift(s2, 2)
    s8 = s4 + shift(s4, 4)
    s16 = s8 + shift(s8, 8)
    return jnp.where(g == 0, s2, jnp.where(g == 1, s4, jnp.where(g == 2, s8, s16)))


def _pool(name, arr, c0, backward, out_dtype):
    n = arr.shape[0]
    gw = 256

    def body(v_ref, o_ref):
        g = pl.program_id(0)
        v = v_ref[...].astype(F32)
        row = lax.broadcasted_iota(jnp.int32, v.shape, 0)
        w = jnp.where(g == 0, 2, jnp.where(g == 1, 4, jnp.where(g == 2, 8, 16)))
        cnt = jnp.minimum(row + 1, w).astype(F32)
        if backward:
            o_ref[...] = (_window_sum(v / cnt, g, row, True) - v).astype(o_ref.dtype)
        else:
            o_ref[...] = (_window_sum(v, g, row, False) / cnt - v).astype(o_ref.dtype)

    return _pc(
        body, name=name, grid=(4,), in_specs=[pl.BlockSpec((n, gw), lambda i, c=c0 // gw: (0, c + i))],
        out_specs=pl.BlockSpec((n, gw), lambda i: (0, i)), out_shape=jax.ShapeDtypeStruct((n, 4 * gw), out_dtype), sem=("parallel",),
    )(arr)


_SCAN_ROWS = 256


def _scan_fwd(bu3, a2):
    n = bu3.shape[0]

    def body(bu_ref, a_ref, h_ref, carry):
        @pl.when(pl.program_id(0) == 0)
        def _():
            carry[...] = jnp.zeros_like(carry)

        ar, ai = a_ref[0:16, :], a_ref[16:32, :]

        def step(t, c):
            hr, hi = c
            nr = ar * hr - ai * hi + bu_ref[t, 0:16, :]
            ni = ar * hi + ai * hr + bu_ref[t, 16:32, :]
            h_ref[t, 0:16, :] = nr
            h_ref[t, 16:32, :] = ni
            return nr, ni

        hr, hi = lax.fori_loop(0, _SCAN_ROWS, step, (carry[0:16, :], carry[16:32, :]), unroll=8)
        carry[0:16, :] = hr
        carry[16:32, :] = hi

    blk = pl.BlockSpec((_SCAN_ROWS, 32, 128), lambda i: (i, 0, 0))
    return _pc(
        body, name="s5_scan_fwd", grid=(n // _SCAN_ROWS,), in_specs=[blk, pl.BlockSpec((32, 128), lambda i: (0, 0))], out_specs=blk,
        out_shape=jax.ShapeDtypeStruct(bu3.shape, F32), scratch_shapes=[pltpu.VMEM((32, 128), F32)], sem=("arbitrary",),
    )(bu3, a2)


def _scan_bwd(dh3, h3, a2):
    n = dh3.shape[0]
    nb = n // _SCAN_ROWS

    def body(dh_ref, h_ref, a_ref, dbu_ref, da_ref, carry):
        @pl.when(pl.program_id(0) == 0)
        def _():
            carry[...] = jnp.zeros_like(carry)
            da_ref[...] = jnp.zeros_like(da_ref)

        ar, ai = a_ref[0:16, :], a_ref[16:32, :]

        def step(tt, c):
            gr, gi, dar, dai = c
            t = _SCAN_ROWS - 1 - tt
            hr, hi = h_ref[t, 0:16, :], h_ref[t, 16:32, :]
            dar = dar + gr * hr + gi * hi
            dai = dai - gr * hi + gi * hr
            ngr = dh_ref[t, 0:16, :] + ar * gr + ai * gi
            ngi = dh_ref[t, 16:32, :] - ai * gr + ar * gi
            dbu_ref[t, 0:16, :] = ngr
            dbu_ref[t, 16:32, :] = ngi
            return ngr, ngi, dar, dai

        z = jnp.zeros((16, 128), F32)
        gr, gi, dar, dai = lax.fori_loop(0, _SCAN_ROWS, step, (carry[0:16, :], carry[16:32, :], z, z), unroll=8)
        carry[0:16, :] = gr
        carry[16:32, :] = gi
        da_ref[0:16, :] += dar
        da_ref[16:32, :] += dai

    blk = pl.BlockSpec((_SCAN_ROWS, 32, 128), lambda i: (nb - 1 - i, 0, 0))
    small = pl.BlockSpec((32, 128), lambda i: (0, 0))
    return _pc(
        body, name="s5_scan_bwd", grid=(nb,), in_specs=[blk, blk, small], out_specs=[blk, small],
        out_shape=[jax.ShapeDtypeStruct(dh3.shape, F32), jax.ShapeDtypeStruct((32, 128), F32)],
        scratch_shapes=[pltpu.VMEM((32, 128), F32)], sem=("arbitrary",),
    )(dh3, h3, a2)


def _bdense(bb_re, bb_im):
    eye = jnp.eye(8, dtype=F32)

    def one(bb):
        return jnp.einsum("sgph,gk->sghkp", bb.reshape(4, 8, S5_STATE, S5_GROUP_DIM), eye).reshape(512, 512)

    return jnp.concatenate([one(bb_re), one(bb_im)], axis=1)


def _cdense(c_re, c_im):
    eye = jnp.eye(8, dtype=F32)

    def one(cc):
        return jnp.einsum("sghp,gk->sgpkh", cc.reshape(4, 8, S5_GROUP_DIM, S5_STATE), eye).reshape(2048, 128)

    return jnp.concatenate([one(c_re), -one(c_im)], axis=0)


_NN = (((1,), (0,)), ((), ()))
_UF_BLOCK = 3072 // 128


def _s5_bu(z, bd):
    return _mm_blocks("mm_s5_bu", z, bd, grid=(1, 8, 1), out_blk=(SEQ, 512), a_blk=(SEQ, 128), a_idx=lambda i, j, k: (0, _UF_BLOCK + j % 4),
                      b_blk=(128, 512), b_idx=lambda i, j, k: (j % 4, j // 4), dims=_NN)


def _s5_bu_dx(dbu, bd):
    return _mm_blocks("mm_s5_bu_dx", dbu, bd, grid=(1, 4, 2), out_blk=(SEQ, 128), a_blk=(SEQ, 512), a_idx=lambda i, j, k: (0, 4 * k + j),
                      b_blk=(128, 512), b_idx=lambda i, j, k: (j, k), dims=_NT)


def _s5_bu_dw(z, dbu):
    return _mm_blocks("mm_s5_bu_dw", z, dbu, grid=(4, 2, 2), out_blk=(128, 512), a_blk=(1024, 128), a_idx=lambda i, j, k: (k, _UF_BLOCK + i),
                      b_blk=(1024, 512), b_idx=lambda i, j, k: (k, 4 * j + i), dims=_TN)


def _s5_y(h2, cf):
    return _mm_blocks("mm_s5_y", h2, cf, grid=(1, 4, 2), out_blk=(SEQ, 128), a_blk=(SEQ, 512), a_idx=lambda i, j, k: (0, 4 * k + j),
                      b_blk=(512, 128), b_idx=lambda i, j, k: (4 * k + j, 0), dims=_NN)


def _s5_y_dx(dyc, cf):
    return _mm_blocks("mm_s5_y_dx", dyc, cf, grid=(1, 8, 1), out_blk=(SEQ, 512), a_blk=(SEQ, 128), a_idx=lambda i, j, k: (0, j % 4),
                      b_blk=(512, 128), b_idx=lambda i, j, k: (j, 0), dims=_NT)


def _s5_y_dw(h2, dyc):
    return _mm_blocks("mm_s5_y_dw", h2, dyc, grid=(8, 1, 2), out_blk=(512, 128), a_blk=(1024, 512), a_idx=lambda i, j, k: (k, i),
                      b_blk=(1024, 128), b_idx=lambda i, j, k: (k, i % 4), dims=_TN)


def _pool_dense(pw):
    eye = jnp.eye(4, dtype=pw.dtype)
    return jnp.einsum("gcd,gk->gckd", pw, eye).reshape(1024, 1024)


def _row2(v):
    return v.reshape(1, -1)


def _even_fwd(x, W, i, zero, rest_of_weights, start_ahead):
    hn = _rw_fwd("rms_fwd", _f_rms, [_cols(x)], [_par(_row2(W["norm_ab"][i]) + zero)], [(D_MODEL, BF16, 1)], _NORM_ROWS)[0]
    z = _mm("mm_in_ab", hn, W["w_in_ab"][i], out_dtype=BF16)
    o, lse = _attn_fwd("attn_self_fwd", z, z, z, out_dtype=F32, **_SELF)
    zero = rest_of_weights(o)
    pooled = _pool("pool_fwd", z, 4096, False, BF16)
    wp = _pool_dense(W["pool_w"][i])
    mixed = _mm("mm_pool", pooled, wp)
    scale = _row2(W["pool_scale"][i]) + zero
    ab = _rw_fwd("gate_ab_fwd", _f_gate_ab, [_cols(o), _cols(z, 3072, 1024), _cols(mixed), _cols(z, 5120, 1024)], [_par(scale)],
                 [(2048, BF16, 2)], 256)[0]
    x1 = _mm("mm_out_ab", ab, W["w_out_ab"][i], res=x)
    return x1, dict(x=x, hn=hn, z=z, o=o, lse=lse, pooled=pooled, wp=wp, mixed=mixed, ab=ab), 0.0


def _even_bwd(dx1, sv, W, G, i, send):
    x, hn, z = sv["x"], sv["hn"], sv["z"]
    dab = _mm("mm_out_ab_dx", dx1, W["w_out_ab"][i], tb=True)
    G["w_out_ab"][i] = _mm("mm_out_ab_dw", sv["ab"], dx1, ta=True, out_dtype=BF16)
    scale = _row2(W["pool_scale"][i]) + send(0)
    (do, dga, dmixed, dgb), (dscale,) = _rw_bwd(
        "gate_ab_bwd", _f_gate_ab, [_cols(sv["o"]), _cols(z, 3072, 1024), _cols(sv["mixed"]), _cols(z, 5120, 1024)], [_par(scale)],
        [_cols(dab, nsplit=2)], [(0, F32), (1, BF16), (2, BF16), (3, BF16)], [0], 256)
    G["pool_scale"][i] = dscale.reshape(-1)
    dpooled = _mm("mm_pool_dx", dmixed, sv["wp"], tb=True)
    dwp = _mm("mm_pool_dw", sv["pooled"], dmixed, ta=True, out_dtype=BF16)
    G["pool_w"][i] = jnp.stack([dwp[g * 256:(g + 1) * 256, g * 256:(g + 1) * 256] for g in range(4)])
    dvb = _pool("pool_bwd", dpooled, 0, True, BF16)
    dq, dk, dv = _attn_bwd("attn_self_bwd", z, z, z, do, sv["o"], sv["lse"], out_dtype=BF16, **_SELF)
    dz = jnp.concatenate([dq, dk, dv, dga, dvb, dgb], axis=1)
    dhn = _mm("mm_in_ab_dx", dz, W["w_in_ab"][i], tb=True)
    G["w_in_ab"][i] = _mm("mm_in_ab_dw", hn, dz, ta=True, out_dtype=BF16, out_stack=W["w_in_ab"][i].shape[2])
    g = _row2(W["norm_ab"][i]) + send(1)
    (dx,), (dg,) = _rw_bwd("rms_bwd", _f_rms_res, [_cols(x)], [_par(g)], [_cols(dx1), _cols(dhn)], [(0, F32)], [0], _NORM_ROWS)
    G["norm_ab"][i] = dg.reshape(-1)
    return dx


def _odd_fwd(x, W, i, zero, rest_of_weights, start_ahead):
    hn = _rw_fwd("rms_fwd", _f_rms, [_cols(x)], [_par(_row2(W["norm_cd"][i]) + zero)], [(D_MODEL, BF16, 1)], _NORM_ROWS)[0]
    z = _mm("mm_in_cd", hn, W["w_in_cd"][i], out_dtype=BF16)
    sgu_p = [_par(_row2(W["sgu_ln_g"][i]), 4), _par(_row2(W["sgu_ln_b"][i]), 4), _par(W["sgu_w"][i], 4), _par(W["sgu_b"][i][..., None], 4)]
    c_out = _rw_fwd("sgu_fwd", _f_sgu, [_cols(z, 0, 1024, 4), _cols(z, 1024, 1024, 4), _cols(z, 2048, 1024, 4)], sgu_p, [(1024, BF16, 4)], 128)[0]
    prep_rows = [_cols(W["s5_a_re"][i]), _cols(W["s5_a_im"][i]), _cols(W["s5_log_dt"][i].reshape(S5_GROUPS, 1))]
    abar_re, abar_im, coef_re, coef_im = _rw_fwd("s5_prep_fwd", _f_s5_prep, prep_rows, [], [(S5_STATE, F32, 1)] * 4, S5_GROUPS)
    bb_rows = [_cols(W["s5_b_re"][i].reshape(2048, 16)), _cols(W["s5_b_im"][i].reshape(2048, 16)), _cols(coef_re.reshape(2048, 1)), _cols(coef_im.reshape(2048, 1))]
    bb_re, bb_im = _rw_fwd("s5_bbar_fwd", _f_bbar, bb_rows, [], [(16, F32, 1)] * 2, 256)
    bd = _bdense(bb_re, bb_im).astype(BF16)
    cf = _cdense(W["s5_c_re"][i], W["s5_c_im"][i]).astype(BF16)
    a2 = jnp.concatenate([abar_re.reshape(16, 128), abar_im.reshape(16, 128)], axis=0)
    bu = _s5_bu(z, bd)
    h3 = _scan_fwd(bu.reshape(SEQ, 32, 128), a2)
    h2 = h3.reshape(SEQ, 4096)
    yc = _s5_y(h2, cf)
    dpar = _row2(W["s5_d"][i]) + start_ahead(h3)
    yg = _rw_fwd("gelu_fwd", _f_gelu_y, [_cols(yc), _cols(z, 3072, 512)], [_par(dpar)], [(512, BF16, 1)], 256)[0]
    zero = rest_of_weights(yg)
    w12 = jnp.concatenate([W["glu_w1"][i], W["glu_w2"][i]], axis=1)
    t12 = _mm("mm_glu", yg, w12)
    d_out = _rw_fwd("glu_gate_fwd", _f_glu_gate, [_cols(t12, nsplit=2), _cols(z, 3584, 512)], [], [(512, BF16, 1)], 256)[0]
    cd = jnp.concatenate([c_out, d_out], axis=1)
    x1 = _mm("mm_out_cd", cd, W["w_out_cd"][i], res=x)
    sv = dict(x=x, hn=hn, z=z, sgu_p=sgu_p, prep_rows=prep_rows, bb_rows=bb_rows, bb=(bb_re, bb_im), bd=bd, cf=cf, a2=a2,
              h3=h3, h2=h2, yc=yc, dpar=dpar, yg=yg, w12=w12, t12=t12, cd=cd)
    return x1, sv, zero


def _odd_bwd(dx1, sv, W, G, i, send):
    x, hn, z = sv["x"], sv["hn"], sv["z"]
    dcd = _mm("mm_out_cd_dx", dx1, W["w_out_cd"][i], tb=True)
    G["w_out_cd"][i] = _mm("mm_out_cd_dw", sv["cd"], dx1, ta=True, out_dtype=BF16)
    (du, dv, dgc), (dlg, dlb, dsw, dsb) = _rw_bwd(
        "sgu_bwd", _f_sgu, [_cols(z, 0, 1024, 4), _cols(z, 1024, 1024, 4), _cols(z, 2048, 1024, 4)], sv["sgu_p"],
        [_cols(dcd, 0, 1024, 4)], [(0, BF16), (1, BF16), (2, BF16)], [0, 1, 2, 3], 128)
    G["sgu_ln_g"][i], G["sgu_ln_b"][i] = dlg.reshape(-1), dlb.reshape(-1)
    G["sgu_w"][i], G["sgu_b"][i] = dsw, dsb[..., 0]
    (dt12, dgd), _ = _rw_bwd("glu_gate_bwd", _f_glu_gate, [_cols(sv["t12"], nsplit=2), _cols(z, 3584, 512)], [], [_cols(dcd, 1024, 512)],
                             [(0, BF16), (1, BF16)], [], 256)
    dyg = _mm("mm_glu_dx", dt12, sv["w12"], tb=True)
    dw12 = _mm("mm_glu_dw", sv["yg"], dt12, ta=True, out_dtype=BF16)
    G["glu_w1"][i], G["glu_w2"][i] = dw12[:, :512], dw12[:, 512:]
    (dyc, duf1), (dd,) = _rw_bwd("gelu_bwd", _f_gelu_y, [_cols(sv["yc"]), _cols(z, 3072, 512)], [_par(sv["dpar"])], [_cols(dyg)],
                                 [(0, BF16), (1, F32)], [0], 256)
    G["s5_d"][i] = dd.reshape(-1)
    dh2 = _s5_y_dx(dyc, sv["cf"])
    dcf = _s5_y_dw(sv["h2"], dyc)
    _, cvjp = jax.vjp(_cdense, W["s5_c_re"][i], W["s5_c_im"][i])
    G["s5_c_re"][i], G["s5_c_im"][i] = cvjp(dcf)
    dbu3, da2 = _scan_bwd(dh2.reshape(SEQ, 32, 128), sv["h3"], sv["a2"])
    dbu = dbu3.reshape(SEQ, 4096)
    duf2 = _s5_bu_dx(dbu, sv["bd"])
    dbd = _s5_bu_dw(z, dbu)
    _, bvjp = jax.vjp(_bdense, *sv["bb"])
    dbb_re, dbb_im = bvjp(dbd)
    (dbr, dbi, dcr, dci), _ = _rw_bwd("s5_bbar_bwd", _f_bbar, sv["bb_rows"], [], [_cols(dbb_re), _cols(dbb_im)],
                                      [(0, F32), (1, F32), (2, F32), (3, F32)], [], 256)
    G["s5_b_re"][i], G["s5_b_im"][i] = dbr.reshape(S5_GROUPS, S5_STATE, S5_GROUP_DIM), dbi.reshape(S5_GROUPS, S5_STATE, S5_GROUP_DIM)
    douts = [_cols(da2[0:16].reshape(S5_GROUPS, S5_STATE)), _cols(da2[16:32].reshape(S5_GROUPS, S5_STATE)),
             _cols(dcr.reshape(S5_GROUPS, S5_STATE)), _cols(dci.reshape(S5_GROUPS, S5_STATE))]
    (dar, dai, dldt), _ = _rw_bwd("s5_prep_bwd", _f_s5_prep, sv["prep_rows"], [], douts, [(0, F32), (1, F32), (2, F32)], [], S5_GROUPS)
    G["s5_a_re"][i], G["s5_a_im"][i], G["s5_log_dt"][i] = dar, dai, dldt.reshape(-1)
    dxd = (duf1 + duf2).astype(BF16)
    dz = jnp.concatenate([du, dv, dgc, dxd, dgd], axis=1)
    dhn = _mm("mm_in_cd_dx", dz, W["w_in_cd"][i], tb=True)
    G["w_in_cd"][i] = _mm("mm_in_cd_dw", hn, dz, ta=True, out_dtype=BF16, out_stack=W["w_in_cd"][i].shape[2])
    g = _row2(W["norm_cd"][i]) + send(0)
    (dx,), (dg,) = _rw_bwd("rms_bwd", _f_rms_res, [_cols(x)], [_par(g)], [_cols(dx1), _cols(dhn)], [(0, F32)], [0], _NORM_ROWS)
    G["norm_cd"][i] = dg.reshape(-1)
    return dx


def _cross_fwd(x1, mem_n, W, l, zero):
    hx = _rw_fwd("rms_fwd", _f_rms, [_cols(x1)], [_par(_row2(W["norm_x"][l]) + zero)], [(D_MODEL, BF16, 1)], _NORM_ROWS)[0]
    qx = _mm("mm_xq", hx, W["w_xq"][l], out_dtype=BF16)
    kv = _mm("mm_xkv", mem_n, W["w_xkv"][l], out_dtype=BF16)
    ox, lse = _attn_fwd("attn_cross_fwd", qx, kv, kv, out_dtype=BF16, **_CROSS)
    x2 = _mm("mm_xo", ox, W["w_xo"][l], res=x1)
    return x2, dict(x1=x1, hx=hx, qx=qx, kv=kv, ox=ox, lse=lse)


def _cross_bwd(dx2, dmem_n, sv, mem_n, W, G, l, zero):
    dox = _mm("mm_xo_dx", dx2, W["w_xo"][l], tb=True, out_dtype=BF16)
    G["w_xo"][l] = _mm("mm_xo_dw", sv["ox"], dx2, ta=True, out_dtype=BF16)
    dqx, dk, dv = _attn_bwd("attn_cross_bwd", sv["qx"], sv["kv"], sv["kv"], dox, sv["ox"], sv["lse"], out_dtype=BF16, **_CROSS)
    dkv = jnp.concatenate([dk, dv], axis=1)
    dhx = _mm("mm_xq_dx", dqx, W["w_xq"][l], tb=True)
    G["w_xq"][l] = _mm("mm_xq_dw", sv["hx"], dqx, ta=True, out_dtype=BF16)
    dmem_n = _mm("mm_xkv_dx", dkv, W["w_xkv"][l], tb=True, res=dmem_n)
    G["w_xkv"][l] = _mm("mm_xkv_dw", mem_n, dkv, ta=True, out_dtype=BF16, out_stack=W["w_xkv"][l].shape[2])
    (dx1,), (dg,) = _rw_bwd("rms_bwd", _f_rms_res, [_cols(sv["x1"])], [_par(_row2(W["norm_x"][l]) + zero)], [_cols(dx2), _cols(dhx)], [(0, F32)], [0], _NORM_ROWS)
    G["norm_x"][l] = dg.reshape(-1)
    return dx1, dmem_n


_PER_LAYER = ("pool_scale", "norm_ab", "norm_cd", "sgu_ln_g", "sgu_ln_b", "sgu_w", "sgu_b", "s5_d", "s5_c_re", "s5_c_im", "s5_b_re", "s5_b_im",
              "s5_a_re", "s5_a_im", "s5_log_dt", "w_in_ab", "pool_w", "w_out_ab", "w_in_cd", "glu_w1", "glu_w2", "w_out_cd")


def _local_step(x, mem, target, W, weights_of, start_ahead, send_grads, after_layer):
    G = {k: [None, None] for k in _PER_LAYER}
    for k in ("norm_x", "w_xq", "w_xkv", "w_xo"):
        G[k] = [None] * DEPTH
    mem_rows = [_cols(mem)]
    mem_par = [_par(_row2(W["mem_norm"]))]
    mem_n = _rw_fwd("rms_fwd_mem", _f_rms, mem_rows, mem_par, [(D_MODEL, BF16, 1)], MEM_LEN)[0]
    saved = []
    for layer in range(DEPTH):
        zero = weights_of(layer, 0, x if layer else mem_n)
        mixer = _even_fwd if layer % 2 == 0 else _odd_fwd
        x, sv, zero = mixer(x, W, layer // 2, zero, functools.partial(weights_of, layer, 1), functools.partial(start_ahead, layer))
        x, svx = _cross_fwd(x, mem_n, W, layer, zero)
        saved.append((sv, svx))
    loss, dx, dfinal = _loss_head(x, target, _row2(W["final_norm"]))
    G["final_norm"] = dfinal.reshape(-1)
    dmem_n, zero = None, 0.0
    for layer in reversed(range(DEPTH)):
        sv, svx = saved[layer]
        dx, dmem_n = _cross_bwd(dx, dmem_n, svx, mem_n, W, G, layer, zero)
        hook = functools.partial(send_grads, layer, G)
        dx = _even_bwd(dx, sv, W, G, layer // 2, hook) if layer % 2 == 0 else _odd_bwd(dx, sv, W, G, layer // 2, hook)
        zero = after_layer(layer, G)
    _, (dmn,) = _rw_bwd("rms_bwd_mem", _f_rms, mem_rows, mem_par, [_cols(dmem_n)], [], [0], 256)
    G["mem_norm"] = dmn.reshape(-1)
    return loss, dx, G


_HBM = pl.BlockSpec(memory_space=pltpu.HBM)
_ANY = pl.BlockSpec(memory_space=pl.ANY)
_SEM = pl.BlockSpec(memory_space=pltpu.SEMAPHORE)
_N_PEERS = N_DEV - 1


def _mesh_pos():
    return lax.axis_index("x"), lax.axis_index("y"), lax.axis_index("c")


def _peer(pos, k):
    x, y, c = pos
    return (x ^ ((k >> 2) & 1), y ^ ((k >> 1) & 1), c ^ (k & 1))


def _lin(pos):
    return 4 * pos[0] + 2 * pos[1] + pos[2]


def _ends(gather, srcs, lands, t, sender, receiver):
    if gather:
        return lands[t].at[sender], lands[t].at[sender]
    whole = len(srcs[t].shape) != len(lands[t].shape)
    return (srcs[t] if whole else srcs[t].at[receiver]), lands[t].at[sender]


def _into_slot(name, b2, r0, r, me, dtype, after):
    c = b2.shape[1]
    tr = _row_block(r, c, 2 << 20)
    assert r0 % tr == 0

    def body(me_ref, x_ref, *rest):
        rest[-1][...] = x_ref[...].astype(dtype)

    extra = [] if after is None else [after]
    grid_spec = pltpu.PrefetchScalarGridSpec(
        num_scalar_prefetch=1, grid=(r // tr,),
        in_specs=[pl.BlockSpec((tr, c), lambda i, me, o=r0 // tr: (o + i, 0))] + [_ANY] * len(extra),
        out_specs=pl.BlockSpec((None, tr, c), lambda i, me: (me[0], i, 0)))
    return pl.pallas_call(
        body, name=name, grid_spec=grid_spec, out_shape=jax.ShapeDtypeStruct((N_DEV, r, c), dtype),
        compiler_params=pltpu.CompilerParams(dimension_semantics=("arbitrary",), vmem_limit_bytes=V7X_VMEM_LIMIT_BYTES),
        interpret=False,
    )(me, b2, *extra)


def _exchange_start(name, gather, srcs, lands, after=None):
    ns, nt = len(srcs), len(lands)
    arrs = list(srcs) + list(lands)
    extra = [] if after is None else [after]

    def body(*refs):
        ins, lnd = refs[:ns], refs[ns:ns + nt]
        refs = refs[len(extra):]
        send_sems, recv_sems = refs[ns + nt], refs[ns + nt + 1]
        token = refs[-1]
        pos = _mesh_pos()
        me = _lin(pos)
        for k in range(1, N_DEV):
            peer = _peer(pos, k)
            for t in range(nt):
                src, dst = _ends(gather, ins, lnd, t, me, _lin(peer))
                pltpu.make_async_remote_copy(
                    src_ref=src, dst_ref=dst, send_sem=send_sems.at[t * _N_PEERS + k - 1], recv_sem=recv_sems.at[t * _N_PEERS + k - 1],
                    device_id=peer, device_id_type=pl.DeviceIdType.MESH).start()
        token[...] = jnp.zeros_like(token)

    out = pl.pallas_call(
        body, name=name,
        out_shape=(pltpu.SemaphoreType.DMA((nt * _N_PEERS,)), pltpu.SemaphoreType.DMA((nt * _N_PEERS,)), *[pltpu.HBM(a.shape, a.dtype) for a in arrs],
                   jax.ShapeDtypeStruct((8, 128), F32)),
        in_specs=[_HBM] * (ns + nt) + [_ANY] * len(extra), out_specs=(_SEM, _SEM, *[_HBM] * (ns + nt), pl.BlockSpec(memory_space=pltpu.VMEM)),
        input_output_aliases={j: 2 + j for j in range(ns + nt)},
        compiler_params=pltpu.CompilerParams(has_side_effects=pltpu.SideEffectType.DATAFLOW_SIDE_EFFECTING),
        interpret=False,
    )(*[pltpu.with_memory_space_constraint(a, pltpu.HBM) for a in arrs], *extra)
    return dict(send=out[0], recv=out[1], srcs=list(out[2:2 + ns]), lands=list(out[2 + ns:2 + ns + nt]), token=out[-1][0, 0], token_arr=out[-1], gather=gather)


def _exchange_wait(name, ex, after):
    ns, nt = len(ex["srcs"]), len(ex["lands"])
    gather = ex["gather"]
    arrs = ex["srcs"] + ex["lands"]
    after = list(after) if isinstance(after, (list, tuple)) else [after]

    def body(*refs):
        ins, lnd = refs[:ns], refs[ns:ns + nt]
        send_sems, recv_sems = refs[ns + nt], refs[ns + nt + 1]
        pos = _mesh_pos()
        me = _lin(pos)
        for k in range(1, N_DEV):
            peer = _peer(pos, k)
            for t in range(nt):
                src, _ = _ends(gather, ins, lnd, t, me, _lin(peer))
                _, dst = _ends(gather, ins, lnd, t, _lin(peer), me)
                cp = pltpu.make_async_remote_copy(
                    src_ref=src, dst_ref=dst, send_sem=send_sems.at[t * _N_PEERS + k - 1], recv_sem=recv_sems.at[t * _N_PEERS + k - 1],
                    device_id=peer, device_id_type=pl.DeviceIdType.MESH)
                cp.wait_send()
                cp.wait_recv()

    out = pl.pallas_call(
        body, name=name, out_shape=tuple(pltpu.HBM(a.shape, a.dtype) for a in arrs),
        in_specs=[_HBM] * (ns + nt) + [_SEM, _SEM] + [_ANY] * len(after), out_specs=tuple([_HBM] * (ns + nt)),
        input_output_aliases={j: j for j in range(ns + nt)},
        compiler_params=pltpu.CompilerParams(has_side_effects=pltpu.SideEffectType.DATAFLOW_SIDE_EFFECTING),
        interpret=False,
    )(*arrs, ex["send"], ex["recv"], *after)
    return list(out[:ns]), list(out[ns:])


def _scatter_begin(name, srcs):
    lands = [lax.empty(s.shape if s.ndim == 3 else (N_DEV,) + s.shape, s.dtype) for s in srcs]
    return _exchange_start(name, False, srcs, lands)


def _adam(name, w, m, v, parts, own, me, layer, bufs):
    r, c = parts.shape[1:]
    tr = _row_block(r, max(c, 128), 2 << 20)
    nb = r // tr

    def body(me_ref, w_ref, m_ref, v_ref, p_ref, own_ref, *rest):
        g_ref, d_ref, nm_ref, nv_ref, acc = rest[-5:]
        acc[...] = jnp.zeros_like(acc)
        for k in range(N_DEV):
            @pl.when(me_ref[0] == k)
            def _():
                acc[...] += own_ref[...].astype(F32)

            @pl.when(me_ref[0] != k)
            def _(k=k):
                acc[...] += p_ref[k].astype(F32)

        g = acc[...]
        mm = ADAM_B1 * m_ref[...] + (1.0 - ADAM_B1) * g
        vv = ADAM_B2 * v_ref[...] + (1.0 - ADAM_B2) * jnp.square(g)
        m_hat = mm / (1.0 - ADAM_B1 ** ADAM_STEP)
        v_hat = vv / (1.0 - ADAM_B2 ** ADAM_STEP)
        g_ref[...] = g
        d_ref[...] = -ADAM_LR * (m_hat / (jnp.sqrt(v_hat) + ADAM_EPS) + ADAM_WD * w_ref[...])
        nm_ref[...] = mm
        nv_ref[...] = vv

    blk = pl.BlockSpec((tr, c), lambda i, me, o=layer * nb: (o + i, 0))
    own_spec = pl.BlockSpec((None, tr, c), lambda i, me: (me[0], i, 0)) if own.ndim == 3 else pl.BlockSpec((tr, c), lambda i, me: (i, 0))
    in_specs = [blk, blk, blk, pl.BlockSpec((N_DEV, tr, c), lambda i, me: (0, i, 0)), own_spec]
    args = [me, w, m, v, parts, own]
    aliases = {}
    if bufs is not None:
        in_specs += [_ANY] * 4
        aliases = {len(args) + j: j for j in range(4)}
        args += list(bufs)
    grid_spec = pltpu.PrefetchScalarGridSpec(
        num_scalar_prefetch=1, grid=(nb,), in_specs=in_specs, out_specs=[blk] * 4, scratch_shapes=[pltpu.VMEM((tr, c), F32)])
    return pl.pallas_call(
        body, name=name, grid_spec=grid_spec, out_shape=[jax.ShapeDtypeStruct(w.shape, F32)] * 4, input_output_aliases=aliases,
        compiler_params=pltpu.CompilerParams(dimension_semantics=("arbitrary",), vmem_limit_bytes=V7X_VMEM_LIMIT_BYTES),
        interpret=False,
    )(*args)


def _row_block(r, c, limit):
    best = None
    for tr in range(16, r + 1, 16):
        if r % tr == 0 and tr * c * 4 <= limit:
            best = tr
    return r if best is None else best


_BIG = ("w_in_ab", "pool_w", "w_out_ab", "w_in_cd", "glu_w1", "glu_w2", "w_out_cd", "w_xq", "w_xkv", "w_xo")
_STACKED = ("w_in_ab", "w_in_cd", "w_xkv")
_MIXER_BIG = (("w_in_ab", "pool_w", "w_out_ab"), ("w_in_cd", "glu_w1", "glu_w2", "w_out_cd"))
_CROSS_BIG = ("w_xq", "w_xkv", "w_xo")
_SMALL_SPLIT = ["norm_cd", "sgu_ln_g", "sgu_ln_b", "s5_d"]
_REPLICATED_ODD = ["sgu_w", "sgu_b", "s5_a_re", "s5_a_im", "s5_log_dt", "s5_b_re", "s5_b_im", "s5_c_re", "s5_c_im", "final_norm"]
_REPLICATED_EVEN = ["norm_ab", "pool_scale", "norm_x", "mem_norm"]
_REPLICATED = _REPLICATED_ODD + _REPLICATED_EVEN
_WEIGHTS = ["norm_ab", "w_in_ab", "pool_w", "pool_scale", "w_out_ab", "norm_cd", "w_in_cd", "sgu_ln_g", "sgu_ln_b", "sgu_w", "sgu_b", "s5_a_re",
            "s5_a_im", "s5_log_dt", "s5_b_re", "s5_b_im", "s5_c_re", "s5_c_im", "s5_d", "glu_w1", "glu_w2", "w_out_cd", "norm_x", "w_xq",
            "w_xkv", "w_xo", "mem_norm", "final_norm"]


def _layer_big(layer):
    return [(n, layer // 2) for n in _MIXER_BIG[layer % 2]] + [(n, layer) for n in _CROSS_BIG]


def _gather_parts(layer):
    big = _layer_big(layer)
    return [big[:1], big[1:]]


def _scatter_parts(layer):
    big = _layer_big(layer)
    return [big[2:], big[:2]] if layer % 2 == 0 else [big]


def _from_slots(name, a):
    if name in _STACKED:
        return a
    if name == "pool_w":
        return a.reshape(N_DEV, 4, 32, 256).transpose(1, 0, 2, 3).reshape(4, 256, 256)
    return a.reshape(-1, a.shape[-1])


def _to_slots(name, g):
    if name in _STACKED:
        return g
    if name == "pool_w":
        return g.reshape(4, N_DEV, 32, 256).transpose(1, 0, 2, 3).reshape(N_DEV, 128, 256)
    return g.reshape(N_DEV, -1, g.shape[-1])


def _rows2d(a):
    return a.reshape(-1, a.shape[-1])


def _small_rows(block):
    return jnp.pad(block, ((0, 0), (0, 128 - block.shape[1])))


def kernel(x, mem, norm_ab, w_in_ab, pool_w, pool_scale, w_out_ab, norm_cd, w_in_cd, sgu_ln_g, sgu_ln_b, sgu_w, sgu_b, s5_a_re, s5_a_im, s5_log_dt, s5_b_re, s5_b_im, s5_c_re, s5_c_im, s5_d, glu_w1, glu_w2, w_out_cd, norm_x, w_xq, w_xkv, w_xo, mem_norm, final_norm, loss_target, m_norm_ab, m_w_in_ab, m_pool_w, m_pool_scale, m_w_out_ab, m_norm_cd, m_w_in_cd, m_sgu_ln_g, m_sgu_ln_b, m_sgu_w, m_sgu_b, m_s5_a_re, m_s5_a_im, m_s5_log_dt, m_s5_b_re, m_s5_b_im, m_s5_c_re, m_s5_c_im, m_s5_d, m_glu_w1, m_glu_w2, m_w_out_cd, m_norm_x, m_w_xq, m_w_xkv, m_w_xo, m_mem_norm, m_final_norm, v_norm_ab, v_w_in_ab, v_pool_w, v_pool_scale, v_w_out_ab, v_norm_cd, v_w_in_cd, v_sgu_ln_g, v_sgu_ln_b, v_sgu_w, v_sgu_b, v_s5_a_re, v_s5_a_im, v_s5_log_dt, v_s5_b_re, v_s5_b_im, v_s5_c_re, v_s5_c_im, v_s5_d, v_glu_w1, v_glu_w2, v_w_out_cd, v_norm_x, v_w_xq, v_w_xkv, v_w_xo, v_mem_norm, v_final_norm):
    args = locals()
    w = {n: args[n] for n in _WEIGHTS}
    m = {n: args["m_" + n] for n in _WEIGHTS}
    v = {n: args["v_" + n] for n in _WEIGHTS}

    me = jnp.reshape(_lin(_mesh_pos()), (1,)).astype(jnp.int32)

    order = [(layer, p) for layer in range(DEPTH) for p in range(len(_gather_parts(layer)))]
    lands, gathers = {}, {}
    for key in order:
        lands[key] = []
        for name, i in _gather_parts(key[0])[key[1]]:
            b2 = _rows2d(w[name])
            r = b2.shape[0] // w[name].shape[0]
            lands[key].append(_into_slot("cast_" + name, b2, i * r, r, me, BF16, None))
    small_blocks = jnp.concatenate([_small_rows(w[n]) for n in _SMALL_SPLIT], axis=0)
    lands[order[0]].append(_into_slot("cast_small", small_blocks, 0, 8, me, F32, None))

    def begin_gather(key, after):
        gathers[key] = _exchange_start("gather%d%s_start" % (key[0], "ab"[key[1]]), True, [], lands[key], after)
        return gathers[key]["token"]

    W = {n: w[n] for n in _REPLICATED}
    W["mem_norm"] = w["mem_norm"] + begin_gather(order[0], None)
    for name in _BIG:
        W[name] = [None] * w[name].shape[0]

    def weights_of(layer, part, after):
        key = (layer, part)
        if key not in gathers:
            return 0.0
        if key == order[0]:
            after = [after] + [a for k in order[1:] for a in lands[k]]
        _, got = _exchange_wait("gather%d%s_wait" % (layer, "ab"[part]), gathers[key], after)
        for (name, i), arr in zip(_gather_parts(layer)[part], got):
            W[name][i] = _from_slots(name, arr)
        if key == order[0]:
            sm = got[-1].reshape(N_DEV, 4, 2, 128)
            for j, n in enumerate(_SMALL_SPLIT):
                width = w[n].shape[1]
                W[n] = sm[:, j, :, :width].transpose(1, 0, 2).reshape(2, N_DEV * width)
        nxt = order.index(key) + 1
        return begin_gather(order[nxt], got[0]) if nxt < len(order) and order[nxt] not in gathers else 0.0

    def start_ahead(layer, after):
        return begin_gather((layer + 1, 0), after) if layer + 1 < DEPTH else 0.0

    scatters, small = {}, {}

    def send_grads(layer, G, part):
        srcs = [_to_slots(name, G[name][i]) for name, i in _scatter_parts(layer)[part]]
        scatters[layer, part] = _scatter_begin("scatter%d%s_start" % (layer, "ab"[part]), srcs)
        return scatters[layer, part]["token"]

    def rows(a):
        if a.ndim == 1:
            return a.reshape(1, -1)
        return a.reshape(-1, 128) if a.shape[-1] < 128 and a.size % 128 == 0 else a.reshape(-1, a.shape[-1])

    def begin_small(tag, G, names, split):
        srcs = [rows(G[n] if n in ("mem_norm", "final_norm") else jnp.stack(G[n])) for n in names]
        if split:
            srcs += [jnp.stack(G[n]).reshape(2, N_DEV, -1).transpose(1, 0, 2) for n in _SMALL_SPLIT]
        small[tag] = _scatter_begin("scatter_small_%s_start" % tag, srcs)
        return small[tag]["token"]

    def after_layer(layer, G):
        return begin_small("odd", G, _REPLICATED_ODD, True) if layer == 1 else 0.0

    loss, dx, G = _local_step(x[0], mem[0], loss_target[0], W, weights_of, start_ahead, send_grads, after_layer)
    loss = lax.psum(loss[0, 0], MESH_AXES)
    begin_small("even", G, _REPLICATED_EVEN, False)

    out = {}
    after = small["even"]["token_arr"]
    for layer, part in scatters:
        own, got = _exchange_wait("scatter%d%s_wait" % (layer, "ab"[part]), scatters[layer, part], after)
        for (name, i), mine, parts in zip(_scatter_parts(layer)[part], own, got):
            out[name] = _adam("adam_" + name, _rows2d(w[name]), _rows2d(m[name]), _rows2d(v[name]), parts, mine, me, i, out.get(name))
        after = [out[name][0] for name, i in _scatter_parts(layer)[part]]
    for name in _BIG:
        out[name] = [a.reshape(w[name].shape) for a in out[name]]
    for tag, names in (("odd", _REPLICATED_ODD + _SMALL_SPLIT), ("even", _REPLICATED_EVEN)):
        own, got = _exchange_wait("scatter_small_%s_wait" % tag, small[tag], after)
        for n, mine, parts in zip(names, own, got):
            as2d = (lambda a: a) if n in _SMALL_SPLIT else rows
            res = _adam("adam_" + n, as2d(w[n]), as2d(m[n]), as2d(v[n]), parts, mine, me, 0, None)
            out[n] = [a.reshape(w[n].shape) for a in res]
        after = [out[n][0] for n in names]

    return (loss, dx[None], *[out[n][0] for n in _WEIGHTS], *[out[n][1] for n in _WEIGHTS], *[out[n][2] for n in _WEIGHTS],
            *[out[n][3] for n in _WEIGHTS])
```
